```python
import jax, jax.numpy as jnp
from jax import lax
import numpy as np

D_MODEL = 2048
BATCH = 16
SEQ = 2048
DEPTH = 1

HEAD_DIM = 64
ATTN_WIDTH = D_MODEL // 2
N_Q_HEADS = ATTN_WIDTH // HEAD_DIM
N_KV_HEADS = 2
GQA_GROUP = N_Q_HEADS // N_KV_HEADS
KV_WIDTH = N_KV_HEADS * HEAD_DIM
WINDOW = 128
BLOCK = 128
GATE_WIDTH = D_MODEL - ATTN_WIDTH
GATE_GROUP_DIM = 128
N_GATE_GROUPS = GATE_WIDTH // GATE_GROUP_DIM
CHUNK = 128
MIX_WIDTH = ATTN_WIDTH + GATE_WIDTH
IN_WIDTH = ATTN_WIDTH + 2 * KV_WIDTH + 2 * GATE_WIDTH
D_FF = -(-8 * D_MODEL // (3 * 256)) * 256
EPS = 1e-6

kernel_name = 'hymba_swa_sink_gmlp_swiglu'


def _rms_norm(x, gain):
    xf = x.astype(jnp.float32)
    y = xf * lax.rsqrt(jnp.mean(xf * xf, axis=-1, keepdims=True) + EPS)
    return (y * gain.astype(jnp.float32)).astype(x.dtype)


def _alibi_slopes(n_heads):
    return jnp.exp2(-8.0 * jnp.arange(1, n_heads + 1, dtype=jnp.float32) / n_heads)


def _sliding_window_attention(q, k, v, sinks):
    bsz, seq = q.shape[0], q.shape[1]
    nb = seq // BLOCK
    qb = q.reshape(bsz, nb, BLOCK, N_KV_HEADS, GQA_GROUP, HEAD_DIM)

    def band(t):
        tb = t.reshape(bsz, nb, BLOCK, N_KV_HEADS, HEAD_DIM)
        prev = jnp.pad(tb, ((0, 0), (1, 0), (0, 0), (0, 0), (0, 0)))[:, :nb]
        return jnp.concatenate([prev, tb], axis=2)

    kb, vb = band(k), band(v)
    scores = jnp.einsum('bnqkgd,bnskd->bkgnqs', qb, kb).astype(jnp.float32) * (HEAD_DIM ** -0.5)
    qi = jnp.arange(BLOCK)[:, None]
    kj = jnp.arange(2 * BLOCK)[None, :]
    dist = qi + BLOCK - kj
    key_pos = jnp.arange(nb)[:, None, None] * BLOCK - BLOCK + kj[None]
    valid = (dist >= 0) & (dist < WINDOW) & (key_pos >= 0)
    slopes = _alibi_slopes(N_Q_HEADS).reshape(N_KV_HEADS, GQA_GROUP, 1, 1, 1)
    logits = jnp.where(valid, scores - slopes * dist.astype(jnp.float32), -jnp.inf)
    sink = jnp.broadcast_to(
        sinks.astype(jnp.float32).reshape(1, N_KV_HEADS, GQA_GROUP, 1, 1, 1),
        logits.shape[:-1] + (1,))
    probs = jax.nn.softmax(jnp.concatenate([logits, sink], axis=-1), axis=-1)[..., :-1]
    out = jnp.einsum('bkgnqs,bnskd->bnqkgd', probs.astype(v.dtype), vb)
    return out.reshape(bsz, seq, ATTN_WIDTH)


def _spatial_gating(z, ln_g, ln_b, w_s, b_s):
    bsz, seq = z.shape[0], z.shape[1]
    nc = seq // CHUNK
    u, v = jnp.split(z, 2, axis=-1)
    vc = v.reshape(bsz, nc, CHUNK, N_GATE_GROUPS, GATE_GROUP_DIM).astype(jnp.float32)
    mu = jnp.mean(vc, axis=-1, keepdims=True)
    var = jnp.mean(jnp.square(vc - mu), axis=-1, keepdims=True)
    vn = ((vc - mu) * lax.rsqrt(var + EPS) * ln_g.astype(jnp.float32)
          + ln_b.astype(jnp.float32)).astype(z.dtype)
    causal = jnp.tril(jnp.ones((CHUNK, CHUNK), dtype=bool))[None]
    w = jnp.where(causal, w_s, jnp.zeros_like(w_s))
    mixed = jnp.einsum('gts,bcsge->bctge', w, vn) + b_s.T[:, :, None]
    return u * mixed.reshape(bsz, seq, GATE_WIDTH)


def _normal(key, shape, scale):
    return jax.random.normal(key, shape, jnp.float32) * scale


def _fwd_setup_inputs(seed: int = 0) -> dict:
    key = jax.random.key(seed)
    ks = jax.random.split(key, 17)
    L = DEPTH
    return {
        'x': _normal(ks[0], (BATCH, SEQ, D_MODEL), 1.0),
        'norm1_g': 1.0 + _normal(ks[1], (L, D_MODEL), 0.1),
        'w_in': _normal(ks[2], (L, D_MODEL, IN_WIDTH), D_MODEL ** -0.5),
        'q_norm_g': 1.0 + _normal(ks[3], (L, HEAD_DIM), 0.1),
        'k_norm_g': 1.0 + _normal(ks[4], (L, HEAD_DIM), 0.1),
        'attn_sinks': _normal(ks[5], (L, N_Q_HEADS), 0.5),
        'gate_ln_g': 1.0 + _normal(ks[6], (L, N_GATE_GROUPS, GATE_GROUP_DIM), 0.1),
        'gate_ln_b': _normal(ks[7], (L, N_GATE_GROUPS, GATE_GROUP_DIM), 0.02),
        'w_spatial': _normal(ks[8], (L, N_GATE_GROUPS, CHUNK, CHUNK), CHUNK ** -0.5),
        'b_spatial': 1.0 + _normal(ks[9], (L, N_GATE_GROUPS, CHUNK), 0.1),
        'out_norm_attn_g': 1.0 + _normal(ks[10], (L, ATTN_WIDTH), 0.1),
        'out_norm_gate_g': 1.0 + _normal(ks[11], (L, GATE_WIDTH), 0.1),
        'w_out': _normal(ks[12], (L, MIX_WIDTH, D_MODEL), MIX_WIDTH ** -0.5),
        'norm2_g': 1.0 + _normal(ks[13], (L, D_MODEL), 0.1),
        'w_ffn_gate': _normal(ks[14], (L, D_MODEL, D_FF), D_MODEL ** -0.5),
        'w_ffn_up': _normal(ks[15], (L, D_MODEL, D_FF), D_MODEL ** -0.5),
        'w_ffn_down': _normal(ks[16], (L, D_FF, D_MODEL), D_FF ** -0.5),
    }


def _fwd_reference(x, norm1_g, w_in, q_norm_g, k_norm_g, attn_sinks, gate_ln_g, gate_ln_b,
              w_spatial, b_spatial, out_norm_attn_g, out_norm_gate_g, w_out, norm2_g,
              w_ffn_gate, w_ffn_up, w_ffn_down):
    bsz, seq = x.shape[0], x.shape[1]
    for l in range(DEPTH):
        h = _rms_norm(x, norm1_g[l])
        proj = h @ w_in[l]
        q, k, v, z = jnp.split(
            proj, [ATTN_WIDTH, ATTN_WIDTH + KV_WIDTH, ATTN_WIDTH + 2 * KV_WIDTH], axis=-1)
        q = _rms_norm(q.reshape(bsz, seq, N_Q_HEADS, HEAD_DIM), q_norm_g[l])
        k = _rms_norm(k.reshape(bsz, seq, N_KV_HEADS, HEAD_DIM), k_norm_g[l])
        v = v.reshape(bsz, seq, N_KV_HEADS, HEAD_DIM)
        y_attn = _sliding_window_attention(q, k, v, attn_sinks[l])
        y_gate = _spatial_gating(jax.nn.gelu(z, approximate=False), gate_ln_g[l], gate_ln_b[l],
                                 w_spatial[l], b_spatial[l])
        y = jnp.concatenate([_rms_norm(y_attn, out_norm_attn_g[l]),
                             _rms_norm(y_gate, out_norm_gate_g[l])], axis=-1)
        x = x + y @ w_out[l]
        h = _rms_norm(x, norm2_g[l])
        x = x + (jax.nn.silu(h @ w_ffn_gate[l]) * (h @ w_ffn_up[l])) @ w_ffn_down[l]
    return x


import jax as _jax
import jax.numpy as _jnp

TWIN_FORMAT = 'train_step'
FWD_PARAMS = ['x', 'norm1_g', 'w_in', 'q_norm_g', 'k_norm_g', 'attn_sinks', 'gate_ln_g', 'gate_ln_b', 'w_spatial', 'b_spatial', 'out_norm_attn_g', 'out_norm_gate_g', 'w_out', 'norm2_g', 'w_ffn_gate', 'w_ffn_up', 'w_ffn_down']
TWIN_WEIGHTS = ['norm1_g', 'w_in', 'q_norm_g', 'k_norm_g', 'attn_sinks', 'gate_ln_g', 'gate_ln_b', 'w_spatial', 'b_spatial', 'out_norm_attn_g', 'out_norm_gate_g', 'w_out', 'norm2_g', 'w_ffn_gate', 'w_ffn_up', 'w_ffn_down']
TWIN_DIFF_INPUT = 'x'
TWIN_INPUTS = ['x', 'norm1_g', 'w_in', 'q_norm_g', 'k_norm_g', 'attn_sinks', 'gate_ln_g', 'gate_ln_b', 'w_spatial', 'b_spatial', 'out_norm_attn_g', 'out_norm_gate_g', 'w_out', 'norm2_g', 'w_ffn_gate', 'w_ffn_up', 'w_ffn_down', 'loss_target', 'm_norm1_g', 'm_w_in', 'm_q_norm_g', 'm_k_norm_g', 'm_attn_sinks', 'm_gate_ln_g', 'm_gate_ln_b', 'm_w_spatial', 'm_b_spatial', 'm_out_norm_attn_g', 'm_out_norm_gate_g', 'm_w_out', 'm_norm2_g', 'm_w_ffn_gate', 'm_w_ffn_up', 'm_w_ffn_down', 'v_norm1_g', 'v_w_in', 'v_q_norm_g', 'v_k_norm_g', 'v_attn_sinks', 'v_gate_ln_g', 'v_gate_ln_b', 'v_w_spatial', 'v_b_spatial', 'v_out_norm_attn_g', 'v_out_norm_gate_g', 'v_w_out', 'v_norm2_g', 'v_w_ffn_gate', 'v_w_ffn_up', 'v_w_ffn_down']
TWIN_OUTPUTS = ['loss', 'grad_x', 'grad_norm1_g', 'grad_w_in', 'grad_q_norm_g', 'grad_k_norm_g', 'grad_attn_sinks', 'grad_gate_ln_g', 'grad_gate_ln_b', 'grad_w_spatial', 'grad_b_spatial', 'grad_out_norm_attn_g', 'grad_out_norm_gate_g', 'grad_w_out', 'grad_norm2_g', 'grad_w_ffn_gate', 'grad_w_ffn_up', 'grad_w_ffn_down', 'delta_norm1_g', 'delta_w_in', 'delta_q_norm_g', 'delta_k_norm_g', 'delta_attn_sinks', 'delta_gate_ln_g', 'delta_gate_ln_b', 'delta_w_spatial', 'delta_b_spatial', 'delta_out_norm_attn_g', 'delta_out_norm_gate_g', 'delta_w_out', 'delta_norm2_g', 'delta_w_ffn_gate', 'delta_w_ffn_up', 'delta_w_ffn_down', 'new_m_norm1_g', 'new_m_w_in', 'new_m_q_norm_g', 'new_m_k_norm_g', 'new_m_attn_sinks', 'new_m_gate_ln_g', 'new_m_gate_ln_b', 'new_m_w_spatial', 'new_m_b_spatial', 'new_m_out_norm_attn_g', 'new_m_out_norm_gate_g', 'new_m_w_out', 'new_m_norm2_g', 'new_m_w_ffn_gate', 'new_m_w_ffn_up', 'new_m_w_ffn_down', 'new_v_norm1_g', 'new_v_w_in', 'new_v_q_norm_g', 'new_v_k_norm_g', 'new_v_attn_sinks', 'new_v_gate_ln_g', 'new_v_gate_ln_b', 'new_v_w_spatial', 'new_v_b_spatial', 'new_v_out_norm_attn_g', 'new_v_out_norm_gate_g', 'new_v_w_out', 'new_v_norm2_g', 'new_v_w_ffn_gate', 'new_v_w_ffn_up', 'new_v_w_ffn_down']
TWIN_LEAF_KINDS = {'loss': 'loss', 'grad_x': 'grad_x', 'grad_norm1_g': 'grad_w', 'grad_w_in': 'grad_w', 'grad_q_norm_g': 'grad_w', 'grad_k_norm_g': 'grad_w', 'grad_attn_sinks': 'grad_w', 'grad_gate_ln_g': 'grad_w', 'grad_gate_ln_b': 'grad_w', 'grad_w_spatial': 'grad_w', 'grad_b_spatial': 'grad_w', 'grad_out_norm_attn_g': 'grad_w', 'grad_out_norm_gate_g': 'grad_w', 'grad_w_out': 'grad_w', 'grad_norm2_g': 'grad_w', 'grad_w_ffn_gate': 'grad_w', 'grad_w_ffn_up': 'grad_w', 'grad_w_ffn_down': 'grad_w', 'delta_norm1_g': 'delta_w', 'delta_w_in': 'delta_w', 'delta_q_norm_g': 'delta_w', 'delta_k_norm_g': 'delta_w', 'delta_attn_sinks': 'delta_w', 'delta_gate_ln_g': 'delta_w', 'delta_gate_ln_b': 'delta_w', 'delta_w_spatial': 'delta_w', 'delta_b_spatial': 'delta_w', 'delta_out_norm_attn_g': 'delta_w', 'delta_out_norm_gate_g': 'delta_w', 'delta_w_out': 'delta_w', 'delta_norm2_g': 'delta_w', 'delta_w_ffn_gate': 'delta_w', 'delta_w_ffn_up': 'delta_w', 'delta_w_ffn_down': 'delta_w', 'new_m_norm1_g': 'new_m', 'new_m_w_in': 'new_m', 'new_m_q_norm_g': 'new_m', 'new_m_k_norm_g': 'new_m', 'new_m_attn_sinks': 'new_m', 'new_m_gate_ln_g': 'new_m', 'new_m_gate_ln_b': 'new_m', 'new_m_w_spatial': 'new_m', 'new_m_b_spatial': 'new_m', 'new_m_out_norm_attn_g': 'new_m', 'new_m_out_norm_gate_g': 'new_m', 'new_m_w_out': 'new_m', 'new_m_norm2_g': 'new_m', 'new_m_w_ffn_gate': 'new_m', 'new_m_w_ffn_up': 'new_m', 'new_m_w_ffn_down': 'new_m', 'new_v_norm1_g': 'new_v', 'new_v_w_in': 'new_v', 'new_v_q_norm_g': 'new_v', 'new_v_k_norm_g': 'new_v', 'new_v_attn_sinks': 'new_v', 'new_v_gate_ln_g': 'new_v', 'new_v_gate_ln_b': 'new_v', 'new_v_w_spatial': 'new_v', 'new_v_b_spatial': 'new_v', 'new_v_out_norm_attn_g': 'new_v', 'new_v_out_norm_gate_g': 'new_v', 'new_v_w_out': 'new_v', 'new_v_norm2_g': 'new_v', 'new_v_w_ffn_gate': 'new_v', 'new_v_w_ffn_up': 'new_v', 'new_v_w_ffn_down': 'new_v'}


def _forward(args):
    return _fwd_reference(*[args[k] for k in FWD_PARAMS])


def _output_shape():
    out = _jax.eval_shape(lambda: _forward(_fwd_setup_inputs(0)))
    return out.shape, out.dtype

N_MICROBATCH = 1
ADAM_LR = 0.001
ADAM_B1 = 0.9
ADAM_B2 = 0.999
ADAM_EPS = 1e-08
ADAM_WD = 0.01
ADAM_STEP = 10
PER_EXAMPLE_BATCH_AXIS = {'x': 0, 'loss_target': 0}
SHARED_INPUTS = []
_WEIGHT_DTYPES = {'norm1_g': _jnp.float32, 'w_in': _jnp.float32, 'q_norm_g': _jnp.float32, 'k_norm_g': _jnp.float32, 'attn_sinks': _jnp.float32, 'gate_ln_g': _jnp.float32, 'gate_ln_b': _jnp.float32, 'w_spatial': _jnp.float32, 'b_spatial': _jnp.float32, 'out_norm_attn_g': _jnp.float32, 'out_norm_gate_g': _jnp.float32, 'w_out': _jnp.float32, 'norm2_g': _jnp.float32, 'w_ffn_gate': _jnp.float32, 'w_ffn_up': _jnp.float32, 'w_ffn_down': _jnp.float32}
MOMENT_SCALE = {'norm1_g': 3.524862e-01, 'w_in': 2.526703e-01, 'q_norm_g': 9.725611e-01, 'k_norm_g': 1.094187e+00, 'attn_sinks': 7.218838e+00, 'gate_ln_g': 7.883549e-01, 'gate_ln_b': 1.293884e-01, 'w_spatial': 1.146674e-01, 'b_spatial': 1.932054e-01, 'out_norm_attn_g': 1.903500e+01, 'out_norm_gate_g': 1.610901e+01, 'w_out': 1.077735e+00, 'norm2_g': 1.245419e+01, 'w_ffn_gate': 2.439894e-01, 'w_ffn_up': 1.286994e-01, 'w_ffn_down': 1.843770e-01}


def _to_microbatches(a, axis):
    t = _jnp.moveaxis(a, axis, 0)
    t = t.reshape((N_MICROBATCH, t.shape[0] // N_MICROBATCH) + t.shape[1:])
    return _jnp.moveaxis(t, 1, axis + 1)


def setup_inputs(seed: int = 0) -> dict:
    inp = _fwd_setup_inputs(seed)
    key = _jax.random.fold_in(_jax.random.key(seed), 7919)
    shape, _ = _output_shape()
    out = dict(inp)
    out["loss_target"] = _jax.random.normal(_jax.random.fold_in(key, 0), shape, _jnp.float32)
    for i, name in enumerate(TWIN_WEIGHTS):
        w = inp[name].astype(_jnp.float32)
        if MOMENT_SCALE is None:
            s = _jnp.sqrt(_jnp.mean(_jnp.square(w)) + 1e-30)
        else:
            s = MOMENT_SCALE[name]
        km, kv = _jax.random.split(_jax.random.fold_in(key, i + 1))
        out[name] = w
        out["m_" + name] = s * _jax.random.normal(km, w.shape, _jnp.float32)
        out["v_" + name] = (s * s) * _jax.random.uniform(kv, w.shape, _jnp.float32, 0.5, 1.5)
    if N_MICROBATCH > 1:
        for name, axis in PER_EXAMPLE_BATCH_AXIS.items():
            out[name] = _to_microbatches(out[name], axis)
    return {'x': out['x'], 'norm1_g': out['norm1_g'], 'w_in': out['w_in'], 'q_norm_g': out['q_norm_g'], 'k_norm_g': out['k_norm_g'], 'attn_sinks': out['attn_sinks'], 'gate_ln_g': out['gate_ln_g'], 'gate_ln_b': out['gate_ln_b'], 'w_spatial': out['w_spatial'], 'b_spatial': out['b_spatial'], 'out_norm_attn_g': out['out_norm_attn_g'], 'out_norm_gate_g': out['out_norm_gate_g'], 'w_out': out['w_out'], 'norm2_g': out['norm2_g'], 'w_ffn_gate': out['w_ffn_gate'], 'w_ffn_up': out['w_ffn_up'], 'w_ffn_down': out['w_ffn_down'], 'loss_target': out['loss_target'], 'm_norm1_g': out['m_norm1_g'], 'm_w_in': out['m_w_in'], 'm_q_norm_g': out['m_q_norm_g'], 'm_k_norm_g': out['m_k_norm_g'], 'm_attn_sinks': out['m_attn_sinks'], 'm_gate_ln_g': out['m_gate_ln_g'], 'm_gate_ln_b': out['m_gate_ln_b'], 'm_w_spatial': out['m_w_spatial'], 'm_b_spatial': out['m_b_spatial'], 'm_out_norm_attn_g': out['m_out_norm_attn_g'], 'm_out_norm_gate_g': out['m_out_norm_gate_g'], 'm_w_out': out['m_w_out'], 'm_norm2_g': out['m_norm2_g'], 'm_w_ffn_gate': out['m_w_ffn_gate'], 'm_w_ffn_up': out['m_w_ffn_up'], 'm_w_ffn_down': out['m_w_ffn_down'], 'v_norm1_g': out['v_norm1_g'], 'v_w_in': out['v_w_in'], 'v_q_norm_g': out['v_q_norm_g'], 'v_k_norm_g': out['v_k_norm_g'], 'v_attn_sinks': out['v_attn_sinks'], 'v_gate_ln_g': out['v_gate_ln_g'], 'v_gate_ln_b': out['v_gate_ln_b'], 'v_w_spatial': out['v_w_spatial'], 'v_b_spatial': out['v_b_spatial'], 'v_out_norm_attn_g': out['v_out_norm_attn_g'], 'v_out_norm_gate_g': out['v_out_norm_gate_g'], 'v_w_out': out['v_w_out'], 'v_norm2_g': out['v_norm2_g'], 'v_w_ffn_gate': out['v_w_ffn_gate'], 'v_w_ffn_up': out['v_w_ffn_up'], 'v_w_ffn_down': out['v_w_ffn_down']}


def _loss(weights, diff, rest, loss_target):
    with _jax.named_scope("forward"):
        args = {**rest, TWIN_DIFF_INPUT: diff, **{k: w.astype(_WEIGHT_DTYPES[k]) for k, w in weights.items()}}
        y = _forward(args)
    with _jax.named_scope("loss_head"):
        err = _jnp.square(y.astype(_jnp.float32) - loss_target)
        return 0.5 * _jnp.sum(_jnp.mean(err, axis=-1)) if err.ndim else 0.5 * err


def _adamw(w, g, m, v):
    m = ADAM_B1 * m + (1.0 - ADAM_B1) * g
    v = ADAM_B2 * v + (1.0 - ADAM_B2) * _jnp.square(g)
    m_hat = m / (1.0 - ADAM_B1 ** ADAM_STEP)
    v_hat = v / (1.0 - ADAM_B2 ** ADAM_STEP)
    delta = -ADAM_LR * (m_hat / (_jnp.sqrt(v_hat) + ADAM_EPS) + ADAM_WD * w)
    return delta, m, v


def reference(x, norm1_g, w_in, q_norm_g, k_norm_g, attn_sinks, gate_ln_g, gate_ln_b, w_spatial, b_spatial, out_norm_attn_g, out_norm_gate_g, w_out, norm2_g, w_ffn_gate, w_ffn_up, w_ffn_down, loss_target, m_norm1_g, m_w_in, m_q_norm_g, m_k_norm_g, m_attn_sinks, m_gate_ln_g, m_gate_ln_b, m_w_spatial, m_b_spatial, m_out_norm_attn_g, m_out_norm_gate_g, m_w_out, m_norm2_g, m_w_ffn_gate, m_w_ffn_up, m_w_ffn_down, v_norm1_g, v_w_in, v_q_norm_g, v_k_norm_g, v_attn_sinks, v_gate_ln_g, v_gate_ln_b, v_w_spatial, v_b_spatial, v_out_norm_attn_g, v_out_norm_gate_g, v_w_out, v_norm2_g, v_w_ffn_gate, v_w_ffn_up, v_w_ffn_down):
    given = dict(x=x, norm1_g=norm1_g, w_in=w_in, q_norm_g=q_norm_g, k_norm_g=k_norm_g, attn_sinks=attn_sinks, gate_ln_g=gate_ln_g, gate_ln_b=gate_ln_b, w_spatial=w_spatial, b_spatial=b_spatial, out_norm_attn_g=out_norm_attn_g, out_norm_gate_g=out_norm_gate_g, w_out=w_out, norm2_g=norm2_g, w_ffn_gate=w_ffn_gate, w_ffn_up=w_ffn_up, w_ffn_down=w_ffn_down, loss_target=loss_target, m_norm1_g=m_norm1_g, m_w_in=m_w_in, m_q_norm_g=m_q_norm_g, m_k_norm_g=m_k_norm_g, m_attn_sinks=m_attn_sinks, m_gate_ln_g=m_gate_ln_g, m_gate_ln_b=m_gate_ln_b, m_w_spatial=m_w_spatial, m_b_spatial=m_b_spatial, m_out_norm_attn_g=m_out_norm_attn_g, m_out_norm_gate_g=m_out_norm_gate_g, m_w_out=m_w_out, m_norm2_g=m_norm2_g, m_w_ffn_gate=m_w_ffn_gate, m_w_ffn_up=m_w_ffn_up, m_w_ffn_down=m_w_ffn_down, v_norm1_g=v_norm1_g, v_w_in=v_w_in, v_q_norm_g=v_q_norm_g, v_k_norm_g=v_k_norm_g, v_attn_sinks=v_attn_sinks, v_gate_ln_g=v_gate_ln_g, v_gate_ln_b=v_gate_ln_b, v_w_spatial=v_w_spatial, v_b_spatial=v_b_spatial, v_out_norm_attn_g=v_out_norm_attn_g, v_out_norm_gate_g=v_out_norm_gate_g, v_w_out=v_w_out, v_norm2_g=v_norm2_g, v_w_ffn_gate=v_w_ffn_gate, v_w_ffn_up=v_w_ffn_up, v_w_ffn_down=v_w_ffn_down)
    weights = {n: given[n] for n in TWIN_WEIGHTS}
    shared = {n: given[n] for n in SHARED_INPUTS}
    per_example = {n: given[n] for n in ['x']}
    grad_fn = _jax.value_and_grad(_loss, argnums=(0, 1))

    def one_microbatch(ex, loss_target):
        ex = dict(ex)
        diff = ex.pop(TWIN_DIFF_INPUT)
        return grad_fn(weights, diff, {**shared, **ex}, loss_target)

    if N_MICROBATCH == 1:
        loss, (grad_w, grad_x) = one_microbatch(per_example, given["loss_target"])
    else:
        def body(carry, xs):
            loss_sum, grad_sum = carry
            l_k, (gw_k, gx_k) = one_microbatch(xs[0], xs[1])
            with _jax.named_scope("update"):
                return (loss_sum + l_k, _jax.tree.map(_jnp.add, grad_sum, gw_k)), gx_k

        init = (_jnp.zeros((), _jnp.float32), _jax.tree.map(_jnp.zeros_like, weights))
        (loss, grad_w), grad_x = _jax.lax.scan(body, init, (per_example, given["loss_target"]))
    with _jax.named_scope("update"):
        delta_w, new_m, new_v = {}, {}, {}
        for n in TWIN_WEIGHTS:
            delta_w[n], new_m[n], new_v[n] = _adamw(weights[n], grad_w[n], given["m_" + n], given["v_" + n])
    return (loss, grad_x, *[grad_w[n] for n in TWIN_WEIGHTS], *[delta_w[n] for n in TWIN_WEIGHTS],
            *[new_m[n] for n in TWIN_WEIGHTS], *[new_v[n] for n in TWIN_WEIGHTS])
```

```python
import functools

import jax
import jax.numpy as jnp
from jax import lax
from jax.experimental import pallas as pl
from jax.experimental.pallas import tpu as pltpu

F32 = jnp.float32
BF16 = jnp.bfloat16
MESH = pl.DeviceIdType.MESH

EPS = 1e-6
HEAD_DIM = 64
N_KV_HEADS = 2
BLK = 128
N_CHIPS = 4
N_DEV = 8
NEG = -1e30

ADAM_LR = 0.001
ADAM_B1 = 0.9
ADAM_B2 = 0.999
ADAM_EPS = 1e-08
ADAM_WD = 0.01
ADAM_STEP = 10

VMEM_LIMIT = 48 * 1024 * 1024

NN = (((1,), (0,)), ((), ()))
NT = (((1,), (1,)), ((), ()))
TN = (((0,), (0,)), ((), ()))
HBM = pl.BlockSpec(memory_space=pltpu.HBM)


def _dot(a, b, dn):
    return lax.dot_general(a, b, dn, preferred_element_type=F32)


def _pick(dim, pref, align=128):
    if dim <= pref:
        return dim
    t = (pref // align) * align
    while t >= align:
        if dim % t == 0:
            return t
        t -= align
    return dim


def _params(sem):
    return pltpu.CompilerParams(dimension_semantics=sem, vmem_limit_bytes=VMEM_LIMIT)


def _mm_body(dn, nk, has_add, *refs):
    if has_add:
        a_ref, b_ref, add_ref, o_ref, acc_ref = refs
    else:
        a_ref, b_ref, o_ref, acc_ref = refs
        add_ref = None
    k = pl.program_id(2)
    p = _dot(a_ref[...].astype(BF16), b_ref[...].astype(BF16), dn)

    def finish(r):
        if add_ref is not None:
            r = r + add_ref[...]
        o_ref[...] = r.astype(o_ref.dtype)

    if nk == 1:
        finish(p)
        return

    @pl.when(k == 0)
    def _():
        acc_ref[...] = p

    @pl.when(k > 0)
    def _():
        acc_ref[...] += p

    @pl.when(k == nk - 1)
    def _():
        finish(acc_ref[...])


def _matmul(a, b, mode, out_dtype, name, add=None, b_shard=None, out_shard=None,
            tm=1024, tn=1024, tk=512):
    if mode == "nn":
        M, K = a.shape
        N = b.shape[0] * b.shape[2] if b_shard == "c" else (b.shape[-1])
    elif mode == "nt":
        M, K = a.shape
        N = b.shape[0] * b.shape[1] if b_shard == "r" else (b.shape[1] if b_shard == "c" else b.shape[0])
    else:
        K, M = a.shape
        N = b.shape[-1]
    slab_r = slab_c = None
    if b_shard == "c":
        slab_c = b.shape[2]
        tn = _pick(slab_c, tn)
    elif b_shard == "r":
        if mode == "nn":
            slab_r = b.shape[1]
            tk = _pick(slab_r, tk)
        else:
            slab_r = b.shape[1]
            tn = _pick(slab_r, tn)
    if out_shard == "c":
        tn = _pick(N // N_CHIPS, tn)
    elif out_shard == "r":
        tm = _pick(M // N_CHIPS, tm)
    tm, tn, tk = _pick(M, tm), _pick(N, tn), _pick(K, tk)
    gm, gn, gk = M // tm, N // tn, K // tk

    if mode == "nn":
        a_spec = pl.BlockSpec((tm, tk), lambda i, j, k: (i, k))
        if b_shard == "c":
            per = slab_c // tn
            b_spec = pl.BlockSpec((None, tk, tn), lambda i, j, k: (j // per, k, j % per))
        elif b_shard == "r":
            per = slab_r // tk
            b_spec = pl.BlockSpec((None, tk, tn), lambda i, j, k: (k // per, k % per, j))
        else:
            b_spec = pl.BlockSpec((tk, tn), lambda i, j, k: (k, j))
    elif mode == "nt":
        a_spec = pl.BlockSpec((tm, tk), lambda i, j, k: (i, k))
        if b_shard == "r":
            per = slab_r // tn
            b_spec = pl.BlockSpec((None, tn, tk), lambda i, j, k: (j // per, j % per, k))
        elif b_shard == "c":
            tk = _pick(b.shape[2], tk)
            gk = K // tk
            per = b.shape[2] // tk
            a_spec = pl.BlockSpec((tm, tk), lambda i, j, k: (i, k))
            b_spec = pl.BlockSpec((None, tn, tk), lambda i, j, k: (k // per, j, k % per))
        else:
            b_spec = pl.BlockSpec((tn, tk), lambda i, j, k: (j, k))
    else:
        a_spec = pl.BlockSpec((tk, tm), lambda i, j, k: (k, i))
        b_spec = pl.BlockSpec((tk, tn), lambda i, j, k: (k, j))

    if out_shard == "c":
        per = (N // N_CHIPS) // tn
        o_spec = pl.BlockSpec((None, tm, tn), lambda i, j, k: (j // per, i, j % per))
        o_shape = jax.ShapeDtypeStruct((N_CHIPS, M, N // N_CHIPS), out_dtype)
    elif out_shard == "r":
        per = (M // N_CHIPS) // tm
        o_spec = pl.BlockSpec((None, tm, tn), lambda i, j, k: (i // per, i % per, j))
        o_shape = jax.ShapeDtypeStruct((N_CHIPS, M // N_CHIPS, N), out_dtype)
    else:
        o_spec = pl.BlockSpec((tm, tn), lambda i, j, k: (i, j))
        o_shape = jax.ShapeDtypeStruct((M, N), out_dtype)

    dn = {"nn": NN, "nt": NT, "tn": TN}[mode]
    in_specs = [a_spec, b_spec]
    args = [a, b]
    if add is not None:
        in_specs.append(pl.BlockSpec((tm, tn), lambda i, j, k: (i, j)))
        args.append(add)
    return pl.pallas_call(
        functools.partial(_mm_body, dn, gk, add is not None),
        name=name,
        grid=(gm, gn, gk),
        in_specs=in_specs,
        out_specs=o_spec,
        out_shape=o_shape,
        scratch_shapes=[pltpu.VMEM((tm, tn), F32)],
        compiler_params=_params(("parallel", "parallel", "arbitrary")),
    )(*args)


def _rms_fwd_body(x_ref, g_ref, h_ref):
    x = x_ref[...]
    r = lax.rsqrt(jnp.mean(x * x, axis=-1, keepdims=True) + EPS)
    h_ref[...] = (x * r * g_ref[...]).astype(h_ref.dtype)


def _rms_fwd(x, g, name):
    T, D = x.shape
    tr = _pick(T, 256, 8)
    return pl.pallas_call(
        functools.partial(_rms_fwd_body),
        name=name,
        grid=(T // tr,),
        in_specs=[pl.BlockSpec((tr, D), lambda i: (i, 0)), pl.BlockSpec((1, D), lambda i: (0, 0))],
        out_specs=pl.BlockSpec((tr, D), lambda i: (i, 0)),
        out_shape=jax.ShapeDtypeStruct((T, D), BF16),
        compiler_params=_params(("parallel",)),
    )(x, g)


def _rms_bwd_body(x_ref, g_ref, dh_ref, res_ref, dx_ref, dg_ref):
    @pl.when(pl.program_id(0) == 0)
    def _():
        dg_ref[...] = jnp.zeros_like(dg_ref)

    x = x_ref[...]
    r = lax.rsqrt(jnp.mean(x * x, axis=-1, keepdims=True) + EPS)
    xh = x * r
    dh = dh_ref[...]
    dg_ref[...] += jnp.sum(dh * xh, axis=0, keepdims=True)
    t = dh * g_ref[...]
    dx_ref[...] = res_ref[...] + r * (t - xh * jnp.mean(t * xh, axis=-1, keepdims=True))


def _rms_bwd(x, g, dh, res, name):
    T, D = x.shape
    tr = _pick(T, 256, 8)
    row = pl.BlockSpec((tr, D), lambda i: (i, 0))
    vec = pl.BlockSpec((1, D), lambda i: (0, 0))
    return pl.pallas_call(
        functools.partial(_rms_bwd_body),
        name=name,
        grid=(T // tr,),
        in_specs=[row, vec, row, row],
        out_specs=[row, vec],
        out_shape=[jax.ShapeDtypeStruct((T, D), F32), jax.ShapeDtypeStruct((1, D), F32)],
        compiler_params=_params(("arbitrary",)),
    )(x, g, dh, res)


def _swiglu_fwd_body(a_ref, b_ref, f_ref):
    a = a_ref[...]
    f_ref[...] = (a * (1.0 / (1.0 + jnp.exp(-a))) * b_ref[...]).astype(f_ref.dtype)


def _swiglu_fwd(a, b):
    T, F = a.shape
    tr, tc = _pick(T, 512, 8), _pick(F, 1408)
    blk = pl.BlockSpec((tr, tc), lambda i, j: (i, j))
    return pl.pallas_call(
        functools.partial(_swiglu_fwd_body),
        name="swiglu_fwd",
        grid=(T // tr, F // tc),
        in_specs=[blk, blk],
        out_specs=blk,
        out_shape=jax.ShapeDtypeStruct((T, F), BF16),
        compiler_params=_params(("parallel", "parallel")),
    )(a, b)


def _swiglu_bwd_body(df_ref, a_ref, b_ref, da_ref, db_ref):
    a = a_ref[...]
    df = df_ref[...]
    s = 1.0 / (1.0 + jnp.exp(-a))
    da_ref[...] = (df * b_ref[...] * (s * (1.0 + a * (1.0 - s)))).astype(da_ref.dtype)
    db_ref[...] = (df * (a * s)).astype(db_ref.dtype)


def _swiglu_bwd(df, a, b):
    T, F = a.shape
    tr, tc = _pick(T, 512, 8), _pick(F, 1408)
    blk = pl.BlockSpec((tr, tc), lambda i, j: (i, j))
    return pl.pallas_call(
        functools.partial(_swiglu_bwd_body),
        name="swiglu_bwd",
        grid=(T // tr, F // tc),
        in_specs=[blk, blk, blk],
        out_specs=[blk, blk],
        out_shape=[jax.ShapeDtypeStruct((T, F), BF16), jax.ShapeDtypeStruct((T, F), BF16)],
        compiler_params=_params(("parallel", "parallel")),
    )(df, a, b)


def _loss_body(inv_d, x2_ref, tgt_ref, dx2_ref, loss_ref):
    @pl.when(pl.program_id(0) == 0)
    def _():
        loss_ref[...] = jnp.zeros_like(loss_ref)

    e = x2_ref[...] - tgt_ref[...]
    dx2_ref[...] = e * inv_d
    row = jnp.sum(e * e, axis=-1, keepdims=True) * inv_d
    loss_ref[...] += 0.5 * jnp.sum(row, axis=0, keepdims=True)


def _loss(x2, tgt):
    T, D = x2.shape
    tr = _pick(T, 256, 8)
    row = pl.BlockSpec((tr, D), lambda i: (i, 0))
    return pl.pallas_call(
        functools.partial(_loss_body, 1.0 / D),
        name="loss_head",
        grid=(T // tr,),
        in_specs=[row, row],
        out_specs=[row, pl.BlockSpec((1, 1), lambda i: (0, 0))],
        out_shape=[jax.ShapeDtypeStruct((T, D), F32), jax.ShapeDtypeStruct((1, 1), F32)],
        compiler_params=_params(("arbitrary",)),
    )(x2, tgt)


def _lo_mask(shape):
    return lax.broadcasted_iota(jnp.int32, shape, len(shape) - 1) < HEAD_DIM


def _half_sums(t, lo):
    s_lo = jnp.sum(jnp.where(lo, t, 0.0), axis=-1, keepdims=True)
    s_hi = jnp.sum(jnp.where(lo, 0.0, t), axis=-1, keepdims=True)
    return jnp.where(lo, s_lo, s_hi)


def _head_rstd(t, lo):
    return lax.rsqrt(_half_sums(t * t, lo) * (1.0 / HEAD_DIM) + EPS)


def _place(t, lo, kv_head):
    if kv_head == 0:
        t_lo = jnp.where(lo, t, 0.0)
        t_hi = pltpu.roll(t_lo, HEAD_DIM, 1)
    else:
        t_hi = jnp.where(lo, 0.0, t)
        t_lo = pltpu.roll(t_hi, HEAD_DIM, 1)
    return jnp.concatenate([t_lo, t_hi], axis=0).astype(BF16)


def _unplace(c0, c1, lo):
    return jnp.where(lo, c0 + pltpu.roll(c0, HEAD_DIM, 1), c1 + pltpu.roll(c1, HEAD_DIM, 1))


def _band(kv_cur, kv_prev, kg, lo2):
    kb = jnp.concatenate([kv_prev[:, :BLK], kv_cur[:, :BLK]], axis=0)
    vb = jnp.concatenate([kv_prev[:, BLK:], kv_cur[:, BLK:]], axis=0)
    rk = _head_rstd(kb, lo2)
    kn = kb * rk * kg
    kk = [_place(kn, lo2, h) for h in range(N_KV_HEADS)]
    vv = [_place(vb, lo2, h) for h in range(N_KV_HEADS)]
    return kb, rk, kk, vv


def _score_geometry(first_i32):
    qi = lax.broadcasted_iota(jnp.int32, (BLK, 4 * BLK), 0)
    col = lax.broadcasted_iota(jnp.int32, (BLK, 4 * BLK), 1)
    kj = col & (2 * BLK - 1)
    dist = qi + BLK - kj
    valid = (dist >= 0) & (dist < BLK) & (kj >= first_i32 * BLK)
    return col, dist.astype(F32), valid


def _pair_probs(qn, kk, col, distf, valid, slope0, slope1, sink0, sink1):
    s = _dot(qn.astype(BF16), kk, NT) * (HEAD_DIM ** -0.5)
    slope = jnp.where(col < 2 * BLK, slope0, slope1)
    logits = jnp.where(valid, s - slope * distf, NEG)
    probs, psink = [], []
    for hh, sk in ((0, sink0), (1, sink1)):
        l = logits[:, 2 * BLK * hh:2 * BLK * (hh + 1)]
        m = jnp.maximum(jnp.max(l, axis=-1, keepdims=True), sk)
        p = jnp.exp(l - m)
        es = jnp.exp(sk - m)
        inv = 1.0 / (jnp.sum(p, axis=-1, keepdims=True) + es)
        probs.append(p * inv)
        psink.append(es * inv)
    return probs, psink


def _gelu(z):
    return 0.5 * z * (1.0 + lax.erf(z * (0.5 ** 0.5)))


def _gelu_grad(z):
    return 0.5 * (1.0 + lax.erf(z * (0.5 ** 0.5))) + z * jnp.exp(-0.5 * z * z) * ((2.0 * jnp.pi) ** -0.5)


def _tril_w(w):
    r = lax.broadcasted_iota(jnp.int32, (BLK, BLK), 0)
    c = lax.broadcasted_iota(jnp.int32, (BLK, BLK), 1)
    return jnp.where(r >= c, w, 0.0), r >= c


def _gate_fwd_group(zu, zv, lg, lb, w, bcol):
    u = _gelu(zu)
    v = _gelu(zv)
    mu = jnp.mean(v, axis=-1, keepdims=True)
    vc = v - mu
    rs = lax.rsqrt(jnp.mean(vc * vc, axis=-1, keepdims=True) + EPS)
    vh = vc * rs
    vn = vh * lg + lb
    wt, tril = _tril_w(w)
    mixed = _dot(wt.astype(BF16), vn.astype(BF16), NN) + bcol
    return u, vh, rs, vn, wt, tril, mixed


class _Dims:
    def __init__(self, seq, attn_w, gate_w):
        self.seq, self.attn_w, self.gate_w = seq, attn_w, gate_w
        self.n_heads = attn_w // HEAD_DIM
        self.group = self.n_heads // N_KV_HEADS
        self.n_pairs = attn_w // BLK
        self.n_groups = gate_w // BLK
        self.kv_col = attn_w // (2 * BLK)
        self.u0 = attn_w + 2 * BLK
        self.v0 = self.u0 + gate_w
        self.in_w = self.v0 + gate_w
        self.slopes = [2.0 ** (-8.0 * (h + 1) / self.n_heads) for h in range(self.n_heads)]


def _mixer_fwd_body(d, sink_ref, proj_ref, kvp_ref, qg_ref, kg_ref, lg_ref, lb_ref, w_ref, b_ref, goa_ref, gog_ref,
                    ya_ref, yg_ref, y_ref):
    i = pl.program_id(0)
    first = (i % (d.seq // BLK) == 0).astype(jnp.int32)
    lo = _lo_mask((BLK, BLK))
    lo2 = _lo_mask((2 * BLK, BLK))
    kv_cur = proj_ref[:, d.attn_w:d.attn_w + 2 * BLK]
    _, _, kk, vv = _band(kv_cur, kvp_ref[...], kg_ref[...], lo2)
    col, distf, valid = _score_geometry(first)
    qg = qg_ref[...]
    for j in range(d.n_pairs):
        h0, h1 = 2 * j, 2 * j + 1
        kh = h0 // d.group
        q2 = proj_ref[:, BLK * j:BLK * (j + 1)]
        qn = q2 * _head_rstd(q2, lo) * qg
        probs, _ = _pair_probs(qn, kk[kh], col, distf, valid, d.slopes[h0], d.slopes[h1],
                               sink_ref[0, h0], sink_ref[0, h1])
        p = jnp.concatenate(probs, axis=1).astype(BF16)
        ya_ref[:, BLK * j:BLK * (j + 1)] = _dot(p, vv[kh], NN)
    for g in range(d.n_groups):
        zu = proj_ref[:, d.u0 + BLK * g:d.u0 + BLK * (g + 1)]
        zv = proj_ref[:, d.v0 + BLK * g:d.v0 + BLK * (g + 1)]
        u, _, _, _, _, _, mixed = _gate_fwd_group(zu, zv, lg_ref[g:g + 1, :], lb_ref[g:g + 1, :], w_ref[g], b_ref[g])
        yg_ref[:, BLK * g:BLK * (g + 1)] = u * mixed
    ya = ya_ref[...]
    ra = lax.rsqrt(jnp.mean(ya * ya, axis=-1, keepdims=True) + EPS)
    y_ref[:, :d.attn_w] = (ya * ra * goa_ref[...]).astype(y_ref.dtype)
    yg = yg_ref[...]
    rg = lax.rsqrt(jnp.mean(yg * yg, axis=-1, keepdims=True) + EPS)
    y_ref[:, d.attn_w:] = (yg * rg * gog_ref[...]).astype(y_ref.dtype)


def _mixer_specs(d, T):
    row = lambda w: pl.BlockSpec((BLK, w), lambda i: (i, 0))
    const2 = lambda a: pl.BlockSpec(a.shape, lambda i: (0, 0))
    const3 = lambda a: pl.BlockSpec(a.shape, lambda i: (0, 0, 0))
    kv_prev = pl.BlockSpec((BLK, 2 * BLK), lambda i: (jnp.maximum(i - 1, 0), d.kv_col))
    return row, const2, const3, kv_prev


def _mixer_fwd(d, proj, sinks, qg2, kg2, lg, lb, wsp, bcol, goa, gog):
    T = proj.shape[0]
    row, const2, const3, kv_prev = _mixer_specs(d, T)
    return pl.pallas_call(
        functools.partial(_mixer_fwd_body, d),
        name="mixer_fwd",
        grid=(T // BLK,),
        in_specs=[pl.BlockSpec(memory_space=pltpu.SMEM), row(d.in_w), kv_prev, const2(qg2), const2(kg2),
                  const2(lg), const2(lb), const3(wsp), const3(bcol), const2(goa), const2(gog)],
        out_specs=[row(d.attn_w), row(d.gate_w), row(d.attn_w + d.gate_w)],
        out_shape=[jax.ShapeDtypeStruct((T, d.attn_w), F32), jax.ShapeDtypeStruct((T, d.gate_w), F32),
                   jax.ShapeDtypeStruct((T, d.attn_w + d.gate_w), BF16)],
        compiler_params=_params(("parallel",)),
    )(sinks, proj, proj, qg2, kg2, lg, lb, wsp, bcol, goa, gog)


def _mixer_bwd_body(d, sink_ref, proj_ref, kvp_ref, ya_ref, yg_ref, dy_ref, qg_ref, kg_ref, lg_ref, lb_ref, w_ref,
                    b_ref, goa_ref, gog_ref,
                    dproj_ref, dkv_ref, dqg_ref, dkg_ref, dsk_ref, dlg_ref, dlb_ref, dw_ref, db_ref, dgoa_ref,
                    dgog_ref):
    i = pl.program_id(0)

    @pl.when(i == 0)
    def _():
        for r in (dqg_ref, dkg_ref, dsk_ref, dlg_ref, dlb_ref, dw_ref, db_ref, dgoa_ref, dgog_ref):
            r[...] = jnp.zeros_like(r)

    first = (i % (d.seq // BLK) == 0).astype(jnp.int32)
    lo = _lo_mask((BLK, BLK))
    lo2 = _lo_mask((2 * BLK, BLK))
    lane_row = lax.broadcasted_iota(jnp.int32, (1, BLK), 1)

    ya = ya_ref[...]
    ra = lax.rsqrt(jnp.mean(ya * ya, axis=-1, keepdims=True) + EPS)
    yah = ya * ra
    dyn = dy_ref[:, :d.attn_w]
    dgoa_ref[...] += jnp.sum(dyn * yah, axis=0, keepdims=True)
    t = dyn * goa_ref[...]
    dya = ra * (t - yah * jnp.mean(t * yah, axis=-1, keepdims=True))
    yg = yg_ref[...]
    rg = lax.rsqrt(jnp.mean(yg * yg, axis=-1, keepdims=True) + EPS)
    ygh = yg * rg
    dyn = dy_ref[:, d.attn_w:]
    dgog_ref[...] += jnp.sum(dyn * ygh, axis=0, keepdims=True)
    t = dyn * gog_ref[...]
    dyg = rg * (t - ygh * jnp.mean(t * ygh, axis=-1, keepdims=True))

    kv_cur = proj_ref[:, d.attn_w:d.attn_w + 2 * BLK]
    kg = kg_ref[...]
    kb, rk, kk, vv = _band(kv_cur, kvp_ref[...], kg, lo2)
    col, distf, valid = _score_geometry(first)
    qg = qg_ref[...]
    ck = [jnp.zeros((2 * BLK, BLK), F32) for _ in range(N_KV_HEADS)]
    cv = [jnp.zeros((2 * BLK, BLK), F32) for _ in range(N_KV_HEADS)]
    dsk = jnp.zeros((1, BLK), F32)
    dqg = jnp.zeros((1, BLK), F32)
    for j in range(d.n_pairs):
        h0, h1 = 2 * j, 2 * j + 1
        kh = h0 // d.group
        cols = slice(BLK * j, BLK * (j + 1))
        q2 = proj_ref[:, cols]
        rq = _head_rstd(q2, lo)
        qh = q2 * rq
        qn = qh * qg
        probs, psink = _pair_probs(qn, kk[kh], col, distf, valid, d.slopes[h0], d.slopes[h1],
                                   sink_ref[0, h0], sink_ref[0, h1])
        do2 = dya[:, cols]
        prod = do2 * ya[:, cols]
        delta = (jnp.sum(jnp.where(lo, prod, 0.0), axis=-1, keepdims=True),
                 jnp.sum(jnp.where(lo, 0.0, prod), axis=-1, keepdims=True))
        do2b = do2.astype(BF16)
        dp = _dot(do2b, vv[kh], NT)
        ds = []
        for hh in (0, 1):
            ds.append(probs[hh] * (dp[:, 2 * BLK * hh:2 * BLK * (hh + 1)] - delta[hh]))
            dsink = -jnp.sum(psink[hh] * delta[hh], axis=0, keepdims=True)
            dsk = dsk + jnp.where(lane_row == (h0 + hh), dsink, 0.0)
        dsb = (jnp.concatenate(ds, axis=1) * (HEAD_DIM ** -0.5)).astype(BF16)
        pb = jnp.concatenate(probs, axis=1).astype(BF16)
        qnb = qn.astype(BF16)
        dqn = _dot(dsb, kk[kh], NN)
        dkk = _dot(dsb, qnb, TN)
        dvv = _dot(pb, do2b, TN)
        ck[kh] = ck[kh] + jnp.where(lo2, dkk[:2 * BLK], 0.0) + jnp.where(lo2, 0.0, dkk[2 * BLK:])
        cv[kh] = cv[kh] + jnp.where(lo2, dvv[:2 * BLK], 0.0) + jnp.where(lo2, 0.0, dvv[2 * BLK:])
        dqg = dqg + jnp.sum(dqn * qh, axis=0, keepdims=True)
        t = dqn * qg
        dq2 = rq * (t - qh * (_half_sums(t * qh, lo) * (1.0 / HEAD_DIM)))
        dproj_ref[:, cols] = dq2.astype(dproj_ref.dtype)
    dsk_ref[...] += dsk
    dqg_ref[...] += dqg
    dkn = _unplace(ck[0], ck[1], lo2)
    dvb = _unplace(cv[0], cv[1], lo2)
    khat = kb * rk
    dkg_ref[...] += jnp.sum(dkn * khat, axis=0, keepdims=True)
    t = dkn * kg
    dkb = rk * (t - khat * (_half_sums(t * khat, lo2) * (1.0 / HEAD_DIM)))
    rows_cur = pl.ds(pl.multiple_of(i * BLK, BLK), BLK)
    rows_prev = pl.ds(pl.multiple_of(jnp.maximum(i - 1, 0) * BLK, BLK), BLK)
    dkv_ref[rows_cur, :] = jnp.concatenate([dkb[BLK:], dvb[BLK:]], axis=1)
    dkv_ref[rows_prev, :] += jnp.concatenate([dkb[:BLK], dvb[:BLK]], axis=1)
    dproj_ref[:, d.attn_w:d.attn_w + 2 * BLK] = jnp.zeros((BLK, 2 * BLK), dproj_ref.dtype)

    for g in range(d.n_groups):
        ucols = slice(d.u0 + BLK * g, d.u0 + BLK * (g + 1))
        vcols = slice(d.v0 + BLK * g, d.v0 + BLK * (g + 1))
        zu = proj_ref[:, ucols]
        zv = proj_ref[:, vcols]
        lg = lg_ref[g:g + 1, :]
        u, vh, rs, vn, wt, tril, mixed = _gate_fwd_group(zu, zv, lg, lb_ref[g:g + 1, :], w_ref[g], b_ref[g])
        dyg_g = dyg[:, BLK * g:BLK * (g + 1)]
        du = dyg_g * mixed
        dmix = dyg_g * u
        dmb = dmix.astype(BF16)
        db_ref[g:g + 1, :] += jnp.sum(jnp.transpose(dmix), axis=0, keepdims=True)
        dw_ref[g] += jnp.where(tril, _dot(dmb, vn.astype(BF16), NT), 0.0)
        dvn = _dot(wt.astype(BF16), dmb, TN)
        dlg_ref[g:g + 1, :] += jnp.sum(dvn * vh, axis=0, keepdims=True)
        dlb_ref[g:g + 1, :] += jnp.sum(dvn, axis=0, keepdims=True)
        dvh = dvn * lg
        dv = rs * (dvh - jnp.mean(dvh, axis=-1, keepdims=True) - vh * jnp.mean(dvh * vh, axis=-1, keepdims=True))
        dproj_ref[:, ucols] = (du * _gelu_grad(zu)).astype(dproj_ref.dtype)
        dproj_ref[:, vcols] = (dv * _gelu_grad(zv)).astype(dproj_ref.dtype)


def _mixer_bwd(d, proj, ya, yg, dy, sinks, qg2, kg2, lg, lb, wsp, bcol, goa, gog):
    T = proj.shape[0]
    row, const2, const3, kv_prev = _mixer_specs(d, T)
    acc2 = lambda s: pl.BlockSpec(s, lambda i: (0, 0))
    G = d.n_groups
    out_shapes = [((T, d.in_w), BF16), ((T, 2 * BLK), F32), ((1, BLK), F32), ((1, BLK), F32), ((1, BLK), F32),
                  ((G, BLK), F32), ((G, BLK), F32), ((G, BLK, BLK), F32), ((G, BLK), F32),
                  ((1, d.attn_w), F32), ((1, d.gate_w), F32)]
    out_specs = [row(d.in_w)] + [acc2(s) for s, _ in out_shapes[1:7]] + \
                [pl.BlockSpec((G, BLK, BLK), lambda i: (0, 0, 0))] + [acc2(s) for s, _ in out_shapes[8:]]
    return pl.pallas_call(
        functools.partial(_mixer_bwd_body, d),
        name="mixer_bwd",
        grid=(T // BLK,),
        in_specs=[pl.BlockSpec(memory_space=pltpu.SMEM), row(d.in_w), kv_prev, row(d.attn_w), row(d.gate_w),
                  row(d.attn_w + d.gate_w), const2(qg2), const2(kg2), const2(lg), const2(lb), const3(wsp),
                  const3(bcol), const2(goa), const2(gog)],
        out_specs=out_specs,
        out_shape=[jax.ShapeDtypeStruct(s, t) for s, t in out_shapes],
        compiler_params=_params(("arbitrary",)),
    )(sinks, proj, proj, ya, yg, dy, qg2, kg2, lg, lb, wsp, bcol, goa, gog)


def _put_kv_body(dkv_ref, dproj_in_ref, dproj_ref):
    del dproj_in_ref
    dproj_ref[...] = dkv_ref[...].astype(dproj_ref.dtype)


def _put_kv(d, dproj, dkv):
    T = dproj.shape[0]
    tr = _pick(T, 1024, 16)
    return pl.pallas_call(
        functools.partial(_put_kv_body),
        name="put_kv",
        grid=(T // tr,),
        in_specs=[pl.BlockSpec((tr, 2 * BLK), lambda i: (i, 0)), pl.BlockSpec(memory_space=pl.ANY)],
        out_specs=pl.BlockSpec((tr, 2 * BLK), lambda i: (i, d.kv_col)),
        out_shape=jax.ShapeDtypeStruct(dproj.shape, dproj.dtype),
        input_output_aliases={1: 0},
        compiler_params=_params(("parallel",)),
    )(dkv, dproj)


def _sum_body(n, out_dtype, *refs):
    o_ref = refs[n]
    r = refs[0][...].astype(F32)
    for k in range(1, n):
        r = r + refs[k][...].astype(F32)
    o_ref[...] = r.astype(out_dtype)


def _sum_slabs(parts, out_dtype, name):
    _, R, C = parts[0].shape
    tr = _pick(R, 512, 16)
    specs, args = [], []
    for a in parts:
        for q in range(a.shape[0]):
            specs.append(pl.BlockSpec((None, tr, C), lambda i, q=q: (q, i, 0)))
            args.append(a)
    return pl.pallas_call(
        functools.partial(_sum_body, len(args), out_dtype),
        name=name,
        grid=(R // tr,),
        in_specs=specs,
        out_specs=pl.BlockSpec((tr, C), lambda i: (i, 0)),
        out_shape=jax.ShapeDtypeStruct((R, C), out_dtype),
        compiler_params=_params(("parallel",)),
    )(*args)


def _adamw_body(w_ref, g_ref, m_ref, v_ref, d_ref, nm_ref, nv_ref):
    g = g_ref[...]
    m = ADAM_B1 * m_ref[...] + (1.0 - ADAM_B1) * g
    v = ADAM_B2 * v_ref[...] + (1.0 - ADAM_B2) * (g * g)
    m_hat = m / (1.0 - ADAM_B1 ** ADAM_STEP)
    v_hat = v / (1.0 - ADAM_B2 ** ADAM_STEP)
    d_ref[...] = -ADAM_LR * (m_hat / (jnp.sqrt(v_hat) + ADAM_EPS) + ADAM_WD * w_ref[...])
    nm_ref[...] = m
    nv_ref[...] = v


def _adamw(w, g, m, v, name):
    R, C = w.shape
    tr = _pick(R, 512, 8)
    blk = pl.BlockSpec((tr, C), lambda i: (i, 0))
    return pl.pallas_call(
        functools.partial(_adamw_body),
        name=name,
        grid=(R // tr,),
        in_specs=[blk] * 4,
        out_specs=[blk] * 3,
        out_shape=[jax.ShapeDtypeStruct((R, C), F32)] * 3,
        compiler_params=_params(("parallel",)),
    )(w, g, m, v)


def _me():
    x, y, c = lax.axis_index("x"), lax.axis_index("y"), lax.axis_index("c")
    chips = [(1 - x, y), (x, 1 - y), (1 - x, 1 - y)]
    return x, y, c, chips


def _ag_body(n_items, *refs):
    srcs = refs[:n_items]
    dsts = refs[n_items:2 * n_items]
    send_sems, recv_sems, local_sems = refs[2 * n_items:]
    x, y, c, chips = _me()
    p = 2 * x + y

    def half(ref, q, hc, h):
        return ref.at[q, pl.ds(hc * h, h)]

    def to_chip(it, j, cx, cy, hc):
        h = srcs[it].shape[0] // 2
        k = it * 6 + j
        return pltpu.make_async_remote_copy(
            src_ref=srcs[it].at[pl.ds(hc * h, h)], dst_ref=half(dsts[it], 2 * cx + cy, hc, h),
            send_sem=send_sems.at[k], recv_sem=recv_sems.at[k], device_id=(cx, cy, hc), device_id_type=MESH)

    def to_sibling(it, j, cx, cy, hc, target_c):
        h = srcs[it].shape[0] // 2
        k = it * 6 + 3 + j
        blk = half(dsts[it], 2 * cx + cy, hc, h)
        return pltpu.make_async_remote_copy(
            src_ref=blk, dst_ref=blk, send_sem=send_sems.at[k], recv_sem=recv_sems.at[k],
            device_id=(x, y, target_c), device_id_type=MESH)

    local, sends = [], []
    for it in range(n_items):
        h = srcs[it].shape[0] // 2
        cp = pltpu.make_async_copy(srcs[it], dsts[it].at[p], local_sems.at[it])
        cp.start()
        local.append(cp)
        for j, (cx, cy) in enumerate(chips):
            k = it * 6 + j
            cp = pltpu.make_async_remote_copy(
                src_ref=srcs[it].at[pl.ds(c * h, h)], dst_ref=half(dsts[it], p, c, h),
                send_sem=send_sems.at[k], recv_sem=recv_sems.at[k], device_id=(cx, cy, c), device_id_type=MESH)
            cp.start()
            sends.append(cp)
    for it in range(n_items):
        for j, (cx, cy) in enumerate(chips):
            to_chip(it, j, cx, cy, c).wait_recv()
            cp = to_sibling(it, j, cx, cy, c, 1 - c)
            cp.start()
            sends.append(cp)
    for it in range(n_items):
        for j, (cx, cy) in enumerate(chips):
            to_sibling(it, j, cx, cy, 1 - c, c).wait_recv()
    for cp in sends:
        cp.wait_send()
    for cp in local:
        cp.wait()


def _ag_weights(shards):
    n = len(shards)
    return pl.pallas_call(
        functools.partial(_ag_body, n),
        name="ag_weights",
        in_specs=[HBM] * n,
        out_specs=[HBM] * n,
        out_shape=[jax.ShapeDtypeStruct((N_CHIPS,) + s.shape, s.dtype) for s in shards],
        scratch_shapes=[pltpu.SemaphoreType.DMA((6 * n,)), pltpu.SemaphoreType.DMA((6 * n,)),
                        pltpu.SemaphoreType.DMA((n,))],
    )(*shards)


def _rs_sibling_body(n_items, *refs):
    gs = refs[:n_items]
    owns = refs[n_items:2 * n_items]
    gots = refs[2 * n_items:3 * n_items]
    send_sems, recv_sems, local_sems = refs[3 * n_items:]
    x, y, c, _ = _me()
    local, sends = [], []
    for it in range(n_items):
        h = gs[it].shape[1] // 2
        cp = pltpu.make_async_copy(gs[it].at[:, pl.ds(c * h, h)], owns[it], local_sems.at[it])
        cp.start()
        local.append(cp)
        cp = pltpu.make_async_remote_copy(
            src_ref=gs[it].at[:, pl.ds((1 - c) * h, h)], dst_ref=gots[it],
            send_sem=send_sems.at[it], recv_sem=recv_sems.at[it], device_id=(x, y, 1 - c), device_id_type=MESH)
        cp.start()
        sends.append(cp)
    for cp in sends:
        cp.wait()
    for cp in local:
        cp.wait()


def _rs_sibling(grads):
    n = len(grads)
    halves = [jax.ShapeDtypeStruct((g.shape[0], g.shape[1] // 2, g.shape[2]), g.dtype) for g in grads]
    outs = pl.pallas_call(
        functools.partial(_rs_sibling_body, n),
        name="rs_sibling",
        in_specs=[HBM] * n,
        out_specs=[HBM] * (2 * n),
        out_shape=halves + halves,
        scratch_shapes=[pltpu.SemaphoreType.DMA((n,)), pltpu.SemaphoreType.DMA((n,)), pltpu.SemaphoreType.DMA((n,))],
    )(*grads)
    return outs[:n], outs[n:]


def _rs_chips_body(n_items, *refs):
    pss = refs[:n_items]
    bufs = refs[n_items:2 * n_items]
    send_sems, recv_sems, local_sems = refs[2 * n_items:]
    x, y, c, chips = _me()
    p = 2 * x + y
    local, sends = [], []
    for it in range(n_items):
        cp = pltpu.make_async_copy(pss[it].at[p], bufs[it].at[p], local_sems.at[it])
        cp.start()
        local.append(cp)
        for j, (cx, cy) in enumerate(chips):
            k = it * 3 + j
            cp = pltpu.make_async_remote_copy(
                src_ref=pss[it].at[2 * cx + cy], dst_ref=bufs[it].at[p],
                send_sem=send_sems.at[k], recv_sem=recv_sems.at[k], device_id=(cx, cy, c), device_id_type=MESH)
            cp.start()
            sends.append(cp)
    for it in range(n_items):
        for j, (cx, cy) in enumerate(chips):
            k = it * 3 + j
            pltpu.make_async_remote_copy(
                src_ref=pss[it].at[p], dst_ref=bufs[it].at[2 * cx + cy],
                send_sem=send_sems.at[k], recv_sem=recv_sems.at[k], device_id=(cx, cy, c),
                device_id_type=MESH).wait_recv()
    for cp in sends:
        cp.wait_send()
    for cp in local:
        cp.wait()


def _rs_chips(pair_sums):
    n = len(pair_sums)
    return pl.pallas_call(
        functools.partial(_rs_chips_body, n),
        name="rs_chips",
        in_specs=[HBM] * n,
        out_specs=[HBM] * n,
        out_shape=[jax.ShapeDtypeStruct(a.shape, a.dtype) for a in pair_sums],
        scratch_shapes=[pltpu.SemaphoreType.DMA((3 * n,)), pltpu.SemaphoreType.DMA((3 * n,)),
                        pltpu.SemaphoreType.DMA((n,))],
    )(*pair_sums)


def _rs_pair_body(n_items, *refs):
    rs = refs[:n_items]
    fulls = refs[n_items:2 * n_items]
    send_sems, recv_sems, local_sems = refs[2 * n_items:]
    x, y, c, _ = _me()
    local, sends = [], []
    for it in range(n_items):
        cp = pltpu.make_async_copy(rs[it], fulls[it].at[c], local_sems.at[it])
        cp.start()
        local.append(cp)
        cp = pltpu.make_async_remote_copy(
            src_ref=rs[it], dst_ref=fulls[it].at[c],
            send_sem=send_sems.at[it], recv_sem=recv_sems.at[it], device_id=(x, y, 1 - c), device_id_type=MESH)
        cp.start()
        sends.append(cp)
    for it in range(n_items):
        pltpu.make_async_remote_copy(
            src_ref=rs[it], dst_ref=fulls[it].at[1 - c],
            send_sem=send_sems.at[it], recv_sem=recv_sems.at[it], device_id=(x, y, 1 - c),
            device_id_type=MESH).wait_recv()
    for cp in sends:
        cp.wait_send()
    for cp in local:
        cp.wait()


def _rs_pair(halves):
    n = len(halves)
    return pl.pallas_call(
        functools.partial(_rs_pair_body, n),
        name="rs_pair",
        in_specs=[HBM] * n,
        out_specs=[HBM] * n,
        out_shape=[jax.ShapeDtypeStruct((2,) + a.shape, a.dtype) for a in halves],
        scratch_shapes=[pltpu.SemaphoreType.DMA((n,)), pltpu.SemaphoreType.DMA((n,)), pltpu.SemaphoreType.DMA((n,))],
    )(*halves)


def _ar_small_body(x_ref, o_ref, buf_ref, send_sems, recv_sems):
    x, y, c, _ = _me()
    me = 4 * x + 2 * y + c
    sends = []
    for r in range(1, N_DEV):
        fx, fy, fc = (r >> 2) & 1, (r >> 1) & 1, r & 1
        peer = (x ^ fx, y ^ fy, c ^ fc)
        cp = pltpu.make_async_remote_copy(
            src_ref=x_ref, dst_ref=buf_ref.at[me], send_sem=send_sems.at[r - 1], recv_sem=recv_sems.at[r - 1],
            device_id=peer, device_id_type=MESH)
        cp.start()
        sends.append(cp)
    buf_ref[me] = x_ref[...]
    for r in range(1, N_DEV):
        fx, fy, fc = (r >> 2) & 1, (r >> 1) & 1, r & 1
        src = 4 * (x ^ fx) + 2 * (y ^ fy) + (c ^ fc)
        pltpu.make_async_remote_copy(
            src_ref=x_ref, dst_ref=buf_ref.at[src], send_sem=send_sems.at[r - 1], recv_sem=recv_sems.at[r - 1],
            device_id=(x ^ fx, y ^ fy, c ^ fc), device_id_type=MESH).wait_recv()
    for cp in sends:
        cp.wait_send()
    acc = buf_ref[0]
    for k in range(1, N_DEV):
        acc = acc + buf_ref[k]
    o_ref[...] = acc


def _ar_small(packed):
    R, C = packed.shape
    return pl.pallas_call(
        functools.partial(_ar_small_body),
        name="ar_small",
        in_specs=[pl.BlockSpec(memory_space=pltpu.VMEM)],
        out_specs=pl.BlockSpec(memory_space=pltpu.VMEM),
        out_shape=jax.ShapeDtypeStruct((R, C), F32),
        scratch_shapes=[pltpu.VMEM((N_DEV, R, C), F32), pltpu.SemaphoreType.DMA((N_DEV - 1,)),
                        pltpu.SemaphoreType.DMA((N_DEV - 1,))],
        compiler_params=pltpu.CompilerParams(vmem_limit_bytes=VMEM_LIMIT),
    )(packed)


def _pack(arrays):
    rows = []
    for a in arrays:
        flat = a.reshape(-1).astype(F32)
        pad = (-flat.shape[0]) % BLK
        rows.append(jnp.pad(flat, (0, pad)).reshape(-1, BLK))
    packed = jnp.concatenate(rows, axis=0)
    pad = (-packed.shape[0]) % 8
    return jnp.pad(packed, ((0, pad), (0, 0)))


def _unpack(packed, shapes):
    out, r = [], 0
    for s in shapes:
        n = 1
        for k in s:
            n *= k
        nr = -(-n // BLK)
        out.append(packed[r:r + nr].reshape(-1)[:n].reshape(s))
        r += nr
    return out


def kernel(x, norm1_g, w_in, q_norm_g, k_norm_g, attn_sinks, gate_ln_g, gate_ln_b, w_spatial, b_spatial, out_norm_attn_g, out_norm_gate_g, w_out, norm2_g, w_ffn_gate, w_ffn_up, w_ffn_down, loss_target, m_norm1_g, m_w_in, m_q_norm_g, m_k_norm_g, m_attn_sinks, m_gate_ln_g, m_gate_ln_b, m_w_spatial, m_b_spatial, m_out_norm_attn_g, m_out_norm_gate_g, m_w_out, m_norm2_g, m_w_ffn_gate, m_w_ffn_up, m_w_ffn_down, v_norm1_g, v_w_in, v_q_norm_g, v_k_norm_g, v_attn_sinks, v_gate_ln_g, v_gate_ln_b, v_w_spatial, v_b_spatial, v_out_norm_attn_g, v_out_norm_gate_g, v_w_out, v_norm2_g, v_w_ffn_gate, v_w_ffn_up, v_w_ffn_down):
    bl, seq, D = x.shape
    T = bl * seq
    attn_w, gate_w = out_norm_attn_g.shape[1], out_norm_gate_g.shape[1]
    d = _Dims(seq, attn_w, gate_w)
    G = d.n_groups
    in_w = d.in_w
    dff = w_ffn_gate.shape[2] * N_CHIPS
    assert w_in.shape[2] * N_CHIPS == in_w and seq % BLK == 0 and attn_w % (2 * BLK) == 0

    big = [w_in[0], w_out[0], w_ffn_gate[0], w_ffn_up[0], w_ffn_down[0]]
    win_g, wout_g, wg_g, wu_g, wd_g = _ag_weights([w.astype(BF16) for w in big])
    win_full = jnp.transpose(win_g, (1, 0, 2)).reshape(D, in_w)
    wout_full = wout_g.reshape(attn_w + gate_w, D)
    wd_full = wd_g.reshape(dff, D)

    qg2 = jnp.tile(q_norm_g, (1, 2))
    kg2 = jnp.tile(k_norm_g, (1, 2))
    lg, lb, wsp = gate_ln_g[0], gate_ln_b[0], w_spatial[0]
    bcol = jnp.broadcast_to(b_spatial[0][:, :, None], (G, BLK, BLK))

    xf = x.reshape(T, D)
    tgt = loss_target.reshape(T, D)
    h1 = _rms_fwd(xf, norm1_g, "norm1_fwd")
    proj = _matmul(h1, win_full, "nn", F32, "proj_fwd", tn=1664, tk=1024)
    ya, yg, yn = _mixer_fwd(d, proj, attn_sinks, qg2, kg2, lg, lb, wsp, bcol, out_norm_attn_g, out_norm_gate_g)
    x1 = _matmul(yn, wout_full, "nn", F32, "out_fwd", add=xf)
    h2 = _rms_fwd(x1, norm2_g, "norm2_fwd")
    a = _matmul(h2, wg_g, "nn", F32, "ffn_gate_fwd", b_shard="c", tn=1408)
    b = _matmul(h2, wu_g, "nn", F32, "ffn_up_fwd", b_shard="c", tn=1408)
    f = _swiglu_fwd(a, b)
    x2 = _matmul(f, wd_full, "nn", F32, "ffn_down_fwd", add=x1)
    dx2, loss_local = _loss(x2, tgt)

    dx2b = dx2.astype(BF16)
    g_d = _matmul(f, dx2b, "tn", BF16, "ffn_down_dw", out_shard="r", tm=1408)
    df = _matmul(dx2b, wd_full, "nt", F32, "ffn_down_dx", tn=1408)
    da, db = _swiglu_bwd(df, a, b)
    g_g = _matmul(h2, da, "tn", BF16, "ffn_gate_dw", out_shard="c", tn=1408)
    g_u = _matmul(h2, db, "tn", BF16, "ffn_up_dw", out_shard="c", tn=1408)
    dh2 = _matmul(da, wg_g, "nt", F32, "ffn_gate_dx", b_shard="c", tk=1408)
    dh2 = _matmul(db, wu_g, "nt", F32, "ffn_up_dx", b_shard="c", tk=1408, add=dh2)
    dx1, dg_norm2 = _rms_bwd(x1, norm2_g, dh2, dx2, "norm2_bwd")
    dx1b = dx1.astype(BF16)
    g_o = _matmul(yn, dx1b, "tn", BF16, "out_dw", out_shard="r")
    dy = _matmul(dx1b, wout_full, "nt", F32, "out_dx")
    (dproj, dkv, dqg, dkg, dsk, dlg, dlb, dwsp, dbsp, dgoa, dgog) = _mixer_bwd(
        d, proj, ya, yg, dy, attn_sinks, qg2, kg2, lg, lb, wsp, bcol, out_norm_attn_g, out_norm_gate_g)
    dproj = _put_kv(d, dproj, dkv)
    g_in_full = _matmul(h1, dproj, "tn", BF16, "proj_dw", tn=1664)
    dh1 = _matmul(dproj, win_full, "nt", F32, "proj_dx", tk=1664)
    dx, dg_norm1 = _rms_bwd(xf, norm1_g, dh1, dx1, "norm1_bwd")
    g_i = jnp.transpose(g_in_full.reshape(D, N_CHIPS, in_w // N_CHIPS), (1, 0, 2))

    grads = [g_i, g_o, g_g, g_u, g_d]
    owns, gots = _rs_sibling(grads)
    pair = [_sum_slabs([o.reshape((1, -1, o.shape[2])), r.reshape((1, -1, r.shape[2]))], BF16, "rs_add_pair_%d" % k)
            .reshape(o.shape) for k, (o, r) in enumerate(zip(owns, gots))]
    bufs = _rs_chips(pair)
    halves = [_sum_slabs([bq], F32, "rs_add_chips_%d" % k) for k, bq in enumerate(bufs)]
    fulls = _rs_pair(halves)
    big_grads = [fl.reshape(w.shape) for fl, w in zip(fulls, big)]

    small_names_w = [norm1_g, q_norm_g, k_norm_g, attn_sinks, gate_ln_g, gate_ln_b, w_spatial, b_spatial,
                     out_norm_attn_g, out_norm_gate_g, norm2_g]
    small_m = [m_norm1_g, m_q_norm_g, m_k_norm_g, m_attn_sinks, m_gate_ln_g, m_gate_ln_b, m_w_spatial, m_b_spatial,
               m_out_norm_attn_g, m_out_norm_gate_g, m_norm2_g]
    small_v = [v_norm1_g, v_q_norm_g, v_k_norm_g, v_attn_sinks, v_gate_ln_g, v_gate_ln_b, v_w_spatial, v_b_spatial,
               v_out_norm_attn_g, v_out_norm_gate_g, v_norm2_g]
    dqg64 = dqg[:, :HEAD_DIM] + dqg[:, HEAD_DIM:]
    dkg64 = dkg[:, :HEAD_DIM] + dkg[:, HEAD_DIM:]
    small_g_local = [dg_norm1, dqg64, dkg64, dsk[:, :d.n_heads], dlg, dlb, dwsp, dbsp, dgoa, dgog, dg_norm2]
    shapes = [w.shape for w in small_names_w]
    sg = _ar_small(_pack(small_g_local))
    sd, snm, snv = _adamw(_pack(small_names_w), sg, _pack(small_m), _pack(small_v), "adamw_small")
    small_g, small_d, small_nm, small_nv = (_unpack(t, shapes) for t in (sg, sd, snm, snv))

    big_m = [m_w_in[0], m_w_out[0], m_w_ffn_gate[0], m_w_ffn_up[0], m_w_ffn_down[0]]
    big_v = [v_w_in[0], v_w_out[0], v_w_ffn_gate[0], v_w_ffn_up[0], v_w_ffn_down[0]]
    big_d, big_nm, big_nv = [], [], []
    for k, (w, g, m, v) in enumerate(zip(big, big_grads, big_m, big_v)):
        dd, nm, nv = _adamw(w, g, m, v, "adamw_big_%d" % k)
        big_d.append(dd)
        big_nm.append(nm)
        big_nv.append(nv)

    loss = lax.psum(loss_local[0, 0], ("x", "y", "c"))

    def order(small, bigs):
        s = list(small)
        bg = [t[None] for t in bigs]
        return [s[0], bg[0], s[1], s[2], s[3], s[4], s[5], s[6], s[7], s[8], s[9], bg[1], s[10], bg[2], bg[3], bg[4]]

    grad_x = dx.reshape(bl, seq, D)
    return (loss, grad_x, *order(small_g, big_grads), *order(small_d, big_d), *order(small_nm, big_nm),
            *order(small_nv, big_nv))
```

```python
import functools

import jax
import jax.numpy as jnp
from jax import lax
from jax.experimental import pallas as pl
from jax.experimental.pallas import tpu as pltpu

F32 = jnp.float32
BF16 = jnp.bfloat16
MESH = pl.DeviceIdType.MESH

EPS = 1e-6
HEAD_DIM = 64
N_KV_HEADS = 2
BLK = 128
N_CHIPS = 4
N_DEV = 8
NEG = -1e30

ADAM_LR = 0.001
ADAM_B1 = 0.9
ADAM_B2 = 0.999
ADAM_EPS = 1e-08
ADAM_WD = 0.01
ADAM_STEP = 10

VMEM_LIMIT = 56 * 1024 * 1024

NN = (((1,), (0,)), ((), ()))
NT = (((1,), (1,)), ((), ()))
TN = (((0,), (0,)), ((), ()))
HBM = pl.BlockSpec(memory_space=pltpu.HBM)


def _dot(a, b, dn):
    return lax.dot_general(a, b, dn, preferred_element_type=F32)


def _pick(dim, pref, align=128):
    if dim <= pref:
        return dim
    t = (pref // align) * align
    while t >= align:
        if dim % t == 0:
            return t
        t -= align
    return dim


def _params(sem):
    return pltpu.CompilerParams(dimension_semantics=sem, vmem_limit_bytes=VMEM_LIMIT)


def _mm_body(dn, nk, has_add, *refs):
    if has_add:
        a_ref, b_ref, add_ref, o_ref = refs[:4]
    else:
        a_ref, b_ref, o_ref = refs[:3]
        add_ref = None
    p = _dot(a_ref[...], b_ref[...], dn)

    def finish(r):
        if add_ref is not None:
            r = r + add_ref[...]
        o_ref[...] = r.astype(o_ref.dtype)

    if nk == 1:
        finish(p)
        return
    acc_ref = refs[-1]
    k = pl.program_id(2)

    @pl.when(k == 0)
    def _():
        acc_ref[...] = jnp.zeros_like(acc_ref)

    acc_ref[...] += p

    @pl.when(k == nk - 1)
    def _():
        finish(acc_ref[...])


def _matmul(a, b, mode, out_dtype, name, *, tm, tn, tk=None, add=None, b_slab=None, out_slab=None):
    if mode == "nn":
        M, K = a.shape
        N = b.shape[0] * b.shape[2] if b_slab == "c" else b.shape[1]
    elif mode == "nt":
        M, K = a.shape
        N = b.shape[1] if b_slab == "k" else b.shape[0]
    else:
        K, M = a.shape
        N = b.shape[1]
    tk = K if tk is None else tk
    tm, tn, tk = _pick(M, tm), _pick(N, tn), _pick(K, tk)
    if b_slab == "c":
        tn = _pick(b.shape[2], tn)
    if b_slab == "k":
        tk = _pick(b.shape[2], tk)
    if out_slab == "c":
        tn = _pick(N // N_CHIPS, tn)
    if out_slab == "r":
        tm = _pick(M // N_CHIPS, tm)
    gm, gn, gk = M // tm, N // tn, K // tk

    if mode == "tn":
        a_spec = pl.BlockSpec((tk, tm), lambda j, i, k: (k, i))
        b_spec = pl.BlockSpec((tk, tn), lambda j, i, k: (k, j))
    else:
        a_spec = pl.BlockSpec((tm, tk), lambda j, i, k: (i, k))
        if b_slab == "c":
            per = b.shape[2] // tn
            b_spec = pl.BlockSpec((None, tk, tn), lambda j, i, k: (j // per, k, j % per))
        elif b_slab == "k":
            per = b.shape[2] // tk
            b_spec = pl.BlockSpec((None, tn, tk), lambda j, i, k: (k // per, j, k % per))
        elif mode == "nn":
            b_spec = pl.BlockSpec((tk, tn), lambda j, i, k: (k, j))
        else:
            b_spec = pl.BlockSpec((tn, tk), lambda j, i, k: (j, k))

    if out_slab == "c":
        per = (N // N_CHIPS) // tn
        o_spec = pl.BlockSpec((None, tm, tn), lambda j, i, k: (j // per, i, j % per))
        o_shape = jax.ShapeDtypeStruct((N_CHIPS, M, N // N_CHIPS), out_dtype)
    elif out_slab == "r":
        per = (M // N_CHIPS) // tm
        o_spec = pl.BlockSpec((None, tm, tn), lambda j, i, k: (i // per, i % per, j))
        o_shape = jax.ShapeDtypeStruct((N_CHIPS, M // N_CHIPS, N), out_dtype)
    else:
        o_spec = pl.BlockSpec((tm, tn), lambda j, i, k: (i, j))
        o_shape = jax.ShapeDtypeStruct((M, N), out_dtype)

    dn = {"nn": NN, "nt": NT, "tn": TN}[mode]
    in_specs = [a_spec, b_spec]
    args = [a, b]
    if add is not None:
        in_specs.append(pl.BlockSpec((tm, tn), lambda j, i, k: (i, j)))
        args.append(add)
    return pl.pallas_call(
        functools.partial(_mm_body, dn, gk, add is not None),
        name=name,
        grid=(gn, gm, gk),
        in_specs=in_specs,
        out_specs=o_spec,
        out_shape=o_shape,
        scratch_shapes=[pltpu.VMEM((tm, tn), F32)] if gk > 1 else [],
        compiler_params=_params(("parallel", "parallel", "arbitrary")),
    )(*args)


def _rms_fwd_body(x_ref, g_ref, h_ref):
    x = x_ref[...]
    r = lax.rsqrt(jnp.mean(x * x, axis=-1, keepdims=True) + EPS)
    h_ref[...] = (x * r * g_ref[...]).astype(h_ref.dtype)


def _rms_fwd(x, g, name):
    T, D = x.shape
    tr = _pick(T, 256, 8)
    return pl.pallas_call(
        functools.partial(_rms_fwd_body),
        name=name,
        grid=(T // tr,),
        in_specs=[pl.BlockSpec((tr, D), lambda i: (i, 0)), pl.BlockSpec((1, D), lambda i: (0, 0))],
        out_specs=pl.BlockSpec((tr, D), lambda i: (i, 0)),
        out_shape=jax.ShapeDtypeStruct((T, D), BF16),
        compiler_params=_params(("parallel",)),
    )(x, g)


def _rms_bwd_body(with_bf16, x_ref, g_ref, dh_ref, res_ref, dx_ref, *rest):
    dg_ref = rest[-1]

    @pl.when(pl.program_id(0) == 0)
    def _():
        dg_ref[...] = jnp.zeros_like(dg_ref)

    x = x_ref[...]
    r = lax.rsqrt(jnp.mean(x * x, axis=-1, keepdims=True) + EPS)
    xh = x * r
    dh = dh_ref[...]
    dg_ref[...] += jnp.sum(dh * xh, axis=0, keepdims=True)
    t = dh * g_ref[...]
    dx = res_ref[...] + r * (t - xh * jnp.mean(t * xh, axis=-1, keepdims=True))
    dx_ref[...] = dx
    if with_bf16:
        rest[0][...] = dx.astype(BF16)


def _rms_bwd(x, g, dh, res, name, with_bf16):
    T, D = x.shape
    tr = _pick(T, 256, 16)
    row = pl.BlockSpec((tr, D), lambda i: (i, 0))
    vec = pl.BlockSpec((1, D), lambda i: (0, 0))
    extra = [jax.ShapeDtypeStruct((T, D), BF16)] if with_bf16 else []
    return pl.pallas_call(
        functools.partial(_rms_bwd_body, with_bf16),
        name=name,
        grid=(T // tr,),
        in_specs=[row, vec, row, row],
        out_specs=[row] + [row] * len(extra) + [vec],
        out_shape=[jax.ShapeDtypeStruct((T, D), F32)] + extra + [jax.ShapeDtypeStruct((1, D), F32)],
        compiler_params=_params(("arbitrary",)),
    )(x, g, dh, res)


def _ffn_up_body(h_ref, wg_ref, wu_ref, a_ref, b_ref, f_ref):
    h = h_ref[...]
    a = _dot(h, wg_ref[...], NN)
    b = _dot(h, wu_ref[...], NN)
    a_ref[...] = a
    b_ref[...] = b
    f_ref[...] = (a * (1.0 / (1.0 + jnp.exp(-a))) * b).astype(f_ref.dtype)


def _ffn_up(h, wg, wu):
    T, D = h.shape
    n, _, fs = wg.shape
    tm, tn = _pick(T, 512), fs
    hs = pl.BlockSpec((tm, D), lambda j, i: (i, 0))
    ws = pl.BlockSpec((None, D, tn), lambda j, i: (j, 0, 0))
    os = pl.BlockSpec((tm, tn), lambda j, i: (i, j))
    return pl.pallas_call(
        functools.partial(_ffn_up_body),
        name="ffn_up_fwd",
        grid=(n, T // tm),
        in_specs=[hs, ws, ws],
        out_specs=[os, os, os],
        out_shape=[jax.ShapeDtypeStruct((T, n * fs), F32), jax.ShapeDtypeStruct((T, n * fs), F32),
                   jax.ShapeDtypeStruct((T, n * fs), BF16)],
        compiler_params=_params(("parallel", "parallel")),
    )(h, wg, wu)


def _ffn_down_dx_body(dx_ref, wd_ref, a_ref, b_ref, da_ref, db_ref):
    df = _dot(dx_ref[...], wd_ref[...], NT)
    a = a_ref[...]
    s = 1.0 / (1.0 + jnp.exp(-a))
    da_ref[...] = (df * b_ref[...] * (s * (1.0 + a * (1.0 - s)))).astype(da_ref.dtype)
    db_ref[...] = (df * (a * s)).astype(db_ref.dtype)


def _ffn_down_dx(dx2b, wd, a, b):
    T, D = dx2b.shape
    F = wd.shape[0]
    tm, tn = _pick(T, 512), _pick(F, 1408)
    xs = pl.BlockSpec((tm, D), lambda j, i: (i, 0))
    ws = pl.BlockSpec((tn, D), lambda j, i: (j, 0))
    os = pl.BlockSpec((tm, tn), lambda j, i: (i, j))
    return pl.pallas_call(
        functools.partial(_ffn_down_dx_body),
        name="ffn_down_dx",
        grid=(F // tn, T // tm),
        in_specs=[xs, ws, os, os],
        out_specs=[os, os],
        out_shape=[jax.ShapeDtypeStruct((T, F), BF16), jax.ShapeDtypeStruct((T, F), BF16)],
        compiler_params=_params(("parallel", "parallel")),
    )(dx2b, wd, a, b)


def _loss_body(inv_d, x2_ref, tgt_ref, dx2_ref, dx2b_ref, loss_ref):
    @pl.when(pl.program_id(0) == 0)
    def _():
        loss_ref[...] = jnp.zeros_like(loss_ref)

    e = x2_ref[...] - tgt_ref[...]
    dx2 = e * inv_d
    dx2_ref[...] = dx2
    dx2b_ref[...] = dx2.astype(BF16)
    row = jnp.sum(e * e, axis=-1, keepdims=True) * inv_d
    loss_ref[...] += 0.5 * jnp.sum(row, axis=0, keepdims=True)


def _loss(x2, tgt):
    T, D = x2.shape
    tr = _pick(T, 256, 16)
    row = pl.BlockSpec((tr, D), lambda i: (i, 0))
    return pl.pallas_call(
        functools.partial(_loss_body, 1.0 / D),
        name="loss_head",
        grid=(T // tr,),
        in_specs=[row, row],
        out_specs=[row, row, pl.BlockSpec((1, 1), lambda i: (0, 0))],
        out_shape=[jax.ShapeDtypeStruct((T, D), F32), jax.ShapeDtypeStruct((T, D), BF16),
                   jax.ShapeDtypeStruct((1, 1), F32)],
        compiler_params=_params(("arbitrary",)),
    )(x2, tgt)


def _lo_mask(shape):
    return lax.broadcasted_iota(jnp.int32, shape, len(shape) - 1) < HEAD_DIM


def _half_sums(t, lo):
    s_lo = jnp.sum(jnp.where(lo, t, 0.0), axis=-1, keepdims=True)
    s_hi = jnp.sum(jnp.where(lo, 0.0, t), axis=-1, keepdims=True)
    return jnp.where(lo, s_lo, s_hi)


def _head_rstd(t, lo):
    return lax.rsqrt(_half_sums(t * t, lo) * (1.0 / HEAD_DIM) + EPS)


def _place(t, lo, kv_head):
    if kv_head == 0:
        t_lo = jnp.where(lo, t, 0.0)
        t_hi = pltpu.roll(t_lo, HEAD_DIM, 1)
    else:
        t_hi = jnp.where(lo, 0.0, t)
        t_lo = pltpu.roll(t_hi, HEAD_DIM, 1)
    return jnp.concatenate([t_lo, t_hi], axis=0).astype(BF16)


def _unplace(c0, c1, lo):
    return jnp.where(lo, c0 + pltpu.roll(c0, HEAD_DIM, 1), c1 + pltpu.roll(c1, HEAD_DIM, 1))


def _band(kv_cur, kv_prev, kg, lo2):
    kb = jnp.concatenate([kv_prev[:, :BLK], kv_cur[:, :BLK]], axis=0)
    vb = jnp.concatenate([kv_prev[:, BLK:], kv_cur[:, BLK:]], axis=0)
    rk = _head_rstd(kb, lo2)
    kn = kb * rk * kg
    kk = [_place(kn, lo2, h) for h in range(N_KV_HEADS)]
    vv = [_place(vb, lo2, h) for h in range(N_KV_HEADS)]
    return kb, rk, kk, vv


def _score_geometry(first_i32):
    qi = lax.broadcasted_iota(jnp.int32, (BLK, 4 * BLK), 0)
    col = lax.broadcasted_iota(jnp.int32, (BLK, 4 * BLK), 1)
    kj = col & (2 * BLK - 1)
    dist = qi + BLK - kj
    valid = (dist >= 0) & (dist < BLK) & (kj >= first_i32 * BLK)
    return col, dist.astype(F32), valid


def _pair_probs(qn, kk, col, distf, valid, slope0, slope1, sink0, sink1):
    s = _dot(qn.astype(BF16), kk, NT) * (HEAD_DIM ** -0.5)
    slope = jnp.where(col < 2 * BLK, slope0, slope1)
    logits = jnp.where(valid, s - slope * distf, NEG)
    probs, psink = [], []
    for hh, sk in ((0, sink0), (1, sink1)):
        l = logits[:, 2 * BLK * hh:2 * BLK * (hh + 1)]
        m = jnp.maximum(jnp.max(l, axis=-1, keepdims=True), sk)
        p = jnp.exp(l - m)
        es = jnp.exp(sk - m)
        inv = 1.0 / (jnp.sum(p, axis=-1, keepdims=True) + es)
        probs.append(p * inv)
        psink.append(es * inv)
    return probs, psink


def _gelu(z):
    return 0.5 * z * (1.0 + lax.erf(z * (0.5 ** 0.5)))


def _gelu_grad(z):
    return 0.5 * (1.0 + lax.erf(z * (0.5 ** 0.5))) + z * jnp.exp(-0.5 * z * z) * ((2.0 * jnp.pi) ** -0.5)


def _tril_w(w):
    r = lax.broadcasted_iota(jnp.int32, (BLK, BLK), 0)
    c = lax.broadcasted_iota(jnp.int32, (BLK, BLK), 1)
    return jnp.where(r >= c, w, 0.0), r >= c


def _gate_fwd_group(zu, zv, lg, lb, w, bcol):
    u = _gelu(zu)
    v = _gelu(zv)
    mu = jnp.mean(v, axis=-1, keepdims=True)
    vc = v - mu
    rs = lax.rsqrt(jnp.mean(vc * vc, axis=-1, keepdims=True) + EPS)
    vh = vc * rs
    vn = vh * lg + lb
    wt, tril = _tril_w(w)
    mixed = _dot(wt.astype(BF16), vn.astype(BF16), NN) + bcol
    return u, vh, rs, vn, wt, tril, mixed


class _Dims:
    def __init__(self, seq, attn_w, gate_w):
        self.seq, self.attn_w, self.gate_w = seq, attn_w, gate_w
        self.n_heads = attn_w // HEAD_DIM
        self.group = self.n_heads // N_KV_HEADS
        self.n_pairs = attn_w // BLK
        self.n_groups = gate_w // BLK
        self.kv_col = attn_w // (2 * BLK)
        self.u0 = attn_w + 2 * BLK
        self.v0 = self.u0 + gate_w
        self.in_w = self.v0 + gate_w
        self.slopes = [2.0 ** (-8.0 * (h + 1) / self.n_heads) for h in range(self.n_heads)]


def _mixer_fwd_body(d, sink_ref, proj_ref, kvp_ref, qg_ref, kg_ref, lg_ref, lb_ref, w_ref, b_ref, goa_ref, gog_ref,
                    ya_ref, yg_ref, y_ref):
    i = pl.program_id(0)
    first = (i % (d.seq // BLK) == 0).astype(jnp.int32)
    lo = _lo_mask((BLK, BLK))
    lo2 = _lo_mask((2 * BLK, BLK))
    kv_cur = proj_ref[:, d.attn_w:d.attn_w + 2 * BLK]
    _, _, kk, vv = _band(kv_cur, kvp_ref[...], kg_ref[...], lo2)
    col, distf, valid = _score_geometry(first)
    qg = qg_ref[...]
    for j in range(d.n_pairs):
        h0, h1 = 2 * j, 2 * j + 1
        kh = h0 // d.group
        q2 = proj_ref[:, BLK * j:BLK * (j + 1)]
        qn = q2 * _head_rstd(q2, lo) * qg
        probs, _ = _pair_probs(qn, kk[kh], col, distf, valid, d.slopes[h0], d.slopes[h1],
                               sink_ref[0, h0], sink_ref[0, h1])
        p = jnp.concatenate(probs, axis=1).astype(BF16)
        ya_ref[:, BLK * j:BLK * (j + 1)] = _dot(p, vv[kh], NN)
    for g in range(d.n_groups):
        zu = proj_ref[:, d.u0 + BLK * g:d.u0 + BLK * (g + 1)]
        zv = proj_ref[:, d.v0 + BLK * g:d.v0 + BLK * (g + 1)]
        u, _, _, _, _, _, mixed = _gate_fwd_group(zu, zv, lg_ref[g:g + 1, :], lb_ref[g:g + 1, :], w_ref[g], b_ref[g])
        yg_ref[:, BLK * g:BLK * (g + 1)] = u * mixed
    ya = ya_ref[...]
    ra = lax.rsqrt(jnp.mean(ya * ya, axis=-1, keepdims=True) + EPS)
    y_ref[:, :d.attn_w] = (ya * ra * goa_ref[...]).astype(y_ref.dtype)
    yg = yg_ref[...]
    rg = lax.rsqrt(jnp.mean(yg * yg, axis=-1, keepdims=True) + EPS)
    y_ref[:, d.attn_w:] = (yg * rg * gog_ref[...]).astype(y_ref.dtype)


def _mixer_specs(d, T):
    row = lambda w: pl.BlockSpec((BLK, w), lambda i: (i, 0))
    const2 = lambda a: pl.BlockSpec(a.shape, lambda i: (0, 0))
    const3 = lambda a: pl.BlockSpec(a.shape, lambda i: (0, 0, 0))
    kv_prev = pl.BlockSpec((BLK, 2 * BLK), lambda i: (jnp.maximum(i - 1, 0), d.kv_col))
    return row, const2, const3, kv_prev


def _mixer_fwd(d, proj, sinks, qg2, kg2, lg, lb, wsp, bcol, goa, gog):
    T = proj.shape[0]
    row, const2, const3, kv_prev = _mixer_specs(d, T)
    return pl.pallas_call(
        functools.partial(_mixer_fwd_body, d),
        name="mixer_fwd",
        grid=(T // BLK,),
        in_specs=[pl.BlockSpec(memory_space=pltpu.SMEM), row(d.in_w), kv_prev, const2(qg2), const2(kg2),
                  const2(lg), const2(lb), const3(wsp), const3(bcol), const2(goa), const2(gog)],
        out_specs=[row(d.attn_w), row(d.gate_w), row(d.attn_w + d.gate_w)],
        out_shape=[jax.ShapeDtypeStruct((T, d.attn_w), F32), jax.ShapeDtypeStruct((T, d.gate_w), F32),
                   jax.ShapeDtypeStruct((T, d.attn_w + d.gate_w), BF16)],
        compiler_params=_params(("parallel",)),
    )(sinks, proj, proj, qg2, kg2, lg, lb, wsp, bcol, goa, gog)


def _mixer_bwd_body(d, sink_ref, proj_ref, kvp_ref, ya_ref, yg_ref, dy_ref, qg_ref, kg_ref, lg_ref, lb_ref, w_ref,
                    b_ref, goa_ref, gog_ref,
                    dproj_ref, dkv_ref, dqg_ref, dkg_ref, dsk_ref, dlg_ref, dlb_ref, dw_ref, db_ref, dgoa_ref,
                    dgog_ref):
    i = pl.program_id(0)

    @pl.when(i == 0)
    def _():
        for r in (dqg_ref, dkg_ref, dsk_ref, dlg_ref, dlb_ref, dw_ref, db_ref, dgoa_ref, dgog_ref):
            r[...] = jnp.zeros_like(r)

    first = (i % (d.seq // BLK) == 0).astype(jnp.int32)
    lo = _lo_mask((BLK, BLK))
    lo2 = _lo_mask((2 * BLK, BLK))
    lane_row = lax.broadcasted_iota(jnp.int32, (1, BLK), 1)

    ya = ya_ref[...]
    ra = lax.rsqrt(jnp.mean(ya * ya, axis=-1, keepdims=True) + EPS)
    yah = ya * ra
    dyn = dy_ref[:, :d.attn_w]
    dgoa_ref[...] += jnp.sum(dyn * yah, axis=0, keepdims=True)
    t = dyn * goa_ref[...]
    dya = ra * (t - yah * jnp.mean(t * yah, axis=-1, keepdims=True))
    yg = yg_ref[...]
    rg = lax.rsqrt(jnp.mean(yg * yg, axis=-1, keepdims=True) + EPS)
    ygh = yg * rg
    dyn = dy_ref[:, d.attn_w:]
    dgog_ref[...] += jnp.sum(dyn * ygh, axis=0, keepdims=True)
    t = dyn * gog_ref[...]
    dyg = rg * (t - ygh * jnp.mean(t * ygh, axis=-1, keepdims=True))

    kv_cur = proj_ref[:, d.attn_w:d.attn_w + 2 * BLK]
    kg = kg_ref[...]
    kb, rk, kk, vv = _band(kv_cur, kvp_ref[...], kg, lo2)
    col, distf, valid = _score_geometry(first)
    qg = qg_ref[...]
    ck = [jnp.zeros((2 * BLK, BLK), F32) for _ in range(N_KV_HEADS)]
    cv = [jnp.zeros((2 * BLK, BLK), F32) for _ in range(N_KV_HEADS)]
    dsk = jnp.zeros((1, BLK), F32)
    dqg = jnp.zeros((1, BLK), F32)
    for j in range(d.n_pairs):
        h0, h1 = 2 * j, 2 * j + 1
        kh = h0 // d.group
        cols = slice(BLK * j, BLK * (j + 1))
        q2 = proj_ref[:, cols]
        rq = _head_rstd(q2, lo)
        qh = q2 * rq
        qn = qh * qg
        probs, psink = _pair_probs(qn, kk[kh], col, distf, valid, d.slopes[h0], d.slopes[h1],
                                   sink_ref[0, h0], sink_ref[0, h1])
        do2 = dya[:, cols]
        prod = do2 * ya[:, cols]
        delta = (jnp.sum(jnp.where(lo, prod, 0.0), axis=-1, keepdims=True),
                 jnp.sum(jnp.where(lo, 0.0, prod), axis=-1, keepdims=True))
        do2b = do2.astype(BF16)
        dp = _dot(do2b, vv[kh], NT)
        ds = []
        for hh in (0, 1):
            ds.append(probs[hh] * (dp[:, 2 * BLK * hh:2 * BLK * (hh + 1)] - delta[hh]))
            dsink = -jnp.sum(psink[hh] * delta[hh], axis=0, keepdims=True)
            dsk = dsk + jnp.where(lane_row == (h0 + hh), dsink, 0.0)
        dsb = (jnp.concatenate(ds, axis=1) * (HEAD_DIM ** -0.5)).astype(BF16)
        pb = jnp.concatenate(probs, axis=1).astype(BF16)
        qnb = qn.astype(BF16)
        dqn = _dot(dsb, kk[kh], NN)
        dkk = _dot(dsb, qnb, TN)
        dvv = _dot(pb, do2b, TN)
        ck[kh] = ck[kh] + jnp.where(lo2, dkk[:2 * BLK], 0.0) + jnp.where(lo2, 0.0, dkk[2 * BLK:])
        cv[kh] = cv[kh] + jnp.where(lo2, dvv[:2 * BLK], 0.0) + jnp.where(lo2, 0.0, dvv[2 * BLK:])
        dqg = dqg + jnp.sum(dqn * qh, axis=0, keepdims=True)
        t = dqn * qg
        dq2 = rq * (t - qh * (_half_sums(t * qh, lo) * (1.0 / HEAD_DIM)))
        dproj_ref[:, cols] = dq2.astype(dproj_ref.dtype)
    dsk_ref[...] += dsk
    dqg_ref[...] += dqg
    dkn = _unplace(ck[0], ck[1], lo2)
    dvb = _unplace(cv[0], cv[1], lo2)
    khat = kb * rk
    dkg_ref[...] += jnp.sum(dkn * khat, axis=0, keepdims=True)
    t = dkn * kg
    dkb = rk * (t - khat * (_half_sums(t * khat, lo2) * (1.0 / HEAD_DIM)))
    rows_cur = pl.ds(pl.multiple_of(i * BLK, BLK), BLK)
    rows_prev = pl.ds(pl.multiple_of(jnp.maximum(i - 1, 0) * BLK, BLK), BLK)
    dkv_ref[rows_cur, :] = jnp.concatenate([dkb[BLK:], dvb[BLK:]], axis=1)
    dkv_ref[rows_prev, :] += jnp.concatenate([dkb[:BLK], dvb[:BLK]], axis=1)
    dproj_ref[:, d.attn_w:d.attn_w + 2 * BLK] = jnp.zeros((BLK, 2 * BLK), dproj_ref.dtype)

    for g in range(d.n_groups):
        ucols = slice(d.u0 + BLK * g, d.u0 + BLK * (g + 1))
        vcols = slice(d.v0 + BLK * g, d.v0 + BLK * (g + 1))
        zu = proj_ref[:, ucols]
        zv = proj_ref[:, vcols]
        lg = lg_ref[g:g + 1, :]
        u, vh, rs, vn, wt, tril, mixed = _gate_fwd_group(zu, zv, lg, lb_ref[g:g + 1, :], w_ref[g], b_ref[g])
        dyg_g = dyg[:, BLK * g:BLK * (g + 1)]
        du = dyg_g * mixed
        dmix = dyg_g * u
        dmb = dmix.astype(BF16)
        db_ref[g:g + 1, :] += jnp.sum(jnp.transpose(dmix), axis=0, keepdims=True)
        dw_ref[g] += jnp.where(tril, _dot(dmb, vn.astype(BF16), NT), 0.0)
        dvn = _dot(wt.astype(BF16), dmb, TN)
        dlg_ref[g:g + 1, :] += jnp.sum(dvn * vh, axis=0, keepdims=True)
        dlb_ref[g:g + 1, :] += jnp.sum(dvn, axis=0, keepdims=True)
        dvh = dvn * lg
        dv = rs * (dvh - jnp.mean(dvh, axis=-1, keepdims=True) - vh * jnp.mean(dvh * vh, axis=-1, keepdims=True))
        dproj_ref[:, ucols] = (du * _gelu_grad(zu)).astype(dproj_ref.dtype)
        dproj_ref[:, vcols] = (dv * _gelu_grad(zv)).astype(dproj_ref.dtype)


def _mixer_bwd(d, proj, ya, yg, dy, sinks, qg2, kg2, lg, lb, wsp, bcol, goa, gog):
    T = proj.shape[0]
    row, const2, const3, kv_prev = _mixer_specs(d, T)
    acc2 = lambda s: pl.BlockSpec(s, lambda i: (0, 0))
    G = d.n_groups
    out_shapes = [((T, d.in_w), BF16), ((T, 2 * BLK), F32), ((1, BLK), F32), ((1, BLK), F32), ((1, BLK), F32),
                  ((G, BLK), F32), ((G, BLK), F32), ((G, BLK, BLK), F32), ((G, BLK), F32),
                  ((1, d.attn_w), F32), ((1, d.gate_w), F32)]
    out_specs = [row(d.in_w)] + [acc2(s) for s, _ in out_shapes[1:7]] + \
                [pl.BlockSpec((G, BLK, BLK), lambda i: (0, 0, 0))] + [acc2(s) for s, _ in out_shapes[8:]]
    return pl.pallas_call(
        functools.partial(_mixer_bwd_body, d),
        name="mixer_bwd",
        grid=(T // BLK,),
        in_specs=[pl.BlockSpec(memory_space=pltpu.SMEM), row(d.in_w), kv_prev, row(d.attn_w), row(d.gate_w),
                  row(d.attn_w + d.gate_w), const2(qg2), const2(kg2), const2(lg), const2(lb), const3(wsp),
                  const3(bcol), const2(goa), const2(gog)],
        out_specs=out_specs,
        out_shape=[jax.ShapeDtypeStruct(s, t) for s, t in out_shapes],
        compiler_params=_params(("arbitrary",)),
    )(sinks, proj, proj, ya, yg, dy, qg2, kg2, lg, lb, wsp, bcol, goa, gog)


def _put_kv_body(dkv_ref, dproj_in_ref, dproj_ref):
    del dproj_in_ref
    dproj_ref[...] = dkv_ref[...].astype(dproj_ref.dtype)


def _put_kv(d, dproj, dkv):
    T = dproj.shape[0]
    tr = _pick(T, 1024, 16)
    return pl.pallas_call(
        functools.partial(_put_kv_body),
        name="put_kv",
        grid=(T // tr,),
        in_specs=[pl.BlockSpec((tr, 2 * BLK), lambda i: (i, 0)), pl.BlockSpec(memory_space=pl.ANY)],
        out_specs=pl.BlockSpec((tr, 2 * BLK), lambda i: (i, d.kv_col)),
        out_shape=jax.ShapeDtypeStruct(dproj.shape, dproj.dtype),
        input_output_aliases={1: 0},
        compiler_params=_params(("parallel",)),
    )(dkv, dproj)


def _sum_body(n, out_dtype, *refs):
    o_ref = refs[n]
    r = refs[0][...].astype(F32)
    for k in range(1, n):
        r = r + refs[k][...].astype(F32)
    o_ref[...] = r.astype(out_dtype)


def _sum_slabs(parts, out_dtype, name):
    _, R, C = parts[0].shape
    tr = _pick(R, 512, 16)
    specs, args = [], []
    for a in parts:
        for q in range(a.shape[0]):
            specs.append(pl.BlockSpec((None, tr, C), lambda i, q=q: (q, i, 0)))
            args.append(a)
    return pl.pallas_call(
        functools.partial(_sum_body, len(args), out_dtype),
        name=name,
        grid=(R // tr,),
        in_specs=specs,
        out_specs=pl.BlockSpec((tr, C), lambda i: (i, 0)),
        out_shape=jax.ShapeDtypeStruct((R, C), out_dtype),
        compiler_params=_params(("parallel",)),
    )(*args)


def _add_pair_body(pc_ref, own_ref, got_ref, o_ref):
    del pc_ref
    o_ref[...] = (own_ref[...].astype(F32) + got_ref[...].astype(F32)).astype(o_ref.dtype)


def _add_pair(g4, got, pc, name):
    n, _, h, C = g4.shape
    tr = _pick(h, 512, 16)
    return pl.pallas_call(
        functools.partial(_add_pair_body),
        name=name,
        grid_spec=pltpu.PrefetchScalarGridSpec(
            num_scalar_prefetch=1,
            grid=(n, h // tr),
            in_specs=[pl.BlockSpec((None, None, tr, C), lambda q, i, pc: (q, pc[1], i, 0)),
                      pl.BlockSpec((None, tr, C), lambda q, i, pc: (q, i, 0))],
            out_specs=pl.BlockSpec((None, tr, C), lambda q, i, pc: (q, i, 0)),
        ),
        out_shape=jax.ShapeDtypeStruct((n, h, C), g4.dtype),
        compiler_params=_params(("parallel", "parallel")),
    )(pc, g4, got)


def _adamw_update(w, g, m, v):
    m = ADAM_B1 * m + (1.0 - ADAM_B1) * g
    v = ADAM_B2 * v + (1.0 - ADAM_B2) * (g * g)
    m_hat = m / (1.0 - ADAM_B1 ** ADAM_STEP)
    v_hat = v / (1.0 - ADAM_B2 ** ADAM_STEP)
    return -ADAM_LR * (m_hat / (jnp.sqrt(v_hat) + ADAM_EPS) + ADAM_WD * w), m, v


def _adamw_body(w_ref, g_ref, m_ref, v_ref, d_ref, nm_ref, nv_ref):
    d_ref[...], nm_ref[...], nv_ref[...] = _adamw_update(w_ref[...], g_ref[...], m_ref[...], v_ref[...])


def _adamw(w, g, m, v, name):
    R, C = w.shape
    tr = _pick(R, 512, 8)
    blk = pl.BlockSpec((tr, C), lambda i: (i, 0))
    return pl.pallas_call(
        functools.partial(_adamw_body),
        name=name,
        grid=(R // tr,),
        in_specs=[blk] * 4,
        out_specs=[blk] * 3,
        out_shape=[jax.ShapeDtypeStruct((R, C), F32)] * 3,
        compiler_params=_params(("parallel",)),
    )(w, g, m, v)


def _adamw_halves_body(pc_ref, w_ref, own_ref, got_ref, m_ref, v_ref, g_ref, d_ref, nm_ref, nv_ref):
    mine = pl.program_id(0) == pc_ref[1]

    def update(g):
        g_ref[...] = g
        d_ref[...], nm_ref[...], nv_ref[...] = _adamw_update(w_ref[...], g, m_ref[...], v_ref[...])

    @pl.when(mine)
    def _():
        update(own_ref[...])

    @pl.when(jnp.logical_not(mine))
    def _():
        update(got_ref[...])


def _adamw_halves(w, own, got, m, v, pc, name):
    h, C = own.shape
    tr = _pick(h, 512, 8)
    full = pl.BlockSpec((None, tr, C), lambda hh, i, pc: (hh, i, 0))
    half = pl.BlockSpec((tr, C), lambda hh, i, pc: (i, 0))
    return pl.pallas_call(
        functools.partial(_adamw_halves_body),
        name=name,
        grid_spec=pltpu.PrefetchScalarGridSpec(
            num_scalar_prefetch=1,
            grid=(2, h // tr),
            in_specs=[full, half, half, full, full],
            out_specs=[full] * 4,
        ),
        out_shape=[jax.ShapeDtypeStruct((2, h, C), F32)] * 4,
        compiler_params=_params(("parallel", "parallel")),
    )(pc, w.reshape(2, h, C), own, got, m.reshape(2, h, C), v.reshape(2, h, C))


def _me():
    x, y, c = lax.axis_index("x"), lax.axis_index("y"), lax.axis_index("c")
    chips = [(1 - x, y), (x, 1 - y), (1 - x, 1 - y)]
    return x, y, c, chips


def _cast_into_body(pc_ref, w_ref, o_ref):
    del pc_ref
    o_ref[...] = w_ref[...].astype(o_ref.dtype)


def _cast_into(w, pc, name):
    Rs, C = w.shape
    h = Rs // 2
    tr = _pick(h, 512, 16)
    return pl.pallas_call(
        functools.partial(_cast_into_body),
        name=name,
        grid_spec=pltpu.PrefetchScalarGridSpec(
            num_scalar_prefetch=1,
            grid=(2, h // tr),
            in_specs=[pl.BlockSpec((None, tr, C), lambda hh, i, pc: (hh, i, 0))],
            out_specs=pl.BlockSpec((None, None, tr, C), lambda hh, i, pc: (pc[0], hh, i, 0)),
        ),
        out_shape=jax.ShapeDtypeStruct((N_CHIPS, 2, h, C), BF16),
        compiler_params=_params(("parallel", "parallel")),
    )(pc, w.reshape(2, h, C))


def _ag_chips_body(n_items, *refs):
    bufs = refs[n_items:2 * n_items]
    send_sems, recv_sems = refs[2 * n_items:]
    x, y, c, chips = _me()
    p = 2 * x + y
    sends = []
    for it in range(n_items):
        for j, (cx, cy) in enumerate(chips):
            k = it * 3 + j
            cp = pltpu.make_async_remote_copy(
                src_ref=bufs[it].at[p, c], dst_ref=bufs[it].at[p, c],
                send_sem=send_sems.at[k], recv_sem=recv_sems.at[k], device_id=(cx, cy, c), device_id_type=MESH)
            cp.start()
            sends.append(cp)
    for it in range(n_items):
        for j, (cx, cy) in enumerate(chips):
            k = it * 3 + j
            theirs = bufs[it].at[2 * cx + cy, c]
            pltpu.make_async_remote_copy(
                src_ref=theirs, dst_ref=theirs, send_sem=send_sems.at[k], recv_sem=recv_sems.at[k],
                device_id=(cx, cy, c), device_id_type=MESH).wait_recv()
    for cp in sends:
        cp.wait_send()


def _ag_chips(bufs):
    n = len(bufs)
    return pl.pallas_call(
        functools.partial(_ag_chips_body, n),
        name="ag_chips",
        in_specs=[HBM] * n,
        out_specs=[HBM] * n,
        out_shape=[jax.ShapeDtypeStruct(b.shape, b.dtype) for b in bufs],
        input_output_aliases={k: k for k in range(n)},
        scratch_shapes=[pltpu.SemaphoreType.DMA((3 * n,)), pltpu.SemaphoreType.DMA((3 * n,))],
    )(*bufs)


def _send_tile_to_sibling(tile_ref, dst_rows, dst_total, send_sem, recv_sem, last):
    x, y, c, _ = _me()
    cp = pltpu.make_async_remote_copy(src_ref=tile_ref, dst_ref=dst_rows, send_sem=send_sem, recv_sem=recv_sem,
                                      device_id=(x, y, 1 - c), device_id_type=MESH)
    cp.start()
    cp.wait_send()

    @pl.when(last)
    def _():
        pltpu.make_async_remote_copy(src_ref=dst_total, dst_ref=dst_total, send_sem=send_sem, recv_sem=recv_sem,
                                     device_id=(x, y, 1 - c), device_id_type=MESH).wait_recv()


def _ag_pair_body(tr, n_i, pc_ref, tile_ref, buf_ref, send_sem, recv_sem):
    j, i = pl.program_id(0), pl.program_id(1)
    q = pc_ref[0] ^ (j + 1)
    c = pc_ref[1]
    rows = pl.ds(pl.multiple_of(i * tr, tr), tr)
    _send_tile_to_sibling(tile_ref, buf_ref.at[pl.ds(q, 1), pl.ds(c, 1), rows], buf_ref.at[pl.ds(0, N_CHIPS - 1), 0],
                          send_sem, recv_sem, jnp.logical_and(j == N_CHIPS - 2, i == n_i - 1))


def _ag_pair(buf, pc, name):
    _, _, h, C = buf.shape
    tr = _pick(h, 512, 16)
    return pl.pallas_call(
        functools.partial(_ag_pair_body, tr, h // tr),
        name=name,
        grid_spec=pltpu.PrefetchScalarGridSpec(
            num_scalar_prefetch=1,
            grid=(N_CHIPS - 1, h // tr),
            in_specs=[pl.BlockSpec((1, 1, tr, C), lambda j, i, pc: (pc[0] ^ (j + 1), pc[1], i, 0))],
            out_specs=HBM,
            scratch_shapes=[pltpu.SemaphoreType.DMA(()), pltpu.SemaphoreType.DMA(())],
        ),
        out_shape=jax.ShapeDtypeStruct(buf.shape, buf.dtype),
        input_output_aliases={1: 0},
        compiler_params=_params(("arbitrary", "arbitrary")),
    )(pc, buf)


def _swap_halves_body(tr, n_q, n_i, pc_ref, tile_ref, got_ref, send_sem, recv_sem):
    del pc_ref
    q, i = pl.program_id(0), pl.program_id(1)
    rows = pl.ds(pl.multiple_of(i * tr, tr), tr)
    _send_tile_to_sibling(tile_ref, got_ref.at[pl.ds(q, 1), :, rows], got_ref, send_sem, recv_sem,
                          jnp.logical_and(q == n_q - 1, i == n_i - 1))


def _swap_halves(g4, pc, name):
    n, _, h, C = g4.shape
    tr = _pick(h, 512, 16)
    return pl.pallas_call(
        functools.partial(_swap_halves_body, tr, n, h // tr),
        name=name,
        grid_spec=pltpu.PrefetchScalarGridSpec(
            num_scalar_prefetch=1,
            grid=(n, h // tr),
            in_specs=[pl.BlockSpec((1, 1, tr, C), lambda q, i, pc: (q, 1 - pc[1], i, 0))],
            out_specs=HBM,
            scratch_shapes=[pltpu.SemaphoreType.DMA(()), pltpu.SemaphoreType.DMA(())],
        ),
        out_shape=jax.ShapeDtypeStruct((n, 1, h, C), g4.dtype),
        compiler_params=_params(("arbitrary", "arbitrary")),
    )(pc, g4).reshape(n, h, C)


def _send_rows_body(tr, n_i, tile_ref, got_ref, send_sem, recv_sem):
    i = pl.program_id(0)
    rows = pl.ds(pl.multiple_of(i * tr, tr), tr)
    _send_tile_to_sibling(tile_ref, got_ref.at[rows], got_ref, send_sem, recv_sem, i == n_i - 1)


def _send_rows(r, name):
    h, C = r.shape
    tr = _pick(h, 256, 8)
    return pl.pallas_call(
        functools.partial(_send_rows_body, tr, h // tr),
        name=name,
        grid=(h // tr,),
        in_specs=[pl.BlockSpec((tr, C), lambda i: (i, 0))],
        out_specs=HBM,
        out_shape=jax.ShapeDtypeStruct((h, C), r.dtype),
        scratch_shapes=[pltpu.SemaphoreType.DMA(()), pltpu.SemaphoreType.DMA(())],
        compiler_params=_params(("arbitrary",)),
    )(r)


def _rs_chips_body(n_items, *refs):
    pss = refs[:n_items]
    bufs = refs[n_items:2 * n_items]
    send_sems, recv_sems, local_sems = refs[2 * n_items:]
    x, y, c, chips = _me()
    p = 2 * x + y
    local, sends = [], []
    for it in range(n_items):
        cp = pltpu.make_async_copy(pss[it].at[p], bufs[it].at[p], local_sems.at[it])
        cp.start()
        local.append(cp)
        for j, (cx, cy) in enumerate(chips):
            k = it * 3 + j
            cp = pltpu.make_async_remote_copy(
                src_ref=pss[it].at[2 * cx + cy], dst_ref=bufs[it].at[p],
                send_sem=send_sems.at[k], recv_sem=recv_sems.at[k], device_id=(cx, cy, c), device_id_type=MESH)
            cp.start()
            sends.append(cp)
    for it in range(n_items):
        for j, (cx, cy) in enumerate(chips):
            k = it * 3 + j
            pltpu.make_async_remote_copy(
                src_ref=pss[it].at[p], dst_ref=bufs[it].at[2 * cx + cy],
                send_sem=send_sems.at[k], recv_sem=recv_sems.at[k], device_id=(cx, cy, c),
                device_id_type=MESH).wait_recv()
    for cp in sends:
        cp.wait_send()
    for cp in local:
        cp.wait()


def _rs_chips(pair_sums):
    n = len(pair_sums)
    return pl.pallas_call(
        functools.partial(_rs_chips_body, n),
        name="rs_chips",
        in_specs=[HBM] * n,
        out_specs=[HBM] * n,
        out_shape=[jax.ShapeDtypeStruct(a.shape, a.dtype) for a in pair_sums],
        scratch_shapes=[pltpu.SemaphoreType.DMA((3 * n,)), pltpu.SemaphoreType.DMA((3 * n,)),
                        pltpu.SemaphoreType.DMA((n,))],
    )(*pair_sums)


def _ar_small_body(x_ref, o_ref, buf_ref, send_sems, recv_sems):
    x, y, c, _ = _me()
    me = 4 * x + 2 * y + c
    sends = []
    for r in range(1, N_DEV):
        fx, fy, fc = (r >> 2) & 1, (r >> 1) & 1, r & 1
        peer = (x ^ fx, y ^ fy, c ^ fc)
        cp = pltpu.make_async_remote_copy(
            src_ref=x_ref, dst_ref=buf_ref.at[me], send_sem=send_sems.at[r - 1], recv_sem=recv_sems.at[r - 1],
            device_id=peer, device_id_type=MESH)
        cp.start()
        sends.append(cp)
    buf_ref[me] = x_ref[...]
    for r in range(1, N_DEV):
        fx, fy, fc = (r >> 2) & 1, (r >> 1) & 1, r & 1
        src = 4 * (x ^ fx) + 2 * (y ^ fy) + (c ^ fc)
        pltpu.make_async_remote_copy(
            src_ref=x_ref, dst_ref=buf_ref.at[src], send_sem=send_sems.at[r - 1], recv_sem=recv_sems.at[r - 1],
            device_id=(x ^ fx, y ^ fy, c ^ fc), device_id_type=MESH).wait_recv()
    for cp in sends:
        cp.wait_send()
    acc = buf_ref[0]
    for k in range(1, N_DEV):
        acc = acc + buf_ref[k]
    o_ref[...] = acc


def _ar_small(packed):
    R, C = packed.shape
    return pl.pallas_call(
        functools.partial(_ar_small_body),
        name="ar_small",
        in_specs=[pl.BlockSpec(memory_space=pltpu.VMEM)],
        out_specs=pl.BlockSpec(memory_space=pltpu.VMEM),
        out_shape=jax.ShapeDtypeStruct((R, C), F32),
        scratch_shapes=[pltpu.VMEM((N_DEV, R, C), F32), pltpu.SemaphoreType.DMA((N_DEV - 1,)),
                        pltpu.SemaphoreType.DMA((N_DEV - 1,))],
        compiler_params=pltpu.CompilerParams(vmem_limit_bytes=VMEM_LIMIT),
    )(packed)


def _pack(arrays):
    rows = []
    for a in arrays:
        flat = a.reshape(-1).astype(F32)
        pad = (-flat.shape[0]) % BLK
        rows.append(jnp.pad(flat, (0, pad)).reshape(-1, BLK))
    packed = jnp.concatenate(rows, axis=0)
    pad = (-packed.shape[0]) % 8
    return jnp.pad(packed, ((0, pad), (0, 0)))


def _unpack(packed, shapes):
    out, r = [], 0
    for s in shapes:
        n = 1
        for k in s:
            n *= k
        nr = -(-n // BLK)
        out.append(packed[r:r + nr].reshape(-1)[:n].reshape(s))
        r += nr
    return out


def kernel(x, norm1_g, w_in, q_norm_g, k_norm_g, attn_sinks, gate_ln_g, gate_ln_b, w_spatial, b_spatial, out_norm_attn_g, out_norm_gate_g, w_out, norm2_g, w_ffn_gate, w_ffn_up, w_ffn_down, loss_target, m_norm1_g, m_w_in, m_q_norm_g, m_k_norm_g, m_attn_sinks, m_gate_ln_g, m_gate_ln_b, m_w_spatial, m_b_spatial, m_out_norm_attn_g, m_out_norm_gate_g, m_w_out, m_norm2_g, m_w_ffn_gate, m_w_ffn_up, m_w_ffn_down, v_norm1_g, v_w_in, v_q_norm_g, v_k_norm_g, v_attn_sinks, v_gate_ln_g, v_gate_ln_b, v_w_spatial, v_b_spatial, v_out_norm_attn_g, v_out_norm_gate_g, v_w_out, v_norm2_g, v_w_ffn_gate, v_w_ffn_up, v_w_ffn_down):
    bl, seq, D = x.shape
    T = bl * seq
    attn_w, gate_w = out_norm_attn_g.shape[1], out_norm_gate_g.shape[1]
    d = _Dims(seq, attn_w, gate_w)
    G = d.n_groups
    in_w = d.in_w
    dff = w_ffn_gate.shape[2] * N_CHIPS
    assert w_in.shape[2] * N_CHIPS == in_w and seq % BLK == 0 and attn_w % (2 * BLK) == 0

    pc = jnp.stack([2 * lax.axis_index("x") + lax.axis_index("y"), lax.axis_index("c")]).astype(jnp.int32)
    big = [w_in[0], w_out[0], w_ffn_gate[0], w_ffn_up[0], w_ffn_down[0]]
    names = ["in", "out", "gate", "up", "down"]
    bufs = _ag_chips([_cast_into(w, pc, "cast_" + n) for w, n in zip(big, names)])
    win_g, wout_g, wg_g, wu_g, wd_g = [
        _ag_pair(bf, pc, "ag_pair_" + n).reshape((N_CHIPS,) + w.shape) for bf, w, n in zip(bufs, big, names)]
    win_full = jnp.transpose(win_g, (1, 0, 2)).reshape(D, in_w)
    wout_full = wout_g.reshape(attn_w + gate_w, D)
    wd_full = wd_g.reshape(dff, D)

    qg2 = jnp.tile(q_norm_g, (1, 2))
    kg2 = jnp.tile(k_norm_g, (1, 2))
    lg, lb, wsp = gate_ln_g[0], gate_ln_b[0], w_spatial[0]
    bcol = jnp.broadcast_to(b_spatial[0][:, :, None], (G, BLK, BLK))

    xf = x.reshape(T, D)
    tgt = loss_target.reshape(T, D)
    h1 = _rms_fwd(xf, norm1_g, "norm1_fwd")
    proj = _matmul(h1, win_full, "nn", F32, "proj_fwd", tm=1024, tn=1664)
    ya, yg, yn = _mixer_fwd(d, proj, attn_sinks, qg2, kg2, lg, lb, wsp, bcol, out_norm_attn_g, out_norm_gate_g)
    x1 = _matmul(yn, wout_full, "nn", F32, "out_fwd", tm=1024, tn=1024, add=xf)
    h2 = _rms_fwd(x1, norm2_g, "norm2_fwd")
    a, b, f = _ffn_up(h2, wg_g, wu_g)
    x2 = _matmul(f, wd_full, "nn", F32, "ffn_down_fwd", tm=1024, tn=1024, tk=1408, add=x1)
    dx2, dx2b, loss_local = _loss(x2, tgt)

    g_d = _matmul(f, dx2b, "tn", BF16, "ffn_down_dw", tm=1408, tn=1024, tk=2048, out_slab="r")
    da, db = _ffn_down_dx(dx2b, wd_full, a, b)
    g_g = _matmul(h2, da, "tn", BF16, "ffn_gate_dw", tm=1024, tn=1408, tk=2048, out_slab="c")
    g_u = _matmul(h2, db, "tn", BF16, "ffn_up_dw", tm=1024, tn=1408, tk=2048, out_slab="c")
    dh2 = _matmul(da, wg_g, "nt", F32, "ffn_gate_dx", tm=1024, tn=1024, tk=1408, b_slab="k")
    dh2 = _matmul(db, wu_g, "nt", F32, "ffn_up_dx", tm=1024, tn=1024, tk=1408, b_slab="k", add=dh2)
    dx1, dx1b, dg_norm2 = _rms_bwd(x1, norm2_g, dh2, dx2, "norm2_bwd", True)
    g_o = _matmul(yn, dx1b, "tn", BF16, "out_dw", tm=512, tn=1024, tk=2048, out_slab="r")
    dy = _matmul(dx1b, wout_full, "nt", F32, "out_dx", tm=1024, tn=1024)
    (dproj, dkv, dqg, dkg, dsk, dlg, dlb, dwsp, dbsp, dgoa, dgog) = _mixer_bwd(
        d, proj, ya, yg, dy, attn_sinks, qg2, kg2, lg, lb, wsp, bcol, out_norm_attn_g, out_norm_gate_g)
    dproj = _put_kv(d, dproj, dkv)
    g_in_full = _matmul(h1, dproj, "tn", BF16, "proj_dw", tm=1024, tn=1664, tk=2048)
    dh1 = _matmul(dproj, win_full, "nt", F32, "proj_dx", tm=1024, tn=1024)
    dx, dg_norm1 = _rms_bwd(xf, norm1_g, dh1, dx1, "norm1_bwd", False)
    g_i = jnp.transpose(g_in_full.reshape(D, N_CHIPS, in_w // N_CHIPS), (1, 0, 2))

    big_m = [m_w_in[0], m_w_out[0], m_w_ffn_gate[0], m_w_ffn_up[0], m_w_ffn_down[0]]
    big_v = [v_w_in[0], v_w_out[0], v_w_ffn_gate[0], v_w_ffn_up[0], v_w_ffn_down[0]]
    g4s = [g.reshape(N_CHIPS, 2, g.shape[1] // 2, g.shape[2]) for g in (g_i, g_o, g_g, g_u, g_d)]
    pair = [_add_pair(g4, _swap_halves(g4, pc, "rs_swap_" + n), pc, "rs_add_pair_" + n) for g4, n in zip(g4s, names)]
    landed = _rs_chips(pair)
    halves = [_sum_slabs([bq], F32, "rs_add_chips_" + n) for bq, n in zip(landed, names)]
    big_grads, big_d, big_nm, big_nv = [], [], [], []
    for w, own, m, v, n in zip(big, halves, big_m, big_v, names):
        outs = _adamw_halves(w, own, _send_rows(own, "rs_send_" + n), m, v, pc, "adamw_" + n)
        for lst, o in zip((big_grads, big_d, big_nm, big_nv), outs):
            lst.append(o.reshape(w.shape))

    small_names_w = [norm1_g, q_norm_g, k_norm_g, attn_sinks, gate_ln_g, gate_ln_b, w_spatial, b_spatial,
                     out_norm_attn_g, out_norm_gate_g, norm2_g]
    small_m = [m_norm1_g, m_q_norm_g, m_k_norm_g, m_attn_sinks, m_gate_ln_g, m_gate_ln_b, m_w_spatial, m_b_spatial,
               m_out_norm_attn_g, m_out_norm_gate_g, m_norm2_g]
    small_v = [v_norm1_g, v_q_norm_g, v_k_norm_g, v_attn_sinks, v_gate_ln_g, v_gate_ln_b, v_w_spatial, v_b_spatial,
               v_out_norm_attn_g, v_out_norm_gate_g, v_norm2_g]
    dqg64 = dqg[:, :HEAD_DIM] + dqg[:, HEAD_DIM:]
    dkg64 = dkg[:, :HEAD_DIM] + dkg[:, HEAD_DIM:]
    small_g_local = [dg_norm1, dqg64, dkg64, dsk[:, :d.n_heads], dlg, dlb, dwsp, dbsp, dgoa, dgog, dg_norm2]
    shapes = [w.shape for w in small_names_w]
    sg = _ar_small(_pack(small_g_local))
    sd, snm, snv = _adamw(_pack(small_names_w), sg, _pack(small_m), _pack(small_v), "adamw_small")
    small_g, small_d, small_nm, small_nv = (_unpack(t, shapes) for t in (sg, sd, snm, snv))

    loss = lax.psum(loss_local[0, 0], ("x", "y", "c"))

    def order(small, bigs):
        s = list(small)
        bg = [t[None] for t in bigs]
        return [s[0], bg[0], s[1], s[2], s[3], s[4], s[5], s[6], s[7], s[8], s[9], bg[1], s[10], bg[2], bg[3], bg[4]]

    grad_x = dx.reshape(bl, seq, D)
    return (loss, grad_x, *order(small_g, big_grads), *order(small_d, big_d), *order(small_nm, big_nm),
            *order(small_nv, big_nv))
```

```python
import functools

import jax
import jax.numpy as jnp
from jax import lax
from jax.experimental import pallas as pl
from jax.experimental.pallas import tpu as pltpu

F32 = jnp.float32
BF16 = jnp.bfloat16
MESH = pl.DeviceIdType.MESH

EPS = 1e-6
HEAD_DIM = 64
N_KV_HEADS = 2
BLK = 128
N_CHIPS = 4
N_DEV = 8
NEG = -1e30

ADAM_LR = 0.001
ADAM_B1 = 0.9
ADAM_B2 = 0.999
ADAM_EPS = 1e-08
ADAM_WD = 0.01
ADAM_STEP = 10

VMEM_LIMIT = 56 * 1024 * 1024

NN = (((1,), (0,)), ((), ()))
NT = (((1,), (1,)), ((), ()))
TN = (((0,), (0,)), ((), ()))
HBM = pl.BlockSpec(memory_space=pltpu.HBM)
ANY = pl.BlockSpec(memory_space=pl.ANY)
SEM = pl.BlockSpec(memory_space=pltpu.SEMAPHORE)
EFFECT = pltpu.SideEffectType.DATAFLOW_SIDE_EFFECTING


def _dot(a, b, dn):
    return lax.dot_general(a, b, dn, preferred_element_type=F32)


def _pick(dim, pref, align=128):
    if dim <= pref:
        return dim
    t = (pref // align) * align
    while t >= align:
        if dim % t == 0:
            return t
        t -= align
    return dim


def _params(sem):
    return pltpu.CompilerParams(dimension_semantics=sem, vmem_limit_bytes=VMEM_LIMIT)


def _mm_body(dn, nk, has_add, has_after, *refs):
    a_ref, b_ref = refs[:2]
    add_ref = refs[2] if has_add else None
    o_ref = refs[2 + has_add + has_after]
    p = _dot(a_ref[...], b_ref[...], dn)

    def finish(r):
        if add_ref is not None:
            r = r + add_ref[...]
        o_ref[...] = r.astype(o_ref.dtype)

    if nk == 1:
        finish(p)
        return
    acc_ref = refs[-1]
    k = pl.program_id(2)

    @pl.when(k == 0)
    def _():
        acc_ref[...] = jnp.zeros_like(acc_ref)

    acc_ref[...] += p

    @pl.when(k == nk - 1)
    def _():
        finish(acc_ref[...])


def _matmul(a, b, mode, out_dtype, name, *, tm, tn, tk=None, add=None, b_slab=None, out_slab=None, after=None):
    if mode == "nn":
        M, K = a.shape
        N = b.shape[0] * b.shape[2] if b_slab == "c" else b.shape[1]
    elif mode == "nt":
        M, K = a.shape
        N = b.shape[1] if b_slab == "k" else b.shape[0]
    else:
        K, M = a.shape
        N = b.shape[1]
    tk = K if tk is None else tk
    tm, tn, tk = _pick(M, tm), _pick(N, tn), _pick(K, tk)
    if b_slab == "c":
        tn = _pick(b.shape[2], tn)
    if b_slab == "k":
        tk = _pick(b.shape[2], tk)
    if out_slab == "c":
        tn = _pick(N // N_CHIPS, tn)
    if out_slab == "r":
        tm = _pick(M // N_CHIPS, tm)
    gm, gn, gk = M // tm, N // tn, K // tk

    if mode == "tn":
        a_spec = pl.BlockSpec((tk, tm), lambda j, i, k: (k, i))
        b_spec = pl.BlockSpec((tk, tn), lambda j, i, k: (k, j))
    else:
        a_spec = pl.BlockSpec((tm, tk), lambda j, i, k: (i, k))
        if b_slab == "c":
            per = b.shape[2] // tn
            b_spec = pl.BlockSpec((None, tk, tn), lambda j, i, k: (j // per, k, j % per))
        elif b_slab == "k":
            per = b.shape[2] // tk
            b_spec = pl.BlockSpec((None, tn, tk), lambda j, i, k: (k // per, j, k % per))
        elif mode == "nn":
            b_spec = pl.BlockSpec((tk, tn), lambda j, i, k: (k, j))
        else:
            b_spec = pl.BlockSpec((tn, tk), lambda j, i, k: (j, k))

    if out_slab == "c":
        per = (N // N_CHIPS) // tn
        o_spec = pl.BlockSpec((None, tm, tn), lambda j, i, k: (j // per, i, j % per))
        o_shape = jax.ShapeDtypeStruct((N_CHIPS, M, N // N_CHIPS), out_dtype)
    elif out_slab == "r":
        per = (M // N_CHIPS) // tm
        o_spec = pl.BlockSpec((None, tm, tn), lambda j, i, k: (i // per, i % per, j))
        o_shape = jax.ShapeDtypeStruct((N_CHIPS, M // N_CHIPS, N), out_dtype)
    else:
        o_spec = pl.BlockSpec((tm, tn), lambda j, i, k: (i, j))
        o_shape = jax.ShapeDtypeStruct((M, N), out_dtype)

    dn = {"nn": NN, "nt": NT, "tn": TN}[mode]
    in_specs = [a_spec, b_spec]
    args = [a, b]
    if add is not None:
        in_specs.append(pl.BlockSpec((tm, tn), lambda j, i, k: (i, j)))
        args.append(add)
    if after is not None:
        in_specs.append(ANY)
        args.append(after)
    return pl.pallas_call(
        functools.partial(_mm_body, dn, gk, add is not None, after is not None),
        name=name,
        grid=(gn, gm, gk),
        in_specs=in_specs,
        out_specs=o_spec,
        out_shape=o_shape,
        scratch_shapes=[pltpu.VMEM((tm, tn), F32)] if gk > 1 else [],
        compiler_params=_params(("parallel", "parallel", "arbitrary")),
    )(*args)


def _rms_fwd_body(x_ref, g_ref, *rest):
    h_ref = rest[-1]
    x = x_ref[...]
    r = lax.rsqrt(jnp.mean(x * x, axis=-1, keepdims=True) + EPS)
    h_ref[...] = (x * r * g_ref[...]).astype(h_ref.dtype)


def _rms_fwd(x, g, name, after=None):
    T, D = x.shape
    tr = _pick(T, 256, 16)
    extra = [] if after is None else [after]
    return pl.pallas_call(
        functools.partial(_rms_fwd_body),
        name=name,
        grid=(T // tr,),
        in_specs=[pl.BlockSpec((tr, D), lambda i: (i, 0)), pl.BlockSpec((1, D), lambda i: (0, 0))] + [ANY] * len(extra),
        out_specs=pl.BlockSpec((tr, D), lambda i: (i, 0)),
        out_shape=jax.ShapeDtypeStruct((T, D), BF16),
        compiler_params=_params(("parallel",)),
    )(x, g, *extra)


def _rms_bwd_body(with_bf16, x_ref, g_ref, dh_ref, res_ref, dx_ref, *rest):
    dg_ref = rest[-1]

    @pl.when(pl.program_id(0) == 0)
    def _():
        dg_ref[...] = jnp.zeros_like(dg_ref)

    x = x_ref[...]
    r = lax.rsqrt(jnp.mean(x * x, axis=-1, keepdims=True) + EPS)
    xh = x * r
    dh = dh_ref[...]
    dg_ref[...] += jnp.sum(dh * xh, axis=0, keepdims=True)
    t = dh * g_ref[...]
    dx = res_ref[...] + r * (t - xh * jnp.mean(t * xh, axis=-1, keepdims=True))
    dx_ref[...] = dx
    if with_bf16:
        rest[0][...] = dx.astype(BF16)


def _rms_bwd(x, g, dh, res, name, with_bf16):
    T, D = x.shape
    tr = _pick(T, 256, 16)
    row = pl.BlockSpec((tr, D), lambda i: (i, 0))
    vec = pl.BlockSpec((1, D), lambda i: (0, 0))
    extra = [jax.ShapeDtypeStruct((T, D), BF16)] if with_bf16 else []
    return pl.pallas_call(
        functools.partial(_rms_bwd_body, with_bf16),
        name=name,
        grid=(T // tr,),
        in_specs=[row, vec, row, row],
        out_specs=[row] + [row] * len(extra) + [vec],
        out_shape=[jax.ShapeDtypeStruct((T, D), F32)] + extra + [jax.ShapeDtypeStruct((1, D), F32)],
        compiler_params=_params(("arbitrary",)),
    )(x, g, dh, res)


def _ffn_up_body(h_ref, wg_ref, wu_ref, a_ref, b_ref, f_ref):
    h = h_ref[...]
    a = _dot(h, wg_ref[...], NN)
    b = _dot(h, wu_ref[...], NN)
    a_ref[...] = a
    b_ref[...] = b
    f_ref[...] = (a * (1.0 / (1.0 + jnp.exp(-a))) * b).astype(f_ref.dtype)


def _ffn_up(h, wg, wu):
    T, D = h.shape
    n, _, fs = wg.shape
    tm, tn = _pick(T, 512), fs
    hs = pl.BlockSpec((tm, D), lambda j, i: (i, 0))
    ws = pl.BlockSpec((None, D, tn), lambda j, i: (j, 0, 0))
    os = pl.BlockSpec((tm, tn), lambda j, i: (i, j))
    return pl.pallas_call(
        functools.partial(_ffn_up_body),
        name="ffn_up_fwd",
        grid=(n, T // tm),
        in_specs=[hs, ws, ws],
        out_specs=[os, os, os],
        out_shape=[jax.ShapeDtypeStruct((T, n * fs), F32), jax.ShapeDtypeStruct((T, n * fs), F32),
                   jax.ShapeDtypeStruct((T, n * fs), BF16)],
        compiler_params=_params(("parallel", "parallel")),
    )(h, wg, wu)


def _ffn_down_dx_body(dx_ref, wd_ref, a_ref, b_ref, after_ref, da_ref, db_ref):
    del after_ref
    df = _dot(dx_ref[...], wd_ref[...], NT)
    a = a_ref[...]
    s = 1.0 / (1.0 + jnp.exp(-a))
    da_ref[...] = (df * b_ref[...] * (s * (1.0 + a * (1.0 - s)))).astype(da_ref.dtype)
    db_ref[...] = (df * (a * s)).astype(db_ref.dtype)


def _ffn_down_dx(dx2b, wd, a, b, after):
    T, D = dx2b.shape
    F = wd.shape[0]
    tm, tn = _pick(T, 512), _pick(F, 1408)
    xs = pl.BlockSpec((tm, D), lambda j, i: (i, 0))
    ws = pl.BlockSpec((tn, D), lambda j, i: (j, 0))
    os = pl.BlockSpec((tm, tn), lambda j, i: (i, j))
    return pl.pallas_call(
        functools.partial(_ffn_down_dx_body),
        name="ffn_down_dx",
        grid=(F // tn, T // tm),
        in_specs=[xs, ws, os, os, ANY],
        out_specs=[os, os],
        out_shape=[jax.ShapeDtypeStruct((T, F), BF16), jax.ShapeDtypeStruct((T, F), BF16)],
        compiler_params=_params(("parallel", "parallel")),
    )(dx2b, wd, a, b, after)


def _loss_body(inv_d, x2_ref, tgt_ref, dx2_ref, dx2b_ref, loss_ref):
    @pl.when(pl.program_id(0) == 0)
    def _():
        loss_ref[...] = jnp.zeros_like(loss_ref)

    e = x2_ref[...] - tgt_ref[...]
    dx2 = e * inv_d
    dx2_ref[...] = dx2
    dx2b_ref[...] = dx2.astype(BF16)
    row = jnp.sum(e * e, axis=-1, keepdims=True) * inv_d
    loss_ref[...] += 0.5 * jnp.sum(row, axis=0, keepdims=True)


def _loss(x2, tgt):
    T, D = x2.shape
    tr = _pick(T, 256, 16)
    row = pl.BlockSpec((tr, D), lambda i: (i, 0))
    return pl.pallas_call(
        functools.partial(_loss_body, 1.0 / D),
        name="loss_head",
        grid=(T // tr,),
        in_specs=[row, row],
        out_specs=[row, row, pl.BlockSpec((1, 1), lambda i: (0, 0))],
        out_shape=[jax.ShapeDtypeStruct((T, D), F32), jax.ShapeDtypeStruct((T, D), BF16),
                   jax.ShapeDtypeStruct((1, 1), F32)],
        compiler_params=_params(("arbitrary",)),
    )(x2, tgt)


def _lo_mask(shape):
    return lax.broadcasted_iota(jnp.int32, shape, len(shape) - 1) < HEAD_DIM


def _half_sums(t, lo):
    s_lo = jnp.sum(jnp.where(lo, t, 0.0), axis=-1, keepdims=True)
    s_hi = jnp.sum(jnp.where(lo, 0.0, t), axis=-1, keepdims=True)
    return jnp.where(lo, s_lo, s_hi)


def _head_rstd(t, lo):
    return lax.rsqrt(_half_sums(t * t, lo) * (1.0 / HEAD_DIM) + EPS)


def _place(t, lo, kv_head):
    if kv_head == 0:
        t_lo = jnp.where(lo, t, 0.0)
        t_hi = pltpu.roll(t_lo, HEAD_DIM, 1)
    else:
        t_hi = jnp.where(lo, 0.0, t)
        t_lo = pltpu.roll(t_hi, HEAD_DIM, 1)
    return jnp.concatenate([t_lo, t_hi], axis=0).astype(BF16)


def _unplace(c0, c1, lo):
    return jnp.where(lo, c0 + pltpu.roll(c0, HEAD_DIM, 1), c1 + pltpu.roll(c1, HEAD_DIM, 1))


def _band(kv_cur, kv_prev, kg, lo2):
    kb = jnp.concatenate([kv_prev[:, :BLK], kv_cur[:, :BLK]], axis=0)
    vb = jnp.concatenate([kv_prev[:, BLK:], kv_cur[:, BLK:]], axis=0)
    rk = _head_rstd(kb, lo2)
    kn = kb * rk * kg
    kk = [_place(kn, lo2, h) for h in range(N_KV_HEADS)]
    vv = [_place(vb, lo2, h) for h in range(N_KV_HEADS)]
    return kb, rk, kk, vv


def _score_geometry(first_i32):
    qi = lax.broadcasted_iota(jnp.int32, (BLK, 4 * BLK), 0)
    col = lax.broadcasted_iota(jnp.int32, (BLK, 4 * BLK), 1)
    kj = col & (2 * BLK - 1)
    dist = qi + BLK - kj
    valid = (dist >= 0) & (dist < BLK) & (kj >= first_i32 * BLK)
    return col, dist.astype(F32), valid


def _pair_probs(qn, kk, col, distf, valid, slope0, slope1, sink0, sink1):
    s = _dot(qn.astype(BF16), kk, NT) * (HEAD_DIM ** -0.5)
    slope = jnp.where(col < 2 * BLK, slope0, slope1)
    logits = jnp.where(valid, s - slope * distf, NEG)
    probs, psink = [], []
    for hh, sk in ((0, sink0), (1, sink1)):
        l = logits[:, 2 * BLK * hh:2 * BLK * (hh + 1)]
        m = jnp.maximum(jnp.max(l, axis=-1, keepdims=True), sk)
        p = jnp.exp(l - m)
        es = jnp.exp(sk - m)
        inv = 1.0 / (jnp.sum(p, axis=-1, keepdims=True) + es)
        probs.append(p * inv)
        psink.append(es * inv)
    return probs, psink


def _gelu(z):
    return 0.5 * z * (1.0 + lax.erf(z * (0.5 ** 0.5)))


def _gelu_grad(z):
    return 0.5 * (1.0 + lax.erf(z * (0.5 ** 0.5))) + z * jnp.exp(-0.5 * z * z) * ((2.0 * jnp.pi) ** -0.5)


def _tril_w(w):
    r = lax.broadcasted_iota(jnp.int32, (BLK, BLK), 0)
    c = lax.broadcasted_iota(jnp.int32, (BLK, BLK), 1)
    return jnp.where(r >= c, w, 0.0), r >= c


def _gate_fwd_group(zu, zv, lg, lb, w, bcol):
    u = _gelu(zu)
    v = _gelu(zv)
    mu = jnp.mean(v, axis=-1, keepdims=True)
    vc = v - mu
    rs = lax.rsqrt(jnp.mean(vc * vc, axis=-1, keepdims=True) + EPS)
    vh = vc * rs
    vn = vh * lg + lb
    wt, tril = _tril_w(w)
    mixed = _dot(wt.astype(BF16), vn.astype(BF16), NN) + bcol
    return u, vh, rs, vn, wt, tril, mixed


class _Dims:
    def __init__(self, seq, attn_w, gate_w):
        self.seq, self.attn_w, self.gate_w = seq, attn_w, gate_w
        self.n_heads = attn_w // HEAD_DIM
        self.group = self.n_heads // N_KV_HEADS
        self.n_pairs = attn_w // BLK
        self.n_groups = gate_w // BLK
        self.kv_col = attn_w // (2 * BLK)
        self.u0 = attn_w + 2 * BLK
        self.v0 = self.u0 + gate_w
        self.in_w = self.v0 + gate_w
        self.slopes = [2.0 ** (-8.0 * (h + 1) / self.n_heads) for h in range(self.n_heads)]


def _mixer_fwd_body(d, sink_ref, proj_ref, kvp_ref, qg_ref, kg_ref, lg_ref, lb_ref, w_ref, b_ref, goa_ref, gog_ref,
                    ya_ref, yg_ref, y_ref):
    i = pl.program_id(0)
    first = (i % (d.seq // BLK) == 0).astype(jnp.int32)
    lo = _lo_mask((BLK, BLK))
    lo2 = _lo_mask((2 * BLK, BLK))
    kv_cur = proj_ref[:, d.attn_w:d.attn_w + 2 * BLK]
    _, _, kk, vv = _band(kv_cur, kvp_ref[...], kg_ref[...], lo2)
    col, distf, valid = _score_geometry(first)
    qg = qg_ref[...]
    for j in range(d.n_pairs):
        h0, h1 = 2 * j, 2 * j + 1
        kh = h0 // d.group
        q2 = proj_ref[:, BLK * j:BLK * (j + 1)]
        qn = q2 * _head_rstd(q2, lo) * qg
        probs, _ = _pair_probs(qn, kk[kh], col, distf, valid, d.slopes[h0], d.slopes[h1],
                               sink_ref[0, h0], sink_ref[0, h1])
        p = jnp.concatenate(probs, axis=1).astype(BF16)
        ya_ref[:, BLK * j:BLK * (j + 1)] = _dot(p, vv[kh], NN)
    for g in range(d.n_groups):
        zu = proj_ref[:, d.u0 + BLK * g:d.u0 + BLK * (g + 1)]
        zv = proj_ref[:, d.v0 + BLK * g:d.v0 + BLK * (g + 1)]
        u, _, _, _, _, _, mixed = _gate_fwd_group(zu, zv, lg_ref[g:g + 1, :], lb_ref[g:g + 1, :], w_ref[g], b_ref[g])
        yg_ref[:, BLK * g:BLK * (g + 1)] = u * mixed
    ya = ya_ref[...]
    ra = lax.rsqrt(jnp.mean(ya * ya, axis=-1, keepdims=True) + EPS)
    y_ref[:, :d.attn_w] = (ya * ra * goa_ref[...]).astype(y_ref.dtype)
    yg = yg_ref[...]
    rg = lax.rsqrt(jnp.mean(yg * yg, axis=-1, keepdims=True) + EPS)
    y_ref[:, d.attn_w:] = (yg * rg * gog_ref[...]).astype(y_ref.dtype)


def _mixer_specs(d, T):
    row = lambda w: pl.BlockSpec((BLK, w), lambda i: (i, 0))
    const2 = lambda a: pl.BlockSpec(a.shape, lambda i: (0, 0))
    const3 = lambda a: pl.BlockSpec(a.shape, lambda i: (0, 0, 0))
    kv_prev = pl.BlockSpec((BLK, 2 * BLK), lambda i: (jnp.maximum(i - 1, 0), d.kv_col))
    return row, const2, const3, kv_prev


def _mixer_fwd(d, proj, sinks, qg2, kg2, lg, lb, wsp, bcol, goa, gog):
    T = proj.shape[0]
    row, const2, const3, kv_prev = _mixer_specs(d, T)
    return pl.pallas_call(
        functools.partial(_mixer_fwd_body, d),
        name="mixer_fwd",
        grid=(T // BLK,),
        in_specs=[pl.BlockSpec(memory_space=pltpu.SMEM), row(d.in_w), kv_prev, const2(qg2), const2(kg2),
                  const2(lg), const2(lb), const3(wsp), const3(bcol), const2(goa), const2(gog)],
        out_specs=[row(d.attn_w), row(d.gate_w), row(d.attn_w + d.gate_w)],
        out_shape=[jax.ShapeDtypeStruct((T, d.attn_w), F32), jax.ShapeDtypeStruct((T, d.gate_w), F32),
                   jax.ShapeDtypeStruct((T, d.attn_w + d.gate_w), BF16)],
        compiler_params=_params(("parallel",)),
    )(sinks, proj, proj, qg2, kg2, lg, lb, wsp, bcol, goa, gog)


def _mixer_bwd_body(d, sink_ref, proj_ref, kvp_ref, ya_ref, yg_ref, dy_ref, qg_ref, kg_ref, lg_ref, lb_ref, w_ref,
                    b_ref, goa_ref, gog_ref,
                    dproj_ref, dkv_ref, dqg_ref, dkg_ref, dsk_ref, dlg_ref, dlb_ref, dw_ref, db_ref, dgoa_ref,
                    dgog_ref):
    i = pl.program_id(0)

    @pl.when(i == 0)
    def _():
        for r in (dqg_ref, dkg_ref, dsk_ref, dlg_ref, dlb_ref, dw_ref, db_ref, dgoa_ref, dgog_ref):
            r[...] = jnp.zeros_like(r)

    first = (i % (d.seq // BLK) == 0).astype(jnp.int32)
    lo = _lo_mask((BLK, BLK))
    lo2 = _lo_mask((2 * BLK, BLK))
    lane_row = lax.broadcasted_iota(jnp.int32, (1, BLK), 1)

    ya = ya_ref[...]
    ra = lax.rsqrt(jnp.mean(ya * ya, axis=-1, keepdims=True) + EPS)
    yah = ya * ra
    dyn = dy_ref[:, :d.attn_w]
    dgoa_ref[...] += jnp.sum(dyn * yah, axis=0, keepdims=True)
    t = dyn * goa_ref[...]
    dya = ra * (t - yah * jnp.mean(t * yah, axis=-1, keepdims=True))
    yg = yg_ref[...]
    rg = lax.rsqrt(jnp.mean(yg * yg, axis=-1, keepdims=True) + EPS)
    ygh = yg * rg
    dyn = dy_ref[:, d.attn_w:]
    dgog_ref[...] += jnp.sum(dyn * ygh, axis=0, keepdims=True)
    t = dyn * gog_ref[...]
    dyg = rg * (t - ygh * jnp.mean(t * ygh, axis=-1, keepdims=True))

    kv_cur = proj_ref[:, d.attn_w:d.attn_w + 2 * BLK]
    kg = kg_ref[...]
    kb, rk, kk, vv = _band(kv_cur, kvp_ref[...], kg, lo2)
    col, distf, valid = _score_geometry(first)
    qg = qg_ref[...]
    ck = [jnp.zeros((2 * BLK, BLK), F32) for _ in range(N_KV_HEADS)]
    cv = [jnp.zeros((2 * BLK, BLK), F32) for _ in range(N_KV_HEADS)]
    dsk = jnp.zeros((1, BLK), F32)
    dqg = jnp.zeros((1, BLK), F32)
    for j in range(d.n_pairs):
        h0, h1 = 2 * j, 2 * j + 1
        kh = h0 // d.group
        cols = slice(BLK * j, BLK * (j + 1))
        q2 = proj_ref[:, cols]
        rq = _head_rstd(q2, lo)
        qh = q2 * rq
        qn = qh * qg
        probs, psink = _pair_probs(qn, kk[kh], col, distf, valid, d.slopes[h0], d.slopes[h1],
                                   sink_ref[0, h0], sink_ref[0, h1])
        do2 = dya[:, cols]
        prod = do2 * ya[:, cols]
        delta = (jnp.sum(jnp.where(lo, prod, 0.0), axis=-1, keepdims=True),
                 jnp.sum(jnp.where(lo, 0.0, prod), axis=-1, keepdims=True))
        do2b = do2.astype(BF16)
        dp = _dot(do2b, vv[kh], NT)
        ds = []
        for hh in (0, 1):
            ds.append(probs[hh] * (dp[:, 2 * BLK * hh:2 * BLK * (hh + 1)] - delta[hh]))
            dsink = -jnp.sum(psink[hh] * delta[hh], axis=0, keepdims=True)
            dsk = dsk + jnp.where(lane_row == (h0 + hh), dsink, 0.0)
        dsb = (jnp.concatenate(ds, axis=1) * (HEAD_DIM ** -0.5)).astype(BF16)
        pb = jnp.concatenate(probs, axis=1).astype(BF16)
        qnb = qn.astype(BF16)
        dqn = _dot(dsb, kk[kh], NN)
        dkk = _dot(dsb, qnb, TN)
        dvv = _dot(pb, do2b, TN)
        ck[kh] = ck[kh] + jnp.where(lo2, dkk[:2 * BLK], 0.0) + jnp.where(lo2, 0.0, dkk[2 * BLK:])
        cv[kh] = cv[kh] + jnp.where(lo2, dvv[:2 * BLK], 0.0) + jnp.where(lo2, 0.0, dvv[2 * BLK:])
        dqg = dqg + jnp.sum(dqn * qh, axis=0, keepdims=True)
        t = dqn * qg
        dq2 = rq * (t - qh * (_half_sums(t * qh, lo) * (1.0 / HEAD_DIM)))
        dproj_ref[:, cols] = dq2.astype(dproj_ref.dtype)
    dsk_ref[...] += dsk
    dqg_ref[...] += dqg
    dkn = _unplace(ck[0], ck[1], lo2)
    dvb = _unplace(cv[0], cv[1], lo2)
    khat = kb * rk
    dkg_ref[...] += jnp.sum(dkn * khat, axis=0, keepdims=True)
    t = dkn * kg
    dkb = rk * (t - khat * (_half_sums(t * khat, lo2) * (1.0 / HEAD_DIM)))
    rows_cur = pl.ds(pl.multiple_of(i * BLK, BLK), BLK)
    rows_prev = pl.ds(pl.multiple_of(jnp.maximum(i - 1, 0) * BLK, BLK), BLK)
    dkv_ref[rows_cur, :] = jnp.concatenate([dkb[BLK:], dvb[BLK:]], axis=1)
    dkv_ref[rows_prev, :] += jnp.concatenate([dkb[:BLK], dvb[:BLK]], axis=1)
    dproj_ref[:, d.attn_w:d.attn_w + 2 * BLK] = jnp.zeros((BLK, 2 * BLK), dproj_ref.dtype)

    for g in range(d.n_groups):
        ucols = slice(d.u0 + BLK * g, d.u0 + BLK * (g + 1))
        vcols = slice(d.v0 + BLK * g, d.v0 + BLK * (g + 1))
        zu = proj_ref[:, ucols]
        zv = proj_ref[:, vcols]
        lg = lg_ref[g:g + 1, :]
        u, vh, rs, vn, wt, tril, mixed = _gate_fwd_group(zu, zv, lg, lb_ref[g:g + 1, :], w_ref[g], b_ref[g])
        dyg_g = dyg[:, BLK * g:BLK * (g + 1)]
        du = dyg_g * mixed
        dmix = dyg_g * u
        dmb = dmix.astype(BF16)
        db_ref[g:g + 1, :] += jnp.sum(jnp.transpose(dmix), axis=0, keepdims=True)
        dw_ref[g] += jnp.where(tril, _dot(dmb, vn.astype(BF16), NT), 0.0)
        dvn = _dot(wt.astype(BF16), dmb, TN)
        dlg_ref[g:g + 1, :] += jnp.sum(dvn * vh, axis=0, keepdims=True)
        dlb_ref[g:g + 1, :] += jnp.sum(dvn, axis=0, keepdims=True)
        dvh = dvn * lg
        dv = rs * (dvh - jnp.mean(dvh, axis=-1, keepdims=True) - vh * jnp.mean(dvh * vh, axis=-1, keepdims=True))
        dproj_ref[:, ucols] = (du * _gelu_grad(zu)).astype(dproj_ref.dtype)
        dproj_ref[:, vcols] = (dv * _gelu_grad(zv)).astype(dproj_ref.dtype)


def _mixer_bwd(d, proj, ya, yg, dy, sinks, qg2, kg2, lg, lb, wsp, bcol, goa, gog):
    T = proj.shape[0]
    row, const2, const3, kv_prev = _mixer_specs(d, T)
    acc2 = lambda s: pl.BlockSpec(s, lambda i: (0, 0))
    G = d.n_groups
    out_shapes = [((T, d.in_w), BF16), ((T, 2 * BLK), F32), ((1, BLK), F32), ((1, BLK), F32), ((1, BLK), F32),
                  ((G, BLK), F32), ((G, BLK), F32), ((G, BLK, BLK), F32), ((G, BLK), F32),
                  ((1, d.attn_w), F32), ((1, d.gate_w), F32)]
    out_specs = [row(d.in_w)] + [acc2(s) for s, _ in out_shapes[1:7]] + \
                [pl.BlockSpec((G, BLK, BLK), lambda i: (0, 0, 0))] + [acc2(s) for s, _ in out_shapes[8:]]
    return pl.pallas_call(
        functools.partial(_mixer_bwd_body, d),
        name="mixer_bwd",
        grid=(T // BLK,),
        in_specs=[pl.BlockSpec(memory_space=pltpu.SMEM), row(d.in_w), kv_prev, row(d.attn_w), row(d.gate_w),
                  row(d.attn_w + d.gate_w), const2(qg2), const2(kg2), const2(lg), const2(lb), const3(wsp),
                  const3(bcol), const2(goa), const2(gog)],
        out_specs=out_specs,
        out_shape=[jax.ShapeDtypeStruct(s, t) for s, t in out_shapes],
        compiler_params=_params(("arbitrary",)),
    )(sinks, proj, proj, ya, yg, dy, qg2, kg2, lg, lb, wsp, bcol, goa, gog)


def _put_kv_body(dkv_ref, dproj_in_ref, dproj_ref):
    del dproj_in_ref
    dproj_ref[...] = dkv_ref[...].astype(dproj_ref.dtype)


def _put_kv(d, dproj, dkv):
    T = dproj.shape[0]
    tr = _pick(T, 1024, 16)
    return pl.pallas_call(
        functools.partial(_put_kv_body),
        name="put_kv",
        grid=(T // tr,),
        in_specs=[pl.BlockSpec((tr, 2 * BLK), lambda i: (i, 0)), pl.BlockSpec(memory_space=pl.ANY)],
        out_specs=pl.BlockSpec((tr, 2 * BLK), lambda i: (i, d.kv_col)),
        out_shape=jax.ShapeDtypeStruct(dproj.shape, dproj.dtype),
        input_output_aliases={1: 0},
        compiler_params=_params(("parallel",)),
    )(dkv, dproj)


def _add_pair_body(pc_ref, own_ref, got_ref, o_ref):
    del pc_ref
    o_ref[...] = (own_ref[...].astype(F32) + got_ref[...].astype(F32)).astype(o_ref.dtype)


def _add_pair(g4, got, pc, name):
    n, _, h, C = g4.shape
    tr = _pick(h, 512, 16)
    return pl.pallas_call(
        functools.partial(_add_pair_body),
        name=name,
        grid_spec=pltpu.PrefetchScalarGridSpec(
            num_scalar_prefetch=1,
            grid=(n, h // tr),
            in_specs=[pl.BlockSpec((None, None, tr, C), lambda q, i, pc: (q, pc[1], i, 0)),
                      pl.BlockSpec((None, tr, C), lambda q, i, pc: (q, i, 0))],
            out_specs=pl.BlockSpec((None, tr, C), lambda q, i, pc: (q, i, 0)),
        ),
        out_shape=jax.ShapeDtypeStruct((n, h, C), g4.dtype),
        compiler_params=_params(("parallel", "parallel")),
    )(pc, g4, got)


def _adamw_update(w, g, m, v):
    m = ADAM_B1 * m + (1.0 - ADAM_B1) * g
    v = ADAM_B2 * v + (1.0 - ADAM_B2) * (g * g)
    m_hat = m / (1.0 - ADAM_B1 ** ADAM_STEP)
    v_hat = v / (1.0 - ADAM_B2 ** ADAM_STEP)
    return -ADAM_LR * (m_hat / (jnp.sqrt(v_hat) + ADAM_EPS) + ADAM_WD * w), m, v


def _adamw_body(w_ref, g_ref, m_ref, v_ref, d_ref, nm_ref, nv_ref):
    d_ref[...], nm_ref[...], nv_ref[...] = _adamw_update(w_ref[...], g_ref[...], m_ref[...], v_ref[...])


def _adamw(w, g, m, v, name):
    R, C = w.shape
    tr = _pick(R, 512, 8)
    blk = pl.BlockSpec((tr, C), lambda i: (i, 0))
    return pl.pallas_call(
        functools.partial(_adamw_body),
        name=name,
        grid=(R // tr,),
        in_specs=[blk] * 4,
        out_specs=[blk] * 3,
        out_shape=[jax.ShapeDtypeStruct((R, C), F32)] * 3,
        compiler_params=_params(("parallel",)),
    )(w, g, m, v)


def _adamw_halves_body(pc_ref, w_ref, own_ref, got_ref, m_ref, v_ref, g_ref, d_ref, nm_ref, nv_ref):
    mine = pl.program_id(0) == pc_ref[1]

    def update(g):
        g_ref[...] = g
        d_ref[...], nm_ref[...], nv_ref[...] = _adamw_update(w_ref[...], g, m_ref[...], v_ref[...])

    @pl.when(mine)
    def _():
        update(own_ref[...])

    @pl.when(jnp.logical_not(mine))
    def _():
        update(got_ref[...])


def _adamw_halves(w, own, got, m, v, pc, name):
    h, C = own.shape
    tr = _pick(h, 512, 8)
    full = pl.BlockSpec((None, tr, C), lambda hh, i, pc: (hh, i, 0))
    half = pl.BlockSpec((tr, C), lambda hh, i, pc: (i, 0))
    return pl.pallas_call(
        functools.partial(_adamw_halves_body),
        name=name,
        grid_spec=pltpu.PrefetchScalarGridSpec(
            num_scalar_prefetch=1,
            grid=(2, h // tr),
            in_specs=[full, half, half, full, full],
            out_specs=[full] * 4,
        ),
        out_shape=[jax.ShapeDtypeStruct((2, h, C), F32)] * 4,
        compiler_params=_params(("parallel", "parallel")),
    )(pc, w.reshape(2, h, C), own, got, m.reshape(2, h, C), v.reshape(2, h, C))


def _me():
    x, y, c = lax.axis_index("x"), lax.axis_index("y"), lax.axis_index("c")
    chips = [(1 - x, y), (x, 1 - y), (1 - x, 1 - y)]
    return x, y, c, chips


def _cast_into_body(pc_ref, w_ref, o_ref):
    del pc_ref
    o_ref[...] = w_ref[...].astype(o_ref.dtype)


def _cast_into(w, pc, name):
    Rs, C = w.shape
    h = Rs // 2
    tr = _pick(h, 512, 16)
    return pl.pallas_call(
        functools.partial(_cast_into_body),
        name=name,
        grid_spec=pltpu.PrefetchScalarGridSpec(
            num_scalar_prefetch=1,
            grid=(2, h // tr),
            in_specs=[pl.BlockSpec((None, tr, C), lambda hh, i, pc: (hh, i, 0))],
            out_specs=pl.BlockSpec((None, None, tr, C), lambda hh, i, pc: (pc[0], hh, i, 0)),
        ),
        out_shape=jax.ShapeDtypeStruct((N_CHIPS, 2, h, C), BF16),
        compiler_params=_params(("parallel", "parallel")),
    )(pc, w.reshape(2, h, C))


def _send_tile_to_sibling(tile_ref, dst_rows, dst_total, send_sem, recv_sem, last):
    x, y, c, _ = _me()
    cp = pltpu.make_async_remote_copy(src_ref=tile_ref, dst_ref=dst_rows, send_sem=send_sem, recv_sem=recv_sem,
                                      device_id=(x, y, 1 - c), device_id_type=MESH)
    cp.start()
    cp.wait_send()

    @pl.when(last)
    def _():
        pltpu.make_async_remote_copy(src_ref=dst_total, dst_ref=dst_total, send_sem=send_sem, recv_sem=recv_sem,
                                     device_id=(x, y, 1 - c), device_id_type=MESH).wait_recv()


def _ag_pair_body(tr, n_i, pc_ref, tile_ref, buf_ref, send_sem, recv_sem):
    j, i = pl.program_id(0), pl.program_id(1)
    q = pc_ref[0] ^ (j + 1)
    c = pc_ref[1]
    rows = pl.ds(pl.multiple_of(i * tr, tr), tr)
    _send_tile_to_sibling(tile_ref, buf_ref.at[pl.ds(q, 1), pl.ds(c, 1), rows], buf_ref.at[pl.ds(0, N_CHIPS - 1), 0],
                          send_sem, recv_sem, jnp.logical_and(j == N_CHIPS - 2, i == n_i - 1))


def _ag_pair(buf, pc, name):
    _, _, h, C = buf.shape
    tr = _pick(h, 512, 16)
    return pl.pallas_call(
        functools.partial(_ag_pair_body, tr, h // tr),
        name=name,
        grid_spec=pltpu.PrefetchScalarGridSpec(
            num_scalar_prefetch=1,
            grid=(N_CHIPS - 1, h // tr),
            in_specs=[pl.BlockSpec((1, 1, tr, C), lambda j, i, pc: (pc[0] ^ (j + 1), pc[1], i, 0))],
            out_specs=HBM,
            scratch_shapes=[pltpu.SemaphoreType.DMA(()), pltpu.SemaphoreType.DMA(())],
        ),
        out_shape=jax.ShapeDtypeStruct(buf.shape, buf.dtype),
        input_output_aliases={1: 0},
        compiler_params=_params(("arbitrary", "arbitrary")),
    )(pc, buf)


def _swap_halves_body(tr, n_q, n_i, pc_ref, tile_ref, got_ref, send_sem, recv_sem):
    del pc_ref
    q, i = pl.program_id(0), pl.program_id(1)
    rows = pl.ds(pl.multiple_of(i * tr, tr), tr)
    _send_tile_to_sibling(tile_ref, got_ref.at[pl.ds(q, 1), :, rows], got_ref, send_sem, recv_sem,
                          jnp.logical_and(q == n_q - 1, i == n_i - 1))


def _swap_halves(g4, pc, name):
    n, _, h, C = g4.shape
    tr = _pick(h, 512, 16)
    return pl.pallas_call(
        functools.partial(_swap_halves_body, tr, n, h // tr),
        name=name,
        grid_spec=pltpu.PrefetchScalarGridSpec(
            num_scalar_prefetch=1,
            grid=(n, h // tr),
            in_specs=[pl.BlockSpec((1, 1, tr, C), lambda q, i, pc: (q, 1 - pc[1], i, 0))],
            out_specs=HBM,
            scratch_shapes=[pltpu.SemaphoreType.DMA(()), pltpu.SemaphoreType.DMA(())],
        ),
        out_shape=jax.ShapeDtypeStruct((n, 1, h, C), g4.dtype),
        compiler_params=_params(("arbitrary", "arbitrary")),
    )(pc, g4).reshape(n, h, C)


def _send_rows_body(tr, n_i, tile_ref, got_ref, send_sem, recv_sem):
    i = pl.program_id(0)
    rows = pl.ds(pl.multiple_of(i * tr, tr), tr)
    _send_tile_to_sibling(tile_ref, got_ref.at[rows], got_ref, send_sem, recv_sem, i == n_i - 1)


def _send_rows(r, name):
    h, C = r.shape
    tr = _pick(h, 256, 8)
    return pl.pallas_call(
        functools.partial(_send_rows_body, tr, h // tr),
        name=name,
        grid=(h // tr,),
        in_specs=[pl.BlockSpec((tr, C), lambda i: (i, 0))],
        out_specs=HBM,
        out_shape=jax.ShapeDtypeStruct((h, C), r.dtype),
        scratch_shapes=[pltpu.SemaphoreType.DMA(()), pltpu.SemaphoreType.DMA(())],
        compiler_params=_params(("arbitrary",)),
    )(r)


def _ici_copy(src, dst, send_sems, recv_sems, j, chip, c):
    return pltpu.make_async_remote_copy(src_ref=src, dst_ref=dst, send_sem=send_sems.at[j], recv_sem=recv_sems.at[j],
                                        device_id=(chip[0], chip[1], c), device_id_type=MESH)


def _token_spec():
    return jax.ShapeDtypeStruct((8, BLK), F32), pl.BlockSpec(memory_space=pltpu.VMEM)


def _ag_start_body(buf_ref, after_ref, send_sems, recv_sems, buf_thru, token_ref):
    del after_ref, buf_thru
    x, y, c, chips = _me()
    mine = buf_ref.at[2 * x + y, c]
    for j, chip in enumerate(chips):
        _ici_copy(mine, mine, send_sems, recv_sems, j, chip, c).start()
    token_ref[...] = jnp.zeros_like(token_ref)


def _ag_start(buf, after, name):
    tok_shape, tok_spec = _token_spec()
    sems = pltpu.SemaphoreType.DMA((N_CHIPS - 1,))
    return pl.pallas_call(
        functools.partial(_ag_start_body),
        name=name,
        in_specs=[HBM, ANY],
        out_specs=[SEM, SEM, HBM, tok_spec],
        out_shape=[sems, sems, pltpu.HBM(buf.shape, buf.dtype), tok_shape],
        input_output_aliases={0: 2},
        compiler_params=pltpu.CompilerParams(has_side_effects=EFFECT),
    )(pltpu.with_memory_space_constraint(buf, pltpu.HBM), after)


def _ag_wait_body(buf_ref, send_sems, recv_sems, after_ref, buf_out):
    del after_ref, buf_out
    x, y, c, chips = _me()
    mine = buf_ref.at[2 * x + y, c]
    for j, chip in enumerate(chips):
        theirs = buf_ref.at[2 * chip[0] + chip[1], c]
        _ici_copy(mine, mine, send_sems, recv_sems, j, chip, c).wait_send()
        _ici_copy(theirs, theirs, send_sems, recv_sems, j, chip, c).wait_recv()


def _ag_wait(buf, send_sems, recv_sems, after, name):
    return pl.pallas_call(
        functools.partial(_ag_wait_body),
        name=name,
        in_specs=[HBM, SEM, SEM, ANY],
        out_specs=HBM,
        out_shape=pltpu.HBM(buf.shape, buf.dtype),
        input_output_aliases={0: 0},
        compiler_params=pltpu.CompilerParams(has_side_effects=EFFECT),
    )(buf, send_sems, recv_sems, after)


def _rs_start_body(pair_ref, land_ref, after_ref, send_sems, recv_sems, pair_thru, land_thru, token_ref):
    del after_ref, pair_thru, land_thru
    x, y, c, chips = _me()
    for j, chip in enumerate(chips):
        _ici_copy(pair_ref.at[2 * chip[0] + chip[1]], land_ref.at[j], send_sems, recv_sems, j, chip, c).start()
    token_ref[...] = jnp.zeros_like(token_ref)


def _rs_start(pair, after, name):
    n, h, C = pair.shape
    tok_shape, tok_spec = _token_spec()
    sems = pltpu.SemaphoreType.DMA((N_CHIPS - 1,))
    land = pltpu.with_memory_space_constraint(lax.empty((N_CHIPS - 1, h, C), pair.dtype), pltpu.HBM)
    return pl.pallas_call(
        functools.partial(_rs_start_body),
        name=name,
        in_specs=[HBM, HBM, ANY],
        out_specs=[SEM, SEM, HBM, HBM, tok_spec],
        out_shape=[sems, sems, pltpu.HBM(pair.shape, pair.dtype), pltpu.HBM(land.shape, land.dtype), tok_shape],
        input_output_aliases={0: 2, 1: 3},
        compiler_params=pltpu.CompilerParams(has_side_effects=EFFECT),
    )(pltpu.with_memory_space_constraint(pair, pltpu.HBM), land, after)


def _rs_wait_body(pair_ref, land_ref, send_sems, recv_sems, after_ref, pair_out, land_out):
    del after_ref, pair_out, land_out
    x, y, c, chips = _me()
    for j, chip in enumerate(chips):
        _ici_copy(pair_ref.at[0], land_ref.at[j], send_sems, recv_sems, j, chip, c).wait_send()
        _ici_copy(pair_ref.at[0], land_ref.at[j], send_sems, recv_sems, j, chip, c).wait_recv()


def _rs_wait(pair, land, send_sems, recv_sems, after, name):
    return pl.pallas_call(
        functools.partial(_rs_wait_body),
        name=name,
        in_specs=[HBM, HBM, SEM, SEM, ANY],
        out_specs=[HBM, HBM],
        out_shape=[pltpu.HBM(pair.shape, pair.dtype), pltpu.HBM(land.shape, land.dtype)],
        input_output_aliases={0: 0, 1: 1},
        compiler_params=pltpu.CompilerParams(has_side_effects=EFFECT),
    )(pair, land, send_sems, recv_sems, after)


def _add_chips_body(pc_ref, own_ref, l0_ref, l1_ref, l2_ref, o_ref):
    del pc_ref
    r = own_ref[...].astype(F32) + l0_ref[...].astype(F32)
    o_ref[...] = r + l1_ref[...].astype(F32) + l2_ref[...].astype(F32)


def _add_chips(pair, land, pc, name):
    _, h, C = pair.shape
    tr = _pick(h, 512, 16)
    slot = lambda j: pl.BlockSpec((None, tr, C), lambda i, pc: (j, i, 0))
    return pl.pallas_call(
        functools.partial(_add_chips_body),
        name=name,
        grid_spec=pltpu.PrefetchScalarGridSpec(
            num_scalar_prefetch=1,
            grid=(h // tr,),
            in_specs=[pl.BlockSpec((None, tr, C), lambda i, pc: (pc[0], i, 0)), slot(0), slot(1), slot(2)],
            out_specs=pl.BlockSpec((tr, C), lambda i, pc: (i, 0)),
        ),
        out_shape=jax.ShapeDtypeStruct((h, C), F32),
        compiler_params=_params(("parallel",)),
    )(pc, pair, land, land, land)


def _ar_small_body(x_ref, o_ref, buf_ref, send_sems, recv_sems):
    x, y, c, _ = _me()
    me = 4 * x + 2 * y + c
    sends = []
    for r in range(1, N_DEV):
        fx, fy, fc = (r >> 2) & 1, (r >> 1) & 1, r & 1
        peer = (x ^ fx, y ^ fy, c ^ fc)
        cp = pltpu.make_async_remote_copy(
            src_ref=x_ref, dst_ref=buf_ref.at[me], send_sem=send_sems.at[r - 1], recv_sem=recv_sems.at[r - 1],
            device_id=peer, device_id_type=MESH)
        cp.start()
        sends.append(cp)
    buf_ref[me] = x_ref[...]
    for r in range(1, N_DEV):
        fx, fy, fc = (r >> 2) & 1, (r >> 1) & 1, r & 1
        src = 4 * (x ^ fx) + 2 * (y ^ fy) + (c ^ fc)
        pltpu.make_async_remote_copy(
            src_ref=x_ref, dst_ref=buf_ref.at[src], send_sem=send_sems.at[r - 1], recv_sem=recv_sems.at[r - 1],
            device_id=(x ^ fx, y ^ fy, c ^ fc), device_id_type=MESH).wait_recv()
    for cp in sends:
        cp.wait_send()
    acc = buf_ref[0]
    for k in range(1, N_DEV):
        acc = acc + buf_ref[k]
    o_ref[...] = acc


def _ar_small(packed):
    R, C = packed.shape
    return pl.pallas_call(
        functools.partial(_ar_small_body),
        name="ar_small",
        in_specs=[pl.BlockSpec(memory_space=pltpu.VMEM)],
        out_specs=pl.BlockSpec(memory_space=pltpu.VMEM),
        out_shape=jax.ShapeDtypeStruct((R, C), F32),
        scratch_shapes=[pltpu.VMEM((N_DEV, R, C), F32), pltpu.SemaphoreType.DMA((N_DEV - 1,)),
                        pltpu.SemaphoreType.DMA((N_DEV - 1,))],
        compiler_params=pltpu.CompilerParams(vmem_limit_bytes=VMEM_LIMIT),
    )(packed)


def _pack(arrays):
    rows = []
    for a in arrays:
        flat = a.reshape(-1).astype(F32)
        pad = (-flat.shape[0]) % BLK
        rows.append(jnp.pad(flat, (0, pad)).reshape(-1, BLK))
    packed = jnp.concatenate(rows, axis=0)
    pad = (-packed.shape[0]) % 8
    return jnp.pad(packed, ((0, pad), (0, 0)))


def _unpack(packed, shapes):
    out, r = [], 0
    for s in shapes:
        n = 1
        for k in s:
            n *= k
        nr = -(-n // BLK)
        out.append(packed[r:r + nr].reshape(-1)[:n].reshape(s))
        r += nr
    return out


def kernel(x, norm1_g, w_in, q_norm_g, k_norm_g, attn_sinks, gate_ln_g, gate_ln_b, w_spatial, b_spatial, out_norm_attn_g, out_norm_gate_g, w_out, norm2_g, w_ffn_gate, w_ffn_up, w_ffn_down, loss_target, m_norm1_g, m_w_in, m_q_norm_g, m_k_norm_g, m_attn_sinks, m_gate_ln_g, m_gate_ln_b, m_w_spatial, m_b_spatial, m_out_norm_attn_g, m_out_norm_gate_g, m_w_out, m_norm2_g, m_w_ffn_gate, m_w_ffn_up, m_w_ffn_down, v_norm1_g, v_w_in, v_q_norm_g, v_k_norm_g, v_attn_sinks, v_gate_ln_g, v_gate_ln_b, v_w_spatial, v_b_spatial, v_out_norm_attn_g, v_out_norm_gate_g, v_w_out, v_norm2_g, v_w_ffn_gate, v_w_ffn_up, v_w_ffn_down):
    bl, seq, D = x.shape
    T = bl * seq
    attn_w, gate_w = out_norm_attn_g.shape[1], out_norm_gate_g.shape[1]
    d = _Dims(seq, attn_w, gate_w)
    G = d.n_groups
    in_w = d.in_w
    dff = w_ffn_gate.shape[2] * N_CHIPS
    assert w_in.shape[2] * N_CHIPS == in_w and seq % BLK == 0 and attn_w % (2 * BLK) == 0

    pc = jnp.stack([2 * lax.axis_index("x") + lax.axis_index("y"), lax.axis_index("c")]).astype(jnp.int32)
    big = [w_in[0], w_out[0], w_ffn_gate[0], w_ffn_up[0], w_ffn_down[0]]
    names = ["in", "out", "gate", "up", "down"]
    started, behind = [], norm1_g
    for w, n in zip(big, names):
        send, recv, buf, behind = _ag_start(_cast_into(w, pc, "cast_" + n), behind, "ag_start_" + n)
        started.append((send, recv, buf))

    def gathered(k, after):
        send, recv, buf = started[k]
        buf = _ag_wait(buf, send, recv, after, "ag_wait_" + names[k])
        return _ag_pair(buf, pc, "ag_pair_" + names[k]).reshape((N_CHIPS,) + big[k].shape)

    qg2 = jnp.tile(q_norm_g, (1, 2))
    kg2 = jnp.tile(k_norm_g, (1, 2))
    lg, lb, wsp = gate_ln_g[0], gate_ln_b[0], w_spatial[0]
    bcol = jnp.broadcast_to(b_spatial[0][:, :, None], (G, BLK, BLK))

    xf = x.reshape(T, D)
    tgt = loss_target.reshape(T, D)
    h1 = _rms_fwd(xf, norm1_g, "norm1_fwd", after=behind)
    win_full = jnp.transpose(gathered(0, h1), (1, 0, 2)).reshape(D, in_w)
    proj = _matmul(h1, win_full, "nn", F32, "proj_fwd", tm=1024, tn=1664)
    ya, yg, yn = _mixer_fwd(d, proj, attn_sinks, qg2, kg2, lg, lb, wsp, bcol, out_norm_attn_g, out_norm_gate_g)
    wout_full = gathered(1, yn).reshape(attn_w + gate_w, D)
    x1 = _matmul(yn, wout_full, "nn", F32, "out_fwd", tm=1024, tn=1024, add=xf)
    h2 = _rms_fwd(x1, norm2_g, "norm2_fwd")
    wg_g, wu_g = gathered(2, h2), gathered(3, h2)
    a, b, f = _ffn_up(h2, wg_g, wu_g)
    wd_full = gathered(4, f).reshape(dff, D)
    x2 = _matmul(f, wd_full, "nn", F32, "ffn_down_fwd", tm=1024, tn=1024, tk=1408, add=x1)
    dx2, dx2b, loss_local = _loss(x2, tgt)

    def reduce_start(g, n):
        g4 = g.reshape(N_CHIPS, 2, g.shape[1] // 2, g.shape[2])
        pair = _add_pair(g4, _swap_halves(g4, pc, "rs_swap_" + n), pc, "rs_add_pair_" + n)
        return _rs_start(pair, g, "rs_start_" + n)

    reducing = {}
    g_d = _matmul(f, dx2b, "tn", BF16, "ffn_down_dw", tm=1408, tn=1024, tk=2048, out_slab="r")
    reducing["down"] = reduce_start(g_d, "down")
    da, db = _ffn_down_dx(dx2b, wd_full, a, b, reducing["down"][4])
    g_g = _matmul(h2, da, "tn", BF16, "ffn_gate_dw", tm=1024, tn=1408, tk=2048, out_slab="c")
    reducing["gate"] = reduce_start(g_g, "gate")
    g_u = _matmul(h2, db, "tn", BF16, "ffn_up_dw", tm=1024, tn=1408, tk=2048, out_slab="c",
                  after=reducing["gate"][4])
    reducing["up"] = reduce_start(g_u, "up")
    dh2 = _matmul(da, wg_g, "nt", F32, "ffn_gate_dx", tm=1024, tn=1024, tk=1408, b_slab="k",
                  after=reducing["up"][4])
    dh2 = _matmul(db, wu_g, "nt", F32, "ffn_up_dx", tm=1024, tn=1024, tk=1408, b_slab="k", add=dh2)
    dx1, dx1b, dg_norm2 = _rms_bwd(x1, norm2_g, dh2, dx2, "norm2_bwd", True)
    g_o = _matmul(yn, dx1b, "tn", BF16, "out_dw", tm=512, tn=1024, tk=2048, out_slab="r")
    reducing["out"] = reduce_start(g_o, "out")
    dy = _matmul(dx1b, wout_full, "nt", F32, "out_dx", tm=1024, tn=1024, after=reducing["out"][4])
    (dproj, dkv, dqg, dkg, dsk, dlg, dlb, dwsp, dbsp, dgoa, dgog) = _mixer_bwd(
        d, proj, ya, yg, dy, attn_sinks, qg2, kg2, lg, lb, wsp, bcol, out_norm_attn_g, out_norm_gate_g)
    dproj = _put_kv(d, dproj, dkv)
    g_in_full = _matmul(h1, dproj, "tn", BF16, "proj_dw", tm=1024, tn=1664, tk=2048)
    g_i = jnp.transpose(g_in_full.reshape(D, N_CHIPS, in_w // N_CHIPS), (1, 0, 2))
    reducing["in"] = reduce_start(g_i, "in")
    dh1 = _matmul(dproj, win_full, "nt", F32, "proj_dx", tm=1024, tn=1024, after=reducing["in"][4])
    dx, dg_norm1 = _rms_bwd(xf, norm1_g, dh1, dx1, "norm1_bwd", False)

    big_m = [m_w_in[0], m_w_out[0], m_w_ffn_gate[0], m_w_ffn_up[0], m_w_ffn_down[0]]
    big_v = [v_w_in[0], v_w_out[0], v_w_ffn_gate[0], v_w_ffn_up[0], v_w_ffn_down[0]]
    big_grads, big_d, big_nm, big_nv = [], [], [], []
    for w, m, v, n in zip(big, big_m, big_v, names):
        send, recv, pair, land, _ = reducing[n]
        pair, land = _rs_wait(pair, land, send, recv, dx, "rs_wait_" + n)
        own = _add_chips(pair, land, pc, "rs_add_chips_" + n)
        outs = _adamw_halves(w, own, _send_rows(own, "rs_send_" + n), m, v, pc, "adamw_" + n)
        for lst, o in zip((big_grads, big_d, big_nm, big_nv), outs):
            lst.append(o.reshape(w.shape))

    small_names_w = [norm1_g, q_norm_g, k_norm_g, attn_sinks, gate_ln_g, gate_ln_b, w_spatial, b_spatial,
                     out_norm_attn_g, out_norm_gate_g, norm2_g]
    small_m = [m_norm1_g, m_q_norm_g, m_k_norm_g, m_attn_sinks, m_gate_ln_g, m_gate_ln_b, m_w_spatial, m_b_spatial,
               m_out_norm_attn_g, m_out_norm_gate_g, m_norm2_g]
    small_v = [v_norm1_g, v_q_norm_g, v_k_norm_g, v_attn_sinks, v_gate_ln_g, v_gate_ln_b, v_w_spatial, v_b_spatial,
               v_out_norm_attn_g, v_out_norm_gate_g, v_norm2_g]
    dqg64 = dqg[:, :HEAD_DIM] + dqg[:, HEAD_DIM:]
    dkg64 = dkg[:, :HEAD_DIM] + dkg[:, HEAD_DIM:]
    small_g_local = [dg_norm1, dqg64, dkg64, dsk[:, :d.n_heads], dlg, dlb, dwsp, dbsp, dgoa, dgog, dg_norm2]
    shapes = [w.shape for w in small_names_w]
    sg = _ar_small(_pack(small_g_local))
    sd, snm, snv = _adamw(_pack(small_names_w), sg, _pack(small_m), _pack(small_v), "adamw_small")
    small_g, small_d, small_nm, small_nv = (_unpack(t, shapes) for t in (sg, sd, snm, snv))

    loss = lax.psum(loss_local[0, 0], ("x", "y", "c"))

    def order(small, bigs):
        s = list(small)
        bg = [t[None] for t in bigs]
        return [s[0], bg[0], s[1], s[2], s[3], s[4], s[5], s[6], s[7], s[8], s[9], bg[1], s[10], bg[2], bg[3], bg[4]]

    grad_x = dx.reshape(bl, seq, D)
    return (loss, grad_x, *order(small_g, big_grads), *order(small_d, big_d), *order(small_nm, big_nm),
            *order(small_nv, big_nv))
```

```python
import functools

import jax
import jax.numpy as jnp
from jax import lax
from jax.experimental import pallas as pl
from jax.experimental.pallas import tpu as pltpu

F32 = jnp.float32
BF16 = jnp.bfloat16
MESH = pl.DeviceIdType.MESH

EPS = 1e-6
HEAD_DIM = 64
N_KV_HEADS = 2
BLK = 128
N_CHIPS = 4
N_DEV = 8
NEG = -1e30

ADAM_LR = 0.001
ADAM_B1 = 0.9
ADAM_B2 = 0.999
ADAM_EPS = 1e-08
ADAM_WD = 0.01
ADAM_STEP = 10

VMEM_LIMIT = 56 * 1024 * 1024

NN = (((1,), (0,)), ((), ()))
NT = (((1,), (1,)), ((), ()))
TN = (((0,), (0,)), ((), ()))
HBM = pl.BlockSpec(memory_space=pltpu.HBM)
ANY = pl.BlockSpec(memory_space=pl.ANY)
SEM = pl.BlockSpec(memory_space=pltpu.SEMAPHORE)
EFFECT = pltpu.SideEffectType.DATAFLOW_SIDE_EFFECTING


def _dot(a, b, dn):
    return lax.dot_general(a, b, dn, preferred_element_type=F32)


def _pick(dim, pref, align=128):
    if dim <= pref:
        return dim
    t = (pref // align) * align
    while t >= align:
        if dim % t == 0:
            return t
        t -= align
    return dim


def _params(sem):
    return pltpu.CompilerParams(dimension_semantics=sem, vmem_limit_bytes=VMEM_LIMIT)


MM_CHUNK = 512


def _col_chunks(tn):
    return [slice(c0, min(c0 + MM_CHUNK, tn)) for c0 in range(0, tn, MM_CHUNK)]


def _mm_body(dn, nk, has_add, has_after, *refs):
    a_ref, b_ref = refs[:2]
    add_ref = refs[2] if has_add else None
    o_ref = refs[2 + has_add + has_after]
    chunks = _col_chunks(o_ref.shape[-1])

    def dot(cols):
        return _dot(a_ref[...], b_ref[cols, :] if dn == NT else b_ref[:, cols], dn)

    def finish(cols, r):
        if add_ref is not None:
            r = r + add_ref[:, cols]
        o_ref[:, cols] = r.astype(o_ref.dtype)

    if nk == 1:
        for cols in chunks:
            finish(cols, dot(cols))
        return
    acc_ref = refs[-1]
    k = pl.program_id(2)

    @pl.when(k == 0)
    def _():
        for cols in chunks:
            acc_ref[:, cols] = dot(cols)

    if nk > 2:
        @pl.when(jnp.logical_and(k > 0, k < nk - 1))
        def _():
            for cols in chunks:
                acc_ref[:, cols] += dot(cols)

    @pl.when(k == nk - 1)
    def _():
        for cols in chunks:
            finish(cols, acc_ref[:, cols] + dot(cols))


def _matmul(a, b, mode, out_dtype, name, *, tm, tn, tk=None, add=None, b_slab=None, out_slab=None, after=None):
    if mode == "nn":
        M, K = a.shape
        N = b.shape[0] * b.shape[2] if b_slab == "c" else b.shape[1]
    elif mode == "nt":
        M, K = a.shape
        N = b.shape[1] if b_slab == "k" else b.shape[0]
    else:
        K, M = a.shape
        N = b.shape[1]
    tk = K if tk is None else tk
    tm, tn, tk = _pick(M, tm), _pick(N, tn), _pick(K, tk)
    if b_slab == "c":
        tn = _pick(b.shape[2], tn)
    if b_slab == "k":
        tk = _pick(b.shape[2], tk)
    if out_slab == "c":
        tn = _pick(N // N_CHIPS, tn)
    if out_slab == "r":
        tm = _pick(M // N_CHIPS, tm)
    gm, gn, gk = M // tm, N // tn, K // tk

    if mode == "tn":
        a_spec = pl.BlockSpec((tk, tm), lambda j, i, k: (k, i))
        b_spec = pl.BlockSpec((tk, tn), lambda j, i, k: (k, j))
    else:
        a_spec = pl.BlockSpec((tm, tk), lambda j, i, k: (i, k))
        if b_slab == "c":
            per = b.shape[2] // tn
            b_spec = pl.BlockSpec((None, tk, tn), lambda j, i, k: (j // per, k, j % per))
        elif b_slab == "k":
            per = b.shape[2] // tk
            b_spec = pl.BlockSpec((None, tn, tk), lambda j, i, k: (k // per, j, k % per))
        elif mode == "nn":
            b_spec = pl.BlockSpec((tk, tn), lambda j, i, k: (k, j))
        else:
            b_spec = pl.BlockSpec((tn, tk), lambda j, i, k: (j, k))

    if out_slab == "c":
        per = (N // N_CHIPS) // tn
        o_spec = pl.BlockSpec((None, tm, tn), lambda j, i, k: (j // per, i, j % per))
        o_shape = jax.ShapeDtypeStruct((N_CHIPS, M, N // N_CHIPS), out_dtype)
    elif out_slab == "r":
        per = (M // N_CHIPS) // tm
        o_spec = pl.BlockSpec((None, tm, tn), lambda j, i, k: (i // per, i % per, j))
        o_shape = jax.ShapeDtypeStruct((N_CHIPS, M // N_CHIPS, N), out_dtype)
    else:
        o_spec = pl.BlockSpec((tm, tn), lambda j, i, k: (i, j))
        o_shape = jax.ShapeDtypeStruct((M, N), out_dtype)

    dn = {"nn": NN, "nt": NT, "tn": TN}[mode]
    in_specs = [a_spec, b_spec]
    args = [a, b]
    if add is not None:
        in_specs.append(pl.BlockSpec((tm, tn), lambda j, i, k: (i, j)))
        args.append(add)
    if after is not None:
        in_specs.append(ANY)
        args.append(after)
    return pl.pallas_call(
        functools.partial(_mm_body, dn, gk, add is not None, after is not None),
        name=name,
        grid=(gn, gm, gk),
        in_specs=in_specs,
        out_specs=o_spec,
        out_shape=o_shape,
        scratch_shapes=[pltpu.VMEM((tm, tn), F32)] if gk > 1 else [],
        compiler_params=_params(("parallel", "parallel", "arbitrary")),
    )(*args)


def _rms_fwd_body(x_ref, g_ref, *rest):
    h_ref = rest[-1]
    x = x_ref[...]
    r = lax.rsqrt(jnp.mean(x * x, axis=-1, keepdims=True) + EPS)
    h_ref[...] = (x * r * g_ref[...]).astype(h_ref.dtype)


def _rms_fwd(x, g, name, after=None):
    T, D = x.shape
    tr = _pick(T, 256, 16)
    extra = [] if after is None else [after]
    return pl.pallas_call(
        functools.partial(_rms_fwd_body),
        name=name,
        grid=(T // tr,),
        in_specs=[pl.BlockSpec((tr, D), lambda i: (i, 0)), pl.BlockSpec((1, D), lambda i: (0, 0))] + [ANY] * len(extra),
        out_specs=pl.BlockSpec((tr, D), lambda i: (i, 0)),
        out_shape=jax.ShapeDtypeStruct((T, D), BF16),
        compiler_params=_params(("parallel",)),
    )(x, g, *extra)


def _rms_bwd_body(with_bf16, x_ref, g_ref, dh_ref, res_ref, dx_ref, *rest):
    dg_ref = rest[-1]

    @pl.when(pl.program_id(0) == 0)
    def _():
        dg_ref[...] = jnp.zeros_like(dg_ref)

    x = x_ref[...]
    r = lax.rsqrt(jnp.mean(x * x, axis=-1, keepdims=True) + EPS)
    xh = x * r
    dh = dh_ref[...]
    dg_ref[...] += jnp.sum(dh * xh, axis=0, keepdims=True)
    t = dh * g_ref[...]
    dx = res_ref[...] + r * (t - xh * jnp.mean(t * xh, axis=-1, keepdims=True))
    dx_ref[...] = dx
    if with_bf16:
        rest[0][...] = dx.astype(BF16)


def _rms_bwd(x, g, dh, res, name, with_bf16):
    T, D = x.shape
    tr = _pick(T, 256, 16)
    row = pl.BlockSpec((tr, D), lambda i: (i, 0))
    vec = pl.BlockSpec((1, D), lambda i: (0, 0))
    extra = [jax.ShapeDtypeStruct((T, D), BF16)] if with_bf16 else []
    return pl.pallas_call(
        functools.partial(_rms_bwd_body, with_bf16),
        name=name,
        grid=(T // tr,),
        in_specs=[row, vec, row, row],
        out_specs=[row] + [row] * len(extra) + [vec],
        out_shape=[jax.ShapeDtypeStruct((T, D), F32)] + extra + [jax.ShapeDtypeStruct((1, D), F32)],
        compiler_params=_params(("arbitrary",)),
    )(x, g, dh, res)


def _ffn_up_body(h_ref, wg_ref, wu_ref, a_ref, b_ref, f_ref):
    for cols in _col_chunks(a_ref.shape[-1]):
        a = _dot(h_ref[...], wg_ref[:, cols], NN)
        b = _dot(h_ref[...], wu_ref[:, cols], NN)
        a_ref[:, cols] = a
        b_ref[:, cols] = b
        f_ref[:, cols] = (a * (1.0 / (1.0 + jnp.exp(-a))) * b).astype(f_ref.dtype)


def _ffn_up(h, wg, wu):
    T, D = h.shape
    n, _, fs = wg.shape
    tm, tn = _pick(T, 512), fs
    hs = pl.BlockSpec((tm, D), lambda j, i: (i, 0))
    ws = pl.BlockSpec((None, D, tn), lambda j, i: (j, 0, 0))
    os = pl.BlockSpec((tm, tn), lambda j, i: (i, j))
    return pl.pallas_call(
        functools.partial(_ffn_up_body),
        name="ffn_up_fwd",
        grid=(n, T // tm),
        in_specs=[hs, ws, ws],
        out_specs=[os, os, os],
        out_shape=[jax.ShapeDtypeStruct((T, n * fs), F32), jax.ShapeDtypeStruct((T, n * fs), F32),
                   jax.ShapeDtypeStruct((T, n * fs), BF16)],
        compiler_params=_params(("parallel", "parallel")),
    )(h, wg, wu)


def _ffn_down_dx_body(dx_ref, wd_ref, a_ref, b_ref, after_ref, da_ref, db_ref):
    del after_ref
    for cols in _col_chunks(da_ref.shape[-1]):
        df = _dot(dx_ref[...], wd_ref[cols, :], NT)
        a = a_ref[:, cols]
        s = 1.0 / (1.0 + jnp.exp(-a))
        da_ref[:, cols] = (df * b_ref[:, cols] * (s * (1.0 + a * (1.0 - s)))).astype(da_ref.dtype)
        db_ref[:, cols] = (df * (a * s)).astype(db_ref.dtype)


def _ffn_down_dx(dx2b, wd, a, b, after):
    T, D = dx2b.shape
    F = wd.shape[0]
    tm, tn = _pick(T, 512), _pick(F, 1408)
    xs = pl.BlockSpec((tm, D), lambda j, i: (i, 0))
    ws = pl.BlockSpec((tn, D), lambda j, i: (j, 0))
    os = pl.BlockSpec((tm, tn), lambda j, i: (i, j))
    return pl.pallas_call(
        functools.partial(_ffn_down_dx_body),
        name="ffn_down_dx",
        grid=(F // tn, T // tm),
        in_specs=[xs, ws, os, os, ANY],
        out_specs=[os, os],
        out_shape=[jax.ShapeDtypeStruct((T, F), BF16), jax.ShapeDtypeStruct((T, F), BF16)],
        compiler_params=_params(("parallel", "parallel")),
    )(dx2b, wd, a, b, after)


def _loss_body(inv_d, x2_ref, tgt_ref, dx2_ref, dx2b_ref, loss_ref):
    @pl.when(pl.program_id(0) == 0)
    def _():
        loss_ref[...] = jnp.zeros_like(loss_ref)

    e = x2_ref[...] - tgt_ref[...]
    dx2 = e * inv_d
    dx2_ref[...] = dx2
    dx2b_ref[...] = dx2.astype(BF16)
    row = jnp.sum(e * e, axis=-1, keepdims=True) * inv_d
    loss_ref[...] += 0.5 * jnp.sum(row, axis=0, keepdims=True)


def _loss(x2, tgt):
    T, D = x2.shape
    tr = _pick(T, 256, 16)
    row = pl.BlockSpec((tr, D), lambda i: (i, 0))
    return pl.pallas_call(
        functools.partial(_loss_body, 1.0 / D),
        name="loss_head",
        grid=(T // tr,),
        in_specs=[row, row],
        out_specs=[row, row, pl.BlockSpec((1, 1), lambda i: (0, 0))],
        out_shape=[jax.ShapeDtypeStruct((T, D), F32), jax.ShapeDtypeStruct((T, D), BF16),
                   jax.ShapeDtypeStruct((1, 1), F32)],
        compiler_params=_params(("arbitrary",)),
    )(x2, tgt)


def _lo_mask(shape):
    return lax.broadcasted_iota(jnp.int32, shape, len(shape) - 1) < HEAD_DIM


def _half_sums(t, lo):
    s_lo = jnp.sum(jnp.where(lo, t, 0.0), axis=-1, keepdims=True)
    s_hi = jnp.sum(jnp.where(lo, 0.0, t), axis=-1, keepdims=True)
    return jnp.where(lo, s_lo, s_hi)


def _head_rstd(t, lo):
    return lax.rsqrt(_half_sums(t * t, lo) * (1.0 / HEAD_DIM) + EPS)


def _place(t, lo, kv_head):
    if kv_head == 0:
        t_lo = jnp.where(lo, t, 0.0)
        t_hi = pltpu.roll(t_lo, HEAD_DIM, 1)
    else:
        t_hi = jnp.where(lo, 0.0, t)
        t_lo = pltpu.roll(t_hi, HEAD_DIM, 1)
    return jnp.concatenate([t_lo, t_hi], axis=0).astype(BF16)


def _unplace(c0, c1, lo):
    return jnp.where(lo, c0 + pltpu.roll(c0, HEAD_DIM, 1), c1 + pltpu.roll(c1, HEAD_DIM, 1))


def _band(kv_cur, kv_prev, kg, lo2):
    kb = jnp.concatenate([kv_prev[:, :BLK], kv_cur[:, :BLK]], axis=0)
    vb = jnp.concatenate([kv_prev[:, BLK:], kv_cur[:, BLK:]], axis=0)
    rk = _head_rstd(kb, lo2)
    kn = kb * rk * kg
    kk = [_place(kn, lo2, h) for h in range(N_KV_HEADS)]
    vv = [_place(vb, lo2, h) for h in range(N_KV_HEADS)]
    return kb, rk, kk, vv


def _score_geometry(first_i32):
    qi = lax.broadcasted_iota(jnp.int32, (BLK, 4 * BLK), 0)
    col = lax.broadcasted_iota(jnp.int32, (BLK, 4 * BLK), 1)
    kj = col & (2 * BLK - 1)
    dist = qi + BLK - kj
    valid = (dist >= 0) & (dist < BLK) & (kj >= first_i32 * BLK)
    return col, dist.astype(F32), valid


def _pair_probs(qn, kk, col, distf, valid, slope0, slope1, sink0, sink1):
    s = _dot(qn.astype(BF16), kk, NT) * (HEAD_DIM ** -0.5)
    slope = jnp.where(col < 2 * BLK, slope0, slope1)
    logits = jnp.where(valid, s - slope * distf, NEG)
    probs, psink = [], []
    for hh, sk in ((0, sink0), (1, sink1)):
        l = logits[:, 2 * BLK * hh:2 * BLK * (hh + 1)]
        m = jnp.maximum(jnp.max(l, axis=-1, keepdims=True), sk)
        p = jnp.exp(l - m)
        es = jnp.exp(sk - m)
        inv = 1.0 / (jnp.sum(p, axis=-1, keepdims=True) + es)
        probs.append(p * inv)
        psink.append(es * inv)
    return probs, psink


def _gelu(z):
    return 0.5 * z * (1.0 + lax.erf(z * (0.5 ** 0.5)))


def _gelu_grad(z):
    return 0.5 * (1.0 + lax.erf(z * (0.5 ** 0.5))) + z * jnp.exp(-0.5 * z * z) * ((2.0 * jnp.pi) ** -0.5)


def _tril_w(w):
    r = lax.broadcasted_iota(jnp.int32, (BLK, BLK), 0)
    c = lax.broadcasted_iota(jnp.int32, (BLK, BLK), 1)
    return jnp.where(r >= c, w, 0.0), r >= c


def _gate_fwd_group(zu, zv, lg, lb, w, bcol):
    u = _gelu(zu)
    v = _gelu(zv)
    mu = jnp.mean(v, axis=-1, keepdims=True)
    vc = v - mu
    rs = lax.rsqrt(jnp.mean(vc * vc, axis=-1, keepdims=True) + EPS)
    vh = vc * rs
    vn = vh * lg + lb
    wt, tril = _tril_w(w)
    mixed = _dot(wt.astype(BF16), vn.astype(BF16), NN) + bcol
    return u, vh, rs, vn, wt, tril, mixed


class _Dims:
    def __init__(self, seq, attn_w, gate_w):
        self.seq, self.attn_w, self.gate_w = seq, attn_w, gate_w
        self.n_heads = attn_w // HEAD_DIM
        self.group = self.n_heads // N_KV_HEADS
        self.n_pairs = attn_w // BLK
        self.n_groups = gate_w // BLK
        self.kv_col = attn_w // (2 * BLK)
        self.u0 = attn_w + 2 * BLK
        self.v0 = self.u0 + gate_w
        self.in_w = self.v0 + gate_w
        self.slopes = [2.0 ** (-8.0 * (h + 1) / self.n_heads) for h in range(self.n_heads)]


def _mixer_fwd_body(d, sink_ref, proj_ref, kvp_ref, qg_ref, kg_ref, lg_ref, lb_ref, w_ref, b_ref, goa_ref, gog_ref,
                    ya_ref, yg_ref, y_ref):
    i = pl.program_id(0)
    first = (i % (d.seq // BLK) == 0).astype(jnp.int32)
    lo = _lo_mask((BLK, BLK))
    lo2 = _lo_mask((2 * BLK, BLK))
    kv_cur = proj_ref[:, d.attn_w:d.attn_w + 2 * BLK]
    _, _, kk, vv = _band(kv_cur, kvp_ref[...], kg_ref[...], lo2)
    col, distf, valid = _score_geometry(first)
    qg = qg_ref[...]
    for j in range(d.n_pairs):
        h0, h1 = 2 * j, 2 * j + 1
        kh = h0 // d.group
        q2 = proj_ref[:, BLK * j:BLK * (j + 1)]
        qn = q2 * _head_rstd(q2, lo) * qg
        probs, _ = _pair_probs(qn, kk[kh], col, distf, valid, d.slopes[h0], d.slopes[h1],
                               sink_ref[0, h0], sink_ref[0, h1])
        p = jnp.concatenate(probs, axis=1).astype(BF16)
        ya_ref[:, BLK * j:BLK * (j + 1)] = _dot(p, vv[kh], NN)
    for g in range(d.n_groups):
        zu = proj_ref[:, d.u0 + BLK * g:d.u0 + BLK * (g + 1)]
        zv = proj_ref[:, d.v0 + BLK * g:d.v0 + BLK * (g + 1)]
        u, _, _, _, _, _, mixed = _gate_fwd_group(zu, zv, lg_ref[g:g + 1, :], lb_ref[g:g + 1, :], w_ref[g], b_ref[g])
        yg_ref[:, BLK * g:BLK * (g + 1)] = u * mixed
    ya = ya_ref[...]
    ra = lax.rsqrt(jnp.mean(ya * ya, axis=-1, keepdims=True) + EPS)
    y_ref[:, :d.attn_w] = (ya * ra * goa_ref[...]).astype(y_ref.dtype)
    yg = yg_ref[...]
    rg = lax.rsqrt(jnp.mean(yg * yg, axis=-1, keepdims=True) + EPS)
    y_ref[:, d.attn_w:] = (yg * rg * gog_ref[...]).astype(y_ref.dtype)


def _mixer_specs(d, T):
    row = lambda w: pl.BlockSpec((BLK, w), lambda i: (i, 0))
    const2 = lambda a: pl.BlockSpec(a.shape, lambda i: (0, 0))
    const3 = lambda a: pl.BlockSpec(a.shape, lambda i: (0, 0, 0))
    kv_prev = pl.BlockSpec((BLK, 2 * BLK), lambda i: (jnp.maximum(i - 1, 0), d.kv_col))
    return row, const2, const3, kv_prev


def _mixer_fwd(d, proj, sinks, qg2, kg2, lg, lb, wsp, bcol, goa, gog):
    T = proj.shape[0]
    row, const2, const3, kv_prev = _mixer_specs(d, T)
    return pl.pallas_call(
        functools.partial(_mixer_fwd_body, d),
        name="mixer_fwd",
        grid=(T // BLK,),
        in_specs=[pl.BlockSpec(memory_space=pltpu.SMEM), row(d.in_w), kv_prev, const2(qg2), const2(kg2),
                  const2(lg), const2(lb), const3(wsp), const3(bcol), const2(goa), const2(gog)],
        out_specs=[row(d.attn_w), row(d.gate_w), row(d.attn_w + d.gate_w)],
        out_shape=[jax.ShapeDtypeStruct((T, d.attn_w), F32), jax.ShapeDtypeStruct((T, d.gate_w), F32),
                   jax.ShapeDtypeStruct((T, d.attn_w + d.gate_w), BF16)],
        compiler_params=_params(("parallel",)),
    )(sinks, proj, proj, qg2, kg2, lg, lb, wsp, bcol, goa, gog)


def _mixer_bwd_body(d, sink_ref, proj_ref, kvp_ref, ya_ref, yg_ref, dy_ref, qg_ref, kg_ref, lg_ref, lb_ref, w_ref,
                    b_ref, goa_ref, gog_ref,
                    dproj_ref, dkv_ref, dqg_ref, dkg_ref, dsk_ref, dlg_ref, dlb_ref, dw_ref, db_ref, dgoa_ref,
                    dgog_ref):
    i = pl.program_id(0)

    @pl.when(i == 0)
    def _():
        for r in (dqg_ref, dkg_ref, dsk_ref, dlg_ref, dlb_ref, dw_ref, db_ref, dgoa_ref, dgog_ref):
            r[...] = jnp.zeros_like(r)

    first = (i % (d.seq // BLK) == 0).astype(jnp.int32)
    lo = _lo_mask((BLK, BLK))
    lo2 = _lo_mask((2 * BLK, BLK))
    lane_row = lax.broadcasted_iota(jnp.int32, (1, BLK), 1)

    ya = ya_ref[...]
    ra = lax.rsqrt(jnp.mean(ya * ya, axis=-1, keepdims=True) + EPS)
    yah = ya * ra
    dyn = dy_ref[:, :d.attn_w]
    dgoa_ref[...] += jnp.sum(dyn * yah, axis=0, keepdims=True)
    t = dyn * goa_ref[...]
    dya = ra * (t - yah * jnp.mean(t * yah, axis=-1, keepdims=True))
    yg = yg_ref[...]
    rg = lax.rsqrt(jnp.mean(yg * yg, axis=-1, keepdims=True) + EPS)
    ygh = yg * rg
    dyn = dy_ref[:, d.attn_w:]
    dgog_ref[...] += jnp.sum(dyn * ygh, axis=0, keepdims=True)
    t = dyn * gog_ref[...]
    dyg = rg * (t - ygh * jnp.mean(t * ygh, axis=-1, keepdims=True))

    kv_cur = proj_ref[:, d.attn_w:d.attn_w + 2 * BLK]
    kg = kg_ref[...]
    kb, rk, kk, vv = _band(kv_cur, kvp_ref[...], kg, lo2)
    col, distf, valid = _score_geometry(first)
    qg = qg_ref[...]
    ck = [jnp.zeros((2 * BLK, BLK), F32) for _ in range(N_KV_HEADS)]
    cv = [jnp.zeros((2 * BLK, BLK), F32) for _ in range(N_KV_HEADS)]
    dsk = jnp.zeros((1, BLK), F32)
    dqg = jnp.zeros((1, BLK), F32)
    for j in range(d.n_pairs):
        h0, h1 = 2 * j, 2 * j + 1
        kh = h0 // d.group
        cols = slice(BLK * j, BLK * (j + 1))
        q2 = proj_ref[:, cols]
        rq = _head_rstd(q2, lo)
        qh = q2 * rq
        qn = qh * qg
        probs, psink = _pair_probs(qn, kk[kh], col, distf, valid, d.slopes[h0], d.slopes[h1],
                                   sink_ref[0, h0], sink_ref[0, h1])
        do2 = dya[:, cols]
        prod = do2 * ya[:, cols]
        delta = (jnp.sum(jnp.where(lo, prod, 0.0), axis=-1, keepdims=True),
                 jnp.sum(jnp.where(lo, 0.0, prod), axis=-1, keepdims=True))
        do2b = do2.astype(BF16)
        dp = _dot(do2b, vv[kh], NT)
        ds = []
        for hh in (0, 1):
            ds.append(probs[hh] * (dp[:, 2 * BLK * hh:2 * BLK * (hh + 1)] - delta[hh]))
            dsink = -jnp.sum(psink[hh] * delta[hh], axis=0, keepdims=True)
            dsk = dsk + jnp.where(lane_row == (h0 + hh), dsink, 0.0)
        dsb = (jnp.concatenate(ds, axis=1) * (HEAD_DIM ** -0.5)).astype(BF16)
        pb = jnp.concatenate(probs, axis=1).astype(BF16)
        qnb = qn.astype(BF16)
        dqn = _dot(dsb, kk[kh], NN)
        dkk = _dot(dsb, qnb, TN)
        dvv = _dot(pb, do2b, TN)
        ck[kh] = ck[kh] + jnp.where(lo2, dkk[:2 * BLK], 0.0) + jnp.where(lo2, 0.0, dkk[2 * BLK:])
        cv[kh] = cv[kh] + jnp.where(lo2, dvv[:2 * BLK], 0.0) + jnp.where(lo2, 0.0, dvv[2 * BLK:])
        dqg = dqg + jnp.sum(dqn * qh, axis=0, keepdims=True)
        t = dqn * qg
        dq2 = rq * (t - qh * (_half_sums(t * qh, lo) * (1.0 / HEAD_DIM)))
        dproj_ref[:, cols] = dq2.astype(dproj_ref.dtype)
    dsk_ref[...] += dsk
    dqg_ref[...] += dqg
    dkn = _unplace(ck[0], ck[1], lo2)
    dvb = _unplace(cv[0], cv[1], lo2)
    khat = kb * rk
    dkg_ref[...] += jnp.sum(dkn * khat, axis=0, keepdims=True)
    t = dkn * kg
    dkb = rk * (t - khat * (_half_sums(t * khat, lo2) * (1.0 / HEAD_DIM)))
    rows_cur = pl.ds(pl.multiple_of(i * BLK, BLK), BLK)
    rows_prev = pl.ds(pl.multiple_of(jnp.maximum(i - 1, 0) * BLK, BLK), BLK)
    dkv_ref[rows_cur, :] = jnp.concatenate([dkb[BLK:], dvb[BLK:]], axis=1)
    dkv_ref[rows_prev, :] += jnp.concatenate([dkb[:BLK], dvb[:BLK]], axis=1)
    dproj_ref[:, d.attn_w:d.attn_w + 2 * BLK] = jnp.zeros((BLK, 2 * BLK), dproj_ref.dtype)

    for g in range(d.n_groups):
        ucols = slice(d.u0 + BLK * g, d.u0 + BLK * (g + 1))
        vcols = slice(d.v0 + BLK * g, d.v0 + BLK * (g + 1))
        zu = proj_ref[:, ucols]
        zv = proj_ref[:, vcols]
        lg = lg_ref[g:g + 1, :]
        u, vh, rs, vn, wt, tril, mixed = _gate_fwd_group(zu, zv, lg, lb_ref[g:g + 1, :], w_ref[g], b_ref[g])
        dyg_g = dyg[:, BLK * g:BLK * (g + 1)]
        du = dyg_g * mixed
        dmix = dyg_g * u
        dmb = dmix.astype(BF16)
        db_ref[g:g + 1, :] += jnp.sum(jnp.transpose(dmix), axis=0, keepdims=True)
        dw_ref[g] += jnp.where(tril, _dot(dmb, vn.astype(BF16), NT), 0.0)
        dvn = _dot(wt.astype(BF16), dmb, TN)
        dlg_ref[g:g + 1, :] += jnp.sum(dvn * vh, axis=0, keepdims=True)
        dlb_ref[g:g + 1, :] += jnp.sum(dvn, axis=0, keepdims=True)
        dvh = dvn * lg
        dv = rs * (dvh - jnp.mean(dvh, axis=-1, keepdims=True) - vh * jnp.mean(dvh * vh, axis=-1, keepdims=True))
        dproj_ref[:, ucols] = (du * _gelu_grad(zu)).astype(dproj_ref.dtype)
        dproj_ref[:, vcols] = (dv * _gelu_grad(zv)).astype(dproj_ref.dtype)


def _mixer_bwd(d, proj, ya, yg, dy, sinks, qg2, kg2, lg, lb, wsp, bcol, goa, gog):
    T = proj.shape[0]
    row, const2, const3, kv_prev = _mixer_specs(d, T)
    acc2 = lambda s: pl.BlockSpec(s, lambda i: (0, 0))
    G = d.n_groups
    out_shapes = [((T, d.in_w), BF16), ((T, 2 * BLK), F32), ((1, BLK), F32), ((1, BLK), F32), ((1, BLK), F32),
                  ((G, BLK), F32), ((G, BLK), F32), ((G, BLK, BLK), F32), ((G, BLK), F32),
                  ((1, d.attn_w), F32), ((1, d.gate_w), F32)]
    out_specs = [row(d.in_w)] + [acc2(s) for s, _ in out_shapes[1:7]] + \
                [pl.BlockSpec((G, BLK, BLK), lambda i: (0, 0, 0))] + [acc2(s) for s, _ in out_shapes[8:]]
    return pl.pallas_call(
        functools.partial(_mixer_bwd_body, d),
        name="mixer_bwd",
        grid=(T // BLK,),
        in_specs=[pl.BlockSpec(memory_space=pltpu.SMEM), row(d.in_w), kv_prev, row(d.attn_w), row(d.gate_w),
                  row(d.attn_w + d.gate_w), const2(qg2), const2(kg2), const2(lg), const2(lb), const3(wsp),
                  const3(bcol), const2(goa), const2(gog)],
        out_specs=out_specs,
        out_shape=[jax.ShapeDtypeStruct(s, t) for s, t in out_shapes],
        compiler_params=_params(("arbitrary",)),
    )(sinks, proj, proj, ya, yg, dy, qg2, kg2, lg, lb, wsp, bcol, goa, gog)


def _put_kv_body(dkv_ref, dproj_in_ref, dproj_ref):
    del dproj_in_ref
    dproj_ref[...] = dkv_ref[...].astype(dproj_ref.dtype)


def _put_kv(d, dproj, dkv):
    T = dproj.shape[0]
    tr = _pick(T, 1024, 16)
    return pl.pallas_call(
        functools.partial(_put_kv_body),
        name="put_kv",
        grid=(T // tr,),
        in_specs=[pl.BlockSpec((tr, 2 * BLK), lambda i: (i, 0)), pl.BlockSpec(memory_space=pl.ANY)],
        out_specs=pl.BlockSpec((tr, 2 * BLK), lambda i: (i, d.kv_col)),
        out_shape=jax.ShapeDtypeStruct(dproj.shape, dproj.dtype),
        input_output_aliases={1: 0},
        compiler_params=_params(("parallel",)),
    )(dkv, dproj)


def _add_pair_body(pc_ref, own_ref, got_ref, o_ref):
    del pc_ref
    o_ref[...] = (own_ref[...].astype(F32) + got_ref[...].astype(F32)).astype(o_ref.dtype)


def _add_pair(g4, got, pc, name):
    n, _, h, C = g4.shape
    tr = _pick(h, 512, 16)
    return pl.pallas_call(
        functools.partial(_add_pair_body),
        name=name,
        grid_spec=pltpu.PrefetchScalarGridSpec(
            num_scalar_prefetch=1,
            grid=(n, h // tr),
            in_specs=[pl.BlockSpec((None, None, tr, C), lambda q, i, pc: (q, pc[1], i, 0)),
                      pl.BlockSpec((None, tr, C), lambda q, i, pc: (q, i, 0))],
            out_specs=pl.BlockSpec((None, tr, C), lambda q, i, pc: (q, i, 0)),
        ),
        out_shape=jax.ShapeDtypeStruct((n, h, C), g4.dtype),
        compiler_params=_params(("parallel", "parallel")),
    )(pc, g4, got)


def _adamw_update(w, g, m, v):
    m = ADAM_B1 * m + (1.0 - ADAM_B1) * g
    v = ADAM_B2 * v + (1.0 - ADAM_B2) * (g * g)
    m_hat = m / (1.0 - ADAM_B1 ** ADAM_STEP)
    v_hat = v / (1.0 - ADAM_B2 ** ADAM_STEP)
    return -ADAM_LR * (m_hat / (jnp.sqrt(v_hat) + ADAM_EPS) + ADAM_WD * w), m, v


def _adamw_body(w_ref, g_ref, m_ref, v_ref, d_ref, nm_ref, nv_ref):
    d_ref[...], nm_ref[...], nv_ref[...] = _adamw_update(w_ref[...], g_ref[...], m_ref[...], v_ref[...])


def _adamw(w, g, m, v, name):
    R, C = w.shape
    tr = _pick(R, 512, 8)
    blk = pl.BlockSpec((tr, C), lambda i: (i, 0))
    return pl.pallas_call(
        functools.partial(_adamw_body),
        name=name,
        grid=(R // tr,),
        in_specs=[blk] * 4,
        out_specs=[blk] * 3,
        out_shape=[jax.ShapeDtypeStruct((R, C), F32)] * 3,
        compiler_params=_params(("parallel",)),
    )(w, g, m, v)


def _adamw_halves_body(pc_ref, w_ref, own_ref, got_ref, m_ref, v_ref, g_ref, d_ref, nm_ref, nv_ref):
    mine = pl.program_id(0) == pc_ref[1]

    def update(g):
        g_ref[...] = g
        d_ref[...], nm_ref[...], nv_ref[...] = _adamw_update(w_ref[...], g, m_ref[...], v_ref[...])

    @pl.when(mine)
    def _():
        update(own_ref[...])

    @pl.when(jnp.logical_not(mine))
    def _():
        update(got_ref[...])


def _adamw_halves(w, own, got, m, v, pc, name):
    h, C = own.shape
    tr = _pick(h, 512, 8)
    full = pl.BlockSpec((None, tr, C), lambda hh, i, pc: (hh, i, 0))
    half = pl.BlockSpec((tr, C), lambda hh, i, pc: (i, 0))
    return pl.pallas_call(
        functools.partial(_adamw_halves_body),
        name=name,
        grid_spec=pltpu.PrefetchScalarGridSpec(
            num_scalar_prefetch=1,
            grid=(2, h // tr),
            in_specs=[full, half, half, full, full],
            out_specs=[full] * 4,
        ),
        out_shape=[jax.ShapeDtypeStruct((2, h, C), F32)] * 4,
        compiler_params=_params(("parallel", "parallel")),
    )(pc, w.reshape(2, h, C), own, got, m.reshape(2, h, C), v.reshape(2, h, C))


def _me():
    x, y, c = lax.axis_index("x"), lax.axis_index("y"), lax.axis_index("c")
    chips = [(1 - x, y), (x, 1 - y), (1 - x, 1 - y)]
    return x, y, c, chips


def _cast_into_body(pc_ref, w_ref, o_ref):
    del pc_ref
    o_ref[...] = w_ref[...].astype(o_ref.dtype)


def _cast_into(w, pc, name):
    Rs, C = w.shape
    h = Rs // 2
    tr = _pick(h, 512, 16)
    return pl.pallas_call(
        functools.partial(_cast_into_body),
        name=name,
        grid_spec=pltpu.PrefetchScalarGridSpec(
            num_scalar_prefetch=1,
            grid=(2, h // tr),
            in_specs=[pl.BlockSpec((None, tr, C), lambda hh, i, pc: (hh, i, 0))],
            out_specs=pl.BlockSpec((None, None, tr, C), lambda hh, i, pc: (pc[0], hh, i, 0)),
        ),
        out_shape=jax.ShapeDtypeStruct((N_CHIPS, 2, h, C), BF16),
        compiler_params=_params(("parallel", "parallel")),
    )(pc, w.reshape(2, h, C))


def _send_tile_to_sibling(tile_ref, dst_rows, dst_total, send_sem, recv_sem, last):
    x, y, c, _ = _me()
    cp = pltpu.make_async_remote_copy(src_ref=tile_ref, dst_ref=dst_rows, send_sem=send_sem, recv_sem=recv_sem,
                                      device_id=(x, y, 1 - c), device_id_type=MESH)
    cp.start()
    cp.wait_send()

    @pl.when(last)
    def _():
        pltpu.make_async_remote_copy(src_ref=dst_total, dst_ref=dst_total, send_sem=send_sem, recv_sem=recv_sem,
                                     device_id=(x, y, 1 - c), device_id_type=MESH).wait_recv()


def _ag_pair_body(tr, n_i, pc_ref, tile_ref, buf_ref, send_sem, recv_sem):
    j, i = pl.program_id(0), pl.program_id(1)
    q = pc_ref[0] ^ (j + 1)
    c = pc_ref[1]
    rows = pl.ds(pl.multiple_of(i * tr, tr), tr)
    _send_tile_to_sibling(tile_ref, buf_ref.at[pl.ds(q, 1), pl.ds(c, 1), rows], buf_ref.at[pl.ds(0, N_CHIPS - 1), 0],
                          send_sem, recv_sem, jnp.logical_and(j == N_CHIPS - 2, i == n_i - 1))


def _ag_pair(buf, pc, name):
    _, _, h, C = buf.shape
    tr = _pick(h, 512, 16)
    return pl.pallas_call(
        functools.partial(_ag_pair_body, tr, h // tr),
        name=name,
        grid_spec=pltpu.PrefetchScalarGridSpec(
            num_scalar_prefetch=1,
            grid=(N_CHIPS - 1, h // tr),
            in_specs=[pl.BlockSpec((1, 1, tr, C), lambda j, i, pc: (pc[0] ^ (j + 1), pc[1], i, 0))],
            out_specs=HBM,
            scratch_shapes=[pltpu.SemaphoreType.DMA(()), pltpu.SemaphoreType.DMA(())],
        ),
        out_shape=jax.ShapeDtypeStruct(buf.shape, buf.dtype),
        input_output_aliases={1: 0},
        compiler_params=_params(("arbitrary", "arbitrary")),
    )(pc, buf)


def _swap_halves_body(tr, n_q, n_i, pc_ref, tile_ref, got_ref, send_sem, recv_sem):
    del pc_ref
    q, i = pl.program_id(0), pl.program_id(1)
    rows = pl.ds(pl.multiple_of(i * tr, tr), tr)
    _send_tile_to_sibling(tile_ref, got_ref.at[pl.ds(q, 1), :, rows], got_ref, send_sem, recv_sem,
                          jnp.logical_and(q == n_q - 1, i == n_i - 1))


def _swap_halves(g4, pc, name):
    n, _, h, C = g4.shape
    tr = _pick(h, 512, 16)
    return pl.pallas_call(
        functools.partial(_swap_halves_body, tr, n, h // tr),
        name=name,
        grid_spec=pltpu.PrefetchScalarGridSpec(
            num_scalar_prefetch=1,
            grid=(n, h // tr),
            in_specs=[pl.BlockSpec((1, 1, tr, C), lambda q, i, pc: (q, 1 - pc[1], i, 0))],
            out_specs=HBM,
            scratch_shapes=[pltpu.SemaphoreType.DMA(()), pltpu.SemaphoreType.DMA(())],
        ),
        out_shape=jax.ShapeDtypeStruct((n, 1, h, C), g4.dtype),
        compiler_params=_params(("arbitrary", "arbitrary")),
    )(pc, g4).reshape(n, h, C)


def _send_rows_body(tr, n_i, tile_ref, got_ref, send_sem, recv_sem):
    i = pl.program_id(0)
    rows = pl.ds(pl.multiple_of(i * tr, tr), tr)
    _send_tile_to_sibling(tile_ref, got_ref.at[rows], got_ref, send_sem, recv_sem, i == n_i - 1)


def _send_rows(r, name):
    h, C = r.shape
    tr = _pick(h, 256, 8)
    return pl.pallas_call(
        functools.partial(_send_rows_body, tr, h // tr),
        name=name,
        grid=(h // tr,),
        in_specs=[pl.BlockSpec((tr, C), lambda i: (i, 0))],
        out_specs=HBM,
        out_shape=jax.ShapeDtypeStruct((h, C), r.dtype),
        scratch_shapes=[pltpu.SemaphoreType.DMA(()), pltpu.SemaphoreType.DMA(())],
        compiler_params=_params(("arbitrary",)),
    )(r)


def _ici_copy(src, dst, send_sems, recv_sems, j, chip, c):
    return pltpu.make_async_remote_copy(src_ref=src, dst_ref=dst, send_sem=send_sems.at[j], recv_sem=recv_sems.at[j],
                                        device_id=(chip[0], chip[1], c), device_id_type=MESH)


def _token_spec():
    return jax.ShapeDtypeStruct((8, BLK), F32), pl.BlockSpec(memory_space=pltpu.VMEM)


def _ag_start_body(buf_ref, after_ref, send_sems, recv_sems, buf_thru, token_ref):
    del after_ref, buf_thru
    x, y, c, chips = _me()
    mine = buf_ref.at[2 * x + y, c]
    for j, chip in enumerate(chips):
        _ici_copy(mine, mine, send_sems, recv_sems, j, chip, c).start()
    token_ref[...] = jnp.zeros_like(token_ref)


def _ag_start(buf, after, name):
    tok_shape, tok_spec = _token_spec()
    sems = pltpu.SemaphoreType.DMA((N_CHIPS - 1,))
    return pl.pallas_call(
        functools.partial(_ag_start_body),
        name=name,
        in_specs=[HBM, ANY],
        out_specs=[SEM, SEM, HBM, tok_spec],
        out_shape=[sems, sems, pltpu.HBM(buf.shape, buf.dtype), tok_shape],
        input_output_aliases={0: 2},
        compiler_params=pltpu.CompilerParams(has_side_effects=EFFECT),
    )(pltpu.with_memory_space_constraint(buf, pltpu.HBM), after)


def _ag_wait_body(buf_ref, send_sems, recv_sems, after_ref, buf_out):
    del after_ref, buf_out
    x, y, c, chips = _me()
    mine = buf_ref.at[2 * x + y, c]
    for j, chip in enumerate(chips):
        theirs = buf_ref.at[2 * chip[0] + chip[1], c]
        _ici_copy(mine, mine, send_sems, recv_sems, j, chip, c).wait_send()
        _ici_copy(theirs, theirs, send_sems, recv_sems, j, chip, c).wait_recv()


def _ag_wait(buf, send_sems, recv_sems, after, name):
    return pl.pallas_call(
        functools.partial(_ag_wait_body),
        name=name,
        in_specs=[HBM, SEM, SEM, ANY],
        out_specs=HBM,
        out_shape=pltpu.HBM(buf.shape, buf.dtype),
        input_output_aliases={0: 0},
        compiler_params=pltpu.CompilerParams(has_side_effects=EFFECT),
    )(buf, send_sems, recv_sems, after)


def _rs_start_body(pair_ref, land_ref, after_ref, send_sems, recv_sems, pair_thru, land_thru, token_ref):
    del after_ref, pair_thru, land_thru
    x, y, c, chips = _me()
    for j, chip in enumerate(chips):
        _ici_copy(pair_ref.at[2 * chip[0] + chip[1]], land_ref.at[j], send_sems, recv_sems, j, chip, c).start()
    token_ref[...] = jnp.zeros_like(token_ref)


def _rs_start(pair, after, name):
    n, h, C = pair.shape
    tok_shape, tok_spec = _token_spec()
    sems = pltpu.SemaphoreType.DMA((N_CHIPS - 1,))
    land = pltpu.with_memory_space_constraint(lax.empty((N_CHIPS - 1, h, C), pair.dtype), pltpu.HBM)
    return pl.pallas_call(
        functools.partial(_rs_start_body),
        name=name,
        in_specs=[HBM, HBM, ANY],
        out_specs=[SEM, SEM, HBM, HBM, tok_spec],
        out_shape=[sems, sems, pltpu.HBM(pair.shape, pair.dtype), pltpu.HBM(land.shape, land.dtype), tok_shape],
        input_output_aliases={0: 2, 1: 3},
        compiler_params=pltpu.CompilerParams(has_side_effects=EFFECT),
    )(pltpu.with_memory_space_constraint(pair, pltpu.HBM), land, after)


def _rs_wait_body(pair_ref, land_ref, send_sems, recv_sems, after_ref, pair_out, land_out):
    del after_ref, pair_out, land_out
    x, y, c, chips = _me()
    for j, chip in enumerate(chips):
        _ici_copy(pair_ref.at[0], land_ref.at[j], send_sems, recv_sems, j, chip, c).wait_send()
        _ici_copy(pair_ref.at[0], land_ref.at[j], send_sems, recv_sems, j, chip, c).wait_recv()


def _rs_wait(pair, land, send_sems, recv_sems, after, name):
    return pl.pallas_call(
        functools.partial(_rs_wait_body),
        name=name,
        in_specs=[HBM, HBM, SEM, SEM, ANY],
        out_specs=[HBM, HBM],
        out_shape=[pltpu.HBM(pair.shape, pair.dtype), pltpu.HBM(land.shape, land.dtype)],
        input_output_aliases={0: 0, 1: 1},
        compiler_params=pltpu.CompilerParams(has_side_effects=EFFECT),
    )(pair, land, send_sems, recv_sems, after)


def _add_chips_body(pc_ref, own_ref, l0_ref, l1_ref, l2_ref, o_ref):
    del pc_ref
    r = own_ref[...].astype(F32) + l0_ref[...].astype(F32)
    o_ref[...] = r + l1_ref[...].astype(F32) + l2_ref[...].astype(F32)


def _add_chips(pair, land, pc, name):
    _, h, C = pair.shape
    tr = _pick(h, 512, 16)
    slot = lambda j: pl.BlockSpec((None, tr, C), lambda i, pc: (j, i, 0))
    return pl.pallas_call(
        functools.partial(_add_chips_body),
        name=name,
        grid_spec=pltpu.PrefetchScalarGridSpec(
            num_scalar_prefetch=1,
            grid=(h // tr,),
            in_specs=[pl.BlockSpec((None, tr, C), lambda i, pc: (pc[0], i, 0)), slot(0), slot(1), slot(2)],
            out_specs=pl.BlockSpec((tr, C), lambda i, pc: (i, 0)),
        ),
        out_shape=jax.ShapeDtypeStruct((h, C), F32),
        compiler_params=_params(("parallel",)),
    )(pc, pair, land, land, land)


def _peer(r):
    x, y, c, _ = _me()
    return (x ^ ((r >> 2) & 1), y ^ ((r >> 1) & 1), c ^ (r & 1))


def _ar_start_body(x_ref, land_ref, send_sems, recv_sems, x_thru, land_thru, token_ref):
    del x_thru, land_thru
    for r in range(1, N_DEV):
        pltpu.make_async_remote_copy(src_ref=x_ref, dst_ref=land_ref.at[r - 1], send_sem=send_sems.at[r - 1],
                                     recv_sem=recv_sems.at[r - 1], device_id=_peer(r), device_id_type=MESH).start()
    token_ref[...] = jnp.zeros_like(token_ref)


def _ar_start(packed):
    tok_shape, tok_spec = _token_spec()
    sems = pltpu.SemaphoreType.DMA((N_DEV - 1,))
    land = pltpu.with_memory_space_constraint(lax.empty((N_DEV - 1,) + packed.shape, packed.dtype), pltpu.HBM)
    return pl.pallas_call(
        functools.partial(_ar_start_body),
        name="ar_start",
        in_specs=[HBM, HBM],
        out_specs=[SEM, SEM, HBM, HBM, tok_spec],
        out_shape=[sems, sems, pltpu.HBM(packed.shape, packed.dtype), pltpu.HBM(land.shape, land.dtype), tok_shape],
        input_output_aliases={0: 2, 1: 3},
        compiler_params=pltpu.CompilerParams(has_side_effects=EFFECT),
    )(pltpu.with_memory_space_constraint(packed, pltpu.HBM), land)


def _ar_wait_body(x_ref, land_ref, send_sems, recv_sems, after_ref, x_out, land_out):
    del after_ref, x_out, land_out
    for r in range(1, N_DEV):
        cp = pltpu.make_async_remote_copy(src_ref=x_ref, dst_ref=land_ref.at[r - 1], send_sem=send_sems.at[r - 1],
                                          recv_sem=recv_sems.at[r - 1], device_id=_peer(r), device_id_type=MESH)
        cp.wait_send()
        cp.wait_recv()


def _ar_wait(packed, land, send_sems, recv_sems, after):
    return pl.pallas_call(
        functools.partial(_ar_wait_body),
        name="ar_wait",
        in_specs=[HBM, HBM, SEM, SEM, ANY],
        out_specs=[HBM, HBM],
        out_shape=[pltpu.HBM(packed.shape, packed.dtype), pltpu.HBM(land.shape, land.dtype)],
        input_output_aliases={0: 0, 1: 1},
        compiler_params=pltpu.CompilerParams(has_side_effects=EFFECT),
    )(packed, land, send_sems, recv_sems, after)


def _ar_sum_body(me_ref, own_ref, *rest):
    o_ref = rest[N_DEV]
    acc = None
    for dev in range(N_DEV):
        term = jnp.where(me_ref[0] == dev, own_ref[...], rest[dev][...])
        acc = term if acc is None else acc + term
    o_ref[...] = acc


def _ar_sum(packed, land, me):
    R, C = packed.shape
    tr = _pick(R, 552, 8)
    own = pl.BlockSpec((tr, C), lambda i, me: (i, 0))
    slot = lambda dev: pl.BlockSpec((None, tr, C), lambda i, me: (jnp.maximum((dev ^ me[0]) - 1, 0), i, 0))
    return pl.pallas_call(
        functools.partial(_ar_sum_body),
        name="ar_sum",
        grid_spec=pltpu.PrefetchScalarGridSpec(
            num_scalar_prefetch=1,
            grid=(R // tr,),
            in_specs=[own] + [slot(dev) for dev in range(N_DEV)],
            out_specs=pl.BlockSpec((tr, C), lambda i, me: (i, 0)),
        ),
        out_shape=jax.ShapeDtypeStruct((R, C), F32),
        compiler_params=_params(("parallel",)),
    )(me, packed, *([land] * N_DEV))


def _pack(arrays):
    rows = []
    for a in arrays:
        flat = a.reshape(-1).astype(F32)
        pad = (-flat.shape[0]) % BLK
        rows.append(jnp.pad(flat, (0, pad)).reshape(-1, BLK))
    packed = jnp.concatenate(rows, axis=0)
    pad = (-packed.shape[0]) % 8
    return jnp.pad(packed, ((0, pad), (0, 0)))


def _unpack(packed, shapes):
    out, r = [], 0
    for s in shapes:
        n = 1
        for k in s:
            n *= k
        nr = -(-n // BLK)
        out.append(packed[r:r + nr].reshape(-1)[:n].reshape(s))
        r += nr
    return out


def kernel(x, norm1_g, w_in, q_norm_g, k_norm_g, attn_sinks, gate_ln_g, gate_ln_b, w_spatial, b_spatial, out_norm_attn_g, out_norm_gate_g, w_out, norm2_g, w_ffn_gate, w_ffn_up, w_ffn_down, loss_target, m_norm1_g, m_w_in, m_q_norm_g, m_k_norm_g, m_attn_sinks, m_gate_ln_g, m_gate_ln_b, m_w_spatial, m_b_spatial, m_out_norm_attn_g, m_out_norm_gate_g, m_w_out, m_norm2_g, m_w_ffn_gate, m_w_ffn_up, m_w_ffn_down, v_norm1_g, v_w_in, v_q_norm_g, v_k_norm_g, v_attn_sinks, v_gate_ln_g, v_gate_ln_b, v_w_spatial, v_b_spatial, v_out_norm_attn_g, v_out_norm_gate_g, v_w_out, v_norm2_g, v_w_ffn_gate, v_w_ffn_up, v_w_ffn_down):
    bl, seq, D = x.shape
    T = bl * seq
    attn_w, gate_w = out_norm_attn_g.shape[1], out_norm_gate_g.shape[1]
    d = _Dims(seq, attn_w, gate_w)
    G = d.n_groups
    in_w = d.in_w
    dff = w_ffn_gate.shape[2] * N_CHIPS
    assert w_in.shape[2] * N_CHIPS == in_w and seq % BLK == 0 and attn_w % (2 * BLK) == 0

    pc = jnp.stack([2 * lax.axis_index("x") + lax.axis_index("y"), lax.axis_index("c")]).astype(jnp.int32)
    big = [w_in[0], w_out[0], w_ffn_gate[0], w_ffn_up[0], w_ffn_down[0]]
    names = ["in", "out", "gate", "up", "down"]
    started, behind = [], norm1_g
    for w, n in zip(big, names):
        send, recv, buf, behind = _ag_start(_cast_into(w, pc, "cast_" + n), behind, "ag_start_" + n)
        started.append((send, recv, buf))

    def gathered(k, after):
        send, recv, buf = started[k]
        buf = _ag_wait(buf, send, recv, after, "ag_wait_" + names[k])
        return _ag_pair(buf, pc, "ag_pair_" + names[k]).reshape((N_CHIPS,) + big[k].shape)

    qg2 = jnp.tile(q_norm_g, (1, 2))
    kg2 = jnp.tile(k_norm_g, (1, 2))
    lg, lb, wsp = gate_ln_g[0], gate_ln_b[0], w_spatial[0]
    bcol = jnp.broadcast_to(b_spatial[0][:, :, None], (G, BLK, BLK))

    xf = x.reshape(T, D)
    tgt = loss_target.reshape(T, D)
    h1 = _rms_fwd(xf, norm1_g, "norm1_fwd", after=behind)
    win_full = jnp.transpose(gathered(0, h1), (1, 0, 2)).reshape(D, in_w)
    proj = _matmul(h1, win_full, "nn", F32, "proj_fwd", tm=1024, tn=1664)
    ya, yg, yn = _mixer_fwd(d, proj, attn_sinks, qg2, kg2, lg, lb, wsp, bcol, out_norm_attn_g, out_norm_gate_g)
    wout_full = gathered(1, yn).reshape(attn_w + gate_w, D)
    x1 = _matmul(yn, wout_full, "nn", F32, "out_fwd", tm=1024, tn=1024, add=xf)
    h2 = _rms_fwd(x1, norm2_g, "norm2_fwd")
    wg_g, wu_g = gathered(2, h2), gathered(3, h2)
    a, b, f = _ffn_up(h2, wg_g, wu_g)
    wd_full = gathered(4, f).reshape(dff, D)
    x2 = _matmul(f, wd_full, "nn", F32, "ffn_down_fwd", tm=1024, tn=1024, tk=1408, add=x1)
    dx2, dx2b, loss_local = _loss(x2, tgt)

    def reduce_start(g, n):
        g4 = g.reshape(N_CHIPS, 2, g.shape[1] // 2, g.shape[2])
        pair = _add_pair(g4, _swap_halves(g4, pc, "rs_swap_" + n), pc, "rs_add_pair_" + n)
        return _rs_start(pair, g, "rs_start_" + n)

    reducing = {}
    g_d = _matmul(f, dx2b, "tn", BF16, "ffn_down_dw", tm=1408, tn=1024, tk=2048, out_slab="r")
    reducing["down"] = reduce_start(g_d, "down")
    da, db = _ffn_down_dx(dx2b, wd_full, a, b, reducing["down"][4])
    g_g = _matmul(h2, da, "tn", BF16, "ffn_gate_dw", tm=1024, tn=1408, tk=2048, out_slab="c")
    reducing["gate"] = reduce_start(g_g, "gate")
    g_u = _matmul(h2, db, "tn", BF16, "ffn_up_dw", tm=1024, tn=1408, tk=2048, out_slab="c",
                  after=reducing["gate"][4])
    reducing["up"] = reduce_start(g_u, "up")
    dh2 = _matmul(da, wg_g, "nt", F32, "ffn_gate_dx", tm=1024, tn=1024, tk=1408, b_slab="k",
                  after=reducing["up"][4])
    dh2 = _matmul(db, wu_g, "nt", F32, "ffn_up_dx", tm=1024, tn=1024, tk=1408, b_slab="k", add=dh2)
    dx1, dx1b, dg_norm2 = _rms_bwd(x1, norm2_g, dh2, dx2, "norm2_bwd", True)
    g_o = _matmul(yn, dx1b, "tn", BF16, "out_dw", tm=512, tn=1024, tk=2048, out_slab="r")
    reducing["out"] = reduce_start(g_o, "out")
    dy = _matmul(dx1b, wout_full, "nt", F32, "out_dx", tm=1024, tn=1024, after=reducing["out"][4])
    (dproj, dkv, dqg, dkg, dsk, dlg, dlb, dwsp, dbsp, dgoa, dgog) = _mixer_bwd(
        d, proj, ya, yg, dy, attn_sinks, qg2, kg2, lg, lb, wsp, bcol, out_norm_attn_g, out_norm_gate_g)
    dproj = _put_kv(d, dproj, dkv)
    g_in_full = _matmul(h1, dproj, "tn", BF16, "proj_dw", tm=1024, tn=1664, tk=2048)
    g_i = jnp.transpose(g_in_full.reshape(D, N_CHIPS, in_w // N_CHIPS), (1, 0, 2))
    reducing["in"] = reduce_start(g_i, "in")
    dh1 = _matmul(dproj, win_full, "nt", F32, "proj_dx", tm=1024, tn=1024, after=reducing["in"][4])
    dx, dg_norm1 = _rms_bwd(xf, norm1_g, dh1, dx1, "norm1_bwd", False)

    dqg64 = dqg[:, :HEAD_DIM] + dqg[:, HEAD_DIM:]
    dkg64 = dkg[:, :HEAD_DIM] + dkg[:, HEAD_DIM:]
    small_g_local = [dg_norm1, dqg64, dkg64, dsk[:, :d.n_heads], dlg, dlb, dwsp, dbsp, dgoa, dgog, dg_norm2]
    ar_send, ar_recv, ar_own, ar_land, ar_token = _ar_start(_pack(small_g_local))

    big_m = [m_w_in[0], m_w_out[0], m_w_ffn_gate[0], m_w_ffn_up[0], m_w_ffn_down[0]]
    big_v = [v_w_in[0], v_w_out[0], v_w_ffn_gate[0], v_w_ffn_up[0], v_w_ffn_down[0]]
    big_grads, big_d, big_nm, big_nv = [], [], [], []
    for w, m, v, n in zip(big, big_m, big_v, names):
        send, recv, pair, land, _ = reducing[n]
        pair, land = _rs_wait(pair, land, send, recv, ar_token, "rs_wait_" + n)
        own = _add_chips(pair, land, pc, "rs_add_chips_" + n)
        outs = _adamw_halves(w, own, _send_rows(own, "rs_send_" + n), m, v, pc, "adamw_" + n)
        for lst, o in zip((big_grads, big_d, big_nm, big_nv), outs):
            lst.append(o.reshape(w.shape))

    small_names_w = [norm1_g, q_norm_g, k_norm_g, attn_sinks, gate_ln_g, gate_ln_b, w_spatial, b_spatial,
                     out_norm_attn_g, out_norm_gate_g, norm2_g]
    small_m = [m_norm1_g, m_q_norm_g, m_k_norm_g, m_attn_sinks, m_gate_ln_g, m_gate_ln_b, m_w_spatial, m_b_spatial,
               m_out_norm_attn_g, m_out_norm_gate_g, m_norm2_g]
    small_v = [v_norm1_g, v_q_norm_g, v_k_norm_g, v_attn_sinks, v_gate_ln_g, v_gate_ln_b, v_w_spatial, v_b_spatial,
               v_out_norm_attn_g, v_out_norm_gate_g, v_norm2_g]
    shapes = [w.shape for w in small_names_w]
    ar_own, ar_land = _ar_wait(ar_own, ar_land, ar_send, ar_recv, big_nv[-1])
    me = (4 * lax.axis_index("x") + 2 * lax.axis_index("y") + lax.axis_index("c")).astype(jnp.int32).reshape(1)
    sg = _ar_sum(ar_own, ar_land, me)
    sd, snm, snv = _adamw(_pack(small_names_w), sg, _pack(small_m), _pack(small_v), "adamw_small")
    small_g, small_d, small_nm, small_nv = (_unpack(t, shapes) for t in (sg, sd, snm, snv))

    loss = lax.psum(loss_local[0, 0], ("x", "y", "c"))

    def order(small, bigs):
        s = list(small)
        bg = [t[None] for t in bigs]
        return [s[0], bg[0], s[1], s[2], s[3], s[4], s[5], s[6], s[7], s[8], s[9], bg[1], s[10], bg[2], bg[3], bg[4]]

    grad_x = dx.reshape(bl, seq, D)
    return (loss, grad_x, *order(small_g, big_grads), *order(small_d, big_d), *order(small_nm, big_nm),
            *order(small_nv, big_nv))
```

```python
import functools

import jax
import jax.numpy as jnp
from jax import lax
from jax.experimental import pallas as pl
from jax.experimental.pallas import tpu as pltpu

F32 = jnp.float32
BF16 = jnp.bfloat16
MESH = pl.DeviceIdType.MESH

EPS = 1e-6
HEAD_DIM = 64
N_KV_HEADS = 2
BLK = 128
N_CHIPS = 4
N_DEV = 8
NEG = -1e30

ADAM_LR = 0.001
ADAM_B1 = 0.9
ADAM_B2 = 0.999
ADAM_EPS = 1e-08
ADAM_WD = 0.01
ADAM_STEP = 10

VMEM_LIMIT = 56 * 1024 * 1024

NN = (((1,), (0,)), ((), ()))
NT = (((1,), (1,)), ((), ()))
TN = (((0,), (0,)), ((), ()))
HBM = pl.BlockSpec(memory_space=pltpu.HBM)
ANY = pl.BlockSpec(memory_space=pl.ANY)
SEM = pl.BlockSpec(memory_space=pltpu.SEMAPHORE)
EFFECT = pltpu.SideEffectType.DATAFLOW_SIDE_EFFECTING


def _dot(a, b, dn):
    return lax.dot_general(a, b, dn, preferred_element_type=F32)


def _pick(dim, pref, align=128):
    if dim <= pref:
        return dim
    t = (pref // align) * align
    while t >= align:
        if dim % t == 0:
            return t
        t -= align
    return dim


def _params(sem):
    return pltpu.CompilerParams(dimension_semantics=sem, vmem_limit_bytes=VMEM_LIMIT)


MM_CHUNK = 512


def _col_chunks(tn):
    return [slice(c0, min(c0 + MM_CHUNK, tn)) for c0 in range(0, tn, MM_CHUNK)]


def _mm_body(dn, nk, has_add, has_after, *refs):
    a_ref, b_ref = refs[:2]
    add_ref = refs[2] if has_add else None
    o_ref = refs[2 + has_add + has_after]
    chunks = _col_chunks(o_ref.shape[-1])

    def dot(cols):
        return _dot(a_ref[...], b_ref[cols, :] if dn == NT else b_ref[:, cols], dn)

    def finish(cols, r):
        if add_ref is not None:
            r = r + add_ref[:, cols]
        o_ref[:, cols] = r.astype(o_ref.dtype)

    if nk == 1:
        for cols in chunks:
            finish(cols, dot(cols))
        return
    acc_ref = refs[-1]
    k = pl.program_id(2)

    @pl.when(k == 0)
    def _():
        for cols in chunks:
            acc_ref[:, cols] = dot(cols)

    if nk > 2:
        @pl.when(jnp.logical_and(k > 0, k < nk - 1))
        def _():
            for cols in chunks:
                acc_ref[:, cols] += dot(cols)

    @pl.when(k == nk - 1)
    def _():
        for cols in chunks:
            finish(cols, acc_ref[:, cols] + dot(cols))


def _matmul(a, b, mode, out_dtype, name, *, tm, tn, tk=None, add=None, b_slab=None, out_slab=None, after=None):
    if mode == "nn":
        M, K = a.shape
        N = b.shape[0] * b.shape[2] if b_slab == "c" else b.shape[1]
    elif mode == "nt":
        M, K = a.shape
        N = b.shape[1] if b_slab == "k" else b.shape[0]
    else:
        K, M = a.shape
        N = b.shape[1]
    tk = K if tk is None else tk
    tm, tn, tk = _pick(M, tm), _pick(N, tn), _pick(K, tk)
    if b_slab == "c":
        tn = _pick(b.shape[2], tn)
    if b_slab == "k":
        tk = _pick(b.shape[2], tk)
    if out_slab == "c":
        tn = _pick(N // N_CHIPS, tn)
    if out_slab == "r":
        tm = _pick(M // N_CHIPS, tm)
    gm, gn, gk = M // tm, N // tn, K // tk

    if mode == "tn":
        a_spec = pl.BlockSpec((tk, tm), lambda j, i, k: (k, i))
        b_spec = pl.BlockSpec((tk, tn), lambda j, i, k: (k, j))
    else:
        a_spec = pl.BlockSpec((tm, tk), lambda j, i, k: (i, k))
        if b_slab == "c":
            per = b.shape[2] // tn
            b_spec = pl.BlockSpec((None, tk, tn), lambda j, i, k: (j // per, k, j % per))
        elif b_slab == "k":
            per = b.shape[2] // tk
            b_spec = pl.BlockSpec((None, tn, tk), lambda j, i, k: (k // per, j, k % per))
        elif mode == "nn":
            b_spec = pl.BlockSpec((tk, tn), lambda j, i, k: (k, j))
        else:
            b_spec = pl.BlockSpec((tn, tk), lambda j, i, k: (j, k))

    if out_slab == "c":
        per = (N // N_CHIPS) // tn
        o_spec = pl.BlockSpec((None, tm, tn), lambda j, i, k: (j // per, i, j % per))
        o_shape = jax.ShapeDtypeStruct((N_CHIPS, M, N // N_CHIPS), out_dtype)
    elif out_slab == "r":
        per = (M // N_CHIPS) // tm
        o_spec = pl.BlockSpec((None, tm, tn), lambda j, i, k: (i // per, i % per, j))
        o_shape = jax.ShapeDtypeStruct((N_CHIPS, M // N_CHIPS, N), out_dtype)
    else:
        o_spec = pl.BlockSpec((tm, tn), lambda j, i, k: (i, j))
        o_shape = jax.ShapeDtypeStruct((M, N), out_dtype)

    dn = {"nn": NN, "nt": NT, "tn": TN}[mode]
    in_specs = [a_spec, b_spec]
    args = [a, b]
    if add is not None:
        in_specs.append(pl.BlockSpec((tm, tn), lambda j, i, k: (i, j)))
        args.append(add)
    if after is not None:
        in_specs.append(ANY)
        args.append(after)
    return pl.pallas_call(
        functools.partial(_mm_body, dn, gk, add is not None, after is not None),
        name=name,
        grid=(gn, gm, gk),
        in_specs=in_specs,
        out_specs=o_spec,
        out_shape=o_shape,
        scratch_shapes=[pltpu.VMEM((tm, tn), F32)] if gk > 1 else [],
        compiler_params=_params(("parallel", "parallel", "arbitrary")),
    )(*args)


def _rms_fwd_body(x_ref, g_ref, *rest):
    h_ref = rest[-1]
    x = x_ref[...]
    r = lax.rsqrt(jnp.mean(x * x, axis=-1, keepdims=True) + EPS)
    h_ref[...] = (x * r * g_ref[...]).astype(h_ref.dtype)


def _rms_fwd(x, g, name, after=None):
    T, D = x.shape
    tr = _pick(T, 256, 16)
    extra = [] if after is None else [after]
    return pl.pallas_call(
        functools.partial(_rms_fwd_body),
        name=name,
        grid=(T // tr,),
        in_specs=[pl.BlockSpec((tr, D), lambda i: (i, 0)), pl.BlockSpec((1, D), lambda i: (0, 0))] + [ANY] * len(extra),
        out_specs=pl.BlockSpec((tr, D), lambda i: (i, 0)),
        out_shape=jax.ShapeDtypeStruct((T, D), BF16),
        compiler_params=_params(("parallel",)),
    )(x, g, *extra)


def _rms_bwd_body(with_bf16, x_ref, g_ref, dh_ref, res_ref, dx_ref, *rest):
    dg_ref = rest[-1]

    @pl.when(pl.program_id(0) == 0)
    def _():
        dg_ref[...] = jnp.zeros_like(dg_ref)

    x = x_ref[...]
    r = lax.rsqrt(jnp.mean(x * x, axis=-1, keepdims=True) + EPS)
    xh = x * r
    dh = dh_ref[...]
    dg_ref[...] += jnp.sum(dh * xh, axis=0, keepdims=True)
    t = dh * g_ref[...]
    dx = res_ref[...] + r * (t - xh * jnp.mean(t * xh, axis=-1, keepdims=True))
    dx_ref[...] = dx
    if with_bf16:
        rest[0][...] = dx.astype(BF16)


def _rms_bwd(x, g, dh, res, name, with_bf16):
    T, D = x.shape
    tr = _pick(T, 256, 16)
    row = pl.BlockSpec((tr, D), lambda i: (i, 0))
    vec = pl.BlockSpec((1, D), lambda i: (0, 0))
    extra = [jax.ShapeDtypeStruct((T, D), BF16)] if with_bf16 else []
    return pl.pallas_call(
        functools.partial(_rms_bwd_body, with_bf16),
        name=name,
        grid=(T // tr,),
        in_specs=[row, vec, row, row],
        out_specs=[row] + [row] * len(extra) + [vec],
        out_shape=[jax.ShapeDtypeStruct((T, D), F32)] + extra + [jax.ShapeDtypeStruct((1, D), F32)],
        compiler_params=_params(("arbitrary",)),
    )(x, g, dh, res)


def _ffn_up_body(h_ref, wg_ref, wu_ref, a_ref, b_ref, f_ref):
    for cols in _col_chunks(a_ref.shape[-1]):
        a = _dot(h_ref[...], wg_ref[:, cols], NN)
        b = _dot(h_ref[...], wu_ref[:, cols], NN)
        a_ref[:, cols] = a
        b_ref[:, cols] = b
        f_ref[:, cols] = (a * (1.0 / (1.0 + jnp.exp(-a))) * b).astype(f_ref.dtype)


def _ffn_up(h, wg, wu):
    T, D = h.shape
    n, _, fs = wg.shape
    tm, tn = _pick(T, 512), fs
    hs = pl.BlockSpec((tm, D), lambda j, i: (i, 0))
    ws = pl.BlockSpec((None, D, tn), lambda j, i: (j, 0, 0))
    os = pl.BlockSpec((tm, tn), lambda j, i: (i, j))
    return pl.pallas_call(
        functools.partial(_ffn_up_body),
        name="ffn_up_fwd",
        grid=(n, T // tm),
        in_specs=[hs, ws, ws],
        out_specs=[os, os, os],
        out_shape=[jax.ShapeDtypeStruct((T, n * fs), F32), jax.ShapeDtypeStruct((T, n * fs), F32),
                   jax.ShapeDtypeStruct((T, n * fs), BF16)],
        compiler_params=_params(("parallel", "parallel")),
    )(h, wg, wu)


def _ffn_down_dx_body(dx_ref, wd_ref, a_ref, b_ref, after_ref, da_ref, db_ref):
    del after_ref
    for cols in _col_chunks(da_ref.shape[-1]):
        df = _dot(dx_ref[...], wd_ref[cols, :], NT)
        a = a_ref[:, cols]
        s = 1.0 / (1.0 + jnp.exp(-a))
        da_ref[:, cols] = (df * b_ref[:, cols] * (s * (1.0 + a * (1.0 - s)))).astype(da_ref.dtype)
        db_ref[:, cols] = (df * (a * s)).astype(db_ref.dtype)


def _ffn_down_dx(dx2b, wd, a, b, after):
    T, D = dx2b.shape
    F = wd.shape[0]
    tm, tn = _pick(T, 512), _pick(F, 1408)
    xs = pl.BlockSpec((tm, D), lambda j, i: (i, 0))
    ws = pl.BlockSpec((tn, D), lambda j, i: (j, 0))
    os = pl.BlockSpec((tm, tn), lambda j, i: (i, j))
    return pl.pallas_call(
        functools.partial(_ffn_down_dx_body),
        name="ffn_down_dx",
        grid=(F // tn, T // tm),
        in_specs=[xs, ws, os, os, ANY],
        out_specs=[os, os],
        out_shape=[jax.ShapeDtypeStruct((T, F), BF16), jax.ShapeDtypeStruct((T, F), BF16)],
        compiler_params=_params(("parallel", "parallel")),
    )(dx2b, wd, a, b, after)


def _ffn_down_loss_body(nk, inv_d, f_ref, wd_ref, x1_ref, tgt_ref, dx2_ref, dx2b_ref, loss_ref, *scratch):
    j, i, k = pl.program_id(0), pl.program_id(1), pl.program_id(2)
    chunks = _col_chunks(dx2_ref.shape[-1])

    def dot(cols):
        return _dot(f_ref[...], wd_ref[:, cols], NN)

    @pl.when(jnp.logical_and(jnp.logical_and(j == 0, i == 0), k == 0))
    def _():
        loss_ref[...] = jnp.zeros_like(loss_ref)

    def finish(ffn_of):
        total = jnp.zeros((1, 1), F32)
        for cols in chunks:
            e = ffn_of(cols) + x1_ref[:, cols] - tgt_ref[:, cols]
            dx2 = e * inv_d
            dx2_ref[:, cols] = dx2
            dx2b_ref[:, cols] = dx2.astype(BF16)
            total = total + jnp.sum(jnp.sum(e * e, axis=-1, keepdims=True), axis=0, keepdims=True)
        loss_ref[...] += (0.5 * inv_d) * total

    if nk == 1:
        finish(dot)
        return
    acc_ref = scratch[0]

    @pl.when(k == 0)
    def _():
        for cols in chunks:
            acc_ref[:, cols] = dot(cols)

    if nk > 2:
        @pl.when(jnp.logical_and(k > 0, k < nk - 1))
        def _():
            for cols in chunks:
                acc_ref[:, cols] += dot(cols)

    @pl.when(k == nk - 1)
    def _():
        finish(lambda cols: acc_ref[:, cols] + dot(cols))


def _ffn_down_loss(f, wd, x1, tgt):
    T, F = f.shape
    D = wd.shape[1]
    tm, tn, tk = _pick(T, 1024), _pick(D, 1024), _pick(F, 1408)
    gm, gn, gk = T // tm, D // tn, F // tk
    tile = pl.BlockSpec((tm, tn), lambda j, i, k: (i, j))
    return pl.pallas_call(
        functools.partial(_ffn_down_loss_body, gk, 1.0 / D),
        name="ffn_down_loss",
        grid=(gn, gm, gk),
        in_specs=[pl.BlockSpec((tm, tk), lambda j, i, k: (i, k)), pl.BlockSpec((tk, tn), lambda j, i, k: (k, j)),
                  tile, tile],
        out_specs=[tile, tile, pl.BlockSpec((1, 1), lambda j, i, k: (0, 0))],
        out_shape=[jax.ShapeDtypeStruct((T, D), F32), jax.ShapeDtypeStruct((T, D), BF16),
                   jax.ShapeDtypeStruct((1, 1), F32)],
        scratch_shapes=[pltpu.VMEM((tm, tn), F32)] if gk > 1 else [],
        compiler_params=_params(("arbitrary", "arbitrary", "arbitrary")),
    )(f, wd, x1, tgt)


def _lo_mask(shape):
    return lax.broadcasted_iota(jnp.int32, shape, len(shape) - 1) < HEAD_DIM


def _half_sums(t, lo):
    s_lo = jnp.sum(jnp.where(lo, t, 0.0), axis=-1, keepdims=True)
    s_hi = jnp.sum(jnp.where(lo, 0.0, t), axis=-1, keepdims=True)
    return jnp.where(lo, s_lo, s_hi)


def _head_rstd(t, lo):
    return lax.rsqrt(_half_sums(t * t, lo) * (1.0 / HEAD_DIM) + EPS)


def _place(t, lo, kv_head):
    if kv_head == 0:
        t_lo = jnp.where(lo, t, 0.0)
        t_hi = pltpu.roll(t_lo, HEAD_DIM, 1)
    else:
        t_hi = jnp.where(lo, 0.0, t)
        t_lo = pltpu.roll(t_hi, HEAD_DIM, 1)
    return jnp.concatenate([t_lo, t_hi], axis=0).astype(BF16)


def _unplace(c0, c1, lo):
    return jnp.where(lo, c0 + pltpu.roll(c0, HEAD_DIM, 1), c1 + pltpu.roll(c1, HEAD_DIM, 1))


def _band(kv_cur, kv_prev, kg, lo2):
    kb = jnp.concatenate([kv_prev[:, :BLK], kv_cur[:, :BLK]], axis=0)
    vb = jnp.concatenate([kv_prev[:, BLK:], kv_cur[:, BLK:]], axis=0)
    rk = _head_rstd(kb, lo2)
    kn = kb * rk * kg
    kk = [_place(kn, lo2, h) for h in range(N_KV_HEADS)]
    vv = [_place(vb, lo2, h) for h in range(N_KV_HEADS)]
    return kb, rk, kk, vv


def _score_geometry(first_i32):
    qi = lax.broadcasted_iota(jnp.int32, (BLK, 4 * BLK), 0)
    col = lax.broadcasted_iota(jnp.int32, (BLK, 4 * BLK), 1)
    kj = col & (2 * BLK - 1)
    dist = qi + BLK - kj
    valid = (dist >= 0) & (dist < BLK) & (kj >= first_i32 * BLK)
    return col, dist.astype(F32), valid


def _pair_probs(qn, kk, col, distf, valid, slope0, slope1, sink0, sink1):
    s = _dot(qn.astype(BF16), kk, NT) * (HEAD_DIM ** -0.5)
    slope = jnp.where(col < 2 * BLK, slope0, slope1)
    logits = jnp.where(valid, s - slope * distf, NEG)
    probs, psink = [], []
    for hh, sk in ((0, sink0), (1, sink1)):
        l = logits[:, 2 * BLK * hh:2 * BLK * (hh + 1)]
        m = jnp.maximum(jnp.max(l, axis=-1, keepdims=True), sk)
        p = jnp.exp(l - m)
        es = jnp.exp(sk - m)
        inv = 1.0 / (jnp.sum(p, axis=-1, keepdims=True) + es)
        probs.append(p * inv)
        psink.append(es * inv)
    return probs, psink


def _gelu(z):
    return 0.5 * z * (1.0 + lax.erf(z * (0.5 ** 0.5)))


def _gelu_grad(z):
    return 0.5 * (1.0 + lax.erf(z * (0.5 ** 0.5))) + z * jnp.exp(-0.5 * z * z) * ((2.0 * jnp.pi) ** -0.5)


def _tril_w(w):
    r = lax.broadcasted_iota(jnp.int32, (BLK, BLK), 0)
    c = lax.broadcasted_iota(jnp.int32, (BLK, BLK), 1)
    return jnp.where(r >= c, w, 0.0), r >= c


def _gate_fwd_group(zu, zv, lg, lb, w, bcol):
    u = _gelu(zu)
    v = _gelu(zv)
    mu = jnp.mean(v, axis=-1, keepdims=True)
    vc = v - mu
    rs = lax.rsqrt(jnp.mean(vc * vc, axis=-1, keepdims=True) + EPS)
    vh = vc * rs
    vn = vh * lg + lb
    wt, tril = _tril_w(w)
    mixed = _dot(wt.astype(BF16), vn.astype(BF16), NN) + bcol
    return u, vh, rs, vn, wt, tril, mixed


class _Dims:
    def __init__(self, seq, attn_w, gate_w):
        self.seq, self.attn_w, self.gate_w = seq, attn_w, gate_w
        self.n_heads = attn_w // HEAD_DIM
        self.group = self.n_heads // N_KV_HEADS
        self.n_pairs = attn_w // BLK
        self.n_groups = gate_w // BLK
        self.kv_col = attn_w // (2 * BLK)
        self.u0 = attn_w + 2 * BLK
        self.v0 = self.u0 + gate_w
        self.in_w = self.v0 + gate_w
        self.slopes = [2.0 ** (-8.0 * (h + 1) / self.n_heads) for h in range(self.n_heads)]


def _mixer_fwd_body(d, sink_ref, proj_ref, kvp_ref, qg_ref, kg_ref, lg_ref, lb_ref, w_ref, b_ref, goa_ref, gog_ref,
                    ya_ref, yg_ref, y_ref):
    i = pl.program_id(0)
    first = (i % (d.seq // BLK) == 0).astype(jnp.int32)
    lo = _lo_mask((BLK, BLK))
    lo2 = _lo_mask((2 * BLK, BLK))
    kv_cur = proj_ref[:, d.attn_w:d.attn_w + 2 * BLK]
    _, _, kk, vv = _band(kv_cur, kvp_ref[...], kg_ref[...], lo2)
    col, distf, valid = _score_geometry(first)
    qg = qg_ref[...]
    for j in range(d.n_pairs):
        h0, h1 = 2 * j, 2 * j + 1
        kh = h0 // d.group
        q2 = proj_ref[:, BLK * j:BLK * (j + 1)]
        qn = q2 * _head_rstd(q2, lo) * qg
        probs, _ = _pair_probs(qn, kk[kh], col, distf, valid, d.slopes[h0], d.slopes[h1],
                               sink_ref[0, h0], sink_ref[0, h1])
        p = jnp.concatenate(probs, axis=1).astype(BF16)
        ya_ref[:, BLK * j:BLK * (j + 1)] = _dot(p, vv[kh], NN)
    for g in range(d.n_groups):
        zu = proj_ref[:, d.u0 + BLK * g:d.u0 + BLK * (g + 1)]
        zv = proj_ref[:, d.v0 + BLK * g:d.v0 + BLK * (g + 1)]
        u, _, _, _, _, _, mixed = _gate_fwd_group(zu, zv, lg_ref[g:g + 1, :], lb_ref[g:g + 1, :], w_ref[g], b_ref[g])
        yg_ref[:, BLK * g:BLK * (g + 1)] = u * mixed
    ya = ya_ref[...]
    ra = lax.rsqrt(jnp.mean(ya * ya, axis=-1, keepdims=True) + EPS)
    y_ref[:, :d.attn_w] = (ya * ra * goa_ref[...]).astype(y_ref.dtype)
    yg = yg_ref[...]
    rg = lax.rsqrt(jnp.mean(yg * yg, axis=-1, keepdims=True) + EPS)
    y_ref[:, d.attn_w:] = (yg * rg * gog_ref[...]).astype(y_ref.dtype)


def _mixer_specs(d, T):
    row = lambda w: pl.BlockSpec((BLK, w), lambda i: (i, 0))
    const2 = lambda a: pl.BlockSpec(a.shape, lambda i: (0, 0))
    const3 = lambda a: pl.BlockSpec(a.shape, lambda i: (0, 0, 0))
    kv_prev = pl.BlockSpec((BLK, 2 * BLK), lambda i: (jnp.maximum(i - 1, 0), d.kv_col))
    return row, const2, const3, kv_prev


def _mixer_fwd(d, proj, sinks, qg2, kg2, lg, lb, wsp, bcol, goa, gog):
    T = proj.shape[0]
    row, const2, const3, kv_prev = _mixer_specs(d, T)
    return pl.pallas_call(
        functools.partial(_mixer_fwd_body, d),
        name="mixer_fwd",
        grid=(T // BLK,),
        in_specs=[pl.BlockSpec(memory_space=pltpu.SMEM), row(d.in_w), kv_prev, const2(qg2), const2(kg2),
                  const2(lg), const2(lb), const3(wsp), const3(bcol), const2(goa), const2(gog)],
        out_specs=[row(d.attn_w), row(d.gate_w), row(d.attn_w + d.gate_w)],
        out_shape=[jax.ShapeDtypeStruct((T, d.attn_w), F32), jax.ShapeDtypeStruct((T, d.gate_w), F32),
                   jax.ShapeDtypeStruct((T, d.attn_w + d.gate_w), BF16)],
        compiler_params=_params(("parallel",)),
    )(sinks, proj, proj, qg2, kg2, lg, lb, wsp, bcol, goa, gog)


def _mixer_bwd_body(d, sink_ref, proj_ref, kvp_ref, ya_ref, yg_ref, dy_ref, qg_ref, kg_ref, lg_ref, lb_ref, w_ref,
                    b_ref, goa_ref, gog_ref,
                    dproj_ref, dkv_ref, dqg_ref, dkg_ref, dsk_ref, dlg_ref, dlb_ref, dw_ref, db_ref, dgoa_ref,
                    dgog_ref):
    i = pl.program_id(0)

    @pl.when(i == 0)
    def _():
        for r in (dqg_ref, dkg_ref, dsk_ref, dlg_ref, dlb_ref, dw_ref, db_ref, dgoa_ref, dgog_ref):
            r[...] = jnp.zeros_like(r)

    first = (i % (d.seq // BLK) == 0).astype(jnp.int32)
    lo = _lo_mask((BLK, BLK))
    lo2 = _lo_mask((2 * BLK, BLK))
    lane_row = lax.broadcasted_iota(jnp.int32, (1, BLK), 1)

    ya = ya_ref[...]
    ra = lax.rsqrt(jnp.mean(ya * ya, axis=-1, keepdims=True) + EPS)
    yah = ya * ra
    dyn = dy_ref[:, :d.attn_w]
    dgoa_ref[...] += jnp.sum(dyn * yah, axis=0, keepdims=True)
    t = dyn * goa_ref[...]
    dya = ra * (t - yah * jnp.mean(t * yah, axis=-1, keepdims=True))
    yg = yg_ref[...]
    rg = lax.rsqrt(jnp.mean(yg * yg, axis=-1, keepdims=True) + EPS)
    ygh = yg * rg
    dyn = dy_ref[:, d.attn_w:]
    dgog_ref[...] += jnp.sum(dyn * ygh, axis=0, keepdims=True)
    t = dyn * gog_ref[...]
    dyg = rg * (t - ygh * jnp.mean(t * ygh, axis=-1, keepdims=True))

    kv_cur = proj_ref[:, d.attn_w:d.attn_w + 2 * BLK]
    kg = kg_ref[...]
    kb, rk, kk, vv = _band(kv_cur, kvp_ref[...], kg, lo2)
    col, distf, valid = _score_geometry(first)
    qg = qg_ref[...]
    ck = [jnp.zeros((2 * BLK, BLK), F32) for _ in range(N_KV_HEADS)]
    cv = [jnp.zeros((2 * BLK, BLK), F32) for _ in range(N_KV_HEADS)]
    dsk = jnp.zeros((1, BLK), F32)
    dqg = jnp.zeros((1, BLK), F32)
    for j in range(d.n_pairs):
        h0, h1 = 2 * j, 2 * j + 1
        kh = h0 // d.group
        cols = slice(BLK * j, BLK * (j + 1))
        q2 = proj_ref[:, cols]
        rq = _head_rstd(q2, lo)
        qh = q2 * rq
        qn = qh * qg
        probs, psink = _pair_probs(qn, kk[kh], col, distf, valid, d.slopes[h0], d.slopes[h1],
                                   sink_ref[0, h0], sink_ref[0, h1])
        do2 = dya[:, cols]
        prod = do2 * ya[:, cols]
        delta = (jnp.sum(jnp.where(lo, prod, 0.0), axis=-1, keepdims=True),
                 jnp.sum(jnp.where(lo, 0.0, prod), axis=-1, keepdims=True))
        do2b = do2.astype(BF16)
        dp = _dot(do2b, vv[kh], NT)
        ds = []
        for hh in (0, 1):
            ds.append(probs[hh] * (dp[:, 2 * BLK * hh:2 * BLK * (hh + 1)] - delta[hh]))
            dsink = -jnp.sum(psink[hh] * delta[hh], axis=0, keepdims=True)
            dsk = dsk + jnp.where(lane_row == (h0 + hh), dsink, 0.0)
        dsb = (jnp.concatenate(ds, axis=1) * (HEAD_DIM ** -0.5)).astype(BF16)
        pb = jnp.concatenate(probs, axis=1).astype(BF16)
        qnb = qn.astype(BF16)
        dqn = _dot(dsb, kk[kh], NN)
        dkk = _dot(dsb, qnb, TN)
        dvv = _dot(pb, do2b, TN)
        ck[kh] = ck[kh] + jnp.where(lo2, dkk[:2 * BLK], 0.0) + jnp.where(lo2, 0.0, dkk[2 * BLK:])
        cv[kh] = cv[kh] + jnp.where(lo2, dvv[:2 * BLK], 0.0) + jnp.where(lo2, 0.0, dvv[2 * BLK:])
        dqg = dqg + jnp.sum(dqn * qh, axis=0, keepdims=True)
        t = dqn * qg
        dq2 = rq * (t - qh * (_half_sums(t * qh, lo) * (1.0 / HEAD_DIM)))
        dproj_ref[:, cols] = dq2.astype(dproj_ref.dtype)
    dsk_ref[...] += dsk
    dqg_ref[...] += dqg
    dkn = _unplace(ck[0], ck[1], lo2)
    dvb = _unplace(cv[0], cv[1], lo2)
    khat = kb * rk
    dkg_ref[...] += jnp.sum(dkn * khat, axis=0, keepdims=True)
    t = dkn * kg
    dkb = rk * (t - khat * (_half_sums(t * khat, lo2) * (1.0 / HEAD_DIM)))
    rows_cur = pl.ds(pl.multiple_of(i * BLK, BLK), BLK)
    rows_prev = pl.ds(pl.multiple_of(jnp.maximum(i - 1, 0) * BLK, BLK), BLK)
    dkv_ref[rows_cur, :] = jnp.concatenate([dkb[BLK:], dvb[BLK:]], axis=1)
    dkv_ref[rows_prev, :] += jnp.concatenate([dkb[:BLK], dvb[:BLK]], axis=1)
    dproj_ref[:, d.attn_w:d.attn_w + 2 * BLK] = jnp.zeros((BLK, 2 * BLK), dproj_ref.dtype)

    for g in range(d.n_groups):
        ucols = slice(d.u0 + BLK * g, d.u0 + BLK * (g + 1))
        vcols = slice(d.v0 + BLK * g, d.v0 + BLK * (g + 1))
        zu = proj_ref[:, ucols]
        zv = proj_ref[:, vcols]
        lg = lg_ref[g:g + 1, :]
        u, vh, rs, vn, wt, tril, mixed = _gate_fwd_group(zu, zv, lg, lb_ref[g:g + 1, :], w_ref[g], b_ref[g])
        dyg_g = dyg[:, BLK * g:BLK * (g + 1)]
        du = dyg_g * mixed
        dmix = dyg_g * u
        dmb = dmix.astype(BF16)
        db_ref[g:g + 1, :] += jnp.sum(jnp.transpose(dmix), axis=0, keepdims=True)
        dw_ref[g] += jnp.where(tril, _dot(dmb, vn.astype(BF16), NT), 0.0)
        dvn = _dot(wt.astype(BF16), dmb, TN)
        dlg_ref[g:g + 1, :] += jnp.sum(dvn * vh, axis=0, keepdims=True)
        dlb_ref[g:g + 1, :] += jnp.sum(dvn, axis=0, keepdims=True)
        dvh = dvn * lg
        dv = rs * (dvh - jnp.mean(dvh, axis=-1, keepdims=True) - vh * jnp.mean(dvh * vh, axis=-1, keepdims=True))
        dproj_ref[:, ucols] = (du * _gelu_grad(zu)).astype(dproj_ref.dtype)
        dproj_ref[:, vcols] = (dv * _gelu_grad(zv)).astype(dproj_ref.dtype)


def _mixer_bwd(d, proj, ya, yg, dy, sinks, qg2, kg2, lg, lb, wsp, bcol, goa, gog):
    T = proj.shape[0]
    row, const2, const3, kv_prev = _mixer_specs(d, T)
    acc2 = lambda s: pl.BlockSpec(s, lambda i: (0, 0))
    G = d.n_groups
    out_shapes = [((T, d.in_w), BF16), ((T, 2 * BLK), F32), ((1, BLK), F32), ((1, BLK), F32), ((1, BLK), F32),
                  ((G, BLK), F32), ((G, BLK), F32), ((G, BLK, BLK), F32), ((G, BLK), F32),
                  ((1, d.attn_w), F32), ((1, d.gate_w), F32)]
    out_specs = [row(d.in_w)] + [acc2(s) for s, _ in out_shapes[1:7]] + \
                [pl.BlockSpec((G, BLK, BLK), lambda i: (0, 0, 0))] + [acc2(s) for s, _ in out_shapes[8:]]
    return pl.pallas_call(
        functools.partial(_mixer_bwd_body, d),
        name="mixer_bwd",
        grid=(T // BLK,),
        in_specs=[pl.BlockSpec(memory_space=pltpu.SMEM), row(d.in_w), kv_prev, row(d.attn_w), row(d.gate_w),
                  row(d.attn_w + d.gate_w), const2(qg2), const2(kg2), const2(lg), const2(lb), const3(wsp),
                  const3(bcol), const2(goa), const2(gog)],
        out_specs=out_specs,
        out_shape=[jax.ShapeDtypeStruct(s, t) for s, t in out_shapes],
        compiler_params=_params(("arbitrary",)),
    )(sinks, proj, proj, ya, yg, dy, qg2, kg2, lg, lb, wsp, bcol, goa, gog)


def _put_kv_body(dkv_ref, dproj_in_ref, dproj_ref):
    del dproj_in_ref
    dproj_ref[...] = dkv_ref[...].astype(dproj_ref.dtype)


def _put_kv(d, dproj, dkv):
    T = dproj.shape[0]
    tr = _pick(T, 1024, 16)
    return pl.pallas_call(
        functools.partial(_put_kv_body),
        name="put_kv",
        grid=(T // tr,),
        in_specs=[pl.BlockSpec((tr, 2 * BLK), lambda i: (i, 0)), pl.BlockSpec(memory_space=pl.ANY)],
        out_specs=pl.BlockSpec((tr, 2 * BLK), lambda i: (i, d.kv_col)),
        out_shape=jax.ShapeDtypeStruct(dproj.shape, dproj.dtype),
        input_output_aliases={1: 0},
        compiler_params=_params(("parallel",)),
    )(dkv, dproj)


def _add_pair_body(pc_ref, own_ref, got_ref, o_ref):
    del pc_ref
    o_ref[...] = (own_ref[...].astype(F32) + got_ref[...].astype(F32)).astype(o_ref.dtype)


def _add_pair(g4, got, pc, name):
    n, _, h, C = g4.shape
    tr = _pick(h, 512, 16)
    return pl.pallas_call(
        functools.partial(_add_pair_body),
        name=name,
        grid_spec=pltpu.PrefetchScalarGridSpec(
            num_scalar_prefetch=1,
            grid=(n, h // tr),
            in_specs=[pl.BlockSpec((None, None, tr, C), lambda q, i, pc: (q, pc[1], i, 0)),
                      pl.BlockSpec((None, tr, C), lambda q, i, pc: (q, i, 0))],
            out_specs=pl.BlockSpec((None, tr, C), lambda q, i, pc: (q, i, 0)),
        ),
        out_shape=jax.ShapeDtypeStruct((n, h, C), g4.dtype),
        compiler_params=_params(("parallel", "parallel")),
    )(pc, g4, got)


def _adamw_update(w, g, m, v):
    m = ADAM_B1 * m + (1.0 - ADAM_B1) * g
    v = ADAM_B2 * v + (1.0 - ADAM_B2) * (g * g)
    m_hat = m / (1.0 - ADAM_B1 ** ADAM_STEP)
    v_hat = v / (1.0 - ADAM_B2 ** ADAM_STEP)
    return -ADAM_LR * (m_hat / (jnp.sqrt(v_hat) + ADAM_EPS) + ADAM_WD * w), m, v


def _adamw_body(w_ref, g_ref, m_ref, v_ref, d_ref, nm_ref, nv_ref):
    d_ref[...], nm_ref[...], nv_ref[...] = _adamw_update(w_ref[...], g_ref[...], m_ref[...], v_ref[...])


def _adamw(w, g, m, v, name):
    R, C = w.shape
    tr = _pick(R, 512, 8)
    blk = pl.BlockSpec((tr, C), lambda i: (i, 0))
    return pl.pallas_call(
        functools.partial(_adamw_body),
        name=name,
        grid=(R // tr,),
        in_specs=[blk] * 4,
        out_specs=[blk] * 3,
        out_shape=[jax.ShapeDtypeStruct((R, C), F32)] * 3,
        compiler_params=_params(("parallel",)),
    )(w, g, m, v)


def _adamw_halves_body(pc_ref, w_ref, own_ref, got_ref, m_ref, v_ref, g_ref, d_ref, nm_ref, nv_ref):
    mine = pl.program_id(0) == pc_ref[1]

    def update(g):
        g_ref[...] = g
        d_ref[...], nm_ref[...], nv_ref[...] = _adamw_update(w_ref[...], g, m_ref[...], v_ref[...])

    @pl.when(mine)
    def _():
        update(own_ref[...])

    @pl.when(jnp.logical_not(mine))
    def _():
        update(got_ref[...])


def _adamw_halves(w, own, got, m, v, pc, name):
    h, C = own.shape
    tr = _pick(h, 512, 8)
    full = pl.BlockSpec((None, tr, C), lambda hh, i, pc: (hh, i, 0))
    mine = pl.BlockSpec((tr, C), lambda hh, i, pc: (jnp.where(hh == pc[1], i, 0), 0))
    theirs = pl.BlockSpec((tr, C), lambda hh, i, pc: (jnp.where(hh == pc[1], 0, i), 0))
    return pl.pallas_call(
        functools.partial(_adamw_halves_body),
        name=name,
        grid_spec=pltpu.PrefetchScalarGridSpec(
            num_scalar_prefetch=1,
            grid=(2, h // tr),
            in_specs=[full, mine, theirs, full, full],
            out_specs=[full] * 4,
        ),
        out_shape=[jax.ShapeDtypeStruct((2, h, C), F32)] * 4,
        compiler_params=_params(("parallel", "parallel")),
    )(pc, w.reshape(2, h, C), own, got, m.reshape(2, h, C), v.reshape(2, h, C))


def _me():
    x, y, c = lax.axis_index("x"), lax.axis_index("y"), lax.axis_index("c")
    chips = [(1 - x, y), (x, 1 - y), (1 - x, 1 - y)]
    return x, y, c, chips


def _cast_into_body(pc_ref, w_ref, o_ref):
    del pc_ref
    o_ref[...] = w_ref[...].astype(o_ref.dtype)


def _cast_into(w, pc, name):
    Rs, C = w.shape
    h = Rs // 2
    tr = _pick(h, 512, 16)
    return pl.pallas_call(
        functools.partial(_cast_into_body),
        name=name,
        grid_spec=pltpu.PrefetchScalarGridSpec(
            num_scalar_prefetch=1,
            grid=(2, h // tr),
            in_specs=[pl.BlockSpec((None, tr, C), lambda hh, i, pc: (hh, i, 0))],
            out_specs=pl.BlockSpec((None, None, tr, C), lambda hh, i, pc: (pc[0], hh, i, 0)),
        ),
        out_shape=jax.ShapeDtypeStruct((N_CHIPS, 2, h, C), BF16),
        compiler_params=_params(("parallel", "parallel")),
    )(pc, w.reshape(2, h, C))


MAX_PIECES = 4


def _send_tile_to_sibling(src_of, dst_of, tr, dst_total, send_sems, recv_sem, last):
    x, y, c, _ = _me()
    pieces = MAX_PIECES if tr % (16 * MAX_PIECES) == 0 else (2 if tr % 32 == 0 else 1)
    n = tr // pieces
    copies = [pltpu.make_async_remote_copy(src_ref=src_of(k * n, n), dst_ref=dst_of(k * n, n), send_sem=send_sems.at[k],
                                           recv_sem=recv_sem, device_id=(x, y, 1 - c), device_id_type=MESH)
              for k in range(pieces)]
    for cp in copies:
        cp.start()
    for cp in copies:
        cp.wait_send()

    @pl.when(last)
    def _():
        pltpu.make_async_remote_copy(src_ref=dst_total, dst_ref=dst_total, send_sem=send_sems.at[0], recv_sem=recv_sem,
                                     device_id=(x, y, 1 - c), device_id_type=MESH).wait_recv()


TILE_SEMS = [pltpu.SemaphoreType.DMA((MAX_PIECES,)), pltpu.SemaphoreType.DMA(())]


def _ag_pair_body(tr, n_i, pc_ref, tile_ref, buf_ref, send_sem, recv_sem):
    j, i = pl.program_id(0), pl.program_id(1)
    q = pc_ref[0] ^ (j + 1)
    c = pc_ref[1]
    r_tile = pl.multiple_of(i * tr, tr)
    _send_tile_to_sibling(lambda r0, n: tile_ref.at[:, :, pl.ds(r0, n)],
                          lambda r0, n: buf_ref.at[pl.ds(q, 1), pl.ds(c, 1), pl.ds(r_tile + r0, n)], tr,
                          buf_ref.at[pl.ds(0, N_CHIPS - 1), 0], send_sem, recv_sem,
                          jnp.logical_and(j == N_CHIPS - 2, i == n_i - 1))


def _ag_pair(buf, pc, name):
    _, _, h, C = buf.shape
    tr = _pick(h, 512, 16)
    return pl.pallas_call(
        functools.partial(_ag_pair_body, tr, h // tr),
        name=name,
        grid_spec=pltpu.PrefetchScalarGridSpec(
            num_scalar_prefetch=1,
            grid=(N_CHIPS - 1, h // tr),
            in_specs=[pl.BlockSpec((1, 1, tr, C), lambda j, i, pc: (pc[0] ^ (j + 1), pc[1], i, 0))],
            out_specs=HBM,
            scratch_shapes=TILE_SEMS,
        ),
        out_shape=jax.ShapeDtypeStruct(buf.shape, buf.dtype),
        input_output_aliases={1: 0},
        compiler_params=_params(("arbitrary", "arbitrary")),
    )(pc, buf)


def _swap_halves_body(tr, n_q, n_i, pc_ref, tile_ref, got_ref, send_sem, recv_sem):
    del pc_ref
    q, i = pl.program_id(0), pl.program_id(1)
    r_tile = pl.multiple_of(i * tr, tr)
    _send_tile_to_sibling(lambda r0, n: tile_ref.at[:, :, pl.ds(r0, n)],
                          lambda r0, n: got_ref.at[pl.ds(q, 1), :, pl.ds(r_tile + r0, n)], tr, got_ref, send_sem, recv_sem,
                          jnp.logical_and(q == n_q - 1, i == n_i - 1))


def _swap_halves(g4, pc, name):
    n, _, h, C = g4.shape
    tr = _pick(h, 512, 16)
    return pl.pallas_call(
        functools.partial(_swap_halves_body, tr, n, h // tr),
        name=name,
        grid_spec=pltpu.PrefetchScalarGridSpec(
            num_scalar_prefetch=1,
            grid=(n, h // tr),
            in_specs=[pl.BlockSpec((1, 1, tr, C), lambda q, i, pc: (q, 1 - pc[1], i, 0))],
            out_specs=HBM,
            scratch_shapes=TILE_SEMS,
        ),
        out_shape=jax.ShapeDtypeStruct((n, 1, h, C), g4.dtype),
        compiler_params=_params(("arbitrary", "arbitrary")),
    )(pc, g4).reshape(n, h, C)


def _ici_copy(src, dst, send_sems, recv_sems, j, chip, c):
    return pltpu.make_async_remote_copy(src_ref=src, dst_ref=dst, send_sem=send_sems.at[j], recv_sem=recv_sems.at[j],
                                        device_id=(chip[0], chip[1], c), device_id_type=MESH)


def _token_spec():
    return jax.ShapeDtypeStruct((8, BLK), F32), pl.BlockSpec(memory_space=pltpu.VMEM)


def _ag_start_body(buf_ref, after_ref, send_sems, recv_sems, buf_thru, token_ref):
    del after_ref, buf_thru
    x, y, c, chips = _me()
    mine = buf_ref.at[2 * x + y, c]
    for j, chip in enumerate(chips):
        _ici_copy(mine, mine, send_sems, recv_sems, j, chip, c).start()
    token_ref[...] = jnp.zeros_like(token_ref)


def _ag_start(buf, after, name):
    tok_shape, tok_spec = _token_spec()
    sems = pltpu.SemaphoreType.DMA((N_CHIPS - 1,))
    return pl.pallas_call(
        functools.partial(_ag_start_body),
        name=name,
        in_specs=[HBM, ANY],
        out_specs=[SEM, SEM, HBM, tok_spec],
        out_shape=[sems, sems, pltpu.HBM(buf.shape, buf.dtype), tok_shape],
        input_output_aliases={0: 2},
        compiler_params=pltpu.CompilerParams(has_side_effects=EFFECT),
    )(pltpu.with_memory_space_constraint(buf, pltpu.HBM), after)


def _ag_wait_body(buf_ref, send_sems, recv_sems, after_ref, buf_out):
    del after_ref, buf_out
    x, y, c, chips = _me()
    mine = buf_ref.at[2 * x + y, c]
    for j, chip in enumerate(chips):
        theirs = buf_ref.at[2 * chip[0] + chip[1], c]
        _ici_copy(mine, mine, send_sems, recv_sems, j, chip, c).wait_send()
        _ici_copy(theirs, theirs, send_sems, recv_sems, j, chip, c).wait_recv()


def _ag_wait(buf, send_sems, recv_sems, after, name):
    return pl.pallas_call(
        functools.partial(_ag_wait_body),
        name=name,
        in_specs=[HBM, SEM, SEM, ANY],
        out_specs=HBM,
        out_shape=pltpu.HBM(buf.shape, buf.dtype),
        input_output_aliases={0: 0},
        compiler_params=pltpu.CompilerParams(has_side_effects=EFFECT),
    )(buf, send_sems, recv_sems, after)


def _rs_start_body(pair_ref, land_ref, after_ref, send_sems, recv_sems, pair_thru, land_thru, token_ref):
    del after_ref, pair_thru, land_thru
    x, y, c, chips = _me()
    for j, chip in enumerate(chips):
        _ici_copy(pair_ref.at[2 * chip[0] + chip[1]], land_ref.at[j], send_sems, recv_sems, j, chip, c).start()
    token_ref[...] = jnp.zeros_like(token_ref)


def _rs_start(pair, after, name):
    n, h, C = pair.shape
    tok_shape, tok_spec = _token_spec()
    sems = pltpu.SemaphoreType.DMA((N_CHIPS - 1,))
    land = pltpu.with_memory_space_constraint(lax.empty((N_CHIPS - 1, h, C), pair.dtype), pltpu.HBM)
    return pl.pallas_call(
        functools.partial(_rs_start_body),
        name=name,
        in_specs=[HBM, HBM, ANY],
        out_specs=[SEM, SEM, HBM, HBM, tok_spec],
        out_shape=[sems, sems, pltpu.HBM(pair.shape, pair.dtype), pltpu.HBM(land.shape, land.dtype), tok_shape],
        input_output_aliases={0: 2, 1: 3},
        compiler_params=pltpu.CompilerParams(has_side_effects=EFFECT),
    )(pltpu.with_memory_space_constraint(pair, pltpu.HBM), land, after)


def _rs_wait_body(pair_ref, land_ref, send_sems, recv_sems, after_ref, pair_out, land_out):
    del after_ref, pair_out, land_out
    x, y, c, chips = _me()
    for j, chip in enumerate(chips):
        _ici_copy(pair_ref.at[0], land_ref.at[j], send_sems, recv_sems, j, chip, c).wait_send()
        _ici_copy(pair_ref.at[0], land_ref.at[j], send_sems, recv_sems, j, chip, c).wait_recv()


def _rs_wait(pair, land, send_sems, recv_sems, after, name):
    return pl.pallas_call(
        functools.partial(_rs_wait_body),
        name=name,
        in_specs=[HBM, HBM, SEM, SEM, ANY],
        out_specs=[HBM, HBM],
        out_shape=[pltpu.HBM(pair.shape, pair.dtype), pltpu.HBM(land.shape, land.dtype)],
        input_output_aliases={0: 0, 1: 1},
        compiler_params=pltpu.CompilerParams(has_side_effects=EFFECT),
    )(pair, land, send_sems, recv_sems, after)


def _add_chips_body(tr, n_i, pc_ref, own_ref, l0_ref, l1_ref, l2_ref, o_ref, got_ref, send_sems, recv_sem):
    del pc_ref
    i = pl.program_id(0)
    r = own_ref[...].astype(F32) + l0_ref[...].astype(F32)
    o_ref[...] = r + l1_ref[...].astype(F32) + l2_ref[...].astype(F32)
    r_tile = pl.multiple_of(i * tr, tr)
    _send_tile_to_sibling(lambda r0, n: o_ref.at[pl.ds(r0, n)], lambda r0, n: got_ref.at[pl.ds(r_tile + r0, n)], tr,
                          got_ref, send_sems, recv_sem, i == n_i - 1)


def _add_chips(pair, land, pc, name):
    _, h, C = pair.shape
    tr = _pick(h, 256, 16)
    slot = lambda j: pl.BlockSpec((None, tr, C), lambda i, pc: (j, i, 0))
    return pl.pallas_call(
        functools.partial(_add_chips_body, tr, h // tr),
        name=name,
        grid_spec=pltpu.PrefetchScalarGridSpec(
            num_scalar_prefetch=1,
            grid=(h // tr,),
            in_specs=[pl.BlockSpec((None, tr, C), lambda i, pc: (pc[0], i, 0)), slot(0), slot(1), slot(2)],
            out_specs=[pl.BlockSpec((tr, C), lambda i, pc: (i, 0)), HBM],
            scratch_shapes=TILE_SEMS,
        ),
        out_shape=[jax.ShapeDtypeStruct((h, C), F32), jax.ShapeDtypeStruct((h, C), F32)],
        compiler_params=_params(("arbitrary",)),
    )(pc, pair, land, land, land)


def _peer(r):
    x, y, c, _ = _me()
    return (x ^ ((r >> 2) & 1), y ^ ((r >> 1) & 1), c ^ (r & 1))


def _ar_start_body(x_ref, land_ref, send_sems, recv_sems, x_thru, land_thru, token_ref):
    del x_thru, land_thru
    for r in range(1, N_DEV):
        pltpu.make_async_remote_copy(src_ref=x_ref, dst_ref=land_ref.at[r - 1], send_sem=send_sems.at[r - 1],
                                     recv_sem=recv_sems.at[r - 1], device_id=_peer(r), device_id_type=MESH).start()
    token_ref[...] = jnp.zeros_like(token_ref)


def _ar_start(packed):
    tok_shape, tok_spec = _token_spec()
    sems = pltpu.SemaphoreType.DMA((N_DEV - 1,))
    land = pltpu.with_memory_space_constraint(lax.empty((N_DEV - 1,) + packed.shape, packed.dtype), pltpu.HBM)
    return pl.pallas_call(
        functools.partial(_ar_start_body),
        name="ar_start",
        in_specs=[HBM, HBM],
        out_specs=[SEM, SEM, HBM, HBM, tok_spec],
        out_shape=[sems, sems, pltpu.HBM(packed.shape, packed.dtype), pltpu.HBM(land.shape, land.dtype), tok_shape],
        input_output_aliases={0: 2, 1: 3},
        compiler_params=pltpu.CompilerParams(has_side_effects=EFFECT),
    )(pltpu.with_memory_space_constraint(packed, pltpu.HBM), land)


def _ar_wait_body(x_ref, land_ref, send_sems, recv_sems, after_ref, x_out, land_out):
    del after_ref, x_out, land_out
    for r in range(1, N_DEV):
        cp = pltpu.make_async_remote_copy(src_ref=x_ref, dst_ref=land_ref.at[r - 1], send_sem=send_sems.at[r - 1],
                                          recv_sem=recv_sems.at[r - 1], device_id=_peer(r), device_id_type=MESH)
        cp.wait_send()
        cp.wait_recv()


def _ar_wait(packed, land, send_sems, recv_sems, after):
    return pl.pallas_call(
        functools.partial(_ar_wait_body),
        name="ar_wait",
        in_specs=[HBM, HBM, SEM, SEM, ANY],
        out_specs=[HBM, HBM],
        out_shape=[pltpu.HBM(packed.shape, packed.dtype), pltpu.HBM(land.shape, land.dtype)],
        input_output_aliases={0: 0, 1: 1},
        compiler_params=pltpu.CompilerParams(has_side_effects=EFFECT),
    )(packed, land, send_sems, recv_sems, after)


def _ar_sum_body(me_ref, own_ref, *rest):
    o_ref = rest[N_DEV]
    acc = None
    for dev in range(N_DEV):
        term = jnp.where(me_ref[0] == dev, own_ref[...], rest[dev][...])
        acc = term if acc is None else acc + term
    o_ref[...] = acc


def _ar_sum(packed, land, me):
    R, C = packed.shape
    tr = _pick(R, 552, 8)
    own = pl.BlockSpec((tr, C), lambda i, me: (i, 0))
    slot = lambda dev: pl.BlockSpec((None, tr, C), lambda i, me: (jnp.maximum((dev ^ me[0]) - 1, 0), i, 0))
    return pl.pallas_call(
        functools.partial(_ar_sum_body),
        name="ar_sum",
        grid_spec=pltpu.PrefetchScalarGridSpec(
            num_scalar_prefetch=1,
            grid=(R // tr,),
            in_specs=[own] + [slot(dev) for dev in range(N_DEV)],
            out_specs=pl.BlockSpec((tr, C), lambda i, me: (i, 0)),
        ),
        out_shape=jax.ShapeDtypeStruct((R, C), F32),
        compiler_params=_params(("parallel",)),
    )(me, packed, *([land] * N_DEV))


def _pack(arrays):
    rows = []
    for a in arrays:
        flat = a.reshape(-1).astype(F32)
        pad = (-flat.shape[0]) % BLK
        rows.append(jnp.pad(flat, (0, pad)).reshape(-1, BLK))
    packed = jnp.concatenate(rows, axis=0)
    pad = (-packed.shape[0]) % 8
    return jnp.pad(packed, ((0, pad), (0, 0)))


def _unpack(packed, shapes):
    out, r = [], 0
    for s in shapes:
        n = 1
        for k in s:
            n *= k
        nr = -(-n // BLK)
        out.append(packed[r:r + nr].reshape(-1)[:n].reshape(s))
        r += nr
    return out


def kernel(x, norm1_g, w_in, q_norm_g, k_norm_g, attn_sinks, gate_ln_g, gate_ln_b, w_spatial, b_spatial, out_norm_attn_g, out_norm_gate_g, w_out, norm2_g, w_ffn_gate, w_ffn_up, w_ffn_down, loss_target, m_norm1_g, m_w_in, m_q_norm_g, m_k_norm_g, m_attn_sinks, m_gate_ln_g, m_gate_ln_b, m_w_spatial, m_b_spatial, m_out_norm_attn_g, m_out_norm_gate_g, m_w_out, m_norm2_g, m_w_ffn_gate, m_w_ffn_up, m_w_ffn_down, v_norm1_g, v_w_in, v_q_norm_g, v_k_norm_g, v_attn_sinks, v_gate_ln_g, v_gate_ln_b, v_w_spatial, v_b_spatial, v_out_norm_attn_g, v_out_norm_gate_g, v_w_out, v_norm2_g, v_w_ffn_gate, v_w_ffn_up, v_w_ffn_down):
    bl, seq, D = x.shape
    T = bl * seq
    attn_w, gate_w = out_norm_attn_g.shape[1], out_norm_gate_g.shape[1]
    d = _Dims(seq, attn_w, gate_w)
    G = d.n_groups
    in_w = d.in_w
    dff = w_ffn_gate.shape[2] * N_CHIPS
    assert w_in.shape[2] * N_CHIPS == in_w and seq % BLK == 0 and attn_w % (2 * BLK) == 0

    pc = jnp.stack([2 * lax.axis_index("x") + lax.axis_index("y"), lax.axis_index("c")]).astype(jnp.int32)
    big = [w_in[0], w_out[0], w_ffn_gate[0], w_ffn_up[0], w_ffn_down[0]]
    names = ["in", "out", "gate", "up", "down"]
    started, behind = [], norm1_g
    for w, n in zip(big, names):
        send, recv, buf, behind = _ag_start(_cast_into(w, pc, "cast_" + n), behind, "ag_start_" + n)
        started.append((send, recv, buf))

    def gathered(k, after):
        send, recv, buf = started[k]
        buf = _ag_wait(buf, send, recv, after, "ag_wait_" + names[k])
        return _ag_pair(buf, pc, "ag_pair_" + names[k]).reshape((N_CHIPS,) + big[k].shape)

    qg2 = jnp.tile(q_norm_g, (1, 2))
    kg2 = jnp.tile(k_norm_g, (1, 2))
    lg, lb, wsp = gate_ln_g[0], gate_ln_b[0], w_spatial[0]
    bcol = jnp.broadcast_to(b_spatial[0][:, :, None], (G, BLK, BLK))

    xf = x.reshape(T, D)
    tgt = loss_target.reshape(T, D)
    h1 = _rms_fwd(xf, norm1_g, "norm1_fwd", after=behind)
    win_full = jnp.transpose(gathered(0, h1), (1, 0, 2)).reshape(D, in_w)
    proj = _matmul(h1, win_full, "nn", F32, "proj_fwd", tm=1024, tn=1664)
    ya, yg, yn = _mixer_fwd(d, proj, attn_sinks, qg2, kg2, lg, lb, wsp, bcol, out_norm_attn_g, out_norm_gate_g)
    wout_full = gathered(1, yn).reshape(attn_w + gate_w, D)
    x1 = _matmul(yn, wout_full, "nn", F32, "out_fwd", tm=1024, tn=1024, add=xf)
    h2 = _rms_fwd(x1, norm2_g, "norm2_fwd")
    wg_g, wu_g = gathered(2, h2), gathered(3, h2)
    a, b, f = _ffn_up(h2, wg_g, wu_g)
    wd_full = gathered(4, f).reshape(dff, D)
    dx2, dx2b, loss_local = _ffn_down_loss(f, wd_full, x1, tgt)

    def reduce_start(g, n):
        g4 = g.reshape(N_CHIPS, 2, g.shape[1] // 2, g.shape[2])
        pair = _add_pair(g4, _swap_halves(g4, pc, "rs_swap_" + n), pc, "rs_add_pair_" + n)
        return _rs_start(pair, g, "rs_start_" + n)

    reducing = {}
    g_d = _matmul(f, dx2b, "tn", BF16, "ffn_down_dw", tm=1408, tn=1024, tk=2048, out_slab="r")
    reducing["down"] = reduce_start(g_d, "down")
    da, db = _ffn_down_dx(dx2b, wd_full, a, b, reducing["down"][4])
    g_g = _matmul(h2, da, "tn", BF16, "ffn_gate_dw", tm=1024, tn=1408, tk=2048, out_slab="c")
    reducing["gate"] = reduce_start(g_g, "gate")
    g_u = _matmul(h2, db, "tn", BF16, "ffn_up_dw", tm=1024, tn=1408, tk=2048, out_slab="c",
                  after=reducing["gate"][4])
    reducing["up"] = reduce_start(g_u, "up")
    dh2 = _matmul(da, wg_g, "nt", F32, "ffn_gate_dx", tm=1024, tn=1024, tk=1408, b_slab="k",
                  after=reducing["up"][4])
    dh2 = _matmul(db, wu_g, "nt", F32, "ffn_up_dx", tm=1024, tn=1024, tk=1408, b_slab="k", add=dh2)
    dx1, dx1b, dg_norm2 = _rms_bwd(x1, norm2_g, dh2, dx2, "norm2_bwd", True)
    g_o = _matmul(yn, dx1b, "tn", BF16, "out_dw", tm=512, tn=1024, tk=2048, out_slab="r")
    reducing["out"] = reduce_start(g_o, "out")
    dy = _matmul(dx1b, wout_full, "nt", F32, "out_dx", tm=1024, tn=1024, after=reducing["out"][4])
    (dproj, dkv, dqg, dkg, dsk, dlg, dlb, dwsp, dbsp, dgoa, dgog) = _mixer_bwd(
        d, proj, ya, yg, dy, attn_sinks, qg2, kg2, lg, lb, wsp, bcol, out_norm_attn_g, out_norm_gate_g)
    dproj = _put_kv(d, dproj, dkv)
    g_in_full = _matmul(h1, dproj, "tn", BF16, "proj_dw", tm=1024, tn=1664, tk=2048)
    g_i = jnp.transpose(g_in_full.reshape(D, N_CHIPS, in_w // N_CHIPS), (1, 0, 2))
    reducing["in"] = reduce_start(g_i, "in")
    dh1 = _matmul(dproj, win_full, "nt", F32, "proj_dx", tm=1024, tn=1024, after=reducing["in"][4])
    dx, dg_norm1 = _rms_bwd(xf, norm1_g, dh1, dx1, "norm1_bwd", False)

    dqg64 = dqg[:, :HEAD_DIM] + dqg[:, HEAD_DIM:]
    dkg64 = dkg[:, :HEAD_DIM] + dkg[:, HEAD_DIM:]
    small_g_local = [dg_norm1, dqg64, dkg64, dsk[:, :d.n_heads], dlg, dlb, dwsp, dbsp, dgoa, dgog, dg_norm2]
    ar_send, ar_recv, ar_own, ar_land, ar_token = _ar_start(_pack(small_g_local))

    big_m = [m_w_in[0], m_w_out[0], m_w_ffn_gate[0], m_w_ffn_up[0], m_w_ffn_down[0]]
    big_v = [v_w_in[0], v_w_out[0], v_w_ffn_gate[0], v_w_ffn_up[0], v_w_ffn_down[0]]
    big_grads, big_d, big_nm, big_nv = [], [], [], []
    for w, m, v, n in zip(big, big_m, big_v, names):
        send, recv, pair, land, _ = reducing[n]
        pair, land = _rs_wait(pair, land, send, recv, ar_token, "rs_wait_" + n)
        own, got = _add_chips(pair, land, pc, "rs_add_chips_" + n)
        outs = _adamw_halves(w, own, got, m, v, pc, "adamw_" + n)
        for lst, o in zip((big_grads, big_d, big_nm, big_nv), outs):
            lst.append(o.reshape(w.shape))

    small_names_w = [norm1_g, q_norm_g, k_norm_g, attn_sinks, gate_ln_g, gate_ln_b, w_spatial, b_spatial,
                     out_norm_attn_g, out_norm_gate_g, norm2_g]
    small_m = [m_norm1_g, m_q_norm_g, m_k_norm_g, m_attn_sinks, m_gate_ln_g, m_gate_ln_b, m_w_spatial, m_b_spatial,
               m_out_norm_attn_g, m_out_norm_gate_g, m_norm2_g]
    small_v = [v_norm1_g, v_q_norm_g, v_k_norm_g, v_attn_sinks, v_gate_ln_g, v_gate_ln_b, v_w_spatial, v_b_spatial,
               v_out_norm_attn_g, v_out_norm_gate_g, v_norm2_g]
    shapes = [w.shape for w in small_names_w]
    ar_own, ar_land = _ar_wait(ar_own, ar_land, ar_send, ar_recv, big_nv[-1])
    me = (4 * lax.axis_index("x") + 2 * lax.axis_index("y") + lax.axis_index("c")).astype(jnp.int32).reshape(1)
    sg = _ar_sum(ar_own, ar_land, me)
    sd, snm, snv = _adamw(_pack(small_names_w), sg, _pack(small_m), _pack(small_v), "adamw_small")
    small_g, small_d, small_nm, small_nv = (_unpack(t, shapes) for t in (sg, sd, snm, snv))

    loss = lax.psum(loss_local[0, 0], ("x", "y", "c"))

    def order(small, bigs):
        s = list(small)
        bg = [t[None] for t in bigs]
        return [s[0], bg[0], s[1], s[2], s[3], s[4], s[5], s[6], s[7], s[8], s[9], bg[1], s[10], bg[2], bg[3], bg[4]]

    grad_x = dx.reshape(bl, seq, D)
    return (loss, grad_x, *order(small_g, big_grads), *order(small_d, big_d), *order(small_nm, big_nm),
            *order(small_nv, big_nv))
```

```python
import functools

import jax
import jax.numpy as jnp
from jax import lax
from jax.experimental import pallas as pl
from jax.experimental.pallas import tpu as pltpu

F32 = jnp.float32
BF16 = jnp.bfloat16
MESH = pl.DeviceIdType.MESH

EPS = 1e-6
HEAD_DIM = 64
N_KV_HEADS = 2
BLK = 128
N_CHIPS = 4
N_DEV = 8
NEG = -1e30

ADAM_LR = 0.001
ADAM_B1 = 0.9
ADAM_B2 = 0.999
ADAM_EPS = 1e-08
ADAM_WD = 0.01
ADAM_STEP = 10

VMEM_LIMIT = 56 * 1024 * 1024

NN = (((1,), (0,)), ((), ()))
NT = (((1,), (1,)), ((), ()))
TN = (((0,), (0,)), ((), ()))
HBM = pl.BlockSpec(memory_space=pltpu.HBM)
ANY = pl.BlockSpec(memory_space=pl.ANY)
SEM = pl.BlockSpec(memory_space=pltpu.SEMAPHORE)
EFFECT = pltpu.SideEffectType.DATAFLOW_SIDE_EFFECTING


def _dot(a, b, dn):
    return lax.dot_general(a, b, dn, preferred_element_type=F32)


def _pick(dim, pref, align=128):
    if dim <= pref:
        return dim
    t = (pref // align) * align
    while t >= align:
        if dim % t == 0:
            return t
        t -= align
    return dim


def _params(sem):
    return pltpu.CompilerParams(dimension_semantics=sem, vmem_limit_bytes=VMEM_LIMIT)


MM_CHUNK = 512


def _col_chunks(tn):
    return [slice(c0, min(c0 + MM_CHUNK, tn)) for c0 in range(0, tn, MM_CHUNK)]


def _mm_body(dn, nk, has_add, has_after, *refs):
    a_ref, b_ref = refs[:2]
    add_ref = refs[2] if has_add else None
    o_ref = refs[2 + has_add + has_after]
    chunks = _col_chunks(o_ref.shape[-1])

    def dot(cols):
        return _dot(a_ref[...], b_ref[cols, :] if dn == NT else b_ref[:, cols], dn)

    def finish(cols, r):
        if add_ref is not None:
            r = r + add_ref[:, cols]
        o_ref[:, cols] = r.astype(o_ref.dtype)

    if nk == 1:
        for cols in chunks:
            finish(cols, dot(cols))
        return
    acc_ref = refs[-1]
    k = pl.program_id(2)

    @pl.when(k == 0)
    def _():
        for cols in chunks:
            acc_ref[:, cols] = dot(cols)

    if nk > 2:
        @pl.when(jnp.logical_and(k > 0, k < nk - 1))
        def _():
            for cols in chunks:
                acc_ref[:, cols] += dot(cols)

    @pl.when(k == nk - 1)
    def _():
        for cols in chunks:
            finish(cols, acc_ref[:, cols] + dot(cols))


def _matmul(a, b, mode, out_dtype, name, *, tm, tn, tk=None, add=None, b_slab=None, out_slab=None, after=None):
    if mode == "nn":
        M, K = a.shape
        N = b.shape[0] * b.shape[2] if b_slab == "c" else b.shape[1]
    elif mode == "nt":
        M, K = a.shape
        N = b.shape[1] if b_slab == "k" else b.shape[0]
    else:
        K, M = a.shape
        N = b.shape[1]
    tk = K if tk is None else tk
    tm, tn, tk = _pick(M, tm), _pick(N, tn), _pick(K, tk)
    if b_slab == "c":
        tn = _pick(b.shape[2], tn)
    if b_slab == "k":
        tk = _pick(b.shape[2], tk)
    if out_slab == "c":
        tn = _pick(N // N_CHIPS, tn)
    if out_slab == "r":
        tm = _pick(M // N_CHIPS, tm)
    gm, gn, gk = M // tm, N // tn, K // tk

    if mode == "tn":
        a_spec = pl.BlockSpec((tk, tm), lambda j, i, k: (k, i))
        b_spec = pl.BlockSpec((tk, tn), lambda j, i, k: (k, j))
    else:
        a_spec = pl.BlockSpec((tm, tk), lambda j, i, k: (i, k))
        if b_slab == "c":
            per = b.shape[2] // tn
            b_spec = pl.BlockSpec((None, tk, tn), lambda j, i, k: (j // per, k, j % per))
        elif b_slab == "k":
            per = b.shape[2] // tk
            b_spec = pl.BlockSpec((None, tn, tk), lambda j, i, k: (k // per, j, k % per))
        elif mode == "nn":
            b_spec = pl.BlockSpec((tk, tn), lambda j, i, k: (k, j))
        else:
            b_spec = pl.BlockSpec((tn, tk), lambda j, i, k: (j, k))

    if out_slab == "c":
        per = (N // N_CHIPS) // tn
        o_spec = pl.BlockSpec((None, tm, tn), lambda j, i, k: (j // per, i, j % per))
        o_shape = jax.ShapeDtypeStruct((N_CHIPS, M, N // N_CHIPS), out_dtype)
    elif out_slab == "r":
        per = (M // N_CHIPS) // tm
        o_spec = pl.BlockSpec((None, tm, tn), lambda j, i, k: (i // per, i % per, j))
        o_shape = jax.ShapeDtypeStruct((N_CHIPS, M // N_CHIPS, N), out_dtype)
    else:
        o_spec = pl.BlockSpec((tm, tn), lambda j, i, k: (i, j))
        o_shape = jax.ShapeDtypeStruct((M, N), out_dtype)

    dn = {"nn": NN, "nt": NT, "tn": TN}[mode]
    in_specs = [a_spec, b_spec]
    args = [a, b]
    if add is not None:
        in_specs.append(pl.BlockSpec((tm, tn), lambda j, i, k: (i, j)))
        args.append(add)
    if after is not None:
        in_specs.append(ANY)
        args.append(after)
    return pl.pallas_call(
        functools.partial(_mm_body, dn, gk, add is not None, after is not None),
        name=name,
        grid=(gn, gm, gk),
        in_specs=in_specs,
        out_specs=o_spec,
        out_shape=o_shape,
        scratch_shapes=[pltpu.VMEM((tm, tn), F32)] if gk > 1 else [],
        compiler_params=_params(("parallel", "parallel", "arbitrary")),
    )(*args)


def _rms_fwd_body(x_ref, g_ref, *rest):
    h_ref = rest[-1]
    x = x_ref[...]
    r = lax.rsqrt(jnp.mean(x * x, axis=-1, keepdims=True) + EPS)
    h_ref[...] = (x * r * g_ref[...]).astype(h_ref.dtype)


def _rms_fwd(x, g, name, after=None):
    T, D = x.shape
    tr = _pick(T, 256, 16)
    extra = [] if after is None else [after]
    return pl.pallas_call(
        functools.partial(_rms_fwd_body),
        name=name,
        grid=(T // tr,),
        in_specs=[pl.BlockSpec((tr, D), lambda i: (i, 0)), pl.BlockSpec((1, D), lambda i: (0, 0))] + [ANY] * len(extra),
        out_specs=pl.BlockSpec((tr, D), lambda i: (i, 0)),
        out_shape=jax.ShapeDtypeStruct((T, D), BF16),
        compiler_params=_params(("parallel",)),
    )(x, g, *extra)


def _rms_bwd_body(with_bf16, x_ref, g_ref, dh_ref, res_ref, dx_ref, *rest):
    dg_ref = rest[-1]

    @pl.when(pl.program_id(0) == 0)
    def _():
        dg_ref[...] = jnp.zeros_like(dg_ref)

    x = x_ref[...]
    r = lax.rsqrt(jnp.mean(x * x, axis=-1, keepdims=True) + EPS)
    xh = x * r
    dh = dh_ref[...]
    dg_ref[...] += jnp.sum(dh * xh, axis=0, keepdims=True)
    t = dh * g_ref[...]
    dx = res_ref[...] + r * (t - xh * jnp.mean(t * xh, axis=-1, keepdims=True))
    dx_ref[...] = dx
    if with_bf16:
        rest[0][...] = dx.astype(BF16)


def _rms_bwd(x, g, dh, res, name, with_bf16):
    T, D = x.shape
    tr = _pick(T, 256, 16)
    row = pl.BlockSpec((tr, D), lambda i: (i, 0))
    vec = pl.BlockSpec((1, D), lambda i: (0, 0))
    extra = [jax.ShapeDtypeStruct((T, D), BF16)] if with_bf16 else []
    return pl.pallas_call(
        functools.partial(_rms_bwd_body, with_bf16),
        name=name,
        grid=(T // tr,),
        in_specs=[row, vec, row, row],
        out_specs=[row] + [row] * len(extra) + [vec],
        out_shape=[jax.ShapeDtypeStruct((T, D), F32)] + extra + [jax.ShapeDtypeStruct((1, D), F32)],
        compiler_params=_params(("arbitrary",)),
    )(x, g, dh, res)


def _ffn_up_body(h_ref, wg_ref, wu_ref, a_ref, b_ref, f_ref):
    for cols in _col_chunks(a_ref.shape[-1]):
        a = _dot(h_ref[...], wg_ref[:, cols], NN)
        b = _dot(h_ref[...], wu_ref[:, cols], NN)
        a_ref[:, cols] = a
        b_ref[:, cols] = b
        f_ref[:, cols] = (a * (1.0 / (1.0 + jnp.exp(-a))) * b).astype(f_ref.dtype)


def _ffn_up(h, wg, wu):
    T, D = h.shape
    n, _, fs = wg.shape
    tm, tn = _pick(T, 512), fs
    hs = pl.BlockSpec((tm, D), lambda j, i: (i, 0))
    ws = pl.BlockSpec((None, D, tn), lambda j, i: (j, 0, 0))
    os = pl.BlockSpec((tm, tn), lambda j, i: (i, j))
    return pl.pallas_call(
        functools.partial(_ffn_up_body),
        name="ffn_up_fwd",
        grid=(n, T // tm),
        in_specs=[hs, ws, ws],
        out_specs=[os, os, os],
        out_shape=[jax.ShapeDtypeStruct((T, n * fs), F32), jax.ShapeDtypeStruct((T, n * fs), F32),
                   jax.ShapeDtypeStruct((T, n * fs), BF16)],
        compiler_params=_params(("parallel", "parallel")),
    )(h, wg, wu)


def _ffn_down_dx_body(dx_ref, wd_ref, a_ref, b_ref, after_ref, da_ref, db_ref):
    del after_ref
    for cols in _col_chunks(da_ref.shape[-1]):
        df = _dot(dx_ref[...], wd_ref[cols, :], NT)
        a = a_ref[:, cols]
        s = 1.0 / (1.0 + jnp.exp(-a))
        da_ref[:, cols] = (df * b_ref[:, cols] * (s * (1.0 + a * (1.0 - s)))).astype(da_ref.dtype)
        db_ref[:, cols] = (df * (a * s)).astype(db_ref.dtype)


def _ffn_down_dx(dx2b, wd, a, b, after):
    T, D = dx2b.shape
    F = wd.shape[0]
    tm, tn = _pick(T, 512), _pick(F, 1408)
    xs = pl.BlockSpec((tm, D), lambda j, i: (i, 0))
    ws = pl.BlockSpec((tn, D), lambda j, i: (j, 0))
    os = pl.BlockSpec((tm, tn), lambda j, i: (i, j))
    return pl.pallas_call(
        functools.partial(_ffn_down_dx_body),
        name="ffn_down_dx",
        grid=(F // tn, T // tm),
        in_specs=[xs, ws, os, os, ANY],
        out_specs=[os, os],
        out_shape=[jax.ShapeDtypeStruct((T, F), BF16), jax.ShapeDtypeStruct((T, F), BF16)],
        compiler_params=_params(("parallel", "parallel")),
    )(dx2b, wd, a, b, after)


def _ffn_down_loss_body(nk, inv_d, f_ref, wd_ref, x1_ref, tgt_ref, dx2_ref, dx2b_ref, loss_ref, *scratch):
    j, i, k = pl.program_id(0), pl.program_id(1), pl.program_id(2)
    chunks = _col_chunks(dx2_ref.shape[-1])

    def dot(cols):
        return _dot(f_ref[...], wd_ref[:, cols], NN)

    @pl.when(jnp.logical_and(jnp.logical_and(j == 0, i == 0), k == 0))
    def _():
        loss_ref[...] = jnp.zeros_like(loss_ref)

    def finish(ffn_of):
        total = jnp.zeros((1, 1), F32)
        for cols in chunks:
            e = ffn_of(cols) + x1_ref[:, cols] - tgt_ref[:, cols]
            dx2 = e * inv_d
            dx2_ref[:, cols] = dx2
            dx2b_ref[:, cols] = dx2.astype(BF16)
            total = total + jnp.sum(jnp.sum(e * e, axis=-1, keepdims=True), axis=0, keepdims=True)
        loss_ref[...] += (0.5 * inv_d) * total

    if nk == 1:
        finish(dot)
        return
    acc_ref = scratch[0]

    @pl.when(k == 0)
    def _():
        for cols in chunks:
            acc_ref[:, cols] = dot(cols)

    if nk > 2:
        @pl.when(jnp.logical_and(k > 0, k < nk - 1))
        def _():
            for cols in chunks:
                acc_ref[:, cols] += dot(cols)

    @pl.when(k == nk - 1)
    def _():
        finish(lambda cols: acc_ref[:, cols] + dot(cols))


def _ffn_down_loss(f, wd, x1, tgt):
    T, F = f.shape
    D = wd.shape[1]
    tm, tn, tk = _pick(T, 1024), _pick(D, 1024), _pick(F, 1408)
    gm, gn, gk = T // tm, D // tn, F // tk
    tile = pl.BlockSpec((tm, tn), lambda j, i, k: (i, j))
    return pl.pallas_call(
        functools.partial(_ffn_down_loss_body, gk, 1.0 / D),
        name="ffn_down_loss",
        grid=(gn, gm, gk),
        in_specs=[pl.BlockSpec((tm, tk), lambda j, i, k: (i, k)), pl.BlockSpec((tk, tn), lambda j, i, k: (k, j)),
                  tile, tile],
        out_specs=[tile, tile, pl.BlockSpec((1, 1), lambda j, i, k: (0, 0))],
        out_shape=[jax.ShapeDtypeStruct((T, D), F32), jax.ShapeDtypeStruct((T, D), BF16),
                   jax.ShapeDtypeStruct((1, 1), F32)],
        scratch_shapes=[pltpu.VMEM((tm, tn), F32)] if gk > 1 else [],
        compiler_params=_params(("arbitrary", "arbitrary", "arbitrary")),
    )(f, wd, x1, tgt)


def _lo_mask(shape):
    return lax.broadcasted_iota(jnp.int32, shape, len(shape) - 1) < HEAD_DIM


def _half_sums(t, lo):
    s_lo = jnp.sum(jnp.where(lo, t, 0.0), axis=-1, keepdims=True)
    s_hi = jnp.sum(jnp.where(lo, 0.0, t), axis=-1, keepdims=True)
    return jnp.where(lo, s_lo, s_hi)


def _head_rstd(t, lo):
    return lax.rsqrt(_half_sums(t * t, lo) * (1.0 / HEAD_DIM) + EPS)


def _place(t, lo, kv_head):
    if kv_head == 0:
        t_lo = jnp.where(lo, t, 0.0)
        t_hi = pltpu.roll(t_lo, HEAD_DIM, 1)
    else:
        t_hi = jnp.where(lo, 0.0, t)
        t_lo = pltpu.roll(t_hi, HEAD_DIM, 1)
    return jnp.concatenate([t_lo, t_hi], axis=0).astype(BF16)


def _unplace(c0, c1, lo):
    return jnp.where(lo, c0 + pltpu.roll(c0, HEAD_DIM, 1), c1 + pltpu.roll(c1, HEAD_DIM, 1))


def _band(kv_cur, kv_prev, kg, lo2):
    kb = jnp.concatenate([kv_prev[:, :BLK], kv_cur[:, :BLK]], axis=0)
    vb = jnp.concatenate([kv_prev[:, BLK:], kv_cur[:, BLK:]], axis=0)
    rk = _head_rstd(kb, lo2)
    kn = kb * rk * kg
    kk = [_place(kn, lo2, h) for h in range(N_KV_HEADS)]
    vv = [_place(vb, lo2, h) for h in range(N_KV_HEADS)]
    return kb, rk, kk, vv


def _score_geometry(first_i32):
    qi = lax.broadcasted_iota(jnp.int32, (BLK, 4 * BLK), 0)
    col = lax.broadcasted_iota(jnp.int32, (BLK, 4 * BLK), 1)
    kj = col & (2 * BLK - 1)
    dist = qi + BLK - kj
    valid = (dist >= 0) & (dist < BLK) & (kj >= first_i32 * BLK)
    return col, dist.astype(F32), valid


def _pair_probs(qn, kk, col, distf, valid, slope0, slope1, sink0, sink1):
    s = _dot(qn.astype(BF16), kk, NT) * (HEAD_DIM ** -0.5)
    slope = jnp.where(col < 2 * BLK, slope0, slope1)
    logits = jnp.where(valid, s - slope * distf, NEG)
    probs, psink = [], []
    for hh, sk in ((0, sink0), (1, sink1)):
        l = logits[:, 2 * BLK * hh:2 * BLK * (hh + 1)]
        m = jnp.maximum(jnp.max(l, axis=-1, keepdims=True), sk)
        p = jnp.exp(l - m)
        es = jnp.exp(sk - m)
        inv = 1.0 / (jnp.sum(p, axis=-1, keepdims=True) + es)
        probs.append(p * inv)
        psink.append(es * inv)
    return probs, psink


def _gelu(z):
    return 0.5 * z * (1.0 + lax.erf(z * (0.5 ** 0.5)))


def _gelu_grad(z):
    return 0.5 * (1.0 + lax.erf(z * (0.5 ** 0.5))) + z * jnp.exp(-0.5 * z * z) * ((2.0 * jnp.pi) ** -0.5)


def _tril_w(w):
    r = lax.broadcasted_iota(jnp.int32, (BLK, BLK), 0)
    c = lax.broadcasted_iota(jnp.int32, (BLK, BLK), 1)
    return jnp.where(r >= c, w, 0.0), r >= c


def _gate_fwd_group(zu, zv, lg, lb, w, bcol):
    u = _gelu(zu)
    v = _gelu(zv)
    mu = jnp.mean(v, axis=-1, keepdims=True)
    vc = v - mu
    rs = lax.rsqrt(jnp.mean(vc * vc, axis=-1, keepdims=True) + EPS)
    vh = vc * rs
    vn = vh * lg + lb
    wt, tril = _tril_w(w)
    mixed = _dot(wt.astype(BF16), vn.astype(BF16), NN) + bcol
    return u, vh, rs, vn, wt, tril, mixed


class _Dims:
    def __init__(self, seq, attn_w, gate_w):
        self.seq, self.attn_w, self.gate_w = seq, attn_w, gate_w
        self.n_heads = attn_w // HEAD_DIM
        self.group = self.n_heads // N_KV_HEADS
        self.n_pairs = attn_w // BLK
        self.n_groups = gate_w // BLK
        self.kv_col = attn_w // (2 * BLK)
        self.u0 = attn_w + 2 * BLK
        self.v0 = self.u0 + gate_w
        self.in_w = self.v0 + gate_w
        self.slopes = [2.0 ** (-8.0 * (h + 1) / self.n_heads) for h in range(self.n_heads)]


def _mixer_fwd_body(d, sink_ref, proj_ref, kvp_ref, qg_ref, kg_ref, lg_ref, lb_ref, w_ref, b_ref, goa_ref, gog_ref,
                    ya_ref, yg_ref, y_ref):
    i = pl.program_id(0)
    first = (i % (d.seq // BLK) == 0).astype(jnp.int32)
    lo = _lo_mask((BLK, BLK))
    lo2 = _lo_mask((2 * BLK, BLK))
    kv_cur = proj_ref[:, d.attn_w:d.attn_w + 2 * BLK]
    _, _, kk, vv = _band(kv_cur, kvp_ref[...], kg_ref[...], lo2)
    col, distf, valid = _score_geometry(first)
    qg = qg_ref[...]
    for j in range(d.n_pairs):
        h0, h1 = 2 * j, 2 * j + 1
        kh = h0 // d.group
        q2 = proj_ref[:, BLK * j:BLK * (j + 1)]
        qn = q2 * _head_rstd(q2, lo) * qg
        probs, _ = _pair_probs(qn, kk[kh], col, distf, valid, d.slopes[h0], d.slopes[h1],
                               sink_ref[0, h0], sink_ref[0, h1])
        p = jnp.concatenate(probs, axis=1).astype(BF16)
        ya_ref[:, BLK * j:BLK * (j + 1)] = _dot(p, vv[kh], NN)
    for g in range(d.n_groups):
        zu = proj_ref[:, d.u0 + BLK * g:d.u0 + BLK * (g + 1)]
        zv = proj_ref[:, d.v0 + BLK * g:d.v0 + BLK * (g + 1)]
        u, _, _, _, _, _, mixed = _gate_fwd_group(zu, zv, lg_ref[g:g + 1, :], lb_ref[g:g + 1, :], w_ref[g], b_ref[g])
        yg_ref[:, BLK * g:BLK * (g + 1)] = u * mixed
    ya = ya_ref[...]
    ra = lax.rsqrt(jnp.mean(ya * ya, axis=-1, keepdims=True) + EPS)
    y_ref[:, :d.attn_w] = (ya * ra * goa_ref[...]).astype(y_ref.dtype)
    yg = yg_ref[...]
    rg = lax.rsqrt(jnp.mean(yg * yg, axis=-1, keepdims=True) + EPS)
    y_ref[:, d.attn_w:] = (yg * rg * gog_ref[...]).astype(y_ref.dtype)


def _mixer_specs(d, T):
    row = lambda w: pl.BlockSpec((BLK, w), lambda i: (i, 0))
    const2 = lambda a: pl.BlockSpec(a.shape, lambda i: (0, 0))
    const3 = lambda a: pl.BlockSpec(a.shape, lambda i: (0, 0, 0))
    kv_prev = pl.BlockSpec((BLK, 2 * BLK), lambda i: (jnp.maximum(i - 1, 0), d.kv_col))
    return row, const2, const3, kv_prev


def _mixer_fwd(d, proj, sinks, qg2, kg2, lg, lb, wsp, bcol, goa, gog):
    T = proj.shape[0]
    row, const2, const3, kv_prev = _mixer_specs(d, T)
    return pl.pallas_call(
        functools.partial(_mixer_fwd_body, d),
        name="mixer_fwd",
        grid=(T // BLK,),
        in_specs=[pl.BlockSpec(memory_space=pltpu.SMEM), row(d.in_w), kv_prev, const2(qg2), const2(kg2),
                  const2(lg), const2(lb), const3(wsp), const3(bcol), const2(goa), const2(gog)],
        out_specs=[row(d.attn_w), row(d.gate_w), row(d.attn_w + d.gate_w)],
        out_shape=[jax.ShapeDtypeStruct((T, d.attn_w), F32), jax.ShapeDtypeStruct((T, d.gate_w), F32),
                   jax.ShapeDtypeStruct((T, d.attn_w + d.gate_w), BF16)],
        compiler_params=_params(("parallel",)),
    )(sinks, proj, proj, qg2, kg2, lg, lb, wsp, bcol, goa, gog)


def _mixer_bwd_body(d, sink_ref, proj_ref, kvp_ref, ya_ref, yg_ref, dy_ref, qg_ref, kg_ref, lg_ref, lb_ref, w_ref,
                    b_ref, goa_ref, gog_ref,
                    dproj_ref, dkv_ref, dqg_ref, dkg_ref, dsk_ref, dlg_ref, dlb_ref, dw_ref, db_ref, dgoa_ref,
                    dgog_ref):
    i = pl.program_id(0)

    @pl.when(i == 0)
    def _():
        for r in (dqg_ref, dkg_ref, dsk_ref, dlg_ref, dlb_ref, dw_ref, db_ref, dgoa_ref, dgog_ref):
            r[...] = jnp.zeros_like(r)

    first = (i % (d.seq // BLK) == 0).astype(jnp.int32)
    lo = _lo_mask((BLK, BLK))
    lo2 = _lo_mask((2 * BLK, BLK))
    lane_row = lax.broadcasted_iota(jnp.int32, (1, BLK), 1)

    ya = ya_ref[...]
    ra = lax.rsqrt(jnp.mean(ya * ya, axis=-1, keepdims=True) + EPS)
    yah = ya * ra
    dyn = dy_ref[:, :d.attn_w]
    dgoa_ref[...] += jnp.sum(dyn * yah, axis=0, keepdims=True)
    t = dyn * goa_ref[...]
    dya = ra * (t - yah * jnp.mean(t * yah, axis=-1, keepdims=True))
    yg = yg_ref[...]
    rg = lax.rsqrt(jnp.mean(yg * yg, axis=-1, keepdims=True) + EPS)
    ygh = yg * rg
    dyn = dy_ref[:, d.attn_w:]
    dgog_ref[...] += jnp.sum(dyn * ygh, axis=0, keepdims=True)
    t = dyn * gog_ref[...]
    dyg = rg * (t - ygh * jnp.mean(t * ygh, axis=-1, keepdims=True))

    kv_cur = proj_ref[:, d.attn_w:d.attn_w + 2 * BLK]
    kg = kg_ref[...]
    kb, rk, kk, vv = _band(kv_cur, kvp_ref[...], kg, lo2)
    col, distf, valid = _score_geometry(first)
    qg = qg_ref[...]
    ck = [jnp.zeros((2 * BLK, BLK), F32) for _ in range(N_KV_HEADS)]
    cv = [jnp.zeros((2 * BLK, BLK), F32) for _ in range(N_KV_HEADS)]
    dsk = jnp.zeros((1, BLK), F32)
    dqg = jnp.zeros((1, BLK), F32)
    for j in range(d.n_pairs):
        h0, h1 = 2 * j, 2 * j + 1
        kh = h0 // d.group
        cols = slice(BLK * j, BLK * (j + 1))
        q2 = proj_ref[:, cols]
        rq = _head_rstd(q2, lo)
        qh = q2 * rq
        qn = qh * qg
        probs, psink = _pair_probs(qn, kk[kh], col, distf, valid, d.slopes[h0], d.slopes[h1],
                                   sink_ref[0, h0], sink_ref[0, h1])
        do2 = dya[:, cols]
        prod = do2 * ya[:, cols]
        delta = (jnp.sum(jnp.where(lo, prod, 0.0), axis=-1, keepdims=True),
                 jnp.sum(jnp.where(lo, 0.0, prod), axis=-1, keepdims=True))
        do2b = do2.astype(BF16)
        dp = _dot(do2b, vv[kh], NT)
        ds = []
        for hh in (0, 1):
            ds.append(probs[hh] * (dp[:, 2 * BLK * hh:2 * BLK * (hh + 1)] - delta[hh]))
            dsink = -jnp.sum(psink[hh] * delta[hh], axis=0, keepdims=True)
            dsk = dsk + jnp.where(lane_row == (h0 + hh), dsink, 0.0)
        dsb = (jnp.concatenate(ds, axis=1) * (HEAD_DIM ** -0.5)).astype(BF16)
        pb = jnp.concatenate(probs, axis=1).astype(BF16)
        qnb = qn.astype(BF16)
        dqn = _dot(dsb, kk[kh], NN)
        dkk = _dot(dsb, qnb, TN)
        dvv = _dot(pb, do2b, TN)
        ck[kh] = ck[kh] + jnp.where(lo2, dkk[:2 * BLK], 0.0) + jnp.where(lo2, 0.0, dkk[2 * BLK:])
        cv[kh] = cv[kh] + jnp.where(lo2, dvv[:2 * BLK], 0.0) + jnp.where(lo2, 0.0, dvv[2 * BLK:])
        dqg = dqg + jnp.sum(dqn * qh, axis=0, keepdims=True)
        t = dqn * qg
        dq2 = rq * (t - qh * (_half_sums(t * qh, lo) * (1.0 / HEAD_DIM)))
        dproj_ref[:, cols] = dq2.astype(dproj_ref.dtype)
    dsk_ref[...] += dsk
    dqg_ref[...] += dqg
    dkn = _unplace(ck[0], ck[1], lo2)
    dvb = _unplace(cv[0], cv[1], lo2)
    khat = kb * rk
    dkg_ref[...] += jnp.sum(dkn * khat, axis=0, keepdims=True)
    t = dkn * kg
    dkb = rk * (t - khat * (_half_sums(t * khat, lo2) * (1.0 / HEAD_DIM)))
    rows_cur = pl.ds(pl.multiple_of(i * BLK, BLK), BLK)
    rows_prev = pl.ds(pl.multiple_of(jnp.maximum(i - 1, 0) * BLK, BLK), BLK)
    dkv_ref[rows_cur, :] = jnp.concatenate([dkb[BLK:], dvb[BLK:]], axis=1)
    dkv_ref[rows_prev, :] += jnp.concatenate([dkb[:BLK], dvb[:BLK]], axis=1)
    dproj_ref[:, d.attn_w:d.attn_w + 2 * BLK] = jnp.zeros((BLK, 2 * BLK), dproj_ref.dtype)

    for g in range(d.n_groups):
        ucols = slice(d.u0 + BLK * g, d.u0 + BLK * (g + 1))
        vcols = slice(d.v0 + BLK * g, d.v0 + BLK * (g + 1))
        zu = proj_ref[:, ucols]
        zv = proj_ref[:, vcols]
        lg = lg_ref[g:g + 1, :]
        u, vh, rs, vn, wt, tril, mixed = _gate_fwd_group(zu, zv, lg, lb_ref[g:g + 1, :], w_ref[g], b_ref[g])
        dyg_g = dyg[:, BLK * g:BLK * (g + 1)]
        du = dyg_g * mixed
        dmix = dyg_g * u
        dmb = dmix.astype(BF16)
        db_ref[g:g + 1, :] += jnp.sum(jnp.transpose(dmix), axis=0, keepdims=True)
        dw_ref[g] += jnp.where(tril, _dot(dmb, vn.astype(BF16), NT), 0.0)
        dvn = _dot(wt.astype(BF16), dmb, TN)
        dlg_ref[g:g + 1, :] += jnp.sum(dvn * vh, axis=0, keepdims=True)
        dlb_ref[g:g + 1, :] += jnp.sum(dvn, axis=0, keepdims=True)
        dvh = dvn * lg
        dv = rs * (dvh - jnp.mean(dvh, axis=-1, keepdims=True) - vh * jnp.mean(dvh * vh, axis=-1, keepdims=True))
        dproj_ref[:, ucols] = (du * _gelu_grad(zu)).astype(dproj_ref.dtype)
        dproj_ref[:, vcols] = (dv * _gelu_grad(zv)).astype(dproj_ref.dtype)


def _mixer_bwd(d, proj, ya, yg, dy, sinks, qg2, kg2, lg, lb, wsp, bcol, goa, gog):
    T = proj.shape[0]
    row, const2, const3, kv_prev = _mixer_specs(d, T)
    acc2 = lambda s: pl.BlockSpec(s, lambda i: (0, 0))
    G = d.n_groups
    out_shapes = [((T, d.in_w), BF16), ((T, 2 * BLK), F32), ((1, BLK), F32), ((1, BLK), F32), ((1, BLK), F32),
                  ((G, BLK), F32), ((G, BLK), F32), ((G, BLK, BLK), F32), ((G, BLK), F32),
                  ((1, d.attn_w), F32), ((1, d.gate_w), F32)]
    out_specs = [row(d.in_w)] + [acc2(s) for s, _ in out_shapes[1:7]] + \
                [pl.BlockSpec((G, BLK, BLK), lambda i: (0, 0, 0))] + [acc2(s) for s, _ in out_shapes[8:]]
    return pl.pallas_call(
        functools.partial(_mixer_bwd_body, d),
        name="mixer_bwd",
        grid=(T // BLK,),
        in_specs=[pl.BlockSpec(memory_space=pltpu.SMEM), row(d.in_w), kv_prev, row(d.attn_w), row(d.gate_w),
                  row(d.attn_w + d.gate_w), const2(qg2), const2(kg2), const2(lg), const2(lb), const3(wsp),
                  const3(bcol), const2(goa), const2(gog)],
        out_specs=out_specs,
        out_shape=[jax.ShapeDtypeStruct(s, t) for s, t in out_shapes],
        compiler_params=_params(("arbitrary",)),
    )(sinks, proj, proj, ya, yg, dy, qg2, kg2, lg, lb, wsp, bcol, goa, gog)


def _put_kv_body(dkv_ref, dproj_in_ref, dproj_ref):
    del dproj_in_ref
    dproj_ref[...] = dkv_ref[...].astype(dproj_ref.dtype)


def _put_kv(d, dproj, dkv):
    T = dproj.shape[0]
    tr = _pick(T, 1024, 16)
    return pl.pallas_call(
        functools.partial(_put_kv_body),
        name="put_kv",
        grid=(T // tr,),
        in_specs=[pl.BlockSpec((tr, 2 * BLK), lambda i: (i, 0)), pl.BlockSpec(memory_space=pl.ANY)],
        out_specs=pl.BlockSpec((tr, 2 * BLK), lambda i: (i, d.kv_col)),
        out_shape=jax.ShapeDtypeStruct(dproj.shape, dproj.dtype),
        input_output_aliases={1: 0},
        compiler_params=_params(("parallel",)),
    )(dkv, dproj)


def _add_pair_body(pc_ref, own_ref, got_ref, o_ref):
    del pc_ref
    o_ref[...] = (own_ref[...].astype(F32) + got_ref[...].astype(F32)).astype(o_ref.dtype)


def _add_pair(g4, got, pc, name):
    n, _, h, C = g4.shape
    tr = _pick(h, 512, 16)
    return pl.pallas_call(
        functools.partial(_add_pair_body),
        name=name,
        grid_spec=pltpu.PrefetchScalarGridSpec(
            num_scalar_prefetch=1,
            grid=(n, h // tr),
            in_specs=[pl.BlockSpec((None, None, tr, C), lambda q, i, pc: (q, pc[1], i, 0)),
                      pl.BlockSpec((None, tr, C), lambda q, i, pc: (q, i, 0))],
            out_specs=pl.BlockSpec((None, tr, C), lambda q, i, pc: (q, i, 0)),
        ),
        out_shape=jax.ShapeDtypeStruct((n, h, C), g4.dtype),
        compiler_params=_params(("parallel", "parallel")),
    )(pc, g4, got)


def _adamw_update(w, g, m, v):
    m = ADAM_B1 * m + (1.0 - ADAM_B1) * g
    v = ADAM_B2 * v + (1.0 - ADAM_B2) * (g * g)
    m_hat = m / (1.0 - ADAM_B1 ** ADAM_STEP)
    v_hat = v / (1.0 - ADAM_B2 ** ADAM_STEP)
    return -ADAM_LR * (m_hat / (jnp.sqrt(v_hat) + ADAM_EPS) + ADAM_WD * w), m, v


def _adamw_body(w_ref, g_ref, m_ref, v_ref, d_ref, nm_ref, nv_ref):
    d_ref[...], nm_ref[...], nv_ref[...] = _adamw_update(w_ref[...], g_ref[...], m_ref[...], v_ref[...])


def _adamw(w, g, m, v, name):
    R, C = w.shape
    tr = _pick(R, 512, 8)
    blk = pl.BlockSpec((tr, C), lambda i: (i, 0))
    return pl.pallas_call(
        functools.partial(_adamw_body),
        name=name,
        grid=(R // tr,),
        in_specs=[blk] * 4,
        out_specs=[blk] * 3,
        out_shape=[jax.ShapeDtypeStruct((R, C), F32)] * 3,
        compiler_params=_params(("parallel",)),
    )(w, g, m, v)


def _adamw_halves_body(pc_ref, w_ref, own_ref, got_ref, m_ref, v_ref, g_ref, d_ref, nm_ref, nv_ref):
    mine = pl.program_id(0) == pc_ref[1]

    def update(g):
        g_ref[...] = g
        d_ref[...], nm_ref[...], nv_ref[...] = _adamw_update(w_ref[...], g, m_ref[...], v_ref[...])

    @pl.when(mine)
    def _():
        update(own_ref[...])

    @pl.when(jnp.logical_not(mine))
    def _():
        update(got_ref[...])


def _adamw_halves(w, own, got, m, v, pc, name):
    h, C = own.shape
    tr = _pick(h, 512, 8)
    full = pl.BlockSpec((None, tr, C), lambda hh, i, pc: (hh, i, 0))
    mine = pl.BlockSpec((tr, C), lambda hh, i, pc: (jnp.where(hh == pc[1], i, 0), 0))
    theirs = pl.BlockSpec((tr, C), lambda hh, i, pc: (jnp.where(hh == pc[1], 0, i), 0))
    return pl.pallas_call(
        functools.partial(_adamw_halves_body),
        name=name,
        grid_spec=pltpu.PrefetchScalarGridSpec(
            num_scalar_prefetch=1,
            grid=(2, h // tr),
            in_specs=[full, mine, theirs, full, full],
            out_specs=[full] * 4,
        ),
        out_shape=[jax.ShapeDtypeStruct((2, h, C), F32)] * 4,
        compiler_params=_params(("parallel", "parallel")),
    )(pc, w.reshape(2, h, C), own, got, m.reshape(2, h, C), v.reshape(2, h, C))


def _me():
    x, y, c = lax.axis_index("x"), lax.axis_index("y"), lax.axis_index("c")
    chips = [(1 - x, y), (x, 1 - y), (1 - x, 1 - y)]
    return x, y, c, chips


def _cast_into_body(pc_ref, w_ref, o_ref):
    del pc_ref
    o_ref[...] = w_ref[...].astype(o_ref.dtype)


def _cast_into(w, pc, name):
    Rs, C = w.shape
    h = Rs // 2
    tr = _pick(h, 512, 16)
    return pl.pallas_call(
        functools.partial(_cast_into_body),
        name=name,
        grid_spec=pltpu.PrefetchScalarGridSpec(
            num_scalar_prefetch=1,
            grid=(2, h // tr),
            in_specs=[pl.BlockSpec((None, tr, C), lambda hh, i, pc: (hh, i, 0))],
            out_specs=pl.BlockSpec((None, None, tr, C), lambda hh, i, pc: (pc[0], hh, i, 0)),
        ),
        out_shape=jax.ShapeDtypeStruct((N_CHIPS, 2, h, C), BF16),
        compiler_params=_params(("parallel", "parallel")),
    )(pc, w.reshape(2, h, C))


MAX_PIECES = 4


def _send_tile_to_sibling(src_of, dst_of, tr, dst_total, send_sems, recv_sem, last):
    x, y, c, _ = _me()
    pieces = MAX_PIECES if tr % (16 * MAX_PIECES) == 0 else (2 if tr % 32 == 0 else 1)
    n = tr // pieces
    copies = [pltpu.make_async_remote_copy(src_ref=src_of(k * n, n), dst_ref=dst_of(k * n, n), send_sem=send_sems.at[k],
                                           recv_sem=recv_sem, device_id=(x, y, 1 - c), device_id_type=MESH)
              for k in range(pieces)]
    for cp in copies:
        cp.start()
    for cp in copies:
        cp.wait_send()

    @pl.when(last)
    def _():
        pltpu.make_async_remote_copy(src_ref=dst_total, dst_ref=dst_total, send_sem=send_sems.at[0], recv_sem=recv_sem,
                                     device_id=(x, y, 1 - c), device_id_type=MESH).wait_recv()


TILE_SEMS = [pltpu.SemaphoreType.DMA((MAX_PIECES,)), pltpu.SemaphoreType.DMA(())]


def _ag_pair_body(tr, n_i, pc_ref, tile_ref, buf_ref, send_sem, recv_sem):
    j, i = pl.program_id(0), pl.program_id(1)
    q = pc_ref[0] ^ (j + 1)
    c = pc_ref[1]
    r_tile = pl.multiple_of(i * tr, tr)
    _send_tile_to_sibling(lambda r0, n: tile_ref.at[:, :, pl.ds(r0, n)],
                          lambda r0, n: buf_ref.at[pl.ds(q, 1), pl.ds(c, 1), pl.ds(r_tile + r0, n)], tr,
                          buf_ref.at[pl.ds(0, N_CHIPS - 1), 0], send_sem, recv_sem,
                          jnp.logical_and(j == N_CHIPS - 2, i == n_i - 1))


def _ag_pair(buf, pc, name):
    _, _, h, C = buf.shape
    tr = _pick(h, 512, 16)
    return pl.pallas_call(
        functools.partial(_ag_pair_body, tr, h // tr),
        name=name,
        grid_spec=pltpu.PrefetchScalarGridSpec(
            num_scalar_prefetch=1,
            grid=(N_CHIPS - 1, h // tr),
            in_specs=[pl.BlockSpec((1, 1, tr, C), lambda j, i, pc: (pc[0] ^ (j + 1), pc[1], i, 0))],
            out_specs=HBM,
            scratch_shapes=TILE_SEMS,
        ),
        out_shape=jax.ShapeDtypeStruct(buf.shape, buf.dtype),
        input_output_aliases={1: 0},
        compiler_params=_params(("arbitrary", "arbitrary")),
    )(pc, buf)


def _swap_halves_body(tr, n_q, n_i, pc_ref, tile_ref, got_ref, send_sem, recv_sem):
    del pc_ref
    q, i = pl.program_id(0), pl.program_id(1)
    r_tile = pl.multiple_of(i * tr, tr)
    _send_tile_to_sibling(lambda r0, n: tile_ref.at[:, :, pl.ds(r0, n)],
                          lambda r0, n: got_ref.at[pl.ds(q, 1), :, pl.ds(r_tile + r0, n)], tr, got_ref, send_sem, recv_sem,
                          jnp.logical_and(q == n_q - 1, i == n_i - 1))


def _swap_halves(g4, pc, name):
    n, _, h, C = g4.shape
    tr = _pick(h, 512, 16)
    return pl.pallas_call(
        functools.partial(_swap_halves_body, tr, n, h // tr),
        name=name,
        grid_spec=pltpu.PrefetchScalarGridSpec(
            num_scalar_prefetch=1,
            grid=(n, h // tr),
            in_specs=[pl.BlockSpec((1, 1, tr, C), lambda q, i, pc: (q, 1 - pc[1], i, 0))],
            out_specs=HBM,
            scratch_shapes=TILE_SEMS,
        ),
        out_shape=jax.ShapeDtypeStruct((n, 1, h, C), g4.dtype),
        compiler_params=_params(("arbitrary", "arbitrary")),
    )(pc, g4).reshape(n, h, C)


def _ici_copy(src, dst, send_sems, recv_sems, j, chip, c):
    return pltpu.make_async_remote_copy(src_ref=src, dst_ref=dst, send_sem=send_sems.at[j], recv_sem=recv_sems.at[j],
                                        device_id=(chip[0], chip[1], c), device_id_type=MESH)


def _token_spec():
    return jax.ShapeDtypeStruct((8, BLK), F32), pl.BlockSpec(memory_space=pltpu.VMEM)


def _ag_start_body(buf_ref, after_ref, send_sems, recv_sems, buf_thru, token_ref):
    del after_ref, buf_thru
    x, y, c, chips = _me()
    mine = buf_ref.at[2 * x + y, c]
    for j, chip in enumerate(chips):
        _ici_copy(mine, mine, send_sems, recv_sems, j, chip, c).start()
    token_ref[...] = jnp.zeros_like(token_ref)


def _ag_start(buf, after, name):
    tok_shape, tok_spec = _token_spec()
    sems = pltpu.SemaphoreType.DMA((N_CHIPS - 1,))
    return pl.pallas_call(
        functools.partial(_ag_start_body),
        name=name,
        in_specs=[HBM, ANY],
        out_specs=[SEM, SEM, HBM, tok_spec],
        out_shape=[sems, sems, pltpu.HBM(buf.shape, buf.dtype), tok_shape],
        input_output_aliases={0: 2},
        compiler_params=pltpu.CompilerParams(has_side_effects=EFFECT),
    )(pltpu.with_memory_space_constraint(buf, pltpu.HBM), after)


def _ag_wait_body(buf_ref, send_sems, recv_sems, after_ref, buf_out):
    del after_ref, buf_out
    x, y, c, chips = _me()
    mine = buf_ref.at[2 * x + y, c]
    for j, chip in enumerate(chips):
        theirs = buf_ref.at[2 * chip[0] + chip[1], c]
        _ici_copy(mine, mine, send_sems, recv_sems, j, chip, c).wait_send()
        _ici_copy(theirs, theirs, send_sems, recv_sems, j, chip, c).wait_recv()


def _ag_wait(buf, send_sems, recv_sems, after, name):
    return pl.pallas_call(
        functools.partial(_ag_wait_body),
        name=name,
        in_specs=[HBM, SEM, SEM, ANY],
        out_specs=HBM,
        out_shape=pltpu.HBM(buf.shape, buf.dtype),
        input_output_aliases={0: 0},
        compiler_params=pltpu.CompilerParams(has_side_effects=EFFECT),
    )(buf, send_sems, recv_sems, after)


def _rs_start_body(pair_ref, land_ref, after_ref, send_sems, recv_sems, pair_thru, land_thru, token_ref):
    del after_ref, pair_thru, land_thru
    x, y, c, chips = _me()
    for j, chip in enumerate(chips):
        _ici_copy(pair_ref.at[2 * chip[0] + chip[1]], land_ref.at[j], send_sems, recv_sems, j, chip, c).start()
    token_ref[...] = jnp.zeros_like(token_ref)


def _rs_start(pair, after, name):
    n, h, C = pair.shape
    tok_shape, tok_spec = _token_spec()
    sems = pltpu.SemaphoreType.DMA((N_CHIPS - 1,))
    land = pltpu.with_memory_space_constraint(lax.empty((N_CHIPS - 1, h, C), pair.dtype), pltpu.HBM)
    return pl.pallas_call(
        functools.partial(_rs_start_body),
        name=name,
        in_specs=[HBM, HBM, ANY],
        out_specs=[SEM, SEM, HBM, HBM, tok_spec],
        out_shape=[sems, sems, pltpu.HBM(pair.shape, pair.dtype), pltpu.HBM(land.shape, land.dtype), tok_shape],
        input_output_aliases={0: 2, 1: 3},
        compiler_params=pltpu.CompilerParams(has_side_effects=EFFECT),
    )(pltpu.with_memory_space_constraint(pair, pltpu.HBM), land, after)


def _rs_wait_body(pair_ref, land_ref, send_sems, recv_sems, after_ref, pair_out, land_out):
    del after_ref, pair_out, land_out
    x, y, c, chips = _me()
    for j, chip in enumerate(chips):
        _ici_copy(pair_ref.at[0], land_ref.at[j], send_sems, recv_sems, j, chip, c).wait_send()
        _ici_copy(pair_ref.at[0], land_ref.at[j], send_sems, recv_sems, j, chip, c).wait_recv()


def _rs_wait(pair, land, send_sems, recv_sems, after, name):
    return pl.pallas_call(
        functools.partial(_rs_wait_body),
        name=name,
        in_specs=[HBM, HBM, SEM, SEM, ANY],
        out_specs=[HBM, HBM],
        out_shape=[pltpu.HBM(pair.shape, pair.dtype), pltpu.HBM(land.shape, land.dtype)],
        input_output_aliases={0: 0, 1: 1},
        compiler_params=pltpu.CompilerParams(has_side_effects=EFFECT),
    )(pair, land, send_sems, recv_sems, after)


def _swap_copy(g4_ref, got_ref, send_sem, recv_sem):
    x, y, c, _ = _me()
    return pltpu.make_async_remote_copy(src_ref=g4_ref.at[:, 1 - c], dst_ref=got_ref, send_sem=send_sem,
                                        recv_sem=recv_sem, device_id=(x, y, 1 - c), device_id_type=MESH)


def _swap_start_body(g4_ref, got_ref, send_sem, recv_sem, g4_thru, got_thru, token_ref):
    del g4_thru, got_thru
    _swap_copy(g4_ref, got_ref, send_sem, recv_sem).start()
    token_ref[...] = jnp.zeros_like(token_ref)


def _swap_start(g4, name):
    n, _, h, C = g4.shape
    tok_shape, tok_spec = _token_spec()
    sem = pltpu.SemaphoreType.DMA(())
    got = pltpu.with_memory_space_constraint(lax.empty((n, h, C), g4.dtype), pltpu.HBM)
    return pl.pallas_call(
        functools.partial(_swap_start_body),
        name=name,
        in_specs=[HBM, HBM],
        out_specs=[SEM, SEM, HBM, HBM, tok_spec],
        out_shape=[sem, sem, pltpu.HBM(g4.shape, g4.dtype), pltpu.HBM(got.shape, got.dtype), tok_shape],
        input_output_aliases={0: 2, 1: 3},
        compiler_params=pltpu.CompilerParams(has_side_effects=EFFECT),
    )(pltpu.with_memory_space_constraint(g4, pltpu.HBM), got)


def _swap_wait_body(g4_ref, got_ref, send_sem, recv_sem, after_ref, g4_out, got_out):
    del after_ref, g4_out, got_out
    cp = _swap_copy(g4_ref, got_ref, send_sem, recv_sem)
    cp.wait_send()
    cp.wait_recv()


def _swap_wait(g4, got, send_sem, recv_sem, after, name):
    return pl.pallas_call(
        functools.partial(_swap_wait_body),
        name=name,
        in_specs=[HBM, HBM, SEM, SEM, ANY],
        out_specs=[HBM, HBM],
        out_shape=[pltpu.HBM(g4.shape, g4.dtype), pltpu.HBM(got.shape, got.dtype)],
        input_output_aliases={0: 0, 1: 1},
        compiler_params=pltpu.CompilerParams(has_side_effects=EFFECT),
    )(g4, got, send_sem, recv_sem, after)


def _add_chips_body(tr, n_i, pc_ref, own_ref, l0_ref, l1_ref, l2_ref, o_ref, got_ref, send_sems, recv_sem):
    del pc_ref
    i = pl.program_id(0)
    r = own_ref[...].astype(F32) + l0_ref[...].astype(F32)
    o_ref[...] = r + l1_ref[...].astype(F32) + l2_ref[...].astype(F32)
    r_tile = pl.multiple_of(i * tr, tr)
    _send_tile_to_sibling(lambda r0, n: o_ref.at[pl.ds(r0, n)], lambda r0, n: got_ref.at[pl.ds(r_tile + r0, n)], tr,
                          got_ref, send_sems, recv_sem, i == n_i - 1)


def _add_chips(pair, land, pc, name):
    _, h, C = pair.shape
    tr = _pick(h, 256, 16)
    slot = lambda j: pl.BlockSpec((None, tr, C), lambda i, pc: (j, i, 0))
    return pl.pallas_call(
        functools.partial(_add_chips_body, tr, h // tr),
        name=name,
        grid_spec=pltpu.PrefetchScalarGridSpec(
            num_scalar_prefetch=1,
            grid=(h // tr,),
            in_specs=[pl.BlockSpec((None, tr, C), lambda i, pc: (pc[0], i, 0)), slot(0), slot(1), slot(2)],
            out_specs=[pl.BlockSpec((tr, C), lambda i, pc: (i, 0)), HBM],
            scratch_shapes=TILE_SEMS,
        ),
        out_shape=[jax.ShapeDtypeStruct((h, C), F32), jax.ShapeDtypeStruct((h, C), F32)],
        compiler_params=_params(("arbitrary",)),
    )(pc, pair, land, land, land)


def _peer(r):
    x, y, c, _ = _me()
    return (x ^ ((r >> 2) & 1), y ^ ((r >> 1) & 1), c ^ (r & 1))


def _ar_start_body(x_ref, land_ref, send_sems, recv_sems, x_thru, land_thru, token_ref):
    del x_thru, land_thru
    for r in range(1, N_DEV):
        pltpu.make_async_remote_copy(src_ref=x_ref, dst_ref=land_ref.at[r - 1], send_sem=send_sems.at[r - 1],
                                     recv_sem=recv_sems.at[r - 1], device_id=_peer(r), device_id_type=MESH).start()
    token_ref[...] = jnp.zeros_like(token_ref)


def _ar_start(packed):
    tok_shape, tok_spec = _token_spec()
    sems = pltpu.SemaphoreType.DMA((N_DEV - 1,))
    land = pltpu.with_memory_space_constraint(lax.empty((N_DEV - 1,) + packed.shape, packed.dtype), pltpu.HBM)
    return pl.pallas_call(
        functools.partial(_ar_start_body),
        name="ar_start",
        in_specs=[HBM, HBM],
        out_specs=[SEM, SEM, HBM, HBM, tok_spec],
        out_shape=[sems, sems, pltpu.HBM(packed.shape, packed.dtype), pltpu.HBM(land.shape, land.dtype), tok_shape],
        input_output_aliases={0: 2, 1: 3},
        compiler_params=pltpu.CompilerParams(has_side_effects=EFFECT),
    )(pltpu.with_memory_space_constraint(packed, pltpu.HBM), land)


def _ar_wait_body(x_ref, land_ref, send_sems, recv_sems, after_ref, x_out, land_out):
    del after_ref, x_out, land_out
    for r in range(1, N_DEV):
        cp = pltpu.make_async_remote_copy(src_ref=x_ref, dst_ref=land_ref.at[r - 1], send_sem=send_sems.at[r - 1],
                                          recv_sem=recv_sems.at[r - 1], device_id=_peer(r), device_id_type=MESH)
        cp.wait_send()
        cp.wait_recv()


def _ar_wait(packed, land, send_sems, recv_sems, after):
    return pl.pallas_call(
        functools.partial(_ar_wait_body),
        name="ar_wait",
        in_specs=[HBM, HBM, SEM, SEM, ANY],
        out_specs=[HBM, HBM],
        out_shape=[pltpu.HBM(packed.shape, packed.dtype), pltpu.HBM(land.shape, land.dtype)],
        input_output_aliases={0: 0, 1: 1},
        compiler_params=pltpu.CompilerParams(has_side_effects=EFFECT),
    )(packed, land, send_sems, recv_sems, after)


def _ar_sum_body(me_ref, own_ref, *rest):
    o_ref = rest[N_DEV]
    acc = None
    for dev in range(N_DEV):
        term = jnp.where(me_ref[0] == dev, own_ref[...], rest[dev][...])
        acc = term if acc is None else acc + term
    o_ref[...] = acc


def _ar_sum(packed, land, me):
    R, C = packed.shape
    tr = _pick(R, 552, 8)
    own = pl.BlockSpec((tr, C), lambda i, me: (i, 0))
    slot = lambda dev: pl.BlockSpec((None, tr, C), lambda i, me: (jnp.maximum((dev ^ me[0]) - 1, 0), i, 0))
    return pl.pallas_call(
        functools.partial(_ar_sum_body),
        name="ar_sum",
        grid_spec=pltpu.PrefetchScalarGridSpec(
            num_scalar_prefetch=1,
            grid=(R // tr,),
            in_specs=[own] + [slot(dev) for dev in range(N_DEV)],
            out_specs=pl.BlockSpec((tr, C), lambda i, me: (i, 0)),
        ),
        out_shape=jax.ShapeDtypeStruct((R, C), F32),
        compiler_params=_params(("parallel",)),
    )(me, packed, *([land] * N_DEV))


def _pack(arrays):
    rows = []
    for a in arrays:
        flat = a.reshape(-1).astype(F32)
        pad = (-flat.shape[0]) % BLK
        rows.append(jnp.pad(flat, (0, pad)).reshape(-1, BLK))
    packed = jnp.concatenate(rows, axis=0)
    pad = (-packed.shape[0]) % 8
    return jnp.pad(packed, ((0, pad), (0, 0)))


def _unpack(packed, shapes):
    out, r = [], 0
    for s in shapes:
        n = 1
        for k in s:
            n *= k
        nr = -(-n // BLK)
        out.append(packed[r:r + nr].reshape(-1)[:n].reshape(s))
        r += nr
    return out


def kernel(x, norm1_g, w_in, q_norm_g, k_norm_g, attn_sinks, gate_ln_g, gate_ln_b, w_spatial, b_spatial, out_norm_attn_g, out_norm_gate_g, w_out, norm2_g, w_ffn_gate, w_ffn_up, w_ffn_down, loss_target, m_norm1_g, m_w_in, m_q_norm_g, m_k_norm_g, m_attn_sinks, m_gate_ln_g, m_gate_ln_b, m_w_spatial, m_b_spatial, m_out_norm_attn_g, m_out_norm_gate_g, m_w_out, m_norm2_g, m_w_ffn_gate, m_w_ffn_up, m_w_ffn_down, v_norm1_g, v_w_in, v_q_norm_g, v_k_norm_g, v_attn_sinks, v_gate_ln_g, v_gate_ln_b, v_w_spatial, v_b_spatial, v_out_norm_attn_g, v_out_norm_gate_g, v_w_out, v_norm2_g, v_w_ffn_gate, v_w_ffn_up, v_w_ffn_down):
    bl, seq, D = x.shape
    T = bl * seq
    attn_w, gate_w = out_norm_attn_g.shape[1], out_norm_gate_g.shape[1]
    d = _Dims(seq, attn_w, gate_w)
    G = d.n_groups
    in_w = d.in_w
    dff = w_ffn_gate.shape[2] * N_CHIPS
    assert w_in.shape[2] * N_CHIPS == in_w and seq % BLK == 0 and attn_w % (2 * BLK) == 0

    pc = jnp.stack([2 * lax.axis_index("x") + lax.axis_index("y"), lax.axis_index("c")]).astype(jnp.int32)
    big = [w_in[0], w_out[0], w_ffn_gate[0], w_ffn_up[0], w_ffn_down[0]]
    names = ["in", "out", "gate", "up", "down"]
    started, behind = [], norm1_g
    for w, n in zip(big, names):
        send, recv, buf, behind = _ag_start(_cast_into(w, pc, "cast_" + n), behind, "ag_start_" + n)
        started.append((send, recv, buf))

    def gathered(k, after):
        send, recv, buf = started[k]
        buf = _ag_wait(buf, send, recv, after, "ag_wait_" + names[k])
        return _ag_pair(buf, pc, "ag_pair_" + names[k]).reshape((N_CHIPS,) + big[k].shape)

    qg2 = jnp.tile(q_norm_g, (1, 2))
    kg2 = jnp.tile(k_norm_g, (1, 2))
    lg, lb, wsp = gate_ln_g[0], gate_ln_b[0], w_spatial[0]
    bcol = jnp.broadcast_to(b_spatial[0][:, :, None], (G, BLK, BLK))

    xf = x.reshape(T, D)
    tgt = loss_target.reshape(T, D)
    h1 = _rms_fwd(xf, norm1_g, "norm1_fwd", after=behind)
    win_full = jnp.transpose(gathered(0, h1), (1, 0, 2)).reshape(D, in_w)
    proj = _matmul(h1, win_full, "nn", F32, "proj_fwd", tm=1024, tn=1664)
    ya, yg, yn = _mixer_fwd(d, proj, attn_sinks, qg2, kg2, lg, lb, wsp, bcol, out_norm_attn_g, out_norm_gate_g)
    wout_full = gathered(1, yn).reshape(attn_w + gate_w, D)
    x1 = _matmul(yn, wout_full, "nn", F32, "out_fwd", tm=1024, tn=1024, add=xf)
    h2 = _rms_fwd(x1, norm2_g, "norm2_fwd")
    wg_g, wu_g = gathered(2, h2), gathered(3, h2)
    a, b, f = _ffn_up(h2, wg_g, wu_g)
    wd_full = gathered(4, f).reshape(dff, D)
    dx2, dx2b, loss_local = _ffn_down_loss(f, wd_full, x1, tgt)

    def swap_start(g, n):
        g4 = g.reshape(N_CHIPS, 2, g.shape[1] // 2, g.shape[2])
        return _swap_start(g4, "rs_swap_start_" + n)

    def reduce_start(swapping, n, after):
        send, recv, g4, got, _ = swapping
        g4, got = _swap_wait(g4, got, send, recv, after, "rs_swap_wait_" + n)
        return _rs_start(_add_pair(g4, got, pc, "rs_add_pair_" + n), got, "rs_start_" + n)

    reducing = {}
    g_d = _matmul(f, dx2b, "tn", BF16, "ffn_down_dw", tm=1408, tn=1024, tk=2048, out_slab="r")
    swap_d = swap_start(g_d, "down")
    da, db = _ffn_down_dx(dx2b, wd_full, a, b, swap_d[4])
    g_g = _matmul(h2, da, "tn", BF16, "ffn_gate_dw", tm=1024, tn=1408, tk=2048, out_slab="c")
    swap_g = swap_start(g_g, "gate")
    reducing["down"] = reduce_start(swap_d, "down", swap_g[4])
    g_u = _matmul(h2, db, "tn", BF16, "ffn_up_dw", tm=1024, tn=1408, tk=2048, out_slab="c",
                  after=reducing["down"][4])
    swap_u = swap_start(g_u, "up")
    reducing["gate"] = reduce_start(swap_g, "gate", swap_u[4])
    dh2 = _matmul(da, wg_g, "nt", F32, "ffn_gate_dx", tm=1024, tn=1024, tk=1408, b_slab="k",
                  after=reducing["gate"][4])
    dh2 = _matmul(db, wu_g, "nt", F32, "ffn_up_dx", tm=1024, tn=1024, tk=1408, b_slab="k", add=dh2)
    reducing["up"] = reduce_start(swap_u, "up", dh2)
    dx1, dx1b, dg_norm2 = _rms_bwd(x1, norm2_g, dh2, dx2, "norm2_bwd", True)
    g_o = _matmul(yn, dx1b, "tn", BF16, "out_dw", tm=512, tn=1024, tk=2048, out_slab="r",
                  after=reducing["up"][4])
    swap_o = swap_start(g_o, "out")
    dy = _matmul(dx1b, wout_full, "nt", F32, "out_dx", tm=1024, tn=1024, after=swap_o[4])
    (dproj, dkv, dqg, dkg, dsk, dlg, dlb, dwsp, dbsp, dgoa, dgog) = _mixer_bwd(
        d, proj, ya, yg, dy, attn_sinks, qg2, kg2, lg, lb, wsp, bcol, out_norm_attn_g, out_norm_gate_g)
    dproj = _put_kv(d, dproj, dkv)
    reducing["out"] = reduce_start(swap_o, "out", dproj)
    g_in_full = _matmul(h1, dproj, "tn", BF16, "proj_dw", tm=1024, tn=1664, tk=2048,
                        after=reducing["out"][4])
    g_i = jnp.transpose(g_in_full.reshape(D, N_CHIPS, in_w // N_CHIPS), (1, 0, 2))
    g4_i = g_i.reshape(N_CHIPS, 2, D // 2, in_w // N_CHIPS)
    pair_i = _add_pair(g4_i, _swap_halves(g4_i, pc, "rs_swap_in"), pc, "rs_add_pair_in")
    reducing["in"] = _rs_start(pair_i, g_i, "rs_start_in")
    dh1 = _matmul(dproj, win_full, "nt", F32, "proj_dx", tm=1024, tn=1024, after=reducing["in"][4])
    dx, dg_norm1 = _rms_bwd(xf, norm1_g, dh1, dx1, "norm1_bwd", False)

    dqg64 = dqg[:, :HEAD_DIM] + dqg[:, HEAD_DIM:]
    dkg64 = dkg[:, :HEAD_DIM] + dkg[:, HEAD_DIM:]
    small_g_local = [dg_norm1, dqg64, dkg64, dsk[:, :d.n_heads], dlg, dlb, dwsp, dbsp, dgoa, dgog, dg_norm2]
    ar_send, ar_recv, ar_own, ar_land, ar_token = _ar_start(_pack(small_g_local))

    big_m = [m_w_in[0], m_w_out[0], m_w_ffn_gate[0], m_w_ffn_up[0], m_w_ffn_down[0]]
    big_v = [v_w_in[0], v_w_out[0], v_w_ffn_gate[0], v_w_ffn_up[0], v_w_ffn_down[0]]
    big_grads, big_d, big_nm, big_nv = [], [], [], []
    for w, m, v, n in zip(big, big_m, big_v, names):
        send, recv, pair, land, _ = reducing[n]
        pair, land = _rs_wait(pair, land, send, recv, ar_token, "rs_wait_" + n)
        own, got = _add_chips(pair, land, pc, "rs_add_chips_" + n)
        outs = _adamw_halves(w, own, got, m, v, pc, "adamw_" + n)
        for lst, o in zip((big_grads, big_d, big_nm, big_nv), outs):
            lst.append(o.reshape(w.shape))

    small_names_w = [norm1_g, q_norm_g, k_norm_g, attn_sinks, gate_ln_g, gate_ln_b, w_spatial, b_spatial,
                     out_norm_attn_g, out_norm_gate_g, norm2_g]
    small_m = [m_norm1_g, m_q_norm_g, m_k_norm_g, m_attn_sinks, m_gate_ln_g, m_gate_ln_b, m_w_spatial, m_b_spatial,
               m_out_norm_attn_g, m_out_norm_gate_g, m_norm2_g]
    small_v = [v_norm1_g, v_q_norm_g, v_k_norm_g, v_attn_sinks, v_gate_ln_g, v_gate_ln_b, v_w_spatial, v_b_spatial,
               v_out_norm_attn_g, v_out_norm_gate_g, v_norm2_g]
    shapes = [w.shape for w in small_names_w]
    ar_own, ar_land = _ar_wait(ar_own, ar_land, ar_send, ar_recv, big_nv[-1])
    me = (4 * lax.axis_index("x") + 2 * lax.axis_index("y") + lax.axis_index("c")).astype(jnp.int32).reshape(1)
    sg = _ar_sum(ar_own, ar_land, me)
    sd, snm, snv = _adamw(_pack(small_names_w), sg, _pack(small_m), _pack(small_v), "adamw_small")
    small_g, small_d, small_nm, small_nv = (_unpack(t, shapes) for t in (sg, sd, snm, snv))

    loss = lax.psum(loss_local[0, 0], ("x", "y", "c"))

    def order(small, bigs):
        s = list(small)
        bg = [t[None] for t in bigs]
        return [s[0], bg[0], s[1], s[2], s[3], s[4], s[5], s[6], s[7], s[8], s[9], bg[1], s[10], bg[2], bg[3], bg[4]]

    grad_x = dx.reshape(bl, seq, D)
    return (loss, grad_x, *order(small_g, big_grads), *order(small_d, big_d), *order(small_nm, big_nm),
            *order(small_nv, big_nv))
```

```python
import functools

import jax
import jax.numpy as jnp
from jax import lax
from jax.experimental import pallas as pl
from jax.experimental.pallas import tpu as pltpu

F32 = jnp.float32
BF16 = jnp.bfloat16
MESH = pl.DeviceIdType.MESH

EPS = 1e-6
HEAD_DIM = 64
N_KV_HEADS = 2
BLK = 128
N_CHIPS = 4
N_DEV = 8
NEG = -1e30

ADAM_LR = 0.001
ADAM_B1 = 0.9
ADAM_B2 = 0.999
ADAM_EPS = 1e-08
ADAM_WD = 0.01
ADAM_STEP = 10

VMEM_LIMIT = 56 * 1024 * 1024

NN = (((1,), (0,)), ((), ()))
NT = (((1,), (1,)), ((), ()))
TN = (((0,), (0,)), ((), ()))
HBM = pl.BlockSpec(memory_space=pltpu.HBM)
ANY = pl.BlockSpec(memory_space=pl.ANY)
SEM = pl.BlockSpec(memory_space=pltpu.SEMAPHORE)
EFFECT = pltpu.SideEffectType.DATAFLOW_SIDE_EFFECTING


def _dot(a, b, dn):
    return lax.dot_general(a, b, dn, preferred_element_type=F32)


def _pick(dim, pref, align=128):
    if dim <= pref:
        return dim
    t = (pref // align) * align
    while t >= align:
        if dim % t == 0:
            return t
        t -= align
    return dim


def _params(sem):
    return pltpu.CompilerParams(dimension_semantics=sem, vmem_limit_bytes=VMEM_LIMIT)


MM_CHUNK = 512


def _col_chunks(tn):
    return [slice(c0, min(c0 + MM_CHUNK, tn)) for c0 in range(0, tn, MM_CHUNK)]


def _mm_body(dn, nk, has_add, has_after, *refs):
    a_ref, b_ref = refs[:2]
    add_ref = refs[2] if has_add else None
    o_ref = refs[2 + has_add + has_after]
    chunks = _col_chunks(o_ref.shape[-1])

    def dot(cols):
        return _dot(a_ref[...], b_ref[cols, :] if dn == NT else b_ref[:, cols], dn)

    def finish(cols, r):
        if add_ref is not None:
            r = r + add_ref[:, cols]
        o_ref[:, cols] = r.astype(o_ref.dtype)

    if nk == 1:
        for cols in chunks:
            finish(cols, dot(cols))
        return
    acc_ref = refs[-1]
    k = pl.program_id(2)

    @pl.when(k == 0)
    def _():
        for cols in chunks:
            acc_ref[:, cols] = dot(cols)

    if nk > 2:
        @pl.when(jnp.logical_and(k > 0, k < nk - 1))
        def _():
            for cols in chunks:
                acc_ref[:, cols] += dot(cols)

    @pl.when(k == nk - 1)
    def _():
        for cols in chunks:
            finish(cols, acc_ref[:, cols] + dot(cols))


def _matmul(a, b, mode, out_dtype, name, *, tm, tn, tk=None, add=None, b_slab=None, out_slab=None, after=None):
    if mode == "nn":
        M, K = a.shape
        N = b.shape[0] * b.shape[2] if b_slab == "c" else b.shape[1]
    elif mode == "nt":
        M, K = a.shape
        N = b.shape[1] if b_slab == "k" else b.shape[0]
    else:
        K, M = a.shape
        N = b.shape[1]
    tk = K if tk is None else tk
    tm, tn, tk = _pick(M, tm), _pick(N, tn), _pick(K, tk)
    if b_slab == "c":
        tn = _pick(b.shape[2], tn)
    if b_slab == "k":
        tk = _pick(b.shape[2], tk)
    if out_slab == "c":
        tn = _pick(N // N_CHIPS, tn)
    if out_slab == "r":
        tm = _pick(M // N_CHIPS, tm)
    gm, gn, gk = M // tm, N // tn, K // tk

    if mode == "tn":
        a_spec = pl.BlockSpec((tk, tm), lambda j, i, k: (k, i))
        b_spec = pl.BlockSpec((tk, tn), lambda j, i, k: (k, j))
    else:
        a_spec = pl.BlockSpec((tm, tk), lambda j, i, k: (i, k))
        if b_slab == "c":
            per = b.shape[2] // tn
            b_spec = pl.BlockSpec((None, tk, tn), lambda j, i, k: (j // per, k, j % per))
        elif b_slab == "k":
            per = b.shape[2] // tk
            b_spec = pl.BlockSpec((None, tn, tk), lambda j, i, k: (k // per, j, k % per))
        elif mode == "nn":
            b_spec = pl.BlockSpec((tk, tn), lambda j, i, k: (k, j))
        else:
            b_spec = pl.BlockSpec((tn, tk), lambda j, i, k: (j, k))

    if out_slab == "c":
        per = (N // N_CHIPS) // tn
        o_spec = pl.BlockSpec((None, tm, tn), lambda j, i, k: (j // per, i, j % per))
        o_shape = jax.ShapeDtypeStruct((N_CHIPS, M, N // N_CHIPS), out_dtype)
    elif out_slab == "r":
        per = (M // N_CHIPS) // tm
        o_spec = pl.BlockSpec((None, tm, tn), lambda j, i, k: (i // per, i % per, j))
        o_shape = jax.ShapeDtypeStruct((N_CHIPS, M // N_CHIPS, N), out_dtype)
    else:
        o_spec = pl.BlockSpec((tm, tn), lambda j, i, k: (i, j))
        o_shape = jax.ShapeDtypeStruct((M, N), out_dtype)

    dn = {"nn": NN, "nt": NT, "tn": TN}[mode]
    in_specs = [a_spec, b_spec]
    args = [a, b]
    if add is not None:
        in_specs.append(pl.BlockSpec((tm, tn), lambda j, i, k: (i, j)))
        args.append(add)
    if after is not None:
        in_specs.append(ANY)
        args.append(after)
    return pl.pallas_call(
        functools.partial(_mm_body, dn, gk, add is not None, after is not None),
        name=name,
        grid=(gn, gm, gk),
        in_specs=in_specs,
        out_specs=o_spec,
        out_shape=o_shape,
        scratch_shapes=[pltpu.VMEM((tm, tn), F32)] if gk > 1 else [],
        compiler_params=_params(("parallel", "parallel", "arbitrary")),
    )(*args)


def _rms_fwd_body(x_ref, g_ref, *rest):
    h_ref = rest[-1]
    x = x_ref[...]
    r = lax.rsqrt(jnp.mean(x * x, axis=-1, keepdims=True) + EPS)
    h_ref[...] = (x * r * g_ref[...]).astype(h_ref.dtype)


def _rms_fwd(x, g, name, after=None):
    T, D = x.shape
    tr = _pick(T, 256, 16)
    extra = [] if after is None else [after]
    return pl.pallas_call(
        functools.partial(_rms_fwd_body),
        name=name,
        grid=(T // tr,),
        in_specs=[pl.BlockSpec((tr, D), lambda i: (i, 0)), pl.BlockSpec((1, D), lambda i: (0, 0))] + [ANY] * len(extra),
        out_specs=pl.BlockSpec((tr, D), lambda i: (i, 0)),
        out_shape=jax.ShapeDtypeStruct((T, D), BF16),
        compiler_params=_params(("parallel",)),
    )(x, g, *extra)


def _rms_bwd_body(with_bf16, x_ref, g_ref, dh_ref, res_ref, dx_ref, *rest):
    dg_ref = rest[-1]

    @pl.when(pl.program_id(0) == 0)
    def _():
        dg_ref[...] = jnp.zeros_like(dg_ref)

    x = x_ref[...]
    r = lax.rsqrt(jnp.mean(x * x, axis=-1, keepdims=True) + EPS)
    xh = x * r
    dh = dh_ref[...]
    dg_ref[...] += jnp.sum(dh * xh, axis=0, keepdims=True)
    t = dh * g_ref[...]
    dx = res_ref[...] + r * (t - xh * jnp.mean(t * xh, axis=-1, keepdims=True))
    dx_ref[...] = dx
    if with_bf16:
        rest[0][...] = dx.astype(BF16)


def _rms_bwd(x, g, dh, res, name, with_bf16):
    T, D = x.shape
    tr = _pick(T, 256, 16)
    row = pl.BlockSpec((tr, D), lambda i: (i, 0))
    vec = pl.BlockSpec((1, D), lambda i: (0, 0))
    extra = [jax.ShapeDtypeStruct((T, D), BF16)] if with_bf16 else []
    return pl.pallas_call(
        functools.partial(_rms_bwd_body, with_bf16),
        name=name,
        grid=(T // tr,),
        in_specs=[row, vec, row, row],
        out_specs=[row] + [row] * len(extra) + [vec],
        out_shape=[jax.ShapeDtypeStruct((T, D), F32)] + extra + [jax.ShapeDtypeStruct((1, D), F32)],
        compiler_params=_params(("arbitrary",)),
    )(x, g, dh, res)


def _ffn_up_body(h_ref, wg_ref, wu_ref, a_ref, b_ref, f_ref):
    for cols in _col_chunks(a_ref.shape[-1]):
        a = _dot(h_ref[...], wg_ref[:, cols], NN)
        b = _dot(h_ref[...], wu_ref[:, cols], NN)
        a_ref[:, cols] = a
        b_ref[:, cols] = b
        f_ref[:, cols] = (a * (1.0 / (1.0 + jnp.exp(-a))) * b).astype(f_ref.dtype)


def _ffn_up(h, wg, wu):
    T, D = h.shape
    n, _, fs = wg.shape
    tm, tn = _pick(T, 512), fs
    hs = pl.BlockSpec((tm, D), lambda j, i: (i, 0))
    ws = pl.BlockSpec((None, D, tn), lambda j, i: (j, 0, 0))
    os = pl.BlockSpec((tm, tn), lambda j, i: (i, j))
    return pl.pallas_call(
        functools.partial(_ffn_up_body),
        name="ffn_up_fwd",
        grid=(n, T // tm),
        in_specs=[hs, ws, ws],
        out_specs=[os, os, os],
        out_shape=[jax.ShapeDtypeStruct((T, n * fs), F32), jax.ShapeDtypeStruct((T, n * fs), F32),
                   jax.ShapeDtypeStruct((T, n * fs), BF16)],
        compiler_params=_params(("parallel", "parallel")),
    )(h, wg, wu)


def _ffn_down_dx_body(dx_ref, wd_ref, a_ref, b_ref, after_ref, da_ref, db_ref):
    del after_ref
    for cols in _col_chunks(da_ref.shape[-1]):
        df = _dot(dx_ref[...], wd_ref[cols, :], NT)
        a = a_ref[:, cols]
        s = 1.0 / (1.0 + jnp.exp(-a))
        da_ref[:, cols] = (df * b_ref[:, cols] * (s * (1.0 + a * (1.0 - s)))).astype(da_ref.dtype)
        db_ref[:, cols] = (df * (a * s)).astype(db_ref.dtype)


def _ffn_down_dx(dx2b, wd, a, b, after):
    T, D = dx2b.shape
    F = wd.shape[0]
    tm, tn = _pick(T, 512), _pick(F, 1408)
    xs = pl.BlockSpec((tm, D), lambda j, i: (i, 0))
    ws = pl.BlockSpec((tn, D), lambda j, i: (j, 0))
    os = pl.BlockSpec((tm, tn), lambda j, i: (i, j))
    return pl.pallas_call(
        functools.partial(_ffn_down_dx_body),
        name="ffn_down_dx",
        grid=(F // tn, T // tm),
        in_specs=[xs, ws, os, os, ANY],
        out_specs=[os, os],
        out_shape=[jax.ShapeDtypeStruct((T, F), BF16), jax.ShapeDtypeStruct((T, F), BF16)],
        compiler_params=_params(("parallel", "parallel")),
    )(dx2b, wd, a, b, after)


def _ffn_down_loss_body(nk, inv_d, f_ref, wd_ref, x1_ref, tgt_ref, dx2_ref, dx2b_ref, loss_ref, *scratch):
    j, i, k = pl.program_id(0), pl.program_id(1), pl.program_id(2)
    chunks = _col_chunks(dx2_ref.shape[-1])

    def dot(cols):
        return _dot(f_ref[...], wd_ref[:, cols], NN)

    @pl.when(jnp.logical_and(jnp.logical_and(j == 0, i == 0), k == 0))
    def _():
        loss_ref[...] = jnp.zeros_like(loss_ref)

    def finish(ffn_of):
        total = jnp.zeros((1, 1), F32)
        for cols in chunks:
            e = ffn_of(cols) + x1_ref[:, cols] - tgt_ref[:, cols]
            dx2 = e * inv_d
            dx2_ref[:, cols] = dx2
            dx2b_ref[:, cols] = dx2.astype(BF16)
            total = total + jnp.sum(jnp.sum(e * e, axis=-1, keepdims=True), axis=0, keepdims=True)
        loss_ref[...] += (0.5 * inv_d) * total

    if nk == 1:
        finish(dot)
        return
    acc_ref = scratch[0]

    @pl.when(k == 0)
    def _():
        for cols in chunks:
            acc_ref[:, cols] = dot(cols)

    if nk > 2:
        @pl.when(jnp.logical_and(k > 0, k < nk - 1))
        def _():
            for cols in chunks:
                acc_ref[:, cols] += dot(cols)

    @pl.when(k == nk - 1)
    def _():
        finish(lambda cols: acc_ref[:, cols] + dot(cols))


def _ffn_down_loss(f, wd, x1, tgt):
    T, F = f.shape
    D = wd.shape[1]
    tm, tn, tk = _pick(T, 1024), _pick(D, 1024), _pick(F, 1408)
    gm, gn, gk = T // tm, D // tn, F // tk
    tile = pl.BlockSpec((tm, tn), lambda j, i, k: (i, j))
    return pl.pallas_call(
        functools.partial(_ffn_down_loss_body, gk, 1.0 / D),
        name="ffn_down_loss",
        grid=(gn, gm, gk),
        in_specs=[pl.BlockSpec((tm, tk), lambda j, i, k: (i, k)), pl.BlockSpec((tk, tn), lambda j, i, k: (k, j)),
                  tile, tile],
        out_specs=[tile, tile, pl.BlockSpec((1, 1), lambda j, i, k: (0, 0))],
        out_shape=[jax.ShapeDtypeStruct((T, D), F32), jax.ShapeDtypeStruct((T, D), BF16),
                   jax.ShapeDtypeStruct((1, 1), F32)],
        scratch_shapes=[pltpu.VMEM((tm, tn), F32)] if gk > 1 else [],
        compiler_params=_params(("arbitrary", "arbitrary", "arbitrary")),
    )(f, wd, x1, tgt)


def _lo_mask(shape):
    return lax.broadcasted_iota(jnp.int32, shape, len(shape) - 1) < HEAD_DIM


def _half_sums(t, lo):
    s_lo = jnp.sum(jnp.where(lo, t, 0.0), axis=-1, keepdims=True)
    s_hi = jnp.sum(jnp.where(lo, 0.0, t), axis=-1, keepdims=True)
    return jnp.where(lo, s_lo, s_hi)


def _head_rstd(t, lo):
    return lax.rsqrt(_half_sums(t * t, lo) * (1.0 / HEAD_DIM) + EPS)


def _place(t, lo, kv_head):
    if kv_head == 0:
        t_lo = jnp.where(lo, t, 0.0)
        t_hi = pltpu.roll(t_lo, HEAD_DIM, 1)
    else:
        t_hi = jnp.where(lo, 0.0, t)
        t_lo = pltpu.roll(t_hi, HEAD_DIM, 1)
    return jnp.concatenate([t_lo, t_hi], axis=0).astype(BF16)


def _unplace(c0, c1, lo):
    return jnp.where(lo, c0 + pltpu.roll(c0, HEAD_DIM, 1), c1 + pltpu.roll(c1, HEAD_DIM, 1))


def _band(kv_cur, kv_prev, kg, lo2):
    kb = jnp.concatenate([kv_prev[:, :BLK], kv_cur[:, :BLK]], axis=0)
    vb = jnp.concatenate([kv_prev[:, BLK:], kv_cur[:, BLK:]], axis=0)
    rk = _head_rstd(kb, lo2)
    kn = kb * rk * kg
    kk = [_place(kn, lo2, h) for h in range(N_KV_HEADS)]
    vv = [_place(vb, lo2, h) for h in range(N_KV_HEADS)]
    return kb, rk, kk, vv


def _score_geometry(first_i32):
    qi = lax.broadcasted_iota(jnp.int32, (BLK, 4 * BLK), 0)
    col = lax.broadcasted_iota(jnp.int32, (BLK, 4 * BLK), 1)
    kj = col & (2 * BLK - 1)
    dist = qi + BLK - kj
    valid = (dist >= 0) & (dist < BLK) & (kj >= first_i32 * BLK)
    return col, dist.astype(F32), valid


def _pair_probs(qn, kk, col, distf, valid, slope0, slope1, sink0, sink1):
    s = _dot(qn.astype(BF16), kk, NT) * (HEAD_DIM ** -0.5)
    slope = jnp.where(col < 2 * BLK, slope0, slope1)
    logits = jnp.where(valid, s - slope * distf, NEG)
    probs, psink = [], []
    for hh, sk in ((0, sink0), (1, sink1)):
        l = logits[:, 2 * BLK * hh:2 * BLK * (hh + 1)]
        m = jnp.maximum(jnp.max(l, axis=-1, keepdims=True), sk)
        p = jnp.exp(l - m)
        es = jnp.exp(sk - m)
        inv = 1.0 / (jnp.sum(p, axis=-1, keepdims=True) + es)
        probs.append(p * inv)
        psink.append(es * inv)
    return probs, psink


def _gelu(z):
    return 0.5 * z * (1.0 + lax.erf(z * (0.5 ** 0.5)))


def _gelu_grad(z):
    return 0.5 * (1.0 + lax.erf(z * (0.5 ** 0.5))) + z * jnp.exp(-0.5 * z * z) * ((2.0 * jnp.pi) ** -0.5)


def _tril_w(w):
    r = lax.broadcasted_iota(jnp.int32, (BLK, BLK), 0)
    c = lax.broadcasted_iota(jnp.int32, (BLK, BLK), 1)
    return jnp.where(r >= c, w, 0.0), r >= c


def _gate_fwd_group(zu, zv, lg, lb, w, bcol):
    u = _gelu(zu)
    v = _gelu(zv)
    mu = jnp.mean(v, axis=-1, keepdims=True)
    vc = v - mu
    rs = lax.rsqrt(jnp.mean(vc * vc, axis=-1, keepdims=True) + EPS)
    vh = vc * rs
    vn = vh * lg + lb
    wt, tril = _tril_w(w)
    mixed = _dot(wt.astype(BF16), vn.astype(BF16), NN) + bcol
    return u, vh, rs, vn, wt, tril, mixed


class _Dims:
    def __init__(self, seq, attn_w, gate_w):
        self.seq, self.attn_w, self.gate_w = seq, attn_w, gate_w
        self.n_heads = attn_w // HEAD_DIM
        self.group = self.n_heads // N_KV_HEADS
        self.n_pairs = attn_w // BLK
        self.n_groups = gate_w // BLK
        self.kv_col = attn_w // (2 * BLK)
        self.u0 = attn_w + 2 * BLK
        self.v0 = self.u0 + gate_w
        self.in_w = self.v0 + gate_w
        self.slopes = [2.0 ** (-8.0 * (h + 1) / self.n_heads) for h in range(self.n_heads)]


def _mixer_fwd_body(d, sink_ref, proj_ref, kvp_ref, qg_ref, kg_ref, lg_ref, lb_ref, w_ref, b_ref, goa_ref, gog_ref,
                    ya_ref, yg_ref, y_ref):
    i = pl.program_id(0)
    first = (i % (d.seq // BLK) == 0).astype(jnp.int32)
    lo = _lo_mask((BLK, BLK))
    lo2 = _lo_mask((2 * BLK, BLK))
    kv_cur = proj_ref[:, d.attn_w:d.attn_w + 2 * BLK]
    _, _, kk, vv = _band(kv_cur, kvp_ref[...], kg_ref[...], lo2)
    col, distf, valid = _score_geometry(first)
    qg = qg_ref[...]
    for j in range(d.n_pairs):
        h0, h1 = 2 * j, 2 * j + 1
        kh = h0 // d.group
        q2 = proj_ref[:, BLK * j:BLK * (j + 1)]
        qn = q2 * _head_rstd(q2, lo) * qg
        probs, _ = _pair_probs(qn, kk[kh], col, distf, valid, d.slopes[h0], d.slopes[h1],
                               sink_ref[0, h0], sink_ref[0, h1])
        p = jnp.concatenate(probs, axis=1).astype(BF16)
        ya_ref[:, BLK * j:BLK * (j + 1)] = _dot(p, vv[kh], NN)
    for g in range(d.n_groups):
        zu = proj_ref[:, d.u0 + BLK * g:d.u0 + BLK * (g + 1)]
        zv = proj_ref[:, d.v0 + BLK * g:d.v0 + BLK * (g + 1)]
        u, _, _, _, _, _, mixed = _gate_fwd_group(zu, zv, lg_ref[g:g + 1, :], lb_ref[g:g + 1, :], w_ref[g], b_ref[g])
        yg_ref[:, BLK * g:BLK * (g + 1)] = u * mixed
    ya = ya_ref[...]
    ra = lax.rsqrt(jnp.mean(ya * ya, axis=-1, keepdims=True) + EPS)
    y_ref[:, :d.attn_w] = (ya * ra * goa_ref[...]).astype(y_ref.dtype)
    yg = yg_ref[...]
    rg = lax.rsqrt(jnp.mean(yg * yg, axis=-1, keepdims=True) + EPS)
    y_ref[:, d.attn_w:] = (yg * rg * gog_ref[...]).astype(y_ref.dtype)


def _mixer_specs(d, T):
    row = lambda w: pl.BlockSpec((BLK, w), lambda i: (i, 0))
    const2 = lambda a: pl.BlockSpec(a.shape, lambda i: (0, 0))
    const3 = lambda a: pl.BlockSpec(a.shape, lambda i: (0, 0, 0))
    kv_prev = pl.BlockSpec((BLK, 2 * BLK), lambda i: (jnp.maximum(i - 1, 0), d.kv_col))
    return row, const2, const3, kv_prev


def _mixer_fwd(d, proj, sinks, qg2, kg2, lg, lb, wsp, bcol, goa, gog):
    T = proj.shape[0]
    row, const2, const3, kv_prev = _mixer_specs(d, T)
    return pl.pallas_call(
        functools.partial(_mixer_fwd_body, d),
        name="mixer_fwd",
        grid=(T // BLK,),
        in_specs=[pl.BlockSpec(memory_space=pltpu.SMEM), row(d.in_w), kv_prev, const2(qg2), const2(kg2),
                  const2(lg), const2(lb), const3(wsp), const3(bcol), const2(goa), const2(gog)],
        out_specs=[row(d.attn_w), row(d.gate_w), row(d.attn_w + d.gate_w)],
        out_shape=[jax.ShapeDtypeStruct((T, d.attn_w), F32), jax.ShapeDtypeStruct((T, d.gate_w), F32),
                   jax.ShapeDtypeStruct((T, d.attn_w + d.gate_w), BF16)],
        compiler_params=_params(("parallel",)),
    )(sinks, proj, proj, qg2, kg2, lg, lb, wsp, bcol, goa, gog)


def _mixer_bwd_body(d, sink_ref, proj_ref, kvp_ref, ya_ref, yg_ref, dy_ref, qg_ref, kg_ref, lg_ref, lb_ref, w_ref,
                    b_ref, goa_ref, gog_ref,
                    dproj_ref, dkv_ref, dqg_ref, dkg_ref, dsk_ref, dlg_ref, dlb_ref, dw_ref, db_ref, dgoa_ref,
                    dgog_ref):
    i = pl.program_id(0)

    @pl.when(i == 0)
    def _():
        for r in (dqg_ref, dkg_ref, dsk_ref, dlg_ref, dlb_ref, dw_ref, db_ref, dgoa_ref, dgog_ref):
            r[...] = jnp.zeros_like(r)

    first = (i % (d.seq // BLK) == 0).astype(jnp.int32)
    lo = _lo_mask((BLK, BLK))
    lo2 = _lo_mask((2 * BLK, BLK))
    lane_row = lax.broadcasted_iota(jnp.int32, (1, BLK), 1)

    ya = ya_ref[...]
    ra = lax.rsqrt(jnp.mean(ya * ya, axis=-1, keepdims=True) + EPS)
    yah = ya * ra
    dyn = dy_ref[:, :d.attn_w]
    dgoa_ref[...] += jnp.sum(dyn * yah, axis=0, keepdims=True)
    t = dyn * goa_ref[...]
    dya = ra * (t - yah * jnp.mean(t * yah, axis=-1, keepdims=True))
    yg = yg_ref[...]
    rg = lax.rsqrt(jnp.mean(yg * yg, axis=-1, keepdims=True) + EPS)
    ygh = yg * rg
    dyn = dy_ref[:, d.attn_w:]
    dgog_ref[...] += jnp.sum(dyn * ygh, axis=0, keepdims=True)
    t = dyn * gog_ref[...]
    dyg = rg * (t - ygh * jnp.mean(t * ygh, axis=-1, keepdims=True))

    kv_cur = proj_ref[:, d.attn_w:d.attn_w + 2 * BLK]
    kg = kg_ref[...]
    kb, rk, kk, vv = _band(kv_cur, kvp_ref[...], kg, lo2)
    col, distf, valid = _score_geometry(first)
    qg = qg_ref[...]
    ck = [jnp.zeros((2 * BLK, BLK), F32) for _ in range(N_KV_HEADS)]
    cv = [jnp.zeros((2 * BLK, BLK), F32) for _ in range(N_KV_HEADS)]
    dsk = jnp.zeros((1, BLK), F32)
    dqg = jnp.zeros((1, BLK), F32)
    for j in range(d.n_pairs):
        h0, h1 = 2 * j, 2 * j + 1
        kh = h0 // d.group
        cols = slice(BLK * j, BLK * (j + 1))
        q2 = proj_ref[:, cols]
        rq = _head_rstd(q2, lo)
        qh = q2 * rq
        qn = qh * qg
        probs, psink = _pair_probs(qn, kk[kh], col, distf, valid, d.slopes[h0], d.slopes[h1],
                                   sink_ref[0, h0], sink_ref[0, h1])
        do2 = dya[:, cols]
        prod = do2 * ya[:, cols]
        delta = (jnp.sum(jnp.where(lo, prod, 0.0), axis=-1, keepdims=True),
                 jnp.sum(jnp.where(lo, 0.0, prod), axis=-1, keepdims=True))
        do2b = do2.astype(BF16)
        dp = _dot(do2b, vv[kh], NT)
        ds = []
        for hh in (0, 1):
            ds.append(probs[hh] * (dp[:, 2 * BLK * hh:2 * BLK * (hh + 1)] - delta[hh]))
            dsink = -jnp.sum(psink[hh] * delta[hh], axis=0, keepdims=True)
            dsk = dsk + jnp.where(lane_row == (h0 + hh), dsink, 0.0)
        dsb = (jnp.concatenate(ds, axis=1) * (HEAD_DIM ** -0.5)).astype(BF16)
        pb = jnp.concatenate(probs, axis=1).astype(BF16)
        qnb = qn.astype(BF16)
        dqn = _dot(dsb, kk[kh], NN)
        dkk = _dot(dsb, qnb, TN)
        dvv = _dot(pb, do2b, TN)
        ck[kh] = ck[kh] + jnp.where(lo2, dkk[:2 * BLK], 0.0) + jnp.where(lo2, 0.0, dkk[2 * BLK:])
        cv[kh] = cv[kh] + jnp.where(lo2, dvv[:2 * BLK], 0.0) + jnp.where(lo2, 0.0, dvv[2 * BLK:])
        dqg = dqg + jnp.sum(dqn * qh, axis=0, keepdims=True)
        t = dqn * qg
        dq2 = rq * (t - qh * (_half_sums(t * qh, lo) * (1.0 / HEAD_DIM)))
        dproj_ref[:, cols] = dq2.astype(dproj_ref.dtype)
    dsk_ref[...] += dsk
    dqg_ref[...] += dqg
    dkn = _unplace(ck[0], ck[1], lo2)
    dvb = _unplace(cv[0], cv[1], lo2)
    khat = kb * rk
    dkg_ref[...] += jnp.sum(dkn * khat, axis=0, keepdims=True)
    t = dkn * kg
    dkb = rk * (t - khat * (_half_sums(t * khat, lo2) * (1.0 / HEAD_DIM)))
    rows_cur = pl.ds(pl.multiple_of(i * BLK, BLK), BLK)
    rows_prev = pl.ds(pl.multiple_of(jnp.maximum(i - 1, 0) * BLK, BLK), BLK)
    dkv_ref[rows_cur, :] = jnp.concatenate([dkb[BLK:], dvb[BLK:]], axis=1)
    dkv_ref[rows_prev, :] += jnp.concatenate([dkb[:BLK], dvb[:BLK]], axis=1)
    dproj_ref[:, d.attn_w:d.attn_w + 2 * BLK] = jnp.zeros((BLK, 2 * BLK), dproj_ref.dtype)

    for g in range(d.n_groups):
        ucols = slice(d.u0 + BLK * g, d.u0 + BLK * (g + 1))
        vcols = slice(d.v0 + BLK * g, d.v0 + BLK * (g + 1))
        zu = proj_ref[:, ucols]
        zv = proj_ref[:, vcols]
        lg = lg_ref[g:g + 1, :]
        u, vh, rs, vn, wt, tril, mixed = _gate_fwd_group(zu, zv, lg, lb_ref[g:g + 1, :], w_ref[g], b_ref[g])
        dyg_g = dyg[:, BLK * g:BLK * (g + 1)]
        du = dyg_g * mixed
        dmix = dyg_g * u
        dmb = dmix.astype(BF16)
        db_ref[g:g + 1, :] += jnp.sum(jnp.transpose(dmix), axis=0, keepdims=True)
        dw_ref[g] += jnp.where(tril, _dot(dmb, vn.astype(BF16), NT), 0.0)
        dvn = _dot(wt.astype(BF16), dmb, TN)
        dlg_ref[g:g + 1, :] += jnp.sum(dvn * vh, axis=0, keepdims=True)
        dlb_ref[g:g + 1, :] += jnp.sum(dvn, axis=0, keepdims=True)
        dvh = dvn * lg
        dv = rs * (dvh - jnp.mean(dvh, axis=-1, keepdims=True) - vh * jnp.mean(dvh * vh, axis=-1, keepdims=True))
        dproj_ref[:, ucols] = (du * _gelu_grad(zu)).astype(dproj_ref.dtype)
        dproj_ref[:, vcols] = (dv * _gelu_grad(zv)).astype(dproj_ref.dtype)


def _mixer_bwd(d, proj, ya, yg, dy, sinks, qg2, kg2, lg, lb, wsp, bcol, goa, gog):
    T = proj.shape[0]
    row, const2, const3, kv_prev = _mixer_specs(d, T)
    acc2 = lambda s: pl.BlockSpec(s, lambda i: (0, 0))
    G = d.n_groups
    out_shapes = [((T, d.in_w), BF16), ((T, 2 * BLK), F32), ((1, BLK), F32), ((1, BLK), F32), ((1, BLK), F32),
                  ((G, BLK), F32), ((G, BLK), F32), ((G, BLK, BLK), F32), ((G, BLK), F32),
                  ((1, d.attn_w), F32), ((1, d.gate_w), F32)]
    out_specs = [row(d.in_w)] + [acc2(s) for s, _ in out_shapes[1:7]] + \
                [pl.BlockSpec((G, BLK, BLK), lambda i: (0, 0, 0))] + [acc2(s) for s, _ in out_shapes[8:]]
    return pl.pallas_call(
        functools.partial(_mixer_bwd_body, d),
        name="mixer_bwd",
        grid=(T // BLK,),
        in_specs=[pl.BlockSpec(memory_space=pltpu.SMEM), row(d.in_w), kv_prev, row(d.attn_w), row(d.gate_w),
                  row(d.attn_w + d.gate_w), const2(qg2), const2(kg2), const2(lg), const2(lb), const3(wsp),
                  const3(bcol), const2(goa), const2(gog)],
        out_specs=out_specs,
        out_shape=[jax.ShapeDtypeStruct(s, t) for s, t in out_shapes],
        compiler_params=_params(("arbitrary",)),
    )(sinks, proj, proj, ya, yg, dy, qg2, kg2, lg, lb, wsp, bcol, goa, gog)


def _put_kv_body(dkv_ref, dproj_in_ref, dproj_ref):
    del dproj_in_ref
    dproj_ref[...] = dkv_ref[...].astype(dproj_ref.dtype)


def _put_kv(d, dproj, dkv):
    T = dproj.shape[0]
    tr = _pick(T, 1024, 16)
    return pl.pallas_call(
        functools.partial(_put_kv_body),
        name="put_kv",
        grid=(T // tr,),
        in_specs=[pl.BlockSpec((tr, 2 * BLK), lambda i: (i, 0)), pl.BlockSpec(memory_space=pl.ANY)],
        out_specs=pl.BlockSpec((tr, 2 * BLK), lambda i: (i, d.kv_col)),
        out_shape=jax.ShapeDtypeStruct(dproj.shape, dproj.dtype),
        input_output_aliases={1: 0},
        compiler_params=_params(("parallel",)),
    )(dkv, dproj)


def _add_pair_body(pc_ref, own_ref, got_ref, o_ref):
    del pc_ref
    o_ref[...] = (own_ref[...].astype(F32) + got_ref[...].astype(F32)).astype(o_ref.dtype)


def _add_pair(g4, got, pc, name):
    n, _, h, C = g4.shape
    tr = _pick(h, 512, 16)
    return pl.pallas_call(
        functools.partial(_add_pair_body),
        name=name,
        grid_spec=pltpu.PrefetchScalarGridSpec(
            num_scalar_prefetch=1,
            grid=(n, h // tr),
            in_specs=[pl.BlockSpec((None, None, tr, C), lambda q, i, pc: (q, pc[1], i, 0)),
                      pl.BlockSpec((None, tr, C), lambda q, i, pc: (q, i, 0))],
            out_specs=pl.BlockSpec((None, tr, C), lambda q, i, pc: (q, i, 0)),
        ),
        out_shape=jax.ShapeDtypeStruct((n, h, C), g4.dtype),
        compiler_params=_params(("parallel", "parallel")),
    )(pc, g4, got)


def _adamw_update(w, g, m, v):
    m = ADAM_B1 * m + (1.0 - ADAM_B1) * g
    v = ADAM_B2 * v + (1.0 - ADAM_B2) * (g * g)
    m_hat = m / (1.0 - ADAM_B1 ** ADAM_STEP)
    v_hat = v / (1.0 - ADAM_B2 ** ADAM_STEP)
    return -ADAM_LR * (m_hat / (jnp.sqrt(v_hat) + ADAM_EPS) + ADAM_WD * w), m, v


def _adamw_body(w_ref, g_ref, m_ref, v_ref, d_ref, nm_ref, nv_ref):
    d_ref[...], nm_ref[...], nv_ref[...] = _adamw_update(w_ref[...], g_ref[...], m_ref[...], v_ref[...])


def _adamw(w, g, m, v, name):
    R, C = w.shape
    tr = _pick(R, 512, 8)
    blk = pl.BlockSpec((tr, C), lambda i: (i, 0))
    return pl.pallas_call(
        functools.partial(_adamw_body),
        name=name,
        grid=(R // tr,),
        in_specs=[blk] * 4,
        out_specs=[blk] * 3,
        out_shape=[jax.ShapeDtypeStruct((R, C), F32)] * 3,
        compiler_params=_params(("parallel",)),
    )(w, g, m, v)


def _adamw_halves_body(pc_ref, w_ref, own_ref, got_ref, m_ref, v_ref, g_ref, d_ref, nm_ref, nv_ref):
    mine = pl.program_id(0) == pc_ref[1]

    def update(g):
        g_ref[...] = g
        d_ref[...], nm_ref[...], nv_ref[...] = _adamw_update(w_ref[...], g, m_ref[...], v_ref[...])

    @pl.when(mine)
    def _():
        update(own_ref[...])

    @pl.when(jnp.logical_not(mine))
    def _():
        update(got_ref[...])


def _adamw_halves(w, own, got, m, v, pc, name):
    h, C = own.shape
    tr = _pick(h, 512, 8)
    nh = h // tr
    full = pl.BlockSpec((None, tr, C), lambda hh, i, pc: (0, hh * nh + i, 0))
    mine = pl.BlockSpec((tr, C), lambda hh, i, pc: (jnp.where(hh == pc[1], i, 0), 0))
    theirs = pl.BlockSpec((tr, C), lambda hh, i, pc: (jnp.where(hh == pc[1], 0, i), 0))
    return pl.pallas_call(
        functools.partial(_adamw_halves_body),
        name=name,
        grid_spec=pltpu.PrefetchScalarGridSpec(
            num_scalar_prefetch=1,
            grid=(2, nh),
            in_specs=[full, mine, theirs, full, full],
            out_specs=[full] * 4,
        ),
        out_shape=[jax.ShapeDtypeStruct(w.shape, F32)] * 4,
        compiler_params=_params(("parallel", "parallel")),
    )(pc, w, own, got, m, v)


def _me():
    x, y, c = lax.axis_index("x"), lax.axis_index("y"), lax.axis_index("c")
    chips = [(1 - x, y), (x, 1 - y), (1 - x, 1 - y)]
    return x, y, c, chips


def _cast_into_body(pc_ref, w_ref, o_ref):
    del pc_ref
    o_ref[...] = w_ref[...].astype(o_ref.dtype)


def _cast_into(w, pc, name):
    _, Rs, C = w.shape
    h = Rs // 2
    tr = _pick(h, 512, 16)
    nh = h // tr
    return pl.pallas_call(
        functools.partial(_cast_into_body),
        name=name,
        grid_spec=pltpu.PrefetchScalarGridSpec(
            num_scalar_prefetch=1,
            grid=(2, nh),
            in_specs=[pl.BlockSpec((None, tr, C), lambda hh, i, pc: (0, hh * nh + i, 0))],
            out_specs=pl.BlockSpec((None, None, tr, C), lambda hh, i, pc: (pc[0], hh, i, 0)),
        ),
        out_shape=jax.ShapeDtypeStruct((N_CHIPS, 2, h, C), BF16),
        compiler_params=_params(("parallel", "parallel")),
    )(pc, w)


MAX_PIECES = 4


def _send_tile_to_sibling(src_of, dst_of, tr, dst_total, send_sems, recv_sem, last):
    x, y, c, _ = _me()
    pieces = MAX_PIECES if tr % (16 * MAX_PIECES) == 0 else (2 if tr % 32 == 0 else 1)
    n = tr // pieces
    copies = [pltpu.make_async_remote_copy(src_ref=src_of(k * n, n), dst_ref=dst_of(k * n, n), send_sem=send_sems.at[k],
                                           recv_sem=recv_sem, device_id=(x, y, 1 - c), device_id_type=MESH)
              for k in range(pieces)]
    for cp in copies:
        cp.start()
    for cp in copies:
        cp.wait_send()

    @pl.when(last)
    def _():
        pltpu.make_async_remote_copy(src_ref=dst_total, dst_ref=dst_total, send_sem=send_sems.at[0], recv_sem=recv_sem,
                                     device_id=(x, y, 1 - c), device_id_type=MESH).wait_recv()


TILE_SEMS = [pltpu.SemaphoreType.DMA((MAX_PIECES,)), pltpu.SemaphoreType.DMA(())]


def _ag_pair_body(tr, n_i, pc_ref, tile_ref, buf_ref, send_sem, recv_sem):
    j, i = pl.program_id(0), pl.program_id(1)
    q = pc_ref[0] ^ (j + 1)
    c = pc_ref[1]
    r_tile = pl.multiple_of(i * tr, tr)
    _send_tile_to_sibling(lambda r0, n: tile_ref.at[:, :, pl.ds(r0, n)],
                          lambda r0, n: buf_ref.at[pl.ds(q, 1), pl.ds(c, 1), pl.ds(r_tile + r0, n)], tr,
                          buf_ref.at[pl.ds(0, N_CHIPS - 1), 0], send_sem, recv_sem,
                          jnp.logical_and(j == N_CHIPS - 2, i == n_i - 1))


def _ag_pair(buf, pc, name):
    _, _, h, C = buf.shape
    tr = _pick(h, 512, 16)
    return pl.pallas_call(
        functools.partial(_ag_pair_body, tr, h // tr),
        name=name,
        grid_spec=pltpu.PrefetchScalarGridSpec(
            num_scalar_prefetch=1,
            grid=(N_CHIPS - 1, h // tr),
            in_specs=[pl.BlockSpec((1, 1, tr, C), lambda j, i, pc: (pc[0] ^ (j + 1), pc[1], i, 0))],
            out_specs=HBM,
            scratch_shapes=TILE_SEMS,
        ),
        out_shape=jax.ShapeDtypeStruct(buf.shape, buf.dtype),
        input_output_aliases={1: 0},
        compiler_params=_params(("arbitrary", "arbitrary")),
    )(pc, buf)


def _swap_halves_body(tr, n_q, n_i, pc_ref, tile_ref, got_ref, send_sem, recv_sem):
    del pc_ref
    q, i = pl.program_id(0), pl.program_id(1)
    r_tile = pl.multiple_of(i * tr, tr)
    _send_tile_to_sibling(lambda r0, n: tile_ref.at[:, :, pl.ds(r0, n)],
                          lambda r0, n: got_ref.at[pl.ds(q, 1), :, pl.ds(r_tile + r0, n)], tr, got_ref, send_sem, recv_sem,
                          jnp.logical_and(q == n_q - 1, i == n_i - 1))


def _swap_halves(g4, pc, name):
    n, _, h, C = g4.shape
    tr = _pick(h, 512, 16)
    return pl.pallas_call(
        functools.partial(_swap_halves_body, tr, n, h // tr),
        name=name,
        grid_spec=pltpu.PrefetchScalarGridSpec(
            num_scalar_prefetch=1,
            grid=(n, h // tr),
            in_specs=[pl.BlockSpec((1, 1, tr, C), lambda q, i, pc: (q, 1 - pc[1], i, 0))],
            out_specs=HBM,
            scratch_shapes=TILE_SEMS,
        ),
        out_shape=jax.ShapeDtypeStruct((n, 1, h, C), g4.dtype),
        compiler_params=_params(("arbitrary", "arbitrary")),
    )(pc, g4).reshape(n, h, C)


def _ici_copy(src, dst, send_sems, recv_sems, j, chip, c):
    return pltpu.make_async_remote_copy(src_ref=src, dst_ref=dst, send_sem=send_sems.at[j], recv_sem=recv_sems.at[j],
                                        device_id=(chip[0], chip[1], c), device_id_type=MESH)


def _token_spec():
    return jax.ShapeDtypeStruct((8, BLK), F32), pl.BlockSpec(memory_space=pltpu.VMEM)


def _ag_start_body(n, *refs):
    bufs, sends, recvs, token_ref = refs[:n], refs[n:2 * n], refs[2 * n:3 * n], refs[-1]
    x, y, c, chips = _me()
    for buf_ref, send_sems, recv_sems in zip(bufs, sends, recvs):
        mine = buf_ref.at[2 * x + y, c]
        for j, chip in enumerate(chips):
            _ici_copy(mine, mine, send_sems, recv_sems, j, chip, c).start()
    token_ref[...] = jnp.zeros_like(token_ref)


def _ag_start(bufs):
    n = len(bufs)
    tok_shape, tok_spec = _token_spec()
    sems = [pltpu.SemaphoreType.DMA((N_CHIPS - 1,))] * n
    outs = pl.pallas_call(
        functools.partial(_ag_start_body, n),
        name="ag_start",
        in_specs=[HBM] * n,
        out_specs=[SEM] * (2 * n) + [HBM] * n + [tok_spec],
        out_shape=sems + sems + [pltpu.HBM(b.shape, b.dtype) for b in bufs] + [tok_shape],
        input_output_aliases={k: 2 * n + k for k in range(n)},
        compiler_params=pltpu.CompilerParams(has_side_effects=EFFECT),
    )(*[pltpu.with_memory_space_constraint(b, pltpu.HBM) for b in bufs])
    return outs[:n], outs[n:2 * n], outs[2 * n:3 * n], outs[-1]


def _ag_wait_body(buf_ref, send_sems, recv_sems, after_ref, buf_out):
    del after_ref, buf_out
    x, y, c, chips = _me()
    mine = buf_ref.at[2 * x + y, c]
    for j, chip in enumerate(chips):
        theirs = buf_ref.at[2 * chip[0] + chip[1], c]
        _ici_copy(mine, mine, send_sems, recv_sems, j, chip, c).wait_send()
        _ici_copy(theirs, theirs, send_sems, recv_sems, j, chip, c).wait_recv()


def _ag_wait(buf, send_sems, recv_sems, after, name):
    return pl.pallas_call(
        functools.partial(_ag_wait_body),
        name=name,
        in_specs=[HBM, SEM, SEM, ANY],
        out_specs=HBM,
        out_shape=pltpu.HBM(buf.shape, buf.dtype),
        input_output_aliases={0: 0},
        compiler_params=pltpu.CompilerParams(has_side_effects=EFFECT),
    )(buf, send_sems, recv_sems, after)


def _rs_start_body(pair_ref, land_ref, after_ref, send_sems, recv_sems, pair_thru, land_thru, token_ref):
    del after_ref, pair_thru, land_thru
    x, y, c, chips = _me()
    for j, chip in enumerate(chips):
        _ici_copy(pair_ref.at[2 * chip[0] + chip[1]], land_ref.at[j], send_sems, recv_sems, j, chip, c).start()
    token_ref[...] = jnp.zeros_like(token_ref)


def _rs_start(pair, after, name):
    n, h, C = pair.shape
    tok_shape, tok_spec = _token_spec()
    sems = pltpu.SemaphoreType.DMA((N_CHIPS - 1,))
    land = pltpu.with_memory_space_constraint(lax.empty((N_CHIPS - 1, h, C), pair.dtype), pltpu.HBM)
    return pl.pallas_call(
        functools.partial(_rs_start_body),
        name=name,
        in_specs=[HBM, HBM, ANY],
        out_specs=[SEM, SEM, HBM, HBM, tok_spec],
        out_shape=[sems, sems, pltpu.HBM(pair.shape, pair.dtype), pltpu.HBM(land.shape, land.dtype), tok_shape],
        input_output_aliases={0: 2, 1: 3},
        compiler_params=pltpu.CompilerParams(has_side_effects=EFFECT),
    )(pltpu.with_memory_space_constraint(pair, pltpu.HBM), land, after)


def _rs_wait_body(pair_ref, land_ref, send_sems, recv_sems, after_ref, pair_out, land_out):
    del after_ref, pair_out, land_out
    x, y, c, chips = _me()
    for j, chip in enumerate(chips):
        _ici_copy(pair_ref.at[0], land_ref.at[j], send_sems, recv_sems, j, chip, c).wait_send()
        _ici_copy(pair_ref.at[0], land_ref.at[j], send_sems, recv_sems, j, chip, c).wait_recv()


def _rs_wait(pair, land, send_sems, recv_sems, after, name):
    return pl.pallas_call(
        functools.partial(_rs_wait_body),
        name=name,
        in_specs=[HBM, HBM, SEM, SEM, ANY],
        out_specs=[HBM, HBM],
        out_shape=[pltpu.HBM(pair.shape, pair.dtype), pltpu.HBM(land.shape, land.dtype)],
        input_output_aliases={0: 0, 1: 1},
        compiler_params=pltpu.CompilerParams(has_side_effects=EFFECT),
    )(pair, land, send_sems, recv_sems, after)


def _swap_copy(g4_ref, got_ref, send_sem, recv_sem):
    x, y, c, _ = _me()
    return pltpu.make_async_remote_copy(src_ref=g4_ref.at[:, 1 - c], dst_ref=got_ref, send_sem=send_sem,
                                        recv_sem=recv_sem, device_id=(x, y, 1 - c), device_id_type=MESH)


def _swap_start_body(g4_ref, got_ref, send_sem, recv_sem, g4_thru, got_thru, token_ref):
    del g4_thru, got_thru
    _swap_copy(g4_ref, got_ref, send_sem, recv_sem).start()
    token_ref[...] = jnp.zeros_like(token_ref)


def _swap_start(g4, name):
    n, _, h, C = g4.shape
    tok_shape, tok_spec = _token_spec()
    sem = pltpu.SemaphoreType.DMA(())
    got = pltpu.with_memory_space_constraint(lax.empty((n, h, C), g4.dtype), pltpu.HBM)
    return pl.pallas_call(
        functools.partial(_swap_start_body),
        name=name,
        in_specs=[HBM, HBM],
        out_specs=[SEM, SEM, HBM, HBM, tok_spec],
        out_shape=[sem, sem, pltpu.HBM(g4.shape, g4.dtype), pltpu.HBM(got.shape, got.dtype), tok_shape],
        input_output_aliases={0: 2, 1: 3},
        compiler_params=pltpu.CompilerParams(has_side_effects=EFFECT),
    )(pltpu.with_memory_space_constraint(g4, pltpu.HBM), got)


def _swap_wait_body(g4_ref, got_ref, send_sem, recv_sem, after_ref, g4_out, got_out):
    del after_ref, g4_out, got_out
    cp = _swap_copy(g4_ref, got_ref, send_sem, recv_sem)
    cp.wait_send()
    cp.wait_recv()


def _swap_wait(g4, got, send_sem, recv_sem, after, name):
    return pl.pallas_call(
        functools.partial(_swap_wait_body),
        name=name,
        in_specs=[HBM, HBM, SEM, SEM, ANY],
        out_specs=[HBM, HBM],
        out_shape=[pltpu.HBM(g4.shape, g4.dtype), pltpu.HBM(got.shape, got.dtype)],
        input_output_aliases={0: 0, 1: 1},
        compiler_params=pltpu.CompilerParams(has_side_effects=EFFECT),
    )(g4, got, send_sem, recv_sem, after)


def _add_chips_body(tr, n_i, pc_ref, own_ref, l0_ref, l1_ref, l2_ref, o_ref, got_ref, send_sems, recv_sem):
    del pc_ref
    i = pl.program_id(0)
    r = own_ref[...].astype(F32) + l0_ref[...].astype(F32)
    o_ref[...] = r + l1_ref[...].astype(F32) + l2_ref[...].astype(F32)
    r_tile = pl.multiple_of(i * tr, tr)
    _send_tile_to_sibling(lambda r0, n: o_ref.at[pl.ds(r0, n)], lambda r0, n: got_ref.at[pl.ds(r_tile + r0, n)], tr,
                          got_ref, send_sems, recv_sem, i == n_i - 1)


def _add_chips(pair, land, pc, name):
    _, h, C = pair.shape
    tr = _pick(h, 256, 16)
    slot = lambda j: pl.BlockSpec((None, tr, C), lambda i, pc: (j, i, 0))
    return pl.pallas_call(
        functools.partial(_add_chips_body, tr, h // tr),
        name=name,
        grid_spec=pltpu.PrefetchScalarGridSpec(
            num_scalar_prefetch=1,
            grid=(h // tr,),
            in_specs=[pl.BlockSpec((None, tr, C), lambda i, pc: (pc[0], i, 0)), slot(0), slot(1), slot(2)],
            out_specs=[pl.BlockSpec((tr, C), lambda i, pc: (i, 0)), HBM],
            scratch_shapes=TILE_SEMS,
        ),
        out_shape=[jax.ShapeDtypeStruct((h, C), F32), jax.ShapeDtypeStruct((h, C), F32)],
        compiler_params=_params(("arbitrary",)),
    )(pc, pair, land, land, land)


def _peer(r):
    x, y, c, _ = _me()
    return (x ^ ((r >> 2) & 1), y ^ ((r >> 1) & 1), c ^ (r & 1))


def _ar_start_body(x_ref, land_ref, send_sems, recv_sems, x_thru, land_thru, token_ref):
    del x_thru, land_thru
    for r in range(1, N_DEV):
        pltpu.make_async_remote_copy(src_ref=x_ref, dst_ref=land_ref.at[r - 1], send_sem=send_sems.at[r - 1],
                                     recv_sem=recv_sems.at[r - 1], device_id=_peer(r), device_id_type=MESH).start()
    token_ref[...] = jnp.zeros_like(token_ref)


def _ar_start(packed):
    tok_shape, tok_spec = _token_spec()
    sems = pltpu.SemaphoreType.DMA((N_DEV - 1,))
    land = pltpu.with_memory_space_constraint(lax.empty((N_DEV - 1,) + packed.shape, packed.dtype), pltpu.HBM)
    return pl.pallas_call(
        functools.partial(_ar_start_body),
        name="ar_start",
        in_specs=[HBM, HBM],
        out_specs=[SEM, SEM, HBM, HBM, tok_spec],
        out_shape=[sems, sems, pltpu.HBM(packed.shape, packed.dtype), pltpu.HBM(land.shape, land.dtype), tok_shape],
        input_output_aliases={0: 2, 1: 3},
        compiler_params=pltpu.CompilerParams(has_side_effects=EFFECT),
    )(pltpu.with_memory_space_constraint(packed, pltpu.HBM), land)


def _ar_wait_body(x_ref, land_ref, send_sems, recv_sems, after_ref, x_out, land_out):
    del after_ref, x_out, land_out
    for r in range(1, N_DEV):
        cp = pltpu.make_async_remote_copy(src_ref=x_ref, dst_ref=land_ref.at[r - 1], send_sem=send_sems.at[r - 1],
                                          recv_sem=recv_sems.at[r - 1], device_id=_peer(r), device_id_type=MESH)
        cp.wait_send()
        cp.wait_recv()


def _ar_wait(packed, land, send_sems, recv_sems, after):
    return pl.pallas_call(
        functools.partial(_ar_wait_body),
        name="ar_wait",
        in_specs=[HBM, HBM, SEM, SEM, ANY],
        out_specs=[HBM, HBM],
        out_shape=[pltpu.HBM(packed.shape, packed.dtype), pltpu.HBM(land.shape, land.dtype)],
        input_output_aliases={0: 0, 1: 1},
        compiler_params=pltpu.CompilerParams(has_side_effects=EFFECT),
    )(packed, land, send_sems, recv_sems, after)


def _ar_sum_body(me_ref, own_ref, *rest):
    o_ref = rest[N_DEV]
    acc = None
    for dev in range(N_DEV):
        term = jnp.where(me_ref[0] == dev, own_ref[...], rest[dev][...])
        acc = term if acc is None else acc + term
    o_ref[...] = acc


def _ar_sum(packed, land, me):
    R, C = packed.shape
    tr = _pick(R, 552, 8)
    own = pl.BlockSpec((tr, C), lambda i, me: (i, 0))
    slot = lambda dev: pl.BlockSpec((None, tr, C), lambda i, me: (jnp.maximum((dev ^ me[0]) - 1, 0), i, 0))
    return pl.pallas_call(
        functools.partial(_ar_sum_body),
        name="ar_sum",
        grid_spec=pltpu.PrefetchScalarGridSpec(
            num_scalar_prefetch=1,
            grid=(R // tr,),
            in_specs=[own] + [slot(dev) for dev in range(N_DEV)],
            out_specs=pl.BlockSpec((tr, C), lambda i, me: (i, 0)),
        ),
        out_shape=jax.ShapeDtypeStruct((R, C), F32),
        compiler_params=_params(("parallel",)),
    )(me, packed, *([land] * N_DEV))


def _pack(arrays):
    rows = []
    for a in arrays:
        flat = a.reshape(-1).astype(F32)
        pad = (-flat.shape[0]) % BLK
        rows.append(jnp.pad(flat, (0, pad)).reshape(-1, BLK))
    packed = jnp.concatenate(rows, axis=0)
    pad = (-packed.shape[0]) % 8
    return jnp.pad(packed, ((0, pad), (0, 0)))


def _unpack(packed, shapes):
    out, r = [], 0
    for s in shapes:
        n = 1
        for k in s:
            n *= k
        nr = -(-n // BLK)
        out.append(packed[r:r + nr].reshape(-1)[:n].reshape(s))
        r += nr
    return out


def kernel(x, norm1_g, w_in, q_norm_g, k_norm_g, attn_sinks, gate_ln_g, gate_ln_b, w_spatial, b_spatial, out_norm_attn_g, out_norm_gate_g, w_out, norm2_g, w_ffn_gate, w_ffn_up, w_ffn_down, loss_target, m_norm1_g, m_w_in, m_q_norm_g, m_k_norm_g, m_attn_sinks, m_gate_ln_g, m_gate_ln_b, m_w_spatial, m_b_spatial, m_out_norm_attn_g, m_out_norm_gate_g, m_w_out, m_norm2_g, m_w_ffn_gate, m_w_ffn_up, m_w_ffn_down, v_norm1_g, v_w_in, v_q_norm_g, v_k_norm_g, v_attn_sinks, v_gate_ln_g, v_gate_ln_b, v_w_spatial, v_b_spatial, v_out_norm_attn_g, v_out_norm_gate_g, v_w_out, v_norm2_g, v_w_ffn_gate, v_w_ffn_up, v_w_ffn_down):
    bl, seq, D = x.shape
    T = bl * seq
    attn_w, gate_w = out_norm_attn_g.shape[1], out_norm_gate_g.shape[1]
    d = _Dims(seq, attn_w, gate_w)
    G = d.n_groups
    in_w = d.in_w
    dff = w_ffn_gate.shape[2] * N_CHIPS
    assert w_in.shape[2] * N_CHIPS == in_w and seq % BLK == 0 and attn_w % (2 * BLK) == 0

    pc = jnp.stack([2 * lax.axis_index("x") + lax.axis_index("y"), lax.axis_index("c")]).astype(jnp.int32)
    big = [w_in, w_out, w_ffn_gate, w_ffn_up, w_ffn_down]
    names = ["in", "out", "gate", "up", "down"]
    sends, recvs, bufs, behind = _ag_start([_cast_into(w, pc, "cast_" + n) for w, n in zip(big, names)])

    def gathered(k, after):
        buf = _ag_wait(bufs[k], sends[k], recvs[k], after, "ag_wait_" + names[k])
        return _ag_pair(buf, pc, "ag_pair_" + names[k]).reshape((N_CHIPS,) + big[k].shape[1:])

    qg2 = jnp.tile(q_norm_g, (1, 2))
    kg2 = jnp.tile(k_norm_g, (1, 2))
    lg, lb, wsp = gate_ln_g[0], gate_ln_b[0], w_spatial[0]
    bcol = jnp.broadcast_to(b_spatial[0][:, :, None], (G, BLK, BLK))

    xf = x.reshape(T, D)
    tgt = loss_target.reshape(T, D)
    h1 = _rms_fwd(xf, norm1_g, "norm1_fwd", after=behind)
    win_full = jnp.transpose(gathered(0, h1), (1, 0, 2)).reshape(D, in_w)
    proj = _matmul(h1, win_full, "nn", F32, "proj_fwd", tm=1024, tn=1664)
    ya, yg, yn = _mixer_fwd(d, proj, attn_sinks, qg2, kg2, lg, lb, wsp, bcol, out_norm_attn_g, out_norm_gate_g)
    wout_full = gathered(1, yn).reshape(attn_w + gate_w, D)
    x1 = _matmul(yn, wout_full, "nn", F32, "out_fwd", tm=1024, tn=1024, add=xf)
    h2 = _rms_fwd(x1, norm2_g, "norm2_fwd")
    wg_g, wu_g = gathered(2, h2), gathered(3, h2)
    a, b, f = _ffn_up(h2, wg_g, wu_g)
    wd_full = gathered(4, f).reshape(dff, D)
    dx2, dx2b, loss_local = _ffn_down_loss(f, wd_full, x1, tgt)

    def swap_start(g, n):
        g4 = g.reshape(N_CHIPS, 2, g.shape[1] // 2, g.shape[2])
        return _swap_start(g4, "rs_swap_start_" + n)

    def reduce_start(swapping, n, after):
        send, recv, g4, got, _ = swapping
        g4, got = _swap_wait(g4, got, send, recv, after, "rs_swap_wait_" + n)
        return _rs_start(_add_pair(g4, got, pc, "rs_add_pair_" + n), got, "rs_start_" + n)

    reducing = {}
    g_d = _matmul(f, dx2b, "tn", BF16, "ffn_down_dw", tm=1408, tn=1024, tk=2048, out_slab="r")
    swap_d = swap_start(g_d, "down")
    da, db = _ffn_down_dx(dx2b, wd_full, a, b, swap_d[4])
    g_g = _matmul(h2, da, "tn", BF16, "ffn_gate_dw", tm=1024, tn=1408, tk=2048, out_slab="c")
    swap_g = swap_start(g_g, "gate")
    reducing["down"] = reduce_start(swap_d, "down", swap_g[4])
    g_u = _matmul(h2, db, "tn", BF16, "ffn_up_dw", tm=1024, tn=1408, tk=2048, out_slab="c",
                  after=reducing["down"][4])
    swap_u = swap_start(g_u, "up")
    reducing["gate"] = reduce_start(swap_g, "gate", swap_u[4])
    dh2 = _matmul(da, wg_g, "nt", F32, "ffn_gate_dx", tm=1024, tn=1024, tk=1408, b_slab="k",
                  after=reducing["gate"][4])
    dh2 = _matmul(db, wu_g, "nt", F32, "ffn_up_dx", tm=1024, tn=1024, tk=1408, b_slab="k", add=dh2)
    reducing["up"] = reduce_start(swap_u, "up", dh2)
    dx1, dx1b, dg_norm2 = _rms_bwd(x1, norm2_g, dh2, dx2, "norm2_bwd", True)
    g_o = _matmul(yn, dx1b, "tn", BF16, "out_dw", tm=512, tn=1024, tk=2048, out_slab="r",
                  after=reducing["up"][4])
    swap_o = swap_start(g_o, "out")
    dy = _matmul(dx1b, wout_full, "nt", F32, "out_dx", tm=1024, tn=1024, after=swap_o[4])
    (dproj, dkv, dqg, dkg, dsk, dlg, dlb, dwsp, dbsp, dgoa, dgog) = _mixer_bwd(
        d, proj, ya, yg, dy, attn_sinks, qg2, kg2, lg, lb, wsp, bcol, out_norm_attn_g, out_norm_gate_g)
    dproj = _put_kv(d, dproj, dkv)
    reducing["out"] = reduce_start(swap_o, "out", dproj)
    g_in_full = _matmul(h1, dproj, "tn", BF16, "proj_dw", tm=1024, tn=1664, tk=2048,
                        after=reducing["out"][4])
    g_i = jnp.transpose(g_in_full.reshape(D, N_CHIPS, in_w // N_CHIPS), (1, 0, 2))
    g4_i = g_i.reshape(N_CHIPS, 2, D // 2, in_w // N_CHIPS)
    pair_i = _add_pair(g4_i, _swap_halves(g4_i, pc, "rs_swap_in"), pc, "rs_add_pair_in")
    reducing["in"] = _rs_start(pair_i, g_i, "rs_start_in")
    dh1 = _matmul(dproj, win_full, "nt", F32, "proj_dx", tm=1024, tn=1024, after=reducing["in"][4])
    dx, dg_norm1 = _rms_bwd(xf, norm1_g, dh1, dx1, "norm1_bwd", False)

    dqg64 = dqg[:, :HEAD_DIM] + dqg[:, HEAD_DIM:]
    dkg64 = dkg[:, :HEAD_DIM] + dkg[:, HEAD_DIM:]
    small_g_local = [dg_norm1, dqg64, dkg64, dsk[:, :d.n_heads], dlg, dlb, dwsp, dbsp, dgoa, dgog, dg_norm2,
                     loss_local]
    ar_send, ar_recv, ar_own, ar_land, ar_token = _ar_start(_pack(small_g_local))

    big_m = [m_w_in, m_w_out, m_w_ffn_gate, m_w_ffn_up, m_w_ffn_down]
    big_v = [v_w_in, v_w_out, v_w_ffn_gate, v_w_ffn_up, v_w_ffn_down]
    big_grads, big_d, big_nm, big_nv = [], [], [], []
    for w, m, v, n in zip(big, big_m, big_v, names):
        send, recv, pair, land, _ = reducing[n]
        pair, land = _rs_wait(pair, land, send, recv, ar_token, "rs_wait_" + n)
        own, got = _add_chips(pair, land, pc, "rs_add_chips_" + n)
        outs = _adamw_halves(w, own, got, m, v, pc, "adamw_" + n)
        for lst, o in zip((big_grads, big_d, big_nm, big_nv), outs):
            lst.append(o)

    small_names_w = [norm1_g, q_norm_g, k_norm_g, attn_sinks, gate_ln_g, gate_ln_b, w_spatial, b_spatial,
                     out_norm_attn_g, out_norm_gate_g, norm2_g]
    small_m = [m_norm1_g, m_q_norm_g, m_k_norm_g, m_attn_sinks, m_gate_ln_g, m_gate_ln_b, m_w_spatial, m_b_spatial,
               m_out_norm_attn_g, m_out_norm_gate_g, m_norm2_g]
    small_v = [v_norm1_g, v_q_norm_g, v_k_norm_g, v_attn_sinks, v_gate_ln_g, v_gate_ln_b, v_w_spatial, v_b_spatial,
               v_out_norm_attn_g, v_out_norm_gate_g, v_norm2_g]
    shapes = [w.shape for w in small_names_w] + [loss_local.shape]
    ride = [jnp.zeros(loss_local.shape, F32)]
    ar_own, ar_land = _ar_wait(ar_own, ar_land, ar_send, ar_recv, big_nv[-1])
    me = (4 * lax.axis_index("x") + 2 * lax.axis_index("y") + lax.axis_index("c")).astype(jnp.int32).reshape(1)
    sg = _ar_sum(ar_own, ar_land, me)
    sd, snm, snv = _adamw(_pack(small_names_w + ride), sg, _pack(small_m + ride), _pack(small_v + ride), "adamw_small")
    small_g, small_d, small_nm, small_nv = (_unpack(t, shapes) for t in (sg, sd, snm, snv))
    loss = small_g[-1][0, 0]

    def order(small, bigs):
        s = list(small)
        bg = list(bigs)
        return [s[0], bg[0], s[1], s[2], s[3], s[4], s[5], s[6], s[7], s[8], s[9], bg[1], s[10], bg[2], bg[3], bg[4]]

    grad_x = dx.reshape(bl, seq, D)
    return (loss, grad_x, *order(small_g, big_grads), *order(small_d, big_d), *order(small_nm, big_nm),
            *order(small_nv, big_nv))
```

```python
import functools

import jax
import jax.numpy as jnp
from jax import lax
from jax.experimental import pallas as pl
from jax.experimental.pallas import tpu as pltpu

F32 = jnp.float32
BF16 = jnp.bfloat16
MESH = pl.DeviceIdType.MESH

EPS = 1e-6
HEAD_DIM = 64
N_KV_HEADS = 2
BLK = 128
N_CHIPS = 4
N_DEV = 8
NEG = -1e30

ADAM_LR = 0.001
ADAM_B1 = 0.9
ADAM_B2 = 0.999
ADAM_EPS = 1e-08
ADAM_WD = 0.01
ADAM_STEP = 10

VMEM_LIMIT = 56 * 1024 * 1024

NN = (((1,), (0,)), ((), ()))
NT = (((1,), (1,)), ((), ()))
TN = (((0,), (0,)), ((), ()))
HBM = pl.BlockSpec(memory_space=pltpu.HBM)
ANY = pl.BlockSpec(memory_space=pl.ANY)
SEM = pl.BlockSpec(memory_space=pltpu.SEMAPHORE)
EFFECT = pltpu.SideEffectType.DATAFLOW_SIDE_EFFECTING


def _dot(a, b, dn):
    return lax.dot_general(a, b, dn, preferred_element_type=F32)


def _pick(dim, pref, align=128):
    if dim <= pref:
        return dim
    t = (pref // align) * align
    while t >= align:
        if dim % t == 0:
            return t
        t -= align
    return dim


def _params(sem):
    return pltpu.CompilerParams(dimension_semantics=sem, vmem_limit_bytes=VMEM_LIMIT)


MM_CHUNK = 512


def _col_chunks(tn):
    return [slice(c0, min(c0 + MM_CHUNK, tn)) for c0 in range(0, tn, MM_CHUNK)]


def _mm_body(dn, nk, has_add, has_after, *refs):
    a_ref, b_ref = refs[:2]
    add_ref = refs[2] if has_add else None
    o_ref = refs[2 + has_add + has_after]
    chunks = _col_chunks(o_ref.shape[-1])

    def dot(cols):
        return _dot(a_ref[...], b_ref[cols, :] if dn == NT else b_ref[:, cols], dn)

    def finish(cols, r):
        if add_ref is not None:
            r = r + add_ref[:, cols]
        o_ref[:, cols] = r.astype(o_ref.dtype)

    if nk == 1:
        for cols in chunks:
            finish(cols, dot(cols))
        return
    acc_ref = refs[-1]
    k = pl.program_id(2)

    @pl.when(k == 0)
    def _():
        for cols in chunks:
            acc_ref[:, cols] = dot(cols)

    if nk > 2:
        @pl.when(jnp.logical_and(k > 0, k < nk - 1))
        def _():
            for cols in chunks:
                acc_ref[:, cols] += dot(cols)

    @pl.when(k == nk - 1)
    def _():
        for cols in chunks:
            finish(cols, acc_ref[:, cols] + dot(cols))


def _matmul(a, b, mode, out_dtype, name, *, tm, tn, tk=None, add=None, b_slab=None, out_slab=None, after=None):
    if mode == "nn":
        M, K = a.shape
        N = b.shape[0] * b.shape[2] if b_slab == "c" else b.shape[1]
    elif mode == "nt":
        M, K = a.shape
        N = b.shape[1] if b_slab == "k" else b.shape[0]
    else:
        K, M = a.shape
        N = b.shape[1]
    tk = K if tk is None else tk
    tm, tn, tk = _pick(M, tm), _pick(N, tn), _pick(K, tk)
    if b_slab == "c":
        tn = _pick(b.shape[2], tn)
    if b_slab == "k":
        tk = _pick(b.shape[2], tk)
    if out_slab == "c":
        tn = _pick(N // N_CHIPS, tn)
    if out_slab == "r":
        tm = _pick(M // N_CHIPS, tm)
    gm, gn, gk = M // tm, N // tn, K // tk

    if mode == "tn":
        a_spec = pl.BlockSpec((tk, tm), lambda j, i, k: (k, i))
        b_spec = pl.BlockSpec((tk, tn), lambda j, i, k: (k, j))
    else:
        a_spec = pl.BlockSpec((tm, tk), lambda j, i, k: (i, k))
        if b_slab == "c":
            per = b.shape[2] // tn
            b_spec = pl.BlockSpec((None, tk, tn), lambda j, i, k: (j // per, k, j % per))
        elif b_slab == "k":
            per = b.shape[2] // tk
            b_spec = pl.BlockSpec((None, tn, tk), lambda j, i, k: (k // per, j, k % per))
        elif mode == "nn":
            b_spec = pl.BlockSpec((tk, tn), lambda j, i, k: (k, j))
        else:
            b_spec = pl.BlockSpec((tn, tk), lambda j, i, k: (j, k))

    if out_slab == "c":
        per = (N // N_CHIPS) // tn
        o_spec = pl.BlockSpec((None, tm, tn), lambda j, i, k: (j // per, i, j % per))
        o_shape = jax.ShapeDtypeStruct((N_CHIPS, M, N // N_CHIPS), out_dtype)
    elif out_slab == "r":
        per = (M // N_CHIPS) // tm
        o_spec = pl.BlockSpec((None, tm, tn), lambda j, i, k: (i // per, i % per, j))
        o_shape = jax.ShapeDtypeStruct((N_CHIPS, M // N_CHIPS, N), out_dtype)
    else:
        o_spec = pl.BlockSpec((tm, tn), lambda j, i, k: (i, j))
        o_shape = jax.ShapeDtypeStruct((M, N), out_dtype)

    dn = {"nn": NN, "nt": NT, "tn": TN}[mode]
    in_specs = [a_spec, b_spec]
    args = [a, b]
    if add is not None:
        in_specs.append(pl.BlockSpec((tm, tn), lambda j, i, k: (i, j)))
        args.append(add)
    if after is not None:
        in_specs.append(ANY)
        args.append(after)
    return pl.pallas_call(
        functools.partial(_mm_body, dn, gk, add is not None, after is not None),
        name=name,
        grid=(gn, gm, gk),
        in_specs=in_specs,
        out_specs=o_spec,
        out_shape=o_shape,
        scratch_shapes=[pltpu.VMEM((tm, tn), F32)] if gk > 1 else [],
        compiler_params=_params(("parallel", "parallel", "arbitrary")),
    )(*args)


def _rms_fwd_body(x_ref, g_ref, *rest):
    h_ref = rest[-1]
    x = x_ref[...]
    r = lax.rsqrt(jnp.mean(x * x, axis=-1, keepdims=True) + EPS)
    h_ref[...] = (x * r * g_ref[...]).astype(h_ref.dtype)


def _rms_fwd(x, g, name, after=None):
    T, D = x.shape
    tr = _pick(T, 256, 16)
    extra = [] if after is None else [after]
    return pl.pallas_call(
        functools.partial(_rms_fwd_body),
        name=name,
        grid=(T // tr,),
        in_specs=[pl.BlockSpec((tr, D), lambda i: (i, 0)), pl.BlockSpec((1, D), lambda i: (0, 0))] + [ANY] * len(extra),
        out_specs=pl.BlockSpec((tr, D), lambda i: (i, 0)),
        out_shape=jax.ShapeDtypeStruct((T, D), BF16),
        compiler_params=_params(("parallel",)),
    )(x, g, *extra)


def _rms_bwd_body(with_bf16, x_ref, g_ref, dh_ref, res_ref, dx_ref, *rest):
    dg_ref = rest[-1]

    @pl.when(pl.program_id(0) == 0)
    def _():
        dg_ref[...] = jnp.zeros_like(dg_ref)

    x = x_ref[...]
    r = lax.rsqrt(jnp.mean(x * x, axis=-1, keepdims=True) + EPS)
    xh = x * r
    dh = dh_ref[...]
    dg_ref[...] += jnp.sum(dh * xh, axis=0, keepdims=True)
    t = dh * g_ref[...]
    dx = res_ref[...] + r * (t - xh * jnp.mean(t * xh, axis=-1, keepdims=True))
    dx_ref[...] = dx
    if with_bf16:
        rest[0][...] = dx.astype(BF16)


def _rms_bwd(x, g, dh, res, name, with_bf16):
    T, D = x.shape
    tr = _pick(T, 256, 16)
    row = pl.BlockSpec((tr, D), lambda i: (i, 0))
    vec = pl.BlockSpec((1, D), lambda i: (0, 0))
    extra = [jax.ShapeDtypeStruct((T, D), BF16)] if with_bf16 else []
    return pl.pallas_call(
        functools.partial(_rms_bwd_body, with_bf16),
        name=name,
        grid=(T // tr,),
        in_specs=[row, vec, row, row],
        out_specs=[row] + [row] * len(extra) + [vec],
        out_shape=[jax.ShapeDtypeStruct((T, D), F32)] + extra + [jax.ShapeDtypeStruct((1, D), F32)],
        compiler_params=_params(("arbitrary",)),
    )(x, g, dh, res)


def _ffn_up_body(h_ref, wg_ref, wu_ref, a_ref, b_ref, f_ref):
    for cols in _col_chunks(a_ref.shape[-1]):
        a = _dot(h_ref[...], wg_ref[:, cols], NN)
        b = _dot(h_ref[...], wu_ref[:, cols], NN)
        a_ref[:, cols] = a
        b_ref[:, cols] = b
        f_ref[:, cols] = (a * (1.0 / (1.0 + jnp.exp(-a))) * b).astype(f_ref.dtype)


def _ffn_up(h, wg, wu):
    T, D = h.shape
    n, _, fs = wg.shape
    tm, tn = _pick(T, 512), fs
    hs = pl.BlockSpec((tm, D), lambda j, i: (i, 0))
    ws = pl.BlockSpec((None, D, tn), lambda j, i: (j, 0, 0))
    os = pl.BlockSpec((tm, tn), lambda j, i: (i, j))
    return pl.pallas_call(
        functools.partial(_ffn_up_body),
        name="ffn_up_fwd",
        grid=(n, T // tm),
        in_specs=[hs, ws, ws],
        out_specs=[os, os, os],
        out_shape=[jax.ShapeDtypeStruct((T, n * fs), F32), jax.ShapeDtypeStruct((T, n * fs), F32),
                   jax.ShapeDtypeStruct((T, n * fs), BF16)],
        compiler_params=_params(("parallel", "parallel")),
    )(h, wg, wu)


def _ffn_down_dx_body(dx_ref, wd_ref, a_ref, b_ref, after_ref, da_ref, db_ref):
    del after_ref
    for cols in _col_chunks(da_ref.shape[-1]):
        df = _dot(dx_ref[...], wd_ref[cols, :], NT)
        a = a_ref[:, cols]
        s = 1.0 / (1.0 + jnp.exp(-a))
        da_ref[:, cols] = (df * b_ref[:, cols] * (s * (1.0 + a * (1.0 - s)))).astype(da_ref.dtype)
        db_ref[:, cols] = (df * (a * s)).astype(db_ref.dtype)


def _ffn_down_dx(dx2b, wd, a, b, after):
    T, D = dx2b.shape
    F = wd.shape[0]
    tm, tn = _pick(T, 512), _pick(F, 1408)
    xs = pl.BlockSpec((tm, D), lambda j, i: (i, 0))
    ws = pl.BlockSpec((tn, D), lambda j, i: (j, 0))
    os = pl.BlockSpec((tm, tn), lambda j, i: (i, j))
    return pl.pallas_call(
        functools.partial(_ffn_down_dx_body),
        name="ffn_down_dx",
        grid=(F // tn, T // tm),
        in_specs=[xs, ws, os, os, ANY],
        out_specs=[os, os],
        out_shape=[jax.ShapeDtypeStruct((T, F), BF16), jax.ShapeDtypeStruct((T, F), BF16)],
        compiler_params=_params(("parallel", "parallel")),
    )(dx2b, wd, a, b, after)


def _ffn_down_loss_body(nk, inv_d, f_ref, wd_ref, x1_ref, tgt_ref, dx2_ref, dx2b_ref, loss_ref, *scratch):
    j, i, k = pl.program_id(0), pl.program_id(1), pl.program_id(2)
    chunks = _col_chunks(dx2_ref.shape[-1])

    def dot(cols):
        return _dot(f_ref[...], wd_ref[:, cols], NN)

    @pl.when(jnp.logical_and(jnp.logical_and(j == 0, i == 0), k == 0))
    def _():
        loss_ref[...] = jnp.zeros_like(loss_ref)

    def finish(ffn_of):
        total = jnp.zeros((1, 1), F32)
        for cols in chunks:
            e = ffn_of(cols) + x1_ref[:, cols] - tgt_ref[:, cols]
            dx2 = e * inv_d
            dx2_ref[:, cols] = dx2
            dx2b_ref[:, cols] = dx2.astype(BF16)
            total = total + jnp.sum(jnp.sum(e * e, axis=-1, keepdims=True), axis=0, keepdims=True)
        loss_ref[...] += (0.5 * inv_d) * total

    if nk == 1:
        finish(dot)
        return
    acc_ref = scratch[0]

    @pl.when(k == 0)
    def _():
        for cols in chunks:
            acc_ref[:, cols] = dot(cols)

    if nk > 2:
        @pl.when(jnp.logical_and(k > 0, k < nk - 1))
        def _():
            for cols in chunks:
                acc_ref[:, cols] += dot(cols)

    @pl.when(k == nk - 1)
    def _():
        finish(lambda cols: acc_ref[:, cols] + dot(cols))


def _ffn_down_loss(f, wd, x1, tgt):
    T, F = f.shape
    D = wd.shape[1]
    tm, tn, tk = _pick(T, 1024), _pick(D, 1024), _pick(F, 1408)
    gm, gn, gk = T // tm, D // tn, F // tk
    tile = pl.BlockSpec((tm, tn), lambda j, i, k: (i, j))
    return pl.pallas_call(
        functools.partial(_ffn_down_loss_body, gk, 1.0 / D),
        name="ffn_down_loss",
        grid=(gn, gm, gk),
        in_specs=[pl.BlockSpec((tm, tk), lambda j, i, k: (i, k)), pl.BlockSpec((tk, tn), lambda j, i, k: (k, j)),
                  tile, tile],
        out_specs=[tile, tile, pl.BlockSpec((1, 1), lambda j, i, k: (0, 0))],
        out_shape=[jax.ShapeDtypeStruct((T, D), F32), jax.ShapeDtypeStruct((T, D), BF16),
                   jax.ShapeDtypeStruct((1, 1), F32)],
        scratch_shapes=[pltpu.VMEM((tm, tn), F32)] if gk > 1 else [],
        compiler_params=_params(("arbitrary", "arbitrary", "arbitrary")),
    )(f, wd, x1, tgt)


def _lo_mask(shape):
    return lax.broadcasted_iota(jnp.int32, shape, len(shape) - 1) < HEAD_DIM


def _half_sums(t, lo):
    s_lo = jnp.sum(jnp.where(lo, t, 0.0), axis=-1, keepdims=True)
    s_hi = jnp.sum(jnp.where(lo, 0.0, t), axis=-1, keepdims=True)
    return jnp.where(lo, s_lo, s_hi)


def _head_rstd(t, lo):
    return lax.rsqrt(_half_sums(t * t, lo) * (1.0 / HEAD_DIM) + EPS)


def _place(t, lo, kv_head):
    if kv_head == 0:
        t_lo = jnp.where(lo, t, 0.0)
        t_hi = pltpu.roll(t_lo, HEAD_DIM, 1)
    else:
        t_hi = jnp.where(lo, 0.0, t)
        t_lo = pltpu.roll(t_hi, HEAD_DIM, 1)
    return jnp.concatenate([t_lo, t_hi], axis=0).astype(BF16)


def _unplace(c0, c1, lo):
    return jnp.where(lo, c0 + pltpu.roll(c0, HEAD_DIM, 1), c1 + pltpu.roll(c1, HEAD_DIM, 1))


def _band(kv_cur, kv_prev, kg, lo2):
    kb = jnp.concatenate([kv_prev[:, :BLK], kv_cur[:, :BLK]], axis=0)
    vb = jnp.concatenate([kv_prev[:, BLK:], kv_cur[:, BLK:]], axis=0)
    rk = _head_rstd(kb, lo2)
    kn = kb * rk * kg
    kk = [_place(kn, lo2, h) for h in range(N_KV_HEADS)]
    vv = [_place(vb, lo2, h) for h in range(N_KV_HEADS)]
    return kb, rk, kk, vv


def _score_geometry(first_i32):
    qi = lax.broadcasted_iota(jnp.int32, (BLK, 4 * BLK), 0)
    col = lax.broadcasted_iota(jnp.int32, (BLK, 4 * BLK), 1)
    kj = col & (2 * BLK - 1)
    dist = qi + BLK - kj
    valid = (dist >= 0) & (dist < BLK) & (kj >= first_i32 * BLK)
    return col, dist.astype(F32), valid


def _pair_probs(qn, kk, col, distf, valid, slope0, slope1, sink0, sink1):
    s = _dot(qn.astype(BF16), kk, NT) * (HEAD_DIM ** -0.5)
    slope = jnp.where(col < 2 * BLK, slope0, slope1)
    logits = jnp.where(valid, s - slope * distf, NEG)
    probs, psink = [], []
    for hh, sk in ((0, sink0), (1, sink1)):
        l = logits[:, 2 * BLK * hh:2 * BLK * (hh + 1)]
        m = jnp.maximum(jnp.max(l, axis=-1, keepdims=True), sk)
        p = jnp.exp(l - m)
        es = jnp.exp(sk - m)
        inv = 1.0 / (jnp.sum(p, axis=-1, keepdims=True) + es)
        probs.append(p * inv)
        psink.append(es * inv)
    return probs, psink


def _gelu(z):
    return 0.5 * z * (1.0 + lax.erf(z * (0.5 ** 0.5)))


def _gelu_grad(z):
    return 0.5 * (1.0 + lax.erf(z * (0.5 ** 0.5))) + z * jnp.exp(-0.5 * z * z) * ((2.0 * jnp.pi) ** -0.5)


def _tril_w(w):
    r = lax.broadcasted_iota(jnp.int32, (BLK, BLK), 0)
    c = lax.broadcasted_iota(jnp.int32, (BLK, BLK), 1)
    return jnp.where(r >= c, w, 0.0), r >= c


def _gate_fwd_group(zu, zv, lg, lb, w, bcol):
    u = _gelu(zu)
    v = _gelu(zv)
    mu = jnp.mean(v, axis=-1, keepdims=True)
    vc = v - mu
    rs = lax.rsqrt(jnp.mean(vc * vc, axis=-1, keepdims=True) + EPS)
    vh = vc * rs
    vn = vh * lg + lb
    wt, tril = _tril_w(w)
    mixed = _dot(wt.astype(BF16), vn.astype(BF16), NN) + bcol
    return u, vh, rs, vn, wt, tril, mixed


class _Dims:
    def __init__(self, seq, attn_w, gate_w):
        self.seq, self.attn_w, self.gate_w = seq, attn_w, gate_w
        self.n_heads = attn_w // HEAD_DIM
        self.group = self.n_heads // N_KV_HEADS
        self.n_pairs = attn_w // BLK
        self.n_groups = gate_w // BLK
        self.kv_col = attn_w // (2 * BLK)
        self.u0 = attn_w + 2 * BLK
        self.v0 = self.u0 + gate_w
        self.in_w = self.v0 + gate_w
        self.slopes = [2.0 ** (-8.0 * (h + 1) / self.n_heads) for h in range(self.n_heads)]


def _mixer_fwd_body(d, sink_ref, proj_ref, kvp_ref, qg_ref, kg_ref, lg_ref, lb_ref, w_ref, b_ref, goa_ref, gog_ref,
                    ya_ref, yg_ref, y_ref):
    i = pl.program_id(0)
    first = (i % (d.seq // BLK) == 0).astype(jnp.int32)
    lo = _lo_mask((BLK, BLK))
    lo2 = _lo_mask((2 * BLK, BLK))
    kv_cur = proj_ref[:, d.attn_w:d.attn_w + 2 * BLK]
    _, _, kk, vv = _band(kv_cur, kvp_ref[...], kg_ref[...], lo2)
    col, distf, valid = _score_geometry(first)
    qg = qg_ref[...]
    for j in range(d.n_pairs):
        h0, h1 = 2 * j, 2 * j + 1
        kh = h0 // d.group
        q2 = proj_ref[:, BLK * j:BLK * (j + 1)]
        qn = q2 * _head_rstd(q2, lo) * qg
        probs, _ = _pair_probs(qn, kk[kh], col, distf, valid, d.slopes[h0], d.slopes[h1],
                               sink_ref[0, h0], sink_ref[0, h1])
        p = jnp.concatenate(probs, axis=1).astype(BF16)
        ya_ref[:, BLK * j:BLK * (j + 1)] = _dot(p, vv[kh], NN)
    for g in range(d.n_groups):
        zu = proj_ref[:, d.u0 + BLK * g:d.u0 + BLK * (g + 1)]
        zv = proj_ref[:, d.v0 + BLK * g:d.v0 + BLK * (g + 1)]
        u, _, _, _, _, _, mixed = _gate_fwd_group(zu, zv, lg_ref[g:g + 1, :], lb_ref[g:g + 1, :], w_ref[g], b_ref[g])
        yg_ref[:, BLK * g:BLK * (g + 1)] = u * mixed
    ya = ya_ref[...]
    ra = lax.rsqrt(jnp.mean(ya * ya, axis=-1, keepdims=True) + EPS)
    y_ref[:, :d.attn_w] = (ya * ra * goa_ref[...]).astype(y_ref.dtype)
    yg = yg_ref[...]
    rg = lax.rsqrt(jnp.mean(yg * yg, axis=-1, keepdims=True) + EPS)
    y_ref[:, d.attn_w:] = (yg * rg * gog_ref[...]).astype(y_ref.dtype)


def _mixer_specs(d, T):
    row = lambda w: pl.BlockSpec((BLK, w), lambda i: (i, 0))
    const2 = lambda a: pl.BlockSpec(a.shape, lambda i: (0, 0))
    const3 = lambda a: pl.BlockSpec(a.shape, lambda i: (0, 0, 0))
    kv_prev = pl.BlockSpec((BLK, 2 * BLK), lambda i: (jnp.maximum(i - 1, 0), d.kv_col))
    return row, const2, const3, kv_prev


def _mixer_fwd(d, proj, sinks, qg2, kg2, lg, lb, wsp, bcol, goa, gog):
    T = proj.shape[0]
    row, const2, const3, kv_prev = _mixer_specs(d, T)
    return pl.pallas_call(
        functools.partial(_mixer_fwd_body, d),
        name="mixer_fwd",
        grid=(T // BLK,),
        in_specs=[pl.BlockSpec(memory_space=pltpu.SMEM), row(d.in_w), kv_prev, const2(qg2), const2(kg2),
                  const2(lg), const2(lb), const3(wsp), const3(bcol), const2(goa), const2(gog)],
        out_specs=[row(d.attn_w), row(d.gate_w), row(d.attn_w + d.gate_w)],
        out_shape=[jax.ShapeDtypeStruct((T, d.attn_w), F32), jax.ShapeDtypeStruct((T, d.gate_w), F32),
                   jax.ShapeDtypeStruct((T, d.attn_w + d.gate_w), BF16)],
        compiler_params=_params(("parallel",)),
    )(sinks, proj, proj, qg2, kg2, lg, lb, wsp, bcol, goa, gog)


def _mixer_bwd_body(d, sink_ref, proj_ref, kvp_ref, ya_ref, yg_ref, dy_ref, qg_ref, kg_ref, lg_ref, lb_ref, w_ref,
                    b_ref, goa_ref, gog_ref,
                    dproj_ref, dkv_ref, dqg_ref, dkg_ref, dsk_ref, dlg_ref, dlb_ref, dw_ref, db_ref, dgoa_ref,
                    dgog_ref):
    i = pl.program_id(0)

    @pl.when(i == 0)
    def _():
        for r in (dqg_ref, dkg_ref, dsk_ref, dlg_ref, dlb_ref, dw_ref, db_ref, dgoa_ref, dgog_ref):
            r[...] = jnp.zeros_like(r)

    first = (i % (d.seq // BLK) == 0).astype(jnp.int32)
    lo = _lo_mask((BLK, BLK))
    lo2 = _lo_mask((2 * BLK, BLK))
    lane_row = lax.broadcasted_iota(jnp.int32, (1, BLK), 1)

    ya = ya_ref[...]
    ra = lax.rsqrt(jnp.mean(ya * ya, axis=-1, keepdims=True) + EPS)
    yah = ya * ra
    dyn = dy_ref[:, :d.attn_w]
    dgoa_ref[...] += jnp.sum(dyn * yah, axis=0, keepdims=True)
    t = dyn * goa_ref[...]
    dya = ra * (t - yah * jnp.mean(t * yah, axis=-1, keepdims=True))
    yg = yg_ref[...]
    rg = lax.rsqrt(jnp.mean(yg * yg, axis=-1, keepdims=True) + EPS)
    ygh = yg * rg
    dyn = dy_ref[:, d.attn_w:]
    dgog_ref[...] += jnp.sum(dyn * ygh, axis=0, keepdims=True)
    t = dyn * gog_ref[...]
    dyg = rg * (t - ygh * jnp.mean(t * ygh, axis=-1, keepdims=True))

    kv_cur = proj_ref[:, d.attn_w:d.attn_w + 2 * BLK]
    kg = kg_ref[...]
    kb, rk, kk, vv = _band(kv_cur, kvp_ref[...], kg, lo2)
    col, distf, valid = _score_geometry(first)
    qg = qg_ref[...]
    ck = [jnp.zeros((2 * BLK, BLK), F32) for _ in range(N_KV_HEADS)]
    cv = [jnp.zeros((2 * BLK, BLK), F32) for _ in range(N_KV_HEADS)]
    dsk = jnp.zeros((1, BLK), F32)
    dqg = jnp.zeros((1, BLK), F32)
    for j in range(d.n_pairs):
        h0, h1 = 2 * j, 2 * j + 1
        kh = h0 // d.group
        cols = slice(BLK * j, BLK * (j + 1))
        q2 = proj_ref[:, cols]
        rq = _head_rstd(q2, lo)
        qh = q2 * rq
        qn = qh * qg
        probs, psink = _pair_probs(qn, kk[kh], col, distf, valid, d.slopes[h0], d.slopes[h1],
                                   sink_ref[0, h0], sink_ref[0, h1])
        do2 = dya[:, cols]
        prod = do2 * ya[:, cols]
        delta = (jnp.sum(jnp.where(lo, prod, 0.0), axis=-1, keepdims=True),
                 jnp.sum(jnp.where(lo, 0.0, prod), axis=-1, keepdims=True))
        do2b = do2.astype(BF16)
        dp = _dot(do2b, vv[kh], NT)
        ds = []
        for hh in (0, 1):
            ds.append(probs[hh] * (dp[:, 2 * BLK * hh:2 * BLK * (hh + 1)] - delta[hh]))
            dsink = -jnp.sum(psink[hh] * delta[hh], axis=0, keepdims=True)
            dsk = dsk + jnp.where(lane_row == (h0 + hh), dsink, 0.0)
        dsb = (jnp.concatenate(ds, axis=1) * (HEAD_DIM ** -0.5)).astype(BF16)
        pb = jnp.concatenate(probs, axis=1).astype(BF16)
        qnb = qn.astype(BF16)
        dqn = _dot(dsb, kk[kh], NN)
        dkk = _dot(dsb, qnb, TN)
        dvv = _dot(pb, do2b, TN)
        ck[kh] = ck[kh] + jnp.where(lo2, dkk[:2 * BLK], 0.0) + jnp.where(lo2, 0.0, dkk[2 * BLK:])
        cv[kh] = cv[kh] + jnp.where(lo2, dvv[:2 * BLK], 0.0) + jnp.where(lo2, 0.0, dvv[2 * BLK:])
        dqg = dqg + jnp.sum(dqn * qh, axis=0, keepdims=True)
        t = dqn * qg
        dq2 = rq * (t - qh * (_half_sums(t * qh, lo) * (1.0 / HEAD_DIM)))
        dproj_ref[:, cols] = dq2.astype(dproj_ref.dtype)
    dsk_ref[...] += dsk
    dqg_ref[...] += dqg
    dkn = _unplace(ck[0], ck[1], lo2)
    dvb = _unplace(cv[0], cv[1], lo2)
    khat = kb * rk
    dkg_ref[...] += jnp.sum(dkn * khat, axis=0, keepdims=True)
    t = dkn * kg
    dkb = rk * (t - khat * (_half_sums(t * khat, lo2) * (1.0 / HEAD_DIM)))
    rows_cur = pl.ds(pl.multiple_of(i * BLK, BLK), BLK)
    rows_prev = pl.ds(pl.multiple_of(jnp.maximum(i - 1, 0) * BLK, BLK), BLK)
    dkv_ref[rows_cur, :] = jnp.concatenate([dkb[BLK:], dvb[BLK:]], axis=1)
    dkv_ref[rows_prev, :] += jnp.concatenate([dkb[:BLK], dvb[:BLK]], axis=1)
    dproj_ref[:, d.attn_w:d.attn_w + 2 * BLK] = jnp.zeros((BLK, 2 * BLK), dproj_ref.dtype)

    for g in range(d.n_groups):
        ucols = slice(d.u0 + BLK * g, d.u0 + BLK * (g + 1))
        vcols = slice(d.v0 + BLK * g, d.v0 + BLK * (g + 1))
        zu = proj_ref[:, ucols]
        zv = proj_ref[:, vcols]
        lg = lg_ref[g:g + 1, :]
        u, vh, rs, vn, wt, tril, mixed = _gate_fwd_group(zu, zv, lg, lb_ref[g:g + 1, :], w_ref[g], b_ref[g])
        dyg_g = dyg[:, BLK * g:BLK * (g + 1)]
        du = dyg_g * mixed
        dmix = dyg_g * u
        dmb = dmix.astype(BF16)
        db_ref[g:g + 1, :] += jnp.sum(jnp.transpose(dmix), axis=0, keepdims=True)
        dw_ref[g] += jnp.where(tril, _dot(dmb, vn.astype(BF16), NT), 0.0)
        dvn = _dot(wt.astype(BF16), dmb, TN)
        dlg_ref[g:g + 1, :] += jnp.sum(dvn * vh, axis=0, keepdims=True)
        dlb_ref[g:g + 1, :] += jnp.sum(dvn, axis=0, keepdims=True)
        dvh = dvn * lg
        dv = rs * (dvh - jnp.mean(dvh, axis=-1, keepdims=True) - vh * jnp.mean(dvh * vh, axis=-1, keepdims=True))
        dproj_ref[:, ucols] = (du * _gelu_grad(zu)).astype(dproj_ref.dtype)
        dproj_ref[:, vcols] = (dv * _gelu_grad(zv)).astype(dproj_ref.dtype)


def _mixer_bwd(d, proj, ya, yg, dy, sinks, qg2, kg2, lg, lb, wsp, bcol, goa, gog):
    T = proj.shape[0]
    row, const2, const3, kv_prev = _mixer_specs(d, T)
    acc2 = lambda s: pl.BlockSpec(s, lambda i: (0, 0))
    G = d.n_groups
    out_shapes = [((T, d.in_w), BF16), ((T, 2 * BLK), F32), ((1, BLK), F32), ((1, BLK), F32), ((1, BLK), F32),
                  ((G, BLK), F32), ((G, BLK), F32), ((G, BLK, BLK), F32), ((G, BLK), F32),
                  ((1, d.attn_w), F32), ((1, d.gate_w), F32)]
    out_specs = [row(d.in_w)] + [acc2(s) for s, _ in out_shapes[1:7]] + \
                [pl.BlockSpec((G, BLK, BLK), lambda i: (0, 0, 0))] + [acc2(s) for s, _ in out_shapes[8:]]
    return pl.pallas_call(
        functools.partial(_mixer_bwd_body, d),
        name="mixer_bwd",
        grid=(T // BLK,),
        in_specs=[pl.BlockSpec(memory_space=pltpu.SMEM), row(d.in_w), kv_prev, row(d.attn_w), row(d.gate_w),
                  row(d.attn_w + d.gate_w), const2(qg2), const2(kg2), const2(lg), const2(lb), const3(wsp),
                  const3(bcol), const2(goa), const2(gog)],
        out_specs=out_specs,
        out_shape=[jax.ShapeDtypeStruct(s, t) for s, t in out_shapes],
        compiler_params=_params(("arbitrary",)),
    )(sinks, proj, proj, ya, yg, dy, qg2, kg2, lg, lb, wsp, bcol, goa, gog)


def _put_kv_body(dkv_ref, dproj_in_ref, dproj_ref):
    del dproj_in_ref
    dproj_ref[...] = dkv_ref[...].astype(dproj_ref.dtype)


def _put_kv(d, dproj, dkv):
    T = dproj.shape[0]
    tr = _pick(T, 1024, 16)
    return pl.pallas_call(
        functools.partial(_put_kv_body),
        name="put_kv",
        grid=(T // tr,),
        in_specs=[pl.BlockSpec((tr, 2 * BLK), lambda i: (i, 0)), pl.BlockSpec(memory_space=pl.ANY)],
        out_specs=pl.BlockSpec((tr, 2 * BLK), lambda i: (i, d.kv_col)),
        out_shape=jax.ShapeDtypeStruct(dproj.shape, dproj.dtype),
        input_output_aliases={1: 0},
        compiler_params=_params(("parallel",)),
    )(dkv, dproj)


def _add_pair_body(pc_ref, own_ref, got_ref, o_ref):
    del pc_ref
    o_ref[...] = (own_ref[...].astype(F32) + got_ref[...].astype(F32)).astype(o_ref.dtype)


def _add_pair(g4, got, pc, name):
    n, _, h, C = g4.shape
    tr = _pick(h, 512, 16)
    return pl.pallas_call(
        functools.partial(_add_pair_body),
        name=name,
        grid_spec=pltpu.PrefetchScalarGridSpec(
            num_scalar_prefetch=1,
            grid=(n, h // tr),
            in_specs=[pl.BlockSpec((None, None, tr, C), lambda q, i, pc: (q, pc[1], i, 0)),
                      pl.BlockSpec((None, tr, C), lambda q, i, pc: (q, i, 0))],
            out_specs=pl.BlockSpec((None, tr, C), lambda q, i, pc: (q, i, 0)),
        ),
        out_shape=jax.ShapeDtypeStruct((n, h, C), g4.dtype),
        compiler_params=_params(("parallel", "parallel")),
    )(pc, g4, got)


def _adamw_update(w, g, m, v):
    m = ADAM_B1 * m + (1.0 - ADAM_B1) * g
    v = ADAM_B2 * v + (1.0 - ADAM_B2) * (g * g)
    m_hat = m / (1.0 - ADAM_B1 ** ADAM_STEP)
    v_hat = v / (1.0 - ADAM_B2 ** ADAM_STEP)
    return -ADAM_LR * (m_hat / (jnp.sqrt(v_hat) + ADAM_EPS) + ADAM_WD * w), m, v


def _adamw_body(w_ref, g_ref, m_ref, v_ref, d_ref, nm_ref, nv_ref):
    d_ref[...], nm_ref[...], nv_ref[...] = _adamw_update(w_ref[...], g_ref[...], m_ref[...], v_ref[...])


def _adamw(w, g, m, v, name):
    R, C = w.shape
    tr = _pick(R, 512, 8)
    blk = pl.BlockSpec((tr, C), lambda i: (i, 0))
    return pl.pallas_call(
        functools.partial(_adamw_body),
        name=name,
        grid=(R // tr,),
        in_specs=[blk] * 4,
        out_specs=[blk] * 3,
        out_shape=[jax.ShapeDtypeStruct((R, C), F32)] * 3,
        compiler_params=_params(("parallel",)),
    )(w, g, m, v)


def _adamw_halves_body(pc_ref, w_ref, own_ref, got_ref, m_ref, v_ref, g_ref, d_ref, nm_ref, nv_ref):
    mine = pl.program_id(0) == pc_ref[1]

    def update(g):
        g_ref[...] = g
        d_ref[...], nm_ref[...], nv_ref[...] = _adamw_update(w_ref[...], g, m_ref[...], v_ref[...])

    @pl.when(mine)
    def _():
        update(own_ref[...].astype(F32))

    @pl.when(jnp.logical_not(mine))
    def _():
        update(got_ref[...].astype(F32))


def _adamw_halves(w, own, got, m, v, pc, name):
    h, C = own.shape
    tr = _pick(h, 512, 8)
    full = pl.BlockSpec((None, tr, C), lambda hh, i, pc: (hh, i, 0))
    mine = pl.BlockSpec((tr, C), lambda hh, i, pc: (jnp.where(hh == pc[1], i, 0), 0))
    theirs = pl.BlockSpec((tr, C), lambda hh, i, pc: (jnp.where(hh == pc[1], 0, i), 0))
    return pl.pallas_call(
        functools.partial(_adamw_halves_body),
        name=name,
        grid_spec=pltpu.PrefetchScalarGridSpec(
            num_scalar_prefetch=1,
            grid=(2, h // tr),
            in_specs=[full, mine, theirs, full, full],
            out_specs=[full] * 4,
        ),
        out_shape=[jax.ShapeDtypeStruct((2, h, C), F32)] * 4,
        compiler_params=_params(("parallel", "parallel")),
    )(pc, w.reshape(2, h, C), own, got, m.reshape(2, h, C), v.reshape(2, h, C))


def _me():
    x, y, c = lax.axis_index("x"), lax.axis_index("y"), lax.axis_index("c")
    chips = [(1 - x, y), (x, 1 - y), (1 - x, 1 - y)]
    return x, y, c, chips


def _cast_into_body(pc_ref, w_ref, o_ref):
    del pc_ref
    o_ref[...] = w_ref[...].astype(o_ref.dtype)


def _cast_into(w, pc, name):
    Rs, C = w.shape
    h = Rs // 2
    tr = _pick(h, 512, 16)
    return pl.pallas_call(
        functools.partial(_cast_into_body),
        name=name,
        grid_spec=pltpu.PrefetchScalarGridSpec(
            num_scalar_prefetch=1,
            grid=(2, h // tr),
            in_specs=[pl.BlockSpec((None, tr, C), lambda hh, i, pc: (hh, i, 0))],
            out_specs=pl.BlockSpec((None, None, tr, C), lambda hh, i, pc: (pc[0], hh, i, 0)),
        ),
        out_shape=jax.ShapeDtypeStruct((N_CHIPS, 2, h, C), BF16),
        compiler_params=_params(("parallel", "parallel")),
    )(pc, w.reshape(2, h, C))


MAX_PIECES = 4


def _send_tile_to_sibling(src_of, dst_of, tr, dst_total, send_sems, recv_sem, last):
    x, y, c, _ = _me()
    pieces = MAX_PIECES if tr % (16 * MAX_PIECES) == 0 else (2 if tr % 32 == 0 else 1)
    n = tr // pieces
    copies = [pltpu.make_async_remote_copy(src_ref=src_of(k * n, n), dst_ref=dst_of(k * n, n), send_sem=send_sems.at[k],
                                           recv_sem=recv_sem, device_id=(x, y, 1 - c), device_id_type=MESH)
              for k in range(pieces)]
    for cp in copies:
        cp.start()
    for cp in copies:
        cp.wait_send()

    @pl.when(last)
    def _():
        pltpu.make_async_remote_copy(src_ref=dst_total, dst_ref=dst_total, send_sem=send_sems.at[0], recv_sem=recv_sem,
                                     device_id=(x, y, 1 - c), device_id_type=MESH).wait_recv()


TILE_SEMS = [pltpu.SemaphoreType.DMA((MAX_PIECES,)), pltpu.SemaphoreType.DMA(())]


def _ag_pair_body(tr, n_i, pc_ref, tile_ref, buf_ref, send_sem, recv_sem):
    j, i = pl.program_id(0), pl.program_id(1)
    q = pc_ref[0] ^ (j + 1)
    c = pc_ref[1]
    r_tile = pl.multiple_of(i * tr, tr)
    _send_tile_to_sibling(lambda r0, n: tile_ref.at[:, :, pl.ds(r0, n)],
                          lambda r0, n: buf_ref.at[pl.ds(q, 1), pl.ds(c, 1), pl.ds(r_tile + r0, n)], tr,
                          buf_ref.at[pl.ds(0, N_CHIPS - 1), 0], send_sem, recv_sem,
                          jnp.logical_and(j == N_CHIPS - 2, i == n_i - 1))


def _ag_pair(buf, pc, name):
    _, _, h, C = buf.shape
    tr = _pick(h, 512, 16)
    return pl.pallas_call(
        functools.partial(_ag_pair_body, tr, h // tr),
        name=name,
        grid_spec=pltpu.PrefetchScalarGridSpec(
            num_scalar_prefetch=1,
            grid=(N_CHIPS - 1, h // tr),
            in_specs=[pl.BlockSpec((1, 1, tr, C), lambda j, i, pc: (pc[0] ^ (j + 1), pc[1], i, 0))],
            out_specs=HBM,
            scratch_shapes=TILE_SEMS,
        ),
        out_shape=jax.ShapeDtypeStruct(buf.shape, buf.dtype),
        input_output_aliases={1: 0},
        compiler_params=_params(("arbitrary", "arbitrary")),
    )(pc, buf)


def _swap_halves_body(tr, n_q, n_i, pc_ref, tile_ref, got_ref, send_sem, recv_sem):
    del pc_ref
    q, i = pl.program_id(0), pl.program_id(1)
    r_tile = pl.multiple_of(i * tr, tr)
    _send_tile_to_sibling(lambda r0, n: tile_ref.at[:, :, pl.ds(r0, n)],
                          lambda r0, n: got_ref.at[pl.ds(q, 1), :, pl.ds(r_tile + r0, n)], tr, got_ref, send_sem, recv_sem,
                          jnp.logical_and(q == n_q - 1, i == n_i - 1))


def _swap_halves(g4, pc, name):
    n, _, h, C = g4.shape
    tr = _pick(h, 512, 16)
    return pl.pallas_call(
        functools.partial(_swap_halves_body, tr, n, h // tr),
        name=name,
        grid_spec=pltpu.PrefetchScalarGridSpec(
            num_scalar_prefetch=1,
            grid=(n, h // tr),
            in_specs=[pl.BlockSpec((1, 1, tr, C), lambda q, i, pc: (q, 1 - pc[1], i, 0))],
            out_specs=HBM,
            scratch_shapes=TILE_SEMS,
        ),
        out_shape=jax.ShapeDtypeStruct((n, 1, h, C), g4.dtype),
        compiler_params=_params(("arbitrary", "arbitrary")),
    )(pc, g4).reshape(n, h, C)


def _ici_copy(src, dst, send_sems, recv_sems, j, chip, c):
    return pltpu.make_async_remote_copy(src_ref=src, dst_ref=dst, send_sem=send_sems.at[j], recv_sem=recv_sems.at[j],
                                        device_id=(chip[0], chip[1], c), device_id_type=MESH)


def _token_spec():
    return jax.ShapeDtypeStruct((8, BLK), F32), pl.BlockSpec(memory_space=pltpu.VMEM)


def _ag_start_body(buf_ref, after_ref, send_sems, recv_sems, buf_thru, token_ref):
    del after_ref, buf_thru
    x, y, c, chips = _me()
    mine = buf_ref.at[2 * x + y, c]
    for j, chip in enumerate(chips):
        _ici_copy(mine, mine, send_sems, recv_sems, j, chip, c).start()
    token_ref[...] = jnp.zeros_like(token_ref)


def _ag_start(buf, after, name):
    tok_shape, tok_spec = _token_spec()
    sems = pltpu.SemaphoreType.DMA((N_CHIPS - 1,))
    return pl.pallas_call(
        functools.partial(_ag_start_body),
        name=name,
        in_specs=[HBM, ANY],
        out_specs=[SEM, SEM, HBM, tok_spec],
        out_shape=[sems, sems, pltpu.HBM(buf.shape, buf.dtype), tok_shape],
        input_output_aliases={0: 2},
        compiler_params=pltpu.CompilerParams(has_side_effects=EFFECT),
    )(pltpu.with_memory_space_constraint(buf, pltpu.HBM), after)


def _ag_wait_body(buf_ref, send_sems, recv_sems, after_ref, buf_out):
    del after_ref, buf_out
    x, y, c, chips = _me()
    mine = buf_ref.at[2 * x + y, c]
    for j, chip in enumerate(chips):
        theirs = buf_ref.at[2 * chip[0] + chip[1], c]
        _ici_copy(mine, mine, send_sems, recv_sems, j, chip, c).wait_send()
        _ici_copy(theirs, theirs, send_sems, recv_sems, j, chip, c).wait_recv()


def _ag_wait(buf, send_sems, recv_sems, after, name):
    return pl.pallas_call(
        functools.partial(_ag_wait_body),
        name=name,
        in_specs=[HBM, SEM, SEM, ANY],
        out_specs=HBM,
        out_shape=pltpu.HBM(buf.shape, buf.dtype),
        input_output_aliases={0: 0},
        compiler_params=pltpu.CompilerParams(has_side_effects=EFFECT),
    )(buf, send_sems, recv_sems, after)


def _rs_start_body(pair_ref, land_ref, after_ref, send_sems, recv_sems, pair_thru, land_thru, token_ref):
    del after_ref, pair_thru, land_thru
    x, y, c, chips = _me()
    for j, chip in enumerate(chips):
        _ici_copy(pair_ref.at[2 * chip[0] + chip[1]], land_ref.at[j], send_sems, recv_sems, j, chip, c).start()
    token_ref[...] = jnp.zeros_like(token_ref)


def _rs_start(pair, after, name):
    n, h, C = pair.shape
    tok_shape, tok_spec = _token_spec()
    sems = pltpu.SemaphoreType.DMA((N_CHIPS - 1,))
    land = pltpu.with_memory_space_constraint(lax.empty((N_CHIPS - 1, h, C), pair.dtype), pltpu.HBM)
    return pl.pallas_call(
        functools.partial(_rs_start_body),
        name=name,
        in_specs=[HBM, HBM, ANY],
        out_specs=[SEM, SEM, HBM, HBM, tok_spec],
        out_shape=[sems, sems, pltpu.HBM(pair.shape, pair.dtype), pltpu.HBM(land.shape, land.dtype), tok_shape],
        input_output_aliases={0: 2, 1: 3},
        compiler_params=pltpu.CompilerParams(has_side_effects=EFFECT),
    )(pltpu.with_memory_space_constraint(pair, pltpu.HBM), land, after)


def _rs_wait_body(pair_ref, land_ref, send_sems, recv_sems, after_ref, pair_out, land_out):
    del after_ref, pair_out, land_out
    x, y, c, chips = _me()
    for j, chip in enumerate(chips):
        _ici_copy(pair_ref.at[0], land_ref.at[j], send_sems, recv_sems, j, chip, c).wait_send()
        _ici_copy(pair_ref.at[0], land_ref.at[j], send_sems, recv_sems, j, chip, c).wait_recv()


def _rs_wait(pair, land, send_sems, recv_sems, after, name):
    return pl.pallas_call(
        functools.partial(_rs_wait_body),
        name=name,
        in_specs=[HBM, HBM, SEM, SEM, ANY],
        out_specs=[HBM, HBM],
        out_shape=[pltpu.HBM(pair.shape, pair.dtype), pltpu.HBM(land.shape, land.dtype)],
        input_output_aliases={0: 0, 1: 1},
        compiler_params=pltpu.CompilerParams(has_side_effects=EFFECT),
    )(pair, land, send_sems, recv_sems, after)


def _swap_copy(g4_ref, got_ref, send_sem, recv_sem):
    x, y, c, _ = _me()
    return pltpu.make_async_remote_copy(src_ref=g4_ref.at[:, 1 - c], dst_ref=got_ref, send_sem=send_sem,
                                        recv_sem=recv_sem, device_id=(x, y, 1 - c), device_id_type=MESH)


def _swap_start_body(g4_ref, got_ref, send_sem, recv_sem, g4_thru, got_thru, token_ref):
    del g4_thru, got_thru
    _swap_copy(g4_ref, got_ref, send_sem, recv_sem).start()
    token_ref[...] = jnp.zeros_like(token_ref)


def _swap_start(g4, name):
    n, _, h, C = g4.shape
    tok_shape, tok_spec = _token_spec()
    sem = pltpu.SemaphoreType.DMA(())
    got = pltpu.with_memory_space_constraint(lax.empty((n, h, C), g4.dtype), pltpu.HBM)
    return pl.pallas_call(
        functools.partial(_swap_start_body),
        name=name,
        in_specs=[HBM, HBM],
        out_specs=[SEM, SEM, HBM, HBM, tok_spec],
        out_shape=[sem, sem, pltpu.HBM(g4.shape, g4.dtype), pltpu.HBM(got.shape, got.dtype), tok_shape],
        input_output_aliases={0: 2, 1: 3},
        compiler_params=pltpu.CompilerParams(has_side_effects=EFFECT),
    )(pltpu.with_memory_space_constraint(g4, pltpu.HBM), got)


def _swap_wait_body(g4_ref, got_ref, send_sem, recv_sem, after_ref, g4_out, got_out):
    del after_ref, g4_out, got_out
    cp = _swap_copy(g4_ref, got_ref, send_sem, recv_sem)
    cp.wait_send()
    cp.wait_recv()


def _swap_wait(g4, got, send_sem, recv_sem, after, name):
    return pl.pallas_call(
        functools.partial(_swap_wait_body),
        name=name,
        in_specs=[HBM, HBM, SEM, SEM, ANY],
        out_specs=[HBM, HBM],
        out_shape=[pltpu.HBM(g4.shape, g4.dtype), pltpu.HBM(got.shape, got.dtype)],
        input_output_aliases={0: 0, 1: 1},
        compiler_params=pltpu.CompilerParams(has_side_effects=EFFECT),
    )(g4, got, send_sem, recv_sem, after)


def _add_chips_body(tr, n_i, pc_ref, own_ref, l0_ref, l1_ref, l2_ref, o_ref, got_ref, send_sems, recv_sem):
    del pc_ref
    i = pl.program_id(0)
    r = own_ref[...].astype(F32) + l0_ref[...].astype(F32)
    o_ref[...] = (r + l1_ref[...].astype(F32) + l2_ref[...].astype(F32)).astype(o_ref.dtype)
    r_tile = pl.multiple_of(i * tr, tr)
    _send_tile_to_sibling(lambda r0, n: o_ref.at[pl.ds(r0, n)], lambda r0, n: got_ref.at[pl.ds(r_tile + r0, n)], tr,
                          got_ref, send_sems, recv_sem, i == n_i - 1)


def _add_chips(pair, land, pc, name):
    _, h, C = pair.shape
    tr = _pick(h, 256, 16)
    slot = lambda j: pl.BlockSpec((None, tr, C), lambda i, pc: (j, i, 0))
    return pl.pallas_call(
        functools.partial(_add_chips_body, tr, h // tr),
        name=name,
        grid_spec=pltpu.PrefetchScalarGridSpec(
            num_scalar_prefetch=1,
            grid=(h // tr,),
            in_specs=[pl.BlockSpec((None, tr, C), lambda i, pc: (pc[0], i, 0)), slot(0), slot(1), slot(2)],
            out_specs=[pl.BlockSpec((tr, C), lambda i, pc: (i, 0)), HBM],
            scratch_shapes=TILE_SEMS,
        ),
        out_shape=[jax.ShapeDtypeStruct((h, C), pair.dtype), jax.ShapeDtypeStruct((h, C), pair.dtype)],
        compiler_params=_params(("arbitrary",)),
    )(pc, pair, land, land, land)


def _peer(r):
    x, y, c, _ = _me()
    return (x ^ ((r >> 2) & 1), y ^ ((r >> 1) & 1), c ^ (r & 1))


def _ar_start_body(x_ref, land_ref, send_sems, recv_sems, x_thru, land_thru, token_ref):
    del x_thru, land_thru
    for r in range(1, N_DEV):
        pltpu.make_async_remote_copy(src_ref=x_ref, dst_ref=land_ref.at[r - 1], send_sem=send_sems.at[r - 1],
                                     recv_sem=recv_sems.at[r - 1], device_id=_peer(r), device_id_type=MESH).start()
    token_ref[...] = jnp.zeros_like(token_ref)


def _ar_start(packed):
    tok_shape, tok_spec = _token_spec()
    sems = pltpu.SemaphoreType.DMA((N_DEV - 1,))
    land = pltpu.with_memory_space_constraint(lax.empty((N_DEV - 1,) + packed.shape, packed.dtype), pltpu.HBM)
    return pl.pallas_call(
        functools.partial(_ar_start_body),
        name="ar_start",
        in_specs=[HBM, HBM],
        out_specs=[SEM, SEM, HBM, HBM, tok_spec],
        out_shape=[sems, sems, pltpu.HBM(packed.shape, packed.dtype), pltpu.HBM(land.shape, land.dtype), tok_shape],
        input_output_aliases={0: 2, 1: 3},
        compiler_params=pltpu.CompilerParams(has_side_effects=EFFECT),
    )(pltpu.with_memory_space_constraint(packed, pltpu.HBM), land)


def _ar_wait_body(x_ref, land_ref, send_sems, recv_sems, after_ref, x_out, land_out):
    del after_ref, x_out, land_out
    for r in range(1, N_DEV):
        cp = pltpu.make_async_remote_copy(src_ref=x_ref, dst_ref=land_ref.at[r - 1], send_sem=send_sems.at[r - 1],
                                          recv_sem=recv_sems.at[r - 1], device_id=_peer(r), device_id_type=MESH)
        cp.wait_send()
        cp.wait_recv()


def _ar_wait(packed, land, send_sems, recv_sems, after):
    return pl.pallas_call(
        functools.partial(_ar_wait_body),
        name="ar_wait",
        in_specs=[HBM, HBM, SEM, SEM, ANY],
        out_specs=[HBM, HBM],
        out_shape=[pltpu.HBM(packed.shape, packed.dtype), pltpu.HBM(land.shape, land.dtype)],
        input_output_aliases={0: 0, 1: 1},
        compiler_params=pltpu.CompilerParams(has_side_effects=EFFECT),
    )(packed, land, send_sems, recv_sems, after)


def _ar_sum_body(me_ref, own_ref, *rest):
    o_ref = rest[N_DEV]
    acc = None
    for dev in range(N_DEV):
        term = jnp.where(me_ref[0] == dev, own_ref[...], rest[dev][...])
        acc = term if acc is None else acc + term
    o_ref[...] = acc


def _ar_sum(packed, land, me):
    R, C = packed.shape
    tr = _pick(R, 552, 8)
    own = pl.BlockSpec((tr, C), lambda i, me: (i, 0))
    slot = lambda dev: pl.BlockSpec((None, tr, C), lambda i, me: (jnp.maximum((dev ^ me[0]) - 1, 0), i, 0))
    return pl.pallas_call(
        functools.partial(_ar_sum_body),
        name="ar_sum",
        grid_spec=pltpu.PrefetchScalarGridSpec(
            num_scalar_prefetch=1,
            grid=(R // tr,),
            in_specs=[own] + [slot(dev) for dev in range(N_DEV)],
            out_specs=pl.BlockSpec((tr, C), lambda i, me: (i, 0)),
        ),
        out_shape=jax.ShapeDtypeStruct((R, C), F32),
        compiler_params=_params(("parallel",)),
    )(me, packed, *([land] * N_DEV))


def _pack(arrays):
    rows = []
    for a in arrays:
        flat = a.reshape(-1).astype(F32)
        pad = (-flat.shape[0]) % BLK
        rows.append(jnp.pad(flat, (0, pad)).reshape(-1, BLK))
    packed = jnp.concatenate(rows, axis=0)
    pad = (-packed.shape[0]) % 8
    return jnp.pad(packed, ((0, pad), (0, 0)))


def _unpack(packed, shapes):
    out, r = [], 0
    for s in shapes:
        n = 1
        for k in s:
            n *= k
        nr = -(-n // BLK)
        out.append(packed[r:r + nr].reshape(-1)[:n].reshape(s))
        r += nr
    return out


def kernel(x, norm1_g, w_in, q_norm_g, k_norm_g, attn_sinks, gate_ln_g, gate_ln_b, w_spatial, b_spatial, out_norm_attn_g, out_norm_gate_g, w_out, norm2_g, w_ffn_gate, w_ffn_up, w_ffn_down, loss_target, m_norm1_g, m_w_in, m_q_norm_g, m_k_norm_g, m_attn_sinks, m_gate_ln_g, m_gate_ln_b, m_w_spatial, m_b_spatial, m_out_norm_attn_g, m_out_norm_gate_g, m_w_out, m_norm2_g, m_w_ffn_gate, m_w_ffn_up, m_w_ffn_down, v_norm1_g, v_w_in, v_q_norm_g, v_k_norm_g, v_attn_sinks, v_gate_ln_g, v_gate_ln_b, v_w_spatial, v_b_spatial, v_out_norm_attn_g, v_out_norm_gate_g, v_w_out, v_norm2_g, v_w_ffn_gate, v_w_ffn_up, v_w_ffn_down):
    bl, seq, D = x.shape
    T = bl * seq
    attn_w, gate_w = out_norm_attn_g.shape[1], out_norm_gate_g.shape[1]
    d = _Dims(seq, attn_w, gate_w)
    G = d.n_groups
    in_w = d.in_w
    dff = w_ffn_gate.shape[2] * N_CHIPS
    assert w_in.shape[2] * N_CHIPS == in_w and seq % BLK == 0 and attn_w % (2 * BLK) == 0

    pc = jnp.stack([2 * lax.axis_index("x") + lax.axis_index("y"), lax.axis_index("c")]).astype(jnp.int32)
    big = [w_in[0], w_out[0], w_ffn_gate[0], w_ffn_up[0], w_ffn_down[0]]
    names = ["in", "out", "gate", "up", "down"]
    xf = x.reshape(T, D)
    tgt = loss_target.reshape(T, D)
    send, recv, buf, behind = _ag_start(_cast_into(big[0], pc, "cast_in"), norm1_g, "ag_start_in")
    started = [(send, recv, buf)]
    h1 = _rms_fwd(xf, norm1_g, "norm1_fwd", after=behind)
    behind = h1
    for w, n in zip(big[1:], names[1:]):
        send, recv, buf, behind = _ag_start(_cast_into(w, pc, "cast_" + n), behind, "ag_start_" + n)
        started.append((send, recv, buf))

    def gathered(k, after):
        send, recv, buf = started[k]
        buf = _ag_wait(buf, send, recv, after, "ag_wait_" + names[k])
        return _ag_pair(buf, pc, "ag_pair_" + names[k]).reshape((N_CHIPS,) + big[k].shape)

    qg2 = jnp.tile(q_norm_g, (1, 2))
    kg2 = jnp.tile(k_norm_g, (1, 2))
    lg, lb, wsp = gate_ln_g[0], gate_ln_b[0], w_spatial[0]
    bcol = jnp.broadcast_to(b_spatial[0][:, :, None], (G, BLK, BLK))

    win_full = jnp.transpose(gathered(0, behind), (1, 0, 2)).reshape(D, in_w)
    proj = _matmul(h1, win_full, "nn", F32, "proj_fwd", tm=1024, tn=1664)
    ya, yg, yn = _mixer_fwd(d, proj, attn_sinks, qg2, kg2, lg, lb, wsp, bcol, out_norm_attn_g, out_norm_gate_g)
    wout_full = gathered(1, yn).reshape(attn_w + gate_w, D)
    x1 = _matmul(yn, wout_full, "nn", F32, "out_fwd", tm=1024, tn=1024, add=xf)
    h2 = _rms_fwd(x1, norm2_g, "norm2_fwd")
    wg_g, wu_g = gathered(2, h2), gathered(3, h2)
    a, b, f = _ffn_up(h2, wg_g, wu_g)
    wd_full = gathered(4, f).reshape(dff, D)
    dx2, dx2b, loss_local = _ffn_down_loss(f, wd_full, x1, tgt)

    def swap_start(g, n):
        g4 = g.reshape(N_CHIPS, 2, g.shape[1] // 2, g.shape[2])
        return _swap_start(g4, "rs_swap_start_" + n)

    def reduce_start(swapping, n, after):
        send, recv, g4, got, _ = swapping
        g4, got = _swap_wait(g4, got, send, recv, after, "rs_swap_wait_" + n)
        return _rs_start(_add_pair(g4, got, pc, "rs_add_pair_" + n), got, "rs_start_" + n)

    reducing = {}
    g_d = _matmul(f, dx2b, "tn", BF16, "ffn_down_dw", tm=1408, tn=1024, tk=2048, out_slab="r")
    swap_d = swap_start(g_d, "down")
    da, db = _ffn_down_dx(dx2b, wd_full, a, b, swap_d[4])
    g_g = _matmul(h2, da, "tn", BF16, "ffn_gate_dw", tm=1024, tn=1408, tk=2048, out_slab="c")
    swap_g = swap_start(g_g, "gate")
    reducing["down"] = reduce_start(swap_d, "down", swap_g[4])
    g_u = _matmul(h2, db, "tn", BF16, "ffn_up_dw", tm=1024, tn=1408, tk=2048, out_slab="c",
                  after=reducing["down"][4])
    swap_u = swap_start(g_u, "up")
    reducing["gate"] = reduce_start(swap_g, "gate", swap_u[4])
    dh2 = _matmul(da, wg_g, "nt", F32, "ffn_gate_dx", tm=1024, tn=1024, tk=1408, b_slab="k",
                  after=reducing["gate"][4])
    dh2 = _matmul(db, wu_g, "nt", F32, "ffn_up_dx", tm=1024, tn=1024, tk=1408, b_slab="k", add=dh2)
    reducing["up"] = reduce_start(swap_u, "up", dh2)
    dx1, dx1b, dg_norm2 = _rms_bwd(x1, norm2_g, dh2, dx2, "norm2_bwd", True)
    g_o = _matmul(yn, dx1b, "tn", BF16, "out_dw", tm=512, tn=1024, tk=2048, out_slab="r",
                  after=reducing["up"][4])
    swap_o = swap_start(g_o, "out")
    dy = _matmul(dx1b, wout_full, "nt", F32, "out_dx", tm=1024, tn=1024, after=swap_o[4])
    (dproj, dkv, dqg, dkg, dsk, dlg, dlb, dwsp, dbsp, dgoa, dgog) = _mixer_bwd(
        d, proj, ya, yg, dy, attn_sinks, qg2, kg2, lg, lb, wsp, bcol, out_norm_attn_g, out_norm_gate_g)
    dproj = _put_kv(d, dproj, dkv)
    reducing["out"] = reduce_start(swap_o, "out", dproj)
    g_in_full = _matmul(h1, dproj, "tn", BF16, "proj_dw", tm=1024, tn=1664, tk=2048,
                        after=reducing["out"][4])
    g_i = jnp.transpose(g_in_full.reshape(D, N_CHIPS, in_w // N_CHIPS), (1, 0, 2))
    g4_i = g_i.reshape(N_CHIPS, 2, D // 2, in_w // N_CHIPS)
    pair_i = _add_pair(g4_i, _swap_halves(g4_i, pc, "rs_swap_in"), pc, "rs_add_pair_in")
    reducing["in"] = _rs_start(pair_i, g_i, "rs_start_in")
    dh1 = _matmul(dproj, win_full, "nt", F32, "proj_dx", tm=1024, tn=1024, after=reducing["in"][4])
    dx, dg_norm1 = _rms_bwd(xf, norm1_g, dh1, dx1, "norm1_bwd", False)

    dqg64 = dqg[:, :HEAD_DIM] + dqg[:, HEAD_DIM:]
    dkg64 = dkg[:, :HEAD_DIM] + dkg[:, HEAD_DIM:]
    small_g_local = [dg_norm1, dqg64, dkg64, dsk[:, :d.n_heads], dlg, dlb, dwsp, dbsp, dgoa, dgog, dg_norm2,
                     loss_local]
    ar_send, ar_recv, ar_own, ar_land, ar_token = _ar_start(_pack(small_g_local))

    big_m = [m_w_in[0], m_w_out[0], m_w_ffn_gate[0], m_w_ffn_up[0], m_w_ffn_down[0]]
    big_v = [v_w_in[0], v_w_out[0], v_w_ffn_gate[0], v_w_ffn_up[0], v_w_ffn_down[0]]
    big_grads, big_d, big_nm, big_nv = [], [], [], []
    for w, m, v, n in zip(big, big_m, big_v, names):
        send, recv, pair, land, _ = reducing[n]
        pair, land = _rs_wait(pair, land, send, recv, ar_token, "rs_wait_" + n)
        own, got = _add_chips(pair, land, pc, "rs_add_chips_" + n)
        outs = _adamw_halves(w, own, got, m, v, pc, "adamw_" + n)
        for lst, o in zip((big_grads, big_d, big_nm, big_nv), outs):
            lst.append(o.reshape(w.shape))

    small_names_w = [norm1_g, q_norm_g, k_norm_g, attn_sinks, gate_ln_g, gate_ln_b, w_spatial, b_spatial,
                     out_norm_attn_g, out_norm_gate_g, norm2_g]
    small_m = [m_norm1_g, m_q_norm_g, m_k_norm_g, m_attn_sinks, m_gate_ln_g, m_gate_ln_b, m_w_spatial, m_b_spatial,
               m_out_norm_attn_g, m_out_norm_gate_g, m_norm2_g]
    small_v = [v_norm1_g, v_q_norm_g, v_k_norm_g, v_attn_sinks, v_gate_ln_g, v_gate_ln_b, v_w_spatial, v_b_spatial,
               v_out_norm_attn_g, v_out_norm_gate_g, v_norm2_g]
    shapes = [w.shape for w in small_names_w] + [loss_local.shape]
    ride = [jnp.zeros(loss_local.shape, F32)]
    ar_own, ar_land = _ar_wait(ar_own, ar_land, ar_send, ar_recv, big_nv[-1])
    me = (4 * lax.axis_index("x") + 2 * lax.axis_index("y") + lax.axis_index("c")).astype(jnp.int32).reshape(1)
    sg = _ar_sum(ar_own, ar_land, me)
    sd, snm, snv = _adamw(_pack(small_names_w + ride), sg, _pack(small_m + ride), _pack(small_v + ride), "adamw_small")
    small_g, small_d, small_nm, small_nv = (_unpack(t, shapes) for t in (sg, sd, snm, snv))
    loss = small_g[-1][0, 0]

    def order(small, bigs):
        s = list(small)
        bg = [t[None] for t in bigs]
        return [s[0], bg[0], s[1], s[2], s[3], s[4], s[5], s[6], s[7], s[8], s[9], bg[1], s[10], bg[2], bg[3], bg[4]]

    grad_x = dx.reshape(bl, seq, D)
    return (loss, grad_x, *order(small_g, big_grads), *order(small_d, big_d), *order(small_nm, big_nm),
            *order(small_nv, big_nv))
```

```python
import functools

import jax
import jax.numpy as jnp
from jax import lax
from jax.experimental import pallas as pl
from jax.experimental.pallas import tpu as pltpu

F32 = jnp.float32
BF16 = jnp.bfloat16
MESH = pl.DeviceIdType.MESH

EPS = 1e-6
HEAD_DIM = 64
N_KV_HEADS = 2
BLK = 128
N_CHIPS = 4
N_DEV = 8
NEG = -1e30

ADAM_LR = 0.001
ADAM_B1 = 0.9
ADAM_B2 = 0.999
ADAM_EPS = 1e-08
ADAM_WD = 0.01
ADAM_STEP = 10

VMEM_LIMIT = 56 * 1024 * 1024

NN = (((1,), (0,)), ((), ()))
NT = (((1,), (1,)), ((), ()))
TN = (((0,), (0,)), ((), ()))
HBM = pl.BlockSpec(memory_space=pltpu.HBM)
ANY = pl.BlockSpec(memory_space=pl.ANY)
SEM = pl.BlockSpec(memory_space=pltpu.SEMAPHORE)
EFFECT = pltpu.SideEffectType.DATAFLOW_SIDE_EFFECTING


def _dot(a, b, dn):
    return lax.dot_general(a, b, dn, preferred_element_type=F32)


def _pick(dim, pref, align=128):
    if dim <= pref:
        return dim
    t = (pref // align) * align
    while t >= align:
        if dim % t == 0:
            return t
        t -= align
    return dim


def _params(sem):
    return pltpu.CompilerParams(dimension_semantics=sem, vmem_limit_bytes=VMEM_LIMIT)


MM_CHUNK = 512


def _col_chunks(tn):
    return [slice(c0, min(c0 + MM_CHUNK, tn)) for c0 in range(0, tn, MM_CHUNK)]


def _mm_body(dn, nk, has_add, has_after, *refs):
    a_ref, b_ref = refs[:2]
    add_ref = refs[2] if has_add else None
    o_ref = refs[2 + has_add + has_after]
    chunks = _col_chunks(o_ref.shape[-1])

    def dot(cols):
        return _dot(a_ref[...], b_ref[cols, :] if dn == NT else b_ref[:, cols], dn)

    def finish(cols, r):
        if add_ref is not None:
            r = r + add_ref[:, cols]
        o_ref[:, cols] = r.astype(o_ref.dtype)

    if nk == 1:
        for cols in chunks:
            finish(cols, dot(cols))
        return
    acc_ref = refs[-1]
    k = pl.program_id(2)

    @pl.when(k == 0)
    def _():
        for cols in chunks:
            acc_ref[:, cols] = dot(cols)

    if nk > 2:
        @pl.when(jnp.logical_and(k > 0, k < nk - 1))
        def _():
            for cols in chunks:
                acc_ref[:, cols] += dot(cols)

    @pl.when(k == nk - 1)
    def _():
        for cols in chunks:
            finish(cols, acc_ref[:, cols] + dot(cols))


def _matmul(a, b, mode, out_dtype, name, *, tm, tn, tk=None, add=None, b_slab=None, out_slab=None, after=None):
    if mode == "nn":
        M, K = a.shape
        N = b.shape[0] * b.shape[2] if b_slab == "c" else b.shape[1]
    elif mode == "nt":
        M, K = a.shape
        N = b.shape[1] if b_slab == "k" else b.shape[0]
    else:
        K, M = a.shape
        N = b.shape[1]
    tk = K if tk is None else tk
    tm, tn, tk = _pick(M, tm), _pick(N, tn), _pick(K, tk)
    if b_slab == "c":
        tn = _pick(b.shape[2], tn)
    if b_slab == "k":
        tk = _pick(b.shape[2], tk)
    if out_slab == "c":
        tn = _pick(N // N_CHIPS, tn)
    if out_slab == "r":
        tm = _pick(M // N_CHIPS, tm)
    gm, gn, gk = M // tm, N // tn, K // tk

    if mode == "tn":
        a_spec = pl.BlockSpec((tk, tm), lambda j, i, k: (k, i))
        b_spec = pl.BlockSpec((tk, tn), lambda j, i, k: (k, j))
    else:
        a_spec = pl.BlockSpec((tm, tk), lambda j, i, k: (i, k))
        if b_slab == "c":
            per = b.shape[2] // tn
            b_spec = pl.BlockSpec((None, tk, tn), lambda j, i, k: (j // per, k, j % per))
        elif b_slab == "k":
            per = b.shape[2] // tk
            b_spec = pl.BlockSpec((None, tn, tk), lambda j, i, k: (k // per, j, k % per))
        elif mode == "nn":
            b_spec = pl.BlockSpec((tk, tn), lambda j, i, k: (k, j))
        else:
            b_spec = pl.BlockSpec((tn, tk), lambda j, i, k: (j, k))

    if out_slab == "c":
        per = (N // N_CHIPS) // tn
        o_spec = pl.BlockSpec((None, tm, tn), lambda j, i, k: (j // per, i, j % per))
        o_shape = jax.ShapeDtypeStruct((N_CHIPS, M, N // N_CHIPS), out_dtype)
    elif out_slab == "r":
        per = (M // N_CHIPS) // tm
        o_spec = pl.BlockSpec((None, tm, tn), lambda j, i, k: (i // per, i % per, j))
        o_shape = jax.ShapeDtypeStruct((N_CHIPS, M // N_CHIPS, N), out_dtype)
    else:
        o_spec = pl.BlockSpec((tm, tn), lambda j, i, k: (i, j))
        o_shape = jax.ShapeDtypeStruct((M, N), out_dtype)

    dn = {"nn": NN, "nt": NT, "tn": TN}[mode]
    in_specs = [a_spec, b_spec]
    args = [a, b]
    if add is not None:
        in_specs.append(pl.BlockSpec((tm, tn), lambda j, i, k: (i, j)))
        args.append(add)
    if after is not None:
        in_specs.append(ANY)
        args.append(after)
    return pl.pallas_call(
        functools.partial(_mm_body, dn, gk, add is not None, after is not None),
        name=name,
        grid=(gn, gm, gk),
        in_specs=in_specs,
        out_specs=o_spec,
        out_shape=o_shape,
        scratch_shapes=[pltpu.VMEM((tm, tn), F32)] if gk > 1 else [],
        compiler_params=_params(("parallel", "parallel", "arbitrary")),
    )(*args)


def _rms_fwd_body(x_ref, g_ref, *rest):
    h_ref = rest[-1]
    x = x_ref[...]
    r = lax.rsqrt(jnp.mean(x * x, axis=-1, keepdims=True) + EPS)
    h_ref[...] = (x * r * g_ref[...]).astype(h_ref.dtype)


def _rms_fwd(x, g, name, after=None):
    T, D = x.shape
    tr = _pick(T, 256, 16)
    extra = [] if after is None else [after]
    return pl.pallas_call(
        functools.partial(_rms_fwd_body),
        name=name,
        grid=(T // tr,),
        in_specs=[pl.BlockSpec((tr, D), lambda i: (i, 0)), pl.BlockSpec((1, D), lambda i: (0, 0))] + [ANY] * len(extra),
        out_specs=pl.BlockSpec((tr, D), lambda i: (i, 0)),
        out_shape=jax.ShapeDtypeStruct((T, D), BF16),
        compiler_params=_params(("parallel",)),
    )(x, g, *extra)


def _rms_bwd_body(with_bf16, x_ref, g_ref, dh_ref, res_ref, dx_ref, *rest):
    dg_ref = rest[-1]

    @pl.when(pl.program_id(0) == 0)
    def _():
        dg_ref[...] = jnp.zeros_like(dg_ref)

    x = x_ref[...]
    r = lax.rsqrt(jnp.mean(x * x, axis=-1, keepdims=True) + EPS)
    xh = x * r
    dh = dh_ref[...]
    dg_ref[...] += jnp.sum(dh * xh, axis=0, keepdims=True)
    t = dh * g_ref[...]
    dx = res_ref[...] + r * (t - xh * jnp.mean(t * xh, axis=-1, keepdims=True))
    dx_ref[...] = dx
    if with_bf16:
        rest[0][...] = dx.astype(BF16)


def _rms_bwd(x, g, dh, res, name, with_bf16):
    T, D = x.shape
    tr = _pick(T, 256, 16)
    row = pl.BlockSpec((tr, D), lambda i: (i, 0))
    vec = pl.BlockSpec((1, D), lambda i: (0, 0))
    extra = [jax.ShapeDtypeStruct((T, D), BF16)] if with_bf16 else []
    return pl.pallas_call(
        functools.partial(_rms_bwd_body, with_bf16),
        name=name,
        grid=(T // tr,),
        in_specs=[row, vec, row, row],
        out_specs=[row] + [row] * len(extra) + [vec],
        out_shape=[jax.ShapeDtypeStruct((T, D), F32)] + extra + [jax.ShapeDtypeStruct((1, D), F32)],
        compiler_params=_params(("arbitrary",)),
    )(x, g, dh, res)


def _ffn_up_body(h_ref, wg_ref, wu_ref, a_ref, b_ref, f_ref):
    for cols in _col_chunks(a_ref.shape[-1]):
        a = _dot(h_ref[...], wg_ref[:, cols], NN)
        b = _dot(h_ref[...], wu_ref[:, cols], NN)
        a_ref[:, cols] = a
        b_ref[:, cols] = b
        f_ref[:, cols] = (a * (1.0 / (1.0 + jnp.exp(-a))) * b).astype(f_ref.dtype)


def _ffn_up(h, wg, wu):
    T, D = h.shape
    F = wg.shape[1]
    tm, tn = _pick(T, 1024), _pick(F, MM_CHUNK)
    hs = pl.BlockSpec((tm, D), lambda j, i: (i, 0))
    ws = pl.BlockSpec((D, tn), lambda j, i: (0, j))
    os = pl.BlockSpec((tm, tn), lambda j, i: (i, j))
    return pl.pallas_call(
        functools.partial(_ffn_up_body),
        name="ffn_up_fwd",
        grid=(F // tn, T // tm),
        in_specs=[hs, ws, ws],
        out_specs=[os, os, os],
        out_shape=[jax.ShapeDtypeStruct((T, F), F32), jax.ShapeDtypeStruct((T, F), F32),
                   jax.ShapeDtypeStruct((T, F), BF16)],
        compiler_params=_params(("parallel", "parallel")),
    )(h, wg, wu)


def _ffn_down_dx_body(dx_ref, wd_ref, a_ref, b_ref, after_ref, da_ref, db_ref):
    del after_ref
    for cols in _col_chunks(da_ref.shape[-1]):
        df = _dot(dx_ref[...], wd_ref[cols, :], NT)
        a = a_ref[:, cols]
        s = 1.0 / (1.0 + jnp.exp(-a))
        da_ref[:, cols] = (df * b_ref[:, cols] * (s * (1.0 + a * (1.0 - s)))).astype(da_ref.dtype)
        db_ref[:, cols] = (df * (a * s)).astype(db_ref.dtype)


def _ffn_down_dx(dx2b, wd, a, b, after):
    T, D = dx2b.shape
    F = wd.shape[0]
    tm, tn = _pick(T, 512), _pick(F, 1408)
    xs = pl.BlockSpec((tm, D), lambda j, i: (i, 0))
    ws = pl.BlockSpec((tn, D), lambda j, i: (j, 0))
    os = pl.BlockSpec((tm, tn), lambda j, i: (i, j))
    return pl.pallas_call(
        functools.partial(_ffn_down_dx_body),
        name="ffn_down_dx",
        grid=(F // tn, T // tm),
        in_specs=[xs, ws, os, os, ANY],
        out_specs=[os, os],
        out_shape=[jax.ShapeDtypeStruct((T, F), BF16), jax.ShapeDtypeStruct((T, F), BF16)],
        compiler_params=_params(("parallel", "parallel")),
    )(dx2b, wd, a, b, after)


def _ffn_down_loss_body(nk, inv_d, f_ref, wd_ref, x1_ref, tgt_ref, dx2_ref, dx2b_ref, loss_ref, *scratch):
    j, i, k = pl.program_id(0), pl.program_id(1), pl.program_id(2)
    chunks = _col_chunks(dx2_ref.shape[-1])

    def dot(cols):
        return _dot(f_ref[...], wd_ref[:, cols], NN)

    @pl.when(jnp.logical_and(jnp.logical_and(j == 0, i == 0), k == 0))
    def _():
        loss_ref[...] = jnp.zeros_like(loss_ref)

    def finish(ffn_of):
        total = jnp.zeros((1, 1), F32)
        for cols in chunks:
            e = ffn_of(cols) + x1_ref[:, cols] - tgt_ref[:, cols]
            dx2 = e * inv_d
            dx2_ref[:, cols] = dx2
            dx2b_ref[:, cols] = dx2.astype(BF16)
            total = total + jnp.sum(jnp.sum(e * e, axis=-1, keepdims=True), axis=0, keepdims=True)
        loss_ref[...] += (0.5 * inv_d) * total

    if nk == 1:
        finish(dot)
        return
    acc_ref = scratch[0]

    @pl.when(k == 0)
    def _():
        for cols in chunks:
            acc_ref[:, cols] = dot(cols)

    if nk > 2:
        @pl.when(jnp.logical_and(k > 0, k < nk - 1))
        def _():
            for cols in chunks:
                acc_ref[:, cols] += dot(cols)

    @pl.when(k == nk - 1)
    def _():
        finish(lambda cols: acc_ref[:, cols] + dot(cols))


def _ffn_down_loss(f, wd, x1, tgt):
    T, F = f.shape
    D = wd.shape[1]
    tm, tn, tk = _pick(T, 1024), _pick(D, 1024), _pick(F, 1408)
    gm, gn, gk = T // tm, D // tn, F // tk
    tile = pl.BlockSpec((tm, tn), lambda j, i, k: (i, j))
    return pl.pallas_call(
        functools.partial(_ffn_down_loss_body, gk, 1.0 / D),
        name="ffn_down_loss",
        grid=(gn, gm, gk),
        in_specs=[pl.BlockSpec((tm, tk), lambda j, i, k: (i, k)), pl.BlockSpec((tk, tn), lambda j, i, k: (k, j)),
                  tile, tile],
        out_specs=[tile, tile, pl.BlockSpec((1, 1), lambda j, i, k: (0, 0))],
        out_shape=[jax.ShapeDtypeStruct((T, D), F32), jax.ShapeDtypeStruct((T, D), BF16),
                   jax.ShapeDtypeStruct((1, 1), F32)],
        scratch_shapes=[pltpu.VMEM((tm, tn), F32)] if gk > 1 else [],
        compiler_params=_params(("arbitrary", "arbitrary", "arbitrary")),
    )(f, wd, x1, tgt)


def _lo_mask(shape):
    return lax.broadcasted_iota(jnp.int32, shape, len(shape) - 1) < HEAD_DIM


def _half_sums(t, lo):
    s_lo = jnp.sum(jnp.where(lo, t, 0.0), axis=-1, keepdims=True)
    s_hi = jnp.sum(jnp.where(lo, 0.0, t), axis=-1, keepdims=True)
    return jnp.where(lo, s_lo, s_hi)


def _head_rstd(t, lo):
    return lax.rsqrt(_half_sums(t * t, lo) * (1.0 / HEAD_DIM) + EPS)


def _place(t, lo, kv_head):
    if kv_head == 0:
        t_lo = jnp.where(lo, t, 0.0)
        t_hi = pltpu.roll(t_lo, HEAD_DIM, 1)
    else:
        t_hi = jnp.where(lo, 0.0, t)
        t_lo = pltpu.roll(t_hi, HEAD_DIM, 1)
    return jnp.concatenate([t_lo, t_hi], axis=0).astype(BF16)


def _unplace(c0, c1, lo):
    return jnp.where(lo, c0 + pltpu.roll(c0, HEAD_DIM, 1), c1 + pltpu.roll(c1, HEAD_DIM, 1))


def _band(kv_cur, kv_prev, kg, lo2):
    kb = jnp.concatenate([kv_prev[:, :BLK], kv_cur[:, :BLK]], axis=0)
    vb = jnp.concatenate([kv_prev[:, BLK:], kv_cur[:, BLK:]], axis=0)
    rk = _head_rstd(kb, lo2)
    kn = kb * rk * kg
    kk = [_place(kn, lo2, h) for h in range(N_KV_HEADS)]
    vv = [_place(vb, lo2, h) for h in range(N_KV_HEADS)]
    return kb, rk, kk, vv


def _score_geometry(first_i32):
    qi = lax.broadcasted_iota(jnp.int32, (BLK, 4 * BLK), 0)
    col = lax.broadcasted_iota(jnp.int32, (BLK, 4 * BLK), 1)
    kj = col & (2 * BLK - 1)
    dist = qi + BLK - kj
    valid = (dist >= 0) & (dist < BLK) & (kj >= first_i32 * BLK)
    return col, dist.astype(F32), valid


def _pair_probs(qn, kk, col, distf, valid, slope0, slope1, sink0, sink1):
    s = _dot(qn.astype(BF16), kk, NT) * (HEAD_DIM ** -0.5)
    slope = jnp.where(col < 2 * BLK, slope0, slope1)
    logits = jnp.where(valid, s - slope * distf, NEG)
    probs, psink = [], []
    for hh, sk in ((0, sink0), (1, sink1)):
        l = logits[:, 2 * BLK * hh:2 * BLK * (hh + 1)]
        m = jnp.maximum(jnp.max(l, axis=-1, keepdims=True), sk)
        p = jnp.exp(l - m)
        es = jnp.exp(sk - m)
        inv = 1.0 / (jnp.sum(p, axis=-1, keepdims=True) + es)
        probs.append(p * inv)
        psink.append(es * inv)
    return probs, psink


def _gelu(z, with_grad=False):
    cdf = 0.5 * (1.0 + lax.erf(z * (0.5 ** 0.5)))
    if not with_grad:
        return z * cdf
    return z * cdf, cdf + z * jnp.exp(-0.5 * z * z) * ((2.0 * jnp.pi) ** -0.5)


def _tril_w(w):
    r = lax.broadcasted_iota(jnp.int32, (BLK, BLK), 0)
    c = lax.broadcasted_iota(jnp.int32, (BLK, BLK), 1)
    return jnp.where(r >= c, w, 0.0), r >= c


def _gate_fwd_group(zu, zv, lg, lb, w, bcol, with_grad=False):
    u, v = _gelu(zu, with_grad), _gelu(zv, with_grad)
    if with_grad:
        (u, du_dz), (v, dv_dz) = u, v
    mu = jnp.mean(v, axis=-1, keepdims=True)
    vc = v - mu
    rs = lax.rsqrt(jnp.mean(vc * vc, axis=-1, keepdims=True) + EPS)
    vh = vc * rs
    vn = vh * lg + lb
    wt, tril = _tril_w(w)
    mixed = _dot(wt.astype(BF16), vn.astype(BF16), NN) + bcol
    if with_grad:
        return u, vh, rs, vn, wt, tril, mixed, du_dz, dv_dz
    return u, vh, rs, vn, wt, tril, mixed


class _Dims:
    def __init__(self, seq, attn_w, gate_w):
        self.seq, self.attn_w, self.gate_w = seq, attn_w, gate_w
        self.n_heads = attn_w // HEAD_DIM
        self.group = self.n_heads // N_KV_HEADS
        self.n_pairs = attn_w // BLK
        self.n_groups = gate_w // BLK
        self.kv_col = attn_w // (2 * BLK)
        self.u0 = attn_w + 2 * BLK
        self.v0 = self.u0 + gate_w
        self.in_w = self.v0 + gate_w
        self.slopes = [2.0 ** (-8.0 * (h + 1) / self.n_heads) for h in range(self.n_heads)]


def _mixer_fwd_body(d, sink_ref, proj_ref, kvp_ref, qg_ref, kg_ref, lg_ref, lb_ref, w_ref, b_ref, goa_ref, gog_ref,
                    ya_ref, yg_ref, y_ref):
    i = pl.program_id(0)
    first = (i % (d.seq // BLK) == 0).astype(jnp.int32)
    lo = _lo_mask((BLK, BLK))
    lo2 = _lo_mask((2 * BLK, BLK))
    kv_cur = proj_ref[:, d.attn_w:d.attn_w + 2 * BLK]
    _, _, kk, vv = _band(kv_cur, kvp_ref[...], kg_ref[...], lo2)
    col, distf, valid = _score_geometry(first)
    qg = qg_ref[...]
    for j in range(d.n_pairs):
        h0, h1 = 2 * j, 2 * j + 1
        kh = h0 // d.group
        q2 = proj_ref[:, BLK * j:BLK * (j + 1)]
        qn = q2 * _head_rstd(q2, lo) * qg
        probs, _ = _pair_probs(qn, kk[kh], col, distf, valid, d.slopes[h0], d.slopes[h1],
                               sink_ref[0, h0], sink_ref[0, h1])
        p = jnp.concatenate(probs, axis=1).astype(BF16)
        ya_ref[:, BLK * j:BLK * (j + 1)] = _dot(p, vv[kh], NN)
    for g in range(d.n_groups):
        zu = proj_ref[:, d.u0 + BLK * g:d.u0 + BLK * (g + 1)]
        zv = proj_ref[:, d.v0 + BLK * g:d.v0 + BLK * (g + 1)]
        u, _, _, _, _, _, mixed = _gate_fwd_group(zu, zv, lg_ref[g:g + 1, :], lb_ref[g:g + 1, :], w_ref[g], b_ref[g])
        yg_ref[:, BLK * g:BLK * (g + 1)] = u * mixed
    ya = ya_ref[...]
    ra = lax.rsqrt(jnp.mean(ya * ya, axis=-1, keepdims=True) + EPS)
    y_ref[:, :d.attn_w] = (ya * ra * goa_ref[...]).astype(y_ref.dtype)
    yg = yg_ref[...]
    rg = lax.rsqrt(jnp.mean(yg * yg, axis=-1, keepdims=True) + EPS)
    y_ref[:, d.attn_w:] = (yg * rg * gog_ref[...]).astype(y_ref.dtype)


def _mixer_specs(d, T):
    row = lambda w: pl.BlockSpec((BLK, w), lambda i: (i, 0))
    const2 = lambda a: pl.BlockSpec(a.shape, lambda i: (0, 0))
    const3 = lambda a: pl.BlockSpec(a.shape, lambda i: (0, 0, 0))
    kv_prev = pl.BlockSpec((BLK, 2 * BLK), lambda i: (jnp.maximum(i - 1, 0), d.kv_col))
    return row, const2, const3, kv_prev


def _mixer_fwd(d, proj, sinks, qg2, kg2, lg, lb, wsp, bcol, goa, gog):
    T = proj.shape[0]
    row, const2, const3, kv_prev = _mixer_specs(d, T)
    return pl.pallas_call(
        functools.partial(_mixer_fwd_body, d),
        name="mixer_fwd",
        grid=(T // BLK,),
        in_specs=[pl.BlockSpec(memory_space=pltpu.SMEM), row(d.in_w), kv_prev, const2(qg2), const2(kg2),
                  const2(lg), const2(lb), const3(wsp), const3(bcol), const2(goa), const2(gog)],
        out_specs=[row(d.attn_w), row(d.gate_w), row(d.attn_w + d.gate_w)],
        out_shape=[jax.ShapeDtypeStruct((T, d.attn_w), F32), jax.ShapeDtypeStruct((T, d.gate_w), F32),
                   jax.ShapeDtypeStruct((T, d.attn_w + d.gate_w), BF16)],
        compiler_params=_params(("parallel",)),
    )(sinks, proj, proj, qg2, kg2, lg, lb, wsp, bcol, goa, gog)


def _mixer_bwd_body(d, sink_ref, proj_ref, kvp_ref, ya_ref, yg_ref, dy_ref, qg_ref, kg_ref, lg_ref, lb_ref, w_ref,
                    b_ref, goa_ref, gog_ref,
                    dproj_ref, dkv_ref, dqg_ref, dkg_ref, dsk_ref, dlg_ref, dlb_ref, dw_ref, db_ref, dgoa_ref,
                    dgog_ref):
    i = pl.program_id(0)

    @pl.when(i == 0)
    def _():
        for r in (dqg_ref, dkg_ref, dsk_ref, dlg_ref, dlb_ref, dw_ref, db_ref, dgoa_ref, dgog_ref):
            r[...] = jnp.zeros_like(r)

    first = (i % (d.seq // BLK) == 0).astype(jnp.int32)
    lo = _lo_mask((BLK, BLK))
    lo2 = _lo_mask((2 * BLK, BLK))
    lane_row = lax.broadcasted_iota(jnp.int32, (1, BLK), 1)

    ya = ya_ref[...]
    ra = lax.rsqrt(jnp.mean(ya * ya, axis=-1, keepdims=True) + EPS)
    yah = ya * ra
    dyn = dy_ref[:, :d.attn_w]
    dgoa_ref[...] += jnp.sum(dyn * yah, axis=0, keepdims=True)
    t = dyn * goa_ref[...]
    dya = ra * (t - yah * jnp.mean(t * yah, axis=-1, keepdims=True))
    yg = yg_ref[...]
    rg = lax.rsqrt(jnp.mean(yg * yg, axis=-1, keepdims=True) + EPS)
    ygh = yg * rg
    dyn = dy_ref[:, d.attn_w:]
    dgog_ref[...] += jnp.sum(dyn * ygh, axis=0, keepdims=True)
    t = dyn * gog_ref[...]
    dyg = rg * (t - ygh * jnp.mean(t * ygh, axis=-1, keepdims=True))

    kv_cur = proj_ref[:, d.attn_w:d.attn_w + 2 * BLK]
    kg = kg_ref[...]
    kb, rk, kk, vv = _band(kv_cur, kvp_ref[...], kg, lo2)
    col, distf, valid = _score_geometry(first)
    qg = qg_ref[...]
    ck = [jnp.zeros((2 * BLK, BLK), F32) for _ in range(N_KV_HEADS)]
    cv = [jnp.zeros((2 * BLK, BLK), F32) for _ in range(N_KV_HEADS)]
    dsk = jnp.zeros((1, BLK), F32)
    dqg = jnp.zeros((1, BLK), F32)
    for j in range(d.n_pairs):
        h0, h1 = 2 * j, 2 * j + 1
        kh = h0 // d.group
        cols = slice(BLK * j, BLK * (j + 1))
        q2 = proj_ref[:, cols]
        rq = _head_rstd(q2, lo)
        qh = q2 * rq
        qn = qh * qg
        probs, psink = _pair_probs(qn, kk[kh], col, distf, valid, d.slopes[h0], d.slopes[h1],
                                   sink_ref[0, h0], sink_ref[0, h1])
        do2 = dya[:, cols]
        prod = do2 * ya[:, cols]
        delta = (jnp.sum(jnp.where(lo, prod, 0.0), axis=-1, keepdims=True),
                 jnp.sum(jnp.where(lo, 0.0, prod), axis=-1, keepdims=True))
        do2b = do2.astype(BF16)
        dp = _dot(do2b, vv[kh], NT)
        ds = []
        for hh in (0, 1):
            ds.append(probs[hh] * (dp[:, 2 * BLK * hh:2 * BLK * (hh + 1)] - delta[hh]))
            dsink = -jnp.sum(psink[hh] * delta[hh], axis=0, keepdims=True)
            dsk = dsk + jnp.where(lane_row == (h0 + hh), dsink, 0.0)
        dsb = (jnp.concatenate(ds, axis=1) * (HEAD_DIM ** -0.5)).astype(BF16)
        pb = jnp.concatenate(probs, axis=1).astype(BF16)
        qnb = qn.astype(BF16)
        dqn = _dot(dsb, kk[kh], NN)
        dkk = _dot(dsb, qnb, TN)
        dvv = _dot(pb, do2b, TN)
        ck[kh] = ck[kh] + jnp.where(lo2, dkk[:2 * BLK], 0.0) + jnp.where(lo2, 0.0, dkk[2 * BLK:])
        cv[kh] = cv[kh] + jnp.where(lo2, dvv[:2 * BLK], 0.0) + jnp.where(lo2, 0.0, dvv[2 * BLK:])
        dqg = dqg + jnp.sum(dqn * qh, axis=0, keepdims=True)
        t = dqn * qg
        dq2 = rq * (t - qh * (_half_sums(t * qh, lo) * (1.0 / HEAD_DIM)))
        dproj_ref[:, cols] = dq2.astype(dproj_ref.dtype)
    dsk_ref[...] += dsk
    dqg_ref[...] += dqg
    dkn = _unplace(ck[0], ck[1], lo2)
    dvb = _unplace(cv[0], cv[1], lo2)
    khat = kb * rk
    dkg_ref[...] += jnp.sum(dkn * khat, axis=0, keepdims=True)
    t = dkn * kg
    dkb = rk * (t - khat * (_half_sums(t * khat, lo2) * (1.0 / HEAD_DIM)))
    rows_cur = pl.ds(pl.multiple_of(i * BLK, BLK), BLK)
    rows_prev = pl.ds(pl.multiple_of(jnp.maximum(i - 1, 0) * BLK, BLK), BLK)
    dkv_ref[rows_cur, :] = jnp.concatenate([dkb[BLK:], dvb[BLK:]], axis=1)
    dkv_ref[rows_prev, :] += jnp.concatenate([dkb[:BLK], dvb[:BLK]], axis=1)
    dproj_ref[:, d.attn_w:d.attn_w + 2 * BLK] = jnp.zeros((BLK, 2 * BLK), dproj_ref.dtype)

    for g in range(d.n_groups):
        ucols = slice(d.u0 + BLK * g, d.u0 + BLK * (g + 1))
        vcols = slice(d.v0 + BLK * g, d.v0 + BLK * (g + 1))
        zu = proj_ref[:, ucols]
        zv = proj_ref[:, vcols]
        lg = lg_ref[g:g + 1, :]
        u, vh, rs, vn, wt, tril, mixed, du_dz, dv_dz = _gate_fwd_group(
            zu, zv, lg, lb_ref[g:g + 1, :], w_ref[g], b_ref[g], with_grad=True)
        dyg_g = dyg[:, BLK * g:BLK * (g + 1)]
        du = dyg_g * mixed
        dmix = dyg_g * u
        dmb = dmix.astype(BF16)
        db_ref[g:g + 1, :] += jnp.sum(jnp.transpose(dmix), axis=0, keepdims=True)
        dw_ref[g] += jnp.where(tril, _dot(dmb, vn.astype(BF16), NT), 0.0)
        dvn = _dot(wt.astype(BF16), dmb, TN)
        dlg_ref[g:g + 1, :] += jnp.sum(dvn * vh, axis=0, keepdims=True)
        dlb_ref[g:g + 1, :] += jnp.sum(dvn, axis=0, keepdims=True)
        dvh = dvn * lg
        dv = rs * (dvh - jnp.mean(dvh, axis=-1, keepdims=True) - vh * jnp.mean(dvh * vh, axis=-1, keepdims=True))
        dproj_ref[:, ucols] = (du * du_dz).astype(dproj_ref.dtype)
        dproj_ref[:, vcols] = (dv * dv_dz).astype(dproj_ref.dtype)


def _mixer_bwd(d, proj, ya, yg, dy, sinks, qg2, kg2, lg, lb, wsp, bcol, goa, gog):
    T = proj.shape[0]
    row, const2, const3, kv_prev = _mixer_specs(d, T)
    acc2 = lambda s: pl.BlockSpec(s, lambda i: (0, 0))
    G = d.n_groups
    out_shapes = [((T, d.in_w), BF16), ((T, 2 * BLK), F32), ((1, BLK), F32), ((1, BLK), F32), ((1, BLK), F32),
                  ((G, BLK), F32), ((G, BLK), F32), ((G, BLK, BLK), F32), ((G, BLK), F32),
                  ((1, d.attn_w), F32), ((1, d.gate_w), F32)]
    out_specs = [row(d.in_w)] + [acc2(s) for s, _ in out_shapes[1:7]] + \
                [pl.BlockSpec((G, BLK, BLK), lambda i: (0, 0, 0))] + [acc2(s) for s, _ in out_shapes[8:]]
    return pl.pallas_call(
        functools.partial(_mixer_bwd_body, d),
        name="mixer_bwd",
        grid=(T // BLK,),
        in_specs=[pl.BlockSpec(memory_space=pltpu.SMEM), row(d.in_w), kv_prev, row(d.attn_w), row(d.gate_w),
                  row(d.attn_w + d.gate_w), const2(qg2), const2(kg2), const2(lg), const2(lb), const3(wsp),
                  const3(bcol), const2(goa), const2(gog)],
        out_specs=out_specs,
        out_shape=[jax.ShapeDtypeStruct(s, t) for s, t in out_shapes],
        compiler_params=_params(("arbitrary",)),
    )(sinks, proj, proj, ya, yg, dy, qg2, kg2, lg, lb, wsp, bcol, goa, gog)


def _put_kv_body(dkv_ref, dproj_in_ref, dproj_ref):
    del dproj_in_ref
    dproj_ref[...] = dkv_ref[...].astype(dproj_ref.dtype)


def _put_kv(d, dproj, dkv):
    T = dproj.shape[0]
    tr = _pick(T, 1024, 16)
    return pl.pallas_call(
        functools.partial(_put_kv_body),
        name="put_kv",
        grid=(T // tr,),
        in_specs=[pl.BlockSpec((tr, 2 * BLK), lambda i: (i, 0)), pl.BlockSpec(memory_space=pl.ANY)],
        out_specs=pl.BlockSpec((tr, 2 * BLK), lambda i: (i, d.kv_col)),
        out_shape=jax.ShapeDtypeStruct(dproj.shape, dproj.dtype),
        input_output_aliases={1: 0},
        compiler_params=_params(("parallel",)),
    )(dkv, dproj)


def _add_pair_body(pc_ref, own_ref, got_ref, o_ref):
    del pc_ref
    o_ref[...] = (own_ref[...].astype(F32) + got_ref[...].astype(F32)).astype(o_ref.dtype)


def _add_pair(g4, got, pc, name):
    n, _, h, C = g4.shape
    tr = _pick(h, 512, 16)
    return pl.pallas_call(
        functools.partial(_add_pair_body),
        name=name,
        grid_spec=pltpu.PrefetchScalarGridSpec(
            num_scalar_prefetch=1,
            grid=(n, h // tr),
            in_specs=[pl.BlockSpec((None, None, tr, C), lambda q, i, pc: (q, pc[1], i, 0)),
                      pl.BlockSpec((None, tr, C), lambda q, i, pc: (q, i, 0))],
            out_specs=pl.BlockSpec((None, tr, C), lambda q, i, pc: (q, i, 0)),
        ),
        out_shape=jax.ShapeDtypeStruct((n, h, C), g4.dtype),
        compiler_params=_params(("parallel", "parallel")),
    )(pc, g4, got)


def _adamw_update(w, g, m, v):
    m = ADAM_B1 * m + (1.0 - ADAM_B1) * g
    v = ADAM_B2 * v + (1.0 - ADAM_B2) * (g * g)
    m_hat = m / (1.0 - ADAM_B1 ** ADAM_STEP)
    v_hat = v / (1.0 - ADAM_B2 ** ADAM_STEP)
    return -ADAM_LR * (m_hat / (jnp.sqrt(v_hat) + ADAM_EPS) + ADAM_WD * w), m, v


def _adamw_body(w_ref, g_ref, m_ref, v_ref, d_ref, nm_ref, nv_ref):
    d_ref[...], nm_ref[...], nv_ref[...] = _adamw_update(w_ref[...], g_ref[...], m_ref[...], v_ref[...])


def _adamw(w, g, m, v, name):
    R, C = w.shape
    tr = _pick(R, 512, 8)
    blk = pl.BlockSpec((tr, C), lambda i: (i, 0))
    return pl.pallas_call(
        functools.partial(_adamw_body),
        name=name,
        grid=(R // tr,),
        in_specs=[blk] * 4,
        out_specs=[blk] * 3,
        out_shape=[jax.ShapeDtypeStruct((R, C), F32)] * 3,
        compiler_params=_params(("parallel",)),
    )(w, g, m, v)


def _adamw_halves_body(pc_ref, w_ref, own_ref, got_ref, m_ref, v_ref, g_ref, d_ref, nm_ref, nv_ref):
    mine = pl.program_id(0) == pc_ref[1]

    def update(g):
        g_ref[...] = g
        d_ref[...], nm_ref[...], nv_ref[...] = _adamw_update(w_ref[...], g, m_ref[...], v_ref[...])

    @pl.when(mine)
    def _():
        update(own_ref[...].astype(F32))

    @pl.when(jnp.logical_not(mine))
    def _():
        update(got_ref[...].astype(F32))


def _adamw_halves(w, own, got, m, v, pc, name):
    h, C = own.shape
    tr = _pick(h, 512, 8)
    full = pl.BlockSpec((None, tr, C), lambda hh, i, pc: (hh, i, 0))
    mine = pl.BlockSpec((tr, C), lambda hh, i, pc: (jnp.where(hh == pc[1], i, 0), 0))
    theirs = pl.BlockSpec((tr, C), lambda hh, i, pc: (jnp.where(hh == pc[1], 0, i), 0))
    return pl.pallas_call(
        functools.partial(_adamw_halves_body),
        name=name,
        grid_spec=pltpu.PrefetchScalarGridSpec(
            num_scalar_prefetch=1,
            grid=(2, h // tr),
            in_specs=[full, mine, theirs, full, full],
            out_specs=[full] * 4,
        ),
        out_shape=[jax.ShapeDtypeStruct((2, h, C), F32)] * 4,
        compiler_params=_params(("parallel", "parallel")),
    )(pc, w.reshape(2, h, C), own, got, m.reshape(2, h, C), v.reshape(2, h, C))


def _me():
    x, y, c = lax.axis_index("x"), lax.axis_index("y"), lax.axis_index("c")
    chips = [(1 - x, y), (x, 1 - y), (1 - x, 1 - y)]
    return x, y, c, chips


def _cast_into_body(pc_ref, w_ref, o_ref):
    del pc_ref
    o_ref[...] = w_ref[...].astype(o_ref.dtype)


def _cast_into(w, pc, name, side_by_side=False):
    Rs, C = w.shape
    h = Rs // 2
    tr = _pick(h, 512, 16)
    if side_by_side:
        out_spec = pl.BlockSpec((None, tr, C), lambda hh, i, pc: (hh, i, pc[0]))
        out_shape = jax.ShapeDtypeStruct((2, h, N_CHIPS * C), BF16)
    else:
        out_spec = pl.BlockSpec((None, None, tr, C), lambda hh, i, pc: (pc[0], hh, i, 0))
        out_shape = jax.ShapeDtypeStruct((N_CHIPS, 2, h, C), BF16)
    return pl.pallas_call(
        functools.partial(_cast_into_body),
        name=name,
        grid_spec=pltpu.PrefetchScalarGridSpec(
            num_scalar_prefetch=1,
            grid=(2, h // tr),
            in_specs=[pl.BlockSpec((None, tr, C), lambda hh, i, pc: (hh, i, 0))],
            out_specs=out_spec,
        ),
        out_shape=out_shape,
        compiler_params=_params(("parallel", "parallel")),
    )(pc, w.reshape(2, h, C))


MAX_PIECES = 4


def _send_tile_to_sibling(src_of, dst_of, tr, dst_total, send_sems, recv_sem, last):
    x, y, c, _ = _me()
    pieces = MAX_PIECES if tr % (16 * MAX_PIECES) == 0 else (2 if tr % 32 == 0 else 1)
    n = tr // pieces
    copies = [pltpu.make_async_remote_copy(src_ref=src_of(k * n, n), dst_ref=dst_of(k * n, n), send_sem=send_sems.at[k],
                                           recv_sem=recv_sem, device_id=(x, y, 1 - c), device_id_type=MESH)
              for k in range(pieces)]
    for cp in copies:
        cp.start()
    for cp in copies:
        cp.wait_send()

    @pl.when(last)
    def _():
        pltpu.make_async_remote_copy(src_ref=dst_total, dst_ref=dst_total, send_sem=send_sems.at[0], recv_sem=recv_sem,
                                     device_id=(x, y, 1 - c), device_id_type=MESH).wait_recv()


TILE_SEMS = [pltpu.SemaphoreType.DMA((MAX_PIECES,)), pltpu.SemaphoreType.DMA(())]


def _ag_pair_body(tr, n_i, pc_ref, tile_ref, buf_ref, send_sem, recv_sem):
    j, i = pl.program_id(0), pl.program_id(1)
    q = pc_ref[0] ^ (j + 1)
    c = pc_ref[1]
    r_tile = pl.multiple_of(i * tr, tr)
    last = jnp.logical_and(j == N_CHIPS - 2, i == n_i - 1)
    if len(buf_ref.shape) == 4:
        _send_tile_to_sibling(lambda r0, n: tile_ref.at[:, :, pl.ds(r0, n)],
                              lambda r0, n: buf_ref.at[pl.ds(q, 1), pl.ds(c, 1), pl.ds(r_tile + r0, n)], tr,
                              buf_ref.at[pl.ds(0, N_CHIPS - 1), 0], send_sem, recv_sem, last)
    else:
        cs = buf_ref.shape[2] // N_CHIPS
        cols = pl.ds(pl.multiple_of(q * cs, BLK), cs)
        _send_tile_to_sibling(lambda r0, n: tile_ref.at[:, pl.ds(r0, n)],
                              lambda r0, n: buf_ref.at[pl.ds(c, 1), pl.ds(r_tile + r0, n), cols], tr,
                              buf_ref.at[0, :, pl.ds(0, (N_CHIPS - 1) * cs)], send_sem, recv_sem, last)


def _ag_pair(buf, pc, name):
    if len(buf.shape) == 4:
        _, _, h, C = buf.shape
        tile = lambda tr: pl.BlockSpec((1, 1, tr, C), lambda j, i, pc: (pc[0] ^ (j + 1), pc[1], i, 0))
    else:
        _, h, C = buf.shape
        tile = lambda tr: pl.BlockSpec((1, tr, C // N_CHIPS), lambda j, i, pc: (pc[1], i, pc[0] ^ (j + 1)))
    tr = _pick(h, 512, 16)
    return pl.pallas_call(
        functools.partial(_ag_pair_body, tr, h // tr),
        name=name,
        grid_spec=pltpu.PrefetchScalarGridSpec(
            num_scalar_prefetch=1,
            grid=(N_CHIPS - 1, h // tr),
            in_specs=[tile(tr)],
            out_specs=HBM,
            scratch_shapes=TILE_SEMS,
        ),
        out_shape=jax.ShapeDtypeStruct(buf.shape, buf.dtype),
        input_output_aliases={1: 0},
        compiler_params=_params(("arbitrary", "arbitrary")),
    )(pc, buf)


def _swap_halves_body(tr, n_q, n_i, pc_ref, tile_ref, got_ref, send_sem, recv_sem):
    del pc_ref
    q, i = pl.program_id(0), pl.program_id(1)
    r_tile = pl.multiple_of(i * tr, tr)
    _send_tile_to_sibling(lambda r0, n: tile_ref.at[:, :, pl.ds(r0, n)],
                          lambda r0, n: got_ref.at[pl.ds(q, 1), :, pl.ds(r_tile + r0, n)], tr, got_ref, send_sem, recv_sem,
                          jnp.logical_and(q == n_q - 1, i == n_i - 1))


def _swap_halves(g4, pc, name):
    n, _, h, C = g4.shape
    tr = _pick(h, 512, 16)
    return pl.pallas_call(
        functools.partial(_swap_halves_body, tr, n, h // tr),
        name=name,
        grid_spec=pltpu.PrefetchScalarGridSpec(
            num_scalar_prefetch=1,
            grid=(n, h // tr),
            in_specs=[pl.BlockSpec((1, 1, tr, C), lambda q, i, pc: (q, 1 - pc[1], i, 0))],
            out_specs=HBM,
            scratch_shapes=TILE_SEMS,
        ),
        out_shape=jax.ShapeDtypeStruct((n, 1, h, C), g4.dtype),
        compiler_params=_params(("arbitrary", "arbitrary")),
    )(pc, g4).reshape(n, h, C)


def _ici_copy(src, dst, send_sems, recv_sems, j, chip, c):
    return pltpu.make_async_remote_copy(src_ref=src, dst_ref=dst, send_sem=send_sems.at[j], recv_sem=recv_sems.at[j],
                                        device_id=(chip[0], chip[1], c), device_id_type=MESH)


def _token_spec():
    return jax.ShapeDtypeStruct((8, BLK), F32), pl.BlockSpec(memory_space=pltpu.VMEM)


def _slab(buf_ref, q, c):
    if len(buf_ref.shape) == 4:
        return buf_ref.at[q, c]
    cs = buf_ref.shape[2] // N_CHIPS
    return buf_ref.at[c, :, pl.ds(pl.multiple_of(q * cs, BLK), cs)]


def _ag_start_body(buf_ref, after_ref, send_sems, recv_sems, buf_thru, token_ref):
    del after_ref, buf_thru
    x, y, c, chips = _me()
    mine = _slab(buf_ref, 2 * x + y, c)
    for j, chip in enumerate(chips):
        _ici_copy(mine, mine, send_sems, recv_sems, j, chip, c).start()
    token_ref[...] = jnp.zeros_like(token_ref)


def _ag_start(buf, after, name):
    tok_shape, tok_spec = _token_spec()
    sems = pltpu.SemaphoreType.DMA((N_CHIPS - 1,))
    return pl.pallas_call(
        functools.partial(_ag_start_body),
        name=name,
        in_specs=[HBM, ANY],
        out_specs=[SEM, SEM, HBM, tok_spec],
        out_shape=[sems, sems, pltpu.HBM(buf.shape, buf.dtype), tok_shape],
        input_output_aliases={0: 2},
        compiler_params=pltpu.CompilerParams(has_side_effects=EFFECT),
    )(pltpu.with_memory_space_constraint(buf, pltpu.HBM), after)


def _ag_wait_body(buf_ref, send_sems, recv_sems, after_ref, buf_out):
    del after_ref, buf_out
    x, y, c, chips = _me()
    mine = _slab(buf_ref, 2 * x + y, c)
    for j, chip in enumerate(chips):
        theirs = _slab(buf_ref, 2 * chip[0] + chip[1], c)
        _ici_copy(mine, mine, send_sems, recv_sems, j, chip, c).wait_send()
        _ici_copy(theirs, theirs, send_sems, recv_sems, j, chip, c).wait_recv()


def _ag_wait(buf, send_sems, recv_sems, after, name):
    return pl.pallas_call(
        functools.partial(_ag_wait_body),
        name=name,
        in_specs=[HBM, SEM, SEM, ANY],
        out_specs=HBM,
        out_shape=pltpu.HBM(buf.shape, buf.dtype),
        input_output_aliases={0: 0},
        compiler_params=pltpu.CompilerParams(has_side_effects=EFFECT),
    )(buf, send_sems, recv_sems, after)


def _rs_start_body(pair_ref, land_ref, after_ref, send_sems, recv_sems, pair_thru, land_thru, token_ref):
    del after_ref, pair_thru, land_thru
    x, y, c, chips = _me()
    for j, chip in enumerate(chips):
        _ici_copy(pair_ref.at[2 * chip[0] + chip[1]], land_ref.at[j], send_sems, recv_sems, j, chip, c).start()
    token_ref[...] = jnp.zeros_like(token_ref)


def _rs_start(pair, after, name):
    n, h, C = pair.shape
    tok_shape, tok_spec = _token_spec()
    sems = pltpu.SemaphoreType.DMA((N_CHIPS - 1,))
    land = pltpu.with_memory_space_constraint(lax.empty((N_CHIPS - 1, h, C), pair.dtype), pltpu.HBM)
    return pl.pallas_call(
        functools.partial(_rs_start_body),
        name=name,
        in_specs=[HBM, HBM, ANY],
        out_specs=[SEM, SEM, HBM, HBM, tok_spec],
        out_shape=[sems, sems, pltpu.HBM(pair.shape, pair.dtype), pltpu.HBM(land.shape, land.dtype), tok_shape],
        input_output_aliases={0: 2, 1: 3},
        compiler_params=pltpu.CompilerParams(has_side_effects=EFFECT),
    )(pltpu.with_memory_space_constraint(pair, pltpu.HBM), land, after)


def _rs_wait_body(pair_ref, land_ref, send_sems, recv_sems, after_ref, pair_out, land_out):
    del after_ref, pair_out, land_out
    x, y, c, chips = _me()
    for j, chip in enumerate(chips):
        _ici_copy(pair_ref.at[0], land_ref.at[j], send_sems, recv_sems, j, chip, c).wait_send()
        _ici_copy(pair_ref.at[0], land_ref.at[j], send_sems, recv_sems, j, chip, c).wait_recv()


def _rs_wait(pair, land, send_sems, recv_sems, after, name):
    return pl.pallas_call(
        functools.partial(_rs_wait_body),
        name=name,
        in_specs=[HBM, HBM, SEM, SEM, ANY],
        out_specs=[HBM, HBM],
        out_shape=[pltpu.HBM(pair.shape, pair.dtype), pltpu.HBM(land.shape, land.dtype)],
        input_output_aliases={0: 0, 1: 1},
        compiler_params=pltpu.CompilerParams(has_side_effects=EFFECT),
    )(pair, land, send_sems, recv_sems, after)


def _swap_copy(g4_ref, got_ref, send_sem, recv_sem):
    x, y, c, _ = _me()
    return pltpu.make_async_remote_copy(src_ref=g4_ref.at[:, 1 - c], dst_ref=got_ref, send_sem=send_sem,
                                        recv_sem=recv_sem, device_id=(x, y, 1 - c), device_id_type=MESH)


def _swap_start_body(g4_ref, got_ref, send_sem, recv_sem, g4_thru, got_thru, token_ref):
    del g4_thru, got_thru
    _swap_copy(g4_ref, got_ref, send_sem, recv_sem).start()
    token_ref[...] = jnp.zeros_like(token_ref)


def _swap_start(g4, name):
    n, _, h, C = g4.shape
    tok_shape, tok_spec = _token_spec()
    sem = pltpu.SemaphoreType.DMA(())
    got = pltpu.with_memory_space_constraint(lax.empty((n, h, C), g4.dtype), pltpu.HBM)
    return pl.pallas_call(
        functools.partial(_swap_start_body),
        name=name,
        in_specs=[HBM, HBM],
        out_specs=[SEM, SEM, HBM, HBM, tok_spec],
        out_shape=[sem, sem, pltpu.HBM(g4.shape, g4.dtype), pltpu.HBM(got.shape, got.dtype), tok_shape],
        input_output_aliases={0: 2, 1: 3},
        compiler_params=pltpu.CompilerParams(has_side_effects=EFFECT),
    )(pltpu.with_memory_space_constraint(g4, pltpu.HBM), got)


def _swap_wait_body(g4_ref, got_ref, send_sem, recv_sem, after_ref, g4_out, got_out):
    del after_ref, g4_out, got_out
    cp = _swap_copy(g4_ref, got_ref, send_sem, recv_sem)
    cp.wait_send()
    cp.wait_recv()


def _swap_wait(g4, got, send_sem, recv_sem, after, name):
    return pl.pallas_call(
        functools.partial(_swap_wait_body),
        name=name,
        in_specs=[HBM, HBM, SEM, SEM, ANY],
        out_specs=[HBM, HBM],
        out_shape=[pltpu.HBM(g4.shape, g4.dtype), pltpu.HBM(got.shape, got.dtype)],
        input_output_aliases={0: 0, 1: 1},
        compiler_params=pltpu.CompilerParams(has_side_effects=EFFECT),
    )(g4, got, send_sem, recv_sem, after)


def _add_chips_body(tr, n_i, pc_ref, own_ref, l0_ref, l1_ref, l2_ref, o_ref, got_ref, send_sems, recv_sem):
    del pc_ref
    i = pl.program_id(0)
    r = own_ref[...].astype(F32) + l0_ref[...].astype(F32)
    o_ref[...] = (r + l1_ref[...].astype(F32) + l2_ref[...].astype(F32)).astype(o_ref.dtype)
    r_tile = pl.multiple_of(i * tr, tr)
    _send_tile_to_sibling(lambda r0, n: o_ref.at[pl.ds(r0, n)], lambda r0, n: got_ref.at[pl.ds(r_tile + r0, n)], tr,
                          got_ref, send_sems, recv_sem, i == n_i - 1)


def _add_chips(pair, land, pc, name):
    _, h, C = pair.shape
    tr = _pick(h, 256, 16)
    slot = lambda j: pl.BlockSpec((None, tr, C), lambda i, pc: (j, i, 0))
    return pl.pallas_call(
        functools.partial(_add_chips_body, tr, h // tr),
        name=name,
        grid_spec=pltpu.PrefetchScalarGridSpec(
            num_scalar_prefetch=1,
            grid=(h // tr,),
            in_specs=[pl.BlockSpec((None, tr, C), lambda i, pc: (pc[0], i, 0)), slot(0), slot(1), slot(2)],
            out_specs=[pl.BlockSpec((tr, C), lambda i, pc: (i, 0)), HBM],
            scratch_shapes=TILE_SEMS,
        ),
        out_shape=[jax.ShapeDtypeStruct((h, C), pair.dtype), jax.ShapeDtypeStruct((h, C), pair.dtype)],
        compiler_params=_params(("arbitrary",)),
    )(pc, pair, land, land, land)


def _peer(r):
    x, y, c, _ = _me()
    return (x ^ ((r >> 2) & 1), y ^ ((r >> 1) & 1), c ^ (r & 1))


def _ar_start_body(x_ref, land_ref, send_sems, recv_sems, x_thru, land_thru, token_ref):
    del x_thru, land_thru
    for r in range(1, N_DEV):
        pltpu.make_async_remote_copy(src_ref=x_ref, dst_ref=land_ref.at[r - 1], send_sem=send_sems.at[r - 1],
                                     recv_sem=recv_sems.at[r - 1], device_id=_peer(r), device_id_type=MESH).start()
    token_ref[...] = jnp.zeros_like(token_ref)


def _ar_start(packed):
    tok_shape, tok_spec = _token_spec()
    sems = pltpu.SemaphoreType.DMA((N_DEV - 1,))
    land = pltpu.with_memory_space_constraint(lax.empty((N_DEV - 1,) + packed.shape, packed.dtype), pltpu.HBM)
    return pl.pallas_call(
        functools.partial(_ar_start_body),
        name="ar_start",
        in_specs=[HBM, HBM],
        out_specs=[SEM, SEM, HBM, HBM, tok_spec],
        out_shape=[sems, sems, pltpu.HBM(packed.shape, packed.dtype), pltpu.HBM(land.shape, land.dtype), tok_shape],
        input_output_aliases={0: 2, 1: 3},
        compiler_params=pltpu.CompilerParams(has_side_effects=EFFECT),
    )(pltpu.with_memory_space_constraint(packed, pltpu.HBM), land)


def _ar_wait_body(x_ref, land_ref, send_sems, recv_sems, after_ref, x_out, land_out):
    del after_ref, x_out, land_out
    for r in range(1, N_DEV):
        cp = pltpu.make_async_remote_copy(src_ref=x_ref, dst_ref=land_ref.at[r - 1], send_sem=send_sems.at[r - 1],
                                          recv_sem=recv_sems.at[r - 1], device_id=_peer(r), device_id_type=MESH)
        cp.wait_send()
        cp.wait_recv()


def _ar_wait(packed, land, send_sems, recv_sems, after):
    return pl.pallas_call(
        functools.partial(_ar_wait_body),
        name="ar_wait",
        in_specs=[HBM, HBM, SEM, SEM, ANY],
        out_specs=[HBM, HBM],
        out_shape=[pltpu.HBM(packed.shape, packed.dtype), pltpu.HBM(land.shape, land.dtype)],
        input_output_aliases={0: 0, 1: 1},
        compiler_params=pltpu.CompilerParams(has_side_effects=EFFECT),
    )(packed, land, send_sems, recv_sems, after)


def _ar_sum_body(me_ref, own_ref, *rest):
    o_ref = rest[N_DEV]
    acc = None
    for dev in range(N_DEV):
        term = jnp.where(me_ref[0] == dev, own_ref[...], rest[dev][...])
        acc = term if acc is None else acc + term
    o_ref[...] = acc


def _ar_sum(packed, land, me):
    R, C = packed.shape
    tr = _pick(R, 552, 8)
    own = pl.BlockSpec((tr, C), lambda i, me: (i, 0))
    slot = lambda dev: pl.BlockSpec((None, tr, C), lambda i, me: (jnp.maximum((dev ^ me[0]) - 1, 0), i, 0))
    return pl.pallas_call(
        functools.partial(_ar_sum_body),
        name="ar_sum",
        grid_spec=pltpu.PrefetchScalarGridSpec(
            num_scalar_prefetch=1,
            grid=(R // tr,),
            in_specs=[own] + [slot(dev) for dev in range(N_DEV)],
            out_specs=pl.BlockSpec((tr, C), lambda i, me: (i, 0)),
        ),
        out_shape=jax.ShapeDtypeStruct((R, C), F32),
        compiler_params=_params(("parallel",)),
    )(me, packed, *([land] * N_DEV))


def _pack(arrays):
    rows = []
    for a in arrays:
        flat = a.reshape(-1).astype(F32)
        pad = (-flat.shape[0]) % BLK
        rows.append(jnp.pad(flat, (0, pad)).reshape(-1, BLK))
    packed = jnp.concatenate(rows, axis=0)
    pad = (-packed.shape[0]) % 8
    return jnp.pad(packed, ((0, pad), (0, 0)))


def _unpack(packed, shapes):
    out, r = [], 0
    for s in shapes:
        n = 1
        for k in s:
            n *= k
        nr = -(-n // BLK)
        out.append(packed[r:r + nr].reshape(-1)[:n].reshape(s))
        r += nr
    return out


def kernel(x, norm1_g, w_in, q_norm_g, k_norm_g, attn_sinks, gate_ln_g, gate_ln_b, w_spatial, b_spatial, out_norm_attn_g, out_norm_gate_g, w_out, norm2_g, w_ffn_gate, w_ffn_up, w_ffn_down, loss_target, m_norm1_g, m_w_in, m_q_norm_g, m_k_norm_g, m_attn_sinks, m_gate_ln_g, m_gate_ln_b, m_w_spatial, m_b_spatial, m_out_norm_attn_g, m_out_norm_gate_g, m_w_out, m_norm2_g, m_w_ffn_gate, m_w_ffn_up, m_w_ffn_down, v_norm1_g, v_w_in, v_q_norm_g, v_k_norm_g, v_attn_sinks, v_gate_ln_g, v_gate_ln_b, v_w_spatial, v_b_spatial, v_out_norm_attn_g, v_out_norm_gate_g, v_w_out, v_norm2_g, v_w_ffn_gate, v_w_ffn_up, v_w_ffn_down):
    bl, seq, D = x.shape
    T = bl * seq
    attn_w, gate_w = out_norm_attn_g.shape[1], out_norm_gate_g.shape[1]
    d = _Dims(seq, attn_w, gate_w)
    G = d.n_groups
    in_w = d.in_w
    dff = w_ffn_gate.shape[2] * N_CHIPS
    assert w_in.shape[2] * N_CHIPS == in_w and seq % BLK == 0 and attn_w % (2 * BLK) == 0

    pc = jnp.stack([2 * lax.axis_index("x") + lax.axis_index("y"), lax.axis_index("c")]).astype(jnp.int32)
    big = [w_in[0], w_out[0], w_ffn_gate[0], w_ffn_up[0], w_ffn_down[0]]
    names = ["in", "out", "gate", "up", "down"]
    xf = x.reshape(T, D)
    tgt = loss_target.reshape(T, D)
    send, recv, buf, behind = _ag_start(_cast_into(big[0], pc, "cast_in"), norm1_g, "ag_start_in")
    started = [(send, recv, buf)]
    h1 = _rms_fwd(xf, norm1_g, "norm1_fwd", after=behind)
    behind = h1
    for w, n in zip(big[1:], names[1:]):
        buf = _cast_into(w, pc, "cast_" + n, side_by_side=n in ("gate", "up"))
        send, recv, buf, behind = _ag_start(buf, behind, "ag_start_" + n)
        started.append((send, recv, buf))

    def gathered(k, after):
        send, recv, buf = started[k]
        buf = _ag_wait(buf, send, recv, after, "ag_wait_" + names[k])
        buf = _ag_pair(buf, pc, "ag_pair_" + names[k])
        rs, cs = big[k].shape
        return buf.reshape(rs, N_CHIPS * cs) if len(buf.shape) == 3 else buf.reshape(N_CHIPS, rs, cs)

    qg2 = jnp.tile(q_norm_g, (1, 2))
    kg2 = jnp.tile(k_norm_g, (1, 2))
    lg, lb, wsp = gate_ln_g[0], gate_ln_b[0], w_spatial[0]
    bcol = jnp.broadcast_to(b_spatial[0][:, :, None], (G, BLK, BLK))

    win_full = jnp.transpose(gathered(0, behind), (1, 0, 2)).reshape(D, in_w)
    proj = _matmul(h1, win_full, "nn", F32, "proj_fwd", tm=1024, tn=1664)
    ya, yg, yn = _mixer_fwd(d, proj, attn_sinks, qg2, kg2, lg, lb, wsp, bcol, out_norm_attn_g, out_norm_gate_g)
    wout_full = gathered(1, yn).reshape(attn_w + gate_w, D)
    x1 = _matmul(yn, wout_full, "nn", F32, "out_fwd", tm=1024, tn=1024, add=xf)
    h2 = _rms_fwd(x1, norm2_g, "norm2_fwd")
    wg_full, wu_full = gathered(2, h2), gathered(3, h2)
    a, b, f = _ffn_up(h2, wg_full, wu_full)
    wd_full = gathered(4, f).reshape(dff, D)
    dx2, dx2b, loss_local = _ffn_down_loss(f, wd_full, x1, tgt)

    def swap_start(g, n):
        g4 = g.reshape(N_CHIPS, 2, g.shape[1] // 2, g.shape[2])
        return _swap_start(g4, "rs_swap_start_" + n)

    def reduce_start(swapping, n, after):
        send, recv, g4, got, _ = swapping
        g4, got = _swap_wait(g4, got, send, recv, after, "rs_swap_wait_" + n)
        return _rs_start(_add_pair(g4, got, pc, "rs_add_pair_" + n), got, "rs_start_" + n)

    reducing = {}
    g_d = _matmul(f, dx2b, "tn", BF16, "ffn_down_dw", tm=1408, tn=1024, tk=2048, out_slab="r")
    swap_d = swap_start(g_d, "down")
    da, db = _ffn_down_dx(dx2b, wd_full, a, b, swap_d[4])
    g_g = _matmul(h2, da, "tn", BF16, "ffn_gate_dw", tm=1024, tn=1408, tk=2048, out_slab="c")
    swap_g = swap_start(g_g, "gate")
    reducing["down"] = reduce_start(swap_d, "down", swap_g[4])
    g_u = _matmul(h2, db, "tn", BF16, "ffn_up_dw", tm=1024, tn=1408, tk=2048, out_slab="c",
                  after=reducing["down"][4])
    swap_u = swap_start(g_u, "up")
    reducing["gate"] = reduce_start(swap_g, "gate", swap_u[4])
    dh2 = _matmul(da, wg_full, "nt", F32, "ffn_gate_dx", tm=1024, tn=1024, tk=2816, after=reducing["gate"][4])
    dh2 = _matmul(db, wu_full, "nt", F32, "ffn_up_dx", tm=1024, tn=1024, tk=2816, add=dh2)
    reducing["up"] = reduce_start(swap_u, "up", dh2)
    dx1, dx1b, dg_norm2 = _rms_bwd(x1, norm2_g, dh2, dx2, "norm2_bwd", True)
    g_o = _matmul(yn, dx1b, "tn", BF16, "out_dw", tm=512, tn=1024, tk=2048, out_slab="r",
                  after=reducing["up"][4])
    swap_o = swap_start(g_o, "out")
    dy = _matmul(dx1b, wout_full, "nt", F32, "out_dx", tm=1024, tn=1024, after=swap_o[4])
    (dproj, dkv, dqg, dkg, dsk, dlg, dlb, dwsp, dbsp, dgoa, dgog) = _mixer_bwd(
        d, proj, ya, yg, dy, attn_sinks, qg2, kg2, lg, lb, wsp, bcol, out_norm_attn_g, out_norm_gate_g)
    dproj = _put_kv(d, dproj, dkv)
    reducing["out"] = reduce_start(swap_o, "out", dproj)
    g_in_full = _matmul(h1, dproj, "tn", BF16, "proj_dw", tm=1024, tn=1664, tk=2048,
                        after=reducing["out"][4])
    g_i = jnp.transpose(g_in_full.reshape(D, N_CHIPS, in_w // N_CHIPS), (1, 0, 2))
    g4_i = g_i.reshape(N_CHIPS, 2, D // 2, in_w // N_CHIPS)
    pair_i = _add_pair(g4_i, _swap_halves(g4_i, pc, "rs_swap_in"), pc, "rs_add_pair_in")
    reducing["in"] = _rs_start(pair_i, g_i, "rs_start_in")
    dh1 = _matmul(dproj, win_full, "nt", F32, "proj_dx", tm=1024, tn=1024, after=reducing["in"][4])
    dx, dg_norm1 = _rms_bwd(xf, norm1_g, dh1, dx1, "norm1_bwd", False)

    dqg64 = dqg[:, :HEAD_DIM] + dqg[:, HEAD_DIM:]
    dkg64 = dkg[:, :HEAD_DIM] + dkg[:, HEAD_DIM:]
    small_g_local = [dg_norm1, dqg64, dkg64, dsk[:, :d.n_heads], dlg, dlb, dwsp, dbsp, dgoa, dgog, dg_norm2,
                     loss_local]
    ar_send, ar_recv, ar_own, ar_land, ar_token = _ar_start(_pack(small_g_local))

    big_m = [m_w_in[0], m_w_out[0], m_w_ffn_gate[0], m_w_ffn_up[0], m_w_ffn_down[0]]
    big_v = [v_w_in[0], v_w_out[0], v_w_ffn_gate[0], v_w_ffn_up[0], v_w_ffn_down[0]]
    big_grads, big_d, big_nm, big_nv = [], [], [], []
    for w, m, v, n in zip(big, big_m, big_v, names):
        send, recv, pair, land, _ = reducing[n]
        pair, land = _rs_wait(pair, land, send, recv, ar_token, "rs_wait_" + n)
        own, got = _add_chips(pair, land, pc, "rs_add_chips_" + n)
        outs = _adamw_halves(w, own, got, m, v, pc, "adamw_" + n)
        for lst, o in zip((big_grads, big_d, big_nm, big_nv), outs):
            lst.append(o.reshape(w.shape))

    small_names_w = [norm1_g, q_norm_g, k_norm_g, attn_sinks, gate_ln_g, gate_ln_b, w_spatial, b_spatial,
                     out_norm_attn_g, out_norm_gate_g, norm2_g]
    small_m = [m_norm1_g, m_q_norm_g, m_k_norm_g, m_attn_sinks, m_gate_ln_g, m_gate_ln_b, m_w_spatial, m_b_spatial,
               m_out_norm_attn_g, m_out_norm_gate_g, m_norm2_g]
    small_v = [v_norm1_g, v_q_norm_g, v_k_norm_g, v_attn_sinks, v_gate_ln_g, v_gate_ln_b, v_w_spatial, v_b_spatial,
               v_out_norm_attn_g, v_out_norm_gate_g, v_norm2_g]
    shapes = [w.shape for w in small_names_w] + [loss_local.shape]
    ride = [jnp.zeros(loss_local.shape, F32)]
    ar_own, ar_land = _ar_wait(ar_own, ar_land, ar_send, ar_recv, big_nv[-1])
    me = (4 * lax.axis_index("x") + 2 * lax.axis_index("y") + lax.axis_index("c")).astype(jnp.int32).reshape(1)
    sg = _ar_sum(ar_own, ar_land, me)
    sd, snm, snv = _adamw(_pack(small_names_w + ride), sg, _pack(small_m + ride), _pack(small_v + ride), "adamw_small")
    small_g, small_d, small_nm, small_nv = (_unpack(t, shapes) for t in (sg, sd, snm, snv))
    loss = small_g[-1][0, 0]

    def order(small, bigs):
        s = list(small)
        bg = [t[None] for t in bigs]
        return [s[0], bg[0], s[1], s[2], s[3], s[4], s[5], s[6], s[7], s[8], s[9], bg[1], s[10], bg[2], bg[3], bg[4]]

    grad_x = dx.reshape(bl, seq, D)
    return (loss, grad_x, *order(small_g, big_grads), *order(small_d, big_d), *order(small_nm, big_nm),
            *order(small_nv, big_nv))
```

```python
import functools

import jax
import jax.numpy as jnp
from jax import lax
from jax.experimental import pallas as pl
from jax.experimental.pallas import tpu as pltpu

F32 = jnp.float32
BF16 = jnp.bfloat16
MESH = pl.DeviceIdType.MESH

EPS = 1e-6
HEAD_DIM = 64
N_KV_HEADS = 2
BLK = 128
N_CHIPS = 4
N_DEV = 8
NEG = -1e30

ADAM_LR = 0.001
ADAM_B1 = 0.9
ADAM_B2 = 0.999
ADAM_EPS = 1e-08
ADAM_WD = 0.01
ADAM_STEP = 10

VMEM_LIMIT = 56 * 1024 * 1024

NN = (((1,), (0,)), ((), ()))
NT = (((1,), (1,)), ((), ()))
TN = (((0,), (0,)), ((), ()))
HBM = pl.BlockSpec(memory_space=pltpu.HBM)
ANY = pl.BlockSpec(memory_space=pl.ANY)
SEM = pl.BlockSpec(memory_space=pltpu.SEMAPHORE)
EFFECT = pltpu.SideEffectType.DATAFLOW_SIDE_EFFECTING


def _dot(a, b, dn):
    return lax.dot_general(a, b, dn, preferred_element_type=F32)


def _pick(dim, pref, align=128):
    if dim <= pref:
        return dim
    t = (pref // align) * align
    while t >= align:
        if dim % t == 0:
            return t
        t -= align
    return dim


def _params(sem):
    return pltpu.CompilerParams(dimension_semantics=sem, vmem_limit_bytes=VMEM_LIMIT)


MM_CHUNK = 512


def _col_chunks(tn):
    return [slice(c0, min(c0 + MM_CHUNK, tn)) for c0 in range(0, tn, MM_CHUNK)]


def _mm_body(dn, nk, has_add, has_after, *refs):
    a_ref, b_ref = refs[:2]
    add_ref = refs[2] if has_add else None
    o_ref = refs[2 + has_add + has_after]
    chunks = _col_chunks(o_ref.shape[-1])

    def dot(cols):
        return _dot(a_ref[...], b_ref[cols, :] if dn == NT else b_ref[:, cols], dn)

    def finish(cols, r):
        if add_ref is not None:
            r = r + add_ref[:, cols]
        o_ref[:, cols] = r.astype(o_ref.dtype)

    if nk == 1:
        for cols in chunks:
            finish(cols, dot(cols))
        return
    acc_ref = refs[-1]
    k = pl.program_id(2)

    @pl.when(k == 0)
    def _():
        for cols in chunks:
            acc_ref[:, cols] = dot(cols)

    if nk > 2:
        @pl.when(jnp.logical_and(k > 0, k < nk - 1))
        def _():
            for cols in chunks:
                acc_ref[:, cols] += dot(cols)

    @pl.when(k == nk - 1)
    def _():
        for cols in chunks:
            finish(cols, acc_ref[:, cols] + dot(cols))


def _matmul(a, b, mode, out_dtype, name, *, tm, tn, tk=None, add=None, out_slab=None, after=None):
    if mode == "nn":
        (M, K), N = a.shape, b.shape[1]
    elif mode == "nt":
        (M, K), N = a.shape, b.shape[0]
    else:
        (K, M), N = a.shape, b.shape[1]
    tk = K if tk is None else tk
    tm, tn, tk = _pick(M, tm), _pick(N, tn), _pick(K, tk)
    if out_slab == "c":
        tn = _pick(N // N_CHIPS, tn)
    if out_slab == "r":
        tm = _pick(M // N_CHIPS, tm)
    gm, gn, gk = M // tm, N // tn, K // tk

    if mode == "tn":
        a_spec = pl.BlockSpec((tk, tm), lambda j, i, k: (k, i))
        b_spec = pl.BlockSpec((tk, tn), lambda j, i, k: (k, j))
    else:
        a_spec = pl.BlockSpec((tm, tk), lambda j, i, k: (i, k))
        if mode == "nn":
            b_spec = pl.BlockSpec((tk, tn), lambda j, i, k: (k, j))
        else:
            b_spec = pl.BlockSpec((tn, tk), lambda j, i, k: (j, k))

    if out_slab == "c":
        per = (N // N_CHIPS) // tn
        o_spec = pl.BlockSpec((None, tm, tn), lambda j, i, k: (j // per, i, j % per))
        o_shape = jax.ShapeDtypeStruct((N_CHIPS, M, N // N_CHIPS), out_dtype)
    elif out_slab == "r":
        per = (M // N_CHIPS) // tm
        o_spec = pl.BlockSpec((None, tm, tn), lambda j, i, k: (i // per, i % per, j))
        o_shape = jax.ShapeDtypeStruct((N_CHIPS, M // N_CHIPS, N), out_dtype)
    else:
        o_spec = pl.BlockSpec((tm, tn), lambda j, i, k: (i, j))
        o_shape = jax.ShapeDtypeStruct((M, N), out_dtype)

    dn = {"nn": NN, "nt": NT, "tn": TN}[mode]
    in_specs = [a_spec, b_spec]
    args = [a, b]
    if add is not None:
        in_specs.append(pl.BlockSpec((tm, tn), lambda j, i, k: (i, j)))
        args.append(add)
    if after is not None:
        in_specs.append(ANY)
        args.append(after)
    return pl.pallas_call(
        functools.partial(_mm_body, dn, gk, add is not None, after is not None),
        name=name,
        grid=(gn, gm, gk),
        in_specs=in_specs,
        out_specs=o_spec,
        out_shape=o_shape,
        scratch_shapes=[pltpu.VMEM((tm, tn), F32)] if gk > 1 else [],
        compiler_params=_params(("parallel", "parallel", "arbitrary")),
    )(*args)


def _rms_fwd_body(x_ref, g_ref, *rest):
    h_ref = rest[-1]
    x = x_ref[...]
    r = lax.rsqrt(jnp.mean(x * x, axis=-1, keepdims=True) + EPS)
    h_ref[...] = (x * r * g_ref[...]).astype(h_ref.dtype)


def _rms_fwd(x, g, name, after=None):
    T, D = x.shape
    tr = _pick(T, 256, 16)
    extra = [] if after is None else [after]
    return pl.pallas_call(
        functools.partial(_rms_fwd_body),
        name=name,
        grid=(T // tr,),
        in_specs=[pl.BlockSpec((tr, D), lambda i: (i, 0)), pl.BlockSpec((1, D), lambda i: (0, 0))] + [ANY] * len(extra),
        out_specs=pl.BlockSpec((tr, D), lambda i: (i, 0)),
        out_shape=jax.ShapeDtypeStruct((T, D), BF16),
        compiler_params=_params(("parallel",)),
    )(x, g, *extra)


def _rms_bwd_body(with_bf16, x_ref, g_ref, dh_ref, res_ref, dx_ref, *rest):
    dg_ref = rest[-1]

    @pl.when(pl.program_id(0) == 0)
    def _():
        dg_ref[...] = jnp.zeros_like(dg_ref)

    x = x_ref[...]
    r = lax.rsqrt(jnp.mean(x * x, axis=-1, keepdims=True) + EPS)
    xh = x * r
    dh = dh_ref[...]
    dg_ref[...] += jnp.sum(dh * xh, axis=0, keepdims=True)
    t = dh * g_ref[...]
    dx = res_ref[...] + r * (t - xh * jnp.mean(t * xh, axis=-1, keepdims=True))
    dx_ref[...] = dx
    if with_bf16:
        rest[0][...] = dx.astype(BF16)


def _rms_bwd(x, g, dh, res, name, with_bf16):
    T, D = x.shape
    tr = _pick(T, 256, 16)
    row = pl.BlockSpec((tr, D), lambda i: (i, 0))
    vec = pl.BlockSpec((1, D), lambda i: (0, 0))
    extra = [jax.ShapeDtypeStruct((T, D), BF16)] if with_bf16 else []
    return pl.pallas_call(
        functools.partial(_rms_bwd_body, with_bf16),
        name=name,
        grid=(T // tr,),
        in_specs=[row, vec, row, row],
        out_specs=[row] + [row] * len(extra) + [vec],
        out_shape=[jax.ShapeDtypeStruct((T, D), F32)] + extra + [jax.ShapeDtypeStruct((1, D), F32)],
        compiler_params=_params(("arbitrary",)),
    )(x, g, dh, res)


def _ffn_up_body(h_ref, wg_ref, wu_ref, a_ref, b_ref, f_ref):
    for cols in _col_chunks(a_ref.shape[-1]):
        a = _dot(h_ref[...], wg_ref[:, cols], NN)
        b = _dot(h_ref[...], wu_ref[:, cols], NN)
        a_ref[:, cols] = a
        b_ref[:, cols] = b
        f_ref[:, cols] = (a * (1.0 / (1.0 + jnp.exp(-a))) * b).astype(f_ref.dtype)


def _ffn_up(h, wg, wu):
    T, D = h.shape
    F = wg.shape[1]
    tm, tn = _pick(T, 1024), _pick(F, MM_CHUNK)
    hs = pl.BlockSpec((tm, D), lambda j, i: (i, 0))
    ws = pl.BlockSpec((D, tn), lambda j, i: (0, j))
    os = pl.BlockSpec((tm, tn), lambda j, i: (i, j))
    return pl.pallas_call(
        functools.partial(_ffn_up_body),
        name="ffn_up_fwd",
        grid=(F // tn, T // tm),
        in_specs=[hs, ws, ws],
        out_specs=[os, os, os],
        out_shape=[jax.ShapeDtypeStruct((T, F), F32), jax.ShapeDtypeStruct((T, F), F32),
                   jax.ShapeDtypeStruct((T, F), BF16)],
        compiler_params=_params(("parallel", "parallel")),
    )(h, wg, wu)


def _ffn_down_dx_body(dx_ref, wd_ref, a_ref, b_ref, after_ref, da_ref, db_ref):
    del after_ref
    for cols in _col_chunks(da_ref.shape[-1]):
        df = _dot(dx_ref[...], wd_ref[cols, :], NT)
        a = a_ref[:, cols]
        s = 1.0 / (1.0 + jnp.exp(-a))
        da_ref[:, cols] = (df * b_ref[:, cols] * (s * (1.0 + a * (1.0 - s)))).astype(da_ref.dtype)
        db_ref[:, cols] = (df * (a * s)).astype(db_ref.dtype)


def _ffn_down_dx(dx2b, wd, a, b, after):
    T, D = dx2b.shape
    F = wd.shape[0]
    tm, tn = _pick(T, 512), _pick(F, 1408)
    xs = pl.BlockSpec((tm, D), lambda j, i: (i, 0))
    ws = pl.BlockSpec((tn, D), lambda j, i: (j, 0))
    os = pl.BlockSpec((tm, tn), lambda j, i: (i, j))
    return pl.pallas_call(
        functools.partial(_ffn_down_dx_body),
        name="ffn_down_dx",
        grid=(F // tn, T // tm),
        in_specs=[xs, ws, os, os, ANY],
        out_specs=[os, os],
        out_shape=[jax.ShapeDtypeStruct((T, F), BF16), jax.ShapeDtypeStruct((T, F), BF16)],
        compiler_params=_params(("parallel", "parallel")),
    )(dx2b, wd, a, b, after)


def _ffn_down_loss_body(nk, inv_d, f_ref, wd_ref, x1_ref, tgt_ref, dx2_ref, dx2b_ref, loss_ref, *scratch):
    j, i, k = pl.program_id(0), pl.program_id(1), pl.program_id(2)
    chunks = _col_chunks(dx2_ref.shape[-1])

    def dot(cols):
        return _dot(f_ref[...], wd_ref[:, cols], NN)

    @pl.when(jnp.logical_and(jnp.logical_and(j == 0, i == 0), k == 0))
    def _():
        loss_ref[...] = jnp.zeros_like(loss_ref)

    def finish(ffn_of):
        total = jnp.zeros((1, 1), F32)
        for cols in chunks:
            e = ffn_of(cols) + x1_ref[:, cols] - tgt_ref[:, cols]
            dx2 = e * inv_d
            dx2_ref[:, cols] = dx2
            dx2b_ref[:, cols] = dx2.astype(BF16)
            total = total + jnp.sum(jnp.sum(e * e, axis=-1, keepdims=True), axis=0, keepdims=True)
        loss_ref[...] += (0.5 * inv_d) * total

    if nk == 1:
        finish(dot)
        return
    acc_ref = scratch[0]

    @pl.when(k == 0)
    def _():
        for cols in chunks:
            acc_ref[:, cols] = dot(cols)

    if nk > 2:
        @pl.when(jnp.logical_and(k > 0, k < nk - 1))
        def _():
            for cols in chunks:
                acc_ref[:, cols] += dot(cols)

    @pl.when(k == nk - 1)
    def _():
        finish(lambda cols: acc_ref[:, cols] + dot(cols))


def _ffn_down_loss(f, wd, x1, tgt):
    T, F = f.shape
    D = wd.shape[1]
    tm, tn, tk = _pick(T, 1024), _pick(D, 1024), _pick(F, 1408)
    gm, gn, gk = T // tm, D // tn, F // tk
    tile = pl.BlockSpec((tm, tn), lambda j, i, k: (i, j))
    return pl.pallas_call(
        functools.partial(_ffn_down_loss_body, gk, 1.0 / D),
        name="ffn_down_loss",
        grid=(gn, gm, gk),
        in_specs=[pl.BlockSpec((tm, tk), lambda j, i, k: (i, k)), pl.BlockSpec((tk, tn), lambda j, i, k: (k, j)),
                  tile, tile],
        out_specs=[tile, tile, pl.BlockSpec((1, 1), lambda j, i, k: (0, 0))],
        out_shape=[jax.ShapeDtypeStruct((T, D), F32), jax.ShapeDtypeStruct((T, D), BF16),
                   jax.ShapeDtypeStruct((1, 1), F32)],
        scratch_shapes=[pltpu.VMEM((tm, tn), F32)] if gk > 1 else [],
        compiler_params=_params(("arbitrary", "arbitrary", "arbitrary")),
    )(f, wd, x1, tgt)


def _lo_mask(shape):
    return lax.broadcasted_iota(jnp.int32, shape, len(shape) - 1) < HEAD_DIM


def _half_sums(t, lo):
    s_lo = jnp.sum(jnp.where(lo, t, 0.0), axis=-1, keepdims=True)
    s_hi = jnp.sum(jnp.where(lo, 0.0, t), axis=-1, keepdims=True)
    return jnp.where(lo, s_lo, s_hi)


def _head_rstd(t, lo):
    return lax.rsqrt(_half_sums(t * t, lo) * (1.0 / HEAD_DIM) + EPS)


def _place(t, lo, kv_head):
    if kv_head == 0:
        t_lo = jnp.where(lo, t, 0.0)
        t_hi = pltpu.roll(t_lo, HEAD_DIM, 1)
    else:
        t_hi = jnp.where(lo, 0.0, t)
        t_lo = pltpu.roll(t_hi, HEAD_DIM, 1)
    return jnp.concatenate([t_lo, t_hi], axis=0).astype(BF16)


def _unplace(c0, c1, lo):
    return jnp.where(lo, c0 + pltpu.roll(c0, HEAD_DIM, 1), c1 + pltpu.roll(c1, HEAD_DIM, 1))


def _band(kv_cur, kv_prev, kg, lo2):
    kb = jnp.concatenate([kv_prev[:, :BLK], kv_cur[:, :BLK]], axis=0)
    vb = jnp.concatenate([kv_prev[:, BLK:], kv_cur[:, BLK:]], axis=0)
    rk = _head_rstd(kb, lo2)
    kn = kb * rk * kg
    kk = [_place(kn, lo2, h) for h in range(N_KV_HEADS)]
    vv = [_place(vb, lo2, h) for h in range(N_KV_HEADS)]
    return kb, rk, kk, vv


def _score_geometry(first_i32):
    qi = lax.broadcasted_iota(jnp.int32, (BLK, 4 * BLK), 0)
    col = lax.broadcasted_iota(jnp.int32, (BLK, 4 * BLK), 1)
    kj = col & (2 * BLK - 1)
    dist = qi + BLK - kj
    valid = (dist >= 0) & (dist < BLK) & (kj >= first_i32 * BLK)
    return col, dist.astype(F32), valid


def _pair_probs(qn, kk, col, distf, valid, slope0, slope1, sink0, sink1):
    s = _dot(qn.astype(BF16), kk, NT) * (HEAD_DIM ** -0.5)
    slope = jnp.where(col < 2 * BLK, slope0, slope1)
    logits = jnp.where(valid, s - slope * distf, NEG)
    probs, psink = [], []
    for hh, sk in ((0, sink0), (1, sink1)):
        l = logits[:, 2 * BLK * hh:2 * BLK * (hh + 1)]
        m = jnp.maximum(jnp.max(l, axis=-1, keepdims=True), sk)
        p = jnp.exp(l - m)
        es = jnp.exp(sk - m)
        inv = 1.0 / (jnp.sum(p, axis=-1, keepdims=True) + es)
        probs.append(p * inv)
        psink.append(es * inv)
    return probs, psink


def _gelu(z, with_grad=False):
    cdf = 0.5 * (1.0 + lax.erf(z * (0.5 ** 0.5)))
    if not with_grad:
        return z * cdf
    return z * cdf, cdf + z * jnp.exp(-0.5 * z * z) * ((2.0 * jnp.pi) ** -0.5)


def _tril_w(w):
    r = lax.broadcasted_iota(jnp.int32, (BLK, BLK), 0)
    c = lax.broadcasted_iota(jnp.int32, (BLK, BLK), 1)
    return jnp.where(r >= c, w, 0.0), r >= c


def _gate_fwd_group(zu, zv, lg, lb, w, bcol, with_grad=False):
    u, v = _gelu(zu, with_grad), _gelu(zv, with_grad)
    if with_grad:
        (u, du_dz), (v, dv_dz) = u, v
    mu = jnp.mean(v, axis=-1, keepdims=True)
    vc = v - mu
    rs = lax.rsqrt(jnp.mean(vc * vc, axis=-1, keepdims=True) + EPS)
    vh = vc * rs
    vn = vh * lg + lb
    wt, tril = _tril_w(w)
    mixed = _dot(wt.astype(BF16), vn.astype(BF16), NN) + bcol
    if with_grad:
        return u, vh, rs, vn, wt, tril, mixed, du_dz, dv_dz
    return u, vh, rs, vn, wt, tril, mixed


class _Dims:
    def __init__(self, seq, attn_w, gate_w):
        self.seq, self.attn_w, self.gate_w = seq, attn_w, gate_w
        self.n_heads = attn_w // HEAD_DIM
        self.group = self.n_heads // N_KV_HEADS
        self.n_pairs = attn_w // BLK
        self.n_groups = gate_w // BLK
        self.kv_col = attn_w // (2 * BLK)
        self.u0 = attn_w + 2 * BLK
        self.v0 = self.u0 + gate_w
        self.in_w = self.v0 + gate_w
        self.slopes = [2.0 ** (-8.0 * (h + 1) / self.n_heads) for h in range(self.n_heads)]


def _mixer_fwd_body(d, sink_ref, proj_ref, kvp_ref, qg_ref, kg_ref, lg_ref, lb_ref, w_ref, b_ref, goa_ref, gog_ref,
                    ya_ref, yg_ref, y_ref):
    i = pl.program_id(0)
    first = (i % (d.seq // BLK) == 0).astype(jnp.int32)
    lo = _lo_mask((BLK, BLK))
    lo2 = _lo_mask((2 * BLK, BLK))
    kv_cur = proj_ref[:, d.attn_w:d.attn_w + 2 * BLK]
    _, _, kk, vv = _band(kv_cur, kvp_ref[...], kg_ref[...], lo2)
    col, distf, valid = _score_geometry(first)
    qg = qg_ref[...]
    for j in range(d.n_pairs):
        h0, h1 = 2 * j, 2 * j + 1
        kh = h0 // d.group
        q2 = proj_ref[:, BLK * j:BLK * (j + 1)]
        qn = q2 * _head_rstd(q2, lo) * qg
        probs, _ = _pair_probs(qn, kk[kh], col, distf, valid, d.slopes[h0], d.slopes[h1],
                               sink_ref[0, h0], sink_ref[0, h1])
        p = jnp.concatenate(probs, axis=1).astype(BF16)
        ya_ref[:, BLK * j:BLK * (j + 1)] = _dot(p, vv[kh], NN)
    for g in range(d.n_groups):
        zu = proj_ref[:, d.u0 + BLK * g:d.u0 + BLK * (g + 1)]
        zv = proj_ref[:, d.v0 + BLK * g:d.v0 + BLK * (g + 1)]
        u, _, _, _, _, _, mixed = _gate_fwd_group(zu, zv, lg_ref[g:g + 1, :], lb_ref[g:g + 1, :], w_ref[g], b_ref[g])
        yg_ref[:, BLK * g:BLK * (g + 1)] = u * mixed
    ya = ya_ref[...]
    ra = lax.rsqrt(jnp.mean(ya * ya, axis=-1, keepdims=True) + EPS)
    y_ref[:, :d.attn_w] = (ya * ra * goa_ref[...]).astype(y_ref.dtype)
    yg = yg_ref[...]
    rg = lax.rsqrt(jnp.mean(yg * yg, axis=-1, keepdims=True) + EPS)
    y_ref[:, d.attn_w:] = (yg * rg * gog_ref[...]).astype(y_ref.dtype)


def _mixer_specs(d, T):
    row = lambda w: pl.BlockSpec((BLK, w), lambda i: (i, 0))
    const2 = lambda a: pl.BlockSpec(a.shape, lambda i: (0, 0))
    const3 = lambda a: pl.BlockSpec(a.shape, lambda i: (0, 0, 0))
    kv_prev = pl.BlockSpec((BLK, 2 * BLK), lambda i: (jnp.maximum(i - 1, 0), d.kv_col))
    return row, const2, const3, kv_prev


def _mixer_fwd(d, proj, sinks, qg2, kg2, lg, lb, wsp, bcol, goa, gog):
    T = proj.shape[0]
    row, const2, const3, kv_prev = _mixer_specs(d, T)
    return pl.pallas_call(
        functools.partial(_mixer_fwd_body, d),
        name="mixer_fwd",
        grid=(T // BLK,),
        in_specs=[pl.BlockSpec(memory_space=pltpu.SMEM), row(d.in_w), kv_prev, const2(qg2), const2(kg2),
                  const2(lg), const2(lb), const3(wsp), const3(bcol), const2(goa), const2(gog)],
        out_specs=[row(d.attn_w), row(d.gate_w), row(d.attn_w + d.gate_w)],
        out_shape=[jax.ShapeDtypeStruct((T, d.attn_w), F32), jax.ShapeDtypeStruct((T, d.gate_w), F32),
                   jax.ShapeDtypeStruct((T, d.attn_w + d.gate_w), BF16)],
        compiler_params=_params(("parallel",)),
    )(sinks, proj, proj, qg2, kg2, lg, lb, wsp, bcol, goa, gog)


def _mixer_bwd_body(d, sink_ref, proj_ref, kvp_ref, ya_ref, yg_ref, dy_ref, qg_ref, kg_ref, lg_ref, lb_ref, w_ref,
                    b_ref, goa_ref, gog_ref,
                    dproj_ref, dkv_ref, dqg_ref, dkg_ref, dsk_ref, dlg_ref, dlb_ref, dw_ref, db_ref, dgoa_ref,
                    dgog_ref):
    i = pl.program_id(0)

    @pl.when(i == 0)
    def _():
        for r in (dqg_ref, dkg_ref, dsk_ref, dlg_ref, dlb_ref, dw_ref, db_ref, dgoa_ref, dgog_ref):
            r[...] = jnp.zeros_like(r)

    first = (i % (d.seq // BLK) == 0).astype(jnp.int32)
    lo = _lo_mask((BLK, BLK))
    lo2 = _lo_mask((2 * BLK, BLK))
    lane_row = lax.broadcasted_iota(jnp.int32, (1, BLK), 1)

    ya = ya_ref[...]
    ra = lax.rsqrt(jnp.mean(ya * ya, axis=-1, keepdims=True) + EPS)
    yah = ya * ra
    dyn = dy_ref[:, :d.attn_w]
    dgoa_ref[...] += jnp.sum(dyn * yah, axis=0, keepdims=True)
    t = dyn * goa_ref[...]
    dya = ra * (t - yah * jnp.mean(t * yah, axis=-1, keepdims=True))
    yg = yg_ref[...]
    rg = lax.rsqrt(jnp.mean(yg * yg, axis=-1, keepdims=True) + EPS)
    ygh = yg * rg
    dyn = dy_ref[:, d.attn_w:]
    dgog_ref[...] += jnp.sum(dyn * ygh, axis=0, keepdims=True)
    t = dyn * gog_ref[...]
    dyg = rg * (t - ygh * jnp.mean(t * ygh, axis=-1, keepdims=True))

    kv_cur = proj_ref[:, d.attn_w:d.attn_w + 2 * BLK]
    kg = kg_ref[...]
    kb, rk, kk, vv = _band(kv_cur, kvp_ref[...], kg, lo2)
    col, distf, valid = _score_geometry(first)
    qg = qg_ref[...]
    ck = [jnp.zeros((BLK, 2 * BLK), F32) for _ in range(N_KV_HEADS)]
    cv = [jnp.zeros((BLK, 2 * BLK), F32) for _ in range(N_KV_HEADS)]
    lo_rows = lax.broadcasted_iota(jnp.int32, (BLK, 2 * BLK), 0) < HEAD_DIM
    dsk = jnp.zeros((1, BLK), F32)
    dqg = jnp.zeros((1, BLK), F32)
    for j in range(d.n_pairs):
        h0, h1 = 2 * j, 2 * j + 1
        kh = h0 // d.group
        cols = slice(BLK * j, BLK * (j + 1))
        q2 = proj_ref[:, cols]
        rq = _head_rstd(q2, lo)
        qh = q2 * rq
        qn = qh * qg
        probs, psink = _pair_probs(qn, kk[kh], col, distf, valid, d.slopes[h0], d.slopes[h1],
                                   sink_ref[0, h0], sink_ref[0, h1])
        do2 = dya[:, cols]
        prod = do2 * ya[:, cols]
        delta = (jnp.sum(jnp.where(lo, prod, 0.0), axis=-1, keepdims=True),
                 jnp.sum(jnp.where(lo, 0.0, prod), axis=-1, keepdims=True))
        do2b = do2.astype(BF16)
        dp = _dot(do2b, vv[kh], NT)
        ds = []
        for hh in (0, 1):
            ds.append(probs[hh] * (dp[:, 2 * BLK * hh:2 * BLK * (hh + 1)] - delta[hh]))
            dsink = -jnp.sum(psink[hh] * delta[hh], axis=0, keepdims=True)
            dsk = dsk + jnp.where(lane_row == (h0 + hh), dsink, 0.0)
        dsb = (jnp.concatenate(ds, axis=1) * (HEAD_DIM ** -0.5)).astype(BF16)
        pb = jnp.concatenate(probs, axis=1).astype(BF16)
        qnb = qn.astype(BF16)
        dqn = _dot(dsb, kk[kh], NN)
        dkk = _dot(qnb, dsb, TN)
        dvv = _dot(do2b, pb, TN)
        ck[kh] = ck[kh] + jnp.where(lo_rows, dkk[:, :2 * BLK], 0.0) + jnp.where(lo_rows, 0.0, dkk[:, 2 * BLK:])
        cv[kh] = cv[kh] + jnp.where(lo_rows, dvv[:, :2 * BLK], 0.0) + jnp.where(lo_rows, 0.0, dvv[:, 2 * BLK:])
        dqg = dqg + jnp.sum(dqn * qh, axis=0, keepdims=True)
        t = dqn * qg
        dq2 = rq * (t - qh * (_half_sums(t * qh, lo) * (1.0 / HEAD_DIM)))
        dproj_ref[:, cols] = dq2.astype(dproj_ref.dtype)
    dsk_ref[...] += dsk
    dqg_ref[...] += dqg
    dkn = _unplace(jnp.transpose(ck[0]), jnp.transpose(ck[1]), lo2)
    dvb = _unplace(jnp.transpose(cv[0]), jnp.transpose(cv[1]), lo2)
    khat = kb * rk
    dkg_ref[...] += jnp.sum(dkn * khat, axis=0, keepdims=True)
    t = dkn * kg
    dkb = rk * (t - khat * (_half_sums(t * khat, lo2) * (1.0 / HEAD_DIM)))
    rows_cur = pl.ds(pl.multiple_of(i * BLK, BLK), BLK)
    rows_prev = pl.ds(pl.multiple_of(jnp.maximum(i - 1, 0) * BLK, BLK), BLK)
    dkv_ref[rows_cur, :] = jnp.concatenate([dkb[BLK:], dvb[BLK:]], axis=1)
    dkv_ref[rows_prev, :] += jnp.concatenate([dkb[:BLK], dvb[:BLK]], axis=1)
    dproj_ref[:, d.attn_w:d.attn_w + 2 * BLK] = jnp.zeros((BLK, 2 * BLK), dproj_ref.dtype)

    for g in range(d.n_groups):
        ucols = slice(d.u0 + BLK * g, d.u0 + BLK * (g + 1))
        vcols = slice(d.v0 + BLK * g, d.v0 + BLK * (g + 1))
        zu = proj_ref[:, ucols]
        zv = proj_ref[:, vcols]
        lg = lg_ref[g:g + 1, :]
        u, vh, rs, vn, wt, tril, mixed, du_dz, dv_dz = _gate_fwd_group(
            zu, zv, lg, lb_ref[g:g + 1, :], w_ref[g], b_ref[g], with_grad=True)
        dyg_g = dyg[:, BLK * g:BLK * (g + 1)]
        du = dyg_g * mixed
        dmix = dyg_g * u
        dmb = dmix.astype(BF16)
        db_ref[g:g + 1, :] += jnp.sum(jnp.transpose(dmix), axis=0, keepdims=True)
        dw_ref[g] += jnp.where(tril, _dot(dmb, vn.astype(BF16), NT), 0.0)
        dvn = _dot(wt.astype(BF16), dmb, TN)
        dlg_ref[g:g + 1, :] += jnp.sum(dvn * vh, axis=0, keepdims=True)
        dlb_ref[g:g + 1, :] += jnp.sum(dvn, axis=0, keepdims=True)
        dvh = dvn * lg
        dv = rs * (dvh - jnp.mean(dvh, axis=-1, keepdims=True) - vh * jnp.mean(dvh * vh, axis=-1, keepdims=True))
        dproj_ref[:, ucols] = (du * du_dz).astype(dproj_ref.dtype)
        dproj_ref[:, vcols] = (dv * dv_dz).astype(dproj_ref.dtype)


def _mixer_bwd(d, proj, ya, yg, dy, sinks, qg2, kg2, lg, lb, wsp, bcol, goa, gog):
    T = proj.shape[0]
    row, const2, const3, kv_prev = _mixer_specs(d, T)
    acc2 = lambda s: pl.BlockSpec(s, lambda i: (0, 0))
    G = d.n_groups
    out_shapes = [((T, d.in_w), BF16), ((T, 2 * BLK), F32), ((1, BLK), F32), ((1, BLK), F32), ((1, BLK), F32),
                  ((G, BLK), F32), ((G, BLK), F32), ((G, BLK, BLK), F32), ((G, BLK), F32),
                  ((1, d.attn_w), F32), ((1, d.gate_w), F32)]
    out_specs = [row(d.in_w)] + [acc2(s) for s, _ in out_shapes[1:7]] + \
                [pl.BlockSpec((G, BLK, BLK), lambda i: (0, 0, 0))] + [acc2(s) for s, _ in out_shapes[8:]]
    return pl.pallas_call(
        functools.partial(_mixer_bwd_body, d),
        name="mixer_bwd",
        grid=(T // BLK,),
        in_specs=[pl.BlockSpec(memory_space=pltpu.SMEM), row(d.in_w), kv_prev, row(d.attn_w), row(d.gate_w),
                  row(d.attn_w + d.gate_w), const2(qg2), const2(kg2), const2(lg), const2(lb), const3(wsp),
                  const3(bcol), const2(goa), const2(gog)],
        out_specs=out_specs,
        out_shape=[jax.ShapeDtypeStruct(s, t) for s, t in out_shapes],
        compiler_params=_params(("arbitrary",)),
    )(sinks, proj, proj, ya, yg, dy, qg2, kg2, lg, lb, wsp, bcol, goa, gog)


def _put_kv_body(dkv_ref, dproj_in_ref, dproj_ref):
    del dproj_in_ref
    dproj_ref[...] = dkv_ref[...].astype(dproj_ref.dtype)


def _put_kv(d, dproj, dkv):
    T = dproj.shape[0]
    tr = _pick(T, 1024, 16)
    return pl.pallas_call(
        functools.partial(_put_kv_body),
        name="put_kv",
        grid=(T // tr,),
        in_specs=[pl.BlockSpec((tr, 2 * BLK), lambda i: (i, 0)), pl.BlockSpec(memory_space=pl.ANY)],
        out_specs=pl.BlockSpec((tr, 2 * BLK), lambda i: (i, d.kv_col)),
        out_shape=jax.ShapeDtypeStruct(dproj.shape, dproj.dtype),
        input_output_aliases={1: 0},
        compiler_params=_params(("parallel",)),
    )(dkv, dproj)


def _add_pair_body(pc_ref, own_ref, got_ref, o_ref):
    del pc_ref
    o_ref[...] = (own_ref[...].astype(F32) + got_ref[...].astype(F32)).astype(o_ref.dtype)


def _add_pair(g4, got, pc, name):
    n, _, h, C = g4.shape
    tr = _pick(h, 512, 16)
    return pl.pallas_call(
        functools.partial(_add_pair_body),
        name=name,
        grid_spec=pltpu.PrefetchScalarGridSpec(
            num_scalar_prefetch=1,
            grid=(n, h // tr),
            in_specs=[pl.BlockSpec((None, None, tr, C), lambda q, i, pc: (q, pc[1], i, 0)),
                      pl.BlockSpec((None, tr, C), lambda q, i, pc: (q, i, 0))],
            out_specs=pl.BlockSpec((None, tr, C), lambda q, i, pc: (q, i, 0)),
        ),
        out_shape=jax.ShapeDtypeStruct((n, h, C), g4.dtype),
        compiler_params=_params(("parallel", "parallel")),
    )(pc, g4, got)


def _adamw_update(w, g, m, v):
    m = ADAM_B1 * m + (1.0 - ADAM_B1) * g
    v = ADAM_B2 * v + (1.0 - ADAM_B2) * (g * g)
    m_hat = m / (1.0 - ADAM_B1 ** ADAM_STEP)
    v_hat = v / (1.0 - ADAM_B2 ** ADAM_STEP)
    return -ADAM_LR * (m_hat / (jnp.sqrt(v_hat) + ADAM_EPS) + ADAM_WD * w), m, v


def _adamw_body(w_ref, g_ref, m_ref, v_ref, d_ref, nm_ref, nv_ref):
    d_ref[...], nm_ref[...], nv_ref[...] = _adamw_update(w_ref[...], g_ref[...], m_ref[...], v_ref[...])


def _adamw(w, g, m, v, name):
    R, C = w.shape
    tr = _pick(R, 512, 8)
    blk = pl.BlockSpec((tr, C), lambda i: (i, 0))
    return pl.pallas_call(
        functools.partial(_adamw_body),
        name=name,
        grid=(R // tr,),
        in_specs=[blk] * 4,
        out_specs=[blk] * 3,
        out_shape=[jax.ShapeDtypeStruct((R, C), F32)] * 3,
        compiler_params=_params(("parallel",)),
    )(w, g, m, v)


def _adamw_halves_body(pc_ref, w_ref, own_ref, got_ref, m_ref, v_ref, g_ref, d_ref, nm_ref, nv_ref):
    mine = pl.program_id(0) == pc_ref[1]

    def update(g):
        g_ref[...] = g
        d_ref[...], nm_ref[...], nv_ref[...] = _adamw_update(w_ref[...], g, m_ref[...], v_ref[...])

    @pl.when(mine)
    def _():
        update(own_ref[...].astype(F32))

    @pl.when(jnp.logical_not(mine))
    def _():
        update(got_ref[...].astype(F32))


def _adamw_halves(w, own, got, m, v, pc, name):
    h, C = own.shape
    tr = _pick(h, 512, 8)
    full = pl.BlockSpec((None, tr, C), lambda hh, i, pc: (hh, i, 0))
    mine = pl.BlockSpec((tr, C), lambda hh, i, pc: (jnp.where(hh == pc[1], i, 0), 0))
    theirs = pl.BlockSpec((tr, C), lambda hh, i, pc: (jnp.where(hh == pc[1], 0, i), 0))
    return pl.pallas_call(
        functools.partial(_adamw_halves_body),
        name=name,
        grid_spec=pltpu.PrefetchScalarGridSpec(
            num_scalar_prefetch=1,
            grid=(2, h // tr),
            in_specs=[full, mine, theirs, full, full],
            out_specs=[full] * 4,
        ),
        out_shape=[jax.ShapeDtypeStruct((2, h, C), F32)] * 4,
        compiler_params=_params(("parallel", "parallel")),
    )(pc, w.reshape(2, h, C), own, got, m.reshape(2, h, C), v.reshape(2, h, C))


def _me():
    x, y, c = lax.axis_index("x"), lax.axis_index("y"), lax.axis_index("c")
    chips = [(1 - x, y), (x, 1 - y), (1 - x, 1 - y)]
    return x, y, c, chips


def _cast_into_body(pc_ref, w_ref, o_ref):
    del pc_ref
    o_ref[...] = w_ref[...].astype(o_ref.dtype)


def _cast_into(w, pc, name, side_by_side=False):
    Rs, C = w.shape
    h = Rs // 2
    tr = _pick(h, 512, 16)
    if side_by_side:
        out_spec = pl.BlockSpec((None, tr, C), lambda hh, i, pc: (hh, i, pc[0]))
        out_shape = jax.ShapeDtypeStruct((2, h, N_CHIPS * C), BF16)
    else:
        out_spec = pl.BlockSpec((None, None, tr, C), lambda hh, i, pc: (pc[0], hh, i, 0))
        out_shape = jax.ShapeDtypeStruct((N_CHIPS, 2, h, C), BF16)
    return pl.pallas_call(
        functools.partial(_cast_into_body),
        name=name,
        grid_spec=pltpu.PrefetchScalarGridSpec(
            num_scalar_prefetch=1,
            grid=(2, h // tr),
            in_specs=[pl.BlockSpec((None, tr, C), lambda hh, i, pc: (hh, i, 0))],
            out_specs=out_spec,
        ),
        out_shape=out_shape,
        compiler_params=_params(("parallel", "parallel")),
    )(pc, w.reshape(2, h, C))


MAX_PIECES = 4


def _send_tile_to_sibling(src_of, dst_of, tr, dst_total, send_sems, recv_sem, last):
    x, y, c, _ = _me()
    pieces = MAX_PIECES if tr % (16 * MAX_PIECES) == 0 else (2 if tr % 32 == 0 else 1)
    n = tr // pieces
    copies = [pltpu.make_async_remote_copy(src_ref=src_of(k * n, n), dst_ref=dst_of(k * n, n), send_sem=send_sems.at[k],
                                           recv_sem=recv_sem, device_id=(x, y, 1 - c), device_id_type=MESH)
              for k in range(pieces)]
    for cp in copies:
        cp.start()
    for cp in copies:
        cp.wait_send()

    @pl.when(last)
    def _():
        pltpu.make_async_remote_copy(src_ref=dst_total, dst_ref=dst_total, send_sem=send_sems.at[0], recv_sem=recv_sem,
                                     device_id=(x, y, 1 - c), device_id_type=MESH).wait_recv()


TILE_SEMS = [pltpu.SemaphoreType.DMA((MAX_PIECES,)), pltpu.SemaphoreType.DMA(())]


def _ag_pair_body(tr, n_i, pc_ref, tile_ref, buf_ref, send_sem, recv_sem):
    j, i = pl.program_id(0), pl.program_id(1)
    q = pc_ref[0] ^ (j + 1)
    c = pc_ref[1]
    r_tile = pl.multiple_of(i * tr, tr)
    last = jnp.logical_and(j == N_CHIPS - 2, i == n_i - 1)
    if len(buf_ref.shape) == 4:
        _send_tile_to_sibling(lambda r0, n: tile_ref.at[:, :, pl.ds(r0, n)],
                              lambda r0, n: buf_ref.at[pl.ds(q, 1), pl.ds(c, 1), pl.ds(r_tile + r0, n)], tr,
                              buf_ref.at[pl.ds(0, N_CHIPS - 1), 0], send_sem, recv_sem, last)
    else:
        cs = buf_ref.shape[2] // N_CHIPS
        cols = pl.ds(pl.multiple_of(q * cs, BLK), cs)
        _send_tile_to_sibling(lambda r0, n: tile_ref.at[:, pl.ds(r0, n)],
                              lambda r0, n: buf_ref.at[pl.ds(c, 1), pl.ds(r_tile + r0, n), cols], tr,
                              buf_ref.at[0, :, pl.ds(0, (N_CHIPS - 1) * cs)], send_sem, recv_sem, last)


def _ag_pair(buf, pc, name):
    if len(buf.shape) == 4:
        _, _, h, C = buf.shape
        tile = lambda tr: pl.BlockSpec((1, 1, tr, C), lambda j, i, pc: (pc[0] ^ (j + 1), pc[1], i, 0))
    else:
        _, h, C = buf.shape
        tile = lambda tr: pl.BlockSpec((1, tr, C // N_CHIPS), lambda j, i, pc: (pc[1], i, pc[0] ^ (j + 1)))
    tr = _pick(h, 512, 16)
    return pl.pallas_call(
        functools.partial(_ag_pair_body, tr, h // tr),
        name=name,
        grid_spec=pltpu.PrefetchScalarGridSpec(
            num_scalar_prefetch=1,
            grid=(N_CHIPS - 1, h // tr),
            in_specs=[tile(tr)],
            out_specs=HBM,
            scratch_shapes=TILE_SEMS,
        ),
        out_shape=jax.ShapeDtypeStruct(buf.shape, buf.dtype),
        input_output_aliases={1: 0},
        compiler_params=_params(("arbitrary", "arbitrary")),
    )(pc, buf)


def _swap_halves_body(tr, n_q, n_i, pc_ref, tile_ref, got_ref, send_sem, recv_sem):
    del pc_ref
    q, i = pl.program_id(0), pl.program_id(1)
    r_tile = pl.multiple_of(i * tr, tr)
    _send_tile_to_sibling(lambda r0, n: tile_ref.at[:, :, pl.ds(r0, n)],
                          lambda r0, n: got_ref.at[pl.ds(q, 1), :, pl.ds(r_tile + r0, n)], tr, got_ref, send_sem, recv_sem,
                          jnp.logical_and(q == n_q - 1, i == n_i - 1))


def _swap_halves(g4, pc, name):
    n, _, h, C = g4.shape
    tr = _pick(h, 512, 16)
    return pl.pallas_call(
        functools.partial(_swap_halves_body, tr, n, h // tr),
        name=name,
        grid_spec=pltpu.PrefetchScalarGridSpec(
            num_scalar_prefetch=1,
            grid=(n, h // tr),
            in_specs=[pl.BlockSpec((1, 1, tr, C), lambda q, i, pc: (q, 1 - pc[1], i, 0))],
            out_specs=HBM,
            scratch_shapes=TILE_SEMS,
        ),
        out_shape=jax.ShapeDtypeStruct((n, 1, h, C), g4.dtype),
        compiler_params=_params(("arbitrary", "arbitrary")),
    )(pc, g4).reshape(n, h, C)


def _ici_copy(src, dst, send_sems, recv_sems, j, chip, c):
    return pltpu.make_async_remote_copy(src_ref=src, dst_ref=dst, send_sem=send_sems.at[j], recv_sem=recv_sems.at[j],
                                        device_id=(chip[0], chip[1], c), device_id_type=MESH)


def _token_spec():
    return jax.ShapeDtypeStruct((8, BLK), F32), pl.BlockSpec(memory_space=pltpu.VMEM)


def _slab(buf_ref, q, c):
    if len(buf_ref.shape) == 4:
        return buf_ref.at[q, c]
    cs = buf_ref.shape[2] // N_CHIPS
    return buf_ref.at[c, :, pl.ds(pl.multiple_of(q * cs, BLK), cs)]


def _ag_start_body(buf_ref, after_ref, send_sems, recv_sems, buf_thru, token_ref):
    del after_ref, buf_thru
    x, y, c, chips = _me()
    mine = _slab(buf_ref, 2 * x + y, c)
    for j, chip in enumerate(chips):
        _ici_copy(mine, mine, send_sems, recv_sems, j, chip, c).start()
    token_ref[...] = jnp.zeros_like(token_ref)


def _ag_start(buf, after, name):
    tok_shape, tok_spec = _token_spec()
    sems = pltpu.SemaphoreType.DMA((N_CHIPS - 1,))
    return pl.pallas_call(
        functools.partial(_ag_start_body),
        name=name,
        in_specs=[HBM, ANY],
        out_specs=[SEM, SEM, HBM, tok_spec],
        out_shape=[sems, sems, pltpu.HBM(buf.shape, buf.dtype), tok_shape],
        input_output_aliases={0: 2},
        compiler_params=pltpu.CompilerParams(has_side_effects=EFFECT),
    )(pltpu.with_memory_space_constraint(buf, pltpu.HBM), after)


def _ag_wait_body(buf_ref, send_sems, recv_sems, after_ref, buf_out):
    del after_ref, buf_out
    x, y, c, chips = _me()
    mine = _slab(buf_ref, 2 * x + y, c)
    for j, chip in enumerate(chips):
        theirs = _slab(buf_ref, 2 * chip[0] + chip[1], c)
        _ici_copy(mine, mine, send_sems, recv_sems, j, chip, c).wait_send()
        _ici_copy(theirs, theirs, send_sems, recv_sems, j, chip, c).wait_recv()


def _ag_wait(buf, send_sems, recv_sems, after, name):
    return pl.pallas_call(
        functools.partial(_ag_wait_body),
        name=name,
        in_specs=[HBM, SEM, SEM, ANY],
        out_specs=HBM,
        out_shape=pltpu.HBM(buf.shape, buf.dtype),
        input_output_aliases={0: 0},
        compiler_params=pltpu.CompilerParams(has_side_effects=EFFECT),
    )(buf, send_sems, recv_sems, after)


def _rs_start_body(pair_ref, land_ref, after_ref, send_sems, recv_sems, pair_thru, land_thru, token_ref):
    del after_ref, pair_thru, land_thru
    x, y, c, chips = _me()
    for j, chip in enumerate(chips):
        _ici_copy(pair_ref.at[2 * chip[0] + chip[1]], land_ref.at[j], send_sems, recv_sems, j, chip, c).start()
    token_ref[...] = jnp.zeros_like(token_ref)


def _rs_start(pair, after, name):
    n, h, C = pair.shape
    tok_shape, tok_spec = _token_spec()
    sems = pltpu.SemaphoreType.DMA((N_CHIPS - 1,))
    land = pltpu.with_memory_space_constraint(lax.empty((N_CHIPS - 1, h, C), pair.dtype), pltpu.HBM)
    return pl.pallas_call(
        functools.partial(_rs_start_body),
        name=name,
        in_specs=[HBM, HBM, ANY],
        out_specs=[SEM, SEM, HBM, HBM, tok_spec],
        out_shape=[sems, sems, pltpu.HBM(pair.shape, pair.dtype), pltpu.HBM(land.shape, land.dtype), tok_shape],
        input_output_aliases={0: 2, 1: 3},
        compiler_params=pltpu.CompilerParams(has_side_effects=EFFECT),
    )(pltpu.with_memory_space_constraint(pair, pltpu.HBM), land, after)


def _rs_wait_body(pair_ref, land_ref, send_sems, recv_sems, after_ref, pair_out, land_out):
    del after_ref, pair_out, land_out
    x, y, c, chips = _me()
    for j, chip in enumerate(chips):
        _ici_copy(pair_ref.at[0], land_ref.at[j], send_sems, recv_sems, j, chip, c).wait_send()
        _ici_copy(pair_ref.at[0], land_ref.at[j], send_sems, recv_sems, j, chip, c).wait_recv()


def _rs_wait(pair, land, send_sems, recv_sems, after, name):
    return pl.pallas_call(
        functools.partial(_rs_wait_body),
        name=name,
        in_specs=[HBM, HBM, SEM, SEM, ANY],
        out_specs=[HBM, HBM],
        out_shape=[pltpu.HBM(pair.shape, pair.dtype), pltpu.HBM(land.shape, land.dtype)],
        input_output_aliases={0: 0, 1: 1},
        compiler_params=pltpu.CompilerParams(has_side_effects=EFFECT),
    )(pair, land, send_sems, recv_sems, after)


def _swap_copy(g4_ref, got_ref, send_sem, recv_sem):
    x, y, c, _ = _me()
    return pltpu.make_async_remote_copy(src_ref=g4_ref.at[:, 1 - c], dst_ref=got_ref, send_sem=send_sem,
                                        recv_sem=recv_sem, device_id=(x, y, 1 - c), device_id_type=MESH)


def _swap_start_body(g4_ref, got_ref, send_sem, recv_sem, g4_thru, got_thru, token_ref):
    del g4_thru, got_thru
    _swap_copy(g4_ref, got_ref, send_sem, recv_sem).start()
    token_ref[...] = jnp.zeros_like(token_ref)


def _swap_start(g4, name):
    n, _, h, C = g4.shape
    tok_shape, tok_spec = _token_spec()
    sem = pltpu.SemaphoreType.DMA(())
    got = pltpu.with_memory_space_constraint(lax.empty((n, h, C), g4.dtype), pltpu.HBM)
    return pl.pallas_call(
        functools.partial(_swap_start_body),
        name=name,
        in_specs=[HBM, HBM],
        out_specs=[SEM, SEM, HBM, HBM, tok_spec],
        out_shape=[sem, sem, pltpu.HBM(g4.shape, g4.dtype), pltpu.HBM(got.shape, got.dtype), tok_shape],
        input_output_aliases={0: 2, 1: 3},
        compiler_params=pltpu.CompilerParams(has_side_effects=EFFECT),
    )(pltpu.with_memory_space_constraint(g4, pltpu.HBM), got)


def _swap_wait_body(g4_ref, got_ref, send_sem, recv_sem, after_ref, g4_out, got_out):
    del after_ref, g4_out, got_out
    cp = _swap_copy(g4_ref, got_ref, send_sem, recv_sem)
    cp.wait_send()
    cp.wait_recv()


def _swap_wait(g4, got, send_sem, recv_sem, after, name):
    return pl.pallas_call(
        functools.partial(_swap_wait_body),
        name=name,
        in_specs=[HBM, HBM, SEM, SEM, ANY],
        out_specs=[HBM, HBM],
        out_shape=[pltpu.HBM(g4.shape, g4.dtype), pltpu.HBM(got.shape, got.dtype)],
        input_output_aliases={0: 0, 1: 1},
        compiler_params=pltpu.CompilerParams(has_side_effects=EFFECT),
    )(g4, got, send_sem, recv_sem, after)


def _add_chips_body(tr, n_i, pc_ref, own_ref, l0_ref, l1_ref, l2_ref, o_ref, got_ref, send_sems, recv_sem):
    del pc_ref
    i = pl.program_id(0)
    r = own_ref[...].astype(F32) + l0_ref[...].astype(F32)
    o_ref[...] = (r + l1_ref[...].astype(F32) + l2_ref[...].astype(F32)).astype(o_ref.dtype)
    r_tile = pl.multiple_of(i * tr, tr)
    _send_tile_to_sibling(lambda r0, n: o_ref.at[pl.ds(r0, n)], lambda r0, n: got_ref.at[pl.ds(r_tile + r0, n)], tr,
                          got_ref, send_sems, recv_sem, i == n_i - 1)


def _add_chips(pair, land, pc, name):
    _, h, C = pair.shape
    tr = _pick(h, 256, 16)
    slot = lambda j: pl.BlockSpec((None, tr, C), lambda i, pc: (j, i, 0))
    return pl.pallas_call(
        functools.partial(_add_chips_body, tr, h // tr),
        name=name,
        grid_spec=pltpu.PrefetchScalarGridSpec(
            num_scalar_prefetch=1,
            grid=(h // tr,),
            in_specs=[pl.BlockSpec((None, tr, C), lambda i, pc: (pc[0], i, 0)), slot(0), slot(1), slot(2)],
            out_specs=[pl.BlockSpec((tr, C), lambda i, pc: (i, 0)), HBM],
            scratch_shapes=TILE_SEMS,
        ),
        out_shape=[jax.ShapeDtypeStruct((h, C), pair.dtype), jax.ShapeDtypeStruct((h, C), pair.dtype)],
        compiler_params=_params(("arbitrary",)),
    )(pc, pair, land, land, land)


def _peer(r):
    x, y, c, _ = _me()
    return (x ^ ((r >> 2) & 1), y ^ ((r >> 1) & 1), c ^ (r & 1))


def _ar_start_body(x_ref, land_ref, send_sems, recv_sems, x_thru, land_thru, token_ref):
    del x_thru, land_thru
    for r in range(1, N_DEV):
        pltpu.make_async_remote_copy(src_ref=x_ref, dst_ref=land_ref.at[r - 1], send_sem=send_sems.at[r - 1],
                                     recv_sem=recv_sems.at[r - 1], device_id=_peer(r), device_id_type=MESH).start()
    token_ref[...] = jnp.zeros_like(token_ref)


def _ar_start(packed):
    tok_shape, tok_spec = _token_spec()
    sems = pltpu.SemaphoreType.DMA((N_DEV - 1,))
    land = pltpu.with_memory_space_constraint(lax.empty((N_DEV - 1,) + packed.shape, packed.dtype), pltpu.HBM)
    return pl.pallas_call(
        functools.partial(_ar_start_body),
        name="ar_start",
        in_specs=[HBM, HBM],
        out_specs=[SEM, SEM, HBM, HBM, tok_spec],
        out_shape=[sems, sems, pltpu.HBM(packed.shape, packed.dtype), pltpu.HBM(land.shape, land.dtype), tok_shape],
        input_output_aliases={0: 2, 1: 3},
        compiler_params=pltpu.CompilerParams(has_side_effects=EFFECT),
    )(pltpu.with_memory_space_constraint(packed, pltpu.HBM), land)


def _ar_wait_body(x_ref, land_ref, send_sems, recv_sems, after_ref, x_out, land_out):
    del after_ref, x_out, land_out
    for r in range(1, N_DEV):
        cp = pltpu.make_async_remote_copy(src_ref=x_ref, dst_ref=land_ref.at[r - 1], send_sem=send_sems.at[r - 1],
                                          recv_sem=recv_sems.at[r - 1], device_id=_peer(r), device_id_type=MESH)
        cp.wait_send()
        cp.wait_recv()


def _ar_wait(packed, land, send_sems, recv_sems, after):
    return pl.pallas_call(
        functools.partial(_ar_wait_body),
        name="ar_wait",
        in_specs=[HBM, HBM, SEM, SEM, ANY],
        out_specs=[HBM, HBM],
        out_shape=[pltpu.HBM(packed.shape, packed.dtype), pltpu.HBM(land.shape, land.dtype)],
        input_output_aliases={0: 0, 1: 1},
        compiler_params=pltpu.CompilerParams(has_side_effects=EFFECT),
    )(packed, land, send_sems, recv_sems, after)


def _ar_sum_body(me_ref, own_ref, *rest):
    o_ref = rest[N_DEV]
    acc = None
    for dev in range(N_DEV):
        term = jnp.where(me_ref[0] == dev, own_ref[...], rest[dev][...])
        acc = term if acc is None else acc + term
    o_ref[...] = acc


def _ar_sum(packed, land, me):
    R, C = packed.shape
    tr = _pick(R, 552, 8)
    own = pl.BlockSpec((tr, C), lambda i, me: (i, 0))
    slot = lambda dev: pl.BlockSpec((None, tr, C), lambda i, me: (jnp.maximum((dev ^ me[0]) - 1, 0), i, 0))
    return pl.pallas_call(
        functools.partial(_ar_sum_body),
        name="ar_sum",
        grid_spec=pltpu.PrefetchScalarGridSpec(
            num_scalar_prefetch=1,
            grid=(R // tr,),
            in_specs=[own] + [slot(dev) for dev in range(N_DEV)],
            out_specs=pl.BlockSpec((tr, C), lambda i, me: (i, 0)),
        ),
        out_shape=jax.ShapeDtypeStruct((R, C), F32),
        compiler_params=_params(("parallel",)),
    )(me, packed, *([land] * N_DEV))


def _pack(arrays):
    rows = []
    for a in arrays:
        flat = a.reshape(-1).astype(F32)
        pad = (-flat.shape[0]) % BLK
        rows.append(jnp.pad(flat, (0, pad)).reshape(-1, BLK))
    packed = jnp.concatenate(rows, axis=0)
    pad = (-packed.shape[0]) % 8
    return jnp.pad(packed, ((0, pad), (0, 0)))


def _unpack(packed, shapes):
    out, r = [], 0
    for s in shapes:
        n = 1
        for k in s:
            n *= k
        nr = -(-n // BLK)
        out.append(packed[r:r + nr].reshape(-1)[:n].reshape(s))
        r += nr
    return out


def kernel(x, norm1_g, w_in, q_norm_g, k_norm_g, attn_sinks, gate_ln_g, gate_ln_b, w_spatial, b_spatial, out_norm_attn_g, out_norm_gate_g, w_out, norm2_g, w_ffn_gate, w_ffn_up, w_ffn_down, loss_target, m_norm1_g, m_w_in, m_q_norm_g, m_k_norm_g, m_attn_sinks, m_gate_ln_g, m_gate_ln_b, m_w_spatial, m_b_spatial, m_out_norm_attn_g, m_out_norm_gate_g, m_w_out, m_norm2_g, m_w_ffn_gate, m_w_ffn_up, m_w_ffn_down, v_norm1_g, v_w_in, v_q_norm_g, v_k_norm_g, v_attn_sinks, v_gate_ln_g, v_gate_ln_b, v_w_spatial, v_b_spatial, v_out_norm_attn_g, v_out_norm_gate_g, v_w_out, v_norm2_g, v_w_ffn_gate, v_w_ffn_up, v_w_ffn_down):
    bl, seq, D = x.shape
    T = bl * seq
    attn_w, gate_w = out_norm_attn_g.shape[1], out_norm_gate_g.shape[1]
    d = _Dims(seq, attn_w, gate_w)
    G = d.n_groups
    in_w = d.in_w
    dff = w_ffn_gate.shape[2] * N_CHIPS
    assert w_in.shape[2] * N_CHIPS == in_w and seq % BLK == 0 and attn_w % (2 * BLK) == 0

    pc = jnp.stack([2 * lax.axis_index("x") + lax.axis_index("y"), lax.axis_index("c")]).astype(jnp.int32)
    big = [w_in[0], w_out[0], w_ffn_gate[0], w_ffn_up[0], w_ffn_down[0]]
    names = ["in", "out", "gate", "up", "down"]
    xf = x.reshape(T, D)
    tgt = loss_target.reshape(T, D)
    send, recv, buf, behind = _ag_start(_cast_into(big[0], pc, "cast_in"), norm1_g, "ag_start_in")
    started = [(send, recv, buf)]
    h1 = _rms_fwd(xf, norm1_g, "norm1_fwd", after=behind)
    behind = h1
    for w, n in zip(big[1:], names[1:]):
        buf = _cast_into(w, pc, "cast_" + n, side_by_side=n in ("gate", "up"))
        send, recv, buf, behind = _ag_start(buf, behind, "ag_start_" + n)
        started.append((send, recv, buf))

    def gathered(k, after):
        send, recv, buf = started[k]
        buf = _ag_wait(buf, send, recv, after, "ag_wait_" + names[k])
        buf = _ag_pair(buf, pc, "ag_pair_" + names[k])
        rs, cs = big[k].shape
        return buf.reshape(rs, N_CHIPS * cs) if len(buf.shape) == 3 else buf.reshape(N_CHIPS, rs, cs)

    qg2 = jnp.tile(q_norm_g, (1, 2))
    kg2 = jnp.tile(k_norm_g, (1, 2))
    lg, lb, wsp = gate_ln_g[0], gate_ln_b[0], w_spatial[0]
    bcol = jnp.broadcast_to(b_spatial[0][:, :, None], (G, BLK, BLK))

    win_full = jnp.transpose(gathered(0, behind), (1, 0, 2)).reshape(D, in_w)
    proj = _matmul(h1, win_full, "nn", F32, "proj_fwd", tm=1024, tn=1664)
    ya, yg, yn = _mixer_fwd(d, proj, attn_sinks, qg2, kg2, lg, lb, wsp, bcol, out_norm_attn_g, out_norm_gate_g)
    wout_full = gathered(1, yn).reshape(attn_w + gate_w, D)
    x1 = _matmul(yn, wout_full, "nn", F32, "out_fwd", tm=1024, tn=1024, add=xf)
    h2 = _rms_fwd(x1, norm2_g, "norm2_fwd")
    wg_full, wu_full = gathered(2, h2), gathered(3, h2)
    a, b, f = _ffn_up(h2, wg_full, wu_full)
    wd_full = gathered(4, f).reshape(dff, D)
    dx2, dx2b, loss_local = _ffn_down_loss(f, wd_full, x1, tgt)

    def swap_start(g, n):
        g4 = g.reshape(N_CHIPS, 2, g.shape[1] // 2, g.shape[2])
        return _swap_start(g4, "rs_swap_start_" + n)

    def reduce_start(swapping, n, after):
        send, recv, g4, got, _ = swapping
        g4, got = _swap_wait(g4, got, send, recv, after, "rs_swap_wait_" + n)
        return _rs_start(_add_pair(g4, got, pc, "rs_add_pair_" + n), got, "rs_start_" + n)

    reducing = {}
    g_d = _matmul(f, dx2b, "tn", BF16, "ffn_down_dw", tm=1408, tn=1024, tk=2048, out_slab="r")
    swap_d = swap_start(g_d, "down")
    da, db = _ffn_down_dx(dx2b, wd_full, a, b, swap_d[4])
    g_g = _matmul(h2, da, "tn", BF16, "ffn_gate_dw", tm=1024, tn=1408, tk=2048, out_slab="c")
    swap_g = swap_start(g_g, "gate")
    reducing["down"] = reduce_start(swap_d, "down", swap_g[4])
    g_u = _matmul(h2, db, "tn", BF16, "ffn_up_dw", tm=1024, tn=1408, tk=2048, out_slab="c",
                  after=reducing["down"][4])
    swap_u = swap_start(g_u, "up")
    reducing["gate"] = reduce_start(swap_g, "gate", swap_u[4])
    dh2 = _matmul(da, wg_full, "nt", F32, "ffn_gate_dx", tm=1024, tn=1024, tk=2816, after=reducing["gate"][4])
    dh2 = _matmul(db, wu_full, "nt", F32, "ffn_up_dx", tm=1024, tn=1024, tk=2816, add=dh2)
    reducing["up"] = reduce_start(swap_u, "up", dh2)
    dx1, dx1b, dg_norm2 = _rms_bwd(x1, norm2_g, dh2, dx2, "norm2_bwd", True)
    g_o = _matmul(yn, dx1b, "tn", BF16, "out_dw", tm=512, tn=1024, tk=2048, out_slab="r",
                  after=reducing["up"][4])
    swap_o = swap_start(g_o, "out")
    dy = _matmul(dx1b, wout_full, "nt", F32, "out_dx", tm=1024, tn=1024, after=swap_o[4])
    (dproj, dkv, dqg, dkg, dsk, dlg, dlb, dwsp, dbsp, dgoa, dgog) = _mixer_bwd(
        d, proj, ya, yg, dy, attn_sinks, qg2, kg2, lg, lb, wsp, bcol, out_norm_attn_g, out_norm_gate_g)
    dproj = _put_kv(d, dproj, dkv)
    reducing["out"] = reduce_start(swap_o, "out", dproj)
    g_in_full = _matmul(h1, dproj, "tn", BF16, "proj_dw", tm=1024, tn=1664, tk=2048,
                        after=reducing["out"][4])
    g_i = jnp.transpose(g_in_full.reshape(D, N_CHIPS, in_w // N_CHIPS), (1, 0, 2))
    g4_i = g_i.reshape(N_CHIPS, 2, D // 2, in_w // N_CHIPS)
    pair_i = _add_pair(g4_i, _swap_halves(g4_i, pc, "rs_swap_in"), pc, "rs_add_pair_in")
    reducing["in"] = _rs_start(pair_i, g_i, "rs_start_in")
    dh1 = _matmul(dproj, win_full, "nt", F32, "proj_dx", tm=1024, tn=1024, after=reducing["in"][4])
    dx, dg_norm1 = _rms_bwd(xf, norm1_g, dh1, dx1, "norm1_bwd", False)

    dqg64 = dqg[:, :HEAD_DIM] + dqg[:, HEAD_DIM:]
    dkg64 = dkg[:, :HEAD_DIM] + dkg[:, HEAD_DIM:]
    small_g_local = [dg_norm1, dqg64, dkg64, dsk[:, :d.n_heads], dlg, dlb, dwsp, dbsp, dgoa, dgog, dg_norm2,
                     loss_local]
    ar_send, ar_recv, ar_own, ar_land, ar_token = _ar_start(_pack(small_g_local))

    big_m = [m_w_in[0], m_w_out[0], m_w_ffn_gate[0], m_w_ffn_up[0], m_w_ffn_down[0]]
    big_v = [v_w_in[0], v_w_out[0], v_w_ffn_gate[0], v_w_ffn_up[0], v_w_ffn_down[0]]
    big_grads, big_d, big_nm, big_nv = [], [], [], []
    for w, m, v, n in zip(big, big_m, big_v, names):
        send, recv, pair, land, _ = reducing[n]
        pair, land = _rs_wait(pair, land, send, recv, ar_token, "rs_wait_" + n)
        own, got = _add_chips(pair, land, pc, "rs_add_chips_" + n)
        outs = _adamw_halves(w, own, got, m, v, pc, "adamw_" + n)
        for lst, o in zip((big_grads, big_d, big_nm, big_nv), outs):
            lst.append(o.reshape(w.shape))

    small_names_w = [norm1_g, q_norm_g, k_norm_g, attn_sinks, gate_ln_g, gate_ln_b, w_spatial, b_spatial,
                     out_norm_attn_g, out_norm_gate_g, norm2_g]
    small_m = [m_norm1_g, m_q_norm_g, m_k_norm_g, m_attn_sinks, m_gate_ln_g, m_gate_ln_b, m_w_spatial, m_b_spatial,
               m_out_norm_attn_g, m_out_norm_gate_g, m_norm2_g]
    small_v = [v_norm1_g, v_q_norm_g, v_k_norm_g, v_attn_sinks, v_gate_ln_g, v_gate_ln_b, v_w_spatial, v_b_spatial,
               v_out_norm_attn_g, v_out_norm_gate_g, v_norm2_g]
    shapes = [w.shape for w in small_names_w] + [loss_local.shape]
    ride = [jnp.zeros(loss_local.shape, F32)]
    ar_own, ar_land = _ar_wait(ar_own, ar_land, ar_send, ar_recv, big_nv[-1])
    me = (4 * lax.axis_index("x") + 2 * lax.axis_index("y") + lax.axis_index("c")).astype(jnp.int32).reshape(1)
    sg = _ar_sum(ar_own, ar_land, me)
    sd, snm, snv = _adamw(_pack(small_names_w + ride), sg, _pack(small_m + ride), _pack(small_v + ride), "adamw_small")
    small_g, small_d, small_nm, small_nv = (_unpack(t, shapes) for t in (sg, sd, snm, snv))
    loss = small_g[-1][0, 0]

    def order(small, bigs):
        s = list(small)
        bg = [t[None] for t in bigs]
        return [s[0], bg[0], s[1], s[2], s[3], s[4], s[5], s[6], s[7], s[8], s[9], bg[1], s[10], bg[2], bg[3], bg[4]]

    grad_x = dx.reshape(bl, seq, D)
    return (loss, grad_x, *order(small_g, big_grads), *order(small_d, big_d), *order(small_nm, big_nm),
            *order(small_nv, big_nv))
```

```python
import functools

import jax
import jax.numpy as jnp
from jax import lax
from jax.experimental import pallas as pl
from jax.experimental.pallas import tpu as pltpu

F32 = jnp.float32
BF16 = jnp.bfloat16
MESH = pl.DeviceIdType.MESH

EPS = 1e-6
HEAD_DIM = 64
N_KV_HEADS = 2
BLK = 128
N_CHIPS = 4
N_DEV = 8
NEG = -1e30

ADAM_LR = 0.001
ADAM_B1 = 0.9
ADAM_B2 = 0.999
ADAM_EPS = 1e-08
ADAM_WD = 0.01
ADAM_STEP = 10

VMEM_LIMIT = 56 * 1024 * 1024

NN = (((1,), (0,)), ((), ()))
NT = (((1,), (1,)), ((), ()))
TN = (((0,), (0,)), ((), ()))
HBM = pl.BlockSpec(memory_space=pltpu.HBM)
ANY = pl.BlockSpec(memory_space=pl.ANY)
SEM = pl.BlockSpec(memory_space=pltpu.SEMAPHORE)
EFFECT = pltpu.SideEffectType.DATAFLOW_SIDE_EFFECTING


def _dot(a, b, dn):
    return lax.dot_general(a, b, dn, preferred_element_type=F32)


def _pick(dim, pref, align=128):
    if dim <= pref:
        return dim
    t = (pref // align) * align
    while t >= align:
        if dim % t == 0:
            return t
        t -= align
    return dim


def _params(sem):
    return pltpu.CompilerParams(dimension_semantics=sem, vmem_limit_bytes=VMEM_LIMIT)


MM_CHUNK = 512


def _col_chunks(tn):
    return [slice(c0, min(c0 + MM_CHUNK, tn)) for c0 in range(0, tn, MM_CHUNK)]


def _mm_body(dn, nk, has_add, has_after, *refs):
    a_ref, b_ref = refs[:2]
    add_ref = refs[2] if has_add else None
    o_ref = refs[2 + has_add + has_after]
    chunks = _col_chunks(o_ref.shape[-1])

    def dot(cols):
        return _dot(a_ref[...], b_ref[cols, :] if dn == NT else b_ref[:, cols], dn)

    def finish(cols, r):
        if add_ref is not None:
            r = r + add_ref[:, cols]
        o_ref[:, cols] = r.astype(o_ref.dtype)

    if nk == 1:
        for cols in chunks:
            finish(cols, dot(cols))
        return
    acc_ref = refs[-1]
    k = pl.program_id(2)

    @pl.when(k == 0)
    def _():
        for cols in chunks:
            acc_ref[:, cols] = dot(cols)

    if nk > 2:
        @pl.when(jnp.logical_and(k > 0, k < nk - 1))
        def _():
            for cols in chunks:
                acc_ref[:, cols] += dot(cols)

    @pl.when(k == nk - 1)
    def _():
        for cols in chunks:
            finish(cols, acc_ref[:, cols] + dot(cols))


def _matmul(a, b, mode, out_dtype, name, *, tm, tn, tk=None, add=None, out_slab=None, after=None):
    if mode == "nn":
        (M, K), N = a.shape, b.shape[1]
    elif mode == "nt":
        (M, K), N = a.shape, b.shape[0]
    else:
        (K, M), N = a.shape, b.shape[1]
    tk = K if tk is None else tk
    tm, tn, tk = _pick(M, tm), _pick(N, tn), _pick(K, tk)
    if out_slab == "c":
        tn = _pick(N // N_CHIPS, tn)
    if out_slab == "r":
        tm = _pick(M // N_CHIPS, tm)
    gm, gn, gk = M // tm, N // tn, K // tk

    if mode == "tn":
        a_spec = pl.BlockSpec((tk, tm), lambda j, i, k: (k, i))
        b_spec = pl.BlockSpec((tk, tn), lambda j, i, k: (k, j))
    else:
        a_spec = pl.BlockSpec((tm, tk), lambda j, i, k: (i, k))
        if mode == "nn":
            b_spec = pl.BlockSpec((tk, tn), lambda j, i, k: (k, j))
        else:
            b_spec = pl.BlockSpec((tn, tk), lambda j, i, k: (j, k))

    if out_slab == "c":
        per = (N // N_CHIPS) // tn
        o_spec = pl.BlockSpec((None, tm, tn), lambda j, i, k: (j // per, i, j % per))
        o_shape = jax.ShapeDtypeStruct((N_CHIPS, M, N // N_CHIPS), out_dtype)
    elif out_slab == "r":
        per = (M // N_CHIPS) // tm
        o_spec = pl.BlockSpec((None, tm, tn), lambda j, i, k: (i // per, i % per, j))
        o_shape = jax.ShapeDtypeStruct((N_CHIPS, M // N_CHIPS, N), out_dtype)
    else:
        o_spec = pl.BlockSpec((tm, tn), lambda j, i, k: (i, j))
        o_shape = jax.ShapeDtypeStruct((M, N), out_dtype)

    dn = {"nn": NN, "nt": NT, "tn": TN}[mode]
    in_specs = [a_spec, b_spec]
    args = [a, b]
    if add is not None:
        in_specs.append(pl.BlockSpec((tm, tn), lambda j, i, k: (i, j)))
        args.append(add)
    if after is not None:
        in_specs.append(ANY)
        args.append(after)
    return pl.pallas_call(
        functools.partial(_mm_body, dn, gk, add is not None, after is not None),
        name=name,
        grid=(gn, gm, gk),
        in_specs=in_specs,
        out_specs=o_spec,
        out_shape=o_shape,
        scratch_shapes=[pltpu.VMEM((tm, tn), F32)] if gk > 1 else [],
        compiler_params=_params(("parallel", "parallel", "arbitrary")),
    )(*args)


def _rms_fwd_body(x_ref, g_ref, *rest):
    h_ref = rest[-1]
    x = x_ref[...]
    r = lax.rsqrt(jnp.mean(x * x, axis=-1, keepdims=True) + EPS)
    h_ref[...] = (x * r * g_ref[...]).astype(h_ref.dtype)


def _rms_fwd(x, g, name, after=None):
    T, D = x.shape
    tr = _pick(T, 256, 16)
    extra = [] if after is None else [after]
    return pl.pallas_call(
        functools.partial(_rms_fwd_body),
        name=name,
        grid=(T // tr,),
        in_specs=[pl.BlockSpec((tr, D), lambda i: (i, 0)), pl.BlockSpec((1, D), lambda i: (0, 0))] + [ANY] * len(extra),
        out_specs=pl.BlockSpec((tr, D), lambda i: (i, 0)),
        out_shape=jax.ShapeDtypeStruct((T, D), BF16),
        compiler_params=_params(("parallel",)),
    )(x, g, *extra)


def _rms_bwd_body(with_bf16, x_ref, g_ref, dh_ref, res_ref, dx_ref, *rest):
    dg_ref = rest[-1]

    @pl.when(pl.program_id(0) == 0)
    def _():
        dg_ref[...] = jnp.zeros_like(dg_ref)

    x = x_ref[...]
    r = lax.rsqrt(jnp.mean(x * x, axis=-1, keepdims=True) + EPS)
    xh = x * r
    dh = dh_ref[...]
    dg_ref[...] += jnp.sum(dh * xh, axis=0, keepdims=True)
    t = dh * g_ref[...]
    dx = res_ref[...] + r * (t - xh * jnp.mean(t * xh, axis=-1, keepdims=True))
    dx_ref[...] = dx
    if with_bf16:
        rest[0][...] = dx.astype(BF16)


def _rms_bwd(x, g, dh, res, name, with_bf16):
    T, D = x.shape
    tr = _pick(T, 256, 16)
    row = pl.BlockSpec((tr, D), lambda i: (i, 0))
    vec = pl.BlockSpec((1, D), lambda i: (0, 0))
    extra = [jax.ShapeDtypeStruct((T, D), BF16)] if with_bf16 else []
    return pl.pallas_call(
        functools.partial(_rms_bwd_body, with_bf16),
        name=name,
        grid=(T // tr,),
        in_specs=[row, vec, row, row],
        out_specs=[row] + [row] * len(extra) + [vec],
        out_shape=[jax.ShapeDtypeStruct((T, D), F32)] + extra + [jax.ShapeDtypeStruct((1, D), F32)],
        compiler_params=_params(("arbitrary",)),
    )(x, g, dh, res)


def _ffn_up_body(h_ref, wg_ref, wu_ref, a_ref, b_ref, f_ref):
    for cols in _col_chunks(a_ref.shape[-1]):
        a = _dot(h_ref[...], wg_ref[:, cols], NN)
        b = _dot(h_ref[...], wu_ref[:, cols], NN)
        a_ref[:, cols] = a
        b_ref[:, cols] = b
        f_ref[:, cols] = (a * (1.0 / (1.0 + jnp.exp(-a))) * b).astype(f_ref.dtype)


def _ffn_up(h, wg, wu):
    T, D = h.shape
    F = wg.shape[1]
    tm, tn = _pick(T, 1024), _pick(F, MM_CHUNK)
    hs = pl.BlockSpec((tm, D), lambda j, i: (i, 0))
    ws = pl.BlockSpec((D, tn), lambda j, i: (0, j))
    os = pl.BlockSpec((tm, tn), lambda j, i: (i, j))
    return pl.pallas_call(
        functools.partial(_ffn_up_body),
        name="ffn_up_fwd",
        grid=(F // tn, T // tm),
        in_specs=[hs, ws, ws],
        out_specs=[os, os, os],
        out_shape=[jax.ShapeDtypeStruct((T, F), F32), jax.ShapeDtypeStruct((T, F), F32),
                   jax.ShapeDtypeStruct((T, F), BF16)],
        compiler_params=_params(("parallel", "parallel")),
    )(h, wg, wu)


def _ffn_down_dx_body(dx_ref, wd_ref, a_ref, b_ref, after_ref, da_ref, db_ref):
    del after_ref
    for cols in _col_chunks(da_ref.shape[-1]):
        df = _dot(dx_ref[...], wd_ref[cols, :], NT)
        a = a_ref[:, cols]
        s = 1.0 / (1.0 + jnp.exp(-a))
        da_ref[:, cols] = (df * b_ref[:, cols] * (s * (1.0 + a * (1.0 - s)))).astype(da_ref.dtype)
        db_ref[:, cols] = (df * (a * s)).astype(db_ref.dtype)


def _ffn_down_dx(dx2b, wd, a, b, after):
    T, D = dx2b.shape
    F = wd.shape[0]
    tm, tn = _pick(T, 512), _pick(F, 1408)
    xs = pl.BlockSpec((tm, D), lambda j, i: (i, 0))
    ws = pl.BlockSpec((tn, D), lambda j, i: (j, 0))
    os = pl.BlockSpec((tm, tn), lambda j, i: (i, j))
    return pl.pallas_call(
        functools.partial(_ffn_down_dx_body),
        name="ffn_down_dx",
        grid=(F // tn, T // tm),
        in_specs=[xs, ws, os, os, ANY],
        out_specs=[os, os],
        out_shape=[jax.ShapeDtypeStruct((T, F), BF16), jax.ShapeDtypeStruct((T, F), BF16)],
        compiler_params=_params(("parallel", "parallel")),
    )(dx2b, wd, a, b, after)


def _ffn_down_loss_body(nk, inv_d, f_ref, wd_ref, x1_ref, tgt_ref, dx2_ref, dx2b_ref, loss_ref, *scratch):
    j, i, k = pl.program_id(0), pl.program_id(1), pl.program_id(2)
    chunks = _col_chunks(dx2_ref.shape[-1])

    def dot(cols):
        return _dot(f_ref[...], wd_ref[:, cols], NN)

    @pl.when(jnp.logical_and(jnp.logical_and(j == 0, i == 0), k == 0))
    def _():
        loss_ref[...] = jnp.zeros_like(loss_ref)

    def finish(ffn_of):
        total = jnp.zeros((1, 1), F32)
        for cols in chunks:
            e = ffn_of(cols) + x1_ref[:, cols] - tgt_ref[:, cols]
            dx2 = e * inv_d
            dx2_ref[:, cols] = dx2
            dx2b_ref[:, cols] = dx2.astype(BF16)
            total = total + jnp.sum(jnp.sum(e * e, axis=-1, keepdims=True), axis=0, keepdims=True)
        loss_ref[...] += (0.5 * inv_d) * total

    if nk == 1:
        finish(dot)
        return
    acc_ref = scratch[0]

    @pl.when(k == 0)
    def _():
        for cols in chunks:
            acc_ref[:, cols] = dot(cols)

    if nk > 2:
        @pl.when(jnp.logical_and(k > 0, k < nk - 1))
        def _():
            for cols in chunks:
                acc_ref[:, cols] += dot(cols)

    @pl.when(k == nk - 1)
    def _():
        finish(lambda cols: acc_ref[:, cols] + dot(cols))


def _ffn_down_loss(f, wd, x1, tgt):
    T, F = f.shape
    D = wd.shape[1]
    tm, tn, tk = _pick(T, 1024), _pick(D, 1024), _pick(F, 1408)
    gm, gn, gk = T // tm, D // tn, F // tk
    tile = pl.BlockSpec((tm, tn), lambda j, i, k: (i, j))
    return pl.pallas_call(
        functools.partial(_ffn_down_loss_body, gk, 1.0 / D),
        name="ffn_down_loss",
        grid=(gn, gm, gk),
        in_specs=[pl.BlockSpec((tm, tk), lambda j, i, k: (i, k)), pl.BlockSpec((tk, tn), lambda j, i, k: (k, j)),
                  tile, tile],
        out_specs=[tile, tile, pl.BlockSpec((1, 1), lambda j, i, k: (0, 0))],
        out_shape=[jax.ShapeDtypeStruct((T, D), F32), jax.ShapeDtypeStruct((T, D), BF16),
                   jax.ShapeDtypeStruct((1, 1), F32)],
        scratch_shapes=[pltpu.VMEM((tm, tn), F32)] if gk > 1 else [],
        compiler_params=_params(("arbitrary", "arbitrary", "arbitrary")),
    )(f, wd, x1, tgt)


def _lo_mask(shape):
    return lax.broadcasted_iota(jnp.int32, shape, len(shape) - 1) < HEAD_DIM


def _half_sums(t, lo):
    s_lo = jnp.sum(jnp.where(lo, t, 0.0), axis=-1, keepdims=True)
    s_hi = jnp.sum(jnp.where(lo, 0.0, t), axis=-1, keepdims=True)
    return jnp.where(lo, s_lo, s_hi)


def _head_rstd(t, lo):
    return lax.rsqrt(_half_sums(t * t, lo) * (1.0 / HEAD_DIM) + EPS)


def _place(t, lo, kv_head):
    if kv_head == 0:
        t_lo = jnp.where(lo, t, 0.0)
        t_hi = pltpu.roll(t_lo, HEAD_DIM, 1)
    else:
        t_hi = jnp.where(lo, 0.0, t)
        t_lo = pltpu.roll(t_hi, HEAD_DIM, 1)
    return jnp.concatenate([t_lo, t_hi], axis=0).astype(BF16)


def _unplace(c0, c1, lo):
    return jnp.where(lo, c0 + pltpu.roll(c0, HEAD_DIM, 1), c1 + pltpu.roll(c1, HEAD_DIM, 1))


def _band(kv_cur, kv_prev, kg, lo2):
    kb = jnp.concatenate([kv_prev[:, :BLK], kv_cur[:, :BLK]], axis=0)
    vb = jnp.concatenate([kv_prev[:, BLK:], kv_cur[:, BLK:]], axis=0)
    rk = _head_rstd(kb, lo2)
    kn = kb * rk * kg
    kk = [_place(kn, lo2, h) for h in range(N_KV_HEADS)]
    vv = [_place(vb, lo2, h) for h in range(N_KV_HEADS)]
    return kb, rk, kk, vv


def _score_geometry(first_i32):
    qi = lax.broadcasted_iota(jnp.int32, (BLK, 4 * BLK), 0)
    col = lax.broadcasted_iota(jnp.int32, (BLK, 4 * BLK), 1)
    kj = col & (2 * BLK - 1)
    dist = qi + BLK - kj
    valid = (dist >= 0) & (dist < BLK) & (kj >= first_i32 * BLK)
    return col, dist.astype(F32), valid


def _pair_logits(qn, kk, col, distf, valid, slope0, slope1):
    s = _dot(qn.astype(BF16), kk, NT) * (HEAD_DIM ** -0.5)
    slope = jnp.where(col < 2 * BLK, slope0, slope1)
    return jnp.where(valid, s - slope * distf, NEG)


def _pair_probs(qn, kk, col, distf, valid, slope0, slope1, sink0, sink1):
    return _softmax_halves(_pair_logits(qn, kk, col, distf, valid, slope0, slope1), sink0, sink1)


def _softmax_halves(logits, sink0, sink1):
    probs, psink = [], []
    for hh, sk in ((0, sink0), (1, sink1)):
        l = logits[:, 2 * BLK * hh:2 * BLK * (hh + 1)]
        m = jnp.maximum(jnp.max(l, axis=-1, keepdims=True), sk)
        p = jnp.exp(l - m)
        es = jnp.exp(sk - m)
        inv = 1.0 / (jnp.sum(p, axis=-1, keepdims=True) + es)
        probs.append(p * inv)
        psink.append(es * inv)
    return probs, psink


def _gelu(z, with_grad=False):
    cdf = 0.5 * (1.0 + lax.erf(z * (0.5 ** 0.5)))
    if not with_grad:
        return z * cdf
    return z * cdf, cdf + z * jnp.exp(-0.5 * z * z) * ((2.0 * jnp.pi) ** -0.5)


def _tril_w(w):
    r = lax.broadcasted_iota(jnp.int32, (BLK, BLK), 0)
    c = lax.broadcasted_iota(jnp.int32, (BLK, BLK), 1)
    return jnp.where(r >= c, w, 0.0), r >= c


def _gate_fwd_group(zu, zv, lg, lb, w, bcol, with_grad=False):
    u, v = _gelu(zu, with_grad), _gelu(zv, with_grad)
    if with_grad:
        (u, du_dz), (v, dv_dz) = u, v
    mu = jnp.mean(v, axis=-1, keepdims=True)
    vc = v - mu
    rs = lax.rsqrt(jnp.mean(vc * vc, axis=-1, keepdims=True) + EPS)
    vh = vc * rs
    vn = vh * lg + lb
    wt, tril = _tril_w(w)
    mixed = _dot(wt.astype(BF16), vn.astype(BF16), NN) + bcol
    if with_grad:
        return u, vh, rs, vn, wt, tril, mixed, du_dz, dv_dz
    return u, vh, rs, vn, wt, tril, mixed


class _Dims:
    def __init__(self, seq, attn_w, gate_w):
        self.seq, self.attn_w, self.gate_w = seq, attn_w, gate_w
        self.n_heads = attn_w // HEAD_DIM
        self.group = self.n_heads // N_KV_HEADS
        self.n_pairs = attn_w // BLK
        self.n_groups = gate_w // BLK
        self.kv_col = attn_w // (2 * BLK)
        self.u0 = attn_w + 2 * BLK
        self.v0 = self.u0 + gate_w
        self.in_w = self.v0 + gate_w
        self.slopes = [2.0 ** (-8.0 * (h + 1) / self.n_heads) for h in range(self.n_heads)]


def _mixer_fwd_body(d, sink_ref, proj_ref, kvp_ref, qg_ref, kg_ref, lg_ref, lb_ref, w_ref, b_ref, goa_ref, gog_ref,
                    ya_ref, yg_ref, y_ref, logit_scr, prob_scr):
    i = pl.program_id(0)
    first = (i % (d.seq // BLK) == 0).astype(jnp.int32)
    lo = _lo_mask((BLK, BLK))
    lo2 = _lo_mask((2 * BLK, BLK))
    kv_cur = proj_ref[:, d.attn_w:d.attn_w + 2 * BLK]
    _, _, kk, vv = _band(kv_cur, kvp_ref[...], kg_ref[...], lo2)
    col, distf, valid = _score_geometry(first)
    qg = qg_ref[...]
    for j in range(d.n_pairs):
        h0, h1 = 2 * j, 2 * j + 1
        q2 = proj_ref[:, BLK * j:BLK * (j + 1)]
        qn = q2 * _head_rstd(q2, lo) * qg
        logit_scr[j] = _pair_logits(qn, kk[h0 // d.group], col, distf, valid, d.slopes[h0], d.slopes[h1])
    for j in range(d.n_pairs):
        probs, _ = _softmax_halves(logit_scr[j], sink_ref[0, 2 * j], sink_ref[0, 2 * j + 1])
        prob_scr[j] = jnp.concatenate(probs, axis=1).astype(BF16)
    for j in range(d.n_pairs):
        ya_ref[:, BLK * j:BLK * (j + 1)] = _dot(prob_scr[j], vv[2 * j // d.group], NN)
    for g in range(d.n_groups):
        zu = proj_ref[:, d.u0 + BLK * g:d.u0 + BLK * (g + 1)]
        zv = proj_ref[:, d.v0 + BLK * g:d.v0 + BLK * (g + 1)]
        u, _, _, _, _, _, mixed = _gate_fwd_group(zu, zv, lg_ref[g:g + 1, :], lb_ref[g:g + 1, :], w_ref[g], b_ref[g])
        yg_ref[:, BLK * g:BLK * (g + 1)] = u * mixed
    ya = ya_ref[...]
    ra = lax.rsqrt(jnp.mean(ya * ya, axis=-1, keepdims=True) + EPS)
    y_ref[:, :d.attn_w] = (ya * ra * goa_ref[...]).astype(y_ref.dtype)
    yg = yg_ref[...]
    rg = lax.rsqrt(jnp.mean(yg * yg, axis=-1, keepdims=True) + EPS)
    y_ref[:, d.attn_w:] = (yg * rg * gog_ref[...]).astype(y_ref.dtype)


def _mixer_specs(d, T):
    row = lambda w: pl.BlockSpec((BLK, w), lambda i: (i, 0))
    const2 = lambda a: pl.BlockSpec(a.shape, lambda i: (0, 0))
    const3 = lambda a: pl.BlockSpec(a.shape, lambda i: (0, 0, 0))
    kv_prev = pl.BlockSpec((BLK, 2 * BLK), lambda i: (jnp.maximum(i - 1, 0), d.kv_col))
    return row, const2, const3, kv_prev


def _mixer_fwd(d, proj, sinks, qg2, kg2, lg, lb, wsp, bcol, goa, gog):
    T = proj.shape[0]
    row, const2, const3, kv_prev = _mixer_specs(d, T)
    return pl.pallas_call(
        functools.partial(_mixer_fwd_body, d),
        name="mixer_fwd",
        grid=(T // BLK,),
        in_specs=[pl.BlockSpec(memory_space=pltpu.SMEM), row(d.in_w), kv_prev, const2(qg2), const2(kg2),
                  const2(lg), const2(lb), const3(wsp), const3(bcol), const2(goa), const2(gog)],
        out_specs=[row(d.attn_w), row(d.gate_w), row(d.attn_w + d.gate_w)],
        out_shape=[jax.ShapeDtypeStruct((T, d.attn_w), F32), jax.ShapeDtypeStruct((T, d.gate_w), F32),
                   jax.ShapeDtypeStruct((T, d.attn_w + d.gate_w), BF16)],
        scratch_shapes=[pltpu.VMEM((d.n_pairs, BLK, 4 * BLK), F32), pltpu.VMEM((d.n_pairs, BLK, 4 * BLK), BF16)],
        compiler_params=_params(("parallel",)),
    )(sinks, proj, proj, qg2, kg2, lg, lb, wsp, bcol, goa, gog)


def _mixer_bwd_body(d, sink_ref, proj_ref, kvp_ref, ya_ref, yg_ref, dy_ref, qg_ref, kg_ref, lg_ref, lb_ref, w_ref,
                    b_ref, goa_ref, gog_ref,
                    dproj_ref, dkv_ref, dqg_ref, dkg_ref, dsk_ref, dlg_ref, dlb_ref, dw_ref, db_ref, dgoa_ref,
                    dgog_ref):
    i = pl.program_id(0)

    @pl.when(i == 0)
    def _():
        for r in (dqg_ref, dkg_ref, dsk_ref, dlg_ref, dlb_ref, dw_ref, db_ref, dgoa_ref, dgog_ref):
            r[...] = jnp.zeros_like(r)

    first = (i % (d.seq // BLK) == 0).astype(jnp.int32)
    lo = _lo_mask((BLK, BLK))
    lo2 = _lo_mask((2 * BLK, BLK))
    lane_row = lax.broadcasted_iota(jnp.int32, (1, BLK), 1)

    ya = ya_ref[...]
    ra = lax.rsqrt(jnp.mean(ya * ya, axis=-1, keepdims=True) + EPS)
    yah = ya * ra
    dyn = dy_ref[:, :d.attn_w]
    dgoa_ref[...] += jnp.sum(dyn * yah, axis=0, keepdims=True)
    t = dyn * goa_ref[...]
    dya = ra * (t - yah * jnp.mean(t * yah, axis=-1, keepdims=True))
    yg = yg_ref[...]
    rg = lax.rsqrt(jnp.mean(yg * yg, axis=-1, keepdims=True) + EPS)
    ygh = yg * rg
    dyn = dy_ref[:, d.attn_w:]
    dgog_ref[...] += jnp.sum(dyn * ygh, axis=0, keepdims=True)
    t = dyn * gog_ref[...]
    dyg = rg * (t - ygh * jnp.mean(t * ygh, axis=-1, keepdims=True))

    kv_cur = proj_ref[:, d.attn_w:d.attn_w + 2 * BLK]
    kg = kg_ref[...]
    kb, rk, kk, vv = _band(kv_cur, kvp_ref[...], kg, lo2)
    col, distf, valid = _score_geometry(first)
    qg = qg_ref[...]
    ck = [jnp.zeros((BLK, 2 * BLK), F32) for _ in range(N_KV_HEADS)]
    cv = [jnp.zeros((BLK, 2 * BLK), F32) for _ in range(N_KV_HEADS)]
    lo_rows = lax.broadcasted_iota(jnp.int32, (BLK, 2 * BLK), 0) < HEAD_DIM
    dsk = jnp.zeros((1, BLK), F32)
    dqg = jnp.zeros((1, BLK), F32)
    for j in range(d.n_pairs):
        h0, h1 = 2 * j, 2 * j + 1
        kh = h0 // d.group
        cols = slice(BLK * j, BLK * (j + 1))
        q2 = proj_ref[:, cols]
        rq = _head_rstd(q2, lo)
        qh = q2 * rq
        qn = qh * qg
        probs, psink = _pair_probs(qn, kk[kh], col, distf, valid, d.slopes[h0], d.slopes[h1],
                                   sink_ref[0, h0], sink_ref[0, h1])
        do2 = dya[:, cols]
        prod = do2 * ya[:, cols]
        delta = (jnp.sum(jnp.where(lo, prod, 0.0), axis=-1, keepdims=True),
                 jnp.sum(jnp.where(lo, 0.0, prod), axis=-1, keepdims=True))
        do2b = do2.astype(BF16)
        dp = _dot(do2b, vv[kh], NT)
        ds = []
        for hh in (0, 1):
            ds.append(probs[hh] * (dp[:, 2 * BLK * hh:2 * BLK * (hh + 1)] - delta[hh]))
            dsink = -jnp.sum(psink[hh] * delta[hh], axis=0, keepdims=True)
            dsk = dsk + jnp.where(lane_row == (h0 + hh), dsink, 0.0)
        dsb = (jnp.concatenate(ds, axis=1) * (HEAD_DIM ** -0.5)).astype(BF16)
        pb = jnp.concatenate(probs, axis=1).astype(BF16)
        qnb = qn.astype(BF16)
        dqn = _dot(dsb, kk[kh], NN)
        dkk = _dot(qnb, dsb, TN)
        dvv = _dot(do2b, pb, TN)
        ck[kh] = ck[kh] + jnp.where(lo_rows, dkk[:, :2 * BLK], 0.0) + jnp.where(lo_rows, 0.0, dkk[:, 2 * BLK:])
        cv[kh] = cv[kh] + jnp.where(lo_rows, dvv[:, :2 * BLK], 0.0) + jnp.where(lo_rows, 0.0, dvv[:, 2 * BLK:])
        dqg = dqg + jnp.sum(dqn * qh, axis=0, keepdims=True)
        t = dqn * qg
        dq2 = rq * (t - qh * (_half_sums(t * qh, lo) * (1.0 / HEAD_DIM)))
        dproj_ref[:, cols] = dq2.astype(dproj_ref.dtype)
    dsk_ref[...] += dsk
    dqg_ref[...] += dqg
    dkn = _unplace(jnp.transpose(ck[0]), jnp.transpose(ck[1]), lo2)
    dvb = _unplace(jnp.transpose(cv[0]), jnp.transpose(cv[1]), lo2)
    khat = kb * rk
    dkg_ref[...] += jnp.sum(dkn * khat, axis=0, keepdims=True)
    t = dkn * kg
    dkb = rk * (t - khat * (_half_sums(t * khat, lo2) * (1.0 / HEAD_DIM)))
    rows_cur = pl.ds(pl.multiple_of(i * BLK, BLK), BLK)
    rows_prev = pl.ds(pl.multiple_of(jnp.maximum(i - 1, 0) * BLK, BLK), BLK)
    dkv_ref[rows_cur, :] = jnp.concatenate([dkb[BLK:], dvb[BLK:]], axis=1)
    dkv_ref[rows_prev, :] += jnp.concatenate([dkb[:BLK], dvb[:BLK]], axis=1)
    dproj_ref[:, d.attn_w:d.attn_w + 2 * BLK] = jnp.zeros((BLK, 2 * BLK), dproj_ref.dtype)

    for g in range(d.n_groups):
        ucols = slice(d.u0 + BLK * g, d.u0 + BLK * (g + 1))
        vcols = slice(d.v0 + BLK * g, d.v0 + BLK * (g + 1))
        zu = proj_ref[:, ucols]
        zv = proj_ref[:, vcols]
        lg = lg_ref[g:g + 1, :]
        u, vh, rs, vn, wt, tril, mixed, du_dz, dv_dz = _gate_fwd_group(
            zu, zv, lg, lb_ref[g:g + 1, :], w_ref[g], b_ref[g], with_grad=True)
        dyg_g = dyg[:, BLK * g:BLK * (g + 1)]
        du = dyg_g * mixed
        dmix = dyg_g * u
        dmb = dmix.astype(BF16)
        db_ref[g:g + 1, :] += jnp.sum(jnp.transpose(dmix), axis=0, keepdims=True)
        dw_ref[g] += jnp.where(tril, _dot(dmb, vn.astype(BF16), NT), 0.0)
        dvn = _dot(wt.astype(BF16), dmb, TN)
        dlg_ref[g:g + 1, :] += jnp.sum(dvn * vh, axis=0, keepdims=True)
        dlb_ref[g:g + 1, :] += jnp.sum(dvn, axis=0, keepdims=True)
        dvh = dvn * lg
        dv = rs * (dvh - jnp.mean(dvh, axis=-1, keepdims=True) - vh * jnp.mean(dvh * vh, axis=-1, keepdims=True))
        dproj_ref[:, ucols] = (du * du_dz).astype(dproj_ref.dtype)
        dproj_ref[:, vcols] = (dv * dv_dz).astype(dproj_ref.dtype)


def _mixer_bwd(d, proj, ya, yg, dy, sinks, qg2, kg2, lg, lb, wsp, bcol, goa, gog):
    T = proj.shape[0]
    row, const2, const3, kv_prev = _mixer_specs(d, T)
    acc2 = lambda s: pl.BlockSpec(s, lambda i: (0, 0))
    G = d.n_groups
    out_shapes = [((T, d.in_w), BF16), ((T, 2 * BLK), F32), ((1, BLK), F32), ((1, BLK), F32), ((1, BLK), F32),
                  ((G, BLK), F32), ((G, BLK), F32), ((G, BLK, BLK), F32), ((G, BLK), F32),
                  ((1, d.attn_w), F32), ((1, d.gate_w), F32)]
    out_specs = [row(d.in_w)] + [acc2(s) for s, _ in out_shapes[1:7]] + \
                [pl.BlockSpec((G, BLK, BLK), lambda i: (0, 0, 0))] + [acc2(s) for s, _ in out_shapes[8:]]
    return pl.pallas_call(
        functools.partial(_mixer_bwd_body, d),
        name="mixer_bwd",
        grid=(T // BLK,),
        in_specs=[pl.BlockSpec(memory_space=pltpu.SMEM), row(d.in_w), kv_prev, row(d.attn_w), row(d.gate_w),
                  row(d.attn_w + d.gate_w), const2(qg2), const2(kg2), const2(lg), const2(lb), const3(wsp),
                  const3(bcol), const2(goa), const2(gog)],
        out_specs=out_specs,
        out_shape=[jax.ShapeDtypeStruct(s, t) for s, t in out_shapes],
        compiler_params=_params(("arbitrary",)),
    )(sinks, proj, proj, ya, yg, dy, qg2, kg2, lg, lb, wsp, bcol, goa, gog)


def _put_kv_body(dkv_ref, dproj_in_ref, dproj_ref):
    del dproj_in_ref
    dproj_ref[...] = dkv_ref[...].astype(dproj_ref.dtype)


def _put_kv(d, dproj, dkv):
    T = dproj.shape[0]
    tr = _pick(T, 1024, 16)
    return pl.pallas_call(
        functools.partial(_put_kv_body),
        name="put_kv",
        grid=(T // tr,),
        in_specs=[pl.BlockSpec((tr, 2 * BLK), lambda i: (i, 0)), pl.BlockSpec(memory_space=pl.ANY)],
        out_specs=pl.BlockSpec((tr, 2 * BLK), lambda i: (i, d.kv_col)),
        out_shape=jax.ShapeDtypeStruct(dproj.shape, dproj.dtype),
        input_output_aliases={1: 0},
        compiler_params=_params(("parallel",)),
    )(dkv, dproj)


def _add_pair_body(pc_ref, own_ref, got_ref, o_ref):
    del pc_ref
    o_ref[...] = (own_ref[...].astype(F32) + got_ref[...].astype(F32)).astype(o_ref.dtype)


def _add_pair(g4, got, pc, name):
    n, _, h, C = g4.shape
    tr = _pick(h, 512, 16)
    return pl.pallas_call(
        functools.partial(_add_pair_body),
        name=name,
        grid_spec=pltpu.PrefetchScalarGridSpec(
            num_scalar_prefetch=1,
            grid=(n, h // tr),
            in_specs=[pl.BlockSpec((None, None, tr, C), lambda q, i, pc: (q, pc[1], i, 0)),
                      pl.BlockSpec((None, tr, C), lambda q, i, pc: (q, i, 0))],
            out_specs=pl.BlockSpec((None, tr, C), lambda q, i, pc: (q, i, 0)),
        ),
        out_shape=jax.ShapeDtypeStruct((n, h, C), g4.dtype),
        compiler_params=_params(("parallel", "parallel")),
    )(pc, g4, got)


def _adamw_update(w, g, m, v):
    m = ADAM_B1 * m + (1.0 - ADAM_B1) * g
    v = ADAM_B2 * v + (1.0 - ADAM_B2) * (g * g)
    m_hat = m / (1.0 - ADAM_B1 ** ADAM_STEP)
    v_hat = v / (1.0 - ADAM_B2 ** ADAM_STEP)
    return -ADAM_LR * (m_hat / (jnp.sqrt(v_hat) + ADAM_EPS) + ADAM_WD * w), m, v


def _adamw_body(w_ref, g_ref, m_ref, v_ref, d_ref, nm_ref, nv_ref):
    d_ref[...], nm_ref[...], nv_ref[...] = _adamw_update(w_ref[...], g_ref[...], m_ref[...], v_ref[...])


def _adamw(w, g, m, v, name):
    R, C = w.shape
    tr = _pick(R, 512, 8)
    blk = pl.BlockSpec((tr, C), lambda i: (i, 0))
    return pl.pallas_call(
        functools.partial(_adamw_body),
        name=name,
        grid=(R // tr,),
        in_specs=[blk] * 4,
        out_specs=[blk] * 3,
        out_shape=[jax.ShapeDtypeStruct((R, C), F32)] * 3,
        compiler_params=_params(("parallel",)),
    )(w, g, m, v)


def _adamw_halves_body(pc_ref, w_ref, own_ref, got_ref, m_ref, v_ref, g_ref, d_ref, nm_ref, nv_ref):
    mine = pl.program_id(0) == pc_ref[1]

    def update(g):
        g_ref[...] = g
        d_ref[...], nm_ref[...], nv_ref[...] = _adamw_update(w_ref[...], g, m_ref[...], v_ref[...])

    @pl.when(mine)
    def _():
        update(own_ref[...].astype(F32))

    @pl.when(jnp.logical_not(mine))
    def _():
        update(got_ref[...].astype(F32))


def _adamw_halves(w, own, got, m, v, pc, name):
    h, C = own.shape
    tr = _pick(h, 512, 8)
    full = pl.BlockSpec((None, tr, C), lambda hh, i, pc: (hh, i, 0))
    mine = pl.BlockSpec((tr, C), lambda hh, i, pc: (jnp.where(hh == pc[1], i, 0), 0))
    theirs = pl.BlockSpec((tr, C), lambda hh, i, pc: (jnp.where(hh == pc[1], 0, i), 0))
    return pl.pallas_call(
        functools.partial(_adamw_halves_body),
        name=name,
        grid_spec=pltpu.PrefetchScalarGridSpec(
            num_scalar_prefetch=1,
            grid=(2, h // tr),
            in_specs=[full, mine, theirs, full, full],
            out_specs=[full] * 4,
        ),
        out_shape=[jax.ShapeDtypeStruct((2, h, C), F32)] * 4,
        compiler_params=_params(("parallel", "parallel")),
    )(pc, w.reshape(2, h, C), own, got, m.reshape(2, h, C), v.reshape(2, h, C))


def _me():
    x, y, c = lax.axis_index("x"), lax.axis_index("y"), lax.axis_index("c")
    chips = [(1 - x, y), (x, 1 - y), (1 - x, 1 - y)]
    return x, y, c, chips


def _cast_into_body(pc_ref, w_ref, o_ref):
    del pc_ref
    o_ref[...] = w_ref[...].astype(o_ref.dtype)


def _cast_into(w, pc, name, side_by_side=False):
    Rs, C = w.shape
    h = Rs // 2
    tr = _pick(h, 512, 16)
    if side_by_side:
        out_spec = pl.BlockSpec((None, tr, C), lambda hh, i, pc: (hh, i, pc[0]))
        out_shape = jax.ShapeDtypeStruct((2, h, N_CHIPS * C), BF16)
    else:
        out_spec = pl.BlockSpec((None, None, tr, C), lambda hh, i, pc: (pc[0], hh, i, 0))
        out_shape = jax.ShapeDtypeStruct((N_CHIPS, 2, h, C), BF16)
    return pl.pallas_call(
        functools.partial(_cast_into_body),
        name=name,
        grid_spec=pltpu.PrefetchScalarGridSpec(
            num_scalar_prefetch=1,
            grid=(2, h // tr),
            in_specs=[pl.BlockSpec((None, tr, C), lambda hh, i, pc: (hh, i, 0))],
            out_specs=out_spec,
        ),
        out_shape=out_shape,
        compiler_params=_params(("parallel", "parallel")),
    )(pc, w.reshape(2, h, C))


MAX_PIECES = 4


def _send_tile_to_sibling(src_of, dst_of, tr, dst_total, send_sems, recv_sem, last):
    x, y, c, _ = _me()
    pieces = MAX_PIECES if tr % (16 * MAX_PIECES) == 0 else (2 if tr % 32 == 0 else 1)
    n = tr // pieces
    copies = [pltpu.make_async_remote_copy(src_ref=src_of(k * n, n), dst_ref=dst_of(k * n, n), send_sem=send_sems.at[k],
                                           recv_sem=recv_sem, device_id=(x, y, 1 - c), device_id_type=MESH)
              for k in range(pieces)]
    for cp in copies:
        cp.start()
    for cp in copies:
        cp.wait_send()

    @pl.when(last)
    def _():
        pltpu.make_async_remote_copy(src_ref=dst_total, dst_ref=dst_total, send_sem=send_sems.at[0], recv_sem=recv_sem,
                                     device_id=(x, y, 1 - c), device_id_type=MESH).wait_recv()


TILE_SEMS = [pltpu.SemaphoreType.DMA((MAX_PIECES,)), pltpu.SemaphoreType.DMA(())]


def _ag_pair_body(tr, n_i, pc_ref, tile_ref, buf_ref, send_sem, recv_sem):
    j, i = pl.program_id(0), pl.program_id(1)
    q = pc_ref[0] ^ (j + 1)
    c = pc_ref[1]
    r_tile = pl.multiple_of(i * tr, tr)
    last = jnp.logical_and(j == N_CHIPS - 2, i == n_i - 1)
    if len(buf_ref.shape) == 4:
        _send_tile_to_sibling(lambda r0, n: tile_ref.at[:, :, pl.ds(r0, n)],
                              lambda r0, n: buf_ref.at[pl.ds(q, 1), pl.ds(c, 1), pl.ds(r_tile + r0, n)], tr,
                              buf_ref.at[pl.ds(0, N_CHIPS - 1), 0], send_sem, recv_sem, last)
    else:
        cs = buf_ref.shape[2] // N_CHIPS
        cols = pl.ds(pl.multiple_of(q * cs, BLK), cs)
        _send_tile_to_sibling(lambda r0, n: tile_ref.at[:, pl.ds(r0, n)],
                              lambda r0, n: buf_ref.at[pl.ds(c, 1), pl.ds(r_tile + r0, n), cols], tr,
                              buf_ref.at[0, :, pl.ds(0, (N_CHIPS - 1) * cs)], send_sem, recv_sem, last)


def _ag_pair(buf, pc, name):
    if len(buf.shape) == 4:
        _, _, h, C = buf.shape
        tile = lambda tr: pl.BlockSpec((1, 1, tr, C), lambda j, i, pc: (pc[0] ^ (j + 1), pc[1], i, 0))
    else:
        _, h, C = buf.shape
        tile = lambda tr: pl.BlockSpec((1, tr, C // N_CHIPS), lambda j, i, pc: (pc[1], i, pc[0] ^ (j + 1)))
    tr = _pick(h, 512, 16)
    return pl.pallas_call(
        functools.partial(_ag_pair_body, tr, h // tr),
        name=name,
        grid_spec=pltpu.PrefetchScalarGridSpec(
            num_scalar_prefetch=1,
            grid=(N_CHIPS - 1, h // tr),
            in_specs=[tile(tr)],
            out_specs=HBM,
            scratch_shapes=TILE_SEMS,
        ),
        out_shape=jax.ShapeDtypeStruct(buf.shape, buf.dtype),
        input_output_aliases={1: 0},
        compiler_params=_params(("arbitrary", "arbitrary")),
    )(pc, buf)


def _swap_halves_body(tr, n_q, n_i, pc_ref, tile_ref, got_ref, send_sem, recv_sem):
    del pc_ref
    q, i = pl.program_id(0), pl.program_id(1)
    r_tile = pl.multiple_of(i * tr, tr)
    _send_tile_to_sibling(lambda r0, n: tile_ref.at[:, :, pl.ds(r0, n)],
                          lambda r0, n: got_ref.at[pl.ds(q, 1), :, pl.ds(r_tile + r0, n)], tr, got_ref, send_sem, recv_sem,
                          jnp.logical_and(q == n_q - 1, i == n_i - 1))


def _swap_halves(g4, pc, name):
    n, _, h, C = g4.shape
    tr = _pick(h, 512, 16)
    return pl.pallas_call(
        functools.partial(_swap_halves_body, tr, n, h // tr),
        name=name,
        grid_spec=pltpu.PrefetchScalarGridSpec(
            num_scalar_prefetch=1,
            grid=(n, h // tr),
            in_specs=[pl.BlockSpec((1, 1, tr, C), lambda q, i, pc: (q, 1 - pc[1], i, 0))],
            out_specs=HBM,
            scratch_shapes=TILE_SEMS,
        ),
        out_shape=jax.ShapeDtypeStruct((n, 1, h, C), g4.dtype),
        compiler_params=_params(("arbitrary", "arbitrary")),
    )(pc, g4).reshape(n, h, C)


def _ici_copy(src, dst, send_sems, recv_sems, j, chip, c):
    return pltpu.make_async_remote_copy(src_ref=src, dst_ref=dst, send_sem=send_sems.at[j], recv_sem=recv_sems.at[j],
                                        device_id=(chip[0], chip[1], c), device_id_type=MESH)


def _token_spec():
    return jax.ShapeDtypeStruct((8, BLK), F32), pl.BlockSpec(memory_space=pltpu.VMEM)


def _slab(buf_ref, q, c):
    if len(buf_ref.shape) == 4:
        return buf_ref.at[q, c]
    cs = buf_ref.shape[2] // N_CHIPS
    return buf_ref.at[c, :, pl.ds(pl.multiple_of(q * cs, BLK), cs)]


def _ag_start_body(buf_ref, after_ref, send_sems, recv_sems, buf_thru, token_ref):
    del after_ref, buf_thru
    x, y, c, chips = _me()
    mine = _slab(buf_ref, 2 * x + y, c)
    for j, chip in enumerate(chips):
        _ici_copy(mine, mine, send_sems, recv_sems, j, chip, c).start()
    token_ref[...] = jnp.zeros_like(token_ref)


def _ag_start(buf, after, name):
    tok_shape, tok_spec = _token_spec()
    sems = pltpu.SemaphoreType.DMA((N_CHIPS - 1,))
    return pl.pallas_call(
        functools.partial(_ag_start_body),
        name=name,
        in_specs=[HBM, ANY],
        out_specs=[SEM, SEM, HBM, tok_spec],
        out_shape=[sems, sems, pltpu.HBM(buf.shape, buf.dtype), tok_shape],
        input_output_aliases={0: 2},
        compiler_params=pltpu.CompilerParams(has_side_effects=EFFECT),
    )(pltpu.with_memory_space_constraint(buf, pltpu.HBM), after)


def _ag_wait_body(buf_ref, send_sems, recv_sems, after_ref, buf_out):
    del after_ref, buf_out
    x, y, c, chips = _me()
    mine = _slab(buf_ref, 2 * x + y, c)
    for j, chip in enumerate(chips):
        theirs = _slab(buf_ref, 2 * chip[0] + chip[1], c)
        _ici_copy(mine, mine, send_sems, recv_sems, j, chip, c).wait_send()
        _ici_copy(theirs, theirs, send_sems, recv_sems, j, chip, c).wait_recv()


def _ag_wait(buf, send_sems, recv_sems, after, name):
    return pl.pallas_call(
        functools.partial(_ag_wait_body),
        name=name,
        in_specs=[HBM, SEM, SEM, ANY],
        out_specs=HBM,
        out_shape=pltpu.HBM(buf.shape, buf.dtype),
        input_output_aliases={0: 0},
        compiler_params=pltpu.CompilerParams(has_side_effects=EFFECT),
    )(buf, send_sems, recv_sems, after)


def _rs_start_body(pair_ref, land_ref, after_ref, send_sems, recv_sems, pair_thru, land_thru, token_ref):
    del after_ref, pair_thru, land_thru
    x, y, c, chips = _me()
    for j, chip in enumerate(chips):
        _ici_copy(pair_ref.at[2 * chip[0] + chip[1]], land_ref.at[j], send_sems, recv_sems, j, chip, c).start()
    token_ref[...] = jnp.zeros_like(token_ref)


def _rs_start(pair, after, name):
    n, h, C = pair.shape
    tok_shape, tok_spec = _token_spec()
    sems = pltpu.SemaphoreType.DMA((N_CHIPS - 1,))
    land = pltpu.with_memory_space_constraint(lax.empty((N_CHIPS - 1, h, C), pair.dtype), pltpu.HBM)
    return pl.pallas_call(
        functools.partial(_rs_start_body),
        name=name,
        in_specs=[HBM, HBM, ANY],
        out_specs=[SEM, SEM, HBM, HBM, tok_spec],
        out_shape=[sems, sems, pltpu.HBM(pair.shape, pair.dtype), pltpu.HBM(land.shape, land.dtype), tok_shape],
        input_output_aliases={0: 2, 1: 3},
        compiler_params=pltpu.CompilerParams(has_side_effects=EFFECT),
    )(pltpu.with_memory_space_constraint(pair, pltpu.HBM), land, after)


def _rs_wait_body(pair_ref, land_ref, send_sems, recv_sems, after_ref, pair_out, land_out):
    del after_ref, pair_out, land_out
    x, y, c, chips = _me()
    for j, chip in enumerate(chips):
        _ici_copy(pair_ref.at[0], land_ref.at[j], send_sems, recv_sems, j, chip, c).wait_send()
        _ici_copy(pair_ref.at[0], land_ref.at[j], send_sems, recv_sems, j, chip, c).wait_recv()


def _rs_wait(pair, land, send_sems, recv_sems, after, name):
    return pl.pallas_call(
        functools.partial(_rs_wait_body),
        name=name,
        in_specs=[HBM, HBM, SEM, SEM, ANY],
        out_specs=[HBM, HBM],
        out_shape=[pltpu.HBM(pair.shape, pair.dtype), pltpu.HBM(land.shape, land.dtype)],
        input_output_aliases={0: 0, 1: 1},
        compiler_params=pltpu.CompilerParams(has_side_effects=EFFECT),
    )(pair, land, send_sems, recv_sems, after)


def _swap_copy(g4_ref, got_ref, send_sem, recv_sem):
    x, y, c, _ = _me()
    return pltpu.make_async_remote_copy(src_ref=g4_ref.at[:, 1 - c], dst_ref=got_ref, send_sem=send_sem,
                                        recv_sem=recv_sem, device_id=(x, y, 1 - c), device_id_type=MESH)


def _swap_start_body(g4_ref, got_ref, send_sem, recv_sem, g4_thru, got_thru, token_ref):
    del g4_thru, got_thru
    _swap_copy(g4_ref, got_ref, send_sem, recv_sem).start()
    token_ref[...] = jnp.zeros_like(token_ref)


def _swap_start(g4, name):
    n, _, h, C = g4.shape
    tok_shape, tok_spec = _token_spec()
    sem = pltpu.SemaphoreType.DMA(())
    got = pltpu.with_memory_space_constraint(lax.empty((n, h, C), g4.dtype), pltpu.HBM)
    return pl.pallas_call(
        functools.partial(_swap_start_body),
        name=name,
        in_specs=[HBM, HBM],
        out_specs=[SEM, SEM, HBM, HBM, tok_spec],
        out_shape=[sem, sem, pltpu.HBM(g4.shape, g4.dtype), pltpu.HBM(got.shape, got.dtype), tok_shape],
        input_output_aliases={0: 2, 1: 3},
        compiler_params=pltpu.CompilerParams(has_side_effects=EFFECT),
    )(pltpu.with_memory_space_constraint(g4, pltpu.HBM), got)


def _swap_wait_body(g4_ref, got_ref, send_sem, recv_sem, after_ref, g4_out, got_out):
    del after_ref, g4_out, got_out
    cp = _swap_copy(g4_ref, got_ref, send_sem, recv_sem)
    cp.wait_send()
    cp.wait_recv()


def _swap_wait(g4, got, send_sem, recv_sem, after, name):
    return pl.pallas_call(
        functools.partial(_swap_wait_body),
        name=name,
        in_specs=[HBM, HBM, SEM, SEM, ANY],
        out_specs=[HBM, HBM],
        out_shape=[pltpu.HBM(g4.shape, g4.dtype), pltpu.HBM(got.shape, got.dtype)],
        input_output_aliases={0: 0, 1: 1},
        compiler_params=pltpu.CompilerParams(has_side_effects=EFFECT),
    )(g4, got, send_sem, recv_sem, after)


def _add_chips_body(tr, n_i, pc_ref, own_ref, l0_ref, l1_ref, l2_ref, o_ref, got_ref, send_sems, recv_sem):
    del pc_ref
    i = pl.program_id(0)
    r = own_ref[...].astype(F32) + l0_ref[...].astype(F32)
    o_ref[...] = (r + l1_ref[...].astype(F32) + l2_ref[...].astype(F32)).astype(o_ref.dtype)
    r_tile = pl.multiple_of(i * tr, tr)
    _send_tile_to_sibling(lambda r0, n: o_ref.at[pl.ds(r0, n)], lambda r0, n: got_ref.at[pl.ds(r_tile + r0, n)], tr,
                          got_ref, send_sems, recv_sem, i == n_i - 1)


def _add_chips(pair, land, pc, name):
    _, h, C = pair.shape
    tr = _pick(h, 256, 16)
    slot = lambda j: pl.BlockSpec((None, tr, C), lambda i, pc: (j, i, 0))
    return pl.pallas_call(
        functools.partial(_add_chips_body, tr, h // tr),
        name=name,
        grid_spec=pltpu.PrefetchScalarGridSpec(
            num_scalar_prefetch=1,
            grid=(h // tr,),
            in_specs=[pl.BlockSpec((None, tr, C), lambda i, pc: (pc[0], i, 0)), slot(0), slot(1), slot(2)],
            out_specs=[pl.BlockSpec((tr, C), lambda i, pc: (i, 0)), HBM],
            scratch_shapes=TILE_SEMS,
        ),
        out_shape=[jax.ShapeDtypeStruct((h, C), pair.dtype), jax.ShapeDtypeStruct((h, C), pair.dtype)],
        compiler_params=_params(("arbitrary",)),
    )(pc, pair, land, land, land)


def _peer(r):
    x, y, c, _ = _me()
    return (x ^ ((r >> 2) & 1), y ^ ((r >> 1) & 1), c ^ (r & 1))


def _ar_start_body(x_ref, land_ref, send_sems, recv_sems, x_thru, land_thru, token_ref):
    del x_thru, land_thru
    for r in range(1, N_DEV):
        pltpu.make_async_remote_copy(src_ref=x_ref, dst_ref=land_ref.at[r - 1], send_sem=send_sems.at[r - 1],
                                     recv_sem=recv_sems.at[r - 1], device_id=_peer(r), device_id_type=MESH).start()
    token_ref[...] = jnp.zeros_like(token_ref)


def _ar_start(packed):
    tok_shape, tok_spec = _token_spec()
    sems = pltpu.SemaphoreType.DMA((N_DEV - 1,))
    land = pltpu.with_memory_space_constraint(lax.empty((N_DEV - 1,) + packed.shape, packed.dtype), pltpu.HBM)
    return pl.pallas_call(
        functools.partial(_ar_start_body),
        name="ar_start",
        in_specs=[HBM, HBM],
        out_specs=[SEM, SEM, HBM, HBM, tok_spec],
        out_shape=[sems, sems, pltpu.HBM(packed.shape, packed.dtype), pltpu.HBM(land.shape, land.dtype), tok_shape],
        input_output_aliases={0: 2, 1: 3},
        compiler_params=pltpu.CompilerParams(has_side_effects=EFFECT),
    )(pltpu.with_memory_space_constraint(packed, pltpu.HBM), land)


def _ar_wait_body(x_ref, land_ref, send_sems, recv_sems, after_ref, x_out, land_out):
    del after_ref, x_out, land_out
    for r in range(1, N_DEV):
        cp = pltpu.make_async_remote_copy(src_ref=x_ref, dst_ref=land_ref.at[r - 1], send_sem=send_sems.at[r - 1],
                                          recv_sem=recv_sems.at[r - 1], device_id=_peer(r), device_id_type=MESH)
        cp.wait_send()
        cp.wait_recv()


def _ar_wait(packed, land, send_sems, recv_sems, after):
    return pl.pallas_call(
        functools.partial(_ar_wait_body),
        name="ar_wait",
        in_specs=[HBM, HBM, SEM, SEM, ANY],
        out_specs=[HBM, HBM],
        out_shape=[pltpu.HBM(packed.shape, packed.dtype), pltpu.HBM(land.shape, land.dtype)],
        input_output_aliases={0: 0, 1: 1},
        compiler_params=pltpu.CompilerParams(has_side_effects=EFFECT),
    )(packed, land, send_sems, recv_sems, after)


def _ar_sum_body(me_ref, own_ref, *rest):
    o_ref = rest[N_DEV]
    acc = None
    for dev in range(N_DEV):
        term = jnp.where(me_ref[0] == dev, own_ref[...], rest[dev][...])
        acc = term if acc is None else acc + term
    o_ref[...] = acc


def _ar_sum(packed, land, me):
    R, C = packed.shape
    tr = _pick(R, 552, 8)
    own = pl.BlockSpec((tr, C), lambda i, me: (i, 0))
    slot = lambda dev: pl.BlockSpec((None, tr, C), lambda i, me: (jnp.maximum((dev ^ me[0]) - 1, 0), i, 0))
    return pl.pallas_call(
        functools.partial(_ar_sum_body),
        name="ar_sum",
        grid_spec=pltpu.PrefetchScalarGridSpec(
            num_scalar_prefetch=1,
            grid=(R // tr,),
            in_specs=[own] + [slot(dev) for dev in range(N_DEV)],
            out_specs=pl.BlockSpec((tr, C), lambda i, me: (i, 0)),
        ),
        out_shape=jax.ShapeDtypeStruct((R, C), F32),
        compiler_params=_params(("parallel",)),
    )(me, packed, *([land] * N_DEV))


def _pack(arrays):
    rows = []
    for a in arrays:
        flat = a.reshape(-1).astype(F32)
        pad = (-flat.shape[0]) % BLK
        rows.append(jnp.pad(flat, (0, pad)).reshape(-1, BLK))
    packed = jnp.concatenate(rows, axis=0)
    pad = (-packed.shape[0]) % 8
    return jnp.pad(packed, ((0, pad), (0, 0)))


def _unpack(packed, shapes):
    out, r = [], 0
    for s in shapes:
        n = 1
        for k in s:
            n *= k
        nr = -(-n // BLK)
        out.append(packed[r:r + nr].reshape(-1)[:n].reshape(s))
        r += nr
    return out


def kernel(x, norm1_g, w_in, q_norm_g, k_norm_g, attn_sinks, gate_ln_g, gate_ln_b, w_spatial, b_spatial, out_norm_attn_g, out_norm_gate_g, w_out, norm2_g, w_ffn_gate, w_ffn_up, w_ffn_down, loss_target, m_norm1_g, m_w_in, m_q_norm_g, m_k_norm_g, m_attn_sinks, m_gate_ln_g, m_gate_ln_b, m_w_spatial, m_b_spatial, m_out_norm_attn_g, m_out_norm_gate_g, m_w_out, m_norm2_g, m_w_ffn_gate, m_w_ffn_up, m_w_ffn_down, v_norm1_g, v_w_in, v_q_norm_g, v_k_norm_g, v_attn_sinks, v_gate_ln_g, v_gate_ln_b, v_w_spatial, v_b_spatial, v_out_norm_attn_g, v_out_norm_gate_g, v_w_out, v_norm2_g, v_w_ffn_gate, v_w_ffn_up, v_w_ffn_down):
    bl, seq, D = x.shape
    T = bl * seq
    attn_w, gate_w = out_norm_attn_g.shape[1], out_norm_gate_g.shape[1]
    d = _Dims(seq, attn_w, gate_w)
    G = d.n_groups
    in_w = d.in_w
    dff = w_ffn_gate.shape[2] * N_CHIPS
    assert w_in.shape[2] * N_CHIPS == in_w and seq % BLK == 0 and attn_w % (2 * BLK) == 0

    pc = jnp.stack([2 * lax.axis_index("x") + lax.axis_index("y"), lax.axis_index("c")]).astype(jnp.int32)
    big = [w_in[0], w_out[0], w_ffn_gate[0], w_ffn_up[0], w_ffn_down[0]]
    names = ["in", "out", "gate", "up", "down"]
    xf = x.reshape(T, D)
    tgt = loss_target.reshape(T, D)
    send, recv, buf, behind = _ag_start(_cast_into(big[0], pc, "cast_in"), norm1_g, "ag_start_in")
    started = [(send, recv, buf)]
    h1 = _rms_fwd(xf, norm1_g, "norm1_fwd", after=behind)
    behind = h1
    for w, n in zip(big[1:], names[1:]):
        buf = _cast_into(w, pc, "cast_" + n, side_by_side=n in ("gate", "up"))
        send, recv, buf, behind = _ag_start(buf, behind, "ag_start_" + n)
        started.append((send, recv, buf))

    def gathered(k, after):
        send, recv, buf = started[k]
        buf = _ag_wait(buf, send, recv, after, "ag_wait_" + names[k])
        buf = _ag_pair(buf, pc, "ag_pair_" + names[k])
        rs, cs = big[k].shape
        return buf.reshape(rs, N_CHIPS * cs) if len(buf.shape) == 3 else buf.reshape(N_CHIPS, rs, cs)

    qg2 = jnp.tile(q_norm_g, (1, 2))
    kg2 = jnp.tile(k_norm_g, (1, 2))
    lg, lb, wsp = gate_ln_g[0], gate_ln_b[0], w_spatial[0]
    bcol = jnp.broadcast_to(b_spatial[0][:, :, None], (G, BLK, BLK))

    win_full = jnp.transpose(gathered(0, behind), (1, 0, 2)).reshape(D, in_w)
    proj = _matmul(h1, win_full, "nn", F32, "proj_fwd", tm=1024, tn=1664)
    ya, yg, yn = _mixer_fwd(d, proj, attn_sinks, qg2, kg2, lg, lb, wsp, bcol, out_norm_attn_g, out_norm_gate_g)
    wout_full = gathered(1, yn).reshape(attn_w + gate_w, D)
    x1 = _matmul(yn, wout_full, "nn", F32, "out_fwd", tm=1024, tn=1024, add=xf)
    h2 = _rms_fwd(x1, norm2_g, "norm2_fwd")
    wg_full, wu_full = gathered(2, h2), gathered(3, h2)
    a, b, f = _ffn_up(h2, wg_full, wu_full)
    wd_full = gathered(4, f).reshape(dff, D)
    dx2, dx2b, loss_local = _ffn_down_loss(f, wd_full, x1, tgt)

    def swap_start(g, n):
        g4 = g.reshape(N_CHIPS, 2, g.shape[1] // 2, g.shape[2])
        return _swap_start(g4, "rs_swap_start_" + n)

    def reduce_start(swapping, n, after):
        send, recv, g4, got, _ = swapping
        g4, got = _swap_wait(g4, got, send, recv, after, "rs_swap_wait_" + n)
        return _rs_start(_add_pair(g4, got, pc, "rs_add_pair_" + n), got, "rs_start_" + n)

    reducing = {}
    g_d = _matmul(f, dx2b, "tn", BF16, "ffn_down_dw", tm=1408, tn=1024, tk=2048, out_slab="r")
    swap_d = swap_start(g_d, "down")
    da, db = _ffn_down_dx(dx2b, wd_full, a, b, swap_d[4])
    g_g = _matmul(h2, da, "tn", BF16, "ffn_gate_dw", tm=1024, tn=1408, tk=2048, out_slab="c")
    swap_g = swap_start(g_g, "gate")
    reducing["down"] = reduce_start(swap_d, "down", swap_g[4])
    g_u = _matmul(h2, db, "tn", BF16, "ffn_up_dw", tm=1024, tn=1408, tk=2048, out_slab="c",
                  after=reducing["down"][4])
    swap_u = swap_start(g_u, "up")
    reducing["gate"] = reduce_start(swap_g, "gate", swap_u[4])
    dh2 = _matmul(da, wg_full, "nt", F32, "ffn_gate_dx", tm=1024, tn=1024, tk=2816, after=reducing["gate"][4])
    dh2 = _matmul(db, wu_full, "nt", F32, "ffn_up_dx", tm=1024, tn=1024, tk=2816, add=dh2)
    reducing["up"] = reduce_start(swap_u, "up", dh2)
    dx1, dx1b, dg_norm2 = _rms_bwd(x1, norm2_g, dh2, dx2, "norm2_bwd", True)
    g_o = _matmul(yn, dx1b, "tn", BF16, "out_dw", tm=512, tn=1024, tk=2048, out_slab="r",
                  after=reducing["up"][4])
    swap_o = swap_start(g_o, "out")
    dy = _matmul(dx1b, wout_full, "nt", F32, "out_dx", tm=1024, tn=1024, after=swap_o[4])
    (dproj, dkv, dqg, dkg, dsk, dlg, dlb, dwsp, dbsp, dgoa, dgog) = _mixer_bwd(
        d, proj, ya, yg, dy, attn_sinks, qg2, kg2, lg, lb, wsp, bcol, out_norm_attn_g, out_norm_gate_g)
    dproj = _put_kv(d, dproj, dkv)
    reducing["out"] = reduce_start(swap_o, "out", dproj)
    g_in_full = _matmul(h1, dproj, "tn", BF16, "proj_dw", tm=1024, tn=1664, tk=2048,
                        after=reducing["out"][4])
    g_i = jnp.transpose(g_in_full.reshape(D, N_CHIPS, in_w // N_CHIPS), (1, 0, 2))
    g4_i = g_i.reshape(N_CHIPS, 2, D // 2, in_w // N_CHIPS)
    pair_i = _add_pair(g4_i, _swap_halves(g4_i, pc, "rs_swap_in"), pc, "rs_add_pair_in")
    reducing["in"] = _rs_start(pair_i, g_i, "rs_start_in")
    dh1 = _matmul(dproj, win_full, "nt", F32, "proj_dx", tm=1024, tn=1024, after=reducing["in"][4])
    dx, dg_norm1 = _rms_bwd(xf, norm1_g, dh1, dx1, "norm1_bwd", False)

    dqg64 = dqg[:, :HEAD_DIM] + dqg[:, HEAD_DIM:]
    dkg64 = dkg[:, :HEAD_DIM] + dkg[:, HEAD_DIM:]
    small_g_local = [dg_norm1, dqg64, dkg64, dsk[:, :d.n_heads], dlg, dlb, dwsp, dbsp, dgoa, dgog, dg_norm2,
                     loss_local]
    ar_send, ar_recv, ar_own, ar_land, ar_token = _ar_start(_pack(small_g_local))

    big_m = [m_w_in[0], m_w_out[0], m_w_ffn_gate[0], m_w_ffn_up[0], m_w_ffn_down[0]]
    big_v = [v_w_in[0], v_w_out[0], v_w_ffn_gate[0], v_w_ffn_up[0], v_w_ffn_down[0]]
    big_grads, big_d, big_nm, big_nv = [], [], [], []
    for w, m, v, n in zip(big, big_m, big_v, names):
        send, recv, pair, land, _ = reducing[n]
        pair, land = _rs_wait(pair, land, send, recv, ar_token, "rs_wait_" + n)
        own, got = _add_chips(pair, land, pc, "rs_add_chips_" + n)
        outs = _adamw_halves(w, own, got, m, v, pc, "adamw_" + n)
        for lst, o in zip((big_grads, big_d, big_nm, big_nv), outs):
            lst.append(o.reshape(w.shape))

    small_names_w = [norm1_g, q_norm_g, k_norm_g, attn_sinks, gate_ln_g, gate_ln_b, w_spatial, b_spatial,
                     out_norm_attn_g, out_norm_gate_g, norm2_g]
    small_m = [m_norm1_g, m_q_norm_g, m_k_norm_g, m_attn_sinks, m_gate_ln_g, m_gate_ln_b, m_w_spatial, m_b_spatial,
               m_out_norm_attn_g, m_out_norm_gate_g, m_norm2_g]
    small_v = [v_norm1_g, v_q_norm_g, v_k_norm_g, v_attn_sinks, v_gate_ln_g, v_gate_ln_b, v_w_spatial, v_b_spatial,
               v_out_norm_attn_g, v_out_norm_gate_g, v_norm2_g]
    shapes = [w.shape for w in small_names_w] + [loss_local.shape]
    ride = [jnp.zeros(loss_local.shape, F32)]
    ar_own, ar_land = _ar_wait(ar_own, ar_land, ar_send, ar_recv, big_nv[-1])
    me = (4 * lax.axis_index("x") + 2 * lax.axis_index("y") + lax.axis_index("c")).astype(jnp.int32).reshape(1)
    sg = _ar_sum(ar_own, ar_land, me)
    sd, snm, snv = _adamw(_pack(small_names_w + ride), sg, _pack(small_m + ride), _pack(small_v + ride), "adamw_small")
    small_g, small_d, small_nm, small_nv = (_unpack(t, shapes) for t in (sg, sd, snm, snv))
    loss = small_g[-1][0, 0]

    def order(small, bigs):
        s = list(small)
        bg = [t[None] for t in bigs]
        return [s[0], bg[0], s[1], s[2], s[3], s[4], s[5], s[6], s[7], s[8], s[9], bg[1], s[10], bg[2], bg[3], bg[4]]

    grad_x = dx.reshape(bl, seq, D)
    return (loss, grad_x, *order(small_g, big_grads), *order(small_d, big_d), *order(small_nm, big_nm),
            *order(small_nv, big_nv))
```

```python
import functools

import jax
import jax.numpy as jnp
from jax import lax
from jax.experimental import pallas as pl
from jax.experimental.pallas import tpu as pltpu

F32 = jnp.float32
BF16 = jnp.bfloat16
MESH = pl.DeviceIdType.MESH

EPS = 1e-6
HEAD_DIM = 64
N_KV_HEADS = 2
BLK = 128
N_CHIPS = 4
N_DEV = 8
NEG = -1e30

ADAM_LR = 0.001
ADAM_B1 = 0.9
ADAM_B2 = 0.999
ADAM_EPS = 1e-08
ADAM_WD = 0.01
ADAM_STEP = 10

VMEM_LIMIT = 56 * 1024 * 1024

NN = (((1,), (0,)), ((), ()))
NT = (((1,), (1,)), ((), ()))
TN = (((0,), (0,)), ((), ()))
HBM = pl.BlockSpec(memory_space=pltpu.HBM)
ANY = pl.BlockSpec(memory_space=pl.ANY)
SEM = pl.BlockSpec(memory_space=pltpu.SEMAPHORE)
EFFECT = pltpu.SideEffectType.DATAFLOW_SIDE_EFFECTING


def _dot(a, b, dn):
    return lax.dot_general(a, b, dn, preferred_element_type=F32)


def _pick(dim, pref, align=128):
    if dim <= pref:
        return dim
    t = (pref // align) * align
    while t >= align:
        if dim % t == 0:
            return t
        t -= align
    return dim


def _params(sem):
    return pltpu.CompilerParams(dimension_semantics=sem, vmem_limit_bytes=VMEM_LIMIT)


MM_CHUNK = 512


def _col_chunks(tn):
    return [slice(c0, min(c0 + MM_CHUNK, tn)) for c0 in range(0, tn, MM_CHUNK)]


def _mm_body(dn, nk, has_add, has_after, *refs):
    a_ref, b_ref = refs[:2]
    add_ref = refs[2] if has_add else None
    o_ref = refs[2 + has_add + has_after]
    chunks = _col_chunks(o_ref.shape[-1])

    def dot(cols):
        return _dot(a_ref[...], b_ref[cols, :] if dn == NT else b_ref[:, cols], dn)

    def finish(cols, r):
        if add_ref is not None:
            r = r + add_ref[:, cols]
        o_ref[:, cols] = r.astype(o_ref.dtype)

    if nk == 1:
        for cols in chunks:
            finish(cols, dot(cols))
        return
    acc_ref = refs[-1]
    k = pl.program_id(2)

    @pl.when(k == 0)
    def _():
        for cols in chunks:
            acc_ref[:, cols] = dot(cols)

    if nk > 2:
        @pl.when(jnp.logical_and(k > 0, k < nk - 1))
        def _():
            for cols in chunks:
                acc_ref[:, cols] += dot(cols)

    @pl.when(k == nk - 1)
    def _():
        for cols in chunks:
            finish(cols, acc_ref[:, cols] + dot(cols))


def _matmul(a, b, mode, out_dtype, name, *, tm, tn, tk=None, add=None, out_slab=None, after=None):
    if mode == "nn":
        (M, K), N = a.shape, b.shape[1]
    elif mode == "nt":
        (M, K), N = a.shape, b.shape[0]
    else:
        (K, M), N = a.shape, b.shape[1]
    tk = K if tk is None else tk
    tm, tn, tk = _pick(M, tm), _pick(N, tn), _pick(K, tk)
    if out_slab == "c":
        tn = _pick(N // N_CHIPS, tn)
    if out_slab == "r":
        tm = _pick(M // N_CHIPS, tm)
    gm, gn, gk = M // tm, N // tn, K // tk

    if mode == "tn":
        a_spec = pl.BlockSpec((tk, tm), lambda j, i, k: (k, i))
        b_spec = pl.BlockSpec((tk, tn), lambda j, i, k: (k, j))
    else:
        a_spec = pl.BlockSpec((tm, tk), lambda j, i, k: (i, k))
        if mode == "nn":
            b_spec = pl.BlockSpec((tk, tn), lambda j, i, k: (k, j))
        else:
            b_spec = pl.BlockSpec((tn, tk), lambda j, i, k: (j, k))

    if out_slab == "c":
        per = (N // N_CHIPS) // tn
        o_spec = pl.BlockSpec((None, tm, tn), lambda j, i, k: (j // per, i, j % per))
        o_shape = jax.ShapeDtypeStruct((N_CHIPS, M, N // N_CHIPS), out_dtype)
    elif out_slab == "r":
        per = (M // N_CHIPS) // tm
        o_spec = pl.BlockSpec((None, tm, tn), lambda j, i, k: (i // per, i % per, j))
        o_shape = jax.ShapeDtypeStruct((N_CHIPS, M // N_CHIPS, N), out_dtype)
    else:
        o_spec = pl.BlockSpec((tm, tn), lambda j, i, k: (i, j))
        o_shape = jax.ShapeDtypeStruct((M, N), out_dtype)

    dn = {"nn": NN, "nt": NT, "tn": TN}[mode]
    in_specs = [a_spec, b_spec]
    args = [a, b]
    if add is not None:
        in_specs.append(pl.BlockSpec((tm, tn), lambda j, i, k: (i, j)))
        args.append(add)
    if after is not None:
        in_specs.append(ANY)
        args.append(after)
    return pl.pallas_call(
        functools.partial(_mm_body, dn, gk, add is not None, after is not None),
        name=name,
        grid=(gn, gm, gk),
        in_specs=in_specs,
        out_specs=o_spec,
        out_shape=o_shape,
        scratch_shapes=[pltpu.VMEM((tm, tn), F32)] if gk > 1 else [],
        compiler_params=_params(("parallel", "parallel", "arbitrary")),
    )(*args)


def _rms_fwd_body(x_ref, g_ref, *rest):
    h_ref = rest[-1]
    x = x_ref[...]
    r = lax.rsqrt(jnp.mean(x * x, axis=-1, keepdims=True) + EPS)
    h_ref[...] = (x * r * g_ref[...]).astype(h_ref.dtype)


def _rms_fwd(x, g, name, after=None):
    T, D = x.shape
    tr = _pick(T, 256, 16)
    extra = [] if after is None else [after]
    return pl.pallas_call(
        functools.partial(_rms_fwd_body),
        name=name,
        grid=(T // tr,),
        in_specs=[pl.BlockSpec((tr, D), lambda i: (i, 0)), pl.BlockSpec((1, D), lambda i: (0, 0))] + [ANY] * len(extra),
        out_specs=pl.BlockSpec((tr, D), lambda i: (i, 0)),
        out_shape=jax.ShapeDtypeStruct((T, D), BF16),
        compiler_params=_params(("parallel",)),
    )(x, g, *extra)


def _rms_bwd_body(with_bf16, x_ref, g_ref, dh_ref, res_ref, dx_ref, *rest):
    dg_ref = rest[-1]

    @pl.when(pl.program_id(0) == 0)
    def _():
        dg_ref[...] = jnp.zeros_like(dg_ref)

    x = x_ref[...]
    r = lax.rsqrt(jnp.mean(x * x, axis=-1, keepdims=True) + EPS)
    xh = x * r
    dh = dh_ref[...]
    dg_ref[...] += jnp.sum(dh * xh, axis=0, keepdims=True)
    t = dh * g_ref[...]
    dx = res_ref[...] + r * (t - xh * jnp.mean(t * xh, axis=-1, keepdims=True))
    dx_ref[...] = dx
    if with_bf16:
        rest[0][...] = dx.astype(BF16)


def _rms_bwd(x, g, dh, res, name, with_bf16):
    T, D = x.shape
    tr = _pick(T, 256, 16)
    row = pl.BlockSpec((tr, D), lambda i: (i, 0))
    vec = pl.BlockSpec((1, D), lambda i: (0, 0))
    extra = [jax.ShapeDtypeStruct((T, D), BF16)] if with_bf16 else []
    return pl.pallas_call(
        functools.partial(_rms_bwd_body, with_bf16),
        name=name,
        grid=(T // tr,),
        in_specs=[row, vec, row, row],
        out_specs=[row] + [row] * len(extra) + [vec],
        out_shape=[jax.ShapeDtypeStruct((T, D), F32)] + extra + [jax.ShapeDtypeStruct((1, D), F32)],
        compiler_params=_params(("arbitrary",)),
    )(x, g, dh, res)


def _ffn_up_body(h_ref, wg_ref, wu_ref, a_ref, b_ref, f_ref):
    for cols in _col_chunks(a_ref.shape[-1]):
        a = _dot(h_ref[...], wg_ref[:, cols], NN)
        b = _dot(h_ref[...], wu_ref[:, cols], NN)
        a_ref[:, cols] = a
        b_ref[:, cols] = b
        f_ref[:, cols] = (a * (1.0 / (1.0 + jnp.exp(-a))) * b).astype(f_ref.dtype)


def _ffn_up(h, wg, wu):
    T, D = h.shape
    F = wg.shape[1]
    tm, tn = _pick(T, 1024), _pick(F, MM_CHUNK)
    hs = pl.BlockSpec((tm, D), lambda j, i: (i, 0))
    ws = pl.BlockSpec((D, tn), lambda j, i: (0, j))
    os = pl.BlockSpec((tm, tn), lambda j, i: (i, j))
    return pl.pallas_call(
        functools.partial(_ffn_up_body),
        name="ffn_up_fwd",
        grid=(F // tn, T // tm),
        in_specs=[hs, ws, ws],
        out_specs=[os, os, os],
        out_shape=[jax.ShapeDtypeStruct((T, F), F32), jax.ShapeDtypeStruct((T, F), F32),
                   jax.ShapeDtypeStruct((T, F), BF16)],
        compiler_params=_params(("parallel", "parallel")),
    )(h, wg, wu)


def _ffn_down_dx_body(dx_ref, wd_ref, a_ref, b_ref, after_ref, da_ref, db_ref):
    del after_ref
    for cols in _col_chunks(da_ref.shape[-1]):
        df = _dot(dx_ref[...], wd_ref[cols, :], NT)
        a = a_ref[:, cols]
        s = 1.0 / (1.0 + jnp.exp(-a))
        da_ref[:, cols] = (df * b_ref[:, cols] * (s * (1.0 + a * (1.0 - s)))).astype(da_ref.dtype)
        db_ref[:, cols] = (df * (a * s)).astype(db_ref.dtype)


def _ffn_down_dx(dx2b, wd, a, b, after):
    T, D = dx2b.shape
    F = wd.shape[0]
    tm, tn = _pick(T, 512), _pick(F, 1408)
    xs = pl.BlockSpec((tm, D), lambda j, i: (i, 0))
    ws = pl.BlockSpec((tn, D), lambda j, i: (j, 0))
    os = pl.BlockSpec((tm, tn), lambda j, i: (i, j))
    return pl.pallas_call(
        functools.partial(_ffn_down_dx_body),
        name="ffn_down_dx",
        grid=(F // tn, T // tm),
        in_specs=[xs, ws, os, os, ANY],
        out_specs=[os, os],
        out_shape=[jax.ShapeDtypeStruct((T, F), BF16), jax.ShapeDtypeStruct((T, F), BF16)],
        compiler_params=_params(("parallel", "parallel")),
    )(dx2b, wd, a, b, after)


def _ffn_down_loss_body(nk, inv_d, f_ref, wd_ref, x1_ref, tgt_ref, dx2_ref, dx2b_ref, loss_ref, *scratch):
    j, i, k = pl.program_id(0), pl.program_id(1), pl.program_id(2)
    chunks = _col_chunks(dx2_ref.shape[-1])

    def dot(cols):
        return _dot(f_ref[...], wd_ref[:, cols], NN)

    @pl.when(jnp.logical_and(jnp.logical_and(j == 0, i == 0), k == 0))
    def _():
        loss_ref[...] = jnp.zeros_like(loss_ref)

    def finish(ffn_of):
        total = jnp.zeros((1, 1), F32)
        for cols in chunks:
            e = ffn_of(cols) + x1_ref[:, cols] - tgt_ref[:, cols]
            dx2 = e * inv_d
            dx2_ref[:, cols] = dx2
            dx2b_ref[:, cols] = dx2.astype(BF16)
            total = total + jnp.sum(jnp.sum(e * e, axis=-1, keepdims=True), axis=0, keepdims=True)
        loss_ref[...] += (0.5 * inv_d) * total

    if nk == 1:
        finish(dot)
        return
    acc_ref = scratch[0]

    @pl.when(k == 0)
    def _():
        for cols in chunks:
            acc_ref[:, cols] = dot(cols)

    if nk > 2:
        @pl.when(jnp.logical_and(k > 0, k < nk - 1))
        def _():
            for cols in chunks:
                acc_ref[:, cols] += dot(cols)

    @pl.when(k == nk - 1)
    def _():
        finish(lambda cols: acc_ref[:, cols] + dot(cols))


def _ffn_down_loss(f, wd, x1, tgt):
    T, F = f.shape
    D = wd.shape[1]
    tm, tn, tk = _pick(T, 1024), _pick(D, 1024), _pick(F, 1408)
    gm, gn, gk = T // tm, D // tn, F // tk
    tile = pl.BlockSpec((tm, tn), lambda j, i, k: (i, j))
    return pl.pallas_call(
        functools.partial(_ffn_down_loss_body, gk, 1.0 / D),
        name="ffn_down_loss",
        grid=(gn, gm, gk),
        in_specs=[pl.BlockSpec((tm, tk), lambda j, i, k: (i, k)), pl.BlockSpec((tk, tn), lambda j, i, k: (k, j)),
                  tile, tile],
        out_specs=[tile, tile, pl.BlockSpec((1, 1), lambda j, i, k: (0, 0))],
        out_shape=[jax.ShapeDtypeStruct((T, D), F32), jax.ShapeDtypeStruct((T, D), BF16),
                   jax.ShapeDtypeStruct((1, 1), F32)],
        scratch_shapes=[pltpu.VMEM((tm, tn), F32)] if gk > 1 else [],
        compiler_params=_params(("arbitrary", "arbitrary", "arbitrary")),
    )(f, wd, x1, tgt)


def _lo_mask(shape):
    return lax.broadcasted_iota(jnp.int32, shape, len(shape) - 1) < HEAD_DIM


def _half_sums(t, lo):
    s_lo = jnp.sum(jnp.where(lo, t, 0.0), axis=-1, keepdims=True)
    s_hi = jnp.sum(jnp.where(lo, 0.0, t), axis=-1, keepdims=True)
    return jnp.where(lo, s_lo, s_hi)


def _head_rstd(t, lo):
    return lax.rsqrt(_half_sums(t * t, lo) * (1.0 / HEAD_DIM) + EPS)


def _place(t, lo, kv_head):
    if kv_head == 0:
        t_lo = jnp.where(lo, t, 0.0)
        t_hi = pltpu.roll(t_lo, HEAD_DIM, 1)
    else:
        t_hi = jnp.where(lo, 0.0, t)
        t_lo = pltpu.roll(t_hi, HEAD_DIM, 1)
    return jnp.concatenate([t_lo, t_hi], axis=0).astype(BF16)


def _unplace(c0, c1, lo):
    return jnp.where(lo, c0 + pltpu.roll(c0, HEAD_DIM, 1), c1 + pltpu.roll(c1, HEAD_DIM, 1))


def _band(kv_cur, kv_prev, kg, lo2):
    kb = jnp.concatenate([kv_prev[:, :BLK], kv_cur[:, :BLK]], axis=0)
    vb = jnp.concatenate([kv_prev[:, BLK:], kv_cur[:, BLK:]], axis=0)
    rk = _head_rstd(kb, lo2)
    kn = kb * rk * kg
    kk = [_place(kn, lo2, h) for h in range(N_KV_HEADS)]
    vv = [_place(vb, lo2, h) for h in range(N_KV_HEADS)]
    return kb, rk, kk, vv


def _score_geometry(first_i32):
    qi = lax.broadcasted_iota(jnp.int32, (BLK, 4 * BLK), 0)
    col = lax.broadcasted_iota(jnp.int32, (BLK, 4 * BLK), 1)
    kj = col & (2 * BLK - 1)
    dist = qi + BLK - kj
    valid = (dist >= 0) & (dist < BLK) & (kj >= first_i32 * BLK)
    return col, dist.astype(F32), valid


def _pair_logits(qn, kk, col, distf, valid, slope0, slope1):
    s = _dot(qn.astype(BF16), kk, NT) * (HEAD_DIM ** -0.5)
    slope = jnp.where(col < 2 * BLK, slope0, slope1)
    return jnp.where(valid, s - slope * distf, NEG)


def _pair_probs(qn, kk, col, distf, valid, slope0, slope1, sink0, sink1):
    return _softmax_halves(_pair_logits(qn, kk, col, distf, valid, slope0, slope1), sink0, sink1)


def _softmax_halves(logits, sink0, sink1):
    probs, psink = [], []
    for hh, sk in ((0, sink0), (1, sink1)):
        l = logits[:, 2 * BLK * hh:2 * BLK * (hh + 1)]
        m = jnp.maximum(jnp.max(l, axis=-1, keepdims=True), sk)
        p = jnp.exp(l - m)
        es = jnp.exp(sk - m)
        inv = 1.0 / (jnp.sum(p, axis=-1, keepdims=True) + es)
        probs.append(p * inv)
        psink.append(es * inv)
    return probs, psink


def _gelu(z, with_grad=False):
    cdf = 0.5 * (1.0 + lax.erf(z * (0.5 ** 0.5)))
    if not with_grad:
        return z * cdf
    return z * cdf, cdf + z * jnp.exp(-0.5 * z * z) * ((2.0 * jnp.pi) ** -0.5)


def _tril_w(w):
    r = lax.broadcasted_iota(jnp.int32, (BLK, BLK), 0)
    c = lax.broadcasted_iota(jnp.int32, (BLK, BLK), 1)
    return jnp.where(r >= c, w, 0.0), r >= c


def _gate_fwd_group(zu, zv, lg, lb, w, bcol, with_grad=False):
    u, v = _gelu(zu, with_grad), _gelu(zv, with_grad)
    if with_grad:
        (u, du_dz), (v, dv_dz) = u, v
    mu = jnp.mean(v, axis=-1, keepdims=True)
    vc = v - mu
    rs = lax.rsqrt(jnp.mean(vc * vc, axis=-1, keepdims=True) + EPS)
    vh = vc * rs
    vn = vh * lg + lb
    wt, tril = _tril_w(w)
    mixed = _dot(wt.astype(BF16), vn.astype(BF16), NN) + bcol
    if with_grad:
        return u, vh, rs, vn, wt, tril, mixed, du_dz, dv_dz
    return u, vh, rs, vn, wt, tril, mixed


class _Dims:
    def __init__(self, seq, attn_w, gate_w):
        self.seq, self.attn_w, self.gate_w = seq, attn_w, gate_w
        self.n_heads = attn_w // HEAD_DIM
        self.group = self.n_heads // N_KV_HEADS
        self.n_pairs = attn_w // BLK
        self.n_groups = gate_w // BLK
        self.kv_col = attn_w // (2 * BLK)
        self.u0 = attn_w + 2 * BLK
        self.v0 = self.u0 + gate_w
        self.in_w = self.v0 + gate_w
        self.slopes = [2.0 ** (-8.0 * (h + 1) / self.n_heads) for h in range(self.n_heads)]


def _mixer_fwd_body(d, sink_ref, proj_ref, kvp_ref, qg_ref, kg_ref, lg_ref, lb_ref, w_ref, b_ref, goa_ref, gog_ref,
                    ya_ref, yg_ref, y_ref, logit_scr, prob_scr):
    i = pl.program_id(0)
    first = (i % (d.seq // BLK) == 0).astype(jnp.int32)
    lo = _lo_mask((BLK, BLK))
    lo2 = _lo_mask((2 * BLK, BLK))
    kv_cur = proj_ref[:, d.attn_w:d.attn_w + 2 * BLK]
    _, _, kk, vv = _band(kv_cur, kvp_ref[...], kg_ref[...], lo2)
    col, distf, valid = _score_geometry(first)
    qg = qg_ref[...]
    for j in range(d.n_pairs):
        h0, h1 = 2 * j, 2 * j + 1
        q2 = proj_ref[:, BLK * j:BLK * (j + 1)]
        qn = q2 * _head_rstd(q2, lo) * qg
        logit_scr[j] = _pair_logits(qn, kk[h0 // d.group], col, distf, valid, d.slopes[h0], d.slopes[h1])
    for j in range(d.n_pairs):
        probs, _ = _softmax_halves(logit_scr[j], sink_ref[0, 2 * j], sink_ref[0, 2 * j + 1])
        prob_scr[j] = jnp.concatenate(probs, axis=1).astype(BF16)
    for j in range(d.n_pairs):
        ya_ref[:, BLK * j:BLK * (j + 1)] = _dot(prob_scr[j], vv[2 * j // d.group], NN)
    for g in range(d.n_groups):
        zu = proj_ref[:, d.u0 + BLK * g:d.u0 + BLK * (g + 1)]
        zv = proj_ref[:, d.v0 + BLK * g:d.v0 + BLK * (g + 1)]
        u, _, _, _, _, _, mixed = _gate_fwd_group(zu, zv, lg_ref[g:g + 1, :], lb_ref[g:g + 1, :], w_ref[g], b_ref[g])
        yg_ref[:, BLK * g:BLK * (g + 1)] = u * mixed
    ya = ya_ref[...]
    ra = lax.rsqrt(jnp.mean(ya * ya, axis=-1, keepdims=True) + EPS)
    y_ref[:, :d.attn_w] = (ya * ra * goa_ref[...]).astype(y_ref.dtype)
    yg = yg_ref[...]
    rg = lax.rsqrt(jnp.mean(yg * yg, axis=-1, keepdims=True) + EPS)
    y_ref[:, d.attn_w:] = (yg * rg * gog_ref[...]).astype(y_ref.dtype)


def _mixer_specs(d, T):
    row = lambda w: pl.BlockSpec((BLK, w), lambda i: (i, 0))
    const2 = lambda a: pl.BlockSpec(a.shape, lambda i: (0, 0))
    const3 = lambda a: pl.BlockSpec(a.shape, lambda i: (0, 0, 0))
    kv_prev = pl.BlockSpec((BLK, 2 * BLK), lambda i: (jnp.maximum(i - 1, 0), d.kv_col))
    return row, const2, const3, kv_prev


def _mixer_fwd(d, proj, sinks, qg2, kg2, lg, lb, wsp, bcol, goa, gog):
    T = proj.shape[0]
    row, const2, const3, kv_prev = _mixer_specs(d, T)
    return pl.pallas_call(
        functools.partial(_mixer_fwd_body, d),
        name="mixer_fwd",
        grid=(T // BLK,),
        in_specs=[pl.BlockSpec(memory_space=pltpu.SMEM), row(d.in_w), kv_prev, const2(qg2), const2(kg2),
                  const2(lg), const2(lb), const3(wsp), const3(bcol), const2(goa), const2(gog)],
        out_specs=[row(d.attn_w), row(d.gate_w), row(d.attn_w + d.gate_w)],
        out_shape=[jax.ShapeDtypeStruct((T, d.attn_w), F32), jax.ShapeDtypeStruct((T, d.gate_w), F32),
                   jax.ShapeDtypeStruct((T, d.attn_w + d.gate_w), BF16)],
        scratch_shapes=[pltpu.VMEM((d.n_pairs, BLK, 4 * BLK), F32), pltpu.VMEM((d.n_pairs, BLK, 4 * BLK), BF16)],
        compiler_params=_params(("parallel",)),
    )(sinks, proj, proj, qg2, kg2, lg, lb, wsp, bcol, goa, gog)


def _mixer_bwd_body(d, sink_ref, proj_ref, kvp_ref, ya_ref, yg_ref, dy_ref, qg_ref, kg_ref, lg_ref, lb_ref, w_ref,
                    b_ref, goa_ref, gog_ref,
                    dproj_ref, dkv_ref, dqg_ref, dkg_ref, dsk_ref, dlg_ref, dlb_ref, dw_ref, db_ref, dgoa_ref,
                    dgog_ref):
    i = pl.program_id(0)

    @pl.when(i == 0)
    def _():
        for r in (dqg_ref, dkg_ref, dsk_ref, dlg_ref, dlb_ref, dw_ref, db_ref, dgoa_ref, dgog_ref):
            r[...] = jnp.zeros_like(r)

    first = (i % (d.seq // BLK) == 0).astype(jnp.int32)
    lo = _lo_mask((BLK, BLK))
    lo2 = _lo_mask((2 * BLK, BLK))
    lane_row = lax.broadcasted_iota(jnp.int32, (1, BLK), 1)

    ya = ya_ref[...]
    ra = lax.rsqrt(jnp.mean(ya * ya, axis=-1, keepdims=True) + EPS)
    yah = ya * ra
    dyn = dy_ref[:, :d.attn_w]
    dgoa_ref[...] += jnp.sum(dyn * yah, axis=0, keepdims=True)
    t = dyn * goa_ref[...]
    dya = ra * (t - yah * jnp.mean(t * yah, axis=-1, keepdims=True))
    yg = yg_ref[...]
    rg = lax.rsqrt(jnp.mean(yg * yg, axis=-1, keepdims=True) + EPS)
    ygh = yg * rg
    dyn = dy_ref[:, d.attn_w:]
    dgog_ref[...] += jnp.sum(dyn * ygh, axis=0, keepdims=True)
    t = dyn * gog_ref[...]
    dyg = rg * (t - ygh * jnp.mean(t * ygh, axis=-1, keepdims=True))

    kv_cur = proj_ref[:, d.attn_w:d.attn_w + 2 * BLK]
    kg = kg_ref[...]
    kb, rk, kk, vv = _band(kv_cur, kvp_ref[...], kg, lo2)
    col, distf, valid = _score_geometry(first)
    qg = qg_ref[...]
    ck = [jnp.zeros((BLK, 2 * BLK), F32) for _ in range(N_KV_HEADS)]
    cv = [jnp.zeros((BLK, 2 * BLK), F32) for _ in range(N_KV_HEADS)]
    lo_rows = lax.broadcasted_iota(jnp.int32, (BLK, 2 * BLK), 0) < HEAD_DIM
    dsk = jnp.zeros((1, BLK), F32)
    dqg = jnp.zeros((1, BLK), F32)
    for j in range(d.n_pairs):
        h0, h1 = 2 * j, 2 * j + 1
        kh = h0 // d.group
        cols = slice(BLK * j, BLK * (j + 1))
        q2 = proj_ref[:, cols]
        rq = _head_rstd(q2, lo)
        qh = q2 * rq
        qn = qh * qg
        probs, psink = _pair_probs(qn, kk[kh], col, distf, valid, d.slopes[h0], d.slopes[h1],
                                   sink_ref[0, h0], sink_ref[0, h1])
        do2 = dya[:, cols]
        prod = do2 * ya[:, cols]
        delta = (jnp.sum(jnp.where(lo, prod, 0.0), axis=-1, keepdims=True),
                 jnp.sum(jnp.where(lo, 0.0, prod), axis=-1, keepdims=True))
        do2b = do2.astype(BF16)
        dp = _dot(do2b, vv[kh], NT)
        ds = []
        for hh in (0, 1):
            ds.append(probs[hh] * (dp[:, 2 * BLK * hh:2 * BLK * (hh + 1)] - delta[hh]))
            dsink = -jnp.sum(psink[hh] * delta[hh], axis=0, keepdims=True)
            dsk = dsk + jnp.where(lane_row == (h0 + hh), dsink, 0.0)
        dsb = (jnp.concatenate(ds, axis=1) * (HEAD_DIM ** -0.5)).astype(BF16)
        pb = jnp.concatenate(probs, axis=1).astype(BF16)
        qnb = qn.astype(BF16)
        dqn = _dot(dsb, kk[kh], NN)
        dkk = _dot(qnb, dsb, TN)
        dvv = _dot(do2b, pb, TN)
        ck[kh] = ck[kh] + jnp.where(lo_rows, dkk[:, :2 * BLK], 0.0) + jnp.where(lo_rows, 0.0, dkk[:, 2 * BLK:])
        cv[kh] = cv[kh] + jnp.where(lo_rows, dvv[:, :2 * BLK], 0.0) + jnp.where(lo_rows, 0.0, dvv[:, 2 * BLK:])
        dqg = dqg + jnp.sum(dqn * qh, axis=0, keepdims=True)
        t = dqn * qg
        dq2 = rq * (t - qh * (_half_sums(t * qh, lo) * (1.0 / HEAD_DIM)))
        dproj_ref[:, cols] = dq2.astype(dproj_ref.dtype)
    dsk_ref[...] += dsk
    dqg_ref[...] += dqg
    dkn = _unplace(jnp.transpose(ck[0]), jnp.transpose(ck[1]), lo2)
    dvb = _unplace(jnp.transpose(cv[0]), jnp.transpose(cv[1]), lo2)
    khat = kb * rk
    dkg_ref[...] += jnp.sum(dkn * khat, axis=0, keepdims=True)
    t = dkn * kg
    dkb = rk * (t - khat * (_half_sums(t * khat, lo2) * (1.0 / HEAD_DIM)))
    rows_cur = pl.ds(pl.multiple_of(i * BLK, BLK), BLK)
    rows_prev = pl.ds(pl.multiple_of(jnp.maximum(i - 1, 0) * BLK, BLK), BLK)
    dkv_ref[rows_cur, :] = jnp.concatenate([dkb[BLK:], dvb[BLK:]], axis=1)
    dkv_ref[rows_prev, :] += jnp.concatenate([dkb[:BLK], dvb[:BLK]], axis=1)
    dproj_ref[:, d.attn_w:d.attn_w + 2 * BLK] = jnp.zeros((BLK, 2 * BLK), dproj_ref.dtype)

    for g in range(d.n_groups):
        ucols = slice(d.u0 + BLK * g, d.u0 + BLK * (g + 1))
        vcols = slice(d.v0 + BLK * g, d.v0 + BLK * (g + 1))
        zu = proj_ref[:, ucols]
        zv = proj_ref[:, vcols]
        lg = lg_ref[g:g + 1, :]
        u, vh, rs, vn, wt, tril, mixed, du_dz, dv_dz = _gate_fwd_group(
            zu, zv, lg, lb_ref[g:g + 1, :], w_ref[g], b_ref[g], with_grad=True)
        dyg_g = dyg[:, BLK * g:BLK * (g + 1)]
        du = dyg_g * mixed
        dmix = dyg_g * u
        dmb = dmix.astype(BF16)
        db_ref[g:g + 1, :] += jnp.sum(jnp.transpose(dmix), axis=0, keepdims=True)
        dw_ref[g] += jnp.where(tril, _dot(dmb, vn.astype(BF16), NT), 0.0)
        dvn = _dot(wt.astype(BF16), dmb, TN)
        dlg_ref[g:g + 1, :] += jnp.sum(dvn * vh, axis=0, keepdims=True)
        dlb_ref[g:g + 1, :] += jnp.sum(dvn, axis=0, keepdims=True)
        dvh = dvn * lg
        dv = rs * (dvh - jnp.mean(dvh, axis=-1, keepdims=True) - vh * jnp.mean(dvh * vh, axis=-1, keepdims=True))
        dproj_ref[:, ucols] = (du * du_dz).astype(dproj_ref.dtype)
        dproj_ref[:, vcols] = (dv * dv_dz).astype(dproj_ref.dtype)


def _mixer_bwd(d, proj, ya, yg, dy, sinks, qg2, kg2, lg, lb, wsp, bcol, goa, gog):
    T = proj.shape[0]
    row, const2, const3, kv_prev = _mixer_specs(d, T)
    acc2 = lambda s: pl.BlockSpec(s, lambda i: (0, 0))
    G = d.n_groups
    out_shapes = [((T, d.in_w), BF16), ((T, 2 * BLK), F32), ((1, BLK), F32), ((1, BLK), F32), ((1, BLK), F32),
                  ((G, BLK), F32), ((G, BLK), F32), ((G, BLK, BLK), F32), ((G, BLK), F32),
                  ((1, d.attn_w), F32), ((1, d.gate_w), F32)]
    out_specs = [row(d.in_w)] + [acc2(s) for s, _ in out_shapes[1:7]] + \
                [pl.BlockSpec((G, BLK, BLK), lambda i: (0, 0, 0))] + [acc2(s) for s, _ in out_shapes[8:]]
    return pl.pallas_call(
        functools.partial(_mixer_bwd_body, d),
        name="mixer_bwd",
        grid=(T // BLK,),
        in_specs=[pl.BlockSpec(memory_space=pltpu.SMEM), row(d.in_w), kv_prev, row(d.attn_w), row(d.gate_w),
                  row(d.attn_w + d.gate_w), const2(qg2), const2(kg2), const2(lg), const2(lb), const3(wsp),
                  const3(bcol), const2(goa), const2(gog)],
        out_specs=out_specs,
        out_shape=[jax.ShapeDtypeStruct(s, t) for s, t in out_shapes],
        compiler_params=_params(("arbitrary",)),
    )(sinks, proj, proj, ya, yg, dy, qg2, kg2, lg, lb, wsp, bcol, goa, gog)


def _put_kv_body(dkv_ref, dproj_in_ref, dproj_ref):
    del dproj_in_ref
    dproj_ref[...] = dkv_ref[...].astype(dproj_ref.dtype)


def _put_kv(d, dproj, dkv):
    T = dproj.shape[0]
    tr = _pick(T, 1024, 16)
    return pl.pallas_call(
        functools.partial(_put_kv_body),
        name="put_kv",
        grid=(T // tr,),
        in_specs=[pl.BlockSpec((tr, 2 * BLK), lambda i: (i, 0)), pl.BlockSpec(memory_space=pl.ANY)],
        out_specs=pl.BlockSpec((tr, 2 * BLK), lambda i: (i, d.kv_col)),
        out_shape=jax.ShapeDtypeStruct(dproj.shape, dproj.dtype),
        input_output_aliases={1: 0},
        compiler_params=_params(("parallel",)),
    )(dkv, dproj)


def _add_pair_body(pc_ref, own_ref, got_ref, o_ref):
    del pc_ref
    o_ref[...] = (own_ref[...].astype(F32) + got_ref[...].astype(F32)).astype(o_ref.dtype)


def _add_pair(g4, got, pc, name):
    n, _, h, C = g4.shape
    tr = _pick(h, 512, 16)
    return pl.pallas_call(
        functools.partial(_add_pair_body),
        name=name,
        grid_spec=pltpu.PrefetchScalarGridSpec(
            num_scalar_prefetch=1,
            grid=(n, h // tr),
            in_specs=[pl.BlockSpec((None, None, tr, C), lambda q, i, pc: (q, pc[1], i, 0)),
                      pl.BlockSpec((None, tr, C), lambda q, i, pc: (q, i, 0))],
            out_specs=pl.BlockSpec((None, tr, C), lambda q, i, pc: (q, i, 0)),
        ),
        out_shape=jax.ShapeDtypeStruct((n, h, C), g4.dtype),
        compiler_params=_params(("parallel", "parallel")),
    )(pc, g4, got)


def _adamw_update(w, g, m, v):
    m = ADAM_B1 * m + (1.0 - ADAM_B1) * g
    v = ADAM_B2 * v + (1.0 - ADAM_B2) * (g * g)
    m_hat = m / (1.0 - ADAM_B1 ** ADAM_STEP)
    v_hat = v / (1.0 - ADAM_B2 ** ADAM_STEP)
    return -ADAM_LR * (m_hat / (jnp.sqrt(v_hat) + ADAM_EPS) + ADAM_WD * w), m, v


def _adamw_body(w_ref, g_ref, m_ref, v_ref, d_ref, nm_ref, nv_ref):
    d_ref[...], nm_ref[...], nv_ref[...] = _adamw_update(w_ref[...], g_ref[...], m_ref[...], v_ref[...])


def _adamw(w, g, m, v, name):
    R, C = w.shape
    tr = _pick(R, 512, 8)
    blk = pl.BlockSpec((tr, C), lambda i: (i, 0))
    return pl.pallas_call(
        functools.partial(_adamw_body),
        name=name,
        grid=(R // tr,),
        in_specs=[blk] * 4,
        out_specs=[blk] * 3,
        out_shape=[jax.ShapeDtypeStruct((R, C), F32)] * 3,
        compiler_params=_params(("parallel",)),
    )(w, g, m, v)


def _adamw_halves_body(pc_ref, w_ref, own_ref, got_ref, m_ref, v_ref, g_ref, d_ref, nm_ref, nv_ref):
    mine = pl.program_id(0) == pc_ref[1]

    def update(g):
        g_ref[...] = g
        d_ref[...], nm_ref[...], nv_ref[...] = _adamw_update(w_ref[...], g, m_ref[...], v_ref[...])

    @pl.when(mine)
    def _():
        update(own_ref[...].astype(F32))

    @pl.when(jnp.logical_not(mine))
    def _():
        update(got_ref[...].astype(F32))


def _adamw_halves(w, own, got, m, v, pc, name):
    h, C = own.shape
    tr = _pick(h, 512, 8)
    full = pl.BlockSpec((None, tr, C), lambda hh, i, pc: (hh, i, 0))
    mine = pl.BlockSpec((tr, C), lambda hh, i, pc: (jnp.where(hh == pc[1], i, 0), 0))
    theirs = pl.BlockSpec((tr, C), lambda hh, i, pc: (jnp.where(hh == pc[1], 0, i), 0))
    return pl.pallas_call(
        functools.partial(_adamw_halves_body),
        name=name,
        grid_spec=pltpu.PrefetchScalarGridSpec(
            num_scalar_prefetch=1,
            grid=(2, h // tr),
            in_specs=[full, mine, theirs, full, full],
            out_specs=[full] * 4,
        ),
        out_shape=[jax.ShapeDtypeStruct((2, h, C), F32)] * 4,
        compiler_params=_params(("parallel", "parallel")),
    )(pc, w.reshape(2, h, C), own, got, m.reshape(2, h, C), v.reshape(2, h, C))


def _me():
    x, y, c = lax.axis_index("x"), lax.axis_index("y"), lax.axis_index("c")
    chips = [(1 - x, y), (x, 1 - y), (1 - x, 1 - y)]
    return x, y, c, chips


def _cast_into_body(pc_ref, w_ref, o_ref):
    del pc_ref
    o_ref[...] = w_ref[...].astype(o_ref.dtype)


def _cast_into(w, pc, name, side_by_side=False):
    Rs, C = w.shape
    h = Rs // 2
    tr = _pick(h, 512, 16)
    if side_by_side:
        out_spec = pl.BlockSpec((None, tr, C), lambda hh, i, pc: (hh, i, pc[0]))
        out_shape = jax.ShapeDtypeStruct((2, h, N_CHIPS * C), BF16)
    else:
        out_spec = pl.BlockSpec((None, None, tr, C), lambda hh, i, pc: (pc[0], hh, i, 0))
        out_shape = jax.ShapeDtypeStruct((N_CHIPS, 2, h, C), BF16)
    return pl.pallas_call(
        functools.partial(_cast_into_body),
        name=name,
        grid_spec=pltpu.PrefetchScalarGridSpec(
            num_scalar_prefetch=1,
            grid=(2, h // tr),
            in_specs=[pl.BlockSpec((None, tr, C), lambda hh, i, pc: (hh, i, 0))],
            out_specs=out_spec,
        ),
        out_shape=out_shape,
        compiler_params=_params(("parallel", "parallel")),
    )(pc, w.reshape(2, h, C))


MAX_PIECES = 4


def _send_tile_to_sibling(src_of, dst_of, tr, dst_total, send_sems, recv_sem, last):
    x, y, c, _ = _me()
    pieces = MAX_PIECES if tr % (16 * MAX_PIECES) == 0 else (2 if tr % 32 == 0 else 1)
    n = tr // pieces
    copies = [pltpu.make_async_remote_copy(src_ref=src_of(k * n, n), dst_ref=dst_of(k * n, n), send_sem=send_sems.at[k],
                                           recv_sem=recv_sem, device_id=(x, y, 1 - c), device_id_type=MESH)
              for k in range(pieces)]
    for cp in copies:
        cp.start()
    for cp in copies:
        cp.wait_send()

    @pl.when(last)
    def _():
        pltpu.make_async_remote_copy(src_ref=dst_total, dst_ref=dst_total, send_sem=send_sems.at[0], recv_sem=recv_sem,
                                     device_id=(x, y, 1 - c), device_id_type=MESH).wait_recv()


TILE_SEMS = [pltpu.SemaphoreType.DMA((MAX_PIECES,)), pltpu.SemaphoreType.DMA(())]


def _ag_pair_body(tr, n_i, pc_ref, tile_ref, buf_ref, send_sem, recv_sem):
    j, i = pl.program_id(0), pl.program_id(1)
    q = pc_ref[0] ^ (j + 1)
    c = pc_ref[1]
    r_tile = pl.multiple_of(i * tr, tr)
    last = jnp.logical_and(j == N_CHIPS - 2, i == n_i - 1)
    if len(buf_ref.shape) == 4:
        _send_tile_to_sibling(lambda r0, n: tile_ref.at[:, :, pl.ds(r0, n)],
                              lambda r0, n: buf_ref.at[pl.ds(q, 1), pl.ds(c, 1), pl.ds(r_tile + r0, n)], tr,
                              buf_ref.at[pl.ds(0, N_CHIPS - 1), 0], send_sem, recv_sem, last)
    else:
        cs = buf_ref.shape[2] // N_CHIPS
        cols = pl.ds(pl.multiple_of(q * cs, BLK), cs)
        _send_tile_to_sibling(lambda r0, n: tile_ref.at[:, pl.ds(r0, n)],
                              lambda r0, n: buf_ref.at[pl.ds(c, 1), pl.ds(r_tile + r0, n), cols], tr,
                              buf_ref.at[0, :, pl.ds(0, (N_CHIPS - 1) * cs)], send_sem, recv_sem, last)


def _ag_pair(buf, pc, name):
    if len(buf.shape) == 4:
        _, _, h, C = buf.shape
        tile = lambda tr: pl.BlockSpec((1, 1, tr, C), lambda j, i, pc: (pc[0] ^ (j + 1), pc[1], i, 0))
    else:
        _, h, C = buf.shape
        tile = lambda tr: pl.BlockSpec((1, tr, C // N_CHIPS), lambda j, i, pc: (pc[1], i, pc[0] ^ (j + 1)))
    tr = _pick(h, 512, 16)
    return pl.pallas_call(
        functools.partial(_ag_pair_body, tr, h // tr),
        name=name,
        grid_spec=pltpu.PrefetchScalarGridSpec(
            num_scalar_prefetch=1,
            grid=(N_CHIPS - 1, h // tr),
            in_specs=[tile(tr)],
            out_specs=HBM,
            scratch_shapes=TILE_SEMS,
        ),
        out_shape=jax.ShapeDtypeStruct(buf.shape, buf.dtype),
        input_output_aliases={1: 0},
        compiler_params=_params(("arbitrary", "arbitrary")),
    )(pc, buf)


def _swap_halves_body(tr, n_q, n_i, pc_ref, tile_ref, got_ref, send_sem, recv_sem):
    del pc_ref
    q, i = pl.program_id(0), pl.program_id(1)
    r_tile = pl.multiple_of(i * tr, tr)
    _send_tile_to_sibling(lambda r0, n: tile_ref.at[:, :, pl.ds(r0, n)],
                          lambda r0, n: got_ref.at[pl.ds(q, 1), :, pl.ds(r_tile + r0, n)], tr, got_ref, send_sem, recv_sem,
                          jnp.logical_and(q == n_q - 1, i == n_i - 1))


def _swap_halves(g4, pc, name):
    n, _, h, C = g4.shape
    tr = _pick(h, 512, 16)
    return pl.pallas_call(
        functools.partial(_swap_halves_body, tr, n, h // tr),
        name=name,
        grid_spec=pltpu.PrefetchScalarGridSpec(
            num_scalar_prefetch=1,
            grid=(n, h // tr),
            in_specs=[pl.BlockSpec((1, 1, tr, C), lambda q, i, pc: (q, 1 - pc[1], i, 0))],
            out_specs=HBM,
            scratch_shapes=TILE_SEMS,
        ),
        out_shape=jax.ShapeDtypeStruct((n, 1, h, C), g4.dtype),
        compiler_params=_params(("arbitrary", "arbitrary")),
    )(pc, g4).reshape(n, h, C)


def _ici_copy(src, dst, send_sems, recv_sems, j, chip, c):
    return pltpu.make_async_remote_copy(src_ref=src, dst_ref=dst, send_sem=send_sems.at[j], recv_sem=recv_sems.at[j],
                                        device_id=(chip[0], chip[1], c), device_id_type=MESH)


def _token_spec():
    return jax.ShapeDtypeStruct((8, BLK), F32), pl.BlockSpec(memory_space=pltpu.VMEM)


def _slab(buf_ref, q, c):
    if len(buf_ref.shape) == 4:
        return buf_ref.at[q, c]
    cs = buf_ref.shape[2] // N_CHIPS
    return buf_ref.at[c, :, pl.ds(pl.multiple_of(q * cs, BLK), cs)]


def _ag_start_body(buf_ref, after_ref, send_sems, recv_sems, buf_thru, token_ref):
    del after_ref, buf_thru
    x, y, c, chips = _me()
    mine = _slab(buf_ref, 2 * x + y, c)
    for j, chip in enumerate(chips):
        _ici_copy(mine, mine, send_sems, recv_sems, j, chip, c).start()
    token_ref[...] = jnp.zeros_like(token_ref)


def _ag_start(buf, after, name):
    tok_shape, tok_spec = _token_spec()
    sems = pltpu.SemaphoreType.DMA((N_CHIPS - 1,))
    return pl.pallas_call(
        functools.partial(_ag_start_body),
        name=name,
        in_specs=[HBM, ANY],
        out_specs=[SEM, SEM, HBM, tok_spec],
        out_shape=[sems, sems, pltpu.HBM(buf.shape, buf.dtype), tok_shape],
        input_output_aliases={0: 2},
        compiler_params=pltpu.CompilerParams(has_side_effects=EFFECT),
    )(pltpu.with_memory_space_constraint(buf, pltpu.HBM), after)


def _ag_wait_body(buf_ref, send_sems, recv_sems, after_ref, buf_out):
    del after_ref, buf_out
    x, y, c, chips = _me()
    mine = _slab(buf_ref, 2 * x + y, c)
    for j, chip in enumerate(chips):
        theirs = _slab(buf_ref, 2 * chip[0] + chip[1], c)
        _ici_copy(mine, mine, send_sems, recv_sems, j, chip, c).wait_send()
        _ici_copy(theirs, theirs, send_sems, recv_sems, j, chip, c).wait_recv()


def _ag_wait(buf, send_sems, recv_sems, after, name):
    return pl.pallas_call(
        functools.partial(_ag_wait_body),
        name=name,
        in_specs=[HBM, SEM, SEM, ANY],
        out_specs=HBM,
        out_shape=pltpu.HBM(buf.shape, buf.dtype),
        input_output_aliases={0: 0},
        compiler_params=pltpu.CompilerParams(has_side_effects=EFFECT),
    )(buf, send_sems, recv_sems, after)


def _rs_start_body(pair_ref, land_ref, after_ref, send_sems, recv_sems, pair_thru, land_thru, token_ref):
    del after_ref, pair_thru, land_thru
    x, y, c, chips = _me()
    for j, chip in enumerate(chips):
        _ici_copy(pair_ref.at[2 * chip[0] + chip[1]], land_ref.at[j], send_sems, recv_sems, j, chip, c).start()
    token_ref[...] = jnp.zeros_like(token_ref)


def _rs_start(pair, after, name):
    n, h, C = pair.shape
    tok_shape, tok_spec = _token_spec()
    sems = pltpu.SemaphoreType.DMA((N_CHIPS - 1,))
    land = pltpu.with_memory_space_constraint(lax.empty((N_CHIPS - 1, h, C), pair.dtype), pltpu.HBM)
    return pl.pallas_call(
        functools.partial(_rs_start_body),
        name=name,
        in_specs=[HBM, HBM, ANY],
        out_specs=[SEM, SEM, HBM, HBM, tok_spec],
        out_shape=[sems, sems, pltpu.HBM(pair.shape, pair.dtype), pltpu.HBM(land.shape, land.dtype), tok_shape],
        input_output_aliases={0: 2, 1: 3},
        compiler_params=pltpu.CompilerParams(has_side_effects=EFFECT),
    )(pltpu.with_memory_space_constraint(pair, pltpu.HBM), land, after)


def _rs_wait_body(pair_ref, land_ref, send_sems, recv_sems, after_ref, pair_out, land_out):
    del after_ref, pair_out, land_out
    x, y, c, chips = _me()
    for j, chip in enumerate(chips):
        _ici_copy(pair_ref.at[0], land_ref.at[j], send_sems, recv_sems, j, chip, c).wait_send()
        _ici_copy(pair_ref.at[0], land_ref.at[j], send_sems, recv_sems, j, chip, c).wait_recv()


def _rs_wait(pair, land, send_sems, recv_sems, after, name):
    return pl.pallas_call(
        functools.partial(_rs_wait_body),
        name=name,
        in_specs=[HBM, HBM, SEM, SEM, ANY],
        out_specs=[HBM, HBM],
        out_shape=[pltpu.HBM(pair.shape, pair.dtype), pltpu.HBM(land.shape, land.dtype)],
        input_output_aliases={0: 0, 1: 1},
        compiler_params=pltpu.CompilerParams(has_side_effects=EFFECT),
    )(pair, land, send_sems, recv_sems, after)


def _swap_copy(g4_ref, got_ref, send_sem, recv_sem):
    x, y, c, _ = _me()
    return pltpu.make_async_remote_copy(src_ref=g4_ref.at[:, 1 - c], dst_ref=got_ref, send_sem=send_sem,
                                        recv_sem=recv_sem, device_id=(x, y, 1 - c), device_id_type=MESH)


def _swap_start_body(g4_ref, got_ref, send_sem, recv_sem, g4_thru, got_thru, token_ref):
    del g4_thru, got_thru
    _swap_copy(g4_ref, got_ref, send_sem, recv_sem).start()
    token_ref[...] = jnp.zeros_like(token_ref)


def _swap_start(g4, name):
    n, _, h, C = g4.shape
    tok_shape, tok_spec = _token_spec()
    sem = pltpu.SemaphoreType.DMA(())
    got = pltpu.with_memory_space_constraint(lax.empty((n, h, C), g4.dtype), pltpu.HBM)
    return pl.pallas_call(
        functools.partial(_swap_start_body),
        name=name,
        in_specs=[HBM, HBM],
        out_specs=[SEM, SEM, HBM, HBM, tok_spec],
        out_shape=[sem, sem, pltpu.HBM(g4.shape, g4.dtype), pltpu.HBM(got.shape, got.dtype), tok_shape],
        input_output_aliases={0: 2, 1: 3},
        compiler_params=pltpu.CompilerParams(has_side_effects=EFFECT),
    )(pltpu.with_memory_space_constraint(g4, pltpu.HBM), got)


def _swap_wait_body(g4_ref, got_ref, send_sem, recv_sem, after_ref, g4_out, got_out):
    del after_ref, g4_out, got_out
    cp = _swap_copy(g4_ref, got_ref, send_sem, recv_sem)
    cp.wait_send()
    cp.wait_recv()


def _swap_wait(g4, got, send_sem, recv_sem, after, name):
    return pl.pallas_call(
        functools.partial(_swap_wait_body),
        name=name,
        in_specs=[HBM, HBM, SEM, SEM, ANY],
        out_specs=[HBM, HBM],
        out_shape=[pltpu.HBM(g4.shape, g4.dtype), pltpu.HBM(got.shape, got.dtype)],
        input_output_aliases={0: 0, 1: 1},
        compiler_params=pltpu.CompilerParams(has_side_effects=EFFECT),
    )(g4, got, send_sem, recv_sem, after)


def _add_chips_body(tr, n_i, pc_ref, own_ref, l0_ref, l1_ref, l2_ref, o_ref, got_ref, send_sems, recv_sem):
    del pc_ref
    i = pl.program_id(0)
    r = own_ref[...].astype(F32) + l0_ref[...].astype(F32)
    o_ref[...] = (r + l1_ref[...].astype(F32) + l2_ref[...].astype(F32)).astype(o_ref.dtype)
    r_tile = pl.multiple_of(i * tr, tr)
    _send_tile_to_sibling(lambda r0, n: o_ref.at[pl.ds(r0, n)], lambda r0, n: got_ref.at[pl.ds(r_tile + r0, n)], tr,
                          got_ref, send_sems, recv_sem, i == n_i - 1)


def _add_chips(pair, land, pc, name):
    _, h, C = pair.shape
    tr = _pick(h, 256, 16)
    slot = lambda j: pl.BlockSpec((None, tr, C), lambda i, pc: (j, i, 0))
    return pl.pallas_call(
        functools.partial(_add_chips_body, tr, h // tr),
        name=name,
        grid_spec=pltpu.PrefetchScalarGridSpec(
            num_scalar_prefetch=1,
            grid=(h // tr,),
            in_specs=[pl.BlockSpec((None, tr, C), lambda i, pc: (pc[0], i, 0)), slot(0), slot(1), slot(2)],
            out_specs=[pl.BlockSpec((tr, C), lambda i, pc: (i, 0)), HBM],
            scratch_shapes=TILE_SEMS,
        ),
        out_shape=[jax.ShapeDtypeStruct((h, C), pair.dtype), jax.ShapeDtypeStruct((h, C), pair.dtype)],
        compiler_params=_params(("arbitrary",)),
    )(pc, pair, land, land, land)


def _peer(r):
    x, y, c, _ = _me()
    return (x ^ ((r >> 2) & 1), y ^ ((r >> 1) & 1), c ^ (r & 1))


def _ar_start_body(x_ref, land_ref, send_sems, recv_sems, x_thru, land_thru, token_ref):
    del x_thru, land_thru
    for r in range(1, N_DEV):
        pltpu.make_async_remote_copy(src_ref=x_ref, dst_ref=land_ref.at[r - 1], send_sem=send_sems.at[r - 1],
                                     recv_sem=recv_sems.at[r - 1], device_id=_peer(r), device_id_type=MESH).start()
    token_ref[...] = jnp.zeros_like(token_ref)


def _ar_start(packed):
    tok_shape, tok_spec = _token_spec()
    sems = pltpu.SemaphoreType.DMA((N_DEV - 1,))
    land = pltpu.with_memory_space_constraint(lax.empty((N_DEV - 1,) + packed.shape, packed.dtype), pltpu.HBM)
    return pl.pallas_call(
        functools.partial(_ar_start_body),
        name="ar_start",
        in_specs=[HBM, HBM],
        out_specs=[SEM, SEM, HBM, HBM, tok_spec],
        out_shape=[sems, sems, pltpu.HBM(packed.shape, packed.dtype), pltpu.HBM(land.shape, land.dtype), tok_shape],
        input_output_aliases={0: 2, 1: 3},
        compiler_params=pltpu.CompilerParams(has_side_effects=EFFECT),
    )(pltpu.with_memory_space_constraint(packed, pltpu.HBM), land)


def _ar_wait_body(x_ref, land_ref, send_sems, recv_sems, after_ref, x_out, land_out):
    del after_ref, x_out, land_out
    for r in range(1, N_DEV):
        cp = pltpu.make_async_remote_copy(src_ref=x_ref, dst_ref=land_ref.at[r - 1], send_sem=send_sems.at[r - 1],
                                          recv_sem=recv_sems.at[r - 1], device_id=_peer(r), device_id_type=MESH)
        cp.wait_send()
        cp.wait_recv()


def _ar_wait(packed, land, send_sems, recv_sems, after):
    return pl.pallas_call(
        functools.partial(_ar_wait_body),
        name="ar_wait",
        in_specs=[HBM, HBM, SEM, SEM, ANY],
        out_specs=[HBM, HBM],
        out_shape=[pltpu.HBM(packed.shape, packed.dtype), pltpu.HBM(land.shape, land.dtype)],
        input_output_aliases={0: 0, 1: 1},
        compiler_params=pltpu.CompilerParams(has_side_effects=EFFECT),
    )(packed, land, send_sems, recv_sems, after)


def _ar_sum_body(me_ref, own_ref, *rest):
    o_ref = rest[N_DEV]
    acc = None
    for dev in range(N_DEV):
        term = jnp.where(me_ref[0] == dev, own_ref[...], rest[dev][...])
        acc = term if acc is None else acc + term
    o_ref[...] = acc


def _ar_sum(packed, land, me):
    R, C = packed.shape
    tr = _pick(R, 552, 8)
    own = pl.BlockSpec((tr, C), lambda i, me: (i, 0))
    slot = lambda dev: pl.BlockSpec((None, tr, C), lambda i, me: (jnp.maximum((dev ^ me[0]) - 1, 0), i, 0))
    return pl.pallas_call(
        functools.partial(_ar_sum_body),
        name="ar_sum",
        grid_spec=pltpu.PrefetchScalarGridSpec(
            num_scalar_prefetch=1,
            grid=(R // tr,),
            in_specs=[own] + [slot(dev) for dev in range(N_DEV)],
            out_specs=pl.BlockSpec((tr, C), lambda i, me: (i, 0)),
        ),
        out_shape=jax.ShapeDtypeStruct((R, C), F32),
        compiler_params=_params(("parallel",)),
    )(me, packed, *([land] * N_DEV))


def _pack(arrays):
    rows = []
    for a in arrays:
        flat = a.reshape(-1).astype(F32)
        pad = (-flat.shape[0]) % BLK
        rows.append(jnp.pad(flat, (0, pad)).reshape(-1, BLK))
    packed = jnp.concatenate(rows, axis=0)
    pad = (-packed.shape[0]) % 8
    return jnp.pad(packed, ((0, pad), (0, 0)))


def _unpack(packed, shapes):
    out, r = [], 0
    for s in shapes:
        n = 1
        for k in s:
            n *= k
        nr = -(-n // BLK)
        out.append(packed[r:r + nr].reshape(-1)[:n].reshape(s))
        r += nr
    return out


def kernel(x, norm1_g, w_in, q_norm_g, k_norm_g, attn_sinks, gate_ln_g, gate_ln_b, w_spatial, b_spatial, out_norm_attn_g, out_norm_gate_g, w_out, norm2_g, w_ffn_gate, w_ffn_up, w_ffn_down, loss_target, m_norm1_g, m_w_in, m_q_norm_g, m_k_norm_g, m_attn_sinks, m_gate_ln_g, m_gate_ln_b, m_w_spatial, m_b_spatial, m_out_norm_attn_g, m_out_norm_gate_g, m_w_out, m_norm2_g, m_w_ffn_gate, m_w_ffn_up, m_w_ffn_down, v_norm1_g, v_w_in, v_q_norm_g, v_k_norm_g, v_attn_sinks, v_gate_ln_g, v_gate_ln_b, v_w_spatial, v_b_spatial, v_out_norm_attn_g, v_out_norm_gate_g, v_w_out, v_norm2_g, v_w_ffn_gate, v_w_ffn_up, v_w_ffn_down):
    bl, seq, D = x.shape
    T = bl * seq
    attn_w, gate_w = out_norm_attn_g.shape[1], out_norm_gate_g.shape[1]
    d = _Dims(seq, attn_w, gate_w)
    G = d.n_groups
    in_w = d.in_w
    dff = w_ffn_gate.shape[2] * N_CHIPS
    assert w_in.shape[2] * N_CHIPS == in_w and seq % BLK == 0 and attn_w % (2 * BLK) == 0

    pc = jnp.stack([2 * lax.axis_index("x") + lax.axis_index("y"), lax.axis_index("c")]).astype(jnp.int32)
    big = [w_in[0], w_out[0], w_ffn_gate[0], w_ffn_up[0], w_ffn_down[0]]
    names = ["in", "out", "gate", "up", "down"]
    xf = x.reshape(T, D)
    tgt = loss_target.reshape(T, D)
    send, recv, buf, behind = _ag_start(_cast_into(big[0], pc, "cast_in"), norm1_g, "ag_start_in")
    started = [(send, recv, buf)]
    h1 = _rms_fwd(xf, norm1_g, "norm1_fwd", after=behind)
    behind = h1
    for w, n in zip(big[1:], names[1:]):
        buf = _cast_into(w, pc, "cast_" + n, side_by_side=n in ("gate", "up"))
        send, recv, buf, behind = _ag_start(buf, behind, "ag_start_" + n)
        started.append((send, recv, buf))

    def gathered(k, after):
        send, recv, buf = started[k]
        buf = _ag_wait(buf, send, recv, after, "ag_wait_" + names[k])
        buf = _ag_pair(buf, pc, "ag_pair_" + names[k])
        rs, cs = big[k].shape
        return buf.reshape(rs, N_CHIPS * cs) if len(buf.shape) == 3 else buf.reshape(N_CHIPS, rs, cs)

    qg2 = jnp.tile(q_norm_g, (1, 2))
    kg2 = jnp.tile(k_norm_g, (1, 2))
    lg, lb, wsp = gate_ln_g[0], gate_ln_b[0], w_spatial[0]
    bcol = jnp.broadcast_to(b_spatial[0][:, :, None], (G, BLK, BLK))

    win_full = jnp.transpose(gathered(0, behind), (1, 0, 2)).reshape(D, in_w)
    proj = _matmul(h1, win_full, "nn", F32, "proj_fwd", tm=512, tn=3328)
    ya, yg, yn = _mixer_fwd(d, proj, attn_sinks, qg2, kg2, lg, lb, wsp, bcol, out_norm_attn_g, out_norm_gate_g)
    wout_full = gathered(1, yn).reshape(attn_w + gate_w, D)
    x1 = _matmul(yn, wout_full, "nn", F32, "out_fwd", tm=1024, tn=1024, add=xf)
    h2 = _rms_fwd(x1, norm2_g, "norm2_fwd")
    wg_full, wu_full = gathered(2, h2), gathered(3, h2)
    a, b, f = _ffn_up(h2, wg_full, wu_full)
    wd_full = gathered(4, f).reshape(dff, D)
    dx2, dx2b, loss_local = _ffn_down_loss(f, wd_full, x1, tgt)

    def swap_start(g, n):
        g4 = g.reshape(N_CHIPS, 2, g.shape[1] // 2, g.shape[2])
        return _swap_start(g4, "rs_swap_start_" + n)

    def reduce_start(swapping, n, after):
        send, recv, g4, got, _ = swapping
        g4, got = _swap_wait(g4, got, send, recv, after, "rs_swap_wait_" + n)
        return _rs_start(_add_pair(g4, got, pc, "rs_add_pair_" + n), got, "rs_start_" + n)

    reducing = {}
    g_d = _matmul(f, dx2b, "tn", BF16, "ffn_down_dw", tm=1408, tn=1024, tk=2048, out_slab="r")
    swap_d = swap_start(g_d, "down")
    da, db = _ffn_down_dx(dx2b, wd_full, a, b, swap_d[4])
    g_g = _matmul(h2, da, "tn", BF16, "ffn_gate_dw", tm=1024, tn=1408, tk=2048, out_slab="c")
    swap_g = swap_start(g_g, "gate")
    reducing["down"] = reduce_start(swap_d, "down", swap_g[4])
    g_u = _matmul(h2, db, "tn", BF16, "ffn_up_dw", tm=1024, tn=1408, tk=2048, out_slab="c",
                  after=reducing["down"][4])
    swap_u = swap_start(g_u, "up")
    reducing["gate"] = reduce_start(swap_g, "gate", swap_u[4])
    dh2 = _matmul(da, wg_full, "nt", F32, "ffn_gate_dx", tm=1024, tn=1024, tk=2816, after=reducing["gate"][4])
    dh2 = _matmul(db, wu_full, "nt", F32, "ffn_up_dx", tm=1024, tn=1024, tk=2816, add=dh2)
    reducing["up"] = reduce_start(swap_u, "up", dh2)
    dx1, dx1b, dg_norm2 = _rms_bwd(x1, norm2_g, dh2, dx2, "norm2_bwd", True)
    g_o = _matmul(yn, dx1b, "tn", BF16, "out_dw", tm=512, tn=1024, tk=2048, out_slab="r",
                  after=reducing["up"][4])
    swap_o = swap_start(g_o, "out")
    dy = _matmul(dx1b, wout_full, "nt", F32, "out_dx", tm=1024, tn=1024, after=swap_o[4])
    (dproj, dkv, dqg, dkg, dsk, dlg, dlb, dwsp, dbsp, dgoa, dgog) = _mixer_bwd(
        d, proj, ya, yg, dy, attn_sinks, qg2, kg2, lg, lb, wsp, bcol, out_norm_attn_g, out_norm_gate_g)
    dproj = _put_kv(d, dproj, dkv)
    reducing["out"] = reduce_start(swap_o, "out", dproj)
    g_in_full = _matmul(h1, dproj, "tn", BF16, "proj_dw", tm=512, tn=3328, tk=2048,
                        after=reducing["out"][4])
    g_i = jnp.transpose(g_in_full.reshape(D, N_CHIPS, in_w // N_CHIPS), (1, 0, 2))
    g4_i = g_i.reshape(N_CHIPS, 2, D // 2, in_w // N_CHIPS)
    pair_i = _add_pair(g4_i, _swap_halves(g4_i, pc, "rs_swap_in"), pc, "rs_add_pair_in")
    reducing["in"] = _rs_start(pair_i, g_i, "rs_start_in")
    dh1 = _matmul(dproj, win_full, "nt", F32, "proj_dx", tm=1024, tn=1024, after=reducing["in"][4])
    dx, dg_norm1 = _rms_bwd(xf, norm1_g, dh1, dx1, "norm1_bwd", False)

    dqg64 = dqg[:, :HEAD_DIM] + dqg[:, HEAD_DIM:]
    dkg64 = dkg[:, :HEAD_DIM] + dkg[:, HEAD_DIM:]
    small_g_local = [dg_norm1, dqg64, dkg64, dsk[:, :d.n_heads], dlg, dlb, dwsp, dbsp, dgoa, dgog, dg_norm2,
                     loss_local]
    ar_send, ar_recv, ar_own, ar_land, ar_token = _ar_start(_pack(small_g_local))

    big_m = [m_w_in[0], m_w_out[0], m_w_ffn_gate[0], m_w_ffn_up[0], m_w_ffn_down[0]]
    big_v = [v_w_in[0], v_w_out[0], v_w_ffn_gate[0], v_w_ffn_up[0], v_w_ffn_down[0]]
    big_grads, big_d, big_nm, big_nv = [], [], [], []
    for w, m, v, n in zip(big, big_m, big_v, names):
        send, recv, pair, land, _ = reducing[n]
        pair, land = _rs_wait(pair, land, send, recv, ar_token, "rs_wait_" + n)
        own, got = _add_chips(pair, land, pc, "rs_add_chips_" + n)
        outs = _adamw_halves(w, own, got, m, v, pc, "adamw_" + n)
        for lst, o in zip((big_grads, big_d, big_nm, big_nv), outs):
            lst.append(o.reshape(w.shape))

    small_names_w = [norm1_g, q_norm_g, k_norm_g, attn_sinks, gate_ln_g, gate_ln_b, w_spatial, b_spatial,
                     out_norm_attn_g, out_norm_gate_g, norm2_g]
    small_m = [m_norm1_g, m_q_norm_g, m_k_norm_g, m_attn_sinks, m_gate_ln_g, m_gate_ln_b, m_w_spatial, m_b_spatial,
               m_out_norm_attn_g, m_out_norm_gate_g, m_norm2_g]
    small_v = [v_norm1_g, v_q_norm_g, v_k_norm_g, v_attn_sinks, v_gate_ln_g, v_gate_ln_b, v_w_spatial, v_b_spatial,
               v_out_norm_attn_g, v_out_norm_gate_g, v_norm2_g]
    shapes = [w.shape for w in small_names_w] + [loss_local.shape]
    ride = [jnp.zeros(loss_local.shape, F32)]
    ar_own, ar_land = _ar_wait(ar_own, ar_land, ar_send, ar_recv, big_nv[-1])
    me = (4 * lax.axis_index("x") + 2 * lax.axis_index("y") + lax.axis_index("c")).astype(jnp.int32).reshape(1)
    sg = _ar_sum(ar_own, ar_land, me)
    sd, snm, snv = _adamw(_pack(small_names_w + ride), sg, _pack(small_m + ride), _pack(small_v + ride), "adamw_small")
    small_g, small_d, small_nm, small_nv = (_unpack(t, shapes) for t in (sg, sd, snm, snv))
    loss = small_g[-1][0, 0]

    def order(small, bigs):
        s = list(small)
        bg = [t[None] for t in bigs]
        return [s[0], bg[0], s[1], s[2], s[3], s[4], s[5], s[6], s[7], s[8], s[9], bg[1], s[10], bg[2], bg[3], bg[4]]

    grad_x = dx.reshape(bl, seq, D)
    return (loss, grad_x, *order(small_g, big_grads), *order(small_d, big_d), *order(small_nm, big_nm),
            *order(small_nv, big_nv))
```

```python
import functools

import jax
import jax.numpy as jnp
from jax import lax
from jax.experimental import pallas as pl
from jax.experimental.pallas import tpu as pltpu

F32 = jnp.float32
BF16 = jnp.bfloat16
MESH = pl.DeviceIdType.MESH

EPS = 1e-6
HEAD_DIM = 64
N_KV_HEADS = 2
BLK = 128
N_CHIPS = 4
N_DEV = 8
NEG = -1e30

ADAM_LR = 0.001
ADAM_B1 = 0.9
ADAM_B2 = 0.999
ADAM_EPS = 1e-08
ADAM_WD = 0.01
ADAM_STEP = 10

VMEM_LIMIT = 56 * 1024 * 1024

NN = (((1,), (0,)), ((), ()))
NT = (((1,), (1,)), ((), ()))
TN = (((0,), (0,)), ((), ()))
HBM = pl.BlockSpec(memory_space=pltpu.HBM)
ANY = pl.BlockSpec(memory_space=pl.ANY)
SEM = pl.BlockSpec(memory_space=pltpu.SEMAPHORE)
EFFECT = pltpu.SideEffectType.DATAFLOW_SIDE_EFFECTING


def _dot(a, b, dn):
    return lax.dot_general(a, b, dn, preferred_element_type=F32)


def _pick(dim, pref, align=128):
    if dim <= pref:
        return dim
    t = (pref // align) * align
    while t >= align:
        if dim % t == 0:
            return t
        t -= align
    return dim


def _params(sem):
    return pltpu.CompilerParams(dimension_semantics=sem, vmem_limit_bytes=VMEM_LIMIT)


MM_CHUNK = 512


def _col_chunks(tn):
    return [slice(c0, min(c0 + MM_CHUNK, tn)) for c0 in range(0, tn, MM_CHUNK)]


def _mm_body(dn, nk, has_add, has_after, *refs):
    a_ref, b_ref = refs[:2]
    add_ref = refs[2] if has_add else None
    o_ref = refs[2 + has_add + has_after]
    chunks = _col_chunks(o_ref.shape[-1])

    def dot(cols):
        return _dot(a_ref[...], b_ref[cols, :] if dn == NT else b_ref[:, cols], dn)

    def finish(cols, r):
        if add_ref is not None:
            r = r + add_ref[:, cols]
        o_ref[:, cols] = r.astype(o_ref.dtype)

    if nk == 1:
        for cols in chunks:
            finish(cols, dot(cols))
        return
    acc_ref = refs[-1]
    k = pl.program_id(2)

    @pl.when(k == 0)
    def _():
        for cols in chunks:
            acc_ref[:, cols] = dot(cols)

    if nk > 2:
        @pl.when(jnp.logical_and(k > 0, k < nk - 1))
        def _():
            for cols in chunks:
                acc_ref[:, cols] += dot(cols)

    @pl.when(k == nk - 1)
    def _():
        for cols in chunks:
            finish(cols, acc_ref[:, cols] + dot(cols))


def _matmul(a, b, mode, out_dtype, name, *, tm, tn, tk=None, add=None, out_slab=None, after=None):
    if mode == "nn":
        (M, K), N = a.shape, b.shape[1]
    elif mode == "nt":
        (M, K), N = a.shape, b.shape[0]
    else:
        (K, M), N = a.shape, b.shape[1]
    tk = K if tk is None else tk
    tm, tn, tk = _pick(M, tm), _pick(N, tn), _pick(K, tk)
    if out_slab == "c":
        tn = _pick(N // N_CHIPS, tn)
    if out_slab == "r":
        tm = _pick(M // N_CHIPS, tm)
    gm, gn, gk = M // tm, N // tn, K // tk

    if mode == "tn":
        a_spec = pl.BlockSpec((tk, tm), lambda j, i, k: (k, i))
        b_spec = pl.BlockSpec((tk, tn), lambda j, i, k: (k, j))
    else:
        a_spec = pl.BlockSpec((tm, tk), lambda j, i, k: (i, k))
        if mode == "nn":
            b_spec = pl.BlockSpec((tk, tn), lambda j, i, k: (k, j))
        else:
            b_spec = pl.BlockSpec((tn, tk), lambda j, i, k: (j, k))

    if out_slab == "c":
        per = (N // N_CHIPS) // tn
        o_spec = pl.BlockSpec((None, tm, tn), lambda j, i, k: (j // per, i, j % per))
        o_shape = jax.ShapeDtypeStruct((N_CHIPS, M, N // N_CHIPS), out_dtype)
    elif out_slab == "r":
        per = (M // N_CHIPS) // tm
        o_spec = pl.BlockSpec((None, tm, tn), lambda j, i, k: (i // per, i % per, j))
        o_shape = jax.ShapeDtypeStruct((N_CHIPS, M // N_CHIPS, N), out_dtype)
    else:
        o_spec = pl.BlockSpec((tm, tn), lambda j, i, k: (i, j))
        o_shape = jax.ShapeDtypeStruct((M, N), out_dtype)

    dn = {"nn": NN, "nt": NT, "tn": TN}[mode]
    in_specs = [a_spec, b_spec]
    args = [a, b]
    if add is not None:
        in_specs.append(pl.BlockSpec((tm, tn), lambda j, i, k: (i, j)))
        args.append(add)
    if after is not None:
        in_specs.append(ANY)
        args.append(after)
    return pl.pallas_call(
        functools.partial(_mm_body, dn, gk, add is not None, after is not None),
        name=name,
        grid=(gn, gm, gk),
        in_specs=in_specs,
        out_specs=o_spec,
        out_shape=o_shape,
        scratch_shapes=[pltpu.VMEM((tm, tn), F32)] if gk > 1 else [],
        compiler_params=_params(("parallel", "parallel", "arbitrary")),
    )(*args)


def _rms_fwd_body(x_ref, g_ref, *rest):
    h_ref = rest[-1]
    x = x_ref[...]
    r = lax.rsqrt(jnp.mean(x * x, axis=-1, keepdims=True) + EPS)
    h_ref[...] = (x * r * g_ref[...]).astype(h_ref.dtype)


def _rms_fwd(x, g, name, after=None):
    T, D = x.shape
    tr = _pick(T, 256, 16)
    extra = [] if after is None else [after]
    return pl.pallas_call(
        functools.partial(_rms_fwd_body),
        name=name,
        grid=(T // tr,),
        in_specs=[pl.BlockSpec((tr, D), lambda i: (i, 0)), pl.BlockSpec((1, D), lambda i: (0, 0))] + [ANY] * len(extra),
        out_specs=pl.BlockSpec((tr, D), lambda i: (i, 0)),
        out_shape=jax.ShapeDtypeStruct((T, D), BF16),
        compiler_params=_params(("parallel",)),
    )(x, g, *extra)


def _rms_bwd_body(with_bf16, x_ref, g_ref, dh_ref, res_ref, dx_ref, *rest):
    dg_ref = rest[-1]

    @pl.when(pl.program_id(0) == 0)
    def _():
        dg_ref[...] = jnp.zeros_like(dg_ref)

    x = x_ref[...]
    r = lax.rsqrt(jnp.mean(x * x, axis=-1, keepdims=True) + EPS)
    xh = x * r
    dh = dh_ref[...]
    dg_ref[...] += jnp.sum(dh * xh, axis=0, keepdims=True)
    t = dh * g_ref[...]
    dx = res_ref[...] + r * (t - xh * jnp.mean(t * xh, axis=-1, keepdims=True))
    dx_ref[...] = dx
    if with_bf16:
        rest[0][...] = dx.astype(BF16)


def _rms_bwd(x, g, dh, res, name, with_bf16):
    T, D = x.shape
    tr = _pick(T, 256, 16)
    row = pl.BlockSpec((tr, D), lambda i: (i, 0))
    vec = pl.BlockSpec((1, D), lambda i: (0, 0))
    extra = [jax.ShapeDtypeStruct((T, D), BF16)] if with_bf16 else []
    return pl.pallas_call(
        functools.partial(_rms_bwd_body, with_bf16),
        name=name,
        grid=(T // tr,),
        in_specs=[row, vec, row, row],
        out_specs=[row] + [row] * len(extra) + [vec],
        out_shape=[jax.ShapeDtypeStruct((T, D), F32)] + extra + [jax.ShapeDtypeStruct((1, D), F32)],
        compiler_params=_params(("arbitrary",)),
    )(x, g, dh, res)


def _ffn_up_body(h_ref, wg_ref, wu_ref, a_ref, b_ref, f_ref):
    for cols in _col_chunks(a_ref.shape[-1]):
        a = _dot(h_ref[...], wg_ref[:, cols], NN)
        b = _dot(h_ref[...], wu_ref[:, cols], NN)
        a_ref[:, cols] = a
        b_ref[:, cols] = b
        f_ref[:, cols] = (a * (1.0 / (1.0 + jnp.exp(-a))) * b).astype(f_ref.dtype)


def _ffn_up(h, wg, wu):
    T, D = h.shape
    F = wg.shape[1]
    tm, tn = _pick(T, 1024), _pick(F, MM_CHUNK)
    hs = pl.BlockSpec((tm, D), lambda j, i: (i, 0))
    ws = pl.BlockSpec((D, tn), lambda j, i: (0, j))
    os = pl.BlockSpec((tm, tn), lambda j, i: (i, j))
    return pl.pallas_call(
        functools.partial(_ffn_up_body),
        name="ffn_up_fwd",
        grid=(F // tn, T // tm),
        in_specs=[hs, ws, ws],
        out_specs=[os, os, os],
        out_shape=[jax.ShapeDtypeStruct((T, F), F32), jax.ShapeDtypeStruct((T, F), F32),
                   jax.ShapeDtypeStruct((T, F), BF16)],
        compiler_params=_params(("parallel", "parallel")),
    )(h, wg, wu)


def _ffn_down_dx_body(dx_ref, wd_ref, a_ref, b_ref, after_ref, da_ref, db_ref):
    del after_ref
    for cols in _col_chunks(da_ref.shape[-1]):
        df = _dot(dx_ref[...], wd_ref[cols, :], NT)
        a = a_ref[:, cols]
        s = 1.0 / (1.0 + jnp.exp(-a))
        da_ref[:, cols] = (df * b_ref[:, cols] * (s * (1.0 + a * (1.0 - s)))).astype(da_ref.dtype)
        db_ref[:, cols] = (df * (a * s)).astype(db_ref.dtype)


def _ffn_down_dx(dx2b, wd, a, b, after):
    T, D = dx2b.shape
    F = wd.shape[0]
    tm, tn = _pick(T, 512), _pick(F, 1408)
    xs = pl.BlockSpec((tm, D), lambda j, i: (i, 0))
    ws = pl.BlockSpec((tn, D), lambda j, i: (j, 0))
    os = pl.BlockSpec((tm, tn), lambda j, i: (i, j))
    return pl.pallas_call(
        functools.partial(_ffn_down_dx_body),
        name="ffn_down_dx",
        grid=(F // tn, T // tm),
        in_specs=[xs, ws, os, os, ANY],
        out_specs=[os, os],
        out_shape=[jax.ShapeDtypeStruct((T, F), BF16), jax.ShapeDtypeStruct((T, F), BF16)],
        compiler_params=_params(("parallel", "parallel")),
    )(dx2b, wd, a, b, after)


def _ffn_down_loss_body(nk, inv_d, f_ref, wd_ref, x1_ref, tgt_ref, dx2_ref, dx2b_ref, loss_ref, *scratch):
    j, i, k = pl.program_id(0), pl.program_id(1), pl.program_id(2)
    chunks = _col_chunks(dx2_ref.shape[-1])

    def dot(cols):
        return _dot(f_ref[...], wd_ref[:, cols], NN)

    @pl.when(jnp.logical_and(jnp.logical_and(j == 0, i == 0), k == 0))
    def _():
        loss_ref[...] = jnp.zeros_like(loss_ref)

    def finish(ffn_of):
        total = jnp.zeros((1, 1), F32)
        for cols in chunks:
            e = ffn_of(cols) + x1_ref[:, cols] - tgt_ref[:, cols]
            dx2 = e * inv_d
            dx2_ref[:, cols] = dx2
            dx2b_ref[:, cols] = dx2.astype(BF16)
            total = total + jnp.sum(jnp.sum(e * e, axis=-1, keepdims=True), axis=0, keepdims=True)
        loss_ref[...] += (0.5 * inv_d) * total

    if nk == 1:
        finish(dot)
        return
    acc_ref = scratch[0]

    @pl.when(k == 0)
    def _():
        for cols in chunks:
            acc_ref[:, cols] = dot(cols)

    if nk > 2:
        @pl.when(jnp.logical_and(k > 0, k < nk - 1))
        def _():
            for cols in chunks:
                acc_ref[:, cols] += dot(cols)

    @pl.when(k == nk - 1)
    def _():
        finish(lambda cols: acc_ref[:, cols] + dot(cols))


def _ffn_down_loss(f, wd, x1, tgt):
    T, F = f.shape
    D = wd.shape[1]
    tm, tn, tk = _pick(T, 1024), _pick(D, 1024), _pick(F, 1408)
    gm, gn, gk = T // tm, D // tn, F // tk
    tile = pl.BlockSpec((tm, tn), lambda j, i, k: (i, j))
    return pl.pallas_call(
        functools.partial(_ffn_down_loss_body, gk, 1.0 / D),
        name="ffn_down_loss",
        grid=(gn, gm, gk),
        in_specs=[pl.BlockSpec((tm, tk), lambda j, i, k: (i, k)), pl.BlockSpec((tk, tn), lambda j, i, k: (k, j)),
                  tile, tile],
        out_specs=[tile, tile, pl.BlockSpec((1, 1), lambda j, i, k: (0, 0))],
        out_shape=[jax.ShapeDtypeStruct((T, D), F32), jax.ShapeDtypeStruct((T, D), BF16),
                   jax.ShapeDtypeStruct((1, 1), F32)],
        scratch_shapes=[pltpu.VMEM((tm, tn), F32)] if gk > 1 else [],
        compiler_params=_params(("arbitrary", "arbitrary", "arbitrary")),
    )(f, wd, x1, tgt)


def _lo_mask(shape):
    return lax.broadcasted_iota(jnp.int32, shape, len(shape) - 1) < HEAD_DIM


def _half_sums(t, lo):
    s_lo = jnp.sum(jnp.where(lo, t, 0.0), axis=-1, keepdims=True)
    s_hi = jnp.sum(jnp.where(lo, 0.0, t), axis=-1, keepdims=True)
    return jnp.where(lo, s_lo, s_hi)


def _head_rstd(t, lo):
    return lax.rsqrt(_half_sums(t * t, lo) * (1.0 / HEAD_DIM) + EPS)


def _place(t, lo, kv_head):
    if kv_head == 0:
        t_lo = jnp.where(lo, t, 0.0)
        t_hi = pltpu.roll(t_lo, HEAD_DIM, 1)
    else:
        t_hi = jnp.where(lo, 0.0, t)
        t_lo = pltpu.roll(t_hi, HEAD_DIM, 1)
    return jnp.concatenate([t_lo, t_hi], axis=0).astype(BF16)


def _unplace(c0, c1, lo):
    return jnp.where(lo, c0 + pltpu.roll(c0, HEAD_DIM, 1), c1 + pltpu.roll(c1, HEAD_DIM, 1))


def _band(kv_cur, kv_prev, kg, lo2):
    kb = jnp.concatenate([kv_prev[:, :BLK], kv_cur[:, :BLK]], axis=0)
    vb = jnp.concatenate([kv_prev[:, BLK:], kv_cur[:, BLK:]], axis=0)
    rk = _head_rstd(kb, lo2)
    kn = kb * rk * kg
    kk = [_place(kn, lo2, h) for h in range(N_KV_HEADS)]
    vv = [_place(vb, lo2, h) for h in range(N_KV_HEADS)]
    return kb, rk, kk, vv


def _score_geometry(first_i32):
    qi = lax.broadcasted_iota(jnp.int32, (BLK, 4 * BLK), 0)
    col = lax.broadcasted_iota(jnp.int32, (BLK, 4 * BLK), 1)
    kj = col & (2 * BLK - 1)
    dist = qi + BLK - kj
    valid = (dist >= 0) & (dist < BLK) & (kj >= first_i32 * BLK)
    return col, dist.astype(F32), valid


def _pair_logits(qn, kk, col, distf, valid, slope0, slope1):
    s = _dot(qn.astype(BF16), kk, NT) * (HEAD_DIM ** -0.5)
    slope = jnp.where(col < 2 * BLK, slope0, slope1)
    return jnp.where(valid, s - slope * distf, NEG)


def _pair_probs(qn, kk, col, distf, valid, slope0, slope1, sink0, sink1):
    return _softmax_halves(_pair_logits(qn, kk, col, distf, valid, slope0, slope1), sink0, sink1)


def _softmax_halves(logits, sink0, sink1):
    probs, psink = [], []
    for hh, sk in ((0, sink0), (1, sink1)):
        l = logits[:, 2 * BLK * hh:2 * BLK * (hh + 1)]
        m = jnp.maximum(jnp.max(l, axis=-1, keepdims=True), sk)
        p = jnp.exp(l - m)
        es = jnp.exp(sk - m)
        inv = 1.0 / (jnp.sum(p, axis=-1, keepdims=True) + es)
        probs.append(p * inv)
        psink.append(es * inv)
    return probs, psink


def _gelu(z, with_grad=False):
    cdf = 0.5 * (1.0 + lax.erf(z * (0.5 ** 0.5)))
    if not with_grad:
        return z * cdf
    return z * cdf, cdf + z * jnp.exp(-0.5 * z * z) * ((2.0 * jnp.pi) ** -0.5)


def _tril_w(w):
    r = lax.broadcasted_iota(jnp.int32, (BLK, BLK), 0)
    c = lax.broadcasted_iota(jnp.int32, (BLK, BLK), 1)
    return jnp.where(r >= c, w, 0.0), r >= c


def _gate_fwd_group(zu, zv, lg, lb, w, bcol, with_grad=False):
    u, v = _gelu(zu, with_grad), _gelu(zv, with_grad)
    if with_grad:
        (u, du_dz), (v, dv_dz) = u, v
    mu = jnp.mean(v, axis=-1, keepdims=True)
    vc = v - mu
    rs = lax.rsqrt(jnp.mean(vc * vc, axis=-1, keepdims=True) + EPS)
    vh = vc * rs
    vn = vh * lg + lb
    wt, tril = _tril_w(w)
    mixed = _dot(wt.astype(BF16), vn.astype(BF16), NN) + bcol
    if with_grad:
        return u, vh, rs, vn, wt, tril, mixed, du_dz, dv_dz
    return u, vh, rs, vn, wt, tril, mixed


class _Dims:
    def __init__(self, seq, attn_w, gate_w):
        self.seq, self.attn_w, self.gate_w = seq, attn_w, gate_w
        self.n_heads = attn_w // HEAD_DIM
        self.group = self.n_heads // N_KV_HEADS
        self.n_pairs = attn_w // BLK
        self.n_groups = gate_w // BLK
        self.kv_col = attn_w // (2 * BLK)
        self.u0 = attn_w + 2 * BLK
        self.v0 = self.u0 + gate_w
        self.in_w = self.v0 + gate_w
        self.slopes = [2.0 ** (-8.0 * (h + 1) / self.n_heads) for h in range(self.n_heads)]


def _mixer_fwd_body(d, sink_ref, proj_ref, kvp_ref, qg_ref, kg_ref, lg_ref, lb_ref, w_ref, b_ref, goa_ref, gog_ref,
                    ya_ref, yg_ref, y_ref, logit_scr, prob_scr):
    i = pl.program_id(0)
    first = (i % (d.seq // BLK) == 0).astype(jnp.int32)
    lo = _lo_mask((BLK, BLK))
    lo2 = _lo_mask((2 * BLK, BLK))
    kv_cur = proj_ref[:, d.attn_w:d.attn_w + 2 * BLK]
    _, _, kk, vv = _band(kv_cur, kvp_ref[...], kg_ref[...], lo2)
    col, distf, valid = _score_geometry(first)
    qg = qg_ref[...]
    for j in range(d.n_pairs):
        h0, h1 = 2 * j, 2 * j + 1
        q2 = proj_ref[:, BLK * j:BLK * (j + 1)]
        qn = q2 * _head_rstd(q2, lo) * qg
        logit_scr[j] = _pair_logits(qn, kk[h0 // d.group], col, distf, valid, d.slopes[h0], d.slopes[h1])
    for j in range(d.n_pairs):
        probs, _ = _softmax_halves(logit_scr[j], sink_ref[0, 2 * j], sink_ref[0, 2 * j + 1])
        prob_scr[j] = jnp.concatenate(probs, axis=1).astype(BF16)
    for j in range(d.n_pairs):
        ya_ref[:, BLK * j:BLK * (j + 1)] = _dot(prob_scr[j], vv[2 * j // d.group], NN)
    for g in range(d.n_groups):
        zu = proj_ref[:, d.u0 + BLK * g:d.u0 + BLK * (g + 1)]
        zv = proj_ref[:, d.v0 + BLK * g:d.v0 + BLK * (g + 1)]
        u, _, _, _, _, _, mixed = _gate_fwd_group(zu, zv, lg_ref[g:g + 1, :], lb_ref[g:g + 1, :], w_ref[g], b_ref[g])
        yg_ref[:, BLK * g:BLK * (g + 1)] = u * mixed
    ya = ya_ref[...]
    ra = lax.rsqrt(jnp.mean(ya * ya, axis=-1, keepdims=True) + EPS)
    y_ref[:, :d.attn_w] = (ya * ra * goa_ref[...]).astype(y_ref.dtype)
    yg = yg_ref[...]
    rg = lax.rsqrt(jnp.mean(yg * yg, axis=-1, keepdims=True) + EPS)
    y_ref[:, d.attn_w:] = (yg * rg * gog_ref[...]).astype(y_ref.dtype)


def _mixer_specs(d, T):
    row = lambda w: pl.BlockSpec((BLK, w), lambda i: (i, 0))
    const2 = lambda a: pl.BlockSpec(a.shape, lambda i: (0, 0))
    const3 = lambda a: pl.BlockSpec(a.shape, lambda i: (0, 0, 0))
    kv_prev = pl.BlockSpec((BLK, 2 * BLK), lambda i: (jnp.maximum(i - 1, 0), d.kv_col))
    return row, const2, const3, kv_prev


def _mixer_fwd(d, proj, sinks, qg2, kg2, lg, lb, wsp, bcol, goa, gog):
    T = proj.shape[0]
    row, const2, const3, kv_prev = _mixer_specs(d, T)
    return pl.pallas_call(
        functools.partial(_mixer_fwd_body, d),
        name="mixer_fwd",
        grid=(T // BLK,),
        in_specs=[pl.BlockSpec(memory_space=pltpu.SMEM), row(d.in_w), kv_prev, const2(qg2), const2(kg2),
                  const2(lg), const2(lb), const3(wsp), const3(bcol), const2(goa), const2(gog)],
        out_specs=[row(d.attn_w), row(d.gate_w), row(d.attn_w + d.gate_w)],
        out_shape=[jax.ShapeDtypeStruct((T, d.attn_w), F32), jax.ShapeDtypeStruct((T, d.gate_w), F32),
                   jax.ShapeDtypeStruct((T, d.attn_w + d.gate_w), BF16)],
        scratch_shapes=[pltpu.VMEM((d.n_pairs, BLK, 4 * BLK), F32), pltpu.VMEM((d.n_pairs, BLK, 4 * BLK), BF16)],
        compiler_params=_params(("parallel",)),
    )(sinks, proj, proj, qg2, kg2, lg, lb, wsp, bcol, goa, gog)


def _mixer_bwd_body(d, sink_ref, proj_ref, kvp_ref, ya_ref, yg_ref, dy_ref, qg_ref, kg_ref, lg_ref, lb_ref, w_ref,
                    b_ref, goa_ref, gog_ref,
                    dproj_ref, dkv_ref, dqg_ref, dkg_ref, dsk_ref, dlg_ref, dlb_ref, dw_ref, db_ref, dgoa_ref,
                    dgog_ref):
    i = pl.program_id(0)

    @pl.when(i == 0)
    def _():
        for r in (dqg_ref, dkg_ref, dsk_ref, dlg_ref, dlb_ref, dw_ref, db_ref, dgoa_ref, dgog_ref):
            r[...] = jnp.zeros_like(r)

    first = (i % (d.seq // BLK) == 0).astype(jnp.int32)
    lo = _lo_mask((BLK, BLK))
    lo2 = _lo_mask((2 * BLK, BLK))
    lane_row = lax.broadcasted_iota(jnp.int32, (1, BLK), 1)

    ya = ya_ref[...]
    ra = lax.rsqrt(jnp.mean(ya * ya, axis=-1, keepdims=True) + EPS)
    yah = ya * ra
    dyn = dy_ref[:, :d.attn_w]
    dgoa_ref[...] += jnp.sum(dyn * yah, axis=0, keepdims=True)
    t = dyn * goa_ref[...]
    dya = ra * (t - yah * jnp.mean(t * yah, axis=-1, keepdims=True))
    yg = yg_ref[...]
    rg = lax.rsqrt(jnp.mean(yg * yg, axis=-1, keepdims=True) + EPS)
    ygh = yg * rg
    dyn = dy_ref[:, d.attn_w:]
    dgog_ref[...] += jnp.sum(dyn * ygh, axis=0, keepdims=True)
    t = dyn * gog_ref[...]
    dyg = rg * (t - ygh * jnp.mean(t * ygh, axis=-1, keepdims=True))

    kv_cur = proj_ref[:, d.attn_w:d.attn_w + 2 * BLK]
    kg = kg_ref[...]
    kb, rk, kk, vv = _band(kv_cur, kvp_ref[...], kg, lo2)
    col, distf, valid = _score_geometry(first)
    qg = qg_ref[...]
    ck = [jnp.zeros((BLK, 2 * BLK), F32) for _ in range(N_KV_HEADS)]
    cv = [jnp.zeros((BLK, 2 * BLK), F32) for _ in range(N_KV_HEADS)]
    lo_rows = lax.broadcasted_iota(jnp.int32, (BLK, 2 * BLK), 0) < HEAD_DIM
    dsk = jnp.zeros((1, BLK), F32)
    dqg = jnp.zeros((1, BLK), F32)
    for j in range(d.n_pairs):
        h0, h1 = 2 * j, 2 * j + 1
        kh = h0 // d.group
        cols = slice(BLK * j, BLK * (j + 1))
        q2 = proj_ref[:, cols]
        rq = _head_rstd(q2, lo)
        qh = q2 * rq
        qn = qh * qg
        probs, psink = _pair_probs(qn, kk[kh], col, distf, valid, d.slopes[h0], d.slopes[h1],
                                   sink_ref[0, h0], sink_ref[0, h1])
        do2 = dya[:, cols]
        prod = do2 * ya[:, cols]
        delta = (jnp.sum(jnp.where(lo, prod, 0.0), axis=-1, keepdims=True),
                 jnp.sum(jnp.where(lo, 0.0, prod), axis=-1, keepdims=True))
        do2b = do2.astype(BF16)
        dp = _dot(do2b, vv[kh], NT)
        ds = []
        for hh in (0, 1):
            ds.append(probs[hh] * (dp[:, 2 * BLK * hh:2 * BLK * (hh + 1)] - delta[hh]))
            dsink = -jnp.sum(psink[hh] * delta[hh], axis=0, keepdims=True)
            dsk = dsk + jnp.where(lane_row == (h0 + hh), dsink, 0.0)
        dsb = (jnp.concatenate(ds, axis=1) * (HEAD_DIM ** -0.5)).astype(BF16)
        pb = jnp.concatenate(probs, axis=1).astype(BF16)
        qnb = qn.astype(BF16)
        dqn = _dot(dsb, kk[kh], NN)
        dkk = _dot(qnb, dsb, TN)
        dvv = _dot(do2b, pb, TN)
        ck[kh] = ck[kh] + jnp.where(lo_rows, dkk[:, :2 * BLK], 0.0) + jnp.where(lo_rows, 0.0, dkk[:, 2 * BLK:])
        cv[kh] = cv[kh] + jnp.where(lo_rows, dvv[:, :2 * BLK], 0.0) + jnp.where(lo_rows, 0.0, dvv[:, 2 * BLK:])
        dqg = dqg + jnp.sum(dqn * qh, axis=0, keepdims=True)
        t = dqn * qg
        dq2 = rq * (t - qh * (_half_sums(t * qh, lo) * (1.0 / HEAD_DIM)))
        dproj_ref[:, cols] = dq2.astype(dproj_ref.dtype)
    dsk_ref[...] += dsk
    dqg_ref[...] += dqg
    dkn = _unplace(jnp.transpose(ck[0]), jnp.transpose(ck[1]), lo2)
    dvb = _unplace(jnp.transpose(cv[0]), jnp.transpose(cv[1]), lo2)
    khat = kb * rk
    dkg_ref[...] += jnp.sum(dkn * khat, axis=0, keepdims=True)
    t = dkn * kg
    dkb = rk * (t - khat * (_half_sums(t * khat, lo2) * (1.0 / HEAD_DIM)))
    rows_cur = pl.ds(pl.multiple_of(i * BLK, BLK), BLK)
    rows_prev = pl.ds(pl.multiple_of(jnp.maximum(i - 1, 0) * BLK, BLK), BLK)
    dkv_ref[rows_cur, :] = jnp.concatenate([dkb[BLK:], dvb[BLK:]], axis=1)
    dkv_ref[rows_prev, :] += jnp.concatenate([dkb[:BLK], dvb[:BLK]], axis=1)
    dproj_ref[:, d.attn_w:d.attn_w + 2 * BLK] = jnp.zeros((BLK, 2 * BLK), dproj_ref.dtype)

    for g in range(d.n_groups):
        ucols = slice(d.u0 + BLK * g, d.u0 + BLK * (g + 1))
        vcols = slice(d.v0 + BLK * g, d.v0 + BLK * (g + 1))
        zu = proj_ref[:, ucols]
        zv = proj_ref[:, vcols]
        lg = lg_ref[g:g + 1, :]
        u, vh, rs, vn, wt, tril, mixed, du_dz, dv_dz = _gate_fwd_group(
            zu, zv, lg, lb_ref[g:g + 1, :], w_ref[g], b_ref[g], with_grad=True)
        dyg_g = dyg[:, BLK * g:BLK * (g + 1)]
        du = dyg_g * mixed
        dmix = dyg_g * u
        dmb = dmix.astype(BF16)
        db_ref[g:g + 1, :] += jnp.sum(jnp.transpose(dmix), axis=0, keepdims=True)
        dw_ref[g] += jnp.where(tril, _dot(dmb, vn.astype(BF16), NT), 0.0)
        dvn = _dot(wt.astype(BF16), dmb, TN)
        dlg_ref[g:g + 1, :] += jnp.sum(dvn * vh, axis=0, keepdims=True)
        dlb_ref[g:g + 1, :] += jnp.sum(dvn, axis=0, keepdims=True)
        dvh = dvn * lg
        dv = rs * (dvh - jnp.mean(dvh, axis=-1, keepdims=True) - vh * jnp.mean(dvh * vh, axis=-1, keepdims=True))
        dproj_ref[:, ucols] = (du * du_dz).astype(dproj_ref.dtype)
        dproj_ref[:, vcols] = (dv * dv_dz).astype(dproj_ref.dtype)


def _mixer_bwd(d, proj, ya, yg, dy, sinks, qg2, kg2, lg, lb, wsp, bcol, goa, gog):
    T = proj.shape[0]
    row, const2, const3, kv_prev = _mixer_specs(d, T)
    acc2 = lambda s: pl.BlockSpec(s, lambda i: (0, 0))
    G = d.n_groups
    out_shapes = [((T, d.in_w), BF16), ((T, 2 * BLK), F32), ((1, BLK), F32), ((1, BLK), F32), ((1, BLK), F32),
                  ((G, BLK), F32), ((G, BLK), F32), ((G, BLK, BLK), F32), ((G, BLK), F32),
                  ((1, d.attn_w), F32), ((1, d.gate_w), F32)]
    out_specs = [row(d.in_w)] + [acc2(s) for s, _ in out_shapes[1:7]] + \
                [pl.BlockSpec((G, BLK, BLK), lambda i: (0, 0, 0))] + [acc2(s) for s, _ in out_shapes[8:]]
    return pl.pallas_call(
        functools.partial(_mixer_bwd_body, d),
        name="mixer_bwd",
        grid=(T // BLK,),
        in_specs=[pl.BlockSpec(memory_space=pltpu.SMEM), row(d.in_w), kv_prev, row(d.attn_w), row(d.gate_w),
                  row(d.attn_w + d.gate_w), const2(qg2), const2(kg2), const2(lg), const2(lb), const3(wsp),
                  const3(bcol), const2(goa), const2(gog)],
        out_specs=out_specs,
        out_shape=[jax.ShapeDtypeStruct(s, t) for s, t in out_shapes],
        compiler_params=_params(("arbitrary",)),
    )(sinks, proj, proj, ya, yg, dy, qg2, kg2, lg, lb, wsp, bcol, goa, gog)


def _put_kv_body(dkv_ref, dproj_in_ref, dproj_ref):
    del dproj_in_ref
    dproj_ref[...] = dkv_ref[...].astype(dproj_ref.dtype)


def _put_kv(d, dproj, dkv):
    T = dproj.shape[0]
    tr = _pick(T, 1024, 16)
    return pl.pallas_call(
        functools.partial(_put_kv_body),
        name="put_kv",
        grid=(T // tr,),
        in_specs=[pl.BlockSpec((tr, 2 * BLK), lambda i: (i, 0)), pl.BlockSpec(memory_space=pl.ANY)],
        out_specs=pl.BlockSpec((tr, 2 * BLK), lambda i: (i, d.kv_col)),
        out_shape=jax.ShapeDtypeStruct(dproj.shape, dproj.dtype),
        input_output_aliases={1: 0},
        compiler_params=_params(("parallel",)),
    )(dkv, dproj)


def _add_pair_body(pc_ref, own_ref, got_ref, o_ref):
    del pc_ref
    o_ref[...] = (own_ref[...].astype(F32) + got_ref[...].astype(F32)).astype(o_ref.dtype)


def _add_pair(g4, got, pc, name):
    n, _, h, C = g4.shape
    tr = _pick(h, 512, 16)
    return pl.pallas_call(
        functools.partial(_add_pair_body),
        name=name,
        grid_spec=pltpu.PrefetchScalarGridSpec(
            num_scalar_prefetch=1,
            grid=(n, h // tr),
            in_specs=[pl.BlockSpec((None, None, tr, C), lambda q, i, pc: (q, pc[1], i, 0)),
                      pl.BlockSpec((None, tr, C), lambda q, i, pc: (q, i, 0))],
            out_specs=pl.BlockSpec((None, tr, C), lambda q, i, pc: (q, i, 0)),
        ),
        out_shape=jax.ShapeDtypeStruct((n, h, C), g4.dtype),
        compiler_params=_params(("parallel", "parallel")),
    )(pc, g4, got)


def _adamw_update(w, g, m, v):
    m = ADAM_B1 * m + (1.0 - ADAM_B1) * g
    v = ADAM_B2 * v + (1.0 - ADAM_B2) * (g * g)
    m_hat = m / (1.0 - ADAM_B1 ** ADAM_STEP)
    v_hat = v / (1.0 - ADAM_B2 ** ADAM_STEP)
    return -ADAM_LR * (m_hat / (jnp.sqrt(v_hat) + ADAM_EPS) + ADAM_WD * w), m, v


def _adamw_body(w_ref, g_ref, m_ref, v_ref, d_ref, nm_ref, nv_ref):
    d_ref[...], nm_ref[...], nv_ref[...] = _adamw_update(w_ref[...], g_ref[...], m_ref[...], v_ref[...])


def _adamw(w, g, m, v, name):
    R, C = w.shape
    tr = _pick(R, 512, 8)
    blk = pl.BlockSpec((tr, C), lambda i: (i, 0))
    return pl.pallas_call(
        functools.partial(_adamw_body),
        name=name,
        grid=(R // tr,),
        in_specs=[blk] * 4,
        out_specs=[blk] * 3,
        out_shape=[jax.ShapeDtypeStruct((R, C), F32)] * 3,
        compiler_params=_params(("parallel",)),
    )(w, g, m, v)


def _adamw_halves_body(pc_ref, w_ref, own_ref, got_ref, m_ref, v_ref, g_ref, d_ref, nm_ref, nv_ref):
    mine = pl.program_id(0) == pc_ref[1]

    def update(g):
        g_ref[...] = g
        d_ref[...], nm_ref[...], nv_ref[...] = _adamw_update(w_ref[...], g, m_ref[...], v_ref[...])

    @pl.when(mine)
    def _():
        update(own_ref[...].astype(F32))

    @pl.when(jnp.logical_not(mine))
    def _():
        update(got_ref[...].astype(F32))


def _adamw_halves(w, own, got, m, v, pc, name):
    h, C = own.shape
    tr = _pick(h, 512, 8)
    full = pl.BlockSpec((None, tr, C), lambda hh, i, pc: (hh, i, 0))
    mine = pl.BlockSpec((tr, C), lambda hh, i, pc: (jnp.where(hh == pc[1], i, 0), 0))
    theirs = pl.BlockSpec((tr, C), lambda hh, i, pc: (jnp.where(hh == pc[1], 0, i), 0))
    return pl.pallas_call(
        functools.partial(_adamw_halves_body),
        name=name,
        grid_spec=pltpu.PrefetchScalarGridSpec(
            num_scalar_prefetch=1,
            grid=(2, h // tr),
            in_specs=[full, mine, theirs, full, full],
            out_specs=[full] * 4,
        ),
        out_shape=[jax.ShapeDtypeStruct((2, h, C), F32)] * 4,
        compiler_params=_params(("parallel", "parallel")),
    )(pc, w.reshape(2, h, C), own, got, m.reshape(2, h, C), v.reshape(2, h, C))


def _me():
    x, y, c = lax.axis_index("x"), lax.axis_index("y"), lax.axis_index("c")
    chips = [(1 - x, y), (x, 1 - y), (1 - x, 1 - y)]
    return x, y, c, chips


def _cast_into_body(pc_ref, w_ref, o_ref):
    del pc_ref
    o_ref[...] = w_ref[...].astype(o_ref.dtype)


def _cast_into(w, pc, name, side_by_side=False):
    Rs, C = w.shape
    h = Rs // 2
    tr = _pick(h, 512, 16)
    if side_by_side:
        out_spec = pl.BlockSpec((None, tr, C), lambda hh, i, pc: (hh, i, pc[0]))
        out_shape = jax.ShapeDtypeStruct((2, h, N_CHIPS * C), BF16)
    else:
        out_spec = pl.BlockSpec((None, None, tr, C), lambda hh, i, pc: (pc[0], hh, i, 0))
        out_shape = jax.ShapeDtypeStruct((N_CHIPS, 2, h, C), BF16)
    return pl.pallas_call(
        functools.partial(_cast_into_body),
        name=name,
        grid_spec=pltpu.PrefetchScalarGridSpec(
            num_scalar_prefetch=1,
            grid=(2, h // tr),
            in_specs=[pl.BlockSpec((None, tr, C), lambda hh, i, pc: (hh, i, 0))],
            out_specs=out_spec,
        ),
        out_shape=out_shape,
        compiler_params=_params(("parallel", "parallel")),
    )(pc, w.reshape(2, h, C))


MAX_PIECES = 4


def _send_tile_to_sibling(src_of, dst_of, tr, dst_total, send_sems, recv_sem, last):
    x, y, c, _ = _me()
    pieces = MAX_PIECES if tr % (16 * MAX_PIECES) == 0 else (2 if tr % 32 == 0 else 1)
    n = tr // pieces
    copies = [pltpu.make_async_remote_copy(src_ref=src_of(k * n, n), dst_ref=dst_of(k * n, n), send_sem=send_sems.at[k],
                                           recv_sem=recv_sem, device_id=(x, y, 1 - c), device_id_type=MESH)
              for k in range(pieces)]
    for cp in copies:
        cp.start()
    for cp in copies:
        cp.wait_send()

    @pl.when(last)
    def _():
        pltpu.make_async_remote_copy(src_ref=dst_total, dst_ref=dst_total, send_sem=send_sems.at[0], recv_sem=recv_sem,
                                     device_id=(x, y, 1 - c), device_id_type=MESH).wait_recv()


TILE_SEMS = [pltpu.SemaphoreType.DMA((MAX_PIECES,)), pltpu.SemaphoreType.DMA(())]


def _ag_pair_body(tr, n_i, pc_ref, tile_ref, buf_ref, send_sem, recv_sem):
    j, i = pl.program_id(0), pl.program_id(1)
    q = pc_ref[0] ^ (j + 1)
    c = pc_ref[1]
    r_tile = pl.multiple_of(i * tr, tr)
    last = jnp.logical_and(j == N_CHIPS - 2, i == n_i - 1)
    if len(buf_ref.shape) == 4:
        _send_tile_to_sibling(lambda r0, n: tile_ref.at[:, :, pl.ds(r0, n)],
                              lambda r0, n: buf_ref.at[pl.ds(q, 1), pl.ds(c, 1), pl.ds(r_tile + r0, n)], tr,
                              buf_ref.at[pl.ds(0, N_CHIPS - 1), 0], send_sem, recv_sem, last)
    else:
        cs = buf_ref.shape[2] // N_CHIPS
        cols = pl.ds(pl.multiple_of(q * cs, BLK), cs)
        _send_tile_to_sibling(lambda r0, n: tile_ref.at[:, pl.ds(r0, n)],
                              lambda r0, n: buf_ref.at[pl.ds(c, 1), pl.ds(r_tile + r0, n), cols], tr,
                              buf_ref.at[0, :, pl.ds(0, (N_CHIPS - 1) * cs)], send_sem, recv_sem, last)


def _ag_pair(buf, pc, name):
    if len(buf.shape) == 4:
        _, _, h, C = buf.shape
        tile = lambda tr: pl.BlockSpec((1, 1, tr, C), lambda j, i, pc: (pc[0] ^ (j + 1), pc[1], i, 0))
    else:
        _, h, C = buf.shape
        tile = lambda tr: pl.BlockSpec((1, tr, C // N_CHIPS), lambda j, i, pc: (pc[1], i, pc[0] ^ (j + 1)))
    tr = _pick(h, 512, 16)
    return pl.pallas_call(
        functools.partial(_ag_pair_body, tr, h // tr),
        name=name,
        grid_spec=pltpu.PrefetchScalarGridSpec(
            num_scalar_prefetch=1,
            grid=(N_CHIPS - 1, h // tr),
            in_specs=[tile(tr)],
            out_specs=HBM,
            scratch_shapes=TILE_SEMS,
        ),
        out_shape=jax.ShapeDtypeStruct(buf.shape, buf.dtype),
        input_output_aliases={1: 0},
        compiler_params=_params(("arbitrary", "arbitrary")),
    )(pc, buf)


def _swap_halves_body(tr, n_q, n_i, pc_ref, tile_ref, got_ref, send_sem, recv_sem):
    del pc_ref
    q, i = pl.program_id(0), pl.program_id(1)
    r_tile = pl.multiple_of(i * tr, tr)
    _send_tile_to_sibling(lambda r0, n: tile_ref.at[:, :, pl.ds(r0, n)],
                          lambda r0, n: got_ref.at[pl.ds(q, 1), :, pl.ds(r_tile + r0, n)], tr, got_ref, send_sem, recv_sem,
                          jnp.logical_and(q == n_q - 1, i == n_i - 1))


def _swap_halves(g4, pc, name):
    n, _, h, C = g4.shape
    tr = _pick(h, 512, 16)
    return pl.pallas_call(
        functools.partial(_swap_halves_body, tr, n, h // tr),
        name=name,
        grid_spec=pltpu.PrefetchScalarGridSpec(
            num_scalar_prefetch=1,
            grid=(n, h // tr),
            in_specs=[pl.BlockSpec((1, 1, tr, C), lambda q, i, pc: (q, 1 - pc[1], i, 0))],
            out_specs=HBM,
            scratch_shapes=TILE_SEMS,
        ),
        out_shape=jax.ShapeDtypeStruct((n, 1, h, C), g4.dtype),
        compiler_params=_params(("arbitrary", "arbitrary")),
    )(pc, g4).reshape(n, h, C)


def _ici_copy(src, dst, send_sems, recv_sems, j, chip, c):
    return pltpu.make_async_remote_copy(src_ref=src, dst_ref=dst, send_sem=send_sems.at[j], recv_sem=recv_sems.at[j],
                                        device_id=(chip[0], chip[1], c), device_id_type=MESH)


def _token_spec():
    return jax.ShapeDtypeStruct((8, BLK), F32), pl.BlockSpec(memory_space=pltpu.VMEM)


def _slab(buf_ref, q, c):
    if len(buf_ref.shape) == 4:
        return buf_ref.at[q, c]
    cs = buf_ref.shape[2] // N_CHIPS
    return buf_ref.at[c, :, pl.ds(pl.multiple_of(q * cs, BLK), cs)]


def _ag_start_body(both_cores, buf_ref, after_ref, send_sems, recv_sems, buf_thru, token_ref):
    del after_ref, buf_thru
    x, y, c, chips = _me()
    mine = _slab(buf_ref, 2 * x + y, c)
    for j, chip in enumerate(chips):
        _ici_copy(mine, mine, send_sems, recv_sems, j, chip, c).start()
    if both_cores:
        for j, chip in enumerate(chips):
            _ici_copy(mine, mine, send_sems, recv_sems, N_CHIPS - 1 + j, chip, 1 - c).start()
    token_ref[...] = jnp.zeros_like(token_ref)


def _ag_start(buf, after, name, both_cores=False):
    tok_shape, tok_spec = _token_spec()
    sems = pltpu.SemaphoreType.DMA(((N_CHIPS - 1) * (2 if both_cores else 1),))
    return pl.pallas_call(
        functools.partial(_ag_start_body, both_cores),
        name=name,
        in_specs=[HBM, ANY],
        out_specs=[SEM, SEM, HBM, tok_spec],
        out_shape=[sems, sems, pltpu.HBM(buf.shape, buf.dtype), tok_shape],
        input_output_aliases={0: 2},
        compiler_params=pltpu.CompilerParams(has_side_effects=EFFECT),
    )(pltpu.with_memory_space_constraint(buf, pltpu.HBM), after)


def _ag_wait_body(both_cores, buf_ref, send_sems, recv_sems, after_ref, buf_out):
    del after_ref, buf_out
    x, y, c, chips = _me()
    mine = _slab(buf_ref, 2 * x + y, c)
    for j, chip in enumerate(chips):
        theirs = _slab(buf_ref, 2 * chip[0] + chip[1], c)
        _ici_copy(mine, mine, send_sems, recv_sems, j, chip, c).wait_send()
        _ici_copy(theirs, theirs, send_sems, recv_sems, j, chip, c).wait_recv()
    if both_cores:
        for j, chip in enumerate(chips):
            theirs = _slab(buf_ref, 2 * chip[0] + chip[1], 1 - c)
            _ici_copy(mine, mine, send_sems, recv_sems, N_CHIPS - 1 + j, chip, 1 - c).wait_send()
            _ici_copy(theirs, theirs, send_sems, recv_sems, N_CHIPS - 1 + j, chip, 1 - c).wait_recv()


def _ag_wait(buf, send_sems, recv_sems, after, name, both_cores=False):
    return pl.pallas_call(
        functools.partial(_ag_wait_body, both_cores),
        name=name,
        in_specs=[HBM, SEM, SEM, ANY],
        out_specs=HBM,
        out_shape=pltpu.HBM(buf.shape, buf.dtype),
        input_output_aliases={0: 0},
        compiler_params=pltpu.CompilerParams(has_side_effects=EFFECT),
    )(buf, send_sems, recv_sems, after)


def _rs_start_body(pair_ref, land_ref, after_ref, send_sems, recv_sems, pair_thru, land_thru, token_ref):
    del after_ref, pair_thru, land_thru
    x, y, c, chips = _me()
    for j, chip in enumerate(chips):
        _ici_copy(pair_ref.at[2 * chip[0] + chip[1]], land_ref.at[j], send_sems, recv_sems, j, chip, c).start()
    token_ref[...] = jnp.zeros_like(token_ref)


def _rs_start(pair, after, name):
    n, h, C = pair.shape
    tok_shape, tok_spec = _token_spec()
    sems = pltpu.SemaphoreType.DMA((N_CHIPS - 1,))
    land = pltpu.with_memory_space_constraint(lax.empty((N_CHIPS - 1, h, C), pair.dtype), pltpu.HBM)
    return pl.pallas_call(
        functools.partial(_rs_start_body),
        name=name,
        in_specs=[HBM, HBM, ANY],
        out_specs=[SEM, SEM, HBM, HBM, tok_spec],
        out_shape=[sems, sems, pltpu.HBM(pair.shape, pair.dtype), pltpu.HBM(land.shape, land.dtype), tok_shape],
        input_output_aliases={0: 2, 1: 3},
        compiler_params=pltpu.CompilerParams(has_side_effects=EFFECT),
    )(pltpu.with_memory_space_constraint(pair, pltpu.HBM), land, after)


def _rs_wait_body(pair_ref, land_ref, send_sems, recv_sems, after_ref, pair_out, land_out):
    del after_ref, pair_out, land_out
    x, y, c, chips = _me()
    for j, chip in enumerate(chips):
        _ici_copy(pair_ref.at[0], land_ref.at[j], send_sems, recv_sems, j, chip, c).wait_send()
        _ici_copy(pair_ref.at[0], land_ref.at[j], send_sems, recv_sems, j, chip, c).wait_recv()


def _rs_wait(pair, land, send_sems, recv_sems, after, name):
    return pl.pallas_call(
        functools.partial(_rs_wait_body),
        name=name,
        in_specs=[HBM, HBM, SEM, SEM, ANY],
        out_specs=[HBM, HBM],
        out_shape=[pltpu.HBM(pair.shape, pair.dtype), pltpu.HBM(land.shape, land.dtype)],
        input_output_aliases={0: 0, 1: 1},
        compiler_params=pltpu.CompilerParams(has_side_effects=EFFECT),
    )(pair, land, send_sems, recv_sems, after)


def _swap_copy(g4_ref, got_ref, send_sem, recv_sem):
    x, y, c, _ = _me()
    return pltpu.make_async_remote_copy(src_ref=g4_ref.at[:, 1 - c], dst_ref=got_ref, send_sem=send_sem,
                                        recv_sem=recv_sem, device_id=(x, y, 1 - c), device_id_type=MESH)


def _swap_start_body(g4_ref, got_ref, send_sem, recv_sem, g4_thru, got_thru, token_ref):
    del g4_thru, got_thru
    _swap_copy(g4_ref, got_ref, send_sem, recv_sem).start()
    token_ref[...] = jnp.zeros_like(token_ref)


def _swap_start(g4, name):
    n, _, h, C = g4.shape
    tok_shape, tok_spec = _token_spec()
    sem = pltpu.SemaphoreType.DMA(())
    got = pltpu.with_memory_space_constraint(lax.empty((n, h, C), g4.dtype), pltpu.HBM)
    return pl.pallas_call(
        functools.partial(_swap_start_body),
        name=name,
        in_specs=[HBM, HBM],
        out_specs=[SEM, SEM, HBM, HBM, tok_spec],
        out_shape=[sem, sem, pltpu.HBM(g4.shape, g4.dtype), pltpu.HBM(got.shape, got.dtype), tok_shape],
        input_output_aliases={0: 2, 1: 3},
        compiler_params=pltpu.CompilerParams(has_side_effects=EFFECT),
    )(pltpu.with_memory_space_constraint(g4, pltpu.HBM), got)


def _swap_wait_body(g4_ref, got_ref, send_sem, recv_sem, after_ref, g4_out, got_out):
    del after_ref, g4_out, got_out
    cp = _swap_copy(g4_ref, got_ref, send_sem, recv_sem)
    cp.wait_send()
    cp.wait_recv()


def _swap_wait(g4, got, send_sem, recv_sem, after, name):
    return pl.pallas_call(
        functools.partial(_swap_wait_body),
        name=name,
        in_specs=[HBM, HBM, SEM, SEM, ANY],
        out_specs=[HBM, HBM],
        out_shape=[pltpu.HBM(g4.shape, g4.dtype), pltpu.HBM(got.shape, got.dtype)],
        input_output_aliases={0: 0, 1: 1},
        compiler_params=pltpu.CompilerParams(has_side_effects=EFFECT),
    )(g4, got, send_sem, recv_sem, after)


def _add_chips_body(tr, n_i, pc_ref, own_ref, l0_ref, l1_ref, l2_ref, o_ref, got_ref, send_sems, recv_sem):
    del pc_ref
    i = pl.program_id(0)
    r = own_ref[...].astype(F32) + l0_ref[...].astype(F32)
    o_ref[...] = (r + l1_ref[...].astype(F32) + l2_ref[...].astype(F32)).astype(o_ref.dtype)
    r_tile = pl.multiple_of(i * tr, tr)
    _send_tile_to_sibling(lambda r0, n: o_ref.at[pl.ds(r0, n)], lambda r0, n: got_ref.at[pl.ds(r_tile + r0, n)], tr,
                          got_ref, send_sems, recv_sem, i == n_i - 1)


def _add_chips(pair, land, pc, name):
    _, h, C = pair.shape
    tr = _pick(h, 256, 16)
    slot = lambda j: pl.BlockSpec((None, tr, C), lambda i, pc: (j, i, 0))
    return pl.pallas_call(
        functools.partial(_add_chips_body, tr, h // tr),
        name=name,
        grid_spec=pltpu.PrefetchScalarGridSpec(
            num_scalar_prefetch=1,
            grid=(h // tr,),
            in_specs=[pl.BlockSpec((None, tr, C), lambda i, pc: (pc[0], i, 0)), slot(0), slot(1), slot(2)],
            out_specs=[pl.BlockSpec((tr, C), lambda i, pc: (i, 0)), HBM],
            scratch_shapes=TILE_SEMS,
        ),
        out_shape=[jax.ShapeDtypeStruct((h, C), pair.dtype), jax.ShapeDtypeStruct((h, C), pair.dtype)],
        compiler_params=_params(("arbitrary",)),
    )(pc, pair, land, land, land)


def _peer(r):
    x, y, c, _ = _me()
    return (x ^ ((r >> 2) & 1), y ^ ((r >> 1) & 1), c ^ (r & 1))


def _ar_start_body(x_ref, land_ref, send_sems, recv_sems, x_thru, land_thru, token_ref):
    del x_thru, land_thru
    for r in range(1, N_DEV):
        pltpu.make_async_remote_copy(src_ref=x_ref, dst_ref=land_ref.at[r - 1], send_sem=send_sems.at[r - 1],
                                     recv_sem=recv_sems.at[r - 1], device_id=_peer(r), device_id_type=MESH).start()
    token_ref[...] = jnp.zeros_like(token_ref)


def _ar_start(packed):
    tok_shape, tok_spec = _token_spec()
    sems = pltpu.SemaphoreType.DMA((N_DEV - 1,))
    land = pltpu.with_memory_space_constraint(lax.empty((N_DEV - 1,) + packed.shape, packed.dtype), pltpu.HBM)
    return pl.pallas_call(
        functools.partial(_ar_start_body),
        name="ar_start",
        in_specs=[HBM, HBM],
        out_specs=[SEM, SEM, HBM, HBM, tok_spec],
        out_shape=[sems, sems, pltpu.HBM(packed.shape, packed.dtype), pltpu.HBM(land.shape, land.dtype), tok_shape],
        input_output_aliases={0: 2, 1: 3},
        compiler_params=pltpu.CompilerParams(has_side_effects=EFFECT),
    )(pltpu.with_memory_space_constraint(packed, pltpu.HBM), land)


def _ar_wait_body(x_ref, land_ref, send_sems, recv_sems, after_ref, x_out, land_out):
    del after_ref, x_out, land_out
    for r in range(1, N_DEV):
        cp = pltpu.make_async_remote_copy(src_ref=x_ref, dst_ref=land_ref.at[r - 1], send_sem=send_sems.at[r - 1],
                                          recv_sem=recv_sems.at[r - 1], device_id=_peer(r), device_id_type=MESH)
        cp.wait_send()
        cp.wait_recv()


def _ar_wait(packed, land, send_sems, recv_sems, after):
    return pl.pallas_call(
        functools.partial(_ar_wait_body),
        name="ar_wait",
        in_specs=[HBM, HBM, SEM, SEM, ANY],
        out_specs=[HBM, HBM],
        out_shape=[pltpu.HBM(packed.shape, packed.dtype), pltpu.HBM(land.shape, land.dtype)],
        input_output_aliases={0: 0, 1: 1},
        compiler_params=pltpu.CompilerParams(has_side_effects=EFFECT),
    )(packed, land, send_sems, recv_sems, after)


def _ar_sum_body(me_ref, own_ref, *rest):
    o_ref = rest[N_DEV]
    acc = None
    for dev in range(N_DEV):
        term = jnp.where(me_ref[0] == dev, own_ref[...], rest[dev][...])
        acc = term if acc is None else acc + term
    o_ref[...] = acc


def _ar_sum(packed, land, me):
    R, C = packed.shape
    tr = _pick(R, 552, 8)
    own = pl.BlockSpec((tr, C), lambda i, me: (i, 0))
    slot = lambda dev: pl.BlockSpec((None, tr, C), lambda i, me: (jnp.maximum((dev ^ me[0]) - 1, 0), i, 0))
    return pl.pallas_call(
        functools.partial(_ar_sum_body),
        name="ar_sum",
        grid_spec=pltpu.PrefetchScalarGridSpec(
            num_scalar_prefetch=1,
            grid=(R // tr,),
            in_specs=[own] + [slot(dev) for dev in range(N_DEV)],
            out_specs=pl.BlockSpec((tr, C), lambda i, me: (i, 0)),
        ),
        out_shape=jax.ShapeDtypeStruct((R, C), F32),
        compiler_params=_params(("parallel",)),
    )(me, packed, *([land] * N_DEV))


def _pack(arrays):
    rows = []
    for a in arrays:
        flat = a.reshape(-1).astype(F32)
        pad = (-flat.shape[0]) % BLK
        rows.append(jnp.pad(flat, (0, pad)).reshape(-1, BLK))
    packed = jnp.concatenate(rows, axis=0)
    pad = (-packed.shape[0]) % 8
    return jnp.pad(packed, ((0, pad), (0, 0)))


def _unpack(packed, shapes):
    out, r = [], 0
    for s in shapes:
        n = 1
        for k in s:
            n *= k
        nr = -(-n // BLK)
        out.append(packed[r:r + nr].reshape(-1)[:n].reshape(s))
        r += nr
    return out


def kernel(x, norm1_g, w_in, q_norm_g, k_norm_g, attn_sinks, gate_ln_g, gate_ln_b, w_spatial, b_spatial, out_norm_attn_g, out_norm_gate_g, w_out, norm2_g, w_ffn_gate, w_ffn_up, w_ffn_down, loss_target, m_norm1_g, m_w_in, m_q_norm_g, m_k_norm_g, m_attn_sinks, m_gate_ln_g, m_gate_ln_b, m_w_spatial, m_b_spatial, m_out_norm_attn_g, m_out_norm_gate_g, m_w_out, m_norm2_g, m_w_ffn_gate, m_w_ffn_up, m_w_ffn_down, v_norm1_g, v_w_in, v_q_norm_g, v_k_norm_g, v_attn_sinks, v_gate_ln_g, v_gate_ln_b, v_w_spatial, v_b_spatial, v_out_norm_attn_g, v_out_norm_gate_g, v_w_out, v_norm2_g, v_w_ffn_gate, v_w_ffn_up, v_w_ffn_down):
    bl, seq, D = x.shape
    T = bl * seq
    attn_w, gate_w = out_norm_attn_g.shape[1], out_norm_gate_g.shape[1]
    d = _Dims(seq, attn_w, gate_w)
    G = d.n_groups
    in_w = d.in_w
    dff = w_ffn_gate.shape[2] * N_CHIPS
    assert w_in.shape[2] * N_CHIPS == in_w and seq % BLK == 0 and attn_w % (2 * BLK) == 0

    pc = jnp.stack([2 * lax.axis_index("x") + lax.axis_index("y"), lax.axis_index("c")]).astype(jnp.int32)
    big = [w_in[0], w_out[0], w_ffn_gate[0], w_ffn_up[0], w_ffn_down[0]]
    names = ["in", "out", "gate", "up", "down"]
    xf = x.reshape(T, D)
    tgt = loss_target.reshape(T, D)
    send, recv, buf, behind = _ag_start(_cast_into(big[0], pc, "cast_in"), norm1_g, "ag_start_in")
    started = [(send, recv, buf)]
    h1 = _rms_fwd(xf, norm1_g, "norm1_fwd", after=behind)
    behind = h1
    for w, n in zip(big[1:], names[1:]):
        buf = _cast_into(w, pc, "cast_" + n, side_by_side=n in ("gate", "up"))
        send, recv, buf, behind = _ag_start(buf, behind, "ag_start_" + n, both_cores=n == "down")
        started.append((send, recv, buf))

    def gathered(k, after):
        send, recv, buf = started[k]
        direct = names[k] == "down"
        buf = _ag_wait(buf, send, recv, after, "ag_wait_" + names[k], both_cores=direct)
        if not direct:
            buf = _ag_pair(buf, pc, "ag_pair_" + names[k])
        rs, cs = big[k].shape
        return buf.reshape(rs, N_CHIPS * cs) if len(buf.shape) == 3 else buf.reshape(N_CHIPS, rs, cs)

    qg2 = jnp.tile(q_norm_g, (1, 2))
    kg2 = jnp.tile(k_norm_g, (1, 2))
    lg, lb, wsp = gate_ln_g[0], gate_ln_b[0], w_spatial[0]
    bcol = jnp.broadcast_to(b_spatial[0][:, :, None], (G, BLK, BLK))

    win_full = jnp.transpose(gathered(0, behind), (1, 0, 2)).reshape(D, in_w)
    proj = _matmul(h1, win_full, "nn", F32, "proj_fwd", tm=512, tn=3328)
    ya, yg, yn = _mixer_fwd(d, proj, attn_sinks, qg2, kg2, lg, lb, wsp, bcol, out_norm_attn_g, out_norm_gate_g)
    wout_full = gathered(1, yn).reshape(attn_w + gate_w, D)
    x1 = _matmul(yn, wout_full, "nn", F32, "out_fwd", tm=1024, tn=1024, add=xf)
    h2 = _rms_fwd(x1, norm2_g, "norm2_fwd")
    wg_full, wu_full = gathered(2, h2), gathered(3, h2)
    a, b, f = _ffn_up(h2, wg_full, wu_full)
    wd_full = gathered(4, f).reshape(dff, D)
    dx2, dx2b, loss_local = _ffn_down_loss(f, wd_full, x1, tgt)

    def swap_start(g, n):
        g4 = g.reshape(N_CHIPS, 2, g.shape[1] // 2, g.shape[2])
        return _swap_start(g4, "rs_swap_start_" + n)

    def reduce_start(swapping, n, after):
        send, recv, g4, got, _ = swapping
        g4, got = _swap_wait(g4, got, send, recv, after, "rs_swap_wait_" + n)
        return _rs_start(_add_pair(g4, got, pc, "rs_add_pair_" + n), got, "rs_start_" + n)

    reducing = {}
    g_d = _matmul(f, dx2b, "tn", BF16, "ffn_down_dw", tm=1408, tn=1024, tk=2048, out_slab="r")
    swap_d = swap_start(g_d, "down")
    da, db = _ffn_down_dx(dx2b, wd_full, a, b, swap_d[4])
    g_g = _matmul(h2, da, "tn", BF16, "ffn_gate_dw", tm=1024, tn=1408, tk=2048, out_slab="c")
    swap_g = swap_start(g_g, "gate")
    reducing["down"] = reduce_start(swap_d, "down", swap_g[4])
    g_u = _matmul(h2, db, "tn", BF16, "ffn_up_dw", tm=1024, tn=1408, tk=2048, out_slab="c",
                  after=reducing["down"][4])
    swap_u = swap_start(g_u, "up")
    reducing["gate"] = reduce_start(swap_g, "gate", swap_u[4])
    dh2 = _matmul(da, wg_full, "nt", F32, "ffn_gate_dx", tm=1024, tn=1024, tk=2816, after=reducing["gate"][4])
    dh2 = _matmul(db, wu_full, "nt", F32, "ffn_up_dx", tm=1024, tn=1024, tk=2816, add=dh2)
    reducing["up"] = reduce_start(swap_u, "up", dh2)
    dx1, dx1b, dg_norm2 = _rms_bwd(x1, norm2_g, dh2, dx2, "norm2_bwd", True)
    g_o = _matmul(yn, dx1b, "tn", BF16, "out_dw", tm=512, tn=1024, tk=2048, out_slab="r",
                  after=reducing["up"][4])
    swap_o = swap_start(g_o, "out")
    dy = _matmul(dx1b, wout_full, "nt", F32, "out_dx", tm=1024, tn=1024, after=swap_o[4])
    (dproj, dkv, dqg, dkg, dsk, dlg, dlb, dwsp, dbsp, dgoa, dgog) = _mixer_bwd(
        d, proj, ya, yg, dy, attn_sinks, qg2, kg2, lg, lb, wsp, bcol, out_norm_attn_g, out_norm_gate_g)
    dproj = _put_kv(d, dproj, dkv)
    reducing["out"] = reduce_start(swap_o, "out", dproj)
    g_in_full = _matmul(h1, dproj, "tn", BF16, "proj_dw", tm=512, tn=3328, tk=2048,
                        after=reducing["out"][4])
    g_i = jnp.transpose(g_in_full.reshape(D, N_CHIPS, in_w // N_CHIPS), (1, 0, 2))
    g4_i = g_i.reshape(N_CHIPS, 2, D // 2, in_w // N_CHIPS)
    pair_i = _add_pair(g4_i, _swap_halves(g4_i, pc, "rs_swap_in"), pc, "rs_add_pair_in")
    reducing["in"] = _rs_start(pair_i, g_i, "rs_start_in")
    dh1 = _matmul(dproj, win_full, "nt", F32, "proj_dx", tm=1024, tn=1024, after=reducing["in"][4])
    dx, dg_norm1 = _rms_bwd(xf, norm1_g, dh1, dx1, "norm1_bwd", False)

    dqg64 = dqg[:, :HEAD_DIM] + dqg[:, HEAD_DIM:]
    dkg64 = dkg[:, :HEAD_DIM] + dkg[:, HEAD_DIM:]
    small_g_local = [dg_norm1, dqg64, dkg64, dsk[:, :d.n_heads], dlg, dlb, dwsp, dbsp, dgoa, dgog, dg_norm2,
                     loss_local]
    ar_send, ar_recv, ar_own, ar_land, ar_token = _ar_start(_pack(small_g_local))

    big_m = [m_w_in[0], m_w_out[0], m_w_ffn_gate[0], m_w_ffn_up[0], m_w_ffn_down[0]]
    big_v = [v_w_in[0], v_w_out[0], v_w_ffn_gate[0], v_w_ffn_up[0], v_w_ffn_down[0]]
    big_grads, big_d, big_nm, big_nv = [], [], [], []
    for w, m, v, n in zip(big, big_m, big_v, names):
        send, recv, pair, land, _ = reducing[n]
        pair, land = _rs_wait(pair, land, send, recv, ar_token, "rs_wait_" + n)
        own, got = _add_chips(pair, land, pc, "rs_add_chips_" + n)
        outs = _adamw_halves(w, own, got, m, v, pc, "adamw_" + n)
        for lst, o in zip((big_grads, big_d, big_nm, big_nv), outs):
            lst.append(o.reshape(w.shape))

    small_names_w = [norm1_g, q_norm_g, k_norm_g, attn_sinks, gate_ln_g, gate_ln_b, w_spatial, b_spatial,
                     out_norm_attn_g, out_norm_gate_g, norm2_g]
    small_m = [m_norm1_g, m_q_norm_g, m_k_norm_g, m_attn_sinks, m_gate_ln_g, m_gate_ln_b, m_w_spatial, m_b_spatial,
               m_out_norm_attn_g, m_out_norm_gate_g, m_norm2_g]
    small_v = [v_norm1_g, v_q_norm_g, v_k_norm_g, v_attn_sinks, v_gate_ln_g, v_gate_ln_b, v_w_spatial, v_b_spatial,
               v_out_norm_attn_g, v_out_norm_gate_g, v_norm2_g]
    shapes = [w.shape for w in small_names_w] + [loss_local.shape]
    ride = [jnp.zeros(loss_local.shape, F32)]
    ar_own, ar_land = _ar_wait(ar_own, ar_land, ar_send, ar_recv, big_nv[-1])
    me = (4 * lax.axis_index("x") + 2 * lax.axis_index("y") + lax.axis_index("c")).astype(jnp.int32).reshape(1)
    sg = _ar_sum(ar_own, ar_land, me)
    sd, snm, snv = _adamw(_pack(small_names_w + ride), sg, _pack(small_m + ride), _pack(small_v + ride), "adamw_small")
    small_g, small_d, small_nm, small_nv = (_unpack(t, shapes) for t in (sg, sd, snm, snv))
    loss = small_g[-1][0, 0]

    def order(small, bigs):
        s = list(small)
        bg = [t[None] for t in bigs]
        return [s[0], bg[0], s[1], s[2], s[3], s[4], s[5], s[6], s[7], s[8], s[9], bg[1], s[10], bg[2], bg[3], bg[4]]

    grad_x = dx.reshape(bl, seq, D)
    return (loss, grad_x, *order(small_g, big_grads), *order(small_d, big_d), *order(small_nm, big_nm),
            *order(small_nv, big_nv))
```

```python
import functools

import jax
import jax.numpy as jnp
from jax import lax
from jax.experimental import pallas as pl
from jax.experimental.pallas import tpu as pltpu

F32 = jnp.float32
BF16 = jnp.bfloat16
MESH = pl.DeviceIdType.MESH

EPS = 1e-6
HEAD_DIM = 64
N_KV_HEADS = 2
BLK = 128
N_CHIPS = 4
N_DEV = 8
NEG = -1e30

ADAM_LR = 0.001
ADAM_B1 = 0.9
ADAM_B2 = 0.999
ADAM_EPS = 1e-08
ADAM_WD = 0.01
ADAM_STEP = 10

VMEM_LIMIT = 56 * 1024 * 1024

NN = (((1,), (0,)), ((), ()))
NT = (((1,), (1,)), ((), ()))
TN = (((0,), (0,)), ((), ()))
HBM = pl.BlockSpec(memory_space=pltpu.HBM)
ANY = pl.BlockSpec(memory_space=pl.ANY)
SEM = pl.BlockSpec(memory_space=pltpu.SEMAPHORE)
EFFECT = pltpu.SideEffectType.DATAFLOW_SIDE_EFFECTING


def _dot(a, b, dn):
    return lax.dot_general(a, b, dn, preferred_element_type=F32)


def _pick(dim, pref, align=128):
    if dim <= pref:
        return dim
    t = (pref // align) * align
    while t >= align:
        if dim % t == 0:
            return t
        t -= align
    return dim


def _params(sem):
    return pltpu.CompilerParams(dimension_semantics=sem, vmem_limit_bytes=VMEM_LIMIT)


MM_CHUNK = 512


def _col_chunks(tn):
    return [slice(c0, min(c0 + MM_CHUNK, tn)) for c0 in range(0, tn, MM_CHUNK)]


def _mm_body(dn, nk, has_add, has_after, *refs):
    a_ref, b_ref = refs[:2]
    add_ref = refs[2] if has_add else None
    o_ref = refs[2 + has_add + has_after]
    chunks = _col_chunks(o_ref.shape[-1])

    def dot(cols):
        return _dot(a_ref[...], b_ref[cols, :] if dn == NT else b_ref[:, cols], dn)

    def finish(cols, r):
        if add_ref is not None:
            r = r + add_ref[:, cols]
        o_ref[:, cols] = r.astype(o_ref.dtype)

    if nk == 1:
        for cols in chunks:
            finish(cols, dot(cols))
        return
    acc_ref = refs[-1]
    k = pl.program_id(2)

    @pl.when(k == 0)
    def _():
        for cols in chunks:
            acc_ref[:, cols] = dot(cols)

    if nk > 2:
        @pl.when(jnp.logical_and(k > 0, k < nk - 1))
        def _():
            for cols in chunks:
                acc_ref[:, cols] += dot(cols)

    @pl.when(k == nk - 1)
    def _():
        for cols in chunks:
            finish(cols, acc_ref[:, cols] + dot(cols))


def _matmul(a, b, mode, out_dtype, name, *, tm, tn, tk=None, add=None, out_slab=None, after=None):
    if mode == "nn":
        (M, K), N = a.shape, b.shape[1]
    elif mode == "nt":
        (M, K), N = a.shape, b.shape[0]
    else:
        (K, M), N = a.shape, b.shape[1]
    tk = K if tk is None else tk
    tm, tn, tk = _pick(M, tm), _pick(N, tn), _pick(K, tk)
    if out_slab == "c":
        tn = _pick(N // N_CHIPS, tn)
    if out_slab == "r":
        tm = _pick(M // N_CHIPS, tm)
    gm, gn, gk = M // tm, N // tn, K // tk

    if mode == "tn":
        a_spec = pl.BlockSpec((tk, tm), lambda j, i, k: (k, i))
        b_spec = pl.BlockSpec((tk, tn), lambda j, i, k: (k, j))
    else:
        a_spec = pl.BlockSpec((tm, tk), lambda j, i, k: (i, k))
        if mode == "nn":
            b_spec = pl.BlockSpec((tk, tn), lambda j, i, k: (k, j))
        else:
            b_spec = pl.BlockSpec((tn, tk), lambda j, i, k: (j, k))

    if out_slab == "c":
        per = (N // N_CHIPS) // tn
        o_spec = pl.BlockSpec((None, tm, tn), lambda j, i, k: (j // per, i, j % per))
        o_shape = jax.ShapeDtypeStruct((N_CHIPS, M, N // N_CHIPS), out_dtype)
    elif out_slab == "r":
        per = (M // N_CHIPS) // tm
        o_spec = pl.BlockSpec((None, tm, tn), lambda j, i, k: (i // per, i % per, j))
        o_shape = jax.ShapeDtypeStruct((N_CHIPS, M // N_CHIPS, N), out_dtype)
    else:
        o_spec = pl.BlockSpec((tm, tn), lambda j, i, k: (i, j))
        o_shape = jax.ShapeDtypeStruct((M, N), out_dtype)

    dn = {"nn": NN, "nt": NT, "tn": TN}[mode]
    in_specs = [a_spec, b_spec]
    args = [a, b]
    if add is not None:
        in_specs.append(pl.BlockSpec((tm, tn), lambda j, i, k: (i, j)))
        args.append(add)
    if after is not None:
        in_specs.append(ANY)
        args.append(after)
    return pl.pallas_call(
        functools.partial(_mm_body, dn, gk, add is not None, after is not None),
        name=name,
        grid=(gn, gm, gk),
        in_specs=in_specs,
        out_specs=o_spec,
        out_shape=o_shape,
        scratch_shapes=[pltpu.VMEM((tm, tn), F32)] if gk > 1 else [],
        compiler_params=_params(("parallel", "parallel", "arbitrary")),
    )(*args)


def _rms_fwd_body(x_ref, g_ref, *rest):
    h_ref = rest[-1]
    x = x_ref[...]
    r = lax.rsqrt(jnp.mean(x * x, axis=-1, keepdims=True) + EPS)
    h_ref[...] = (x * r * g_ref[...]).astype(h_ref.dtype)


def _rms_fwd(x, g, name, after=None):
    T, D = x.shape
    tr = _pick(T, 256, 16)
    extra = [] if after is None else [after]
    return pl.pallas_call(
        functools.partial(_rms_fwd_body),
        name=name,
        grid=(T // tr,),
        in_specs=[pl.BlockSpec((tr, D), lambda i: (i, 0)), pl.BlockSpec((1, D), lambda i: (0, 0))] + [ANY] * len(extra),
        out_specs=pl.BlockSpec((tr, D), lambda i: (i, 0)),
        out_shape=jax.ShapeDtypeStruct((T, D), BF16),
        compiler_params=_params(("parallel",)),
    )(x, g, *extra)


def _rms_bwd_body(with_bf16, x_ref, g_ref, dh_ref, res_ref, dx_ref, *rest):
    dg_ref = rest[-1]

    @pl.when(pl.program_id(0) == 0)
    def _():
        dg_ref[...] = jnp.zeros_like(dg_ref)

    x = x_ref[...]
    r = lax.rsqrt(jnp.mean(x * x, axis=-1, keepdims=True) + EPS)
    xh = x * r
    dh = dh_ref[...]
    dg_ref[...] += jnp.sum(dh * xh, axis=0, keepdims=True)
    t = dh * g_ref[...]
    dx = res_ref[...] + r * (t - xh * jnp.mean(t * xh, axis=-1, keepdims=True))
    dx_ref[...] = dx
    if with_bf16:
        rest[0][...] = dx.astype(BF16)


def _rms_bwd(x, g, dh, res, name, with_bf16):
    T, D = x.shape
    tr = _pick(T, 256, 16)
    row = pl.BlockSpec((tr, D), lambda i: (i, 0))
    vec = pl.BlockSpec((1, D), lambda i: (0, 0))
    extra = [jax.ShapeDtypeStruct((T, D), BF16)] if with_bf16 else []
    return pl.pallas_call(
        functools.partial(_rms_bwd_body, with_bf16),
        name=name,
        grid=(T // tr,),
        in_specs=[row, vec, row, row],
        out_specs=[row] + [row] * len(extra) + [vec],
        out_shape=[jax.ShapeDtypeStruct((T, D), F32)] + extra + [jax.ShapeDtypeStruct((1, D), F32)],
        compiler_params=_params(("arbitrary",)),
    )(x, g, dh, res)


def _ffn_up_body(h_ref, wg_ref, wu_ref, a_ref, b_ref, f_ref):
    for cols in _col_chunks(a_ref.shape[-1]):
        a = _dot(h_ref[...], wg_ref[:, cols], NN)
        b = _dot(h_ref[...], wu_ref[:, cols], NN)
        a_ref[:, cols] = a
        b_ref[:, cols] = b
        f_ref[:, cols] = (a * (1.0 / (1.0 + jnp.exp(-a))) * b).astype(f_ref.dtype)


def _ffn_up(h, wg, wu):
    T, D = h.shape
    F = wg.shape[1]
    tm, tn = _pick(T, 1024), _pick(F, MM_CHUNK)
    hs = pl.BlockSpec((tm, D), lambda j, i: (i, 0))
    ws = pl.BlockSpec((D, tn), lambda j, i: (0, j))
    os = pl.BlockSpec((tm, tn), lambda j, i: (i, j))
    return pl.pallas_call(
        functools.partial(_ffn_up_body),
        name="ffn_up_fwd",
        grid=(F // tn, T // tm),
        in_specs=[hs, ws, ws],
        out_specs=[os, os, os],
        out_shape=[jax.ShapeDtypeStruct((T, F), F32), jax.ShapeDtypeStruct((T, F), F32),
                   jax.ShapeDtypeStruct((T, F), BF16)],
        compiler_params=_params(("parallel", "parallel")),
    )(h, wg, wu)


def _ffn_down_dx_body(dx_ref, wd_ref, a_ref, b_ref, after_ref, da_ref, db_ref):
    del after_ref
    for cols in _col_chunks(da_ref.shape[-1]):
        df = _dot(dx_ref[...], wd_ref[cols, :], NT)
        a = a_ref[:, cols]
        s = 1.0 / (1.0 + jnp.exp(-a))
        da_ref[:, cols] = (df * b_ref[:, cols] * (s * (1.0 + a * (1.0 - s)))).astype(da_ref.dtype)
        db_ref[:, cols] = (df * (a * s)).astype(db_ref.dtype)


def _ffn_down_dx(dx2b, wd, a, b, after):
    T, D = dx2b.shape
    F = wd.shape[0]
    tm, tn = _pick(T, 512), _pick(F, 1408)
    xs = pl.BlockSpec((tm, D), lambda j, i: (i, 0))
    ws = pl.BlockSpec((tn, D), lambda j, i: (j, 0))
    os = pl.BlockSpec((tm, tn), lambda j, i: (i, j))
    return pl.pallas_call(
        functools.partial(_ffn_down_dx_body),
        name="ffn_down_dx",
        grid=(F // tn, T // tm),
        in_specs=[xs, ws, os, os, ANY],
        out_specs=[os, os],
        out_shape=[jax.ShapeDtypeStruct((T, F), BF16), jax.ShapeDtypeStruct((T, F), BF16)],
        compiler_params=_params(("parallel", "parallel")),
    )(dx2b, wd, a, b, after)


def _ffn_down_loss_body(nk, inv_d, f_ref, wd_ref, x1_ref, tgt_ref, dx2_ref, dx2b_ref, loss_ref, *scratch):
    j, i, k = pl.program_id(0), pl.program_id(1), pl.program_id(2)
    chunks = _col_chunks(dx2_ref.shape[-1])

    def dot(cols):
        return _dot(f_ref[...], wd_ref[:, cols], NN)

    @pl.when(jnp.logical_and(jnp.logical_and(j == 0, i == 0), k == 0))
    def _():
        loss_ref[...] = jnp.zeros_like(loss_ref)

    def finish(ffn_of):
        total = jnp.zeros((1, 1), F32)
        for cols in chunks:
            e = ffn_of(cols) + x1_ref[:, cols] - tgt_ref[:, cols]
            dx2 = e * inv_d
            dx2_ref[:, cols] = dx2
            dx2b_ref[:, cols] = dx2.astype(BF16)
            total = total + jnp.sum(jnp.sum(e * e, axis=-1, keepdims=True), axis=0, keepdims=True)
        loss_ref[...] += (0.5 * inv_d) * total

    if nk == 1:
        finish(dot)
        return
    acc_ref = scratch[0]

    @pl.when(k == 0)
    def _():
        for cols in chunks:
            acc_ref[:, cols] = dot(cols)

    if nk > 2:
        @pl.when(jnp.logical_and(k > 0, k < nk - 1))
        def _():
            for cols in chunks:
                acc_ref[:, cols] += dot(cols)

    @pl.when(k == nk - 1)
    def _():
        finish(lambda cols: acc_ref[:, cols] + dot(cols))


def _ffn_down_loss(f, wd, x1, tgt):
    T, F = f.shape
    D = wd.shape[1]
    tm, tn, tk = _pick(T, 1024), _pick(D, 1024), _pick(F, 2816)
    gm, gn, gk = T // tm, D // tn, F // tk
    tile = pl.BlockSpec((tm, tn), lambda j, i, k: (i, j))
    return pl.pallas_call(
        functools.partial(_ffn_down_loss_body, gk, 1.0 / D),
        name="ffn_down_loss",
        grid=(gn, gm, gk),
        in_specs=[pl.BlockSpec((tm, tk), lambda j, i, k: (i, k)), pl.BlockSpec((tk, tn), lambda j, i, k: (k, j)),
                  tile, tile],
        out_specs=[tile, tile, pl.BlockSpec((1, 1), lambda j, i, k: (0, 0))],
        out_shape=[jax.ShapeDtypeStruct((T, D), F32), jax.ShapeDtypeStruct((T, D), BF16),
                   jax.ShapeDtypeStruct((1, 1), F32)],
        scratch_shapes=[pltpu.VMEM((tm, tn), F32)] if gk > 1 else [],
        compiler_params=_params(("arbitrary", "arbitrary", "arbitrary")),
    )(f, wd, x1, tgt)


def _lo_mask(shape):
    return lax.broadcasted_iota(jnp.int32, shape, len(shape) - 1) < HEAD_DIM


def _half_sums(t, lo):
    s_lo = jnp.sum(jnp.where(lo, t, 0.0), axis=-1, keepdims=True)
    s_hi = jnp.sum(jnp.where(lo, 0.0, t), axis=-1, keepdims=True)
    return jnp.where(lo, s_lo, s_hi)


def _head_rstd(t, lo):
    return lax.rsqrt(_half_sums(t * t, lo) * (1.0 / HEAD_DIM) + EPS)


def _place(t, lo, kv_head):
    if kv_head == 0:
        t_lo = jnp.where(lo, t, 0.0)
        t_hi = pltpu.roll(t_lo, HEAD_DIM, 1)
    else:
        t_hi = jnp.where(lo, 0.0, t)
        t_lo = pltpu.roll(t_hi, HEAD_DIM, 1)
    return jnp.concatenate([t_lo, t_hi], axis=0).astype(BF16)


def _unplace(c0, c1, lo):
    return jnp.where(lo, c0 + pltpu.roll(c0, HEAD_DIM, 1), c1 + pltpu.roll(c1, HEAD_DIM, 1))


def _band(kv_cur, kv_prev, kg, lo2):
    kb = jnp.concatenate([kv_prev[:, :BLK], kv_cur[:, :BLK]], axis=0)
    vb = jnp.concatenate([kv_prev[:, BLK:], kv_cur[:, BLK:]], axis=0)
    rk = _head_rstd(kb, lo2)
    kn = kb * rk * kg
    kk = [_place(kn, lo2, h) for h in range(N_KV_HEADS)]
    vv = [_place(vb, lo2, h) for h in range(N_KV_HEADS)]
    return kb, rk, kk, vv


def _score_geometry(first_i32):
    qi = lax.broadcasted_iota(jnp.int32, (BLK, 4 * BLK), 0)
    col = lax.broadcasted_iota(jnp.int32, (BLK, 4 * BLK), 1)
    kj = col & (2 * BLK - 1)
    dist = qi + BLK - kj
    valid = (dist >= 0) & (dist < BLK) & (kj >= first_i32 * BLK)
    return col, dist.astype(F32), valid


def _pair_logits(qn, kk, col, distf, valid, slope0, slope1):
    s = _dot(qn.astype(BF16), kk, NT) * (HEAD_DIM ** -0.5)
    slope = jnp.where(col < 2 * BLK, slope0, slope1)
    return jnp.where(valid, s - slope * distf, NEG)


def _pair_probs(qn, kk, col, distf, valid, slope0, slope1, sink0, sink1):
    return _softmax_halves(_pair_logits(qn, kk, col, distf, valid, slope0, slope1), sink0, sink1)


def _softmax_halves(logits, sink0, sink1):
    probs, psink = [], []
    for hh, sk in ((0, sink0), (1, sink1)):
        l = logits[:, 2 * BLK * hh:2 * BLK * (hh + 1)]
        m = jnp.maximum(jnp.max(l, axis=-1, keepdims=True), sk)
        p = jnp.exp(l - m)
        es = jnp.exp(sk - m)
        inv = 1.0 / (jnp.sum(p, axis=-1, keepdims=True) + es)
        probs.append(p * inv)
        psink.append(es * inv)
    return probs, psink


def _gelu(z, with_grad=False):
    cdf = 0.5 * (1.0 + lax.erf(z * (0.5 ** 0.5)))
    if not with_grad:
        return z * cdf
    return z * cdf, cdf + z * jnp.exp(-0.5 * z * z) * ((2.0 * jnp.pi) ** -0.5)


def _tril_w(w):
    r = lax.broadcasted_iota(jnp.int32, (BLK, BLK), 0)
    c = lax.broadcasted_iota(jnp.int32, (BLK, BLK), 1)
    return jnp.where(r >= c, w, 0.0), r >= c


def _gate_fwd_group(zu, zv, lg, lb, w, bcol, with_grad=False):
    u, v = _gelu(zu, with_grad), _gelu(zv, with_grad)
    if with_grad:
        (u, du_dz), (v, dv_dz) = u, v
    mu = jnp.mean(v, axis=-1, keepdims=True)
    vc = v - mu
    rs = lax.rsqrt(jnp.mean(vc * vc, axis=-1, keepdims=True) + EPS)
    vh = vc * rs
    vn = vh * lg + lb
    wt, tril = _tril_w(w)
    mixed = _dot(wt.astype(BF16), vn.astype(BF16), NN) + bcol
    if with_grad:
        return u, vh, rs, vn, wt, tril, mixed, du_dz, dv_dz
    return u, vh, rs, vn, wt, tril, mixed


class _Dims:
    def __init__(self, seq, attn_w, gate_w):
        self.seq, self.attn_w, self.gate_w = seq, attn_w, gate_w
        self.n_heads = attn_w // HEAD_DIM
        self.group = self.n_heads // N_KV_HEADS
        self.n_pairs = attn_w // BLK
        self.n_groups = gate_w // BLK
        self.kv_col = attn_w // (2 * BLK)
        self.u0 = attn_w + 2 * BLK
        self.v0 = self.u0 + gate_w
        self.in_w = self.v0 + gate_w
        self.slopes = [2.0 ** (-8.0 * (h + 1) / self.n_heads) for h in range(self.n_heads)]


def _mixer_fwd_body(d, sink_ref, proj_ref, kvp_ref, qg_ref, kg_ref, lg_ref, lb_ref, w_ref, b_ref, goa_ref, gog_ref,
                    ya_ref, yg_ref, y_ref, logit_scr, prob_scr):
    i = pl.program_id(0)
    first = (i % (d.seq // BLK) == 0).astype(jnp.int32)
    lo = _lo_mask((BLK, BLK))
    lo2 = _lo_mask((2 * BLK, BLK))
    kv_cur = proj_ref[:, d.attn_w:d.attn_w + 2 * BLK]
    _, _, kk, vv = _band(kv_cur, kvp_ref[...], kg_ref[...], lo2)
    col, distf, valid = _score_geometry(first)
    qg = qg_ref[...]
    for j in range(d.n_pairs):
        h0, h1 = 2 * j, 2 * j + 1
        q2 = proj_ref[:, BLK * j:BLK * (j + 1)]
        qn = q2 * _head_rstd(q2, lo) * qg
        logit_scr[j] = _pair_logits(qn, kk[h0 // d.group], col, distf, valid, d.slopes[h0], d.slopes[h1])
    for j in range(d.n_pairs):
        probs, _ = _softmax_halves(logit_scr[j], sink_ref[0, 2 * j], sink_ref[0, 2 * j + 1])
        prob_scr[j] = jnp.concatenate(probs, axis=1).astype(BF16)
    for j in range(d.n_pairs):
        ya_ref[:, BLK * j:BLK * (j + 1)] = _dot(prob_scr[j], vv[2 * j // d.group], NN)
    for g in range(d.n_groups):
        zu = proj_ref[:, d.u0 + BLK * g:d.u0 + BLK * (g + 1)]
        zv = proj_ref[:, d.v0 + BLK * g:d.v0 + BLK * (g + 1)]
        u, _, _, _, _, _, mixed = _gate_fwd_group(zu, zv, lg_ref[g:g + 1, :], lb_ref[g:g + 1, :], w_ref[g], b_ref[g])
        yg_ref[:, BLK * g:BLK * (g + 1)] = u * mixed
    ya = ya_ref[...]
    ra = lax.rsqrt(jnp.mean(ya * ya, axis=-1, keepdims=True) + EPS)
    y_ref[:, :d.attn_w] = (ya * ra * goa_ref[...]).astype(y_ref.dtype)
    yg = yg_ref[...]
    rg = lax.rsqrt(jnp.mean(yg * yg, axis=-1, keepdims=True) + EPS)
    y_ref[:, d.attn_w:] = (yg * rg * gog_ref[...]).astype(y_ref.dtype)


def _mixer_specs(d, T):
    row = lambda w: pl.BlockSpec((BLK, w), lambda i: (i, 0))
    const2 = lambda a: pl.BlockSpec(a.shape, lambda i: (0, 0))
    const3 = lambda a: pl.BlockSpec(a.shape, lambda i: (0, 0, 0))
    kv_prev = pl.BlockSpec((BLK, 2 * BLK), lambda i: (jnp.maximum(i - 1, 0), d.kv_col))
    return row, const2, const3, kv_prev


def _mixer_fwd(d, proj, sinks, qg2, kg2, lg, lb, wsp, bcol, goa, gog):
    T = proj.shape[0]
    row, const2, const3, kv_prev = _mixer_specs(d, T)
    return pl.pallas_call(
        functools.partial(_mixer_fwd_body, d),
        name="mixer_fwd",
        grid=(T // BLK,),
        in_specs=[pl.BlockSpec(memory_space=pltpu.SMEM), row(d.in_w), kv_prev, const2(qg2), const2(kg2),
                  const2(lg), const2(lb), const3(wsp), const3(bcol), const2(goa), const2(gog)],
        out_specs=[row(d.attn_w), row(d.gate_w), row(d.attn_w + d.gate_w)],
        out_shape=[jax.ShapeDtypeStruct((T, d.attn_w), F32), jax.ShapeDtypeStruct((T, d.gate_w), F32),
                   jax.ShapeDtypeStruct((T, d.attn_w + d.gate_w), BF16)],
        scratch_shapes=[pltpu.VMEM((d.n_pairs, BLK, 4 * BLK), F32), pltpu.VMEM((d.n_pairs, BLK, 4 * BLK), BF16)],
        compiler_params=_params(("parallel",)),
    )(sinks, proj, proj, qg2, kg2, lg, lb, wsp, bcol, goa, gog)


def _mixer_bwd_body(d, sink_ref, proj_ref, kvp_ref, ya_ref, yg_ref, dy_ref, qg_ref, kg_ref, lg_ref, lb_ref, w_ref,
                    b_ref, goa_ref, gog_ref,
                    dproj_ref, dkv_ref, dqg_ref, dkg_ref, dsk_ref, dlg_ref, dlb_ref, dw_ref, db_ref, dgoa_ref,
                    dgog_ref):
    i = pl.program_id(0)

    @pl.when(i == 0)
    def _():
        for r in (dqg_ref, dkg_ref, dsk_ref, dlg_ref, dlb_ref, dw_ref, db_ref, dgoa_ref, dgog_ref):
            r[...] = jnp.zeros_like(r)

    first = (i % (d.seq // BLK) == 0).astype(jnp.int32)
    lo = _lo_mask((BLK, BLK))
    lo2 = _lo_mask((2 * BLK, BLK))
    lane_row = lax.broadcasted_iota(jnp.int32, (1, BLK), 1)

    ya = ya_ref[...]
    ra = lax.rsqrt(jnp.mean(ya * ya, axis=-1, keepdims=True) + EPS)
    yah = ya * ra
    dyn = dy_ref[:, :d.attn_w]
    dgoa_ref[...] += jnp.sum(dyn * yah, axis=0, keepdims=True)
    t = dyn * goa_ref[...]
    dya = ra * (t - yah * jnp.mean(t * yah, axis=-1, keepdims=True))
    yg = yg_ref[...]
    rg = lax.rsqrt(jnp.mean(yg * yg, axis=-1, keepdims=True) + EPS)
    ygh = yg * rg
    dyn = dy_ref[:, d.attn_w:]
    dgog_ref[...] += jnp.sum(dyn * ygh, axis=0, keepdims=True)
    t = dyn * gog_ref[...]
    dyg = rg * (t - ygh * jnp.mean(t * ygh, axis=-1, keepdims=True))

    kv_cur = proj_ref[:, d.attn_w:d.attn_w + 2 * BLK]
    kg = kg_ref[...]
    kb, rk, kk, vv = _band(kv_cur, kvp_ref[...], kg, lo2)
    col, distf, valid = _score_geometry(first)
    qg = qg_ref[...]
    ck = [jnp.zeros((BLK, 2 * BLK), F32) for _ in range(N_KV_HEADS)]
    cv = [jnp.zeros((BLK, 2 * BLK), F32) for _ in range(N_KV_HEADS)]
    lo_rows = lax.broadcasted_iota(jnp.int32, (BLK, 2 * BLK), 0) < HEAD_DIM
    dsk = jnp.zeros((1, BLK), F32)
    dqg = jnp.zeros((1, BLK), F32)
    for j in range(d.n_pairs):
        h0, h1 = 2 * j, 2 * j + 1
        kh = h0 // d.group
        cols = slice(BLK * j, BLK * (j + 1))
        q2 = proj_ref[:, cols]
        rq = _head_rstd(q2, lo)
        qh = q2 * rq
        qn = qh * qg
        probs, psink = _pair_probs(qn, kk[kh], col, distf, valid, d.slopes[h0], d.slopes[h1],
                                   sink_ref[0, h0], sink_ref[0, h1])
        do2 = dya[:, cols]
        prod = do2 * ya[:, cols]
        delta = (jnp.sum(jnp.where(lo, prod, 0.0), axis=-1, keepdims=True),
                 jnp.sum(jnp.where(lo, 0.0, prod), axis=-1, keepdims=True))
        do2b = do2.astype(BF16)
        dp = _dot(do2b, vv[kh], NT)
        ds = []
        for hh in (0, 1):
            ds.append(probs[hh] * (dp[:, 2 * BLK * hh:2 * BLK * (hh + 1)] - delta[hh]))
            dsink = -jnp.sum(psink[hh] * delta[hh], axis=0, keepdims=True)
            dsk = dsk + jnp.where(lane_row == (h0 + hh), dsink, 0.0)
        dsb = (jnp.concatenate(ds, axis=1) * (HEAD_DIM ** -0.5)).astype(BF16)
        pb = jnp.concatenate(probs, axis=1).astype(BF16)
        qnb = qn.astype(BF16)
        dqn = _dot(dsb, kk[kh], NN)
        dkk = _dot(qnb, dsb, TN)
        dvv = _dot(do2b, pb, TN)
        ck[kh] = ck[kh] + jnp.where(lo_rows, dkk[:, :2 * BLK], 0.0) + jnp.where(lo_rows, 0.0, dkk[:, 2 * BLK:])
        cv[kh] = cv[kh] + jnp.where(lo_rows, dvv[:, :2 * BLK], 0.0) + jnp.where(lo_rows, 0.0, dvv[:, 2 * BLK:])
        dqg = dqg + jnp.sum(dqn * qh, axis=0, keepdims=True)
        t = dqn * qg
        dq2 = rq * (t - qh * (_half_sums(t * qh, lo) * (1.0 / HEAD_DIM)))
        dproj_ref[:, cols] = dq2.astype(dproj_ref.dtype)
    dsk_ref[...] += dsk
    dqg_ref[...] += dqg
    dkn = _unplace(jnp.transpose(ck[0]), jnp.transpose(ck[1]), lo2)
    dvb = _unplace(jnp.transpose(cv[0]), jnp.transpose(cv[1]), lo2)
    khat = kb * rk
    dkg_ref[...] += jnp.sum(dkn * khat, axis=0, keepdims=True)
    t = dkn * kg
    dkb = rk * (t - khat * (_half_sums(t * khat, lo2) * (1.0 / HEAD_DIM)))
    rows_cur = pl.ds(pl.multiple_of(i * BLK, BLK), BLK)
    rows_prev = pl.ds(pl.multiple_of(jnp.maximum(i - 1, 0) * BLK, BLK), BLK)
    dkv_ref[rows_cur, :] = jnp.concatenate([dkb[BLK:], dvb[BLK:]], axis=1)
    dkv_ref[rows_prev, :] += jnp.concatenate([dkb[:BLK], dvb[:BLK]], axis=1)
    dproj_ref[:, d.attn_w:d.attn_w + 2 * BLK] = jnp.zeros((BLK, 2 * BLK), dproj_ref.dtype)

    for g in range(d.n_groups):
        ucols = slice(d.u0 + BLK * g, d.u0 + BLK * (g + 1))
        vcols = slice(d.v0 + BLK * g, d.v0 + BLK * (g + 1))
        zu = proj_ref[:, ucols]
        zv = proj_ref[:, vcols]
        lg = lg_ref[g:g + 1, :]
        u, vh, rs, vn, wt, tril, mixed, du_dz, dv_dz = _gate_fwd_group(
            zu, zv, lg, lb_ref[g:g + 1, :], w_ref[g], b_ref[g], with_grad=True)
        dyg_g = dyg[:, BLK * g:BLK * (g + 1)]
        du = dyg_g * mixed
        dmix = dyg_g * u
        dmb = dmix.astype(BF16)
        db_ref[g:g + 1, :] += jnp.sum(jnp.transpose(dmix), axis=0, keepdims=True)
        dw_ref[g] += jnp.where(tril, _dot(dmb, vn.astype(BF16), NT), 0.0)
        dvn = _dot(wt.astype(BF16), dmb, TN)
        dlg_ref[g:g + 1, :] += jnp.sum(dvn * vh, axis=0, keepdims=True)
        dlb_ref[g:g + 1, :] += jnp.sum(dvn, axis=0, keepdims=True)
        dvh = dvn * lg
        dv = rs * (dvh - jnp.mean(dvh, axis=-1, keepdims=True) - vh * jnp.mean(dvh * vh, axis=-1, keepdims=True))
        dproj_ref[:, ucols] = (du * du_dz).astype(dproj_ref.dtype)
        dproj_ref[:, vcols] = (dv * dv_dz).astype(dproj_ref.dtype)


def _mixer_bwd(d, proj, ya, yg, dy, sinks, qg2, kg2, lg, lb, wsp, bcol, goa, gog):
    T = proj.shape[0]
    row, const2, const3, kv_prev = _mixer_specs(d, T)
    acc2 = lambda s: pl.BlockSpec(s, lambda i: (0, 0))
    G = d.n_groups
    out_shapes = [((T, d.in_w), BF16), ((T, 2 * BLK), F32), ((1, BLK), F32), ((1, BLK), F32), ((1, BLK), F32),
                  ((G, BLK), F32), ((G, BLK), F32), ((G, BLK, BLK), F32), ((G, BLK), F32),
                  ((1, d.attn_w), F32), ((1, d.gate_w), F32)]
    out_specs = [row(d.in_w)] + [acc2(s) for s, _ in out_shapes[1:7]] + \
                [pl.BlockSpec((G, BLK, BLK), lambda i: (0, 0, 0))] + [acc2(s) for s, _ in out_shapes[8:]]
    return pl.pallas_call(
        functools.partial(_mixer_bwd_body, d),
        name="mixer_bwd",
        grid=(T // BLK,),
        in_specs=[pl.BlockSpec(memory_space=pltpu.SMEM), row(d.in_w), kv_prev, row(d.attn_w), row(d.gate_w),
                  row(d.attn_w + d.gate_w), const2(qg2), const2(kg2), const2(lg), const2(lb), const3(wsp),
                  const3(bcol), const2(goa), const2(gog)],
        out_specs=out_specs,
        out_shape=[jax.ShapeDtypeStruct(s, t) for s, t in out_shapes],
        compiler_params=_params(("arbitrary",)),
    )(sinks, proj, proj, ya, yg, dy, qg2, kg2, lg, lb, wsp, bcol, goa, gog)


def _put_kv_body(dkv_ref, dproj_in_ref, dproj_ref):
    del dproj_in_ref
    dproj_ref[...] = dkv_ref[...].astype(dproj_ref.dtype)


def _put_kv(d, dproj, dkv):
    T = dproj.shape[0]
    tr = _pick(T, 1024, 16)
    return pl.pallas_call(
        functools.partial(_put_kv_body),
        name="put_kv",
        grid=(T // tr,),
        in_specs=[pl.BlockSpec((tr, 2 * BLK), lambda i: (i, 0)), pl.BlockSpec(memory_space=pl.ANY)],
        out_specs=pl.BlockSpec((tr, 2 * BLK), lambda i: (i, d.kv_col)),
        out_shape=jax.ShapeDtypeStruct(dproj.shape, dproj.dtype),
        input_output_aliases={1: 0},
        compiler_params=_params(("parallel",)),
    )(dkv, dproj)


def _add_pair_body(pc_ref, own_ref, got_ref, o_ref):
    del pc_ref
    o_ref[...] = (own_ref[...].astype(F32) + got_ref[...].astype(F32)).astype(o_ref.dtype)


def _add_pair(g4, got, pc, name):
    n, _, h, C = g4.shape
    tr = _pick(h, 512, 16)
    return pl.pallas_call(
        functools.partial(_add_pair_body),
        name=name,
        grid_spec=pltpu.PrefetchScalarGridSpec(
            num_scalar_prefetch=1,
            grid=(n, h // tr),
            in_specs=[pl.BlockSpec((None, None, tr, C), lambda q, i, pc: (q, pc[1], i, 0)),
                      pl.BlockSpec((None, tr, C), lambda q, i, pc: (q, i, 0))],
            out_specs=pl.BlockSpec((None, tr, C), lambda q, i, pc: (q, i, 0)),
        ),
        out_shape=jax.ShapeDtypeStruct((n, h, C), g4.dtype),
        compiler_params=_params(("parallel", "parallel")),
    )(pc, g4, got)


def _adamw_update(w, g, m, v):
    m = ADAM_B1 * m + (1.0 - ADAM_B1) * g
    v = ADAM_B2 * v + (1.0 - ADAM_B2) * (g * g)
    m_hat = m / (1.0 - ADAM_B1 ** ADAM_STEP)
    v_hat = v / (1.0 - ADAM_B2 ** ADAM_STEP)
    return -ADAM_LR * (m_hat / (jnp.sqrt(v_hat) + ADAM_EPS) + ADAM_WD * w), m, v


def _adamw_body(w_ref, g_ref, m_ref, v_ref, d_ref, nm_ref, nv_ref):
    d_ref[...], nm_ref[...], nv_ref[...] = _adamw_update(w_ref[...], g_ref[...], m_ref[...], v_ref[...])


def _adamw(w, g, m, v, name):
    R, C = w.shape
    tr = _pick(R, 512, 8)
    blk = pl.BlockSpec((tr, C), lambda i: (i, 0))
    return pl.pallas_call(
        functools.partial(_adamw_body),
        name=name,
        grid=(R // tr,),
        in_specs=[blk] * 4,
        out_specs=[blk] * 3,
        out_shape=[jax.ShapeDtypeStruct((R, C), F32)] * 3,
        compiler_params=_params(("parallel",)),
    )(w, g, m, v)


def _adamw_halves_body(pc_ref, w_ref, own_ref, got_ref, m_ref, v_ref, g_ref, d_ref, nm_ref, nv_ref):
    mine = pl.program_id(0) == pc_ref[1]

    def update(g):
        g_ref[...] = g
        d_ref[...], nm_ref[...], nv_ref[...] = _adamw_update(w_ref[...], g, m_ref[...], v_ref[...])

    @pl.when(mine)
    def _():
        update(own_ref[...].astype(F32))

    @pl.when(jnp.logical_not(mine))
    def _():
        update(got_ref[...].astype(F32))


def _adamw_halves(w, own, got, m, v, pc, name):
    h, C = own.shape
    tr = _pick(h, 512, 8)
    full = pl.BlockSpec((None, tr, C), lambda hh, i, pc: (hh, i, 0))
    mine = pl.BlockSpec((tr, C), lambda hh, i, pc: (jnp.where(hh == pc[1], i, 0), 0))
    theirs = pl.BlockSpec((tr, C), lambda hh, i, pc: (jnp.where(hh == pc[1], 0, i), 0))
    return pl.pallas_call(
        functools.partial(_adamw_halves_body),
        name=name,
        grid_spec=pltpu.PrefetchScalarGridSpec(
            num_scalar_prefetch=1,
            grid=(2, h // tr),
            in_specs=[full, mine, theirs, full, full],
            out_specs=[full] * 4,
        ),
        out_shape=[jax.ShapeDtypeStruct((2, h, C), F32)] * 4,
        compiler_params=_params(("parallel", "parallel")),
    )(pc, w.reshape(2, h, C), own, got, m.reshape(2, h, C), v.reshape(2, h, C))


def _me():
    x, y, c = lax.axis_index("x"), lax.axis_index("y"), lax.axis_index("c")
    chips = [(1 - x, y), (x, 1 - y), (1 - x, 1 - y)]
    return x, y, c, chips


def _cast_into_body(pc_ref, w_ref, o_ref):
    del pc_ref
    o_ref[...] = w_ref[...].astype(o_ref.dtype)


def _cast_into(w, pc, name, side_by_side=False):
    Rs, C = w.shape
    h = Rs // 2
    tr = _pick(h, 512, 16)
    if side_by_side:
        out_spec = pl.BlockSpec((None, tr, C), lambda hh, i, pc: (hh, i, pc[0]))
        out_shape = jax.ShapeDtypeStruct((2, h, N_CHIPS * C), BF16)
    else:
        out_spec = pl.BlockSpec((None, None, tr, C), lambda hh, i, pc: (pc[0], hh, i, 0))
        out_shape = jax.ShapeDtypeStruct((N_CHIPS, 2, h, C), BF16)
    return pl.pallas_call(
        functools.partial(_cast_into_body),
        name=name,
        grid_spec=pltpu.PrefetchScalarGridSpec(
            num_scalar_prefetch=1,
            grid=(2, h // tr),
            in_specs=[pl.BlockSpec((None, tr, C), lambda hh, i, pc: (hh, i, 0))],
            out_specs=out_spec,
        ),
        out_shape=out_shape,
        compiler_params=_params(("parallel", "parallel")),
    )(pc, w.reshape(2, h, C))


MAX_PIECES = 4


def _send_tile_to_sibling(src_of, dst_of, tr, dst_total, send_sems, recv_sem, last):
    x, y, c, _ = _me()
    pieces = MAX_PIECES if tr % (16 * MAX_PIECES) == 0 else (2 if tr % 32 == 0 else 1)
    n = tr // pieces
    copies = [pltpu.make_async_remote_copy(src_ref=src_of(k * n, n), dst_ref=dst_of(k * n, n), send_sem=send_sems.at[k],
                                           recv_sem=recv_sem, device_id=(x, y, 1 - c), device_id_type=MESH)
              for k in range(pieces)]
    for cp in copies:
        cp.start()
    for cp in copies:
        cp.wait_send()

    @pl.when(last)
    def _():
        pltpu.make_async_remote_copy(src_ref=dst_total, dst_ref=dst_total, send_sem=send_sems.at[0], recv_sem=recv_sem,
                                     device_id=(x, y, 1 - c), device_id_type=MESH).wait_recv()


TILE_SEMS = [pltpu.SemaphoreType.DMA((MAX_PIECES,)), pltpu.SemaphoreType.DMA(())]


def _ag_pair_body(tr, n_i, pc_ref, tile_ref, buf_ref, send_sem, recv_sem):
    j, i = pl.program_id(0), pl.program_id(1)
    q = pc_ref[0] ^ (j + 1)
    c = pc_ref[1]
    r_tile = pl.multiple_of(i * tr, tr)
    last = jnp.logical_and(j == N_CHIPS - 2, i == n_i - 1)
    if len(buf_ref.shape) == 4:
        _send_tile_to_sibling(lambda r0, n: tile_ref.at[:, :, pl.ds(r0, n)],
                              lambda r0, n: buf_ref.at[pl.ds(q, 1), pl.ds(c, 1), pl.ds(r_tile + r0, n)], tr,
                              buf_ref.at[pl.ds(0, N_CHIPS - 1), 0], send_sem, recv_sem, last)
    else:
        cs = buf_ref.shape[2] // N_CHIPS
        cols = pl.ds(pl.multiple_of(q * cs, BLK), cs)
        _send_tile_to_sibling(lambda r0, n: tile_ref.at[:, pl.ds(r0, n)],
                              lambda r0, n: buf_ref.at[pl.ds(c, 1), pl.ds(r_tile + r0, n), cols], tr,
                              buf_ref.at[0, :, pl.ds(0, (N_CHIPS - 1) * cs)], send_sem, recv_sem, last)


def _ag_pair(buf, pc, name):
    if len(buf.shape) == 4:
        _, _, h, C = buf.shape
        tile = lambda tr: pl.BlockSpec((1, 1, tr, C), lambda j, i, pc: (pc[0] ^ (j + 1), pc[1], i, 0))
    else:
        _, h, C = buf.shape
        tile = lambda tr: pl.BlockSpec((1, tr, C // N_CHIPS), lambda j, i, pc: (pc[1], i, pc[0] ^ (j + 1)))
    tr = _pick(h, 512, 16)
    return pl.pallas_call(
        functools.partial(_ag_pair_body, tr, h // tr),
        name=name,
        grid_spec=pltpu.PrefetchScalarGridSpec(
            num_scalar_prefetch=1,
            grid=(N_CHIPS - 1, h // tr),
            in_specs=[tile(tr)],
            out_specs=HBM,
            scratch_shapes=TILE_SEMS,
        ),
        out_shape=jax.ShapeDtypeStruct(buf.shape, buf.dtype),
        input_output_aliases={1: 0},
        compiler_params=_params(("arbitrary", "arbitrary")),
    )(pc, buf)


def _swap_halves_body(tr, n_q, n_i, pc_ref, tile_ref, got_ref, send_sem, recv_sem):
    del pc_ref
    q, i = pl.program_id(0), pl.program_id(1)
    r_tile = pl.multiple_of(i * tr, tr)
    _send_tile_to_sibling(lambda r0, n: tile_ref.at[:, :, pl.ds(r0, n)],
                          lambda r0, n: got_ref.at[pl.ds(q, 1), :, pl.ds(r_tile + r0, n)], tr, got_ref, send_sem, recv_sem,
                          jnp.logical_and(q == n_q - 1, i == n_i - 1))


def _swap_halves(g4, pc, name):
    n, _, h, C = g4.shape
    tr = _pick(h, 512, 16)
    return pl.pallas_call(
        functools.partial(_swap_halves_body, tr, n, h // tr),
        name=name,
        grid_spec=pltpu.PrefetchScalarGridSpec(
            num_scalar_prefetch=1,
            grid=(n, h // tr),
            in_specs=[pl.BlockSpec((1, 1, tr, C), lambda q, i, pc: (q, 1 - pc[1], i, 0))],
            out_specs=HBM,
            scratch_shapes=TILE_SEMS,
        ),
        out_shape=jax.ShapeDtypeStruct((n, 1, h, C), g4.dtype),
        compiler_params=_params(("arbitrary", "arbitrary")),
    )(pc, g4).reshape(n, h, C)


def _ici_copy(src, dst, send_sems, recv_sems, j, chip, c):
    return pltpu.make_async_remote_copy(src_ref=src, dst_ref=dst, send_sem=send_sems.at[j], recv_sem=recv_sems.at[j],
                                        device_id=(chip[0], chip[1], c), device_id_type=MESH)


def _token_spec():
    return jax.ShapeDtypeStruct((8, BLK), F32), pl.BlockSpec(memory_space=pltpu.VMEM)


def _slab(buf_ref, q, c):
    if len(buf_ref.shape) == 4:
        return buf_ref.at[q, c]
    cs = buf_ref.shape[2] // N_CHIPS
    return buf_ref.at[c, :, pl.ds(pl.multiple_of(q * cs, BLK), cs)]


def _ag_start_body(both_cores, buf_ref, after_ref, send_sems, recv_sems, buf_thru, token_ref):
    del after_ref, buf_thru
    x, y, c, chips = _me()
    mine = _slab(buf_ref, 2 * x + y, c)
    for j, chip in enumerate(chips):
        _ici_copy(mine, mine, send_sems, recv_sems, j, chip, c).start()
    if both_cores:
        for j, chip in enumerate(chips):
            _ici_copy(mine, mine, send_sems, recv_sems, N_CHIPS - 1 + j, chip, 1 - c).start()
    token_ref[...] = jnp.zeros_like(token_ref)


def _ag_start(buf, after, name, both_cores=False):
    tok_shape, tok_spec = _token_spec()
    sems = pltpu.SemaphoreType.DMA(((N_CHIPS - 1) * (2 if both_cores else 1),))
    return pl.pallas_call(
        functools.partial(_ag_start_body, both_cores),
        name=name,
        in_specs=[HBM, ANY],
        out_specs=[SEM, SEM, HBM, tok_spec],
        out_shape=[sems, sems, pltpu.HBM(buf.shape, buf.dtype), tok_shape],
        input_output_aliases={0: 2},
        compiler_params=pltpu.CompilerParams(has_side_effects=EFFECT),
    )(pltpu.with_memory_space_constraint(buf, pltpu.HBM), after)


def _ag_wait_body(both_cores, buf_ref, send_sems, recv_sems, after_ref, buf_out):
    del after_ref, buf_out
    x, y, c, chips = _me()
    mine = _slab(buf_ref, 2 * x + y, c)
    for j, chip in enumerate(chips):
        theirs = _slab(buf_ref, 2 * chip[0] + chip[1], c)
        _ici_copy(mine, mine, send_sems, recv_sems, j, chip, c).wait_send()
        _ici_copy(theirs, theirs, send_sems, recv_sems, j, chip, c).wait_recv()
    if both_cores:
        for j, chip in enumerate(chips):
            theirs = _slab(buf_ref, 2 * chip[0] + chip[1], 1 - c)
            _ici_copy(mine, mine, send_sems, recv_sems, N_CHIPS - 1 + j, chip, 1 - c).wait_send()
            _ici_copy(theirs, theirs, send_sems, recv_sems, N_CHIPS - 1 + j, chip, 1 - c).wait_recv()


def _ag_wait(buf, send_sems, recv_sems, after, name, both_cores=False):
    return pl.pallas_call(
        functools.partial(_ag_wait_body, both_cores),
        name=name,
        in_specs=[HBM, SEM, SEM, ANY],
        out_specs=HBM,
        out_shape=pltpu.HBM(buf.shape, buf.dtype),
        input_output_aliases={0: 0},
        compiler_params=pltpu.CompilerParams(has_side_effects=EFFECT),
    )(buf, send_sems, recv_sems, after)


def _rs_start_body(pair_ref, land_ref, after_ref, send_sems, recv_sems, pair_thru, land_thru, token_ref):
    del after_ref, pair_thru, land_thru
    x, y, c, chips = _me()
    for j, chip in enumerate(chips):
        _ici_copy(pair_ref.at[2 * chip[0] + chip[1]], land_ref.at[j], send_sems, recv_sems, j, chip, c).start()
    token_ref[...] = jnp.zeros_like(token_ref)


def _rs_start(pair, after, name):
    n, h, C = pair.shape
    tok_shape, tok_spec = _token_spec()
    sems = pltpu.SemaphoreType.DMA((N_CHIPS - 1,))
    land = pltpu.with_memory_space_constraint(lax.empty((N_CHIPS - 1, h, C), pair.dtype), pltpu.HBM)
    return pl.pallas_call(
        functools.partial(_rs_start_body),
        name=name,
        in_specs=[HBM, HBM, ANY],
        out_specs=[SEM, SEM, HBM, HBM, tok_spec],
        out_shape=[sems, sems, pltpu.HBM(pair.shape, pair.dtype), pltpu.HBM(land.shape, land.dtype), tok_shape],
        input_output_aliases={0: 2, 1: 3},
        compiler_params=pltpu.CompilerParams(has_side_effects=EFFECT),
    )(pltpu.with_memory_space_constraint(pair, pltpu.HBM), land, after)


def _rs_wait_body(pair_ref, land_ref, send_sems, recv_sems, after_ref, pair_out, land_out):
    del after_ref, pair_out, land_out
    x, y, c, chips = _me()
    for j, chip in enumerate(chips):
        _ici_copy(pair_ref.at[0], land_ref.at[j], send_sems, recv_sems, j, chip, c).wait_send()
        _ici_copy(pair_ref.at[0], land_ref.at[j], send_sems, recv_sems, j, chip, c).wait_recv()


def _rs_wait(pair, land, send_sems, recv_sems, after, name):
    return pl.pallas_call(
        functools.partial(_rs_wait_body),
        name=name,
        in_specs=[HBM, HBM, SEM, SEM, ANY],
        out_specs=[HBM, HBM],
        out_shape=[pltpu.HBM(pair.shape, pair.dtype), pltpu.HBM(land.shape, land.dtype)],
        input_output_aliases={0: 0, 1: 1},
        compiler_params=pltpu.CompilerParams(has_side_effects=EFFECT),
    )(pair, land, send_sems, recv_sems, after)


def _swap_copy(g4_ref, got_ref, send_sem, recv_sem):
    x, y, c, _ = _me()
    return pltpu.make_async_remote_copy(src_ref=g4_ref.at[:, 1 - c], dst_ref=got_ref, send_sem=send_sem,
                                        recv_sem=recv_sem, device_id=(x, y, 1 - c), device_id_type=MESH)


def _swap_start_body(g4_ref, got_ref, send_sem, recv_sem, g4_thru, got_thru, token_ref):
    del g4_thru, got_thru
    _swap_copy(g4_ref, got_ref, send_sem, recv_sem).start()
    token_ref[...] = jnp.zeros_like(token_ref)


def _swap_start(g4, name):
    n, _, h, C = g4.shape
    tok_shape, tok_spec = _token_spec()
    sem = pltpu.SemaphoreType.DMA(())
    got = pltpu.with_memory_space_constraint(lax.empty((n, h, C), g4.dtype), pltpu.HBM)
    return pl.pallas_call(
        functools.partial(_swap_start_body),
        name=name,
        in_specs=[HBM, HBM],
        out_specs=[SEM, SEM, HBM, HBM, tok_spec],
        out_shape=[sem, sem, pltpu.HBM(g4.shape, g4.dtype), pltpu.HBM(got.shape, got.dtype), tok_shape],
        input_output_aliases={0: 2, 1: 3},
        compiler_params=pltpu.CompilerParams(has_side_effects=EFFECT),
    )(pltpu.with_memory_space_constraint(g4, pltpu.HBM), got)


def _swap_wait_body(g4_ref, got_ref, send_sem, recv_sem, after_ref, g4_out, got_out):
    del after_ref, g4_out, got_out
    cp = _swap_copy(g4_ref, got_ref, send_sem, recv_sem)
    cp.wait_send()
    cp.wait_recv()


def _swap_wait(g4, got, send_sem, recv_sem, after, name):
    return pl.pallas_call(
        functools.partial(_swap_wait_body),
        name=name,
        in_specs=[HBM, HBM, SEM, SEM, ANY],
        out_specs=[HBM, HBM],
        out_shape=[pltpu.HBM(g4.shape, g4.dtype), pltpu.HBM(got.shape, got.dtype)],
        input_output_aliases={0: 0, 1: 1},
        compiler_params=pltpu.CompilerParams(has_side_effects=EFFECT),
    )(g4, got, send_sem, recv_sem, after)


def _add_chips_body(tr, n_i, pc_ref, own_ref, l0_ref, l1_ref, l2_ref, o_ref, got_ref, send_sems, recv_sem):
    del pc_ref
    i = pl.program_id(0)
    r = own_ref[...].astype(F32) + l0_ref[...].astype(F32)
    o_ref[...] = (r + l1_ref[...].astype(F32) + l2_ref[...].astype(F32)).astype(o_ref.dtype)
    r_tile = pl.multiple_of(i * tr, tr)
    _send_tile_to_sibling(lambda r0, n: o_ref.at[pl.ds(r0, n)], lambda r0, n: got_ref.at[pl.ds(r_tile + r0, n)], tr,
                          got_ref, send_sems, recv_sem, i == n_i - 1)


def _add_chips(pair, land, pc, name):
    _, h, C = pair.shape
    tr = _pick(h, 256, 16)
    slot = lambda j: pl.BlockSpec((None, tr, C), lambda i, pc: (j, i, 0))
    return pl.pallas_call(
        functools.partial(_add_chips_body, tr, h // tr),
        name=name,
        grid_spec=pltpu.PrefetchScalarGridSpec(
            num_scalar_prefetch=1,
            grid=(h // tr,),
            in_specs=[pl.BlockSpec((None, tr, C), lambda i, pc: (pc[0], i, 0)), slot(0), slot(1), slot(2)],
            out_specs=[pl.BlockSpec((tr, C), lambda i, pc: (i, 0)), HBM],
            scratch_shapes=TILE_SEMS,
        ),
        out_shape=[jax.ShapeDtypeStruct((h, C), pair.dtype), jax.ShapeDtypeStruct((h, C), pair.dtype)],
        compiler_params=_params(("arbitrary",)),
    )(pc, pair, land, land, land)


def _peer(r):
    x, y, c, _ = _me()
    return (x ^ ((r >> 2) & 1), y ^ ((r >> 1) & 1), c ^ (r & 1))


def _ar_start_body(x_ref, land_ref, send_sems, recv_sems, x_thru, land_thru, token_ref):
    del x_thru, land_thru
    for r in range(1, N_DEV):
        pltpu.make_async_remote_copy(src_ref=x_ref, dst_ref=land_ref.at[r - 1], send_sem=send_sems.at[r - 1],
                                     recv_sem=recv_sems.at[r - 1], device_id=_peer(r), device_id_type=MESH).start()
    token_ref[...] = jnp.zeros_like(token_ref)


def _ar_start(packed):
    tok_shape, tok_spec = _token_spec()
    sems = pltpu.SemaphoreType.DMA((N_DEV - 1,))
    land = pltpu.with_memory_space_constraint(lax.empty((N_DEV - 1,) + packed.shape, packed.dtype), pltpu.HBM)
    return pl.pallas_call(
        functools.partial(_ar_start_body),
        name="ar_start",
        in_specs=[HBM, HBM],
        out_specs=[SEM, SEM, HBM, HBM, tok_spec],
        out_shape=[sems, sems, pltpu.HBM(packed.shape, packed.dtype), pltpu.HBM(land.shape, land.dtype), tok_shape],
        input_output_aliases={0: 2, 1: 3},
        compiler_params=pltpu.CompilerParams(has_side_effects=EFFECT),
    )(pltpu.with_memory_space_constraint(packed, pltpu.HBM), land)


def _ar_wait_body(x_ref, land_ref, send_sems, recv_sems, after_ref, x_out, land_out):
    del after_ref, x_out, land_out
    for r in range(1, N_DEV):
        cp = pltpu.make_async_remote_copy(src_ref=x_ref, dst_ref=land_ref.at[r - 1], send_sem=send_sems.at[r - 1],
                                          recv_sem=recv_sems.at[r - 1], device_id=_peer(r), device_id_type=MESH)
        cp.wait_send()
        cp.wait_recv()


def _ar_wait(packed, land, send_sems, recv_sems, after):
    return pl.pallas_call(
        functools.partial(_ar_wait_body),
        name="ar_wait",
        in_specs=[HBM, HBM, SEM, SEM, ANY],
        out_specs=[HBM, HBM],
        out_shape=[pltpu.HBM(packed.shape, packed.dtype), pltpu.HBM(land.shape, land.dtype)],
        input_output_aliases={0: 0, 1: 1},
        compiler_params=pltpu.CompilerParams(has_side_effects=EFFECT),
    )(packed, land, send_sems, recv_sems, after)


def _ar_sum_body(me_ref, own_ref, *rest):
    o_ref = rest[N_DEV]
    acc = None
    for dev in range(N_DEV):
        term = jnp.where(me_ref[0] == dev, own_ref[...], rest[dev][...])
        acc = term if acc is None else acc + term
    o_ref[...] = acc


def _ar_sum(packed, land, me):
    R, C = packed.shape
    tr = _pick(R, 552, 8)
    own = pl.BlockSpec((tr, C), lambda i, me: (i, 0))
    slot = lambda dev: pl.BlockSpec((None, tr, C), lambda i, me: (jnp.maximum((dev ^ me[0]) - 1, 0), i, 0))
    return pl.pallas_call(
        functools.partial(_ar_sum_body),
        name="ar_sum",
        grid_spec=pltpu.PrefetchScalarGridSpec(
            num_scalar_prefetch=1,
            grid=(R // tr,),
            in_specs=[own] + [slot(dev) for dev in range(N_DEV)],
            out_specs=pl.BlockSpec((tr, C), lambda i, me: (i, 0)),
        ),
        out_shape=jax.ShapeDtypeStruct((R, C), F32),
        compiler_params=_params(("parallel",)),
    )(me, packed, *([land] * N_DEV))


def _pack(arrays):
    rows = []
    for a in arrays:
        flat = a.reshape(-1).astype(F32)
        pad = (-flat.shape[0]) % BLK
        rows.append(jnp.pad(flat, (0, pad)).reshape(-1, BLK))
    packed = jnp.concatenate(rows, axis=0)
    pad = (-packed.shape[0]) % 8
    return jnp.pad(packed, ((0, pad), (0, 0)))


def _unpack(packed, shapes):
    out, r = [], 0
    for s in shapes:
        n = 1
        for k in s:
            n *= k
        nr = -(-n // BLK)
        out.append(packed[r:r + nr].reshape(-1)[:n].reshape(s))
        r += nr
    return out


def kernel(x, norm1_g, w_in, q_norm_g, k_norm_g, attn_sinks, gate_ln_g, gate_ln_b, w_spatial, b_spatial, out_norm_attn_g, out_norm_gate_g, w_out, norm2_g, w_ffn_gate, w_ffn_up, w_ffn_down, loss_target, m_norm1_g, m_w_in, m_q_norm_g, m_k_norm_g, m_attn_sinks, m_gate_ln_g, m_gate_ln_b, m_w_spatial, m_b_spatial, m_out_norm_attn_g, m_out_norm_gate_g, m_w_out, m_norm2_g, m_w_ffn_gate, m_w_ffn_up, m_w_ffn_down, v_norm1_g, v_w_in, v_q_norm_g, v_k_norm_g, v_attn_sinks, v_gate_ln_g, v_gate_ln_b, v_w_spatial, v_b_spatial, v_out_norm_attn_g, v_out_norm_gate_g, v_w_out, v_norm2_g, v_w_ffn_gate, v_w_ffn_up, v_w_ffn_down):
    bl, seq, D = x.shape
    T = bl * seq
    attn_w, gate_w = out_norm_attn_g.shape[1], out_norm_gate_g.shape[1]
    d = _Dims(seq, attn_w, gate_w)
    G = d.n_groups
    in_w = d.in_w
    dff = w_ffn_gate.shape[2] * N_CHIPS
    assert w_in.shape[2] * N_CHIPS == in_w and seq % BLK == 0 and attn_w % (2 * BLK) == 0

    pc = jnp.stack([2 * lax.axis_index("x") + lax.axis_index("y"), lax.axis_index("c")]).astype(jnp.int32)
    big = [w_in[0], w_out[0], w_ffn_gate[0], w_ffn_up[0], w_ffn_down[0]]
    names = ["in", "out", "gate", "up", "down"]
    xf = x.reshape(T, D)
    tgt = loss_target.reshape(T, D)
    send, recv, buf, behind = _ag_start(_cast_into(big[0], pc, "cast_in"), norm1_g, "ag_start_in")
    started = [(send, recv, buf)]
    h1 = _rms_fwd(xf, norm1_g, "norm1_fwd", after=behind)
    behind = h1
    for w, n in zip(big[1:], names[1:]):
        buf = _cast_into(w, pc, "cast_" + n, side_by_side=n in ("gate", "up"))
        send, recv, buf, behind = _ag_start(buf, behind, "ag_start_" + n, both_cores=n == "down")
        started.append((send, recv, buf))

    def gathered(k, after):
        send, recv, buf = started[k]
        direct = names[k] == "down"
        buf = _ag_wait(buf, send, recv, after, "ag_wait_" + names[k], both_cores=direct)
        if not direct:
            buf = _ag_pair(buf, pc, "ag_pair_" + names[k])
        rs, cs = big[k].shape
        return buf.reshape(rs, N_CHIPS * cs) if len(buf.shape) == 3 else buf.reshape(N_CHIPS, rs, cs)

    qg2 = jnp.tile(q_norm_g, (1, 2))
    kg2 = jnp.tile(k_norm_g, (1, 2))
    lg, lb, wsp = gate_ln_g[0], gate_ln_b[0], w_spatial[0]
    bcol = jnp.broadcast_to(b_spatial[0][:, :, None], (G, BLK, BLK))

    win_full = jnp.transpose(gathered(0, behind), (1, 0, 2)).reshape(D, in_w)
    proj = _matmul(h1, win_full, "nn", F32, "proj_fwd", tm=512, tn=3328)
    ya, yg, yn = _mixer_fwd(d, proj, attn_sinks, qg2, kg2, lg, lb, wsp, bcol, out_norm_attn_g, out_norm_gate_g)
    wout_full = gathered(1, yn).reshape(attn_w + gate_w, D)
    x1 = _matmul(yn, wout_full, "nn", F32, "out_fwd", tm=1024, tn=1024, add=xf)
    h2 = _rms_fwd(x1, norm2_g, "norm2_fwd")
    wg_full, wu_full = gathered(2, h2), gathered(3, h2)
    a, b, f = _ffn_up(h2, wg_full, wu_full)
    wd_full = gathered(4, f).reshape(dff, D)
    dx2, dx2b, loss_local = _ffn_down_loss(f, wd_full, x1, tgt)

    def swap_start(g, n):
        g4 = g.reshape(N_CHIPS, 2, g.shape[1] // 2, g.shape[2])
        return _swap_start(g4, "rs_swap_start_" + n)

    def reduce_start(swapping, n, after):
        send, recv, g4, got, _ = swapping
        g4, got = _swap_wait(g4, got, send, recv, after, "rs_swap_wait_" + n)
        return _rs_start(_add_pair(g4, got, pc, "rs_add_pair_" + n), got, "rs_start_" + n)

    reducing = {}
    g_d = _matmul(f, dx2b, "tn", BF16, "ffn_down_dw", tm=1408, tn=1024, tk=2048, out_slab="r")
    swap_d = swap_start(g_d, "down")
    da, db = _ffn_down_dx(dx2b, wd_full, a, b, swap_d[4])
    g_g = _matmul(h2, da, "tn", BF16, "ffn_gate_dw", tm=1024, tn=1408, tk=2048, out_slab="c")
    swap_g = swap_start(g_g, "gate")
    reducing["down"] = reduce_start(swap_d, "down", swap_g[4])
    g_u = _matmul(h2, db, "tn", BF16, "ffn_up_dw", tm=1024, tn=1408, tk=2048, out_slab="c",
                  after=reducing["down"][4])
    swap_u = swap_start(g_u, "up")
    reducing["gate"] = reduce_start(swap_g, "gate", swap_u[4])
    dh2 = _matmul(da, wg_full, "nt", F32, "ffn_gate_dx", tm=1024, tn=1024, tk=2816, after=reducing["gate"][4])
    dh2 = _matmul(db, wu_full, "nt", F32, "ffn_up_dx", tm=1024, tn=1024, tk=2816, add=dh2)
    reducing["up"] = reduce_start(swap_u, "up", dh2)
    dx1, dx1b, dg_norm2 = _rms_bwd(x1, norm2_g, dh2, dx2, "norm2_bwd", True)
    g_o = _matmul(yn, dx1b, "tn", BF16, "out_dw", tm=512, tn=1024, tk=2048, out_slab="r",
                  after=reducing["up"][4])
    swap_o = swap_start(g_o, "out")
    dy = _matmul(dx1b, wout_full, "nt", F32, "out_dx", tm=1024, tn=1024, after=swap_o[4])
    (dproj, dkv, dqg, dkg, dsk, dlg, dlb, dwsp, dbsp, dgoa, dgog) = _mixer_bwd(
        d, proj, ya, yg, dy, attn_sinks, qg2, kg2, lg, lb, wsp, bcol, out_norm_attn_g, out_norm_gate_g)
    dproj = _put_kv(d, dproj, dkv)
    reducing["out"] = reduce_start(swap_o, "out", dproj)
    g_in_full = _matmul(h1, dproj, "tn", BF16, "proj_dw", tm=512, tn=3328, tk=2048,
                        after=reducing["out"][4])
    g_i = jnp.transpose(g_in_full.reshape(D, N_CHIPS, in_w // N_CHIPS), (1, 0, 2))
    g4_i = g_i.reshape(N_CHIPS, 2, D // 2, in_w // N_CHIPS)
    pair_i = _add_pair(g4_i, _swap_halves(g4_i, pc, "rs_swap_in"), pc, "rs_add_pair_in")
    reducing["in"] = _rs_start(pair_i, g_i, "rs_start_in")
    dh1 = _matmul(dproj, win_full, "nt", F32, "proj_dx", tm=1024, tn=1024, after=reducing["in"][4])
    dx, dg_norm1 = _rms_bwd(xf, norm1_g, dh1, dx1, "norm1_bwd", False)

    dqg64 = dqg[:, :HEAD_DIM] + dqg[:, HEAD_DIM:]
    dkg64 = dkg[:, :HEAD_DIM] + dkg[:, HEAD_DIM:]
    small_g_local = [dg_norm1, dqg64, dkg64, dsk[:, :d.n_heads], dlg, dlb, dwsp, dbsp, dgoa, dgog, dg_norm2,
                     loss_local]
    ar_send, ar_recv, ar_own, ar_land, ar_token = _ar_start(_pack(small_g_local))

    big_m = [m_w_in[0], m_w_out[0], m_w_ffn_gate[0], m_w_ffn_up[0], m_w_ffn_down[0]]
    big_v = [v_w_in[0], v_w_out[0], v_w_ffn_gate[0], v_w_ffn_up[0], v_w_ffn_down[0]]
    big_grads, big_d, big_nm, big_nv = [], [], [], []
    for w, m, v, n in zip(big, big_m, big_v, names):
        send, recv, pair, land, _ = reducing[n]
        pair, land = _rs_wait(pair, land, send, recv, ar_token, "rs_wait_" + n)
        own, got = _add_chips(pair, land, pc, "rs_add_chips_" + n)
        outs = _adamw_halves(w, own, got, m, v, pc, "adamw_" + n)
        for lst, o in zip((big_grads, big_d, big_nm, big_nv), outs):
            lst.append(o.reshape(w.shape))

    small_names_w = [norm1_g, q_norm_g, k_norm_g, attn_sinks, gate_ln_g, gate_ln_b, w_spatial, b_spatial,
                     out_norm_attn_g, out_norm_gate_g, norm2_g]
    small_m = [m_norm1_g, m_q_norm_g, m_k_norm_g, m_attn_sinks, m_gate_ln_g, m_gate_ln_b, m_w_spatial, m_b_spatial,
               m_out_norm_attn_g, m_out_norm_gate_g, m_norm2_g]
    small_v = [v_norm1_g, v_q_norm_g, v_k_norm_g, v_attn_sinks, v_gate_ln_g, v_gate_ln_b, v_w_spatial, v_b_spatial,
               v_out_norm_attn_g, v_out_norm_gate_g, v_norm2_g]
    shapes = [w.shape for w in small_names_w] + [loss_local.shape]
    ride = [jnp.zeros(loss_local.shape, F32)]
    ar_own, ar_land = _ar_wait(ar_own, ar_land, ar_send, ar_recv, big_nv[-1])
    me = (4 * lax.axis_index("x") + 2 * lax.axis_index("y") + lax.axis_index("c")).astype(jnp.int32).reshape(1)
    sg = _ar_sum(ar_own, ar_land, me)
    sd, snm, snv = _adamw(_pack(small_names_w + ride), sg, _pack(small_m + ride), _pack(small_v + ride), "adamw_small")
    small_g, small_d, small_nm, small_nv = (_unpack(t, shapes) for t in (sg, sd, snm, snv))
    loss = small_g[-1][0, 0]

    def order(small, bigs):
        s = list(small)
        bg = [t[None] for t in bigs]
        return [s[0], bg[0], s[1], s[2], s[3], s[4], s[5], s[6], s[7], s[8], s[9], bg[1], s[10], bg[2], bg[3], bg[4]]

    grad_x = dx.reshape(bl, seq, D)
    return (loss, grad_x, *order(small_g, big_grads), *order(small_d, big_d), *order(small_nm, big_nm),
            *order(small_nv, big_nv))
```

```python
import functools

import jax
import jax.numpy as jnp
from jax import lax
from jax.experimental import pallas as pl
from jax.experimental.pallas import tpu as pltpu

F32 = jnp.float32
BF16 = jnp.bfloat16
MESH = pl.DeviceIdType.MESH

EPS = 1e-6
HEAD_DIM = 64
N_KV_HEADS = 2
BLK = 128
N_CHIPS = 4
N_DEV = 8
NEG = -1e30

ADAM_LR = 0.001
ADAM_B1 = 0.9
ADAM_B2 = 0.999
ADAM_EPS = 1e-08
ADAM_WD = 0.01
ADAM_STEP = 10

VMEM_LIMIT = 56 * 1024 * 1024

NN = (((1,), (0,)), ((), ()))
NT = (((1,), (1,)), ((), ()))
TN = (((0,), (0,)), ((), ()))
HBM = pl.BlockSpec(memory_space=pltpu.HBM)
ANY = pl.BlockSpec(memory_space=pl.ANY)
SEM = pl.BlockSpec(memory_space=pltpu.SEMAPHORE)
EFFECT = pltpu.SideEffectType.DATAFLOW_SIDE_EFFECTING


def _dot(a, b, dn):
    return lax.dot_general(a, b, dn, preferred_element_type=F32)


def _pick(dim, pref, align=128):
    if dim <= pref:
        return dim
    t = (pref // align) * align
    while t >= align:
        if dim % t == 0:
            return t
        t -= align
    return dim


def _params(sem):
    return pltpu.CompilerParams(dimension_semantics=sem, vmem_limit_bytes=VMEM_LIMIT)


MM_CHUNK = 512


def _col_chunks(tn):
    return [slice(c0, min(c0 + MM_CHUNK, tn)) for c0 in range(0, tn, MM_CHUNK)]


def _mm_body(dn, nk, has_add, has_after, *refs):
    a_ref, b_ref = refs[:2]
    add_ref = refs[2] if has_add else None
    o_ref = refs[2 + has_add + has_after]
    chunks = _col_chunks(o_ref.shape[-1])

    def dot(cols):
        return _dot(a_ref[...], b_ref[cols, :] if dn == NT else b_ref[:, cols], dn)

    def finish(cols, r):
        if add_ref is not None:
            r = r + add_ref[:, cols]
        o_ref[:, cols] = r.astype(o_ref.dtype)

    if nk == 1:
        for cols in chunks:
            finish(cols, dot(cols))
        return
    acc_ref = refs[-1]
    k = pl.program_id(2)

    @pl.when(k == 0)
    def _():
        for cols in chunks:
            acc_ref[:, cols] = dot(cols)

    if nk > 2:
        @pl.when(jnp.logical_and(k > 0, k < nk - 1))
        def _():
            for cols in chunks:
                acc_ref[:, cols] += dot(cols)

    @pl.when(k == nk - 1)
    def _():
        for cols in chunks:
            finish(cols, acc_ref[:, cols] + dot(cols))


def _matmul(a, b, mode, out_dtype, name, *, tm, tn, tk=None, add=None, out_slab=None, after=None):
    if mode == "nn":
        (M, K), N = a.shape, b.shape[1]
    elif mode == "nt":
        (M, K), N = a.shape, b.shape[0]
    else:
        (K, M), N = a.shape, b.shape[1]
    tk = K if tk is None else tk
    tm, tn, tk = _pick(M, tm), _pick(N, tn), _pick(K, tk)
    if out_slab == "c":
        tn = _pick(N // N_CHIPS, tn)
    if out_slab == "r":
        tm = _pick(M // N_CHIPS, tm)
    gm, gn, gk = M // tm, N // tn, K // tk

    if mode == "tn":
        a_spec = pl.BlockSpec((tk, tm), lambda j, i, k: (k, i))
        b_spec = pl.BlockSpec((tk, tn), lambda j, i, k: (k, j))
    else:
        a_spec = pl.BlockSpec((tm, tk), lambda j, i, k: (i, k))
        if mode == "nn":
            b_spec = pl.BlockSpec((tk, tn), lambda j, i, k: (k, j))
        else:
            b_spec = pl.BlockSpec((tn, tk), lambda j, i, k: (j, k))

    if out_slab == "c":
        per = (N // N_CHIPS) // tn
        o_spec = pl.BlockSpec((None, tm, tn), lambda j, i, k: (j // per, i, j % per))
        o_shape = jax.ShapeDtypeStruct((N_CHIPS, M, N // N_CHIPS), out_dtype)
    elif out_slab == "r":
        per = (M // N_CHIPS) // tm
        o_spec = pl.BlockSpec((None, tm, tn), lambda j, i, k: (i // per, i % per, j))
        o_shape = jax.ShapeDtypeStruct((N_CHIPS, M // N_CHIPS, N), out_dtype)
    else:
        o_spec = pl.BlockSpec((tm, tn), lambda j, i, k: (i, j))
        o_shape = jax.ShapeDtypeStruct((M, N), out_dtype)

    dn = {"nn": NN, "nt": NT, "tn": TN}[mode]
    in_specs = [a_spec, b_spec]
    args = [a, b]
    if add is not None:
        in_specs.append(pl.BlockSpec((tm, tn), lambda j, i, k: (i, j)))
        args.append(add)
    if after is not None:
        in_specs.append(ANY)
        args.append(after)
    return pl.pallas_call(
        functools.partial(_mm_body, dn, gk, add is not None, after is not None),
        name=name,
        grid=(gn, gm, gk),
        in_specs=in_specs,
        out_specs=o_spec,
        out_shape=o_shape,
        scratch_shapes=[pltpu.VMEM((tm, tn), F32)] if gk > 1 else [],
        compiler_params=_params(("parallel", "parallel", "arbitrary")),
    )(*args)


def _rms_fwd_body(x_ref, g_ref, *rest):
    h_ref = rest[-1]
    x = x_ref[...]
    r = lax.rsqrt(jnp.mean(x * x, axis=-1, keepdims=True) + EPS)
    h_ref[...] = (x * r * g_ref[...]).astype(h_ref.dtype)


def _rms_fwd(x, g, name, after=None):
    T, D = x.shape
    tr = _pick(T, 512, 16)
    extra = [] if after is None else [after]
    return pl.pallas_call(
        functools.partial(_rms_fwd_body),
        name=name,
        grid=(T // tr,),
        in_specs=[pl.BlockSpec((tr, D), lambda i: (i, 0)), pl.BlockSpec((1, D), lambda i: (0, 0))] + [ANY] * len(extra),
        out_specs=pl.BlockSpec((tr, D), lambda i: (i, 0)),
        out_shape=jax.ShapeDtypeStruct((T, D), BF16),
        compiler_params=_params(("parallel",)),
    )(x, g, *extra)


def _rms_bwd_body(with_bf16, x_ref, g_ref, dh_ref, res_ref, dx_ref, *rest):
    dg_ref = rest[-1]

    @pl.when(pl.program_id(0) == 0)
    def _():
        dg_ref[...] = jnp.zeros_like(dg_ref)

    x = x_ref[...]
    r = lax.rsqrt(jnp.mean(x * x, axis=-1, keepdims=True) + EPS)
    xh = x * r
    dh = dh_ref[...]
    dg_ref[...] += jnp.sum(dh * xh, axis=0, keepdims=True)
    t = dh * g_ref[...]
    dx = res_ref[...] + r * (t - xh * jnp.mean(t * xh, axis=-1, keepdims=True))
    dx_ref[...] = dx
    if with_bf16:
        rest[0][...] = dx.astype(BF16)


def _rms_bwd(x, g, dh, res, name, with_bf16):
    T, D = x.shape
    tr = _pick(T, 512, 16)
    row = pl.BlockSpec((tr, D), lambda i: (i, 0))
    vec = pl.BlockSpec((1, D), lambda i: (0, 0))
    extra = [jax.ShapeDtypeStruct((T, D), BF16)] if with_bf16 else []
    return pl.pallas_call(
        functools.partial(_rms_bwd_body, with_bf16),
        name=name,
        grid=(T // tr,),
        in_specs=[row, vec, row, row],
        out_specs=[row] + [row] * len(extra) + [vec],
        out_shape=[jax.ShapeDtypeStruct((T, D), F32)] + extra + [jax.ShapeDtypeStruct((1, D), F32)],
        compiler_params=_params(("arbitrary",)),
    )(x, g, dh, res)


def _ffn_up_body(h_ref, wg_ref, wu_ref, a_ref, b_ref, f_ref):
    for cols in _col_chunks(a_ref.shape[-1]):
        a = _dot(h_ref[...], wg_ref[:, cols], NN)
        b = _dot(h_ref[...], wu_ref[:, cols], NN)
        a_ref[:, cols] = a
        b_ref[:, cols] = b
        f_ref[:, cols] = (a * (1.0 / (1.0 + jnp.exp(-a))) * b).astype(f_ref.dtype)


def _ffn_up(h, wg, wu):
    T, D = h.shape
    F = wg.shape[1]
    tm, tn = _pick(T, 1024), _pick(F, MM_CHUNK)
    hs = pl.BlockSpec((tm, D), lambda j, i: (i, 0))
    ws = pl.BlockSpec((D, tn), lambda j, i: (0, j))
    os = pl.BlockSpec((tm, tn), lambda j, i: (i, j))
    return pl.pallas_call(
        functools.partial(_ffn_up_body),
        name="ffn_up_fwd",
        grid=(F // tn, T // tm),
        in_specs=[hs, ws, ws],
        out_specs=[os, os, os],
        out_shape=[jax.ShapeDtypeStruct((T, F), F32), jax.ShapeDtypeStruct((T, F), F32),
                   jax.ShapeDtypeStruct((T, F), BF16)],
        compiler_params=_params(("parallel", "parallel")),
    )(h, wg, wu)


def _ffn_down_dx_body(dx_ref, wd_ref, a_ref, b_ref, after_ref, da_ref, db_ref):
    del after_ref
    for cols in _col_chunks(da_ref.shape[-1]):
        df = _dot(dx_ref[...], wd_ref[cols, :], NT)
        a = a_ref[:, cols]
        s = 1.0 / (1.0 + jnp.exp(-a))
        da_ref[:, cols] = (df * b_ref[:, cols] * (s * (1.0 + a * (1.0 - s)))).astype(da_ref.dtype)
        db_ref[:, cols] = (df * (a * s)).astype(db_ref.dtype)


def _ffn_down_dx(dx2b, wd, a, b, after):
    T, D = dx2b.shape
    F = wd.shape[0]
    tm, tn = _pick(T, 512), _pick(F, 1408)
    xs = pl.BlockSpec((tm, D), lambda j, i: (i, 0))
    ws = pl.BlockSpec((tn, D), lambda j, i: (j, 0))
    os = pl.BlockSpec((tm, tn), lambda j, i: (i, j))
    return pl.pallas_call(
        functools.partial(_ffn_down_dx_body),
        name="ffn_down_dx",
        grid=(F // tn, T // tm),
        in_specs=[xs, ws, os, os, ANY],
        out_specs=[os, os],
        out_shape=[jax.ShapeDtypeStruct((T, F), BF16), jax.ShapeDtypeStruct((T, F), BF16)],
        compiler_params=_params(("parallel", "parallel")),
    )(dx2b, wd, a, b, after)


def _ffn_down_loss_body(nk, inv_d, f_ref, wd_ref, x1_ref, tgt_ref, dx2_ref, dx2b_ref, loss_ref, *scratch):
    j, i, k = pl.program_id(0), pl.program_id(1), pl.program_id(2)
    chunks = _col_chunks(dx2_ref.shape[-1])

    def dot(cols):
        return _dot(f_ref[...], wd_ref[:, cols], NN)

    @pl.when(jnp.logical_and(jnp.logical_and(j == 0, i == 0), k == 0))
    def _():
        loss_ref[...] = jnp.zeros_like(loss_ref)

    def finish(ffn_of):
        total = jnp.zeros((1, 1), F32)
        for cols in chunks:
            e = ffn_of(cols) + x1_ref[:, cols] - tgt_ref[:, cols]
            dx2 = e * inv_d
            dx2_ref[:, cols] = dx2
            dx2b_ref[:, cols] = dx2.astype(BF16)
            total = total + jnp.sum(jnp.sum(e * e, axis=-1, keepdims=True), axis=0, keepdims=True)
        loss_ref[...] += (0.5 * inv_d) * total

    if nk == 1:
        finish(dot)
        return
    acc_ref = scratch[0]

    @pl.when(k == 0)
    def _():
        for cols in chunks:
            acc_ref[:, cols] = dot(cols)

    if nk > 2:
        @pl.when(jnp.logical_and(k > 0, k < nk - 1))
        def _():
            for cols in chunks:
                acc_ref[:, cols] += dot(cols)

    @pl.when(k == nk - 1)
    def _():
        finish(lambda cols: acc_ref[:, cols] + dot(cols))


def _ffn_down_loss(f, wd, x1, tgt):
    T, F = f.shape
    D = wd.shape[1]
    tm, tn, tk = _pick(T, 1024), _pick(D, 1024), _pick(F, 2816)
    gm, gn, gk = T // tm, D // tn, F // tk
    tile = pl.BlockSpec((tm, tn), lambda j, i, k: (i, j))
    return pl.pallas_call(
        functools.partial(_ffn_down_loss_body, gk, 1.0 / D),
        name="ffn_down_loss",
        grid=(gn, gm, gk),
        in_specs=[pl.BlockSpec((tm, tk), lambda j, i, k: (i, k)), pl.BlockSpec((tk, tn), lambda j, i, k: (k, j)),
                  tile, tile],
        out_specs=[tile, tile, pl.BlockSpec((1, 1), lambda j, i, k: (0, 0))],
        out_shape=[jax.ShapeDtypeStruct((T, D), F32), jax.ShapeDtypeStruct((T, D), BF16),
                   jax.ShapeDtypeStruct((1, 1), F32)],
        scratch_shapes=[pltpu.VMEM((tm, tn), F32)] if gk > 1 else [],
        compiler_params=_params(("arbitrary", "arbitrary", "arbitrary")),
    )(f, wd, x1, tgt)


def _lo_mask(shape):
    return lax.broadcasted_iota(jnp.int32, shape, len(shape) - 1) < HEAD_DIM


def _half_sums(t, lo):
    s_lo = jnp.sum(jnp.where(lo, t, 0.0), axis=-1, keepdims=True)
    s_hi = jnp.sum(jnp.where(lo, 0.0, t), axis=-1, keepdims=True)
    return jnp.where(lo, s_lo, s_hi)


def _head_rstd(t, lo):
    return lax.rsqrt(_half_sums(t * t, lo) * (1.0 / HEAD_DIM) + EPS)


def _place(t, lo, kv_head):
    if kv_head == 0:
        t_lo = jnp.where(lo, t, 0.0)
        t_hi = pltpu.roll(t_lo, HEAD_DIM, 1)
    else:
        t_hi = jnp.where(lo, 0.0, t)
        t_lo = pltpu.roll(t_hi, HEAD_DIM, 1)
    return jnp.concatenate([t_lo, t_hi], axis=0).astype(BF16)


def _unplace(c0, c1, lo):
    return jnp.where(lo, c0 + pltpu.roll(c0, HEAD_DIM, 1), c1 + pltpu.roll(c1, HEAD_DIM, 1))


def _band(kv_cur, kv_prev, kg, lo2):
    kb = jnp.concatenate([kv_prev[:, :BLK], kv_cur[:, :BLK]], axis=0)
    vb = jnp.concatenate([kv_prev[:, BLK:], kv_cur[:, BLK:]], axis=0)
    rk = _head_rstd(kb, lo2)
    kn = kb * rk * kg
    kk = [_place(kn, lo2, h) for h in range(N_KV_HEADS)]
    vv = [_place(vb, lo2, h) for h in range(N_KV_HEADS)]
    return kb, rk, kk, vv


def _score_geometry(first_i32):
    qi = lax.broadcasted_iota(jnp.int32, (BLK, 4 * BLK), 0)
    col = lax.broadcasted_iota(jnp.int32, (BLK, 4 * BLK), 1)
    kj = col & (2 * BLK - 1)
    dist = qi + BLK - kj
    valid = (dist >= 0) & (dist < BLK) & (kj >= first_i32 * BLK)
    return col, dist.astype(F32), valid


def _pair_logits(qn, kk, col, distf, valid, slope0, slope1):
    s = _dot(qn.astype(BF16), kk, NT) * (HEAD_DIM ** -0.5)
    slope = jnp.where(col < 2 * BLK, slope0, slope1)
    return jnp.where(valid, s - slope * distf, NEG)


def _pair_probs(qn, kk, col, distf, valid, slope0, slope1, sink0, sink1):
    return _softmax_halves(_pair_logits(qn, kk, col, distf, valid, slope0, slope1), sink0, sink1)


def _softmax_halves(logits, sink0, sink1):
    probs, psink = [], []
    for hh, sk in ((0, sink0), (1, sink1)):
        l = logits[:, 2 * BLK * hh:2 * BLK * (hh + 1)]
        m = jnp.maximum(jnp.max(l, axis=-1, keepdims=True), sk)
        p = jnp.exp(l - m)
        es = jnp.exp(sk - m)
        inv = 1.0 / (jnp.sum(p, axis=-1, keepdims=True) + es)
        probs.append(p * inv)
        psink.append(es * inv)
    return probs, psink


def _gelu(z, with_grad=False):
    cdf = 0.5 * (1.0 + lax.erf(z * (0.5 ** 0.5)))
    if not with_grad:
        return z * cdf
    return z * cdf, cdf + z * jnp.exp(-0.5 * z * z) * ((2.0 * jnp.pi) ** -0.5)


def _tril_w(w):
    r = lax.broadcasted_iota(jnp.int32, (BLK, BLK), 0)
    c = lax.broadcasted_iota(jnp.int32, (BLK, BLK), 1)
    return jnp.where(r >= c, w, 0.0), r >= c


def _gate_fwd_group(zu, zv, lg, lb, w, bcol, with_grad=False):
    u, v = _gelu(zu, with_grad), _gelu(zv, with_grad)
    if with_grad:
        (u, du_dz), (v, dv_dz) = u, v
    mu = jnp.mean(v, axis=-1, keepdims=True)
    vc = v - mu
    rs = lax.rsqrt(jnp.mean(vc * vc, axis=-1, keepdims=True) + EPS)
    vh = vc * rs
    vn = vh * lg + lb
    wt, tril = _tril_w(w)
    mixed = _dot(wt.astype(BF16), vn.astype(BF16), NN) + bcol
    if with_grad:
        return u, vh, rs, vn, wt, tril, mixed, du_dz, dv_dz
    return u, vh, rs, vn, wt, tril, mixed


class _Dims:
    def __init__(self, seq, attn_w, gate_w):
        self.seq, self.attn_w, self.gate_w = seq, attn_w, gate_w
        self.n_heads = attn_w // HEAD_DIM
        self.group = self.n_heads // N_KV_HEADS
        self.n_pairs = attn_w // BLK
        self.n_groups = gate_w // BLK
        self.kv_col = attn_w // (2 * BLK)
        self.u0 = attn_w + 2 * BLK
        self.v0 = self.u0 + gate_w
        self.in_w = self.v0 + gate_w
        self.slopes = [2.0 ** (-8.0 * (h + 1) / self.n_heads) for h in range(self.n_heads)]


def _mixer_fwd_body(d, sink_ref, proj_ref, kvp_ref, qg_ref, kg_ref, lg_ref, lb_ref, w_ref, b_ref, goa_ref, gog_ref,
                    ya_ref, yg_ref, y_ref, logit_scr, prob_scr):
    i = pl.program_id(0)
    first = (i % (d.seq // BLK) == 0).astype(jnp.int32)
    lo = _lo_mask((BLK, BLK))
    lo2 = _lo_mask((2 * BLK, BLK))
    kv_cur = proj_ref[:, d.attn_w:d.attn_w + 2 * BLK]
    _, _, kk, vv = _band(kv_cur, kvp_ref[...], kg_ref[...], lo2)
    col, distf, valid = _score_geometry(first)
    qg = qg_ref[...]
    for j in range(d.n_pairs):
        h0, h1 = 2 * j, 2 * j + 1
        q2 = proj_ref[:, BLK * j:BLK * (j + 1)]
        qn = q2 * _head_rstd(q2, lo) * qg
        logit_scr[j] = _pair_logits(qn, kk[h0 // d.group], col, distf, valid, d.slopes[h0], d.slopes[h1])
    for j in range(d.n_pairs):
        probs, _ = _softmax_halves(logit_scr[j], sink_ref[0, 2 * j], sink_ref[0, 2 * j + 1])
        prob_scr[j] = jnp.concatenate(probs, axis=1).astype(BF16)
    for j in range(d.n_pairs):
        ya_ref[:, BLK * j:BLK * (j + 1)] = _dot(prob_scr[j], vv[2 * j // d.group], NN)
    for g in range(d.n_groups):
        zu = proj_ref[:, d.u0 + BLK * g:d.u0 + BLK * (g + 1)]
        zv = proj_ref[:, d.v0 + BLK * g:d.v0 + BLK * (g + 1)]
        u, _, _, _, _, _, mixed = _gate_fwd_group(zu, zv, lg_ref[g:g + 1, :], lb_ref[g:g + 1, :], w_ref[g], b_ref[g])
        yg_ref[:, BLK * g:BLK * (g + 1)] = u * mixed
    ya = ya_ref[...]
    ra = lax.rsqrt(jnp.mean(ya * ya, axis=-1, keepdims=True) + EPS)
    y_ref[:, :d.attn_w] = (ya * ra * goa_ref[...]).astype(y_ref.dtype)
    yg = yg_ref[...]
    rg = lax.rsqrt(jnp.mean(yg * yg, axis=-1, keepdims=True) + EPS)
    y_ref[:, d.attn_w:] = (yg * rg * gog_ref[...]).astype(y_ref.dtype)


def _mixer_specs(d, T):
    row = lambda w: pl.BlockSpec((BLK, w), lambda i: (i, 0))
    const2 = lambda a: pl.BlockSpec(a.shape, lambda i: (0, 0))
    const3 = lambda a: pl.BlockSpec(a.shape, lambda i: (0, 0, 0))
    kv_prev = pl.BlockSpec((BLK, 2 * BLK), lambda i: (jnp.maximum(i - 1, 0), d.kv_col))
    return row, const2, const3, kv_prev


def _mixer_fwd(d, proj, sinks, qg2, kg2, lg, lb, wsp, bcol, goa, gog):
    T = proj.shape[0]
    row, const2, const3, kv_prev = _mixer_specs(d, T)
    return pl.pallas_call(
        functools.partial(_mixer_fwd_body, d),
        name="mixer_fwd",
        grid=(T // BLK,),
        in_specs=[pl.BlockSpec(memory_space=pltpu.SMEM), row(d.in_w), kv_prev, const2(qg2), const2(kg2),
                  const2(lg), const2(lb), const3(wsp), const3(bcol), const2(goa), const2(gog)],
        out_specs=[row(d.attn_w), row(d.gate_w), row(d.attn_w + d.gate_w)],
        out_shape=[jax.ShapeDtypeStruct((T, d.attn_w), F32), jax.ShapeDtypeStruct((T, d.gate_w), F32),
                   jax.ShapeDtypeStruct((T, d.attn_w + d.gate_w), BF16)],
        scratch_shapes=[pltpu.VMEM((d.n_pairs, BLK, 4 * BLK), F32), pltpu.VMEM((d.n_pairs, BLK, 4 * BLK), BF16)],
        compiler_params=_params(("parallel",)),
    )(sinks, proj, proj, qg2, kg2, lg, lb, wsp, bcol, goa, gog)


def _mixer_bwd_body(d, sink_ref, proj_ref, kvp_ref, ya_ref, yg_ref, dy_ref, qg_ref, kg_ref, lg_ref, lb_ref, w_ref,
                    b_ref, goa_ref, gog_ref,
                    dproj_ref, dkv_ref, dqg_ref, dkg_ref, dsk_ref, dlg_ref, dlb_ref, dw_ref, db_ref, dgoa_ref,
                    dgog_ref):
    i = pl.program_id(0)

    @pl.when(i == 0)
    def _():
        for r in (dqg_ref, dkg_ref, dsk_ref, dlg_ref, dlb_ref, dw_ref, db_ref, dgoa_ref, dgog_ref):
            r[...] = jnp.zeros_like(r)

    first = (i % (d.seq // BLK) == 0).astype(jnp.int32)
    lo = _lo_mask((BLK, BLK))
    lo2 = _lo_mask((2 * BLK, BLK))
    lane_row = lax.broadcasted_iota(jnp.int32, (1, BLK), 1)

    ya = ya_ref[...]
    ra = lax.rsqrt(jnp.mean(ya * ya, axis=-1, keepdims=True) + EPS)
    yah = ya * ra
    dyn = dy_ref[:, :d.attn_w]
    dgoa_ref[...] += jnp.sum(dyn * yah, axis=0, keepdims=True)
    t = dyn * goa_ref[...]
    dya = ra * (t - yah * jnp.mean(t * yah, axis=-1, keepdims=True))
    yg = yg_ref[...]
    rg = lax.rsqrt(jnp.mean(yg * yg, axis=-1, keepdims=True) + EPS)
    ygh = yg * rg
    dyn = dy_ref[:, d.attn_w:]
    dgog_ref[...] += jnp.sum(dyn * ygh, axis=0, keepdims=True)
    t = dyn * gog_ref[...]
    dyg = rg * (t - ygh * jnp.mean(t * ygh, axis=-1, keepdims=True))

    kv_cur = proj_ref[:, d.attn_w:d.attn_w + 2 * BLK]
    kg = kg_ref[...]
    kb, rk, kk, vv = _band(kv_cur, kvp_ref[...], kg, lo2)
    col, distf, valid = _score_geometry(first)
    qg = qg_ref[...]
    ck = [jnp.zeros((BLK, 2 * BLK), F32) for _ in range(N_KV_HEADS)]
    cv = [jnp.zeros((BLK, 2 * BLK), F32) for _ in range(N_KV_HEADS)]
    lo_rows = lax.broadcasted_iota(jnp.int32, (BLK, 2 * BLK), 0) < HEAD_DIM
    dsk = jnp.zeros((1, BLK), F32)
    dqg = jnp.zeros((1, BLK), F32)
    for j in range(d.n_pairs):
        h0, h1 = 2 * j, 2 * j + 1
        kh = h0 // d.group
        cols = slice(BLK * j, BLK * (j + 1))
        q2 = proj_ref[:, cols]
        rq = _head_rstd(q2, lo)
        qh = q2 * rq
        qn = qh * qg
        probs, psink = _pair_probs(qn, kk[kh], col, distf, valid, d.slopes[h0], d.slopes[h1],
                                   sink_ref[0, h0], sink_ref[0, h1])
        do2 = dya[:, cols]
        prod = do2 * ya[:, cols]
        delta = (jnp.sum(jnp.where(lo, prod, 0.0), axis=-1, keepdims=True),
                 jnp.sum(jnp.where(lo, 0.0, prod), axis=-1, keepdims=True))
        do2b = do2.astype(BF16)
        dp = _dot(do2b, vv[kh], NT)
        ds = []
        for hh in (0, 1):
            ds.append(probs[hh] * (dp[:, 2 * BLK * hh:2 * BLK * (hh + 1)] - delta[hh]))
            dsink = -jnp.sum(psink[hh] * delta[hh], axis=0, keepdims=True)
            dsk = dsk + jnp.where(lane_row == (h0 + hh), dsink, 0.0)
        dsb = (jnp.concatenate(ds, axis=1) * (HEAD_DIM ** -0.5)).astype(BF16)
        pb = jnp.concatenate(probs, axis=1).astype(BF16)
        qnb = qn.astype(BF16)
        dqn = _dot(dsb, kk[kh], NN)
        dkk = _dot(qnb, dsb, TN)
        dvv = _dot(do2b, pb, TN)
        ck[kh] = ck[kh] + jnp.where(lo_rows, dkk[:, :2 * BLK], 0.0) + jnp.where(lo_rows, 0.0, dkk[:, 2 * BLK:])
        cv[kh] = cv[kh] + jnp.where(lo_rows, dvv[:, :2 * BLK], 0.0) + jnp.where(lo_rows, 0.0, dvv[:, 2 * BLK:])
        dqg = dqg + jnp.sum(dqn * qh, axis=0, keepdims=True)
        t = dqn * qg
        dq2 = rq * (t - qh * (_half_sums(t * qh, lo) * (1.0 / HEAD_DIM)))
        dproj_ref[:, cols] = dq2.astype(dproj_ref.dtype)
    dsk_ref[...] += dsk
    dqg_ref[...] += dqg
    dkn = _unplace(jnp.transpose(ck[0]), jnp.transpose(ck[1]), lo2)
    dvb = _unplace(jnp.transpose(cv[0]), jnp.transpose(cv[1]), lo2)
    khat = kb * rk
    dkg_ref[...] += jnp.sum(dkn * khat, axis=0, keepdims=True)
    t = dkn * kg
    dkb = rk * (t - khat * (_half_sums(t * khat, lo2) * (1.0 / HEAD_DIM)))
    rows_cur = pl.ds(pl.multiple_of(i * BLK, BLK), BLK)
    rows_prev = pl.ds(pl.multiple_of(jnp.maximum(i - 1, 0) * BLK, BLK), BLK)
    dkv_ref[rows_cur, :] = jnp.concatenate([dkb[BLK:], dvb[BLK:]], axis=1)
    dkv_ref[rows_prev, :] += jnp.concatenate([dkb[:BLK], dvb[:BLK]], axis=1)
    dproj_ref[:, d.attn_w:d.attn_w + 2 * BLK] = jnp.zeros((BLK, 2 * BLK), dproj_ref.dtype)

    for g in range(d.n_groups):
        ucols = slice(d.u0 + BLK * g, d.u0 + BLK * (g + 1))
        vcols = slice(d.v0 + BLK * g, d.v0 + BLK * (g + 1))
        zu = proj_ref[:, ucols]
        zv = proj_ref[:, vcols]
        lg = lg_ref[g:g + 1, :]
        u, vh, rs, vn, wt, tril, mixed, du_dz, dv_dz = _gate_fwd_group(
            zu, zv, lg, lb_ref[g:g + 1, :], w_ref[g], b_ref[g], with_grad=True)
        dyg_g = dyg[:, BLK * g:BLK * (g + 1)]
        du = dyg_g * mixed
        dmix = dyg_g * u
        dmb = dmix.astype(BF16)
        db_ref[g:g + 1, :] += jnp.sum(jnp.transpose(dmix), axis=0, keepdims=True)
        dw_ref[g] += jnp.where(tril, _dot(dmb, vn.astype(BF16), NT), 0.0)
        dvn = _dot(wt.astype(BF16), dmb, TN)
        dlg_ref[g:g + 1, :] += jnp.sum(dvn * vh, axis=0, keepdims=True)
        dlb_ref[g:g + 1, :] += jnp.sum(dvn, axis=0, keepdims=True)
        dvh = dvn * lg
        dv = rs * (dvh - jnp.mean(dvh, axis=-1, keepdims=True) - vh * jnp.mean(dvh * vh, axis=-1, keepdims=True))
        dproj_ref[:, ucols] = (du * du_dz).astype(dproj_ref.dtype)
        dproj_ref[:, vcols] = (dv * dv_dz).astype(dproj_ref.dtype)


def _mixer_bwd(d, proj, ya, yg, dy, sinks, qg2, kg2, lg, lb, wsp, bcol, goa, gog):
    T = proj.shape[0]
    row, const2, const3, kv_prev = _mixer_specs(d, T)
    acc2 = lambda s: pl.BlockSpec(s, lambda i: (0, 0))
    G = d.n_groups
    out_shapes = [((T, d.in_w), BF16), ((T, 2 * BLK), F32), ((1, BLK), F32), ((1, BLK), F32), ((1, BLK), F32),
                  ((G, BLK), F32), ((G, BLK), F32), ((G, BLK, BLK), F32), ((G, BLK), F32),
                  ((1, d.attn_w), F32), ((1, d.gate_w), F32)]
    out_specs = [row(d.in_w)] + [acc2(s) for s, _ in out_shapes[1:7]] + \
                [pl.BlockSpec((G, BLK, BLK), lambda i: (0, 0, 0))] + [acc2(s) for s, _ in out_shapes[8:]]
    return pl.pallas_call(
        functools.partial(_mixer_bwd_body, d),
        name="mixer_bwd",
        grid=(T // BLK,),
        in_specs=[pl.BlockSpec(memory_space=pltpu.SMEM), row(d.in_w), kv_prev, row(d.attn_w), row(d.gate_w),
                  row(d.attn_w + d.gate_w), const2(qg2), const2(kg2), const2(lg), const2(lb), const3(wsp),
                  const3(bcol), const2(goa), const2(gog)],
        out_specs=out_specs,
        out_shape=[jax.ShapeDtypeStruct(s, t) for s, t in out_shapes],
        compiler_params=_params(("arbitrary",)),
    )(sinks, proj, proj, ya, yg, dy, qg2, kg2, lg, lb, wsp, bcol, goa, gog)


def _put_kv_body(dkv_ref, dproj_in_ref, dproj_ref):
    del dproj_in_ref
    dproj_ref[...] = dkv_ref[...].astype(dproj_ref.dtype)


def _put_kv(d, dproj, dkv):
    T = dproj.shape[0]
    tr = _pick(T, 1024, 16)
    return pl.pallas_call(
        functools.partial(_put_kv_body),
        name="put_kv",
        grid=(T // tr,),
        in_specs=[pl.BlockSpec((tr, 2 * BLK), lambda i: (i, 0)), pl.BlockSpec(memory_space=pl.ANY)],
        out_specs=pl.BlockSpec((tr, 2 * BLK), lambda i: (i, d.kv_col)),
        out_shape=jax.ShapeDtypeStruct(dproj.shape, dproj.dtype),
        input_output_aliases={1: 0},
        compiler_params=_params(("parallel",)),
    )(dkv, dproj)


def _add_pair_body(pc_ref, own_ref, got_ref, o_ref):
    del pc_ref
    o_ref[...] = (own_ref[...].astype(F32) + got_ref[...].astype(F32)).astype(o_ref.dtype)


def _add_pair(g4, got, pc, name):
    n, _, h, C = g4.shape
    tr = _pick(h, 512, 16)
    return pl.pallas_call(
        functools.partial(_add_pair_body),
        name=name,
        grid_spec=pltpu.PrefetchScalarGridSpec(
            num_scalar_prefetch=1,
            grid=(n, h // tr),
            in_specs=[pl.BlockSpec((None, None, tr, C), lambda q, i, pc: (q, pc[1], i, 0)),
                      pl.BlockSpec((None, tr, C), lambda q, i, pc: (q, i, 0))],
            out_specs=pl.BlockSpec((None, tr, C), lambda q, i, pc: (q, i, 0)),
        ),
        out_shape=jax.ShapeDtypeStruct((n, h, C), g4.dtype),
        compiler_params=_params(("parallel", "parallel")),
    )(pc, g4, got)


def _adamw_update(w, g, m, v):
    m = ADAM_B1 * m + (1.0 - ADAM_B1) * g
    v = ADAM_B2 * v + (1.0 - ADAM_B2) * (g * g)
    m_hat = m / (1.0 - ADAM_B1 ** ADAM_STEP)
    v_hat = v / (1.0 - ADAM_B2 ** ADAM_STEP)
    return -ADAM_LR * (m_hat / (jnp.sqrt(v_hat) + ADAM_EPS) + ADAM_WD * w), m, v


def _adamw_body(w_ref, g_ref, m_ref, v_ref, d_ref, nm_ref, nv_ref):
    d_ref[...], nm_ref[...], nv_ref[...] = _adamw_update(w_ref[...], g_ref[...], m_ref[...], v_ref[...])


def _adamw(w, g, m, v, name):
    R, C = w.shape
    tr = _pick(R, 512, 8)
    blk = pl.BlockSpec((tr, C), lambda i: (i, 0))
    return pl.pallas_call(
        functools.partial(_adamw_body),
        name=name,
        grid=(R // tr,),
        in_specs=[blk] * 4,
        out_specs=[blk] * 3,
        out_shape=[jax.ShapeDtypeStruct((R, C), F32)] * 3,
        compiler_params=_params(("parallel",)),
    )(w, g, m, v)


def _adamw_halves_body(pc_ref, w_ref, own_ref, got_ref, m_ref, v_ref, g_ref, d_ref, nm_ref, nv_ref):
    mine = pl.program_id(0) == pc_ref[1]

    def update(g):
        g_ref[...] = g
        d_ref[...], nm_ref[...], nv_ref[...] = _adamw_update(w_ref[...], g, m_ref[...], v_ref[...])

    @pl.when(mine)
    def _():
        update(own_ref[...].astype(F32))

    @pl.when(jnp.logical_not(mine))
    def _():
        update(got_ref[...].astype(F32))


def _adamw_halves(w, own, got, m, v, pc, name):
    h, C = own.shape
    tr = _pick(h, 512, 8)
    full = pl.BlockSpec((None, tr, C), lambda hh, i, pc: (hh, i, 0))
    mine = pl.BlockSpec((tr, C), lambda hh, i, pc: (jnp.where(hh == pc[1], i, 0), 0))
    theirs = pl.BlockSpec((tr, C), lambda hh, i, pc: (jnp.where(hh == pc[1], 0, i), 0))
    return pl.pallas_call(
        functools.partial(_adamw_halves_body),
        name=name,
        grid_spec=pltpu.PrefetchScalarGridSpec(
            num_scalar_prefetch=1,
            grid=(2, h // tr),
            in_specs=[full, mine, theirs, full, full],
            out_specs=[full] * 4,
        ),
        out_shape=[jax.ShapeDtypeStruct((2, h, C), F32)] * 4,
        compiler_params=_params(("parallel", "parallel")),
    )(pc, w.reshape(2, h, C), own, got, m.reshape(2, h, C), v.reshape(2, h, C))


def _me():
    x, y, c = lax.axis_index("x"), lax.axis_index("y"), lax.axis_index("c")
    chips = [(1 - x, y), (x, 1 - y), (1 - x, 1 - y)]
    return x, y, c, chips


def _cast_into_body(pc_ref, w_ref, o_ref):
    del pc_ref
    o_ref[...] = w_ref[...].astype(o_ref.dtype)


def _cast_into(w, pc, name, side_by_side=False):
    Rs, C = w.shape
    h = Rs // 2
    tr = _pick(h, 512, 16)
    if side_by_side:
        out_spec = pl.BlockSpec((None, tr, C), lambda hh, i, pc: (hh, i, pc[0]))
        out_shape = jax.ShapeDtypeStruct((2, h, N_CHIPS * C), BF16)
    else:
        out_spec = pl.BlockSpec((None, None, tr, C), lambda hh, i, pc: (pc[0], hh, i, 0))
        out_shape = jax.ShapeDtypeStruct((N_CHIPS, 2, h, C), BF16)
    return pl.pallas_call(
        functools.partial(_cast_into_body),
        name=name,
        grid_spec=pltpu.PrefetchScalarGridSpec(
            num_scalar_prefetch=1,
            grid=(2, h // tr),
            in_specs=[pl.BlockSpec((None, tr, C), lambda hh, i, pc: (hh, i, 0))],
            out_specs=out_spec,
        ),
        out_shape=out_shape,
        compiler_params=_params(("parallel", "parallel")),
    )(pc, w.reshape(2, h, C))


MAX_PIECES = 4


def _send_tile_to_sibling(src_of, dst_of, tr, dst_total, send_sems, recv_sem, last):
    x, y, c, _ = _me()
    pieces = MAX_PIECES if tr % (16 * MAX_PIECES) == 0 else (2 if tr % 32 == 0 else 1)
    n = tr // pieces
    copies = [pltpu.make_async_remote_copy(src_ref=src_of(k * n, n), dst_ref=dst_of(k * n, n), send_sem=send_sems.at[k],
                                           recv_sem=recv_sem, device_id=(x, y, 1 - c), device_id_type=MESH)
              for k in range(pieces)]
    for cp in copies:
        cp.start()
    for cp in copies:
        cp.wait_send()

    @pl.when(last)
    def _():
        pltpu.make_async_remote_copy(src_ref=dst_total, dst_ref=dst_total, send_sem=send_sems.at[0], recv_sem=recv_sem,
                                     device_id=(x, y, 1 - c), device_id_type=MESH).wait_recv()


TILE_SEMS = [pltpu.SemaphoreType.DMA((MAX_PIECES,)), pltpu.SemaphoreType.DMA(())]


def _ag_pair_body(tr, n_i, pc_ref, tile_ref, buf_ref, send_sem, recv_sem):
    j, i = pl.program_id(0), pl.program_id(1)
    q = pc_ref[0] ^ (j + 1)
    c = pc_ref[1]
    r_tile = pl.multiple_of(i * tr, tr)
    last = jnp.logical_and(j == N_CHIPS - 2, i == n_i - 1)
    if len(buf_ref.shape) == 4:
        _send_tile_to_sibling(lambda r0, n: tile_ref.at[:, :, pl.ds(r0, n)],
                              lambda r0, n: buf_ref.at[pl.ds(q, 1), pl.ds(c, 1), pl.ds(r_tile + r0, n)], tr,
                              buf_ref.at[pl.ds(0, N_CHIPS - 1), 0], send_sem, recv_sem, last)
    else:
        cs = buf_ref.shape[2] // N_CHIPS
        cols = pl.ds(pl.multiple_of(q * cs, BLK), cs)
        _send_tile_to_sibling(lambda r0, n: tile_ref.at[:, pl.ds(r0, n)],
                              lambda r0, n: buf_ref.at[pl.ds(c, 1), pl.ds(r_tile + r0, n), cols], tr,
                              buf_ref.at[0, :, pl.ds(0, (N_CHIPS - 1) * cs)], send_sem, recv_sem, last)


def _ag_pair(buf, pc, name):
    if len(buf.shape) == 4:
        _, _, h, C = buf.shape
        tile = lambda tr: pl.BlockSpec((1, 1, tr, C), lambda j, i, pc: (pc[0] ^ (j + 1), pc[1], i, 0))
    else:
        _, h, C = buf.shape
        tile = lambda tr: pl.BlockSpec((1, tr, C // N_CHIPS), lambda j, i, pc: (pc[1], i, pc[0] ^ (j + 1)))
    tr = _pick(h, 512, 16)
    return pl.pallas_call(
        functools.partial(_ag_pair_body, tr, h // tr),
        name=name,
        grid_spec=pltpu.PrefetchScalarGridSpec(
            num_scalar_prefetch=1,
            grid=(N_CHIPS - 1, h // tr),
            in_specs=[tile(tr)],
            out_specs=HBM,
            scratch_shapes=TILE_SEMS,
        ),
        out_shape=jax.ShapeDtypeStruct(buf.shape, buf.dtype),
        input_output_aliases={1: 0},
        compiler_params=_params(("arbitrary", "arbitrary")),
    )(pc, buf)


def _swap_halves_body(tr, n_q, n_i, pc_ref, tile_ref, got_ref, send_sem, recv_sem):
    del pc_ref
    q, i = pl.program_id(0), pl.program_id(1)
    r_tile = pl.multiple_of(i * tr, tr)
    _send_tile_to_sibling(lambda r0, n: tile_ref.at[:, :, pl.ds(r0, n)],
                          lambda r0, n: got_ref.at[pl.ds(q, 1), :, pl.ds(r_tile + r0, n)], tr, got_ref, send_sem, recv_sem,
                          jnp.logical_and(q == n_q - 1, i == n_i - 1))


def _swap_halves(g4, pc, name):
    n, _, h, C = g4.shape
    tr = _pick(h, 512, 16)
    return pl.pallas_call(
        functools.partial(_swap_halves_body, tr, n, h // tr),
        name=name,
        grid_spec=pltpu.PrefetchScalarGridSpec(
            num_scalar_prefetch=1,
            grid=(n, h // tr),
            in_specs=[pl.BlockSpec((1, 1, tr, C), lambda q, i, pc: (q, 1 - pc[1], i, 0))],
            out_specs=HBM,
            scratch_shapes=TILE_SEMS,
        ),
        out_shape=jax.ShapeDtypeStruct((n, 1, h, C), g4.dtype),
        compiler_params=_params(("arbitrary", "arbitrary")),
    )(pc, g4).reshape(n, h, C)


def _ici_copy(src, dst, send_sems, recv_sems, j, chip, c):
    return pltpu.make_async_remote_copy(src_ref=src, dst_ref=dst, send_sem=send_sems.at[j], recv_sem=recv_sems.at[j],
                                        device_id=(chip[0], chip[1], c), device_id_type=MESH)


def _token_spec():
    return jax.ShapeDtypeStruct((8, BLK), F32), pl.BlockSpec(memory_space=pltpu.VMEM)


def _slab(buf_ref, q, c):
    if len(buf_ref.shape) == 4:
        return buf_ref.at[q, c]
    cs = buf_ref.shape[2] // N_CHIPS
    return buf_ref.at[c, :, pl.ds(pl.multiple_of(q * cs, BLK), cs)]


def _ag_start_body(both_cores, buf_ref, after_ref, send_sems, recv_sems, buf_thru, token_ref):
    del after_ref, buf_thru
    x, y, c, chips = _me()
    mine = _slab(buf_ref, 2 * x + y, c)
    for j, chip in enumerate(chips):
        _ici_copy(mine, mine, send_sems, recv_sems, j, chip, c).start()
    if both_cores:
        for j, chip in enumerate(chips):
            _ici_copy(mine, mine, send_sems, recv_sems, N_CHIPS - 1 + j, chip, 1 - c).start()
    token_ref[...] = jnp.zeros_like(token_ref)


def _ag_start(buf, after, name, both_cores=False):
    tok_shape, tok_spec = _token_spec()
    sems = pltpu.SemaphoreType.DMA(((N_CHIPS - 1) * (2 if both_cores else 1),))
    return pl.pallas_call(
        functools.partial(_ag_start_body, both_cores),
        name=name,
        in_specs=[HBM, ANY],
        out_specs=[SEM, SEM, HBM, tok_spec],
        out_shape=[sems, sems, pltpu.HBM(buf.shape, buf.dtype), tok_shape],
        input_output_aliases={0: 2},
        compiler_params=pltpu.CompilerParams(has_side_effects=EFFECT),
    )(pltpu.with_memory_space_constraint(buf, pltpu.HBM), after)


def _ag_wait_body(both_cores, buf_ref, send_sems, recv_sems, after_ref, buf_out):
    del after_ref, buf_out
    x, y, c, chips = _me()
    mine = _slab(buf_ref, 2 * x + y, c)
    for j, chip in enumerate(chips):
        theirs = _slab(buf_ref, 2 * chip[0] + chip[1], c)
        _ici_copy(mine, mine, send_sems, recv_sems, j, chip, c).wait_send()
        _ici_copy(theirs, theirs, send_sems, recv_sems, j, chip, c).wait_recv()
    if both_cores:
        for j, chip in enumerate(chips):
            theirs = _slab(buf_ref, 2 * chip[0] + chip[1], 1 - c)
            _ici_copy(mine, mine, send_sems, recv_sems, N_CHIPS - 1 + j, chip, 1 - c).wait_send()
            _ici_copy(theirs, theirs, send_sems, recv_sems, N_CHIPS - 1 + j, chip, 1 - c).wait_recv()


def _ag_wait(buf, send_sems, recv_sems, after, name, both_cores=False):
    return pl.pallas_call(
        functools.partial(_ag_wait_body, both_cores),
        name=name,
        in_specs=[HBM, SEM, SEM, ANY],
        out_specs=HBM,
        out_shape=pltpu.HBM(buf.shape, buf.dtype),
        input_output_aliases={0: 0},
        compiler_params=pltpu.CompilerParams(has_side_effects=EFFECT),
    )(buf, send_sems, recv_sems, after)


def _rs_start_body(pair_ref, land_ref, after_ref, send_sems, recv_sems, pair_thru, land_thru, token_ref):
    del after_ref, pair_thru, land_thru
    x, y, c, chips = _me()
    for j, chip in enumerate(chips):
        _ici_copy(pair_ref.at[2 * chip[0] + chip[1]], land_ref.at[j], send_sems, recv_sems, j, chip, c).start()
    token_ref[...] = jnp.zeros_like(token_ref)


def _rs_start(pair, after, name):
    n, h, C = pair.shape
    tok_shape, tok_spec = _token_spec()
    sems = pltpu.SemaphoreType.DMA((N_CHIPS - 1,))
    land = pltpu.with_memory_space_constraint(lax.empty((N_CHIPS - 1, h, C), pair.dtype), pltpu.HBM)
    return pl.pallas_call(
        functools.partial(_rs_start_body),
        name=name,
        in_specs=[HBM, HBM, ANY],
        out_specs=[SEM, SEM, HBM, HBM, tok_spec],
        out_shape=[sems, sems, pltpu.HBM(pair.shape, pair.dtype), pltpu.HBM(land.shape, land.dtype), tok_shape],
        input_output_aliases={0: 2, 1: 3},
        compiler_params=pltpu.CompilerParams(has_side_effects=EFFECT),
    )(pltpu.with_memory_space_constraint(pair, pltpu.HBM), land, after)


def _rs_wait_body(pair_ref, land_ref, send_sems, recv_sems, after_ref, pair_out, land_out):
    del after_ref, pair_out, land_out
    x, y, c, chips = _me()
    for j, chip in enumerate(chips):
        _ici_copy(pair_ref.at[0], land_ref.at[j], send_sems, recv_sems, j, chip, c).wait_send()
        _ici_copy(pair_ref.at[0], land_ref.at[j], send_sems, recv_sems, j, chip, c).wait_recv()


def _rs_wait(pair, land, send_sems, recv_sems, after, name):
    return pl.pallas_call(
        functools.partial(_rs_wait_body),
        name=name,
        in_specs=[HBM, HBM, SEM, SEM, ANY],
        out_specs=[HBM, HBM],
        out_shape=[pltpu.HBM(pair.shape, pair.dtype), pltpu.HBM(land.shape, land.dtype)],
        input_output_aliases={0: 0, 1: 1},
        compiler_params=pltpu.CompilerParams(has_side_effects=EFFECT),
    )(pair, land, send_sems, recv_sems, after)


def _swap_copy(g4_ref, got_ref, send_sem, recv_sem):
    x, y, c, _ = _me()
    return pltpu.make_async_remote_copy(src_ref=g4_ref.at[:, 1 - c], dst_ref=got_ref, send_sem=send_sem,
                                        recv_sem=recv_sem, device_id=(x, y, 1 - c), device_id_type=MESH)


def _swap_start_body(g4_ref, got_ref, send_sem, recv_sem, g4_thru, got_thru, token_ref):
    del g4_thru, got_thru
    _swap_copy(g4_ref, got_ref, send_sem, recv_sem).start()
    token_ref[...] = jnp.zeros_like(token_ref)


def _swap_start(g4, name):
    n, _, h, C = g4.shape
    tok_shape, tok_spec = _token_spec()
    sem = pltpu.SemaphoreType.DMA(())
    got = pltpu.with_memory_space_constraint(lax.empty((n, h, C), g4.dtype), pltpu.HBM)
    return pl.pallas_call(
        functools.partial(_swap_start_body),
        name=name,
        in_specs=[HBM, HBM],
        out_specs=[SEM, SEM, HBM, HBM, tok_spec],
        out_shape=[sem, sem, pltpu.HBM(g4.shape, g4.dtype), pltpu.HBM(got.shape, got.dtype), tok_shape],
        input_output_aliases={0: 2, 1: 3},
        compiler_params=pltpu.CompilerParams(has_side_effects=EFFECT),
    )(pltpu.with_memory_space_constraint(g4, pltpu.HBM), got)


def _swap_wait_body(g4_ref, got_ref, send_sem, recv_sem, after_ref, g4_out, got_out):
    del after_ref, g4_out, got_out
    cp = _swap_copy(g4_ref, got_ref, send_sem, recv_sem)
    cp.wait_send()
    cp.wait_recv()


def _swap_wait(g4, got, send_sem, recv_sem, after, name):
    return pl.pallas_call(
        functools.partial(_swap_wait_body),
        name=name,
        in_specs=[HBM, HBM, SEM, SEM, ANY],
        out_specs=[HBM, HBM],
        out_shape=[pltpu.HBM(g4.shape, g4.dtype), pltpu.HBM(got.shape, got.dtype)],
        input_output_aliases={0: 0, 1: 1},
        compiler_params=pltpu.CompilerParams(has_side_effects=EFFECT),
    )(g4, got, send_sem, recv_sem, after)


def _add_chips_body(tr, n_i, pc_ref, own_ref, l0_ref, l1_ref, l2_ref, o_ref, got_ref, send_sems, recv_sem):
    del pc_ref
    i = pl.program_id(0)
    r = own_ref[...].astype(F32) + l0_ref[...].astype(F32)
    o_ref[...] = (r + l1_ref[...].astype(F32) + l2_ref[...].astype(F32)).astype(o_ref.dtype)
    r_tile = pl.multiple_of(i * tr, tr)
    _send_tile_to_sibling(lambda r0, n: o_ref.at[pl.ds(r0, n)], lambda r0, n: got_ref.at[pl.ds(r_tile + r0, n)], tr,
                          got_ref, send_sems, recv_sem, i == n_i - 1)


def _add_chips(pair, land, pc, name):
    _, h, C = pair.shape
    tr = _pick(h, 256, 16)
    slot = lambda j: pl.BlockSpec((None, tr, C), lambda i, pc: (j, i, 0))
    return pl.pallas_call(
        functools.partial(_add_chips_body, tr, h // tr),
        name=name,
        grid_spec=pltpu.PrefetchScalarGridSpec(
            num_scalar_prefetch=1,
            grid=(h // tr,),
            in_specs=[pl.BlockSpec((None, tr, C), lambda i, pc: (pc[0], i, 0)), slot(0), slot(1), slot(2)],
            out_specs=[pl.BlockSpec((tr, C), lambda i, pc: (i, 0)), HBM],
            scratch_shapes=TILE_SEMS,
        ),
        out_shape=[jax.ShapeDtypeStruct((h, C), pair.dtype), jax.ShapeDtypeStruct((h, C), pair.dtype)],
        compiler_params=_params(("arbitrary",)),
    )(pc, pair, land, land, land)


def _peer(r):
    x, y, c, _ = _me()
    return (x ^ ((r >> 2) & 1), y ^ ((r >> 1) & 1), c ^ (r & 1))


def _ar_start_body(x_ref, land_ref, send_sems, recv_sems, x_thru, land_thru, token_ref):
    del x_thru, land_thru
    for r in range(1, N_DEV):
        pltpu.make_async_remote_copy(src_ref=x_ref, dst_ref=land_ref.at[r - 1], send_sem=send_sems.at[r - 1],
                                     recv_sem=recv_sems.at[r - 1], device_id=_peer(r), device_id_type=MESH).start()
    token_ref[...] = jnp.zeros_like(token_ref)


def _ar_start(packed):
    tok_shape, tok_spec = _token_spec()
    sems = pltpu.SemaphoreType.DMA((N_DEV - 1,))
    land = pltpu.with_memory_space_constraint(lax.empty((N_DEV - 1,) + packed.shape, packed.dtype), pltpu.HBM)
    return pl.pallas_call(
        functools.partial(_ar_start_body),
        name="ar_start",
        in_specs=[HBM, HBM],
        out_specs=[SEM, SEM, HBM, HBM, tok_spec],
        out_shape=[sems, sems, pltpu.HBM(packed.shape, packed.dtype), pltpu.HBM(land.shape, land.dtype), tok_shape],
        input_output_aliases={0: 2, 1: 3},
        compiler_params=pltpu.CompilerParams(has_side_effects=EFFECT),
    )(pltpu.with_memory_space_constraint(packed, pltpu.HBM), land)


def _ar_wait_body(x_ref, land_ref, send_sems, recv_sems, after_ref, x_out, land_out):
    del after_ref, x_out, land_out
    for r in range(1, N_DEV):
        cp = pltpu.make_async_remote_copy(src_ref=x_ref, dst_ref=land_ref.at[r - 1], send_sem=send_sems.at[r - 1],
                                          recv_sem=recv_sems.at[r - 1], device_id=_peer(r), device_id_type=MESH)
        cp.wait_send()
        cp.wait_recv()


def _ar_wait(packed, land, send_sems, recv_sems, after):
    return pl.pallas_call(
        functools.partial(_ar_wait_body),
        name="ar_wait",
        in_specs=[HBM, HBM, SEM, SEM, ANY],
        out_specs=[HBM, HBM],
        out_shape=[pltpu.HBM(packed.shape, packed.dtype), pltpu.HBM(land.shape, land.dtype)],
        input_output_aliases={0: 0, 1: 1},
        compiler_params=pltpu.CompilerParams(has_side_effects=EFFECT),
    )(packed, land, send_sems, recv_sems, after)


def _ar_sum_body(me_ref, own_ref, *rest):
    o_ref = rest[N_DEV]
    acc = None
    for dev in range(N_DEV):
        term = jnp.where(me_ref[0] == dev, own_ref[...], rest[dev][...])
        acc = term if acc is None else acc + term
    o_ref[...] = acc


def _ar_sum(packed, land, me):
    R, C = packed.shape
    tr = _pick(R, 552, 8)
    own = pl.BlockSpec((tr, C), lambda i, me: (i, 0))
    slot = lambda dev: pl.BlockSpec((None, tr, C), lambda i, me: (jnp.maximum((dev ^ me[0]) - 1, 0), i, 0))
    return pl.pallas_call(
        functools.partial(_ar_sum_body),
        name="ar_sum",
        grid_spec=pltpu.PrefetchScalarGridSpec(
            num_scalar_prefetch=1,
            grid=(R // tr,),
            in_specs=[own] + [slot(dev) for dev in range(N_DEV)],
            out_specs=pl.BlockSpec((tr, C), lambda i, me: (i, 0)),
        ),
        out_shape=jax.ShapeDtypeStruct((R, C), F32),
        compiler_params=_params(("parallel",)),
    )(me, packed, *([land] * N_DEV))


def _pack(arrays):
    rows = []
    for a in arrays:
        flat = a.reshape(-1).astype(F32)
        pad = (-flat.shape[0]) % BLK
        rows.append(jnp.pad(flat, (0, pad)).reshape(-1, BLK))
    packed = jnp.concatenate(rows, axis=0)
    pad = (-packed.shape[0]) % 8
    return jnp.pad(packed, ((0, pad), (0, 0)))


def _unpack(packed, shapes):
    out, r = [], 0
    for s in shapes:
        n = 1
        for k in s:
            n *= k
        nr = -(-n // BLK)
        out.append(packed[r:r + nr].reshape(-1)[:n].reshape(s))
        r += nr
    return out


def kernel(x, norm1_g, w_in, q_norm_g, k_norm_g, attn_sinks, gate_ln_g, gate_ln_b, w_spatial, b_spatial, out_norm_attn_g, out_norm_gate_g, w_out, norm2_g, w_ffn_gate, w_ffn_up, w_ffn_down, loss_target, m_norm1_g, m_w_in, m_q_norm_g, m_k_norm_g, m_attn_sinks, m_gate_ln_g, m_gate_ln_b, m_w_spatial, m_b_spatial, m_out_norm_attn_g, m_out_norm_gate_g, m_w_out, m_norm2_g, m_w_ffn_gate, m_w_ffn_up, m_w_ffn_down, v_norm1_g, v_w_in, v_q_norm_g, v_k_norm_g, v_attn_sinks, v_gate_ln_g, v_gate_ln_b, v_w_spatial, v_b_spatial, v_out_norm_attn_g, v_out_norm_gate_g, v_w_out, v_norm2_g, v_w_ffn_gate, v_w_ffn_up, v_w_ffn_down):
    bl, seq, D = x.shape
    T = bl * seq
    attn_w, gate_w = out_norm_attn_g.shape[1], out_norm_gate_g.shape[1]
    d = _Dims(seq, attn_w, gate_w)
    G = d.n_groups
    in_w = d.in_w
    dff = w_ffn_gate.shape[2] * N_CHIPS
    assert w_in.shape[2] * N_CHIPS == in_w and seq % BLK == 0 and attn_w % (2 * BLK) == 0

    pc = jnp.stack([2 * lax.axis_index("x") + lax.axis_index("y"), lax.axis_index("c")]).astype(jnp.int32)
    big = [w_in[0], w_out[0], w_ffn_gate[0], w_ffn_up[0], w_ffn_down[0]]
    names = ["in", "out", "gate", "up", "down"]
    xf = x.reshape(T, D)
    tgt = loss_target.reshape(T, D)
    send, recv, buf, behind = _ag_start(_cast_into(big[0], pc, "cast_in"), norm1_g, "ag_start_in")
    started = [(send, recv, buf)]
    h1 = _rms_fwd(xf, norm1_g, "norm1_fwd", after=behind)
    behind = h1
    for w, n in zip(big[1:], names[1:]):
        buf = _cast_into(w, pc, "cast_" + n, side_by_side=n in ("gate", "up"))
        send, recv, buf, behind = _ag_start(buf, behind, "ag_start_" + n, both_cores=n == "down")
        started.append((send, recv, buf))

    def gathered(k, after):
        send, recv, buf = started[k]
        direct = names[k] == "down"
        buf = _ag_wait(buf, send, recv, after, "ag_wait_" + names[k], both_cores=direct)
        if not direct:
            buf = _ag_pair(buf, pc, "ag_pair_" + names[k])
        rs, cs = big[k].shape
        return buf.reshape(rs, N_CHIPS * cs) if len(buf.shape) == 3 else buf.reshape(N_CHIPS, rs, cs)

    qg2 = jnp.tile(q_norm_g, (1, 2))
    kg2 = jnp.tile(k_norm_g, (1, 2))
    lg, lb, wsp = gate_ln_g[0], gate_ln_b[0], w_spatial[0]
    bcol = jnp.broadcast_to(b_spatial[0][:, :, None], (G, BLK, BLK))

    win_full = jnp.transpose(gathered(0, behind), (1, 0, 2)).reshape(D, in_w)
    proj = _matmul(h1, win_full, "nn", F32, "proj_fwd", tm=512, tn=3328)
    ya, yg, yn = _mixer_fwd(d, proj, attn_sinks, qg2, kg2, lg, lb, wsp, bcol, out_norm_attn_g, out_norm_gate_g)
    wout_full = gathered(1, yn).reshape(attn_w + gate_w, D)
    x1 = _matmul(yn, wout_full, "nn", F32, "out_fwd", tm=1024, tn=1024, add=xf)
    h2 = _rms_fwd(x1, norm2_g, "norm2_fwd")
    wg_full, wu_full = gathered(2, h2), gathered(3, h2)
    a, b, f = _ffn_up(h2, wg_full, wu_full)
    wd_full = gathered(4, f).reshape(dff, D)
    dx2, dx2b, loss_local = _ffn_down_loss(f, wd_full, x1, tgt)

    def swap_start(g, n):
        g4 = g.reshape(N_CHIPS, 2, g.shape[1] // 2, g.shape[2])
        return _swap_start(g4, "rs_swap_start_" + n)

    def reduce_start(swapping, n, after):
        send, recv, g4, got, _ = swapping
        g4, got = _swap_wait(g4, got, send, recv, after, "rs_swap_wait_" + n)
        return _rs_start(_add_pair(g4, got, pc, "rs_add_pair_" + n), got, "rs_start_" + n)

    reducing = {}
    g_d = _matmul(f, dx2b, "tn", BF16, "ffn_down_dw", tm=1408, tn=1024, out_slab="r")
    swap_d = swap_start(g_d, "down")
    da, db = _ffn_down_dx(dx2b, wd_full, a, b, swap_d[4])
    g_g = _matmul(h2, da, "tn", BF16, "ffn_gate_dw", tm=1024, tn=1408, out_slab="c")
    swap_g = swap_start(g_g, "gate")
    reducing["down"] = reduce_start(swap_d, "down", swap_g[4])
    g_u = _matmul(h2, db, "tn", BF16, "ffn_up_dw", tm=1024, tn=1408, out_slab="c", after=reducing["down"][4])
    swap_u = swap_start(g_u, "up")
    reducing["gate"] = reduce_start(swap_g, "gate", swap_u[4])
    dh2 = _matmul(da, wg_full, "nt", F32, "ffn_gate_dx", tm=1024, tn=1024, tk=2816, after=reducing["gate"][4])
    dh2 = _matmul(db, wu_full, "nt", F32, "ffn_up_dx", tm=1024, tn=1024, tk=2816, add=dh2)
    reducing["up"] = reduce_start(swap_u, "up", dh2)
    dx1, dx1b, dg_norm2 = _rms_bwd(x1, norm2_g, dh2, dx2, "norm2_bwd", True)
    g_o = _matmul(yn, dx1b, "tn", BF16, "out_dw", tm=512, tn=1024, out_slab="r", after=reducing["up"][4])
    swap_o = swap_start(g_o, "out")
    dy = _matmul(dx1b, wout_full, "nt", F32, "out_dx", tm=1024, tn=1024, after=swap_o[4])
    (dproj, dkv, dqg, dkg, dsk, dlg, dlb, dwsp, dbsp, dgoa, dgog) = _mixer_bwd(
        d, proj, ya, yg, dy, attn_sinks, qg2, kg2, lg, lb, wsp, bcol, out_norm_attn_g, out_norm_gate_g)
    dproj = _put_kv(d, dproj, dkv)
    reducing["out"] = reduce_start(swap_o, "out", dproj)
    g_in_full = _matmul(h1, dproj, "tn", BF16, "proj_dw", tm=512, tn=3328, tk=2048,
                        after=reducing["out"][4])
    g_i = jnp.transpose(g_in_full.reshape(D, N_CHIPS, in_w // N_CHIPS), (1, 0, 2))
    g4_i = g_i.reshape(N_CHIPS, 2, D // 2, in_w // N_CHIPS)
    pair_i = _add_pair(g4_i, _swap_halves(g4_i, pc, "rs_swap_in"), pc, "rs_add_pair_in")
    reducing["in"] = _rs_start(pair_i, g_i, "rs_start_in")
    dh1 = _matmul(dproj, win_full, "nt", F32, "proj_dx", tm=1024, tn=1024, after=reducing["in"][4])
    dx, dg_norm1 = _rms_bwd(xf, norm1_g, dh1, dx1, "norm1_bwd", False)

    dqg64 = dqg[:, :HEAD_DIM] + dqg[:, HEAD_DIM:]
    dkg64 = dkg[:, :HEAD_DIM] + dkg[:, HEAD_DIM:]
    small_g_local = [dg_norm1, dqg64, dkg64, dsk[:, :d.n_heads], dlg, dlb, dwsp, dbsp, dgoa, dgog, dg_norm2,
                     loss_local]
    ar_send, ar_recv, ar_own, ar_land, ar_token = _ar_start(_pack(small_g_local))

    big_m = [m_w_in[0], m_w_out[0], m_w_ffn_gate[0], m_w_ffn_up[0], m_w_ffn_down[0]]
    big_v = [v_w_in[0], v_w_out[0], v_w_ffn_gate[0], v_w_ffn_up[0], v_w_ffn_down[0]]
    big_grads, big_d, big_nm, big_nv = [], [], [], []
    for w, m, v, n in zip(big, big_m, big_v, names):
        send, recv, pair, land, _ = reducing[n]
        pair, land = _rs_wait(pair, land, send, recv, ar_token, "rs_wait_" + n)
        own, got = _add_chips(pair, land, pc, "rs_add_chips_" + n)
        outs = _adamw_halves(w, own, got, m, v, pc, "adamw_" + n)
        for lst, o in zip((big_grads, big_d, big_nm, big_nv), outs):
            lst.append(o.reshape(w.shape))

    small_names_w = [norm1_g, q_norm_g, k_norm_g, attn_sinks, gate_ln_g, gate_ln_b, w_spatial, b_spatial,
                     out_norm_attn_g, out_norm_gate_g, norm2_g]
    small_m = [m_norm1_g, m_q_norm_g, m_k_norm_g, m_attn_sinks, m_gate_ln_g, m_gate_ln_b, m_w_spatial, m_b_spatial,
               m_out_norm_attn_g, m_out_norm_gate_g, m_norm2_g]
    small_v = [v_norm1_g, v_q_norm_g, v_k_norm_g, v_attn_sinks, v_gate_ln_g, v_gate_ln_b, v_w_spatial, v_b_spatial,
               v_out_norm_attn_g, v_out_norm_gate_g, v_norm2_g]
    shapes = [w.shape for w in small_names_w] + [loss_local.shape]
    ride = [jnp.zeros(loss_local.shape, F32)]
    ar_own, ar_land = _ar_wait(ar_own, ar_land, ar_send, ar_recv, big_nv[-1])
    me = (4 * lax.axis_index("x") + 2 * lax.axis_index("y") + lax.axis_index("c")).astype(jnp.int32).reshape(1)
    sg = _ar_sum(ar_own, ar_land, me)
    sd, snm, snv = _adamw(_pack(small_names_w + ride), sg, _pack(small_m + ride), _pack(small_v + ride), "adamw_small")
    small_g, small_d, small_nm, small_nv = (_unpack(t, shapes) for t in (sg, sd, snm, snv))
    loss = small_g[-1][0, 0]

    def order(small, bigs):
        s = list(small)
        bg = [t[None] for t in bigs]
        return [s[0], bg[0], s[1], s[2], s[3], s[4], s[5], s[6], s[7], s[8], s[9], bg[1], s[10], bg[2], bg[3], bg[4]]

    grad_x = dx.reshape(bl, seq, D)
    return (loss, grad_x, *order(small_g, big_grads), *order(small_d, big_d), *order(small_nm, big_nm),
            *order(small_nv, big_nv))
```

```python
import functools

import jax
import jax.numpy as jnp
from jax import lax
from jax.experimental import pallas as pl
from jax.experimental.pallas import tpu as pltpu

F32 = jnp.float32
BF16 = jnp.bfloat16
MESH = pl.DeviceIdType.MESH

EPS = 1e-6
HEAD_DIM = 64
N_KV_HEADS = 2
BLK = 128
N_CHIPS = 4
N_DEV = 8
NEG = -1e30

ADAM_LR = 0.001
ADAM_B1 = 0.9
ADAM_B2 = 0.999
ADAM_EPS = 1e-08
ADAM_WD = 0.01
ADAM_STEP = 10

VMEM_LIMIT = 56 * 1024 * 1024

NN = (((1,), (0,)), ((), ()))
NT = (((1,), (1,)), ((), ()))
TN = (((0,), (0,)), ((), ()))
HBM = pl.BlockSpec(memory_space=pltpu.HBM)
ANY = pl.BlockSpec(memory_space=pl.ANY)
SEM = pl.BlockSpec(memory_space=pltpu.SEMAPHORE)
EFFECT = pltpu.SideEffectType.DATAFLOW_SIDE_EFFECTING


def _dot(a, b, dn):
    return lax.dot_general(a, b, dn, preferred_element_type=F32)


def _pick(dim, pref, align=128):
    if dim <= pref:
        return dim
    t = (pref // align) * align
    while t >= align:
        if dim % t == 0:
            return t
        t -= align
    return dim


def _params(sem):
    return pltpu.CompilerParams(dimension_semantics=sem, vmem_limit_bytes=VMEM_LIMIT)


MM_CHUNK = 512
MM_TILE = 1024


def _col_chunks(tn):
    return [slice(c0, min(c0 + MM_CHUNK, tn)) for c0 in range(0, tn, MM_CHUNK)]


def _mm_body(dn, nk, has_add, has_after, *refs):
    a_ref, b_ref = refs[:2]
    add_ref = refs[2] if has_add else None
    o_ref = refs[2 + has_add + has_after]
    chunks = _col_chunks(o_ref.shape[-1])

    def dot(cols):
        return _dot(a_ref[...], b_ref[cols, :] if dn == NT else b_ref[:, cols], dn)

    def finish(cols, r):
        if add_ref is not None:
            r = r + add_ref[:, cols]
        o_ref[:, cols] = r.astype(o_ref.dtype)

    if nk == 1:
        for cols in chunks:
            finish(cols, dot(cols))
        return
    acc_ref = refs[-1]
    k = pl.program_id(2)

    @pl.when(k == 0)
    def _():
        for cols in chunks:
            acc_ref[:, cols] = dot(cols)

    if nk > 2:
        @pl.when(jnp.logical_and(k > 0, k < nk - 1))
        def _():
            for cols in chunks:
                acc_ref[:, cols] += dot(cols)

    @pl.when(k == nk - 1)
    def _():
        for cols in chunks:
            finish(cols, acc_ref[:, cols] + dot(cols))


def _matmul(a, b, mode, out_dtype, name, *, tm, tn, tk=None, add=None, out_slab=None, after=None):
    if mode == "nn":
        (M, K), N = a.shape, b.shape[1]
    elif mode == "nt":
        (M, K), N = a.shape, b.shape[0]
    else:
        (K, M), N = a.shape, b.shape[1]
    tk = K if tk is None else tk
    tm, tn, tk = _pick(M, tm), _pick(N, tn), _pick(K, tk)
    if out_slab == "c":
        tn = _pick(N // N_CHIPS, tn)
    if out_slab == "r":
        tm = _pick(M // N_CHIPS, tm)
    gm, gn, gk = M // tm, N // tn, K // tk

    if mode == "tn":
        a_spec = pl.BlockSpec((tk, tm), lambda j, i, k: (k, i))
        b_spec = pl.BlockSpec((tk, tn), lambda j, i, k: (k, j))
    else:
        a_spec = pl.BlockSpec((tm, tk), lambda j, i, k: (i, k))
        if mode == "nn":
            b_spec = pl.BlockSpec((tk, tn), lambda j, i, k: (k, j))
        else:
            b_spec = pl.BlockSpec((tn, tk), lambda j, i, k: (j, k))

    if out_slab == "c":
        per = (N // N_CHIPS) // tn
        o_spec = pl.BlockSpec((None, tm, tn), lambda j, i, k: (j // per, i, j % per))
        o_shape = jax.ShapeDtypeStruct((N_CHIPS, M, N // N_CHIPS), out_dtype)
    elif out_slab == "r":
        per = (M // N_CHIPS) // tm
        o_spec = pl.BlockSpec((None, tm, tn), lambda j, i, k: (i // per, i % per, j))
        o_shape = jax.ShapeDtypeStruct((N_CHIPS, M // N_CHIPS, N), out_dtype)
    else:
        o_spec = pl.BlockSpec((tm, tn), lambda j, i, k: (i, j))
        o_shape = jax.ShapeDtypeStruct((M, N), out_dtype)

    dn = {"nn": NN, "nt": NT, "tn": TN}[mode]
    in_specs = [a_spec, b_spec]
    args = [a, b]
    if add is not None:
        in_specs.append(pl.BlockSpec((tm, tn), lambda j, i, k: (i, j)))
        args.append(add)
    if after is not None:
        in_specs.append(ANY)
        args.append(after)
    return pl.pallas_call(
        functools.partial(_mm_body, dn, gk, add is not None, after is not None),
        name=name,
        grid=(gn, gm, gk),
        in_specs=in_specs,
        out_specs=o_spec,
        out_shape=o_shape,
        scratch_shapes=[pltpu.VMEM((tm, tn), F32)] if gk > 1 else [],
        compiler_params=_params(("parallel", "parallel", "arbitrary")),
    )(*args)


def _rms_fwd_body(x_ref, g_ref, *rest):
    h_ref = rest[-1]
    x = x_ref[...]
    r = lax.rsqrt(jnp.mean(x * x, axis=-1, keepdims=True) + EPS)
    h_ref[...] = (x * r * g_ref[...]).astype(h_ref.dtype)


def _rms_fwd(x, g, name, after=None):
    T, D = x.shape
    tr = _pick(T, 512, 16)
    extra = [] if after is None else [after]
    return pl.pallas_call(
        functools.partial(_rms_fwd_body),
        name=name,
        grid=(T // tr,),
        in_specs=[pl.BlockSpec((tr, D), lambda i: (i, 0)), pl.BlockSpec((1, D), lambda i: (0, 0))] + [ANY] * len(extra),
        out_specs=pl.BlockSpec((tr, D), lambda i: (i, 0)),
        out_shape=jax.ShapeDtypeStruct((T, D), BF16),
        compiler_params=_params(("parallel",)),
    )(x, g, *extra)


def _rms_bwd_body(with_bf16, x_ref, g_ref, dh_ref, res_ref, dx_ref, *rest):
    dg_ref = rest[-1]

    @pl.when(pl.program_id(0) == 0)
    def _():
        dg_ref[...] = jnp.zeros_like(dg_ref)

    x = x_ref[...]
    r = lax.rsqrt(jnp.mean(x * x, axis=-1, keepdims=True) + EPS)
    xh = x * r
    dh = dh_ref[...]
    dg_ref[...] += jnp.sum(dh * xh, axis=0, keepdims=True)
    t = dh * g_ref[...]
    dx = res_ref[...] + r * (t - xh * jnp.mean(t * xh, axis=-1, keepdims=True))
    dx_ref[...] = dx
    if with_bf16:
        rest[0][...] = dx.astype(BF16)


def _rms_bwd(x, g, dh, res, name, with_bf16):
    T, D = x.shape
    tr = _pick(T, 512, 16)
    row = pl.BlockSpec((tr, D), lambda i: (i, 0))
    vec = pl.BlockSpec((1, D), lambda i: (0, 0))
    extra = [jax.ShapeDtypeStruct((T, D), BF16)] if with_bf16 else []
    return pl.pallas_call(
        functools.partial(_rms_bwd_body, with_bf16),
        name=name,
        grid=(T // tr,),
        in_specs=[row, vec, row, row],
        out_specs=[row] + [row] * len(extra) + [vec],
        out_shape=[jax.ShapeDtypeStruct((T, D), F32)] + extra + [jax.ShapeDtypeStruct((1, D), F32)],
        compiler_params=_params(("arbitrary",)),
    )(x, g, dh, res)


def _ffn_up_body(h_ref, wg_ref, wu_ref, a_ref, b_ref, f_ref):
    for cols in _col_chunks(a_ref.shape[-1]):
        a = _dot(h_ref[...], wg_ref[:, cols], NN)
        b = _dot(h_ref[...], wu_ref[:, cols], NN)
        a_ref[:, cols] = a
        b_ref[:, cols] = b
        f_ref[:, cols] = (a * (1.0 / (1.0 + jnp.exp(-a))) * b).astype(f_ref.dtype)


def _ffn_up(h, wg, wu):
    T, D = h.shape
    F = wg.shape[1]
    tm, tn = _pick(T, MM_TILE), _pick(F, MM_CHUNK)
    hs = pl.BlockSpec((tm, D), lambda j, i: (i, 0))
    ws = pl.BlockSpec((D, tn), lambda j, i: (0, j))
    os = pl.BlockSpec((tm, tn), lambda j, i: (i, j))
    return pl.pallas_call(
        functools.partial(_ffn_up_body),
        name="ffn_up_fwd",
        grid=(F // tn, T // tm),
        in_specs=[hs, ws, ws],
        out_specs=[os, os, os],
        out_shape=[jax.ShapeDtypeStruct((T, F), F32), jax.ShapeDtypeStruct((T, F), F32),
                   jax.ShapeDtypeStruct((T, F), BF16)],
        compiler_params=_params(("parallel", "parallel")),
    )(h, wg, wu)


def _ffn_down_dx_body(dx_ref, wd_ref, a_ref, b_ref, after_ref, da_ref, db_ref):
    del after_ref
    for cols in _col_chunks(da_ref.shape[-1]):
        df = _dot(dx_ref[...], wd_ref[cols, :], NT)
        a = a_ref[:, cols]
        s = 1.0 / (1.0 + jnp.exp(-a))
        da_ref[:, cols] = (df * b_ref[:, cols] * (s * (1.0 + a * (1.0 - s)))).astype(da_ref.dtype)
        db_ref[:, cols] = (df * (a * s)).astype(db_ref.dtype)


def _ffn_down_dx(dx2b, wd, a, b, after):
    T, D = dx2b.shape
    F = wd.shape[0]
    tm, tn = _pick(T, MM_TILE // 2), _pick(F, F // N_CHIPS)
    xs = pl.BlockSpec((tm, D), lambda j, i: (i, 0))
    ws = pl.BlockSpec((tn, D), lambda j, i: (j, 0))
    os = pl.BlockSpec((tm, tn), lambda j, i: (i, j))
    return pl.pallas_call(
        functools.partial(_ffn_down_dx_body),
        name="ffn_down_dx",
        grid=(F // tn, T // tm),
        in_specs=[xs, ws, os, os, ANY],
        out_specs=[os, os],
        out_shape=[jax.ShapeDtypeStruct((T, F), BF16), jax.ShapeDtypeStruct((T, F), BF16)],
        compiler_params=_params(("parallel", "parallel")),
    )(dx2b, wd, a, b, after)


def _ffn_down_loss_body(nk, inv_d, f_ref, wd_ref, x1_ref, tgt_ref, dx2_ref, dx2b_ref, loss_ref, *scratch):
    j, i, k = pl.program_id(0), pl.program_id(1), pl.program_id(2)
    chunks = _col_chunks(dx2_ref.shape[-1])

    def dot(cols):
        return _dot(f_ref[...], wd_ref[:, cols], NN)

    @pl.when(jnp.logical_and(jnp.logical_and(j == 0, i == 0), k == 0))
    def _():
        loss_ref[...] = jnp.zeros_like(loss_ref)

    def finish(ffn_of):
        total = jnp.zeros((1, 1), F32)
        for cols in chunks:
            e = ffn_of(cols) + x1_ref[:, cols] - tgt_ref[:, cols]
            dx2 = e * inv_d
            dx2_ref[:, cols] = dx2
            dx2b_ref[:, cols] = dx2.astype(BF16)
            total = total + jnp.sum(jnp.sum(e * e, axis=-1, keepdims=True), axis=0, keepdims=True)
        loss_ref[...] += (0.5 * inv_d) * total

    if nk == 1:
        finish(dot)
        return
    acc_ref = scratch[0]

    @pl.when(k == 0)
    def _():
        for cols in chunks:
            acc_ref[:, cols] = dot(cols)

    if nk > 2:
        @pl.when(jnp.logical_and(k > 0, k < nk - 1))
        def _():
            for cols in chunks:
                acc_ref[:, cols] += dot(cols)

    @pl.when(k == nk - 1)
    def _():
        finish(lambda cols: acc_ref[:, cols] + dot(cols))


def _ffn_down_loss(f, wd, x1, tgt):
    T, F = f.shape
    D = wd.shape[1]
    tm, tn, tk = _pick(T, MM_TILE), _pick(D, MM_TILE), _pick(F, F // 2)
    gm, gn, gk = T // tm, D // tn, F // tk
    tile = pl.BlockSpec((tm, tn), lambda j, i, k: (i, j))
    return pl.pallas_call(
        functools.partial(_ffn_down_loss_body, gk, 1.0 / D),
        name="ffn_down_loss",
        grid=(gn, gm, gk),
        in_specs=[pl.BlockSpec((tm, tk), lambda j, i, k: (i, k)), pl.BlockSpec((tk, tn), lambda j, i, k: (k, j)),
                  tile, tile],
        out_specs=[tile, tile, pl.BlockSpec((1, 1), lambda j, i, k: (0, 0))],
        out_shape=[jax.ShapeDtypeStruct((T, D), F32), jax.ShapeDtypeStruct((T, D), BF16),
                   jax.ShapeDtypeStruct((1, 1), F32)],
        scratch_shapes=[pltpu.VMEM((tm, tn), F32)] if gk > 1 else [],
        compiler_params=_params(("arbitrary", "arbitrary", "arbitrary")),
    )(f, wd, x1, tgt)


def _lo_mask(shape):
    return lax.broadcasted_iota(jnp.int32, shape, len(shape) - 1) < HEAD_DIM


def _half_sums(t, lo):
    s_lo = jnp.sum(jnp.where(lo, t, 0.0), axis=-1, keepdims=True)
    s_hi = jnp.sum(jnp.where(lo, 0.0, t), axis=-1, keepdims=True)
    return jnp.where(lo, s_lo, s_hi)


def _head_rstd(t, lo):
    return lax.rsqrt(_half_sums(t * t, lo) * (1.0 / HEAD_DIM) + EPS)


def _place(t, lo, kv_head):
    if kv_head == 0:
        t_lo = jnp.where(lo, t, 0.0)
        t_hi = pltpu.roll(t_lo, HEAD_DIM, 1)
    else:
        t_hi = jnp.where(lo, 0.0, t)
        t_lo = pltpu.roll(t_hi, HEAD_DIM, 1)
    return jnp.concatenate([t_lo, t_hi], axis=0).astype(BF16)


def _unplace(c0, c1, lo):
    return jnp.where(lo, c0 + pltpu.roll(c0, HEAD_DIM, 1), c1 + pltpu.roll(c1, HEAD_DIM, 1))


def _band(kv_cur, kv_prev, kg, lo2):
    kb = jnp.concatenate([kv_prev[:, :BLK], kv_cur[:, :BLK]], axis=0)
    vb = jnp.concatenate([kv_prev[:, BLK:], kv_cur[:, BLK:]], axis=0)
    rk = _head_rstd(kb, lo2)
    kn = kb * rk * kg
    kk = [_place(kn, lo2, h) for h in range(N_KV_HEADS)]
    vv = [_place(vb, lo2, h) for h in range(N_KV_HEADS)]
    return kb, rk, kk, vv


def _score_geometry(first_i32):
    qi = lax.broadcasted_iota(jnp.int32, (BLK, 4 * BLK), 0)
    col = lax.broadcasted_iota(jnp.int32, (BLK, 4 * BLK), 1)
    kj = col & (2 * BLK - 1)
    dist = qi + BLK - kj
    valid = (dist >= 0) & (dist < BLK) & (kj >= first_i32 * BLK)
    return col, dist.astype(F32), valid


def _pair_logits(qn, kk, col, distf, valid, slope0, slope1):
    s = _dot(qn.astype(BF16), kk, NT) * (HEAD_DIM ** -0.5)
    slope = jnp.where(col < 2 * BLK, slope0, slope1)
    return jnp.where(valid, s - slope * distf, NEG)


def _pair_probs(qn, kk, col, distf, valid, slope0, slope1, sink0, sink1):
    return _softmax_halves(_pair_logits(qn, kk, col, distf, valid, slope0, slope1), sink0, sink1)


def _softmax_halves(logits, sink0, sink1):
    probs, psink = [], []
    for hh, sk in ((0, sink0), (1, sink1)):
        l = logits[:, 2 * BLK * hh:2 * BLK * (hh + 1)]
        m = jnp.maximum(jnp.max(l, axis=-1, keepdims=True), sk)
        p = jnp.exp(l - m)
        es = jnp.exp(sk - m)
        inv = 1.0 / (jnp.sum(p, axis=-1, keepdims=True) + es)
        probs.append(p * inv)
        psink.append(es * inv)
    return probs, psink


def _gelu(z, with_grad=False):
    cdf = 0.5 * (1.0 + lax.erf(z * (0.5 ** 0.5)))
    if not with_grad:
        return z * cdf
    return z * cdf, cdf + z * jnp.exp(-0.5 * z * z) * ((2.0 * jnp.pi) ** -0.5)


def _tril_w(w):
    r = lax.broadcasted_iota(jnp.int32, (BLK, BLK), 0)
    c = lax.broadcasted_iota(jnp.int32, (BLK, BLK), 1)
    return jnp.where(r >= c, w, 0.0), r >= c


def _gate_fwd_group(zu, zv, lg, lb, w, bcol, with_grad=False):
    u, v = _gelu(zu, with_grad), _gelu(zv, with_grad)
    if with_grad:
        (u, du_dz), (v, dv_dz) = u, v
    mu = jnp.mean(v, axis=-1, keepdims=True)
    vc = v - mu
    rs = lax.rsqrt(jnp.mean(vc * vc, axis=-1, keepdims=True) + EPS)
    vh = vc * rs
    vn = vh * lg + lb
    wt, tril = _tril_w(w)
    mixed = _dot(wt.astype(BF16), vn.astype(BF16), NN) + bcol
    if with_grad:
        return u, vh, rs, vn, wt, tril, mixed, du_dz, dv_dz
    return u, vh, rs, vn, wt, tril, mixed


class _Dims:
    def __init__(self, seq, attn_w, gate_w):
        self.seq, self.attn_w, self.gate_w = seq, attn_w, gate_w
        self.n_heads = attn_w // HEAD_DIM
        self.group = self.n_heads // N_KV_HEADS
        self.n_pairs = attn_w // BLK
        self.n_groups = gate_w // BLK
        self.kv_col = attn_w // (2 * BLK)
        self.u0 = attn_w + 2 * BLK
        self.v0 = self.u0 + gate_w
        self.in_w = self.v0 + gate_w
        self.slopes = [2.0 ** (-8.0 * (h + 1) / self.n_heads) for h in range(self.n_heads)]


def _mixer_fwd_body(d, sink_ref, proj_ref, kvp_ref, qg_ref, kg_ref, lg_ref, lb_ref, w_ref, b_ref, goa_ref, gog_ref,
                    ya_ref, yg_ref, y_ref, logit_scr, prob_scr):
    i = pl.program_id(0)
    first = (i % (d.seq // BLK) == 0).astype(jnp.int32)
    lo = _lo_mask((BLK, BLK))
    lo2 = _lo_mask((2 * BLK, BLK))
    kv_cur = proj_ref[:, d.attn_w:d.attn_w + 2 * BLK]
    _, _, kk, vv = _band(kv_cur, kvp_ref[...], kg_ref[...], lo2)
    col, distf, valid = _score_geometry(first)
    qg = qg_ref[...]
    for j in range(d.n_pairs):
        h0, h1 = 2 * j, 2 * j + 1
        q2 = proj_ref[:, BLK * j:BLK * (j + 1)]
        qn = q2 * _head_rstd(q2, lo) * qg
        logit_scr[j] = _pair_logits(qn, kk[h0 // d.group], col, distf, valid, d.slopes[h0], d.slopes[h1])
    for j in range(d.n_pairs):
        probs, _ = _softmax_halves(logit_scr[j], sink_ref[0, 2 * j], sink_ref[0, 2 * j + 1])
        prob_scr[j] = jnp.concatenate(probs, axis=1).astype(BF16)
    for g in range(d.n_groups):
        zu = proj_ref[:, d.u0 + BLK * g:d.u0 + BLK * (g + 1)]
        zv = proj_ref[:, d.v0 + BLK * g:d.v0 + BLK * (g + 1)]
        u, _, _, _, _, _, mixed = _gate_fwd_group(zu, zv, lg_ref[g:g + 1, :], lb_ref[g:g + 1, :], w_ref[g], b_ref[g])
        yg_ref[:, BLK * g:BLK * (g + 1)] = u * mixed
    for j in range(d.n_pairs):
        ya_ref[:, BLK * j:BLK * (j + 1)] = _dot(prob_scr[j], vv[2 * j // d.group], NN)
    ya = ya_ref[...]
    ra = lax.rsqrt(jnp.mean(ya * ya, axis=-1, keepdims=True) + EPS)
    y_ref[:, :d.attn_w] = (ya * ra * goa_ref[...]).astype(y_ref.dtype)
    yg = yg_ref[...]
    rg = lax.rsqrt(jnp.mean(yg * yg, axis=-1, keepdims=True) + EPS)
    y_ref[:, d.attn_w:] = (yg * rg * gog_ref[...]).astype(y_ref.dtype)


def _mixer_specs(d, T):
    row = lambda w: pl.BlockSpec((BLK, w), lambda i: (i, 0))
    const2 = lambda a: pl.BlockSpec(a.shape, lambda i: (0, 0))
    const3 = lambda a: pl.BlockSpec(a.shape, lambda i: (0, 0, 0))
    kv_prev = pl.BlockSpec((BLK, 2 * BLK), lambda i: (jnp.maximum(i - 1, 0), d.kv_col))
    return row, const2, const3, kv_prev


def _mixer_fwd(d, proj, sinks, qg2, kg2, lg, lb, wsp, bcol, goa, gog):
    T = proj.shape[0]
    row, const2, const3, kv_prev = _mixer_specs(d, T)
    return pl.pallas_call(
        functools.partial(_mixer_fwd_body, d),
        name="mixer_fwd",
        grid=(T // BLK,),
        in_specs=[pl.BlockSpec(memory_space=pltpu.SMEM), row(d.in_w), kv_prev, const2(qg2), const2(kg2),
                  const2(lg), const2(lb), const3(wsp), const3(bcol), const2(goa), const2(gog)],
        out_specs=[row(d.attn_w), row(d.gate_w), row(d.attn_w + d.gate_w)],
        out_shape=[jax.ShapeDtypeStruct((T, d.attn_w), F32), jax.ShapeDtypeStruct((T, d.gate_w), F32),
                   jax.ShapeDtypeStruct((T, d.attn_w + d.gate_w), BF16)],
        scratch_shapes=[pltpu.VMEM((d.n_pairs, BLK, 4 * BLK), F32), pltpu.VMEM((d.n_pairs, BLK, 4 * BLK), BF16)],
        compiler_params=_params(("parallel",)),
    )(sinks, proj, proj, qg2, kg2, lg, lb, wsp, bcol, goa, gog)


def _mixer_bwd_body(d, sink_ref, proj_ref, kvp_ref, ya_ref, yg_ref, dy_ref, qg_ref, kg_ref, lg_ref, lb_ref, w_ref,
                    b_ref, goa_ref, gog_ref,
                    dproj_ref, dkv_ref, dqg_ref, dkg_ref, dsk_ref, dlg_ref, dlb_ref, dw_ref, db_ref, dgoa_ref,
                    dgog_ref):
    i = pl.program_id(0)

    @pl.when(i == 0)
    def _():
        for r in (dqg_ref, dkg_ref, dsk_ref, dlg_ref, dlb_ref, dw_ref, db_ref, dgoa_ref, dgog_ref):
            r[...] = jnp.zeros_like(r)

    first = (i % (d.seq // BLK) == 0).astype(jnp.int32)
    lo = _lo_mask((BLK, BLK))
    lo2 = _lo_mask((2 * BLK, BLK))
    lane_row = lax.broadcasted_iota(jnp.int32, (1, BLK), 1)

    ya = ya_ref[...]
    ra = lax.rsqrt(jnp.mean(ya * ya, axis=-1, keepdims=True) + EPS)
    yah = ya * ra
    dyn = dy_ref[:, :d.attn_w]
    dgoa_ref[...] += jnp.sum(dyn * yah, axis=0, keepdims=True)
    t = dyn * goa_ref[...]
    dya = ra * (t - yah * jnp.mean(t * yah, axis=-1, keepdims=True))
    yg = yg_ref[...]
    rg = lax.rsqrt(jnp.mean(yg * yg, axis=-1, keepdims=True) + EPS)
    ygh = yg * rg
    dyn = dy_ref[:, d.attn_w:]
    dgog_ref[...] += jnp.sum(dyn * ygh, axis=0, keepdims=True)
    t = dyn * gog_ref[...]
    dyg = rg * (t - ygh * jnp.mean(t * ygh, axis=-1, keepdims=True))

    for g in range(d.n_groups):
        ucols = slice(d.u0 + BLK * g, d.u0 + BLK * (g + 1))
        vcols = slice(d.v0 + BLK * g, d.v0 + BLK * (g + 1))
        zu = proj_ref[:, ucols]
        zv = proj_ref[:, vcols]
        lg = lg_ref[g:g + 1, :]
        u, vh, rs, vn, wt, tril, mixed, du_dz, dv_dz = _gate_fwd_group(
            zu, zv, lg, lb_ref[g:g + 1, :], w_ref[g], b_ref[g], with_grad=True)
        dyg_g = dyg[:, BLK * g:BLK * (g + 1)]
        du = dyg_g * mixed
        dmix = dyg_g * u
        dmb = dmix.astype(BF16)
        db_ref[g:g + 1, :] += jnp.sum(jnp.transpose(dmix), axis=0, keepdims=True)
        dw_ref[g] += jnp.where(tril, _dot(dmb, vn.astype(BF16), NT), 0.0)
        dvn = _dot(wt.astype(BF16), dmb, TN)
        dlg_ref[g:g + 1, :] += jnp.sum(dvn * vh, axis=0, keepdims=True)
        dlb_ref[g:g + 1, :] += jnp.sum(dvn, axis=0, keepdims=True)
        dvh = dvn * lg
        dv = rs * (dvh - jnp.mean(dvh, axis=-1, keepdims=True) - vh * jnp.mean(dvh * vh, axis=-1, keepdims=True))
        dproj_ref[:, ucols] = (du * du_dz).astype(dproj_ref.dtype)
        dproj_ref[:, vcols] = (dv * dv_dz).astype(dproj_ref.dtype)

    kv_cur = proj_ref[:, d.attn_w:d.attn_w + 2 * BLK]
    kg = kg_ref[...]
    kb, rk, kk, vv = _band(kv_cur, kvp_ref[...], kg, lo2)
    col, distf, valid = _score_geometry(first)
    qg = qg_ref[...]
    ck = [jnp.zeros((BLK, 2 * BLK), F32) for _ in range(N_KV_HEADS)]
    cv = [jnp.zeros((BLK, 2 * BLK), F32) for _ in range(N_KV_HEADS)]
    lo_rows = lax.broadcasted_iota(jnp.int32, (BLK, 2 * BLK), 0) < HEAD_DIM
    dsk = jnp.zeros((1, BLK), F32)
    dqg = jnp.zeros((1, BLK), F32)
    for j in range(d.n_pairs):
        h0, h1 = 2 * j, 2 * j + 1
        kh = h0 // d.group
        cols = slice(BLK * j, BLK * (j + 1))
        q2 = proj_ref[:, cols]
        rq = _head_rstd(q2, lo)
        qh = q2 * rq
        qn = qh * qg
        probs, psink = _pair_probs(qn, kk[kh], col, distf, valid, d.slopes[h0], d.slopes[h1],
                                   sink_ref[0, h0], sink_ref[0, h1])
        do2 = dya[:, cols]
        prod = do2 * ya[:, cols]
        delta = (jnp.sum(jnp.where(lo, prod, 0.0), axis=-1, keepdims=True),
                 jnp.sum(jnp.where(lo, 0.0, prod), axis=-1, keepdims=True))
        do2b = do2.astype(BF16)
        dp = _dot(do2b, vv[kh], NT)
        ds = []
        for hh in (0, 1):
            ds.append(probs[hh] * (dp[:, 2 * BLK * hh:2 * BLK * (hh + 1)] - delta[hh]))
            dsink = -jnp.sum(psink[hh] * delta[hh], axis=0, keepdims=True)
            dsk = dsk + jnp.where(lane_row == (h0 + hh), dsink, 0.0)
        dsb = (jnp.concatenate(ds, axis=1) * (HEAD_DIM ** -0.5)).astype(BF16)
        pb = jnp.concatenate(probs, axis=1).astype(BF16)
        qnb = qn.astype(BF16)
        dqn = _dot(dsb, kk[kh], NN)
        dkk = _dot(qnb, dsb, TN)
        dvv = _dot(do2b, pb, TN)
        ck[kh] = ck[kh] + jnp.where(lo_rows, dkk[:, :2 * BLK], 0.0) + jnp.where(lo_rows, 0.0, dkk[:, 2 * BLK:])
        cv[kh] = cv[kh] + jnp.where(lo_rows, dvv[:, :2 * BLK], 0.0) + jnp.where(lo_rows, 0.0, dvv[:, 2 * BLK:])
        dqg = dqg + jnp.sum(dqn * qh, axis=0, keepdims=True)
        t = dqn * qg
        dq2 = rq * (t - qh * (_half_sums(t * qh, lo) * (1.0 / HEAD_DIM)))
        dproj_ref[:, cols] = dq2.astype(dproj_ref.dtype)
    dsk_ref[...] += dsk
    dqg_ref[...] += dqg
    dkn = _unplace(jnp.transpose(ck[0]), jnp.transpose(ck[1]), lo2)
    dvb = _unplace(jnp.transpose(cv[0]), jnp.transpose(cv[1]), lo2)
    khat = kb * rk
    dkg_ref[...] += jnp.sum(dkn * khat, axis=0, keepdims=True)
    t = dkn * kg
    dkb = rk * (t - khat * (_half_sums(t * khat, lo2) * (1.0 / HEAD_DIM)))
    rows_cur = pl.ds(pl.multiple_of(i * BLK, BLK), BLK)
    rows_prev = pl.ds(pl.multiple_of(jnp.maximum(i - 1, 0) * BLK, BLK), BLK)
    dkv_ref[rows_cur, :] = jnp.concatenate([dkb[BLK:], dvb[BLK:]], axis=1)
    dkv_ref[rows_prev, :] += jnp.concatenate([dkb[:BLK], dvb[:BLK]], axis=1)
    dproj_ref[:, d.attn_w:d.attn_w + 2 * BLK] = jnp.zeros((BLK, 2 * BLK), dproj_ref.dtype)


def _mixer_bwd(d, proj, ya, yg, dy, sinks, qg2, kg2, lg, lb, wsp, bcol, goa, gog):
    T = proj.shape[0]
    row, const2, const3, kv_prev = _mixer_specs(d, T)
    acc2 = lambda s: pl.BlockSpec(s, lambda i: (0, 0))
    G = d.n_groups
    out_shapes = [((T, d.in_w), BF16), ((T, 2 * BLK), F32), ((1, BLK), F32), ((1, BLK), F32), ((1, BLK), F32),
                  ((G, BLK), F32), ((G, BLK), F32), ((G, BLK, BLK), F32), ((G, BLK), F32),
                  ((1, d.attn_w), F32), ((1, d.gate_w), F32)]
    out_specs = [row(d.in_w)] + [acc2(s) for s, _ in out_shapes[1:7]] + \
                [pl.BlockSpec((G, BLK, BLK), lambda i: (0, 0, 0))] + [acc2(s) for s, _ in out_shapes[8:]]
    return pl.pallas_call(
        functools.partial(_mixer_bwd_body, d),
        name="mixer_bwd",
        grid=(T // BLK,),
        in_specs=[pl.BlockSpec(memory_space=pltpu.SMEM), row(d.in_w), kv_prev, row(d.attn_w), row(d.gate_w),
                  row(d.attn_w + d.gate_w), const2(qg2), const2(kg2), const2(lg), const2(lb), const3(wsp),
                  const3(bcol), const2(goa), const2(gog)],
        out_specs=out_specs,
        out_shape=[jax.ShapeDtypeStruct(s, t) for s, t in out_shapes],
        compiler_params=_params(("arbitrary",)),
    )(sinks, proj, proj, ya, yg, dy, qg2, kg2, lg, lb, wsp, bcol, goa, gog)


def _put_kv_body(dkv_ref, dproj_in_ref, dproj_ref):
    del dproj_in_ref
    dproj_ref[...] = dkv_ref[...].astype(dproj_ref.dtype)


def _put_kv(d, dproj, dkv):
    T = dproj.shape[0]
    tr = _pick(T, 1024, 16)
    return pl.pallas_call(
        functools.partial(_put_kv_body),
        name="put_kv",
        grid=(T // tr,),
        in_specs=[pl.BlockSpec((tr, 2 * BLK), lambda i: (i, 0)), pl.BlockSpec(memory_space=pl.ANY)],
        out_specs=pl.BlockSpec((tr, 2 * BLK), lambda i: (i, d.kv_col)),
        out_shape=jax.ShapeDtypeStruct(dproj.shape, dproj.dtype),
        input_output_aliases={1: 0},
        compiler_params=_params(("parallel",)),
    )(dkv, dproj)


def _add_pair_body(pc_ref, own_ref, got_ref, o_ref):
    del pc_ref
    o_ref[...] = (own_ref[...].astype(F32) + got_ref[...].astype(F32)).astype(o_ref.dtype)


def _add_pair(g4, got, pc, name):
    n, _, h, C = g4.shape
    tr = _pick(h, 512, 16)
    return pl.pallas_call(
        functools.partial(_add_pair_body),
        name=name,
        grid_spec=pltpu.PrefetchScalarGridSpec(
            num_scalar_prefetch=1,
            grid=(n, h // tr),
            in_specs=[pl.BlockSpec((None, None, tr, C), lambda q, i, pc: (q, pc[1], i, 0)),
                      pl.BlockSpec((None, tr, C), lambda q, i, pc: (q, i, 0))],
            out_specs=pl.BlockSpec((None, tr, C), lambda q, i, pc: (q, i, 0)),
        ),
        out_shape=jax.ShapeDtypeStruct((n, h, C), g4.dtype),
        compiler_params=_params(("parallel", "parallel")),
    )(pc, g4, got)


def _adamw_update(w, g, m, v):
    m = ADAM_B1 * m + (1.0 - ADAM_B1) * g
    v = ADAM_B2 * v + (1.0 - ADAM_B2) * (g * g)
    m_hat = m / (1.0 - ADAM_B1 ** ADAM_STEP)
    v_hat = v / (1.0 - ADAM_B2 ** ADAM_STEP)
    return -ADAM_LR * (m_hat / (jnp.sqrt(v_hat) + ADAM_EPS) + ADAM_WD * w), m, v


def _adamw_body(w_ref, g_ref, m_ref, v_ref, d_ref, nm_ref, nv_ref):
    d_ref[...], nm_ref[...], nv_ref[...] = _adamw_update(w_ref[...], g_ref[...], m_ref[...], v_ref[...])


def _adamw(w, g, m, v, name):
    R, C = w.shape
    tr = _pick(R, 512, 8)
    blk = pl.BlockSpec((tr, C), lambda i: (i, 0))
    return pl.pallas_call(
        functools.partial(_adamw_body),
        name=name,
        grid=(R // tr,),
        in_specs=[blk] * 4,
        out_specs=[blk] * 3,
        out_shape=[jax.ShapeDtypeStruct((R, C), F32)] * 3,
        compiler_params=_params(("parallel",)),
    )(w, g, m, v)


def _adamw_halves_body(pc_ref, w_ref, own_ref, got_ref, m_ref, v_ref, g_ref, d_ref, nm_ref, nv_ref):
    mine = pl.program_id(0) == pc_ref[1]

    def update(g):
        g_ref[...] = g
        d_ref[...], nm_ref[...], nv_ref[...] = _adamw_update(w_ref[...], g, m_ref[...], v_ref[...])

    @pl.when(mine)
    def _():
        update(own_ref[...].astype(F32))

    @pl.when(jnp.logical_not(mine))
    def _():
        update(got_ref[...].astype(F32))


def _adamw_halves(w, own, got, m, v, pc, name):
    h, C = own.shape
    tr = _pick(h, 512, 8)
    full = pl.BlockSpec((None, tr, C), lambda hh, i, pc: (hh, i, 0))
    mine = pl.BlockSpec((tr, C), lambda hh, i, pc: (jnp.where(hh == pc[1], i, 0), 0))
    theirs = pl.BlockSpec((tr, C), lambda hh, i, pc: (jnp.where(hh == pc[1], 0, i), 0))
    return pl.pallas_call(
        functools.partial(_adamw_halves_body),
        name=name,
        grid_spec=pltpu.PrefetchScalarGridSpec(
            num_scalar_prefetch=1,
            grid=(2, h // tr),
            in_specs=[full, mine, theirs, full, full],
            out_specs=[full] * 4,
        ),
        out_shape=[jax.ShapeDtypeStruct((2, h, C), F32)] * 4,
        compiler_params=_params(("parallel", "parallel")),
    )(pc, w.reshape(2, h, C), own, got, m.reshape(2, h, C), v.reshape(2, h, C))


def _me():
    x, y, c = lax.axis_index("x"), lax.axis_index("y"), lax.axis_index("c")
    chips = [(1 - x, y), (x, 1 - y), (1 - x, 1 - y)]
    return x, y, c, chips


def _cast_into_body(pc_ref, w_ref, o_ref):
    del pc_ref
    o_ref[...] = w_ref[...].astype(o_ref.dtype)


def _cast_into(w, pc, name, side_by_side=False):
    Rs, C = w.shape
    h = Rs // 2
    tr = _pick(h, 512, 16)
    if side_by_side:
        out_spec = pl.BlockSpec((None, tr, C), lambda hh, i, pc: (hh, i, pc[0]))
        out_shape = jax.ShapeDtypeStruct((2, h, N_CHIPS * C), BF16)
    else:
        out_spec = pl.BlockSpec((None, None, tr, C), lambda hh, i, pc: (pc[0], hh, i, 0))
        out_shape = jax.ShapeDtypeStruct((N_CHIPS, 2, h, C), BF16)
    return pl.pallas_call(
        functools.partial(_cast_into_body),
        name=name,
        grid_spec=pltpu.PrefetchScalarGridSpec(
            num_scalar_prefetch=1,
            grid=(2, h // tr),
            in_specs=[pl.BlockSpec((None, tr, C), lambda hh, i, pc: (hh, i, 0))],
            out_specs=out_spec,
        ),
        out_shape=out_shape,
        compiler_params=_params(("parallel", "parallel")),
    )(pc, w.reshape(2, h, C))


MAX_PIECES = 4


def _send_tile_to_sibling(src_of, dst_of, tr, dst_total, send_sems, recv_sem, last):
    x, y, c, _ = _me()
    pieces = MAX_PIECES if tr % (16 * MAX_PIECES) == 0 else (2 if tr % 32 == 0 else 1)
    n = tr // pieces
    copies = [pltpu.make_async_remote_copy(src_ref=src_of(k * n, n), dst_ref=dst_of(k * n, n), send_sem=send_sems.at[k],
                                           recv_sem=recv_sem, device_id=(x, y, 1 - c), device_id_type=MESH)
              for k in range(pieces)]
    for cp in copies:
        cp.start()
    for cp in copies:
        cp.wait_send()

    @pl.when(last)
    def _():
        pltpu.make_async_remote_copy(src_ref=dst_total, dst_ref=dst_total, send_sem=send_sems.at[0], recv_sem=recv_sem,
                                     device_id=(x, y, 1 - c), device_id_type=MESH).wait_recv()


TILE_SEMS = [pltpu.SemaphoreType.DMA((MAX_PIECES,)), pltpu.SemaphoreType.DMA(())]


def _ag_pair_body(tr, n_i, pc_ref, tile_ref, buf_ref, send_sem, recv_sem):
    j, i = pl.program_id(0), pl.program_id(1)
    q = pc_ref[0] ^ (j + 1)
    c = pc_ref[1]
    r_tile = pl.multiple_of(i * tr, tr)
    last = jnp.logical_and(j == N_CHIPS - 2, i == n_i - 1)
    if len(buf_ref.shape) == 4:
        _send_tile_to_sibling(lambda r0, n: tile_ref.at[:, :, pl.ds(r0, n)],
                              lambda r0, n: buf_ref.at[pl.ds(q, 1), pl.ds(c, 1), pl.ds(r_tile + r0, n)], tr,
                              buf_ref.at[pl.ds(0, N_CHIPS - 1), 0], send_sem, recv_sem, last)
    else:
        cs = buf_ref.shape[2] // N_CHIPS
        cols = pl.ds(pl.multiple_of(q * cs, BLK), cs)
        _send_tile_to_sibling(lambda r0, n: tile_ref.at[:, pl.ds(r0, n)],
                              lambda r0, n: buf_ref.at[pl.ds(c, 1), pl.ds(r_tile + r0, n), cols], tr,
                              buf_ref.at[0, :, pl.ds(0, (N_CHIPS - 1) * cs)], send_sem, recv_sem, last)


def _ag_pair(buf, pc, name):
    if len(buf.shape) == 4:
        _, _, h, C = buf.shape
        tile = lambda tr: pl.BlockSpec((1, 1, tr, C), lambda j, i, pc: (pc[0] ^ (j + 1), pc[1], i, 0))
    else:
        _, h, C = buf.shape
        tile = lambda tr: pl.BlockSpec((1, tr, C // N_CHIPS), lambda j, i, pc: (pc[1], i, pc[0] ^ (j + 1)))
    tr = _pick(h, 512, 16)
    return pl.pallas_call(
        functools.partial(_ag_pair_body, tr, h // tr),
        name=name,
        grid_spec=pltpu.PrefetchScalarGridSpec(
            num_scalar_prefetch=1,
            grid=(N_CHIPS - 1, h // tr),
            in_specs=[tile(tr)],
            out_specs=HBM,
            scratch_shapes=TILE_SEMS,
        ),
        out_shape=jax.ShapeDtypeStruct(buf.shape, buf.dtype),
        input_output_aliases={1: 0},
        compiler_params=_params(("arbitrary", "arbitrary")),
    )(pc, buf)


def _swap_halves_body(tr, n_q, n_i, pc_ref, tile_ref, got_ref, send_sem, recv_sem):
    del pc_ref
    q, i = pl.program_id(0), pl.program_id(1)
    r_tile = pl.multiple_of(i * tr, tr)
    _send_tile_to_sibling(lambda r0, n: tile_ref.at[:, :, pl.ds(r0, n)],
                          lambda r0, n: got_ref.at[pl.ds(q, 1), :, pl.ds(r_tile + r0, n)], tr, got_ref, send_sem, recv_sem,
                          jnp.logical_and(q == n_q - 1, i == n_i - 1))


def _swap_halves(g4, pc, name):
    n, _, h, C = g4.shape
    tr = _pick(h, 512, 16)
    return pl.pallas_call(
        functools.partial(_swap_halves_body, tr, n, h // tr),
        name=name,
        grid_spec=pltpu.PrefetchScalarGridSpec(
            num_scalar_prefetch=1,
            grid=(n, h // tr),
            in_specs=[pl.BlockSpec((1, 1, tr, C), lambda q, i, pc: (q, 1 - pc[1], i, 0))],
            out_specs=HBM,
            scratch_shapes=TILE_SEMS,
        ),
        out_shape=jax.ShapeDtypeStruct((n, 1, h, C), g4.dtype),
        compiler_params=_params(("arbitrary", "arbitrary")),
    )(pc, g4).reshape(n, h, C)


def _ici_copy(src, dst, send_sems, recv_sems, j, chip, c):
    return pltpu.make_async_remote_copy(src_ref=src, dst_ref=dst, send_sem=send_sems.at[j], recv_sem=recv_sems.at[j],
                                        device_id=(chip[0], chip[1], c), device_id_type=MESH)


def _token_spec():
    return jax.ShapeDtypeStruct((8, BLK), F32), pl.BlockSpec(memory_space=pltpu.VMEM)


def _slab(buf_ref, q, c):
    if len(buf_ref.shape) == 4:
        return buf_ref.at[q, c]
    cs = buf_ref.shape[2] // N_CHIPS
    return buf_ref.at[c, :, pl.ds(pl.multiple_of(q * cs, BLK), cs)]


def _ag_start_body(both_cores, buf_ref, after_ref, send_sems, recv_sems, buf_thru, token_ref):
    del after_ref, buf_thru
    x, y, c, chips = _me()
    mine = _slab(buf_ref, 2 * x + y, c)
    for j, chip in enumerate(chips):
        _ici_copy(mine, mine, send_sems, recv_sems, j, chip, c).start()
    if both_cores:
        for j, chip in enumerate(chips):
            _ici_copy(mine, mine, send_sems, recv_sems, N_CHIPS - 1 + j, chip, 1 - c).start()
    token_ref[...] = jnp.zeros_like(token_ref)


def _ag_start(buf, after, name, both_cores=False):
    tok_shape, tok_spec = _token_spec()
    sems = pltpu.SemaphoreType.DMA(((N_CHIPS - 1) * (2 if both_cores else 1),))
    return pl.pallas_call(
        functools.partial(_ag_start_body, both_cores),
        name=name,
        in_specs=[HBM, ANY],
        out_specs=[SEM, SEM, HBM, tok_spec],
        out_shape=[sems, sems, pltpu.HBM(buf.shape, buf.dtype), tok_shape],
        input_output_aliases={0: 2},
        compiler_params=pltpu.CompilerParams(has_side_effects=EFFECT),
    )(pltpu.with_memory_space_constraint(buf, pltpu.HBM), after)


def _ag_wait_body(both_cores, buf_ref, send_sems, recv_sems, after_ref, buf_out):
    del after_ref, buf_out
    x, y, c, chips = _me()
    mine = _slab(buf_ref, 2 * x + y, c)
    for j, chip in enumerate(chips):
        theirs = _slab(buf_ref, 2 * chip[0] + chip[1], c)
        _ici_copy(mine, mine, send_sems, recv_sems, j, chip, c).wait_send()
        _ici_copy(theirs, theirs, send_sems, recv_sems, j, chip, c).wait_recv()
    if both_cores:
        for j, chip in enumerate(chips):
            theirs = _slab(buf_ref, 2 * chip[0] + chip[1], 1 - c)
            _ici_copy(mine, mine, send_sems, recv_sems, N_CHIPS - 1 + j, chip, 1 - c).wait_send()
            _ici_copy(theirs, theirs, send_sems, recv_sems, N_CHIPS - 1 + j, chip, 1 - c).wait_recv()


def _ag_wait(buf, send_sems, recv_sems, after, name, both_cores=False):
    return pl.pallas_call(
        functools.partial(_ag_wait_body, both_cores),
        name=name,
        in_specs=[HBM, SEM, SEM, ANY],
        out_specs=HBM,
        out_shape=pltpu.HBM(buf.shape, buf.dtype),
        input_output_aliases={0: 0},
        compiler_params=pltpu.CompilerParams(has_side_effects=EFFECT),
    )(buf, send_sems, recv_sems, after)


def _rs_start_body(pair_ref, land_ref, after_ref, send_sems, recv_sems, pair_thru, land_thru, token_ref):
    del after_ref, pair_thru, land_thru
    x, y, c, chips = _me()
    for j, chip in enumerate(chips):
        _ici_copy(pair_ref.at[2 * chip[0] + chip[1]], land_ref.at[j], send_sems, recv_sems, j, chip, c).start()
    token_ref[...] = jnp.zeros_like(token_ref)


def _rs_start(pair, after, name):
    n, h, C = pair.shape
    tok_shape, tok_spec = _token_spec()
    sems = pltpu.SemaphoreType.DMA((N_CHIPS - 1,))
    land = pltpu.with_memory_space_constraint(lax.empty((N_CHIPS - 1, h, C), pair.dtype), pltpu.HBM)
    return pl.pallas_call(
        functools.partial(_rs_start_body),
        name=name,
        in_specs=[HBM, HBM, ANY],
        out_specs=[SEM, SEM, HBM, HBM, tok_spec],
        out_shape=[sems, sems, pltpu.HBM(pair.shape, pair.dtype), pltpu.HBM(land.shape, land.dtype), tok_shape],
        input_output_aliases={0: 2, 1: 3},
        compiler_params=pltpu.CompilerParams(has_side_effects=EFFECT),
    )(pltpu.with_memory_space_constraint(pair, pltpu.HBM), land, after)


def _rs_wait_body(pair_ref, land_ref, send_sems, recv_sems, after_ref, pair_out, land_out):
    del after_ref, pair_out, land_out
    x, y, c, chips = _me()
    for j, chip in enumerate(chips):
        _ici_copy(pair_ref.at[0], land_ref.at[j], send_sems, recv_sems, j, chip, c).wait_send()
        _ici_copy(pair_ref.at[0], land_ref.at[j], send_sems, recv_sems, j, chip, c).wait_recv()


def _rs_wait(pair, land, send_sems, recv_sems, after, name):
    return pl.pallas_call(
        functools.partial(_rs_wait_body),
        name=name,
        in_specs=[HBM, HBM, SEM, SEM, ANY],
        out_specs=[HBM, HBM],
        out_shape=[pltpu.HBM(pair.shape, pair.dtype), pltpu.HBM(land.shape, land.dtype)],
        input_output_aliases={0: 0, 1: 1},
        compiler_params=pltpu.CompilerParams(has_side_effects=EFFECT),
    )(pair, land, send_sems, recv_sems, after)


def _swap_copy(g4_ref, got_ref, send_sem, recv_sem):
    x, y, c, _ = _me()
    return pltpu.make_async_remote_copy(src_ref=g4_ref.at[:, 1 - c], dst_ref=got_ref, send_sem=send_sem,
                                        recv_sem=recv_sem, device_id=(x, y, 1 - c), device_id_type=MESH)


def _swap_start_body(g4_ref, got_ref, send_sem, recv_sem, g4_thru, got_thru, token_ref):
    del g4_thru, got_thru
    _swap_copy(g4_ref, got_ref, send_sem, recv_sem).start()
    token_ref[...] = jnp.zeros_like(token_ref)


def _swap_start(g4, name):
    n, _, h, C = g4.shape
    tok_shape, tok_spec = _token_spec()
    sem = pltpu.SemaphoreType.DMA(())
    got = pltpu.with_memory_space_constraint(lax.empty((n, h, C), g4.dtype), pltpu.HBM)
    return pl.pallas_call(
        functools.partial(_swap_start_body),
        name=name,
        in_specs=[HBM, HBM],
        out_specs=[SEM, SEM, HBM, HBM, tok_spec],
        out_shape=[sem, sem, pltpu.HBM(g4.shape, g4.dtype), pltpu.HBM(got.shape, got.dtype), tok_shape],
        input_output_aliases={0: 2, 1: 3},
        compiler_params=pltpu.CompilerParams(has_side_effects=EFFECT),
    )(pltpu.with_memory_space_constraint(g4, pltpu.HBM), got)


def _swap_wait_body(g4_ref, got_ref, send_sem, recv_sem, after_ref, g4_out, got_out):
    del after_ref, g4_out, got_out
    cp = _swap_copy(g4_ref, got_ref, send_sem, recv_sem)
    cp.wait_send()
    cp.wait_recv()


def _swap_wait(g4, got, send_sem, recv_sem, after, name):
    return pl.pallas_call(
        functools.partial(_swap_wait_body),
        name=name,
        in_specs=[HBM, HBM, SEM, SEM, ANY],
        out_specs=[HBM, HBM],
        out_shape=[pltpu.HBM(g4.shape, g4.dtype), pltpu.HBM(got.shape, got.dtype)],
        input_output_aliases={0: 0, 1: 1},
        compiler_params=pltpu.CompilerParams(has_side_effects=EFFECT),
    )(g4, got, send_sem, recv_sem, after)


def _add_chips_body(tr, n_i, pc_ref, own_ref, l0_ref, l1_ref, l2_ref, o_ref, got_ref, send_sems, recv_sem):
    del pc_ref
    i = pl.program_id(0)
    r = own_ref[...].astype(F32) + l0_ref[...].astype(F32)
    o_ref[...] = (r + l1_ref[...].astype(F32) + l2_ref[...].astype(F32)).astype(o_ref.dtype)
    r_tile = pl.multiple_of(i * tr, tr)
    _send_tile_to_sibling(lambda r0, n: o_ref.at[pl.ds(r0, n)], lambda r0, n: got_ref.at[pl.ds(r_tile + r0, n)], tr,
                          got_ref, send_sems, recv_sem, i == n_i - 1)


def _add_chips(pair, land, pc, name):
    _, h, C = pair.shape
    tr = _pick(h, 256, 16)
    slot = lambda j: pl.BlockSpec((None, tr, C), lambda i, pc: (j, i, 0))
    return pl.pallas_call(
        functools.partial(_add_chips_body, tr, h // tr),
        name=name,
        grid_spec=pltpu.PrefetchScalarGridSpec(
            num_scalar_prefetch=1,
            grid=(h // tr,),
            in_specs=[pl.BlockSpec((None, tr, C), lambda i, pc: (pc[0], i, 0)), slot(0), slot(1), slot(2)],
            out_specs=[pl.BlockSpec((tr, C), lambda i, pc: (i, 0)), HBM],
            scratch_shapes=TILE_SEMS,
        ),
        out_shape=[jax.ShapeDtypeStruct((h, C), pair.dtype), jax.ShapeDtypeStruct((h, C), pair.dtype)],
        compiler_params=_params(("arbitrary",)),
    )(pc, pair, land, land, land)


def _peer(r):
    x, y, c, _ = _me()
    return (x ^ ((r >> 2) & 1), y ^ ((r >> 1) & 1), c ^ (r & 1))


def _ar_start_body(x_ref, land_ref, send_sems, recv_sems, x_thru, land_thru, token_ref):
    del x_thru, land_thru
    for r in range(1, N_DEV):
        pltpu.make_async_remote_copy(src_ref=x_ref, dst_ref=land_ref.at[r - 1], send_sem=send_sems.at[r - 1],
                                     recv_sem=recv_sems.at[r - 1], device_id=_peer(r), device_id_type=MESH).start()
    token_ref[...] = jnp.zeros_like(token_ref)


def _ar_start(packed):
    tok_shape, tok_spec = _token_spec()
    sems = pltpu.SemaphoreType.DMA((N_DEV - 1,))
    land = pltpu.with_memory_space_constraint(lax.empty((N_DEV - 1,) + packed.shape, packed.dtype), pltpu.HBM)
    return pl.pallas_call(
        functools.partial(_ar_start_body),
        name="ar_start",
        in_specs=[HBM, HBM],
        out_specs=[SEM, SEM, HBM, HBM, tok_spec],
        out_shape=[sems, sems, pltpu.HBM(packed.shape, packed.dtype), pltpu.HBM(land.shape, land.dtype), tok_shape],
        input_output_aliases={0: 2, 1: 3},
        compiler_params=pltpu.CompilerParams(has_side_effects=EFFECT),
    )(pltpu.with_memory_space_constraint(packed, pltpu.HBM), land)


def _ar_wait_body(x_ref, land_ref, send_sems, recv_sems, after_ref, x_out, land_out):
    del after_ref, x_out, land_out
    for r in range(1, N_DEV):
        cp = pltpu.make_async_remote_copy(src_ref=x_ref, dst_ref=land_ref.at[r - 1], send_sem=send_sems.at[r - 1],
                                          recv_sem=recv_sems.at[r - 1], device_id=_peer(r), device_id_type=MESH)
        cp.wait_send()
        cp.wait_recv()


def _ar_wait(packed, land, send_sems, recv_sems, after):
    return pl.pallas_call(
        functools.partial(_ar_wait_body),
        name="ar_wait",
        in_specs=[HBM, HBM, SEM, SEM, ANY],
        out_specs=[HBM, HBM],
        out_shape=[pltpu.HBM(packed.shape, packed.dtype), pltpu.HBM(land.shape, land.dtype)],
        input_output_aliases={0: 0, 1: 1},
        compiler_params=pltpu.CompilerParams(has_side_effects=EFFECT),
    )(packed, land, send_sems, recv_sems, after)


def _ar_sum_body(me_ref, own_ref, *rest):
    o_ref = rest[N_DEV]
    acc = None
    for dev in range(N_DEV):
        term = jnp.where(me_ref[0] == dev, own_ref[...], rest[dev][...])
        acc = term if acc is None else acc + term
    o_ref[...] = acc


def _ar_sum(packed, land, me):
    R, C = packed.shape
    tr = _pick(R, 552, 8)
    own = pl.BlockSpec((tr, C), lambda i, me: (i, 0))
    slot = lambda dev: pl.BlockSpec((None, tr, C), lambda i, me: (jnp.maximum((dev ^ me[0]) - 1, 0), i, 0))
    return pl.pallas_call(
        functools.partial(_ar_sum_body),
        name="ar_sum",
        grid_spec=pltpu.PrefetchScalarGridSpec(
            num_scalar_prefetch=1,
            grid=(R // tr,),
            in_specs=[own] + [slot(dev) for dev in range(N_DEV)],
            out_specs=pl.BlockSpec((tr, C), lambda i, me: (i, 0)),
        ),
        out_shape=jax.ShapeDtypeStruct((R, C), F32),
        compiler_params=_params(("parallel",)),
    )(me, packed, *([land] * N_DEV))


def _pack(arrays):
    rows = []
    for a in arrays:
        flat = a.reshape(-1).astype(F32)
        pad = (-flat.shape[0]) % BLK
        rows.append(jnp.pad(flat, (0, pad)).reshape(-1, BLK))
    packed = jnp.concatenate(rows, axis=0)
    pad = (-packed.shape[0]) % 8
    return jnp.pad(packed, ((0, pad), (0, 0)))


def _unpack(packed, shapes):
    out, r = [], 0
    for s in shapes:
        n = 1
        for k in s:
            n *= k
        nr = -(-n // BLK)
        out.append(packed[r:r + nr].reshape(-1)[:n].reshape(s))
        r += nr
    return out


def kernel(x, norm1_g, w_in, q_norm_g, k_norm_g, attn_sinks, gate_ln_g, gate_ln_b, w_spatial, b_spatial, out_norm_attn_g, out_norm_gate_g, w_out, norm2_g, w_ffn_gate, w_ffn_up, w_ffn_down, loss_target, m_norm1_g, m_w_in, m_q_norm_g, m_k_norm_g, m_attn_sinks, m_gate_ln_g, m_gate_ln_b, m_w_spatial, m_b_spatial, m_out_norm_attn_g, m_out_norm_gate_g, m_w_out, m_norm2_g, m_w_ffn_gate, m_w_ffn_up, m_w_ffn_down, v_norm1_g, v_w_in, v_q_norm_g, v_k_norm_g, v_attn_sinks, v_gate_ln_g, v_gate_ln_b, v_w_spatial, v_b_spatial, v_out_norm_attn_g, v_out_norm_gate_g, v_w_out, v_norm2_g, v_w_ffn_gate, v_w_ffn_up, v_w_ffn_down):
    bl, seq, D = x.shape
    T = bl * seq
    attn_w, gate_w = out_norm_attn_g.shape[1], out_norm_gate_g.shape[1]
    d = _Dims(seq, attn_w, gate_w)
    G = d.n_groups
    in_w = d.in_w
    slab = w_ffn_gate.shape[2]
    dff = slab * N_CHIPS
    assert w_in.shape[2] * N_CHIPS == in_w and seq % BLK == 0 and attn_w % (2 * BLK) == 0

    pc = jnp.stack([2 * lax.axis_index("x") + lax.axis_index("y"), lax.axis_index("c")]).astype(jnp.int32)
    big = [w_in[0], w_out[0], w_ffn_gate[0], w_ffn_up[0], w_ffn_down[0]]
    names = ["in", "out", "gate", "up", "down"]
    xf = x.reshape(T, D)
    tgt = loss_target.reshape(T, D)
    send, recv, buf, behind = _ag_start(_cast_into(big[0], pc, "cast_in"), norm1_g, "ag_start_in")
    started = [(send, recv, buf)]
    h1 = _rms_fwd(xf, norm1_g, "norm1_fwd", after=behind)
    behind = h1
    for w, n in zip(big[1:], names[1:]):
        buf = _cast_into(w, pc, "cast_" + n, side_by_side=n in ("gate", "up"))
        send, recv, buf, behind = _ag_start(buf, behind, "ag_start_" + n, both_cores=n == "down")
        started.append((send, recv, buf))

    def gathered(k, after):
        send, recv, buf = started[k]
        direct = names[k] == "down"
        buf = _ag_wait(buf, send, recv, after, "ag_wait_" + names[k], both_cores=direct)
        if not direct:
            buf = _ag_pair(buf, pc, "ag_pair_" + names[k])
        rs, cs = big[k].shape
        return buf.reshape(rs, N_CHIPS * cs) if len(buf.shape) == 3 else buf.reshape(N_CHIPS, rs, cs)

    qg2 = jnp.tile(q_norm_g, (1, 2))
    kg2 = jnp.tile(k_norm_g, (1, 2))
    lg, lb, wsp = gate_ln_g[0], gate_ln_b[0], w_spatial[0]
    bcol = jnp.broadcast_to(b_spatial[0][:, :, None], (G, BLK, BLK))

    win_full = jnp.transpose(gathered(0, behind), (1, 0, 2)).reshape(D, in_w)
    proj = _matmul(h1, win_full, "nn", F32, "proj_fwd", tm=MM_TILE // 2, tn=in_w)
    ya, yg, yn = _mixer_fwd(d, proj, attn_sinks, qg2, kg2, lg, lb, wsp, bcol, out_norm_attn_g, out_norm_gate_g)
    wout_full = gathered(1, yn).reshape(attn_w + gate_w, D)
    x1 = _matmul(yn, wout_full, "nn", F32, "out_fwd", tm=MM_TILE, tn=MM_TILE, add=xf)
    h2 = _rms_fwd(x1, norm2_g, "norm2_fwd")
    wg_full, wu_full = gathered(2, h2), gathered(3, h2)
    a, b, f = _ffn_up(h2, wg_full, wu_full)
    wd_full = gathered(4, f).reshape(dff, D)
    dx2, dx2b, loss_local = _ffn_down_loss(f, wd_full, x1, tgt)

    def swap_start(g, n):
        g4 = g.reshape(N_CHIPS, 2, g.shape[1] // 2, g.shape[2])
        return _swap_start(g4, "rs_swap_start_" + n)

    def reduce_start(swapping, n, after):
        send, recv, g4, got, _ = swapping
        g4, got = _swap_wait(g4, got, send, recv, after, "rs_swap_wait_" + n)
        return _rs_start(_add_pair(g4, got, pc, "rs_add_pair_" + n), got, "rs_start_" + n)

    reducing = {}
    g_d = _matmul(f, dx2b, "tn", BF16, "ffn_down_dw", tm=slab, tn=MM_TILE, out_slab="r")
    swap_d = swap_start(g_d, "down")
    da, db = _ffn_down_dx(dx2b, wd_full, a, b, swap_d[4])
    g_g = _matmul(h2, da, "tn", BF16, "ffn_gate_dw", tm=MM_TILE, tn=slab, out_slab="c")
    swap_g = swap_start(g_g, "gate")
    reducing["down"] = reduce_start(swap_d, "down", swap_g[4])
    g_u = _matmul(h2, db, "tn", BF16, "ffn_up_dw", tm=MM_TILE, tn=slab, out_slab="c", after=reducing["down"][4])
    swap_u = swap_start(g_u, "up")
    reducing["gate"] = reduce_start(swap_g, "gate", swap_u[4])
    dh2 = _matmul(da, wg_full, "nt", F32, "ffn_gate_dx", tm=MM_TILE, tn=MM_TILE, tk=dff // 2,
                  after=reducing["gate"][4])
    dh2 = _matmul(db, wu_full, "nt", F32, "ffn_up_dx", tm=MM_TILE, tn=MM_TILE, tk=dff // 2, add=dh2)
    reducing["up"] = reduce_start(swap_u, "up", dh2)
    dx1, dx1b, dg_norm2 = _rms_bwd(x1, norm2_g, dh2, dx2, "norm2_bwd", True)
    g_o = _matmul(yn, dx1b, "tn", BF16, "out_dw", tm=MM_TILE // 2, tn=MM_TILE, out_slab="r",
                  after=reducing["up"][4])
    swap_o = swap_start(g_o, "out")
    dy = _matmul(dx1b, wout_full, "nt", F32, "out_dx", tm=MM_TILE, tn=MM_TILE, after=swap_o[4])
    (dproj, dkv, dqg, dkg, dsk, dlg, dlb, dwsp, dbsp, dgoa, dgog) = _mixer_bwd(
        d, proj, ya, yg, dy, attn_sinks, qg2, kg2, lg, lb, wsp, bcol, out_norm_attn_g, out_norm_gate_g)
    dproj = _put_kv(d, dproj, dkv)
    reducing["out"] = reduce_start(swap_o, "out", dproj)
    g_in_full = _matmul(h1, dproj, "tn", BF16, "proj_dw", tm=MM_TILE // 2, tn=in_w, tk=T // 2,
                        after=reducing["out"][4])
    g_i = jnp.transpose(g_in_full.reshape(D, N_CHIPS, in_w // N_CHIPS), (1, 0, 2))
    g4_i = g_i.reshape(N_CHIPS, 2, D // 2, in_w // N_CHIPS)
    pair_i = _add_pair(g4_i, _swap_halves(g4_i, pc, "rs_swap_in"), pc, "rs_add_pair_in")
    reducing["in"] = _rs_start(pair_i, g_i, "rs_start_in")
    dh1 = _matmul(dproj, win_full, "nt", F32, "proj_dx", tm=MM_TILE, tn=MM_TILE, after=reducing["in"][4])
    dx, dg_norm1 = _rms_bwd(xf, norm1_g, dh1, dx1, "norm1_bwd", False)

    dqg64 = dqg[:, :HEAD_DIM] + dqg[:, HEAD_DIM:]
    dkg64 = dkg[:, :HEAD_DIM] + dkg[:, HEAD_DIM:]
    small_g_local = [dg_norm1, dqg64, dkg64, dsk[:, :d.n_heads], dlg, dlb, dwsp, dbsp, dgoa, dgog, dg_norm2,
                     loss_local]
    ar_send, ar_recv, ar_own, ar_land, ar_token = _ar_start(_pack(small_g_local))

    big_m = [m_w_in[0], m_w_out[0], m_w_ffn_gate[0], m_w_ffn_up[0], m_w_ffn_down[0]]
    big_v = [v_w_in[0], v_w_out[0], v_w_ffn_gate[0], v_w_ffn_up[0], v_w_ffn_down[0]]
    big_grads, big_d, big_nm, big_nv = [], [], [], []
    for w, m, v, n in zip(big, big_m, big_v, names):
        send, recv, pair, land, _ = reducing[n]
        pair, land = _rs_wait(pair, land, send, recv, ar_token, "rs_wait_" + n)
        own, got = _add_chips(pair, land, pc, "rs_add_chips_" + n)
        outs = _adamw_halves(w, own, got, m, v, pc, "adamw_" + n)
        for lst, o in zip((big_grads, big_d, big_nm, big_nv), outs):
            lst.append(o.reshape(w.shape))

    small_names_w = [norm1_g, q_norm_g, k_norm_g, attn_sinks, gate_ln_g, gate_ln_b, w_spatial, b_spatial,
                     out_norm_attn_g, out_norm_gate_g, norm2_g]
    small_m = [m_norm1_g, m_q_norm_g, m_k_norm_g, m_attn_sinks, m_gate_ln_g, m_gate_ln_b, m_w_spatial, m_b_spatial,
               m_out_norm_attn_g, m_out_norm_gate_g, m_norm2_g]
    small_v = [v_norm1_g, v_q_norm_g, v_k_norm_g, v_attn_sinks, v_gate_ln_g, v_gate_ln_b, v_w_spatial, v_b_spatial,
               v_out_norm_attn_g, v_out_norm_gate_g, v_norm2_g]
    shapes = [w.shape for w in small_names_w] + [loss_local.shape]
    ride = [jnp.zeros(loss_local.shape, F32)]
    ar_own, ar_land = _ar_wait(ar_own, ar_land, ar_send, ar_recv, big_nv[-1])
    me = (4 * lax.axis_index("x") + 2 * lax.axis_index("y") + lax.axis_index("c")).astype(jnp.int32).reshape(1)
    sg = _ar_sum(ar_own, ar_land, me)
    sd, snm, snv = _adamw(_pack(small_names_w + ride), sg, _pack(small_m + ride), _pack(small_v + ride), "adamw_small")
    small_g, small_d, small_nm, small_nv = (_unpack(t, shapes) for t in (sg, sd, snm, snv))
    loss = small_g[-1][0, 0]

    def order(small, bigs):
        s = list(small)
        bg = [t[None] for t in bigs]
        return [s[0], bg[0], s[1], s[2], s[3], s[4], s[5], s[6], s[7], s[8], s[9], bg[1], s[10], bg[2], bg[3], bg[4]]

    grad_x = dx.reshape(bl, seq, D)
    return (loss, grad_x, *order(small_g, big_grads), *order(small_d, big_d), *order(small_nm, big_nm),
            *order(small_nv, big_nv))
```

```python
import functools

import jax
import jax.numpy as jnp
from jax import lax
from jax.experimental import pallas as pl
from jax.experimental.pallas import tpu as pltpu

F32 = jnp.float32
BF16 = jnp.bfloat16
MESH = pl.DeviceIdType.MESH

EPS = 1e-6
HEAD_DIM = 64
N_KV_HEADS = 2
BLK = 128
N_CHIPS = 4
N_DEV = 8
NEG = -1e30

ADAM_LR = 0.001
ADAM_B1 = 0.9
ADAM_B2 = 0.999
ADAM_EPS = 1e-08
ADAM_WD = 0.01
ADAM_STEP = 10

VMEM_LIMIT = 56 * 1024 * 1024

NN = (((1,), (0,)), ((), ()))
NT = (((1,), (1,)), ((), ()))
TN = (((0,), (0,)), ((), ()))
HBM = pl.BlockSpec(memory_space=pltpu.HBM)
ANY = pl.BlockSpec(memory_space=pl.ANY)
SEM = pl.BlockSpec(memory_space=pltpu.SEMAPHORE)
EFFECT = pltpu.SideEffectType.DATAFLOW_SIDE_EFFECTING


def _dot(a, b, dn):
    return lax.dot_general(a, b, dn, preferred_element_type=F32)


def _pick(dim, pref, align=128):
    if dim <= pref:
        return dim
    t = (pref // align) * align
    while t >= align:
        if dim % t == 0:
            return t
        t -= align
    return dim


def _params(sem):
    return pltpu.CompilerParams(dimension_semantics=sem, vmem_limit_bytes=VMEM_LIMIT)


MM_CHUNK = 512
MM_TILE = 1024


def _col_chunks(tn):
    return [slice(c0, min(c0 + MM_CHUNK, tn)) for c0 in range(0, tn, MM_CHUNK)]


def _mm_body(dn, nk, has_add, has_after, *refs):
    a_ref, b_ref = refs[:2]
    add_ref = refs[2] if has_add else None
    o_ref = refs[2 + has_add + has_after]
    chunks = _col_chunks(o_ref.shape[-1])

    def dot(cols):
        return _dot(a_ref[...], b_ref[cols, :] if dn == NT else b_ref[:, cols], dn)

    def finish(cols, r):
        if add_ref is not None:
            r = r + add_ref[:, cols]
        o_ref[:, cols] = r.astype(o_ref.dtype)

    if nk == 1:
        for cols in chunks:
            finish(cols, dot(cols))
        return
    acc_ref = refs[-1]
    k = pl.program_id(2)

    @pl.when(k == 0)
    def _():
        for cols in chunks:
            acc_ref[:, cols] = dot(cols)

    if nk > 2:
        @pl.when(jnp.logical_and(k > 0, k < nk - 1))
        def _():
            for cols in chunks:
                acc_ref[:, cols] += dot(cols)

    @pl.when(k == nk - 1)
    def _():
        for cols in chunks:
            finish(cols, acc_ref[:, cols] + dot(cols))


def _matmul(a, b, mode, out_dtype, name, *, tm, tn, tk=None, add=None, out_slab=None, after=None):
    if mode == "nn":
        (M, K), N = a.shape, b.shape[1]
    elif mode == "nt":
        (M, K), N = a.shape, b.shape[0]
    else:
        (K, M), N = a.shape, b.shape[1]
    tk = K if tk is None else tk
    tm, tn, tk = _pick(M, tm), _pick(N, tn), _pick(K, tk)
    if out_slab == "c":
        tn = _pick(N // N_CHIPS, tn)
    if out_slab == "r":
        tm = _pick(M // N_CHIPS, tm)
    gm, gn, gk = M // tm, N // tn, K // tk

    if mode == "tn":
        a_spec = pl.BlockSpec((tk, tm), lambda j, i, k: (k, i))
        b_spec = pl.BlockSpec((tk, tn), lambda j, i, k: (k, j))
    else:
        a_spec = pl.BlockSpec((tm, tk), lambda j, i, k: (i, k))
        if mode == "nn":
            b_spec = pl.BlockSpec((tk, tn), lambda j, i, k: (k, j))
        else:
            b_spec = pl.BlockSpec((tn, tk), lambda j, i, k: (j, k))

    if out_slab == "c":
        per = (N // N_CHIPS) // tn
        o_spec = pl.BlockSpec((None, tm, tn), lambda j, i, k: (j // per, i, j % per))
        o_shape = jax.ShapeDtypeStruct((N_CHIPS, M, N // N_CHIPS), out_dtype)
    elif out_slab == "r":
        per = (M // N_CHIPS) // tm
        o_spec = pl.BlockSpec((None, tm, tn), lambda j, i, k: (i // per, i % per, j))
        o_shape = jax.ShapeDtypeStruct((N_CHIPS, M // N_CHIPS, N), out_dtype)
    else:
        o_spec = pl.BlockSpec((tm, tn), lambda j, i, k: (i, j))
        o_shape = jax.ShapeDtypeStruct((M, N), out_dtype)

    dn = {"nn": NN, "nt": NT, "tn": TN}[mode]
    in_specs = [a_spec, b_spec]
    args = [a, b]
    if add is not None:
        in_specs.append(pl.BlockSpec((tm, tn), lambda j, i, k: (i, j)))
        args.append(add)
    if after is not None:
        in_specs.append(ANY)
        args.append(after)
    return pl.pallas_call(
        functools.partial(_mm_body, dn, gk, add is not None, after is not None),
        name=name,
        grid=(gn, gm, gk),
        in_specs=in_specs,
        out_specs=o_spec,
        out_shape=o_shape,
        scratch_shapes=[pltpu.VMEM((tm, tn), F32)] if gk > 1 else [],
        compiler_params=_params(("parallel", "parallel", "arbitrary")),
    )(*args)


def _rms_fwd_body(x_ref, g_ref, *rest):
    h_ref = rest[-1]
    x = x_ref[...]
    r = lax.rsqrt(jnp.mean(x * x, axis=-1, keepdims=True) + EPS)
    h_ref[...] = (x * r * g_ref[...]).astype(h_ref.dtype)


def _rms_fwd(x, g, name, after=None):
    T, D = x.shape
    tr = _pick(T, 512, 16)
    extra = [] if after is None else [after]
    return pl.pallas_call(
        functools.partial(_rms_fwd_body),
        name=name,
        grid=(T // tr,),
        in_specs=[pl.BlockSpec((tr, D), lambda i: (i, 0)), pl.BlockSpec((1, D), lambda i: (0, 0))] + [ANY] * len(extra),
        out_specs=pl.BlockSpec((tr, D), lambda i: (i, 0)),
        out_shape=jax.ShapeDtypeStruct((T, D), BF16),
        compiler_params=_params(("parallel",)),
    )(x, g, *extra)


def _rms_bwd_body(with_bf16, x_ref, g_ref, dh_ref, res_ref, dx_ref, *rest):
    dg_ref = rest[-1]

    @pl.when(pl.program_id(0) == 0)
    def _():
        dg_ref[...] = jnp.zeros_like(dg_ref)

    x = x_ref[...]
    r = lax.rsqrt(jnp.mean(x * x, axis=-1, keepdims=True) + EPS)
    xh = x * r
    dh = dh_ref[...]
    dg_ref[...] += jnp.sum(dh * xh, axis=0, keepdims=True)
    t = dh * g_ref[...]
    dx = res_ref[...] + r * (t - xh * jnp.mean(t * xh, axis=-1, keepdims=True))
    dx_ref[...] = dx
    if with_bf16:
        rest[0][...] = dx.astype(BF16)


def _rms_bwd(x, g, dh, res, name, with_bf16):
    T, D = x.shape
    tr = _pick(T, 512, 16)
    row = pl.BlockSpec((tr, D), lambda i: (i, 0))
    vec = pl.BlockSpec((1, D), lambda i: (0, 0))
    extra = [jax.ShapeDtypeStruct((T, D), BF16)] if with_bf16 else []
    return pl.pallas_call(
        functools.partial(_rms_bwd_body, with_bf16),
        name=name,
        grid=(T // tr,),
        in_specs=[row, vec, row, row],
        out_specs=[row] + [row] * len(extra) + [vec],
        out_shape=[jax.ShapeDtypeStruct((T, D), F32)] + extra + [jax.ShapeDtypeStruct((1, D), F32)],
        compiler_params=_params(("arbitrary",)),
    )(x, g, dh, res)


def _ffn_up_body(h_ref, wg_ref, wu_ref, a_ref, b_ref, f_ref):
    for cols in _col_chunks(a_ref.shape[-1]):
        a = _dot(h_ref[...], wg_ref[:, cols], NN)
        b = _dot(h_ref[...], wu_ref[:, cols], NN)
        a_ref[:, cols] = a
        b_ref[:, cols] = b
        f_ref[:, cols] = (a * (1.0 / (1.0 + jnp.exp(-a))) * b).astype(f_ref.dtype)


def _ffn_up(h, wg, wu):
    T, D = h.shape
    F = wg.shape[1]
    tm, tn = _pick(T, MM_TILE), _pick(F, MM_CHUNK)
    hs = pl.BlockSpec((tm, D), lambda j, i: (i, 0))
    ws = pl.BlockSpec((D, tn), lambda j, i: (0, j))
    os = pl.BlockSpec((tm, tn), lambda j, i: (i, j))
    return pl.pallas_call(
        functools.partial(_ffn_up_body),
        name="ffn_up_fwd",
        grid=(F // tn, T // tm),
        in_specs=[hs, ws, ws],
        out_specs=[os, os, os],
        out_shape=[jax.ShapeDtypeStruct((T, F), F32), jax.ShapeDtypeStruct((T, F), F32),
                   jax.ShapeDtypeStruct((T, F), BF16)],
        compiler_params=_params(("parallel", "parallel")),
    )(h, wg, wu)


def _ffn_down_dx_body(dx_ref, wd_ref, a_ref, b_ref, after_ref, da_ref, db_ref):
    del after_ref
    for cols in _col_chunks(da_ref.shape[-1]):
        df = _dot(dx_ref[...], wd_ref[cols, :], NT)
        a = a_ref[:, cols]
        s = 1.0 / (1.0 + jnp.exp(-a))
        da_ref[:, cols] = (df * b_ref[:, cols] * (s * (1.0 + a * (1.0 - s)))).astype(da_ref.dtype)
        db_ref[:, cols] = (df * (a * s)).astype(db_ref.dtype)


def _ffn_down_dx(dx2b, wd, a, b, after):
    T, D = dx2b.shape
    F = wd.shape[0]
    tm, tn = _pick(T, MM_TILE // 2), _pick(F, F // N_CHIPS)
    xs = pl.BlockSpec((tm, D), lambda j, i: (i, 0))
    ws = pl.BlockSpec((tn, D), lambda j, i: (j, 0))
    os = pl.BlockSpec((tm, tn), lambda j, i: (i, j))
    return pl.pallas_call(
        functools.partial(_ffn_down_dx_body),
        name="ffn_down_dx",
        grid=(F // tn, T // tm),
        in_specs=[xs, ws, os, os, ANY],
        out_specs=[os, os],
        out_shape=[jax.ShapeDtypeStruct((T, F), BF16), jax.ShapeDtypeStruct((T, F), BF16)],
        compiler_params=_params(("parallel", "parallel")),
    )(dx2b, wd, a, b, after)


def _ffn_down_loss_body(nk, inv_d, f_ref, wd_ref, x1_ref, tgt_ref, dx2_ref, dx2b_ref, loss_ref, *scratch):
    j, i, k = pl.program_id(0), pl.program_id(1), pl.program_id(2)
    chunks = _col_chunks(dx2_ref.shape[-1])

    def dot(cols):
        return _dot(f_ref[...], wd_ref[:, cols], NN)

    @pl.when(jnp.logical_and(jnp.logical_and(j == 0, i == 0), k == 0))
    def _():
        loss_ref[...] = jnp.zeros_like(loss_ref)

    def finish(ffn_of):
        total = jnp.zeros((1, 1), F32)
        for cols in chunks:
            e = ffn_of(cols) + x1_ref[:, cols] - tgt_ref[:, cols]
            dx2 = e * inv_d
            dx2_ref[:, cols] = dx2
            dx2b_ref[:, cols] = dx2.astype(BF16)
            total = total + jnp.sum(jnp.sum(e * e, axis=-1, keepdims=True), axis=0, keepdims=True)
        loss_ref[...] += (0.5 * inv_d) * total

    if nk == 1:
        finish(dot)
        return
    acc_ref = scratch[0]

    @pl.when(k == 0)
    def _():
        for cols in chunks:
            acc_ref[:, cols] = dot(cols)

    if nk > 2:
        @pl.when(jnp.logical_and(k > 0, k < nk - 1))
        def _():
            for cols in chunks:
                acc_ref[:, cols] += dot(cols)

    @pl.when(k == nk - 1)
    def _():
        finish(lambda cols: acc_ref[:, cols] + dot(cols))


def _ffn_down_loss(f, wd, x1, tgt):
    T, F = f.shape
    D = wd.shape[1]
    tm, tn, tk = _pick(T, MM_TILE), _pick(D, MM_TILE), _pick(F, F // 2)
    gm, gn, gk = T // tm, D // tn, F // tk
    tile = pl.BlockSpec((tm, tn), lambda j, i, k: (i, j))
    return pl.pallas_call(
        functools.partial(_ffn_down_loss_body, gk, 1.0 / D),
        name="ffn_down_loss",
        grid=(gn, gm, gk),
        in_specs=[pl.BlockSpec((tm, tk), lambda j, i, k: (i, k)), pl.BlockSpec((tk, tn), lambda j, i, k: (k, j)),
                  tile, tile],
        out_specs=[tile, tile, pl.BlockSpec((1, 1), lambda j, i, k: (0, 0))],
        out_shape=[jax.ShapeDtypeStruct((T, D), F32), jax.ShapeDtypeStruct((T, D), BF16),
                   jax.ShapeDtypeStruct((1, 1), F32)],
        scratch_shapes=[pltpu.VMEM((tm, tn), F32)] if gk > 1 else [],
        compiler_params=_params(("arbitrary", "arbitrary", "arbitrary")),
    )(f, wd, x1, tgt)


def _lo_mask(shape):
    return lax.broadcasted_iota(jnp.int32, shape, len(shape) - 1) < HEAD_DIM


def _half_sums(t, lo):
    s_lo = jnp.sum(jnp.where(lo, t, 0.0), axis=-1, keepdims=True)
    s_hi = jnp.sum(jnp.where(lo, 0.0, t), axis=-1, keepdims=True)
    return jnp.where(lo, s_lo, s_hi)


def _head_rstd(t, lo):
    return lax.rsqrt(_half_sums(t * t, lo) * (1.0 / HEAD_DIM) + EPS)


def _place(t, lo, kv_head):
    if kv_head == 0:
        t_lo = jnp.where(lo, t, 0.0)
        t_hi = pltpu.roll(t_lo, HEAD_DIM, 1)
    else:
        t_hi = jnp.where(lo, 0.0, t)
        t_lo = pltpu.roll(t_hi, HEAD_DIM, 1)
    return jnp.concatenate([t_lo, t_hi], axis=0).astype(BF16)


def _unplace(c0, c1, lo):
    return jnp.where(lo, c0 + pltpu.roll(c0, HEAD_DIM, 1), c1 + pltpu.roll(c1, HEAD_DIM, 1))


def _band(kv_cur, kv_prev, kg, lo2):
    kb = jnp.concatenate([kv_prev[:, :BLK], kv_cur[:, :BLK]], axis=0)
    vb = jnp.concatenate([kv_prev[:, BLK:], kv_cur[:, BLK:]], axis=0)
    rk = _head_rstd(kb, lo2)
    kn = kb * rk * kg
    kk = [_place(kn, lo2, h) for h in range(N_KV_HEADS)]
    vv = [_place(vb, lo2, h) for h in range(N_KV_HEADS)]
    return kb, rk, kk, vv


def _score_geometry(first_i32):
    qi = lax.broadcasted_iota(jnp.int32, (BLK, 4 * BLK), 0)
    col = lax.broadcasted_iota(jnp.int32, (BLK, 4 * BLK), 1)
    kj = col & (2 * BLK - 1)
    dist = qi + BLK - kj
    valid = (dist >= 0) & (dist < BLK) & (kj >= first_i32 * BLK)
    return col, dist.astype(F32), valid


def _pair_logits(qn, kk, col, distf, valid, slope0, slope1):
    s = _dot(qn.astype(BF16), kk, NT) * (HEAD_DIM ** -0.5)
    slope = jnp.where(col < 2 * BLK, slope0, slope1)
    return jnp.where(valid, s - slope * distf, NEG)


def _pair_probs(qn, kk, col, distf, valid, slope0, slope1, sink0, sink1):
    return _softmax_halves(_pair_logits(qn, kk, col, distf, valid, slope0, slope1), sink0, sink1)


def _softmax_halves(logits, sink0, sink1):
    probs, psink = [], []
    for hh, sk in ((0, sink0), (1, sink1)):
        l = logits[:, 2 * BLK * hh:2 * BLK * (hh + 1)]
        m = jnp.maximum(jnp.max(l, axis=-1, keepdims=True), sk)
        p = jnp.exp(l - m)
        es = jnp.exp(sk - m)
        inv = 1.0 / (jnp.sum(p, axis=-1, keepdims=True) + es)
        probs.append(p * inv)
        psink.append(es * inv)
    return probs, psink


def _gelu(z, with_grad=False):
    cdf = 0.5 * (1.0 + lax.erf(z * (0.5 ** 0.5)))
    if not with_grad:
        return z * cdf
    return z * cdf, cdf + z * jnp.exp(-0.5 * z * z) * ((2.0 * jnp.pi) ** -0.5)


def _tril_w(w):
    r = lax.broadcasted_iota(jnp.int32, (BLK, BLK), 0)
    c = lax.broadcasted_iota(jnp.int32, (BLK, BLK), 1)
    return jnp.where(r >= c, w, 0.0), r >= c


def _gate_fwd_group(zu, zv, lg, lb, w, bcol, with_grad=False):
    u, v = _gelu(zu, with_grad), _gelu(zv, with_grad)
    if with_grad:
        (u, du_dz), (v, dv_dz) = u, v
    mu = jnp.mean(v, axis=-1, keepdims=True)
    vc = v - mu
    rs = lax.rsqrt(jnp.mean(vc * vc, axis=-1, keepdims=True) + EPS)
    vh = vc * rs
    vn = vh * lg + lb
    wt, tril = _tril_w(w)
    mixed = _dot(wt.astype(BF16), vn.astype(BF16), NN) + bcol
    if with_grad:
        return u, vh, rs, vn, wt, tril, mixed, du_dz, dv_dz
    return u, vh, rs, vn, wt, tril, mixed


class _Dims:
    def __init__(self, seq, attn_w, gate_w):
        self.seq, self.attn_w, self.gate_w = seq, attn_w, gate_w
        self.n_heads = attn_w // HEAD_DIM
        self.group = self.n_heads // N_KV_HEADS
        self.n_pairs = attn_w // BLK
        self.n_groups = gate_w // BLK
        self.kv_col = attn_w // (2 * BLK)
        self.u0 = attn_w + 2 * BLK
        self.v0 = self.u0 + gate_w
        self.in_w = self.v0 + gate_w
        self.slopes = [2.0 ** (-8.0 * (h + 1) / self.n_heads) for h in range(self.n_heads)]


def _mixer_fwd_body(d, sink_ref, proj_ref, kvp_ref, qg_ref, kg_ref, lg_ref, lb_ref, w_ref, b_ref, goa_ref, gog_ref,
                    ya_ref, yg_ref, y_ref, logit_scr, prob_scr):
    i = pl.program_id(0)
    first = (i % (d.seq // BLK) == 0).astype(jnp.int32)
    lo = _lo_mask((BLK, BLK))
    lo2 = _lo_mask((2 * BLK, BLK))
    kv_cur = proj_ref[:, d.attn_w:d.attn_w + 2 * BLK]
    _, _, kk, vv = _band(kv_cur, kvp_ref[...], kg_ref[...], lo2)
    col, distf, valid = _score_geometry(first)
    qg = qg_ref[...]
    for j in range(d.n_pairs):
        h0, h1 = 2 * j, 2 * j + 1
        q2 = proj_ref[:, BLK * j:BLK * (j + 1)]
        qn = q2 * _head_rstd(q2, lo) * qg
        logit_scr[j] = _pair_logits(qn, kk[h0 // d.group], col, distf, valid, d.slopes[h0], d.slopes[h1])
    for j in range(d.n_pairs):
        probs, _ = _softmax_halves(logit_scr[j], sink_ref[0, 2 * j], sink_ref[0, 2 * j + 1])
        prob_scr[j] = jnp.concatenate(probs, axis=1).astype(BF16)
    for g in range(d.n_groups):
        zu = proj_ref[:, d.u0 + BLK * g:d.u0 + BLK * (g + 1)]
        zv = proj_ref[:, d.v0 + BLK * g:d.v0 + BLK * (g + 1)]
        u, _, _, _, _, _, mixed = _gate_fwd_group(zu, zv, lg_ref[g:g + 1, :], lb_ref[g:g + 1, :], w_ref[g], b_ref[g])
        yg_ref[:, BLK * g:BLK * (g + 1)] = u * mixed
    for j in range(d.n_pairs):
        ya_ref[:, BLK * j:BLK * (j + 1)] = _dot(prob_scr[j], vv[2 * j // d.group], NN)
    ya = ya_ref[...]
    ra = lax.rsqrt(jnp.mean(ya * ya, axis=-1, keepdims=True) + EPS)
    y_ref[:, :d.attn_w] = (ya * ra * goa_ref[...]).astype(y_ref.dtype)
    yg = yg_ref[...]
    rg = lax.rsqrt(jnp.mean(yg * yg, axis=-1, keepdims=True) + EPS)
    y_ref[:, d.attn_w:] = (yg * rg * gog_ref[...]).astype(y_ref.dtype)


def _mixer_specs(d, T):
    row = lambda w: pl.BlockSpec((BLK, w), lambda i: (i, 0))
    const2 = lambda a: pl.BlockSpec(a.shape, lambda i: (0, 0))
    const3 = lambda a: pl.BlockSpec(a.shape, lambda i: (0, 0, 0))
    kv_prev = pl.BlockSpec((BLK, 2 * BLK), lambda i: (jnp.maximum(i - 1, 0), d.kv_col))
    return row, const2, const3, kv_prev


def _mixer_fwd(d, proj, sinks, qg2, kg2, lg, lb, wsp, bcol, goa, gog):
    T = proj.shape[0]
    row, const2, const3, kv_prev = _mixer_specs(d, T)
    return pl.pallas_call(
        functools.partial(_mixer_fwd_body, d),
        name="mixer_fwd",
        grid=(T // BLK,),
        in_specs=[pl.BlockSpec(memory_space=pltpu.SMEM), row(d.in_w), kv_prev, const2(qg2), const2(kg2),
                  const2(lg), const2(lb), const3(wsp), const3(bcol), const2(goa), const2(gog)],
        out_specs=[row(d.attn_w), row(d.gate_w), row(d.attn_w + d.gate_w)],
        out_shape=[jax.ShapeDtypeStruct((T, d.attn_w), F32), jax.ShapeDtypeStruct((T, d.gate_w), F32),
                   jax.ShapeDtypeStruct((T, d.attn_w + d.gate_w), BF16)],
        scratch_shapes=[pltpu.VMEM((d.n_pairs, BLK, 4 * BLK), F32), pltpu.VMEM((d.n_pairs, BLK, 4 * BLK), BF16)],
        compiler_params=_params(("parallel",)),
    )(sinks, proj, proj, qg2, kg2, lg, lb, wsp, bcol, goa, gog)


def _mixer_bwd_body(d, sink_ref, proj_ref, kvp_ref, ya_ref, yg_ref, dy_ref, qg_ref, kg_ref, lg_ref, lb_ref, w_ref,
                    b_ref, goa_ref, gog_ref,
                    dproj_ref, dkv_ref, dqg_ref, dkg_ref, dsk_ref, dlg_ref, dlb_ref, dw_ref, db_ref, dgoa_ref,
                    dgog_ref):
    i = pl.program_id(0)

    @pl.when(i == 0)
    def _():
        for r in (dqg_ref, dkg_ref, dsk_ref, dlg_ref, dlb_ref, dw_ref, db_ref, dgoa_ref, dgog_ref):
            r[...] = jnp.zeros_like(r)

    first = (i % (d.seq // BLK) == 0).astype(jnp.int32)
    lo = _lo_mask((BLK, BLK))
    lo2 = _lo_mask((2 * BLK, BLK))
    lane_row = lax.broadcasted_iota(jnp.int32, (1, BLK), 1)

    ya = ya_ref[...]
    ra = lax.rsqrt(jnp.mean(ya * ya, axis=-1, keepdims=True) + EPS)
    yah = ya * ra
    dyn = dy_ref[:, :d.attn_w]
    dgoa_ref[...] += jnp.sum(dyn * yah, axis=0, keepdims=True)
    t = dyn * goa_ref[...]
    dya = ra * (t - yah * jnp.mean(t * yah, axis=-1, keepdims=True))
    yg = yg_ref[...]
    rg = lax.rsqrt(jnp.mean(yg * yg, axis=-1, keepdims=True) + EPS)
    ygh = yg * rg
    dyn = dy_ref[:, d.attn_w:]
    dgog_ref[...] += jnp.sum(dyn * ygh, axis=0, keepdims=True)
    t = dyn * gog_ref[...]
    dyg = rg * (t - ygh * jnp.mean(t * ygh, axis=-1, keepdims=True))

    for g in range(d.n_groups):
        ucols = slice(d.u0 + BLK * g, d.u0 + BLK * (g + 1))
        vcols = slice(d.v0 + BLK * g, d.v0 + BLK * (g + 1))
        zu = proj_ref[:, ucols]
        zv = proj_ref[:, vcols]
        lg = lg_ref[g:g + 1, :]
        u, vh, rs, vn, wt, tril, mixed, du_dz, dv_dz = _gate_fwd_group(
            zu, zv, lg, lb_ref[g:g + 1, :], w_ref[g], b_ref[g], with_grad=True)
        dyg_g = dyg[:, BLK * g:BLK * (g + 1)]
        du = dyg_g * mixed
        dmix = dyg_g * u
        dmb = dmix.astype(BF16)
        db_ref[g:g + 1, :] += jnp.sum(jnp.transpose(dmix), axis=0, keepdims=True)
        dw_ref[g] += jnp.where(tril, _dot(dmb, vn.astype(BF16), NT), 0.0)
        dvn = _dot(wt.astype(BF16), dmb, TN)
        dlg_ref[g:g + 1, :] += jnp.sum(dvn * vh, axis=0, keepdims=True)
        dlb_ref[g:g + 1, :] += jnp.sum(dvn, axis=0, keepdims=True)
        dvh = dvn * lg
        dv = rs * (dvh - jnp.mean(dvh, axis=-1, keepdims=True) - vh * jnp.mean(dvh * vh, axis=-1, keepdims=True))
        dproj_ref[:, ucols] = (du * du_dz).astype(dproj_ref.dtype)
        dproj_ref[:, vcols] = (dv * dv_dz).astype(dproj_ref.dtype)

    kv_cur = proj_ref[:, d.attn_w:d.attn_w + 2 * BLK]
    kg = kg_ref[...]
    kb, rk, kk, vv = _band(kv_cur, kvp_ref[...], kg, lo2)
    col, distf, valid = _score_geometry(first)
    qg = qg_ref[...]
    ck = [jnp.zeros((BLK, 2 * BLK), F32) for _ in range(N_KV_HEADS)]
    cv = [jnp.zeros((BLK, 2 * BLK), F32) for _ in range(N_KV_HEADS)]
    lo_rows = lax.broadcasted_iota(jnp.int32, (BLK, 2 * BLK), 0) < HEAD_DIM
    dsk = jnp.zeros((1, BLK), F32)
    dqg = jnp.zeros((1, BLK), F32)
    for j in range(d.n_pairs):
        h0, h1 = 2 * j, 2 * j + 1
        kh = h0 // d.group
        cols = slice(BLK * j, BLK * (j + 1))
        q2 = proj_ref[:, cols]
        rq = _head_rstd(q2, lo)
        qh = q2 * rq
        qn = qh * qg
        probs, psink = _pair_probs(qn, kk[kh], col, distf, valid, d.slopes[h0], d.slopes[h1],
                                   sink_ref[0, h0], sink_ref[0, h1])
        do2 = dya[:, cols]
        prod = do2 * ya[:, cols]
        delta = (jnp.sum(jnp.where(lo, prod, 0.0), axis=-1, keepdims=True),
                 jnp.sum(jnp.where(lo, 0.0, prod), axis=-1, keepdims=True))
        do2b = do2.astype(BF16)
        dp = _dot(do2b, vv[kh], NT)
        ds = []
        for hh in (0, 1):
            ds.append(probs[hh] * (dp[:, 2 * BLK * hh:2 * BLK * (hh + 1)] - delta[hh]))
            dsink = -jnp.sum(psink[hh] * delta[hh], axis=0, keepdims=True)
            dsk = dsk + jnp.where(lane_row == (h0 + hh), dsink, 0.0)
        dsb = (jnp.concatenate(ds, axis=1) * (HEAD_DIM ** -0.5)).astype(BF16)
        pb = jnp.concatenate(probs, axis=1).astype(BF16)
        qnb = qn.astype(BF16)
        dqn = _dot(dsb, kk[kh], NN)
        dkk = _dot(qnb, dsb, TN)
        dvv = _dot(do2b, pb, TN)
        ck[kh] = ck[kh] + jnp.where(lo_rows, dkk[:, :2 * BLK], 0.0) + jnp.where(lo_rows, 0.0, dkk[:, 2 * BLK:])
        cv[kh] = cv[kh] + jnp.where(lo_rows, dvv[:, :2 * BLK], 0.0) + jnp.where(lo_rows, 0.0, dvv[:, 2 * BLK:])
        dqg = dqg + jnp.sum(dqn * qh, axis=0, keepdims=True)
        t = dqn * qg
        dq2 = rq * (t - qh * (_half_sums(t * qh, lo) * (1.0 / HEAD_DIM)))
        dproj_ref[:, cols] = dq2.astype(dproj_ref.dtype)
    dsk_ref[...] += dsk
    dqg_ref[...] += dqg
    dkn = _unplace(jnp.transpose(ck[0]), jnp.transpose(ck[1]), lo2)
    dvb = _unplace(jnp.transpose(cv[0]), jnp.transpose(cv[1]), lo2)
    khat = kb * rk
    dkg_ref[...] += jnp.sum(dkn * khat, axis=0, keepdims=True)
    t = dkn * kg
    dkb = rk * (t - khat * (_half_sums(t * khat, lo2) * (1.0 / HEAD_DIM)))
    rows_cur = pl.ds(pl.multiple_of(i * BLK, BLK), BLK)
    rows_prev = pl.ds(pl.multiple_of(jnp.maximum(i - 1, 0) * BLK, BLK), BLK)
    dkv_ref[rows_cur, :] = jnp.concatenate([dkb[BLK:], dvb[BLK:]], axis=1)
    dkv_ref[rows_prev, :] += jnp.concatenate([dkb[:BLK], dvb[:BLK]], axis=1)
    dproj_ref[:, d.attn_w:d.attn_w + 2 * BLK] = jnp.zeros((BLK, 2 * BLK), dproj_ref.dtype)


def _mixer_bwd(d, proj, ya, yg, dy, sinks, qg2, kg2, lg, lb, wsp, bcol, goa, gog):
    T = proj.shape[0]
    row, const2, const3, kv_prev = _mixer_specs(d, T)
    acc2 = lambda s: pl.BlockSpec(s, lambda i: (0, 0))
    G = d.n_groups
    out_shapes = [((T, d.in_w), BF16), ((T, 2 * BLK), F32), ((1, BLK), F32), ((1, BLK), F32), ((1, BLK), F32),
                  ((G, BLK), F32), ((G, BLK), F32), ((G, BLK, BLK), F32), ((G, BLK), F32),
                  ((1, d.attn_w), F32), ((1, d.gate_w), F32)]
    out_specs = [row(d.in_w)] + [acc2(s) for s, _ in out_shapes[1:7]] + \
                [pl.BlockSpec((G, BLK, BLK), lambda i: (0, 0, 0))] + [acc2(s) for s, _ in out_shapes[8:]]
    return pl.pallas_call(
        functools.partial(_mixer_bwd_body, d),
        name="mixer_bwd",
        grid=(T // BLK,),
        in_specs=[pl.BlockSpec(memory_space=pltpu.SMEM), row(d.in_w), kv_prev, row(d.attn_w), row(d.gate_w),
                  row(d.attn_w + d.gate_w), const2(qg2), const2(kg2), const2(lg), const2(lb), const3(wsp),
                  const3(bcol), const2(goa), const2(gog)],
        out_specs=out_specs,
        out_shape=[jax.ShapeDtypeStruct(s, t) for s, t in out_shapes],
        compiler_params=_params(("arbitrary",)),
    )(sinks, proj, proj, ya, yg, dy, qg2, kg2, lg, lb, wsp, bcol, goa, gog)


def _put_kv_body(dkv_ref, dproj_in_ref, dproj_ref):
    del dproj_in_ref
    dproj_ref[...] = dkv_ref[...].astype(dproj_ref.dtype)


def _put_kv(d, dproj, dkv):
    T = dproj.shape[0]
    tr = _pick(T, 1024, 16)
    return pl.pallas_call(
        functools.partial(_put_kv_body),
        name="put_kv",
        grid=(T // tr,),
        in_specs=[pl.BlockSpec((tr, 2 * BLK), lambda i: (i, 0)), pl.BlockSpec(memory_space=pl.ANY)],
        out_specs=pl.BlockSpec((tr, 2 * BLK), lambda i: (i, d.kv_col)),
        out_shape=jax.ShapeDtypeStruct(dproj.shape, dproj.dtype),
        input_output_aliases={1: 0},
        compiler_params=_params(("parallel",)),
    )(dkv, dproj)


def _add_pair_body(pc_ref, own_ref, got_ref, o_ref):
    del pc_ref
    o_ref[...] = (own_ref[...].astype(F32) + got_ref[...].astype(F32)).astype(o_ref.dtype)


def _add_pair(g4, got, pc, name):
    n, _, h, C = g4.shape
    tr = _pick(h, 512, 16)
    return pl.pallas_call(
        functools.partial(_add_pair_body),
        name=name,
        grid_spec=pltpu.PrefetchScalarGridSpec(
            num_scalar_prefetch=1,
            grid=(n, h // tr),
            in_specs=[pl.BlockSpec((None, None, tr, C), lambda q, i, pc: (q, pc[1], i, 0)),
                      pl.BlockSpec((None, tr, C), lambda q, i, pc: (q, i, 0))],
            out_specs=pl.BlockSpec((None, tr, C), lambda q, i, pc: (q, i, 0)),
        ),
        out_shape=jax.ShapeDtypeStruct((n, h, C), g4.dtype),
        compiler_params=_params(("parallel", "parallel")),
    )(pc, g4, got)


def _adamw_update(w, g, m, v):
    m = ADAM_B1 * m + (1.0 - ADAM_B1) * g
    v = ADAM_B2 * v + (1.0 - ADAM_B2) * (g * g)
    m_hat = m / (1.0 - ADAM_B1 ** ADAM_STEP)
    v_hat = v / (1.0 - ADAM_B2 ** ADAM_STEP)
    return -ADAM_LR * (m_hat / (jnp.sqrt(v_hat) + ADAM_EPS) + ADAM_WD * w), m, v


def _adamw_body(w_ref, g_ref, m_ref, v_ref, d_ref, nm_ref, nv_ref):
    d_ref[...], nm_ref[...], nv_ref[...] = _adamw_update(w_ref[...], g_ref[...], m_ref[...], v_ref[...])


def _adamw(w, g, m, v, name):
    R, C = w.shape
    tr = _pick(R, 512, 8)
    blk = pl.BlockSpec((tr, C), lambda i: (i, 0))
    return pl.pallas_call(
        functools.partial(_adamw_body),
        name=name,
        grid=(R // tr,),
        in_specs=[blk] * 4,
        out_specs=[blk] * 3,
        out_shape=[jax.ShapeDtypeStruct((R, C), F32)] * 3,
        compiler_params=_params(("parallel",)),
    )(w, g, m, v)


def _adamw_step_body(w_ref, gin_ref, m_ref, v_ref, g_ref, d_ref, nm_ref, nv_ref):
    g = gin_ref[...].astype(F32)
    g_ref[...] = g
    d_ref[...], nm_ref[...], nv_ref[...] = _adamw_update(w_ref[...], g, m_ref[...], v_ref[...])


def _adamw_halves_body(tr, steps, pc_ref, w_ref, own_ref, got_ref, m_ref, v_ref, g_ref, d_ref, nm_ref, nv_ref):
    C = own_ref.shape[1]
    ins = pl.BlockSpec((tr, C), lambda i: (i, 0), pipeline_mode=pl.Buffered(3))
    outs = pl.BlockSpec((tr, C), lambda i: (i, 0))
    pipe = pltpu.emit_pipeline(_adamw_step_body, grid=(steps,), in_specs=[ins] * 4, out_specs=[outs] * 4)
    c = pc_ref[1]
    for hh, gin in ((c, own_ref), (1 - c, got_ref)):
        pipe(w_ref.at[hh], gin, m_ref.at[hh], v_ref.at[hh],
             g_ref.at[hh], d_ref.at[hh], nm_ref.at[hh], nv_ref.at[hh])


def _adamw_halves(w, own, got, m, v, pc, name):
    h, C = own.shape
    tr = _pick(h, max(8, 393216 // C), 8)
    return pl.pallas_call(
        functools.partial(_adamw_halves_body, tr, h // tr),
        name=name,
        in_specs=[pl.BlockSpec(memory_space=pltpu.SMEM)] + [ANY] * 5,
        out_specs=[ANY] * 4,
        out_shape=[jax.ShapeDtypeStruct((2, h, C), F32)] * 4,
        compiler_params=pltpu.CompilerParams(vmem_limit_bytes=VMEM_LIMIT),
    )(pc, w.reshape(2, h, C), own, got, m.reshape(2, h, C), v.reshape(2, h, C))


def _me():
    x, y, c = lax.axis_index("x"), lax.axis_index("y"), lax.axis_index("c")
    chips = [(1 - x, y), (x, 1 - y), (1 - x, 1 - y)]
    return x, y, c, chips


def _cast_into_body(pc_ref, w_ref, o_ref):
    del pc_ref
    o_ref[...] = w_ref[...].astype(o_ref.dtype)


def _cast_into(w, pc, name, side_by_side=False):
    Rs, C = w.shape
    h = Rs // 2
    tr = _pick(h, 512, 16)
    if side_by_side:
        out_spec = pl.BlockSpec((None, tr, C), lambda hh, i, pc: (hh, i, pc[0]))
        out_shape = jax.ShapeDtypeStruct((2, h, N_CHIPS * C), BF16)
    else:
        out_spec = pl.BlockSpec((None, None, tr, C), lambda hh, i, pc: (pc[0], hh, i, 0))
        out_shape = jax.ShapeDtypeStruct((N_CHIPS, 2, h, C), BF16)
    return pl.pallas_call(
        functools.partial(_cast_into_body),
        name=name,
        grid_spec=pltpu.PrefetchScalarGridSpec(
            num_scalar_prefetch=1,
            grid=(2, h // tr),
            in_specs=[pl.BlockSpec((None, tr, C), lambda hh, i, pc: (hh, i, 0))],
            out_specs=out_spec,
        ),
        out_shape=out_shape,
        compiler_params=_params(("parallel", "parallel")),
    )(pc, w.reshape(2, h, C))


MAX_PIECES = 4


def _send_tile_to_sibling(src_of, dst_of, tr, dst_total, send_sems, recv_sem, last):
    x, y, c, _ = _me()
    pieces = MAX_PIECES if tr % (16 * MAX_PIECES) == 0 else (2 if tr % 32 == 0 else 1)
    n = tr // pieces
    copies = [pltpu.make_async_remote_copy(src_ref=src_of(k * n, n), dst_ref=dst_of(k * n, n), send_sem=send_sems.at[k],
                                           recv_sem=recv_sem, device_id=(x, y, 1 - c), device_id_type=MESH)
              for k in range(pieces)]
    for cp in copies:
        cp.start()
    for cp in copies:
        cp.wait_send()

    @pl.when(last)
    def _():
        pltpu.make_async_remote_copy(src_ref=dst_total, dst_ref=dst_total, send_sem=send_sems.at[0], recv_sem=recv_sem,
                                     device_id=(x, y, 1 - c), device_id_type=MESH).wait_recv()


TILE_SEMS = [pltpu.SemaphoreType.DMA((MAX_PIECES,)), pltpu.SemaphoreType.DMA(())]


def _ag_pair_body(tr, n_i, pc_ref, tile_ref, buf_ref, send_sem, recv_sem):
    j, i = pl.program_id(0), pl.program_id(1)
    q = pc_ref[0] ^ (j + 1)
    c = pc_ref[1]
    r_tile = pl.multiple_of(i * tr, tr)
    last = jnp.logical_and(j == N_CHIPS - 2, i == n_i - 1)
    if len(buf_ref.shape) == 4:
        _send_tile_to_sibling(lambda r0, n: tile_ref.at[:, :, pl.ds(r0, n)],
                              lambda r0, n: buf_ref.at[pl.ds(q, 1), pl.ds(c, 1), pl.ds(r_tile + r0, n)], tr,
                              buf_ref.at[pl.ds(0, N_CHIPS - 1), 0], send_sem, recv_sem, last)
    else:
        cs = buf_ref.shape[2] // N_CHIPS
        cols = pl.ds(pl.multiple_of(q * cs, BLK), cs)
        _send_tile_to_sibling(lambda r0, n: tile_ref.at[:, pl.ds(r0, n)],
                              lambda r0, n: buf_ref.at[pl.ds(c, 1), pl.ds(r_tile + r0, n), cols], tr,
                              buf_ref.at[0, :, pl.ds(0, (N_CHIPS - 1) * cs)], send_sem, recv_sem, last)


def _ag_pair(buf, pc, name):
    if len(buf.shape) == 4:
        _, _, h, C = buf.shape
        tile = lambda tr: pl.BlockSpec((1, 1, tr, C), lambda j, i, pc: (pc[0] ^ (j + 1), pc[1], i, 0))
    else:
        _, h, C = buf.shape
        tile = lambda tr: pl.BlockSpec((1, tr, C // N_CHIPS), lambda j, i, pc: (pc[1], i, pc[0] ^ (j + 1)))
    tr = _pick(h, 512, 16)
    return pl.pallas_call(
        functools.partial(_ag_pair_body, tr, h // tr),
        name=name,
        grid_spec=pltpu.PrefetchScalarGridSpec(
            num_scalar_prefetch=1,
            grid=(N_CHIPS - 1, h // tr),
            in_specs=[tile(tr)],
            out_specs=HBM,
            scratch_shapes=TILE_SEMS,
        ),
        out_shape=jax.ShapeDtypeStruct(buf.shape, buf.dtype),
        input_output_aliases={1: 0},
        compiler_params=_params(("arbitrary", "arbitrary")),
    )(pc, buf)


def _swap_halves_body(tr, n_q, n_i, pc_ref, tile_ref, got_ref, send_sem, recv_sem):
    del pc_ref
    q, i = pl.program_id(0), pl.program_id(1)
    r_tile = pl.multiple_of(i * tr, tr)
    _send_tile_to_sibling(lambda r0, n: tile_ref.at[:, :, pl.ds(r0, n)],
                          lambda r0, n: got_ref.at[pl.ds(q, 1), :, pl.ds(r_tile + r0, n)], tr, got_ref, send_sem, recv_sem,
                          jnp.logical_and(q == n_q - 1, i == n_i - 1))


def _swap_halves(g4, pc, name):
    n, _, h, C = g4.shape
    tr = _pick(h, 512, 16)
    return pl.pallas_call(
        functools.partial(_swap_halves_body, tr, n, h // tr),
        name=name,
        grid_spec=pltpu.PrefetchScalarGridSpec(
            num_scalar_prefetch=1,
            grid=(n, h // tr),
            in_specs=[pl.BlockSpec((1, 1, tr, C), lambda q, i, pc: (q, 1 - pc[1], i, 0))],
            out_specs=HBM,
            scratch_shapes=TILE_SEMS,
        ),
        out_shape=jax.ShapeDtypeStruct((n, 1, h, C), g4.dtype),
        compiler_params=_params(("arbitrary", "arbitrary")),
    )(pc, g4).reshape(n, h, C)


def _ici_copy(src, dst, send_sems, recv_sems, j, chip, c):
    return pltpu.make_async_remote_copy(src_ref=src, dst_ref=dst, send_sem=send_sems.at[j], recv_sem=recv_sems.at[j],
                                        device_id=(chip[0], chip[1], c), device_id_type=MESH)


def _token_spec():
    return jax.ShapeDtypeStruct((8, BLK), F32), pl.BlockSpec(memory_space=pltpu.VMEM)


def _slab(buf_ref, q, c):
    if len(buf_ref.shape) == 4:
        return buf_ref.at[q, c]
    cs = buf_ref.shape[2] // N_CHIPS
    return buf_ref.at[c, :, pl.ds(pl.multiple_of(q * cs, BLK), cs)]


def _ag_start_body(both_cores, buf_ref, after_ref, send_sems, recv_sems, buf_thru, token_ref):
    del after_ref, buf_thru
    x, y, c, chips = _me()
    mine = _slab(buf_ref, 2 * x + y, c)
    for j, chip in enumerate(chips):
        _ici_copy(mine, mine, send_sems, recv_sems, j, chip, c).start()
    if both_cores:
        for j, chip in enumerate(chips):
            _ici_copy(mine, mine, send_sems, recv_sems, N_CHIPS - 1 + j, chip, 1 - c).start()
    token_ref[...] = jnp.zeros_like(token_ref)


def _ag_start(buf, after, name, both_cores=False):
    tok_shape, tok_spec = _token_spec()
    sems = pltpu.SemaphoreType.DMA(((N_CHIPS - 1) * (2 if both_cores else 1),))
    return pl.pallas_call(
        functools.partial(_ag_start_body, both_cores),
        name=name,
        in_specs=[HBM, ANY],
        out_specs=[SEM, SEM, HBM, tok_spec],
        out_shape=[sems, sems, pltpu.HBM(buf.shape, buf.dtype), tok_shape],
        input_output_aliases={0: 2},
        compiler_params=pltpu.CompilerParams(has_side_effects=EFFECT),
    )(pltpu.with_memory_space_constraint(buf, pltpu.HBM), after)


def _ag_wait_body(both_cores, buf_ref, send_sems, recv_sems, after_ref, buf_out):
    del after_ref, buf_out
    x, y, c, chips = _me()
    mine = _slab(buf_ref, 2 * x + y, c)
    for j, chip in enumerate(chips):
        theirs = _slab(buf_ref, 2 * chip[0] + chip[1], c)
        _ici_copy(mine, mine, send_sems, recv_sems, j, chip, c).wait_send()
        _ici_copy(theirs, theirs, send_sems, recv_sems, j, chip, c).wait_recv()
    if both_cores:
        for j, chip in enumerate(chips):
            theirs = _slab(buf_ref, 2 * chip[0] + chip[1], 1 - c)
            _ici_copy(mine, mine, send_sems, recv_sems, N_CHIPS - 1 + j, chip, 1 - c).wait_send()
            _ici_copy(theirs, theirs, send_sems, recv_sems, N_CHIPS - 1 + j, chip, 1 - c).wait_recv()


def _ag_wait(buf, send_sems, recv_sems, after, name, both_cores=False):
    return pl.pallas_call(
        functools.partial(_ag_wait_body, both_cores),
        name=name,
        in_specs=[HBM, SEM, SEM, ANY],
        out_specs=HBM,
        out_shape=pltpu.HBM(buf.shape, buf.dtype),
        input_output_aliases={0: 0},
        compiler_params=pltpu.CompilerParams(has_side_effects=EFFECT),
    )(buf, send_sems, recv_sems, after)


def _rs_start_body(pair_ref, land_ref, after_ref, send_sems, recv_sems, pair_thru, land_thru, token_ref):
    del after_ref, pair_thru, land_thru
    x, y, c, chips = _me()
    for j, chip in enumerate(chips):
        _ici_copy(pair_ref.at[2 * chip[0] + chip[1]], land_ref.at[j], send_sems, recv_sems, j, chip, c).start()
    token_ref[...] = jnp.zeros_like(token_ref)


def _rs_start(pair, after, name):
    n, h, C = pair.shape
    tok_shape, tok_spec = _token_spec()
    sems = pltpu.SemaphoreType.DMA((N_CHIPS - 1,))
    land = pltpu.with_memory_space_constraint(lax.empty((N_CHIPS - 1, h, C), pair.dtype), pltpu.HBM)
    return pl.pallas_call(
        functools.partial(_rs_start_body),
        name=name,
        in_specs=[HBM, HBM, ANY],
        out_specs=[SEM, SEM, HBM, HBM, tok_spec],
        out_shape=[sems, sems, pltpu.HBM(pair.shape, pair.dtype), pltpu.HBM(land.shape, land.dtype), tok_shape],
        input_output_aliases={0: 2, 1: 3},
        compiler_params=pltpu.CompilerParams(has_side_effects=EFFECT),
    )(pltpu.with_memory_space_constraint(pair, pltpu.HBM), land, after)


def _rs_wait_body(pair_ref, land_ref, send_sems, recv_sems, after_ref, pair_out, land_out):
    del after_ref, pair_out, land_out
    x, y, c, chips = _me()
    for j, chip in enumerate(chips):
        _ici_copy(pair_ref.at[0], land_ref.at[j], send_sems, recv_sems, j, chip, c).wait_send()
        _ici_copy(pair_ref.at[0], land_ref.at[j], send_sems, recv_sems, j, chip, c).wait_recv()


def _rs_wait(pair, land, send_sems, recv_sems, after, name):
    return pl.pallas_call(
        functools.partial(_rs_wait_body),
        name=name,
        in_specs=[HBM, HBM, SEM, SEM, ANY],
        out_specs=[HBM, HBM],
        out_shape=[pltpu.HBM(pair.shape, pair.dtype), pltpu.HBM(land.shape, land.dtype)],
        input_output_aliases={0: 0, 1: 1},
        compiler_params=pltpu.CompilerParams(has_side_effects=EFFECT),
    )(pair, land, send_sems, recv_sems, after)


def _swap_copy(g4_ref, got_ref, send_sem, recv_sem):
    x, y, c, _ = _me()
    return pltpu.make_async_remote_copy(src_ref=g4_ref.at[:, 1 - c], dst_ref=got_ref, send_sem=send_sem,
                                        recv_sem=recv_sem, device_id=(x, y, 1 - c), device_id_type=MESH)


def _swap_start_body(g4_ref, got_ref, send_sem, recv_sem, g4_thru, got_thru, token_ref):
    del g4_thru, got_thru
    _swap_copy(g4_ref, got_ref, send_sem, recv_sem).start()
    token_ref[...] = jnp.zeros_like(token_ref)


def _swap_start(g4, name):
    n, _, h, C = g4.shape
    tok_shape, tok_spec = _token_spec()
    sem = pltpu.SemaphoreType.DMA(())
    got = pltpu.with_memory_space_constraint(lax.empty((n, h, C), g4.dtype), pltpu.HBM)
    return pl.pallas_call(
        functools.partial(_swap_start_body),
        name=name,
        in_specs=[HBM, HBM],
        out_specs=[SEM, SEM, HBM, HBM, tok_spec],
        out_shape=[sem, sem, pltpu.HBM(g4.shape, g4.dtype), pltpu.HBM(got.shape, got.dtype), tok_shape],
        input_output_aliases={0: 2, 1: 3},
        compiler_params=pltpu.CompilerParams(has_side_effects=EFFECT),
    )(pltpu.with_memory_space_constraint(g4, pltpu.HBM), got)


def _swap_wait_body(g4_ref, got_ref, send_sem, recv_sem, after_ref, g4_out, got_out):
    del after_ref, g4_out, got_out
    cp = _swap_copy(g4_ref, got_ref, send_sem, recv_sem)
    cp.wait_send()
    cp.wait_recv()


def _swap_wait(g4, got, send_sem, recv_sem, after, name):
    return pl.pallas_call(
        functools.partial(_swap_wait_body),
        name=name,
        in_specs=[HBM, HBM, SEM, SEM, ANY],
        out_specs=[HBM, HBM],
        out_shape=[pltpu.HBM(g4.shape, g4.dtype), pltpu.HBM(got.shape, got.dtype)],
        input_output_aliases={0: 0, 1: 1},
        compiler_params=pltpu.CompilerParams(has_side_effects=EFFECT),
    )(g4, got, send_sem, recv_sem, after)


def _add_chips_body(tr, n_i, pc_ref, own_ref, l0_ref, l1_ref, l2_ref, o_ref, got_ref, send_sems, recv_sem):
    del pc_ref
    i = pl.program_id(0)
    r = own_ref[...].astype(F32) + l0_ref[...].astype(F32)
    o_ref[...] = (r + l1_ref[...].astype(F32) + l2_ref[...].astype(F32)).astype(o_ref.dtype)
    r_tile = pl.multiple_of(i * tr, tr)
    _send_tile_to_sibling(lambda r0, n: o_ref.at[pl.ds(r0, n)], lambda r0, n: got_ref.at[pl.ds(r_tile + r0, n)], tr,
                          got_ref, send_sems, recv_sem, i == n_i - 1)


def _add_chips(pair, land, pc, name):
    _, h, C = pair.shape
    tr = _pick(h, 256, 16)
    slot = lambda j: pl.BlockSpec((None, tr, C), lambda i, pc: (j, i, 0))
    return pl.pallas_call(
        functools.partial(_add_chips_body, tr, h // tr),
        name=name,
        grid_spec=pltpu.PrefetchScalarGridSpec(
            num_scalar_prefetch=1,
            grid=(h // tr,),
            in_specs=[pl.BlockSpec((None, tr, C), lambda i, pc: (pc[0], i, 0)), slot(0), slot(1), slot(2)],
            out_specs=[pl.BlockSpec((tr, C), lambda i, pc: (i, 0)), HBM],
            scratch_shapes=TILE_SEMS,
        ),
        out_shape=[jax.ShapeDtypeStruct((h, C), pair.dtype), jax.ShapeDtypeStruct((h, C), pair.dtype)],
        compiler_params=_params(("arbitrary",)),
    )(pc, pair, land, land, land)


def _peer(r):
    x, y, c, _ = _me()
    return (x ^ ((r >> 2) & 1), y ^ ((r >> 1) & 1), c ^ (r & 1))


def _ar_start_body(x_ref, land_ref, send_sems, recv_sems, x_thru, land_thru, token_ref):
    del x_thru, land_thru
    for r in range(1, N_DEV):
        pltpu.make_async_remote_copy(src_ref=x_ref, dst_ref=land_ref.at[r - 1], send_sem=send_sems.at[r - 1],
                                     recv_sem=recv_sems.at[r - 1], device_id=_peer(r), device_id_type=MESH).start()
    token_ref[...] = jnp.zeros_like(token_ref)


def _ar_start(packed):
    tok_shape, tok_spec = _token_spec()
    sems = pltpu.SemaphoreType.DMA((N_DEV - 1,))
    land = pltpu.with_memory_space_constraint(lax.empty((N_DEV - 1,) + packed.shape, packed.dtype), pltpu.HBM)
    return pl.pallas_call(
        functools.partial(_ar_start_body),
        name="ar_start",
        in_specs=[HBM, HBM],
        out_specs=[SEM, SEM, HBM, HBM, tok_spec],
        out_shape=[sems, sems, pltpu.HBM(packed.shape, packed.dtype), pltpu.HBM(land.shape, land.dtype), tok_shape],
        input_output_aliases={0: 2, 1: 3},
        compiler_params=pltpu.CompilerParams(has_side_effects=EFFECT),
    )(pltpu.with_memory_space_constraint(packed, pltpu.HBM), land)


def _ar_wait_body(x_ref, land_ref, send_sems, recv_sems, after_ref, x_out, land_out):
    del after_ref, x_out, land_out
    for r in range(1, N_DEV):
        cp = pltpu.make_async_remote_copy(src_ref=x_ref, dst_ref=land_ref.at[r - 1], send_sem=send_sems.at[r - 1],
                                          recv_sem=recv_sems.at[r - 1], device_id=_peer(r), device_id_type=MESH)
        cp.wait_send()
        cp.wait_recv()


def _ar_wait(packed, land, send_sems, recv_sems, after):
    return pl.pallas_call(
        functools.partial(_ar_wait_body),
        name="ar_wait",
        in_specs=[HBM, HBM, SEM, SEM, ANY],
        out_specs=[HBM, HBM],
        out_shape=[pltpu.HBM(packed.shape, packed.dtype), pltpu.HBM(land.shape, land.dtype)],
        input_output_aliases={0: 0, 1: 1},
        compiler_params=pltpu.CompilerParams(has_side_effects=EFFECT),
    )(packed, land, send_sems, recv_sems, after)


def _ar_sum_body(me_ref, own_ref, *rest):
    o_ref = rest[N_DEV]
    acc = None
    for dev in range(N_DEV):
        term = jnp.where(me_ref[0] == dev, own_ref[...], rest[dev][...])
        acc = term if acc is None else acc + term
    o_ref[...] = acc


def _ar_sum(packed, land, me):
    R, C = packed.shape
    tr = _pick(R, 552, 8)
    own = pl.BlockSpec((tr, C), lambda i, me: (i, 0))
    slot = lambda dev: pl.BlockSpec((None, tr, C), lambda i, me: (jnp.maximum((dev ^ me[0]) - 1, 0), i, 0))
    return pl.pallas_call(
        functools.partial(_ar_sum_body),
        name="ar_sum",
        grid_spec=pltpu.PrefetchScalarGridSpec(
            num_scalar_prefetch=1,
            grid=(R // tr,),
            in_specs=[own] + [slot(dev) for dev in range(N_DEV)],
            out_specs=pl.BlockSpec((tr, C), lambda i, me: (i, 0)),
        ),
        out_shape=jax.ShapeDtypeStruct((R, C), F32),
        compiler_params=_params(("parallel",)),
    )(me, packed, *([land] * N_DEV))


def _pack(arrays):
    rows = []
    for a in arrays:
        flat = a.reshape(-1).astype(F32)
        pad = (-flat.shape[0]) % BLK
        rows.append(jnp.pad(flat, (0, pad)).reshape(-1, BLK))
    packed = jnp.concatenate(rows, axis=0)
    pad = (-packed.shape[0]) % 8
    return jnp.pad(packed, ((0, pad), (0, 0)))


def _unpack(packed, shapes):
    out, r = [], 0
    for s in shapes:
        n = 1
        for k in s:
            n *= k
        nr = -(-n // BLK)
        out.append(packed[r:r + nr].reshape(-1)[:n].reshape(s))
        r += nr
    return out


def kernel(x, norm1_g, w_in, q_norm_g, k_norm_g, attn_sinks, gate_ln_g, gate_ln_b, w_spatial, b_spatial, out_norm_attn_g, out_norm_gate_g, w_out, norm2_g, w_ffn_gate, w_ffn_up, w_ffn_down, loss_target, m_norm1_g, m_w_in, m_q_norm_g, m_k_norm_g, m_attn_sinks, m_gate_ln_g, m_gate_ln_b, m_w_spatial, m_b_spatial, m_out_norm_attn_g, m_out_norm_gate_g, m_w_out, m_norm2_g, m_w_ffn_gate, m_w_ffn_up, m_w_ffn_down, v_norm1_g, v_w_in, v_q_norm_g, v_k_norm_g, v_attn_sinks, v_gate_ln_g, v_gate_ln_b, v_w_spatial, v_b_spatial, v_out_norm_attn_g, v_out_norm_gate_g, v_w_out, v_norm2_g, v_w_ffn_gate, v_w_ffn_up, v_w_ffn_down):
    bl, seq, D = x.shape
    T = bl * seq
    attn_w, gate_w = out_norm_attn_g.shape[1], out_norm_gate_g.shape[1]
    d = _Dims(seq, attn_w, gate_w)
    G = d.n_groups
    in_w = d.in_w
    slab = w_ffn_gate.shape[2]
    dff = slab * N_CHIPS
    assert w_in.shape[2] * N_CHIPS == in_w and seq % BLK == 0 and attn_w % (2 * BLK) == 0

    pc = jnp.stack([2 * lax.axis_index("x") + lax.axis_index("y"), lax.axis_index("c")]).astype(jnp.int32)
    big = [w_in[0], w_out[0], w_ffn_gate[0], w_ffn_up[0], w_ffn_down[0]]
    names = ["in", "out", "gate", "up", "down"]
    xf = x.reshape(T, D)
    tgt = loss_target.reshape(T, D)
    send, recv, buf, behind = _ag_start(_cast_into(big[0], pc, "cast_in"), norm1_g, "ag_start_in")
    started = [(send, recv, buf)]
    h1 = _rms_fwd(xf, norm1_g, "norm1_fwd", after=behind)
    behind = h1
    for w, n in zip(big[1:], names[1:]):
        buf = _cast_into(w, pc, "cast_" + n, side_by_side=n in ("gate", "up"))
        send, recv, buf, behind = _ag_start(buf, behind, "ag_start_" + n, both_cores=n == "down")
        started.append((send, recv, buf))

    def gathered(k, after):
        send, recv, buf = started[k]
        direct = names[k] == "down"
        buf = _ag_wait(buf, send, recv, after, "ag_wait_" + names[k], both_cores=direct)
        if not direct:
            buf = _ag_pair(buf, pc, "ag_pair_" + names[k])
        rs, cs = big[k].shape
        return buf.reshape(rs, N_CHIPS * cs) if len(buf.shape) == 3 else buf.reshape(N_CHIPS, rs, cs)

    qg2 = jnp.tile(q_norm_g, (1, 2))
    kg2 = jnp.tile(k_norm_g, (1, 2))
    lg, lb, wsp = gate_ln_g[0], gate_ln_b[0], w_spatial[0]
    bcol = jnp.broadcast_to(b_spatial[0][:, :, None], (G, BLK, BLK))

    win_full = jnp.transpose(gathered(0, behind), (1, 0, 2)).reshape(D, in_w)
    proj = _matmul(h1, win_full, "nn", F32, "proj_fwd", tm=MM_TILE // 2, tn=in_w)
    ya, yg, yn = _mixer_fwd(d, proj, attn_sinks, qg2, kg2, lg, lb, wsp, bcol, out_norm_attn_g, out_norm_gate_g)
    wout_full = gathered(1, yn).reshape(attn_w + gate_w, D)
    x1 = _matmul(yn, wout_full, "nn", F32, "out_fwd", tm=MM_TILE, tn=MM_TILE, add=xf)
    h2 = _rms_fwd(x1, norm2_g, "norm2_fwd")
    wg_full, wu_full = gathered(2, h2), gathered(3, h2)
    a, b, f = _ffn_up(h2, wg_full, wu_full)
    wd_full = gathered(4, f).reshape(dff, D)
    dx2, dx2b, loss_local = _ffn_down_loss(f, wd_full, x1, tgt)

    def swap_start(g, n):
        g4 = g.reshape(N_CHIPS, 2, g.shape[1] // 2, g.shape[2])
        return _swap_start(g4, "rs_swap_start_" + n)

    def reduce_start(swapping, n, after):
        send, recv, g4, got, _ = swapping
        g4, got = _swap_wait(g4, got, send, recv, after, "rs_swap_wait_" + n)
        return _rs_start(_add_pair(g4, got, pc, "rs_add_pair_" + n), got, "rs_start_" + n)

    reducing = {}
    g_d = _matmul(f, dx2b, "tn", BF16, "ffn_down_dw", tm=slab, tn=MM_TILE, out_slab="r")
    swap_d = swap_start(g_d, "down")
    da, db = _ffn_down_dx(dx2b, wd_full, a, b, swap_d[4])
    g_g = _matmul(h2, da, "tn", BF16, "ffn_gate_dw", tm=MM_TILE, tn=slab, out_slab="c")
    swap_g = swap_start(g_g, "gate")
    reducing["down"] = reduce_start(swap_d, "down", swap_g[4])
    g_u = _matmul(h2, db, "tn", BF16, "ffn_up_dw", tm=MM_TILE, tn=slab, out_slab="c", after=reducing["down"][4])
    swap_u = swap_start(g_u, "up")
    reducing["gate"] = reduce_start(swap_g, "gate", swap_u[4])
    dh2 = _matmul(da, wg_full, "nt", F32, "ffn_gate_dx", tm=MM_TILE, tn=MM_TILE, tk=dff // 2,
                  after=reducing["gate"][4])
    dh2 = _matmul(db, wu_full, "nt", F32, "ffn_up_dx", tm=MM_TILE, tn=MM_TILE, tk=dff // 2, add=dh2)
    reducing["up"] = reduce_start(swap_u, "up", dh2)
    dx1, dx1b, dg_norm2 = _rms_bwd(x1, norm2_g, dh2, dx2, "norm2_bwd", True)
    g_o = _matmul(yn, dx1b, "tn", BF16, "out_dw", tm=MM_TILE // 2, tn=MM_TILE, out_slab="r",
                  after=reducing["up"][4])
    swap_o = swap_start(g_o, "out")
    dy = _matmul(dx1b, wout_full, "nt", F32, "out_dx", tm=MM_TILE, tn=MM_TILE, after=swap_o[4])
    (dproj, dkv, dqg, dkg, dsk, dlg, dlb, dwsp, dbsp, dgoa, dgog) = _mixer_bwd(
        d, proj, ya, yg, dy, attn_sinks, qg2, kg2, lg, lb, wsp, bcol, out_norm_attn_g, out_norm_gate_g)
    dproj = _put_kv(d, dproj, dkv)
    reducing["out"] = reduce_start(swap_o, "out", dproj)
    g_in_full = _matmul(h1, dproj, "tn", BF16, "proj_dw", tm=MM_TILE // 2, tn=in_w, tk=T // 2,
                        after=reducing["out"][4])
    g_i = jnp.transpose(g_in_full.reshape(D, N_CHIPS, in_w // N_CHIPS), (1, 0, 2))
    g4_i = g_i.reshape(N_CHIPS, 2, D // 2, in_w // N_CHIPS)
    pair_i = _add_pair(g4_i, _swap_halves(g4_i, pc, "rs_swap_in"), pc, "rs_add_pair_in")
    reducing["in"] = _rs_start(pair_i, g_i, "rs_start_in")
    dh1 = _matmul(dproj, win_full, "nt", F32, "proj_dx", tm=MM_TILE, tn=MM_TILE, after=reducing["in"][4])
    dx, dg_norm1 = _rms_bwd(xf, norm1_g, dh1, dx1, "norm1_bwd", False)

    dqg64 = dqg[:, :HEAD_DIM] + dqg[:, HEAD_DIM:]
    dkg64 = dkg[:, :HEAD_DIM] + dkg[:, HEAD_DIM:]
    small_g_local = [dg_norm1, dqg64, dkg64, dsk[:, :d.n_heads], dlg, dlb, dwsp, dbsp, dgoa, dgog, dg_norm2,
                     loss_local]
    ar_send, ar_recv, ar_own, ar_land, ar_token = _ar_start(_pack(small_g_local))

    big_m = [m_w_in[0], m_w_out[0], m_w_ffn_gate[0], m_w_ffn_up[0], m_w_ffn_down[0]]
    big_v = [v_w_in[0], v_w_out[0], v_w_ffn_gate[0], v_w_ffn_up[0], v_w_ffn_down[0]]
    big_grads, big_d, big_nm, big_nv = [], [], [], []
    for w, m, v, n in zip(big, big_m, big_v, names):
        send, recv, pair, land, _ = reducing[n]
        pair, land = _rs_wait(pair, land, send, recv, ar_token, "rs_wait_" + n)
        own, got = _add_chips(pair, land, pc, "rs_add_chips_" + n)
        outs = _adamw_halves(w, own, got, m, v, pc, "adamw_" + n)
        for lst, o in zip((big_grads, big_d, big_nm, big_nv), outs):
            lst.append(o.reshape(w.shape))

    small_names_w = [norm1_g, q_norm_g, k_norm_g, attn_sinks, gate_ln_g, gate_ln_b, w_spatial, b_spatial,
                     out_norm_attn_g, out_norm_gate_g, norm2_g]
    small_m = [m_norm1_g, m_q_norm_g, m_k_norm_g, m_attn_sinks, m_gate_ln_g, m_gate_ln_b, m_w_spatial, m_b_spatial,
               m_out_norm_attn_g, m_out_norm_gate_g, m_norm2_g]
    small_v = [v_norm1_g, v_q_norm_g, v_k_norm_g, v_attn_sinks, v_gate_ln_g, v_gate_ln_b, v_w_spatial, v_b_spatial,
               v_out_norm_attn_g, v_out_norm_gate_g, v_norm2_g]
    shapes = [w.shape for w in small_names_w] + [loss_local.shape]
    ride = [jnp.zeros(loss_local.shape, F32)]
    ar_own, ar_land = _ar_wait(ar_own, ar_land, ar_send, ar_recv, big_nv[-1])
    me = (4 * lax.axis_index("x") + 2 * lax.axis_index("y") + lax.axis_index("c")).astype(jnp.int32).reshape(1)
    sg = _ar_sum(ar_own, ar_land, me)
    sd, snm, snv = _adamw(_pack(small_names_w + ride), sg, _pack(small_m + ride), _pack(small_v + ride), "adamw_small")
    small_g, small_d, small_nm, small_nv = (_unpack(t, shapes) for t in (sg, sd, snm, snv))
    loss = small_g[-1][0, 0]

    def order(small, bigs):
        s = list(small)
        bg = [t[None] for t in bigs]
        return [s[0], bg[0], s[1], s[2], s[3], s[4], s[5], s[6], s[7], s[8], s[9], bg[1], s[10], bg[2], bg[3], bg[4]]

    grad_x = dx.reshape(bl, seq, D)
    return (loss, grad_x, *order(small_g, big_grads), *order(small_d, big_d), *order(small_nm, big_nm),
            *order(small_nv, big_nv))
```

```python
import functools

import jax
import jax.numpy as jnp
from jax import lax
from jax.experimental import pallas as pl
from jax.experimental.pallas import tpu as pltpu

F32 = jnp.float32
BF16 = jnp.bfloat16
MESH = pl.DeviceIdType.MESH

EPS = 1e-6
HEAD_DIM = 64
N_KV_HEADS = 2
BLK = 128
N_CHIPS = 4
N_DEV = 8
NEG = -1e30

ADAM_LR = 0.001
ADAM_B1 = 0.9
ADAM_B2 = 0.999
ADAM_EPS = 1e-08
ADAM_WD = 0.01
ADAM_STEP = 10

VMEM_LIMIT = 56 * 1024 * 1024

NN = (((1,), (0,)), ((), ()))
NT = (((1,), (1,)), ((), ()))
TN = (((0,), (0,)), ((), ()))
HBM = pl.BlockSpec(memory_space=pltpu.HBM)
ANY = pl.BlockSpec(memory_space=pl.ANY)
SEM = pl.BlockSpec(memory_space=pltpu.SEMAPHORE)
EFFECT = pltpu.SideEffectType.DATAFLOW_SIDE_EFFECTING


def _dot(a, b, dn):
    return lax.dot_general(a, b, dn, preferred_element_type=F32)


def _pick(dim, pref, align=128):
    if dim <= pref:
        return dim
    t = (pref // align) * align
    while t >= align:
        if dim % t == 0:
            return t
        t -= align
    return dim


def _params(sem):
    return pltpu.CompilerParams(dimension_semantics=sem, vmem_limit_bytes=VMEM_LIMIT)


MM_CHUNK = 512
MM_TILE = 1024


def _col_chunks(tn):
    return [slice(c0, min(c0 + MM_CHUNK, tn)) for c0 in range(0, tn, MM_CHUNK)]


def _mm_body(dn, nk, has_add, has_after, *refs):
    a_ref, b_ref = refs[:2]
    add_ref = refs[2] if has_add else None
    o_ref = refs[2 + has_add + has_after]
    chunks = _col_chunks(o_ref.shape[-1])

    def dot(cols):
        return _dot(a_ref[...], b_ref[cols, :] if dn == NT else b_ref[:, cols], dn)

    def finish(cols, r):
        if add_ref is not None:
            r = r + add_ref[:, cols]
        o_ref[:, cols] = r.astype(o_ref.dtype)

    if nk == 1:
        for cols in chunks:
            finish(cols, dot(cols))
        return
    acc_ref = refs[-1]
    k = pl.program_id(2)

    @pl.when(k == 0)
    def _():
        for cols in chunks:
            acc_ref[:, cols] = dot(cols)

    if nk > 2:
        @pl.when(jnp.logical_and(k > 0, k < nk - 1))
        def _():
            for cols in chunks:
                acc_ref[:, cols] += dot(cols)

    @pl.when(k == nk - 1)
    def _():
        for cols in chunks:
            finish(cols, acc_ref[:, cols] + dot(cols))


def _matmul(a, b, mode, out_dtype, name, *, tm, tn, tk=None, add=None, out_slab=None, after=None):
    if mode == "nn":
        (M, K), N = a.shape, b.shape[1]
    elif mode == "nt":
        (M, K), N = a.shape, b.shape[0]
    else:
        (K, M), N = a.shape, b.shape[1]
    tk = K if tk is None else tk
    tm, tn, tk = _pick(M, tm), _pick(N, tn), _pick(K, tk)
    if out_slab == "c":
        tn = _pick(N // N_CHIPS, tn)
    if out_slab == "r":
        tm = _pick(M // N_CHIPS, tm)
    gm, gn, gk = M // tm, N // tn, K // tk

    if mode == "tn":
        a_spec = pl.BlockSpec((tk, tm), lambda j, i, k: (k, i))
        b_spec = pl.BlockSpec((tk, tn), lambda j, i, k: (k, j))
    else:
        a_spec = pl.BlockSpec((tm, tk), lambda j, i, k: (i, k))
        if mode == "nn":
            b_spec = pl.BlockSpec((tk, tn), lambda j, i, k: (k, j))
        else:
            b_spec = pl.BlockSpec((tn, tk), lambda j, i, k: (j, k))

    if out_slab == "c":
        per = (N // N_CHIPS) // tn
        o_spec = pl.BlockSpec((None, tm, tn), lambda j, i, k: (j // per, i, j % per))
        o_shape = jax.ShapeDtypeStruct((N_CHIPS, M, N // N_CHIPS), out_dtype)
    elif out_slab == "r":
        per = (M // N_CHIPS) // tm
        o_spec = pl.BlockSpec((None, tm, tn), lambda j, i, k: (i // per, i % per, j))
        o_shape = jax.ShapeDtypeStruct((N_CHIPS, M // N_CHIPS, N), out_dtype)
    else:
        o_spec = pl.BlockSpec((tm, tn), lambda j, i, k: (i, j))
        o_shape = jax.ShapeDtypeStruct((M, N), out_dtype)

    dn = {"nn": NN, "nt": NT, "tn": TN}[mode]
    in_specs = [a_spec, b_spec]
    args = [a, b]
    if add is not None:
        in_specs.append(pl.BlockSpec((tm, tn), lambda j, i, k: (i, j)))
        args.append(add)
    if after is not None:
        in_specs.append(ANY)
        args.append(after)
    return pl.pallas_call(
        functools.partial(_mm_body, dn, gk, add is not None, after is not None),
        name=name,
        grid=(gn, gm, gk),
        in_specs=in_specs,
        out_specs=o_spec,
        out_shape=o_shape,
        scratch_shapes=[pltpu.VMEM((tm, tn), F32)] if gk > 1 else [],
        compiler_params=_params(("parallel", "parallel", "arbitrary")),
    )(*args)


def _rms_fwd_body(x_ref, g_ref, *rest):
    h_ref = rest[-1]
    x = x_ref[...]
    r = lax.rsqrt(jnp.mean(x * x, axis=-1, keepdims=True) + EPS)
    h_ref[...] = (x * r * g_ref[...]).astype(h_ref.dtype)


def _rms_fwd(x, g, name, after=None):
    T, D = x.shape
    tr = _pick(T, 512, 16)
    extra = [] if after is None else [after]
    return pl.pallas_call(
        functools.partial(_rms_fwd_body),
        name=name,
        grid=(T // tr,),
        in_specs=[pl.BlockSpec((tr, D), lambda i: (i, 0)), pl.BlockSpec((1, D), lambda i: (0, 0))] + [ANY] * len(extra),
        out_specs=pl.BlockSpec((tr, D), lambda i: (i, 0)),
        out_shape=jax.ShapeDtypeStruct((T, D), BF16),
        compiler_params=_params(("parallel",)),
    )(x, g, *extra)


def _rms_bwd_body(with_bf16, tr, x_hbm, g_ref, dh_hbm, res_hbm, dx_hbm, *rest):
    dg_ref = rest[-1]
    dg_ref[...] = jnp.zeros_like(dg_ref)
    D = g_ref.shape[1]

    def step(x_ref, dh_ref, res_ref, dx_ref, *dxb):
        x = x_ref[...]
        r = lax.rsqrt(jnp.mean(x * x, axis=-1, keepdims=True) + EPS)
        xh = x * r
        dh = dh_ref[...]
        dg_ref[...] += jnp.sum(dh * xh, axis=0, keepdims=True)
        t = dh * g_ref[...]
        dx = res_ref[...] + r * (t - xh * jnp.mean(t * xh, axis=-1, keepdims=True))
        dx_ref[...] = dx
        if with_bf16:
            dxb[0][...] = dx.astype(BF16)

    ins = pl.BlockSpec((tr, D), lambda i: (i, 0), pipeline_mode=pl.Buffered(3))
    outs = pl.BlockSpec((tr, D), lambda i: (i, 0))
    n_out = 2 if with_bf16 else 1
    pltpu.emit_pipeline(step, grid=(x_hbm.shape[0] // tr,), in_specs=[ins] * 3, out_specs=[outs] * n_out)(
        x_hbm, dh_hbm, res_hbm, dx_hbm, *rest[:-1])


def _rms_bwd(x, g, dh, res, name, with_bf16):
    T, D = x.shape
    tr = _pick(T, 256, 16)
    vec = pl.BlockSpec((1, D), lambda: (0, 0))
    extra = [jax.ShapeDtypeStruct((T, D), BF16)] if with_bf16 else []
    return pl.pallas_call(
        functools.partial(_rms_bwd_body, with_bf16, tr),
        name=name,
        in_specs=[ANY, vec, ANY, ANY],
        out_specs=[ANY] + [ANY] * len(extra) + [vec],
        out_shape=[jax.ShapeDtypeStruct((T, D), F32)] + extra + [jax.ShapeDtypeStruct((1, D), F32)],
        compiler_params=pltpu.CompilerParams(vmem_limit_bytes=VMEM_LIMIT),
    )(x, g, dh, res)


def _ffn_up_body(h_ref, wg_ref, wu_ref, a_ref, b_ref, f_ref):
    for cols in _col_chunks(a_ref.shape[-1]):
        a = _dot(h_ref[...], wg_ref[:, cols], NN)
        b = _dot(h_ref[...], wu_ref[:, cols], NN)
        a_ref[:, cols] = a
        b_ref[:, cols] = b
        f_ref[:, cols] = (a * (1.0 / (1.0 + jnp.exp(-a))) * b).astype(f_ref.dtype)


def _ffn_up(h, wg, wu):
    T, D = h.shape
    F = wg.shape[1]
    tm, tn = _pick(T, MM_TILE), _pick(F, MM_CHUNK)
    hs = pl.BlockSpec((tm, D), lambda j, i: (i, 0))
    ws = pl.BlockSpec((D, tn), lambda j, i: (0, j))
    os = pl.BlockSpec((tm, tn), lambda j, i: (i, j))
    return pl.pallas_call(
        functools.partial(_ffn_up_body),
        name="ffn_up_fwd",
        grid=(F // tn, T // tm),
        in_specs=[hs, ws, ws],
        out_specs=[os, os, os],
        out_shape=[jax.ShapeDtypeStruct((T, F), F32), jax.ShapeDtypeStruct((T, F), F32),
                   jax.ShapeDtypeStruct((T, F), BF16)],
        compiler_params=_params(("parallel", "parallel")),
    )(h, wg, wu)


def _ffn_down_dx_body(dx_ref, wd_ref, a_ref, b_ref, after_ref, da_ref, db_ref):
    del after_ref
    for cols in _col_chunks(da_ref.shape[-1]):
        df = _dot(dx_ref[...], wd_ref[cols, :], NT)
        a = a_ref[:, cols]
        s = 1.0 / (1.0 + jnp.exp(-a))
        da_ref[:, cols] = (df * b_ref[:, cols] * (s * (1.0 + a * (1.0 - s)))).astype(da_ref.dtype)
        db_ref[:, cols] = (df * (a * s)).astype(db_ref.dtype)


def _ffn_down_dx(dx2b, wd, a, b, after):
    T, D = dx2b.shape
    F = wd.shape[0]
    tm, tn = _pick(T, MM_TILE // 2), _pick(F, F // N_CHIPS)
    xs = pl.BlockSpec((tm, D), lambda j, i: (i, 0))
    ws = pl.BlockSpec((tn, D), lambda j, i: (j, 0))
    os = pl.BlockSpec((tm, tn), lambda j, i: (i, j))
    return pl.pallas_call(
        functools.partial(_ffn_down_dx_body),
        name="ffn_down_dx",
        grid=(F // tn, T // tm),
        in_specs=[xs, ws, os, os, ANY],
        out_specs=[os, os],
        out_shape=[jax.ShapeDtypeStruct((T, F), BF16), jax.ShapeDtypeStruct((T, F), BF16)],
        compiler_params=_params(("parallel", "parallel")),
    )(dx2b, wd, a, b, after)


def _ffn_down_loss_body(nk, inv_d, f_ref, wd_ref, x1_ref, tgt_ref, dx2_ref, dx2b_ref, loss_ref, *scratch):
    j, i, k = pl.program_id(0), pl.program_id(1), pl.program_id(2)
    chunks = _col_chunks(dx2_ref.shape[-1])

    def dot(cols):
        return _dot(f_ref[...], wd_ref[:, cols], NN)

    @pl.when(jnp.logical_and(jnp.logical_and(j == 0, i == 0), k == 0))
    def _():
        loss_ref[...] = jnp.zeros_like(loss_ref)

    def finish(ffn_of):
        total = jnp.zeros((1, 1), F32)
        for cols in chunks:
            e = ffn_of(cols) + x1_ref[:, cols] - tgt_ref[:, cols]
            dx2 = e * inv_d
            dx2_ref[:, cols] = dx2
            dx2b_ref[:, cols] = dx2.astype(BF16)
            total = total + jnp.sum(jnp.sum(e * e, axis=-1, keepdims=True), axis=0, keepdims=True)
        loss_ref[...] += (0.5 * inv_d) * total

    if nk == 1:
        finish(dot)
        return
    acc_ref = scratch[0]

    @pl.when(k == 0)
    def _():
        for cols in chunks:
            acc_ref[:, cols] = dot(cols)

    if nk > 2:
        @pl.when(jnp.logical_and(k > 0, k < nk - 1))
        def _():
            for cols in chunks:
                acc_ref[:, cols] += dot(cols)

    @pl.when(k == nk - 1)
    def _():
        finish(lambda cols: acc_ref[:, cols] + dot(cols))


def _ffn_down_loss(f, wd, x1, tgt):
    T, F = f.shape
    D = wd.shape[1]
    tm, tn, tk = _pick(T, MM_TILE), _pick(D, MM_TILE), _pick(F, F // 2)
    gm, gn, gk = T // tm, D // tn, F // tk
    tile = pl.BlockSpec((tm, tn), lambda j, i, k: (i, j))
    return pl.pallas_call(
        functools.partial(_ffn_down_loss_body, gk, 1.0 / D),
        name="ffn_down_loss",
        grid=(gn, gm, gk),
        in_specs=[pl.BlockSpec((tm, tk), lambda j, i, k: (i, k)), pl.BlockSpec((tk, tn), lambda j, i, k: (k, j)),
                  tile, tile],
        out_specs=[tile, tile, pl.BlockSpec((1, 1), lambda j, i, k: (0, 0))],
        out_shape=[jax.ShapeDtypeStruct((T, D), F32), jax.ShapeDtypeStruct((T, D), BF16),
                   jax.ShapeDtypeStruct((1, 1), F32)],
        scratch_shapes=[pltpu.VMEM((tm, tn), F32)] if gk > 1 else [],
        compiler_params=_params(("arbitrary", "arbitrary", "arbitrary")),
    )(f, wd, x1, tgt)


def _lo_mask(shape):
    return lax.broadcasted_iota(jnp.int32, shape, len(shape) - 1) < HEAD_DIM


def _half_sums(t, lo):
    s_lo = jnp.sum(jnp.where(lo, t, 0.0), axis=-1, keepdims=True)
    s_hi = jnp.sum(jnp.where(lo, 0.0, t), axis=-1, keepdims=True)
    return jnp.where(lo, s_lo, s_hi)


def _head_rstd(t, lo):
    return lax.rsqrt(_half_sums(t * t, lo) * (1.0 / HEAD_DIM) + EPS)


def _place(t, lo, kv_head):
    if kv_head == 0:
        t_lo = jnp.where(lo, t, 0.0)
        t_hi = pltpu.roll(t_lo, HEAD_DIM, 1)
    else:
        t_hi = jnp.where(lo, 0.0, t)
        t_lo = pltpu.roll(t_hi, HEAD_DIM, 1)
    return jnp.concatenate([t_lo, t_hi], axis=0).astype(BF16)


def _unplace(c0, c1, lo):
    return jnp.where(lo, c0 + pltpu.roll(c0, HEAD_DIM, 1), c1 + pltpu.roll(c1, HEAD_DIM, 1))


def _band(kv_cur, kv_prev, kg, lo2):
    kb = jnp.concatenate([kv_prev[:, :BLK], kv_cur[:, :BLK]], axis=0)
    vb = jnp.concatenate([kv_prev[:, BLK:], kv_cur[:, BLK:]], axis=0)
    rk = _head_rstd(kb, lo2)
    kn = kb * rk * kg
    kk = [_place(kn, lo2, h) for h in range(N_KV_HEADS)]
    vv = [_place(vb, lo2, h) for h in range(N_KV_HEADS)]
    return kb, rk, kk, vv


def _score_geometry(first_i32):
    qi = lax.broadcasted_iota(jnp.int32, (BLK, 4 * BLK), 0)
    col = lax.broadcasted_iota(jnp.int32, (BLK, 4 * BLK), 1)
    kj = col & (2 * BLK - 1)
    dist = qi + BLK - kj
    valid = (dist >= 0) & (dist < BLK) & (kj >= first_i32 * BLK)
    return col, dist.astype(F32), valid


def _pair_logits(qn, kk, col, distf, valid, slope0, slope1):
    s = _dot(qn.astype(BF16), kk, NT) * (HEAD_DIM ** -0.5)
    slope = jnp.where(col < 2 * BLK, slope0, slope1)
    return jnp.where(valid, s - slope * distf, NEG)


def _pair_probs(qn, kk, col, distf, valid, slope0, slope1, sink0, sink1):
    return _softmax_halves(_pair_logits(qn, kk, col, distf, valid, slope0, slope1), sink0, sink1)


def _softmax_halves(logits, sink0, sink1):
    probs, psink = [], []
    for hh, sk in ((0, sink0), (1, sink1)):
        l = logits[:, 2 * BLK * hh:2 * BLK * (hh + 1)]
        m = jnp.maximum(jnp.max(l, axis=-1, keepdims=True), sk)
        p = jnp.exp(l - m)
        es = jnp.exp(sk - m)
        inv = 1.0 / (jnp.sum(p, axis=-1, keepdims=True) + es)
        probs.append(p * inv)
        psink.append(es * inv)
    return probs, psink


def _gelu(z, with_grad=False):
    cdf = 0.5 * (1.0 + lax.erf(z * (0.5 ** 0.5)))
    if not with_grad:
        return z * cdf
    return z * cdf, cdf + z * jnp.exp(-0.5 * z * z) * ((2.0 * jnp.pi) ** -0.5)


def _tril_w(w):
    r = lax.broadcasted_iota(jnp.int32, (BLK, BLK), 0)
    c = lax.broadcasted_iota(jnp.int32, (BLK, BLK), 1)
    return jnp.where(r >= c, w, 0.0), r >= c


def _gate_fwd_group(zu, zv, lg, lb, w, bcol, with_grad=False):
    u, v = _gelu(zu, with_grad), _gelu(zv, with_grad)
    if with_grad:
        (u, du_dz), (v, dv_dz) = u, v
    mu = jnp.mean(v, axis=-1, keepdims=True)
    vc = v - mu
    rs = lax.rsqrt(jnp.mean(vc * vc, axis=-1, keepdims=True) + EPS)
    vh = vc * rs
    vn = vh * lg + lb
    wt, tril = _tril_w(w)
    mixed = _dot(wt.astype(BF16), vn.astype(BF16), NN) + bcol
    if with_grad:
        return u, vh, rs, vn, wt, tril, mixed, du_dz, dv_dz
    return u, vh, rs, vn, wt, tril, mixed


class _Dims:
    def __init__(self, seq, attn_w, gate_w):
        self.seq, self.attn_w, self.gate_w = seq, attn_w, gate_w
        self.n_heads = attn_w // HEAD_DIM
        self.group = self.n_heads // N_KV_HEADS
        self.n_pairs = attn_w // BLK
        self.n_groups = gate_w // BLK
        self.kv_col = attn_w // (2 * BLK)
        self.u0 = attn_w + 2 * BLK
        self.v0 = self.u0 + gate_w
        self.in_w = self.v0 + gate_w
        self.slopes = [2.0 ** (-8.0 * (h + 1) / self.n_heads) for h in range(self.n_heads)]


def _mixer_fwd_body(d, sink_ref, proj_ref, kvp_ref, qg_ref, kg_ref, lg_ref, lb_ref, w_ref, b_ref, goa_ref, gog_ref,
                    ya_ref, yg_ref, y_ref, logit_scr, prob_scr):
    i = pl.program_id(0)
    first = (i % (d.seq // BLK) == 0).astype(jnp.int32)
    lo = _lo_mask((BLK, BLK))
    lo2 = _lo_mask((2 * BLK, BLK))
    kv_cur = proj_ref[:, d.attn_w:d.attn_w + 2 * BLK]
    _, _, kk, vv = _band(kv_cur, kvp_ref[...], kg_ref[...], lo2)
    col, distf, valid = _score_geometry(first)
    qg = qg_ref[...]
    for j in range(d.n_pairs):
        h0, h1 = 2 * j, 2 * j + 1
        q2 = proj_ref[:, BLK * j:BLK * (j + 1)]
        qn = q2 * _head_rstd(q2, lo) * qg
        logit_scr[j] = _pair_logits(qn, kk[h0 // d.group], col, distf, valid, d.slopes[h0], d.slopes[h1])
    for j in range(d.n_pairs):
        probs, _ = _softmax_halves(logit_scr[j], sink_ref[0, 2 * j], sink_ref[0, 2 * j + 1])
        prob_scr[j] = jnp.concatenate(probs, axis=1).astype(BF16)
    for g in range(d.n_groups):
        zu = proj_ref[:, d.u0 + BLK * g:d.u0 + BLK * (g + 1)]
        zv = proj_ref[:, d.v0 + BLK * g:d.v0 + BLK * (g + 1)]
        u, _, _, _, _, _, mixed = _gate_fwd_group(zu, zv, lg_ref[g:g + 1, :], lb_ref[g:g + 1, :], w_ref[g], b_ref[g])
        yg_ref[:, BLK * g:BLK * (g + 1)] = u * mixed
    for j in range(d.n_pairs):
        ya_ref[:, BLK * j:BLK * (j + 1)] = _dot(prob_scr[j], vv[2 * j // d.group], NN)
    ya = ya_ref[...]
    ra = lax.rsqrt(jnp.mean(ya * ya, axis=-1, keepdims=True) + EPS)
    y_ref[:, :d.attn_w] = (ya * ra * goa_ref[...]).astype(y_ref.dtype)
    yg = yg_ref[...]
    rg = lax.rsqrt(jnp.mean(yg * yg, axis=-1, keepdims=True) + EPS)
    y_ref[:, d.attn_w:] = (yg * rg * gog_ref[...]).astype(y_ref.dtype)


def _mixer_specs(d, T):
    row = lambda w: pl.BlockSpec((BLK, w), lambda i: (i, 0))
    const2 = lambda a: pl.BlockSpec(a.shape, lambda i: (0, 0))
    const3 = lambda a: pl.BlockSpec(a.shape, lambda i: (0, 0, 0))
    kv_prev = pl.BlockSpec((BLK, 2 * BLK), lambda i: (jnp.maximum(i - 1, 0), d.kv_col))
    return row, const2, const3, kv_prev


def _mixer_fwd(d, proj, sinks, qg2, kg2, lg, lb, wsp, bcol, goa, gog):
    T = proj.shape[0]
    row, const2, const3, kv_prev = _mixer_specs(d, T)
    return pl.pallas_call(
        functools.partial(_mixer_fwd_body, d),
        name="mixer_fwd",
        grid=(T // BLK,),
        in_specs=[pl.BlockSpec(memory_space=pltpu.SMEM), row(d.in_w), kv_prev, const2(qg2), const2(kg2),
                  const2(lg), const2(lb), const3(wsp), const3(bcol), const2(goa), const2(gog)],
        out_specs=[row(d.attn_w), row(d.gate_w), row(d.attn_w + d.gate_w)],
        out_shape=[jax.ShapeDtypeStruct((T, d.attn_w), F32), jax.ShapeDtypeStruct((T, d.gate_w), F32),
                   jax.ShapeDtypeStruct((T, d.attn_w + d.gate_w), BF16)],
        scratch_shapes=[pltpu.VMEM((d.n_pairs, BLK, 4 * BLK), F32), pltpu.VMEM((d.n_pairs, BLK, 4 * BLK), BF16)],
        compiler_params=_params(("parallel",)),
    )(sinks, proj, proj, qg2, kg2, lg, lb, wsp, bcol, goa, gog)


def _mixer_bwd_body(d, sink_ref, proj_ref, kvp_ref, ya_ref, yg_ref, dy_ref, qg_ref, kg_ref, lg_ref, lb_ref, w_ref,
                    b_ref, goa_ref, gog_ref,
                    dproj_ref, dkv_ref, dqg_ref, dkg_ref, dsk_ref, dlg_ref, dlb_ref, dw_ref, db_ref, dgoa_ref,
                    dgog_ref):
    i = pl.program_id(0)

    @pl.when(i == 0)
    def _():
        for r in (dqg_ref, dkg_ref, dsk_ref, dlg_ref, dlb_ref, dw_ref, db_ref, dgoa_ref, dgog_ref):
            r[...] = jnp.zeros_like(r)

    first = (i % (d.seq // BLK) == 0).astype(jnp.int32)
    lo = _lo_mask((BLK, BLK))
    lo2 = _lo_mask((2 * BLK, BLK))
    lane_row = lax.broadcasted_iota(jnp.int32, (1, BLK), 1)

    ya = ya_ref[...]
    ra = lax.rsqrt(jnp.mean(ya * ya, axis=-1, keepdims=True) + EPS)
    yah = ya * ra
    dyn = dy_ref[:, :d.attn_w]
    dgoa_ref[...] += jnp.sum(dyn * yah, axis=0, keepdims=True)
    t = dyn * goa_ref[...]
    dya = ra * (t - yah * jnp.mean(t * yah, axis=-1, keepdims=True))
    yg = yg_ref[...]
    rg = lax.rsqrt(jnp.mean(yg * yg, axis=-1, keepdims=True) + EPS)
    ygh = yg * rg
    dyn = dy_ref[:, d.attn_w:]
    dgog_ref[...] += jnp.sum(dyn * ygh, axis=0, keepdims=True)
    t = dyn * gog_ref[...]
    dyg = rg * (t - ygh * jnp.mean(t * ygh, axis=-1, keepdims=True))

    for g in range(d.n_groups):
        ucols = slice(d.u0 + BLK * g, d.u0 + BLK * (g + 1))
        vcols = slice(d.v0 + BLK * g, d.v0 + BLK * (g + 1))
        zu = proj_ref[:, ucols]
        zv = proj_ref[:, vcols]
        lg = lg_ref[g:g + 1, :]
        u, vh, rs, vn, wt, tril, mixed, du_dz, dv_dz = _gate_fwd_group(
            zu, zv, lg, lb_ref[g:g + 1, :], w_ref[g], b_ref[g], with_grad=True)
        dyg_g = dyg[:, BLK * g:BLK * (g + 1)]
        du = dyg_g * mixed
        dmix = dyg_g * u
        dmb = dmix.astype(BF16)
        db_ref[g:g + 1, :] += jnp.sum(jnp.transpose(dmix), axis=0, keepdims=True)
        dw_ref[g] += jnp.where(tril, _dot(dmb, vn.astype(BF16), NT), 0.0)
        dvn = _dot(wt.astype(BF16), dmb, TN)
        dlg_ref[g:g + 1, :] += jnp.sum(dvn * vh, axis=0, keepdims=True)
        dlb_ref[g:g + 1, :] += jnp.sum(dvn, axis=0, keepdims=True)
        dvh = dvn * lg
        dv = rs * (dvh - jnp.mean(dvh, axis=-1, keepdims=True) - vh * jnp.mean(dvh * vh, axis=-1, keepdims=True))
        dproj_ref[:, ucols] = (du * du_dz).astype(dproj_ref.dtype)
        dproj_ref[:, vcols] = (dv * dv_dz).astype(dproj_ref.dtype)

    kv_cur = proj_ref[:, d.attn_w:d.attn_w + 2 * BLK]
    kg = kg_ref[...]
    kb, rk, kk, vv = _band(kv_cur, kvp_ref[...], kg, lo2)
    col, distf, valid = _score_geometry(first)
    qg = qg_ref[...]
    ck = [jnp.zeros((BLK, 2 * BLK), F32) for _ in range(N_KV_HEADS)]
    cv = [jnp.zeros((BLK, 2 * BLK), F32) for _ in range(N_KV_HEADS)]
    lo_rows = lax.broadcasted_iota(jnp.int32, (BLK, 2 * BLK), 0) < HEAD_DIM
    dsk = jnp.zeros((1, BLK), F32)
    dqg = jnp.zeros((1, BLK), F32)
    for j in range(d.n_pairs):
        h0, h1 = 2 * j, 2 * j + 1
        kh = h0 // d.group
        cols = slice(BLK * j, BLK * (j + 1))
        q2 = proj_ref[:, cols]
        rq = _head_rstd(q2, lo)
        qh = q2 * rq
        qn = qh * qg
        probs, psink = _pair_probs(qn, kk[kh], col, distf, valid, d.slopes[h0], d.slopes[h1],
                                   sink_ref[0, h0], sink_ref[0, h1])
        do2 = dya[:, cols]
        prod = do2 * ya[:, cols]
        delta = (jnp.sum(jnp.where(lo, prod, 0.0), axis=-1, keepdims=True),
                 jnp.sum(jnp.where(lo, 0.0, prod), axis=-1, keepdims=True))
        do2b = do2.astype(BF16)
        dp = _dot(do2b, vv[kh], NT)
        ds = []
        for hh in (0, 1):
            ds.append(probs[hh] * (dp[:, 2 * BLK * hh:2 * BLK * (hh + 1)] - delta[hh]))
            dsink = -jnp.sum(psink[hh] * delta[hh], axis=0, keepdims=True)
            dsk = dsk + jnp.where(lane_row == (h0 + hh), dsink, 0.0)
        dsb = (jnp.concatenate(ds, axis=1) * (HEAD_DIM ** -0.5)).astype(BF16)
        pb = jnp.concatenate(probs, axis=1).astype(BF16)
        qnb = qn.astype(BF16)
        dqn = _dot(dsb, kk[kh], NN)
        dkk = _dot(qnb, dsb, TN)
        dvv = _dot(do2b, pb, TN)
        ck[kh] = ck[kh] + jnp.where(lo_rows, dkk[:, :2 * BLK], 0.0) + jnp.where(lo_rows, 0.0, dkk[:, 2 * BLK:])
        cv[kh] = cv[kh] + jnp.where(lo_rows, dvv[:, :2 * BLK], 0.0) + jnp.where(lo_rows, 0.0, dvv[:, 2 * BLK:])
        dqg = dqg + jnp.sum(dqn * qh, axis=0, keepdims=True)
        t = dqn * qg
        dq2 = rq * (t - qh * (_half_sums(t * qh, lo) * (1.0 / HEAD_DIM)))
        dproj_ref[:, cols] = dq2.astype(dproj_ref.dtype)
    dsk_ref[...] += dsk
    dqg_ref[...] += dqg
    dkn = _unplace(jnp.transpose(ck[0]), jnp.transpose(ck[1]), lo2)
    dvb = _unplace(jnp.transpose(cv[0]), jnp.transpose(cv[1]), lo2)
    khat = kb * rk
    dkg_ref[...] += jnp.sum(dkn * khat, axis=0, keepdims=True)
    t = dkn * kg
    dkb = rk * (t - khat * (_half_sums(t * khat, lo2) * (1.0 / HEAD_DIM)))
    rows_cur = pl.ds(pl.multiple_of(i * BLK, BLK), BLK)
    rows_prev = pl.ds(pl.multiple_of(jnp.maximum(i - 1, 0) * BLK, BLK), BLK)
    dkv_ref[rows_cur, :] = jnp.concatenate([dkb[BLK:], dvb[BLK:]], axis=1)
    dkv_ref[rows_prev, :] += jnp.concatenate([dkb[:BLK], dvb[:BLK]], axis=1)
    dproj_ref[:, d.attn_w:d.attn_w + 2 * BLK] = jnp.zeros((BLK, 2 * BLK), dproj_ref.dtype)


def _mixer_bwd(d, proj, ya, yg, dy, sinks, qg2, kg2, lg, lb, wsp, bcol, goa, gog):
    T = proj.shape[0]
    row, const2, const3, kv_prev = _mixer_specs(d, T)
    acc2 = lambda s: pl.BlockSpec(s, lambda i: (0, 0))
    G = d.n_groups
    out_shapes = [((T, d.in_w), BF16), ((T, 2 * BLK), F32), ((1, BLK), F32), ((1, BLK), F32), ((1, BLK), F32),
                  ((G, BLK), F32), ((G, BLK), F32), ((G, BLK, BLK), F32), ((G, BLK), F32),
                  ((1, d.attn_w), F32), ((1, d.gate_w), F32)]
    out_specs = [row(d.in_w)] + [acc2(s) for s, _ in out_shapes[1:7]] + \
                [pl.BlockSpec((G, BLK, BLK), lambda i: (0, 0, 0))] + [acc2(s) for s, _ in out_shapes[8:]]
    return pl.pallas_call(
        functools.partial(_mixer_bwd_body, d),
        name="mixer_bwd",
        grid=(T // BLK,),
        in_specs=[pl.BlockSpec(memory_space=pltpu.SMEM), row(d.in_w), kv_prev, row(d.attn_w), row(d.gate_w),
                  row(d.attn_w + d.gate_w), const2(qg2), const2(kg2), const2(lg), const2(lb), const3(wsp),
                  const3(bcol), const2(goa), const2(gog)],
        out_specs=out_specs,
        out_shape=[jax.ShapeDtypeStruct(s, t) for s, t in out_shapes],
        compiler_params=_params(("arbitrary",)),
    )(sinks, proj, proj, ya, yg, dy, qg2, kg2, lg, lb, wsp, bcol, goa, gog)


def _put_kv_body(dkv_ref, dproj_in_ref, dproj_ref):
    del dproj_in_ref
    dproj_ref[...] = dkv_ref[...].astype(dproj_ref.dtype)


def _put_kv(d, dproj, dkv):
    T = dproj.shape[0]
    tr = _pick(T, 1024, 16)
    return pl.pallas_call(
        functools.partial(_put_kv_body),
        name="put_kv",
        grid=(T // tr,),
        in_specs=[pl.BlockSpec((tr, 2 * BLK), lambda i: (i, 0)), pl.BlockSpec(memory_space=pl.ANY)],
        out_specs=pl.BlockSpec((tr, 2 * BLK), lambda i: (i, d.kv_col)),
        out_shape=jax.ShapeDtypeStruct(dproj.shape, dproj.dtype),
        input_output_aliases={1: 0},
        compiler_params=_params(("parallel",)),
    )(dkv, dproj)


def _add_pair_body(pc_ref, own_ref, got_ref, o_ref):
    del pc_ref
    o_ref[...] = (own_ref[...].astype(F32) + got_ref[...].astype(F32)).astype(o_ref.dtype)


def _add_pair(g4, got, pc, name):
    n, _, h, C = g4.shape
    tr = _pick(h, 512, 16)
    return pl.pallas_call(
        functools.partial(_add_pair_body),
        name=name,
        grid_spec=pltpu.PrefetchScalarGridSpec(
            num_scalar_prefetch=1,
            grid=(n, h // tr),
            in_specs=[pl.BlockSpec((None, None, tr, C), lambda q, i, pc: (q, pc[1], i, 0)),
                      pl.BlockSpec((None, tr, C), lambda q, i, pc: (q, i, 0))],
            out_specs=pl.BlockSpec((None, tr, C), lambda q, i, pc: (q, i, 0)),
        ),
        out_shape=jax.ShapeDtypeStruct((n, h, C), g4.dtype),
        compiler_params=_params(("parallel", "parallel")),
    )(pc, g4, got)


def _adamw_update(w, g, m, v):
    m = ADAM_B1 * m + (1.0 - ADAM_B1) * g
    v = ADAM_B2 * v + (1.0 - ADAM_B2) * (g * g)
    m_hat = m / (1.0 - ADAM_B1 ** ADAM_STEP)
    v_hat = v / (1.0 - ADAM_B2 ** ADAM_STEP)
    return -ADAM_LR * (m_hat / (jnp.sqrt(v_hat) + ADAM_EPS) + ADAM_WD * w), m, v


def _adamw_body(w_ref, g_ref, m_ref, v_ref, d_ref, nm_ref, nv_ref):
    d_ref[...], nm_ref[...], nv_ref[...] = _adamw_update(w_ref[...], g_ref[...], m_ref[...], v_ref[...])


def _adamw(w, g, m, v, name):
    R, C = w.shape
    tr = _pick(R, 512, 8)
    blk = pl.BlockSpec((tr, C), lambda i: (i, 0))
    return pl.pallas_call(
        functools.partial(_adamw_body),
        name=name,
        grid=(R // tr,),
        in_specs=[blk] * 4,
        out_specs=[blk] * 3,
        out_shape=[jax.ShapeDtypeStruct((R, C), F32)] * 3,
        compiler_params=_params(("parallel",)),
    )(w, g, m, v)


def _adamw_step_body(w_ref, gin_ref, m_ref, v_ref, g_ref, d_ref, nm_ref, nv_ref):
    g = gin_ref[...].astype(F32)
    g_ref[...] = g
    d_ref[...], nm_ref[...], nv_ref[...] = _adamw_update(w_ref[...], g, m_ref[...], v_ref[...])


def _adamw_halves_body(tr, steps, pc_ref, w_ref, own_ref, got_ref, m_ref, v_ref, g_ref, d_ref, nm_ref, nv_ref):
    C = own_ref.shape[1]
    ins = pl.BlockSpec((tr, C), lambda i: (i, 0), pipeline_mode=pl.Buffered(3))
    outs = pl.BlockSpec((tr, C), lambda i: (i, 0))
    pipe = pltpu.emit_pipeline(_adamw_step_body, grid=(steps,), in_specs=[ins] * 4, out_specs=[outs] * 4)
    c = pc_ref[1]
    for hh, gin in ((c, own_ref), (1 - c, got_ref)):
        pipe(w_ref.at[hh], gin, m_ref.at[hh], v_ref.at[hh],
             g_ref.at[hh], d_ref.at[hh], nm_ref.at[hh], nv_ref.at[hh])


def _adamw_halves(w, own, got, m, v, pc, name):
    h, C = own.shape
    tr = _pick(h, max(8, 393216 // C), 8)
    return pl.pallas_call(
        functools.partial(_adamw_halves_body, tr, h // tr),
        name=name,
        in_specs=[pl.BlockSpec(memory_space=pltpu.SMEM)] + [ANY] * 5,
        out_specs=[ANY] * 4,
        out_shape=[jax.ShapeDtypeStruct((2, h, C), F32)] * 4,
        compiler_params=pltpu.CompilerParams(vmem_limit_bytes=VMEM_LIMIT),
    )(pc, w.reshape(2, h, C), own, got, m.reshape(2, h, C), v.reshape(2, h, C))


def _me():
    x, y, c = lax.axis_index("x"), lax.axis_index("y"), lax.axis_index("c")
    chips = [(1 - x, y), (x, 1 - y), (1 - x, 1 - y)]
    return x, y, c, chips


def _cast_into_body(pc_ref, w_ref, o_ref):
    del pc_ref
    o_ref[...] = w_ref[...].astype(o_ref.dtype)


def _cast_into(w, pc, name, side_by_side=False):
    Rs, C = w.shape
    h = Rs // 2
    tr = _pick(h, 512, 16)
    if side_by_side:
        out_spec = pl.BlockSpec((None, tr, C), lambda hh, i, pc: (hh, i, pc[0]))
        out_shape = jax.ShapeDtypeStruct((2, h, N_CHIPS * C), BF16)
    else:
        out_spec = pl.BlockSpec((None, None, tr, C), lambda hh, i, pc: (pc[0], hh, i, 0))
        out_shape = jax.ShapeDtypeStruct((N_CHIPS, 2, h, C), BF16)
    return pl.pallas_call(
        functools.partial(_cast_into_body),
        name=name,
        grid_spec=pltpu.PrefetchScalarGridSpec(
            num_scalar_prefetch=1,
            grid=(2, h // tr),
            in_specs=[pl.BlockSpec((None, tr, C), lambda hh, i, pc: (hh, i, 0))],
            out_specs=out_spec,
        ),
        out_shape=out_shape,
        compiler_params=_params(("parallel", "parallel")),
    )(pc, w.reshape(2, h, C))


MAX_PIECES = 4


def _send_tile_to_sibling(src_of, dst_of, tr, dst_total, send_sems, recv_sem, last):
    x, y, c, _ = _me()
    pieces = MAX_PIECES if tr % (16 * MAX_PIECES) == 0 else (2 if tr % 32 == 0 else 1)
    n = tr // pieces
    copies = [pltpu.make_async_remote_copy(src_ref=src_of(k * n, n), dst_ref=dst_of(k * n, n), send_sem=send_sems.at[k],
                                           recv_sem=recv_sem, device_id=(x, y, 1 - c), device_id_type=MESH)
              for k in range(pieces)]
    for cp in copies:
        cp.start()
    for cp in copies:
        cp.wait_send()

    @pl.when(last)
    def _():
        pltpu.make_async_remote_copy(src_ref=dst_total, dst_ref=dst_total, send_sem=send_sems.at[0], recv_sem=recv_sem,
                                     device_id=(x, y, 1 - c), device_id_type=MESH).wait_recv()


TILE_SEMS = [pltpu.SemaphoreType.DMA((MAX_PIECES,)), pltpu.SemaphoreType.DMA(())]


def _ag_pair_body(tr, n_i, pc_ref, tile_ref, buf_ref, send_sem, recv_sem):
    j, i = pl.program_id(0), pl.program_id(1)
    q = pc_ref[0] ^ (j + 1)
    c = pc_ref[1]
    r_tile = pl.multiple_of(i * tr, tr)
    last = jnp.logical_and(j == N_CHIPS - 2, i == n_i - 1)
    if len(buf_ref.shape) == 4:
        _send_tile_to_sibling(lambda r0, n: tile_ref.at[:, :, pl.ds(r0, n)],
                              lambda r0, n: buf_ref.at[pl.ds(q, 1), pl.ds(c, 1), pl.ds(r_tile + r0, n)], tr,
                              buf_ref.at[pl.ds(0, N_CHIPS - 1), 0], send_sem, recv_sem, last)
    else:
        cs = buf_ref.shape[2] // N_CHIPS
        cols = pl.ds(pl.multiple_of(q * cs, BLK), cs)
        _send_tile_to_sibling(lambda r0, n: tile_ref.at[:, pl.ds(r0, n)],
                              lambda r0, n: buf_ref.at[pl.ds(c, 1), pl.ds(r_tile + r0, n), cols], tr,
                              buf_ref.at[0, :, pl.ds(0, (N_CHIPS - 1) * cs)], send_sem, recv_sem, last)


def _ag_pair(buf, pc, name):
    if len(buf.shape) == 4:
        _, _, h, C = buf.shape
        tile = lambda tr: pl.BlockSpec((1, 1, tr, C), lambda j, i, pc: (pc[0] ^ (j + 1), pc[1], i, 0))
    else:
        _, h, C = buf.shape
        tile = lambda tr: pl.BlockSpec((1, tr, C // N_CHIPS), lambda j, i, pc: (pc[1], i, pc[0] ^ (j + 1)))
    tr = _pick(h, 512, 16)
    return pl.pallas_call(
        functools.partial(_ag_pair_body, tr, h // tr),
        name=name,
        grid_spec=pltpu.PrefetchScalarGridSpec(
            num_scalar_prefetch=1,
            grid=(N_CHIPS - 1, h // tr),
            in_specs=[tile(tr)],
            out_specs=HBM,
            scratch_shapes=TILE_SEMS,
        ),
        out_shape=jax.ShapeDtypeStruct(buf.shape, buf.dtype),
        input_output_aliases={1: 0},
        compiler_params=_params(("arbitrary", "arbitrary")),
    )(pc, buf)


def _swap_halves_body(tr, n_q, n_i, pc_ref, tile_ref, got_ref, send_sem, recv_sem):
    del pc_ref
    q, i = pl.program_id(0), pl.program_id(1)
    r_tile = pl.multiple_of(i * tr, tr)
    _send_tile_to_sibling(lambda r0, n: tile_ref.at[:, :, pl.ds(r0, n)],
                          lambda r0, n: got_ref.at[pl.ds(q, 1), :, pl.ds(r_tile + r0, n)], tr, got_ref, send_sem, recv_sem,
                          jnp.logical_and(q == n_q - 1, i == n_i - 1))


def _swap_halves(g4, pc, name):
    n, _, h, C = g4.shape
    tr = _pick(h, 512, 16)
    return pl.pallas_call(
        functools.partial(_swap_halves_body, tr, n, h // tr),
        name=name,
        grid_spec=pltpu.PrefetchScalarGridSpec(
            num_scalar_prefetch=1,
            grid=(n, h // tr),
            in_specs=[pl.BlockSpec((1, 1, tr, C), lambda q, i, pc: (q, 1 - pc[1], i, 0))],
            out_specs=HBM,
            scratch_shapes=TILE_SEMS,
        ),
        out_shape=jax.ShapeDtypeStruct((n, 1, h, C), g4.dtype),
        compiler_params=_params(("arbitrary", "arbitrary")),
    )(pc, g4).reshape(n, h, C)


def _ici_copy(src, dst, send_sems, recv_sems, j, chip, c):
    return pltpu.make_async_remote_copy(src_ref=src, dst_ref=dst, send_sem=send_sems.at[j], recv_sem=recv_sems.at[j],
                                        device_id=(chip[0], chip[1], c), device_id_type=MESH)


def _token_spec():
    return jax.ShapeDtypeStruct((8, BLK), F32), pl.BlockSpec(memory_space=pltpu.VMEM)


def _slab(buf_ref, q, c):
    if len(buf_ref.shape) == 4:
        return buf_ref.at[q, c]
    cs = buf_ref.shape[2] // N_CHIPS
    return buf_ref.at[c, :, pl.ds(pl.multiple_of(q * cs, BLK), cs)]


def _ag_start_body(both_cores, buf_ref, after_ref, send_sems, recv_sems, buf_thru, token_ref):
    del after_ref, buf_thru
    x, y, c, chips = _me()
    mine = _slab(buf_ref, 2 * x + y, c)
    for j, chip in enumerate(chips):
        _ici_copy(mine, mine, send_sems, recv_sems, j, chip, c).start()
    if both_cores:
        for j, chip in enumerate(chips):
            _ici_copy(mine, mine, send_sems, recv_sems, N_CHIPS - 1 + j, chip, 1 - c).start()
    token_ref[...] = jnp.zeros_like(token_ref)


def _ag_start(buf, after, name, both_cores=False):
    tok_shape, tok_spec = _token_spec()
    sems = pltpu.SemaphoreType.DMA(((N_CHIPS - 1) * (2 if both_cores else 1),))
    return pl.pallas_call(
        functools.partial(_ag_start_body, both_cores),
        name=name,
        in_specs=[HBM, ANY],
        out_specs=[SEM, SEM, HBM, tok_spec],
        out_shape=[sems, sems, pltpu.HBM(buf.shape, buf.dtype), tok_shape],
        input_output_aliases={0: 2},
        compiler_params=pltpu.CompilerParams(has_side_effects=EFFECT),
    )(pltpu.with_memory_space_constraint(buf, pltpu.HBM), after)


def _ag_wait_body(both_cores, buf_ref, send_sems, recv_sems, after_ref, buf_out):
    del after_ref, buf_out
    x, y, c, chips = _me()
    mine = _slab(buf_ref, 2 * x + y, c)
    for j, chip in enumerate(chips):
        theirs = _slab(buf_ref, 2 * chip[0] + chip[1], c)
        _ici_copy(mine, mine, send_sems, recv_sems, j, chip, c).wait_send()
        _ici_copy(theirs, theirs, send_sems, recv_sems, j, chip, c).wait_recv()
    if both_cores:
        for j, chip in enumerate(chips):
            theirs = _slab(buf_ref, 2 * chip[0] + chip[1], 1 - c)
            _ici_copy(mine, mine, send_sems, recv_sems, N_CHIPS - 1 + j, chip, 1 - c).wait_send()
            _ici_copy(theirs, theirs, send_sems, recv_sems, N_CHIPS - 1 + j, chip, 1 - c).wait_recv()


def _ag_wait(buf, send_sems, recv_sems, after, name, both_cores=False):
    return pl.pallas_call(
        functools.partial(_ag_wait_body, both_cores),
        name=name,
        in_specs=[HBM, SEM, SEM, ANY],
        out_specs=HBM,
        out_shape=pltpu.HBM(buf.shape, buf.dtype),
        input_output_aliases={0: 0},
        compiler_params=pltpu.CompilerParams(has_side_effects=EFFECT),
    )(buf, send_sems, recv_sems, after)


def _rs_start_body(pair_ref, land_ref, after_ref, send_sems, recv_sems, pair_thru, land_thru, token_ref):
    del after_ref, pair_thru, land_thru
    x, y, c, chips = _me()
    for j, chip in enumerate(chips):
        _ici_copy(pair_ref.at[2 * chip[0] + chip[1]], land_ref.at[j], send_sems, recv_sems, j, chip, c).start()
    token_ref[...] = jnp.zeros_like(token_ref)


def _rs_start(pair, after, name):
    n, h, C = pair.shape
    tok_shape, tok_spec = _token_spec()
    sems = pltpu.SemaphoreType.DMA((N_CHIPS - 1,))
    land = pltpu.with_memory_space_constraint(lax.empty((N_CHIPS - 1, h, C), pair.dtype), pltpu.HBM)
    return pl.pallas_call(
        functools.partial(_rs_start_body),
        name=name,
        in_specs=[HBM, HBM, ANY],
        out_specs=[SEM, SEM, HBM, HBM, tok_spec],
        out_shape=[sems, sems, pltpu.HBM(pair.shape, pair.dtype), pltpu.HBM(land.shape, land.dtype), tok_shape],
        input_output_aliases={0: 2, 1: 3},
        compiler_params=pltpu.CompilerParams(has_side_effects=EFFECT),
    )(pltpu.with_memory_space_constraint(pair, pltpu.HBM), land, after)


def _rs_wait_body(pair_ref, land_ref, send_sems, recv_sems, after_ref, pair_out, land_out):
    del after_ref, pair_out, land_out
    x, y, c, chips = _me()
    for j, chip in enumerate(chips):
        _ici_copy(pair_ref.at[0], land_ref.at[j], send_sems, recv_sems, j, chip, c).wait_send()
        _ici_copy(pair_ref.at[0], land_ref.at[j], send_sems, recv_sems, j, chip, c).wait_recv()


def _rs_wait(pair, land, send_sems, recv_sems, after, name):
    return pl.pallas_call(
        functools.partial(_rs_wait_body),
        name=name,
        in_specs=[HBM, HBM, SEM, SEM, ANY],
        out_specs=[HBM, HBM],
        out_shape=[pltpu.HBM(pair.shape, pair.dtype), pltpu.HBM(land.shape, land.dtype)],
        input_output_aliases={0: 0, 1: 1},
        compiler_params=pltpu.CompilerParams(has_side_effects=EFFECT),
    )(pair, land, send_sems, recv_sems, after)


def _swap_copy(g4_ref, got_ref, send_sem, recv_sem):
    x, y, c, _ = _me()
    return pltpu.make_async_remote_copy(src_ref=g4_ref.at[:, 1 - c], dst_ref=got_ref, send_sem=send_sem,
                                        recv_sem=recv_sem, device_id=(x, y, 1 - c), device_id_type=MESH)


def _swap_start_body(g4_ref, got_ref, send_sem, recv_sem, g4_thru, got_thru, token_ref):
    del g4_thru, got_thru
    _swap_copy(g4_ref, got_ref, send_sem, recv_sem).start()
    token_ref[...] = jnp.zeros_like(token_ref)


def _swap_start(g4, name):
    n, _, h, C = g4.shape
    tok_shape, tok_spec = _token_spec()
    sem = pltpu.SemaphoreType.DMA(())
    got = pltpu.with_memory_space_constraint(lax.empty((n, h, C), g4.dtype), pltpu.HBM)
    return pl.pallas_call(
        functools.partial(_swap_start_body),
        name=name,
        in_specs=[HBM, HBM],
        out_specs=[SEM, SEM, HBM, HBM, tok_spec],
        out_shape=[sem, sem, pltpu.HBM(g4.shape, g4.dtype), pltpu.HBM(got.shape, got.dtype), tok_shape],
        input_output_aliases={0: 2, 1: 3},
        compiler_params=pltpu.CompilerParams(has_side_effects=EFFECT),
    )(pltpu.with_memory_space_constraint(g4, pltpu.HBM), got)


def _swap_wait_body(g4_ref, got_ref, send_sem, recv_sem, after_ref, g4_out, got_out):
    del after_ref, g4_out, got_out
    cp = _swap_copy(g4_ref, got_ref, send_sem, recv_sem)
    cp.wait_send()
    cp.wait_recv()


def _swap_wait(g4, got, send_sem, recv_sem, after, name):
    return pl.pallas_call(
        functools.partial(_swap_wait_body),
        name=name,
        in_specs=[HBM, HBM, SEM, SEM, ANY],
        out_specs=[HBM, HBM],
        out_shape=[pltpu.HBM(g4.shape, g4.dtype), pltpu.HBM(got.shape, got.dtype)],
        input_output_aliases={0: 0, 1: 1},
        compiler_params=pltpu.CompilerParams(has_side_effects=EFFECT),
    )(g4, got, send_sem, recv_sem, after)


def _add_chips_body(tr, n_i, pc_ref, own_ref, l0_ref, l1_ref, l2_ref, o_ref, got_ref, send_sems, recv_sem):
    del pc_ref
    i = pl.program_id(0)
    r = own_ref[...].astype(F32) + l0_ref[...].astype(F32)
    o_ref[...] = (r + l1_ref[...].astype(F32) + l2_ref[...].astype(F32)).astype(o_ref.dtype)
    r_tile = pl.multiple_of(i * tr, tr)
    _send_tile_to_sibling(lambda r0, n: o_ref.at[pl.ds(r0, n)], lambda r0, n: got_ref.at[pl.ds(r_tile + r0, n)], tr,
                          got_ref, send_sems, recv_sem, i == n_i - 1)


def _add_chips(pair, land, pc, name):
    _, h, C = pair.shape
    tr = _pick(h, 256, 16)
    slot = lambda j: pl.BlockSpec((None, tr, C), lambda i, pc: (j, i, 0))
    return pl.pallas_call(
        functools.partial(_add_chips_body, tr, h // tr),
        name=name,
        grid_spec=pltpu.PrefetchScalarGridSpec(
            num_scalar_prefetch=1,
            grid=(h // tr,),
            in_specs=[pl.BlockSpec((None, tr, C), lambda i, pc: (pc[0], i, 0)), slot(0), slot(1), slot(2)],
            out_specs=[pl.BlockSpec((tr, C), lambda i, pc: (i, 0)), HBM],
            scratch_shapes=TILE_SEMS,
        ),
        out_shape=[jax.ShapeDtypeStruct((h, C), pair.dtype), jax.ShapeDtypeStruct((h, C), pair.dtype)],
        compiler_params=_params(("arbitrary",)),
    )(pc, pair, land, land, land)


def _peer(r):
    x, y, c, _ = _me()
    return (x ^ ((r >> 2) & 1), y ^ ((r >> 1) & 1), c ^ (r & 1))


def _ar_start_body(x_ref, land_ref, send_sems, recv_sems, x_thru, land_thru, token_ref):
    del x_thru, land_thru
    for r in range(1, N_DEV):
        pltpu.make_async_remote_copy(src_ref=x_ref, dst_ref=land_ref.at[r - 1], send_sem=send_sems.at[r - 1],
                                     recv_sem=recv_sems.at[r - 1], device_id=_peer(r), device_id_type=MESH).start()
    token_ref[...] = jnp.zeros_like(token_ref)


def _ar_start(packed):
    tok_shape, tok_spec = _token_spec()
    sems = pltpu.SemaphoreType.DMA((N_DEV - 1,))
    land = pltpu.with_memory_space_constraint(lax.empty((N_DEV - 1,) + packed.shape, packed.dtype), pltpu.HBM)
    return pl.pallas_call(
        functools.partial(_ar_start_body),
        name="ar_start",
        in_specs=[HBM, HBM],
        out_specs=[SEM, SEM, HBM, HBM, tok_spec],
        out_shape=[sems, sems, pltpu.HBM(packed.shape, packed.dtype), pltpu.HBM(land.shape, land.dtype), tok_shape],
        input_output_aliases={0: 2, 1: 3},
        compiler_params=pltpu.CompilerParams(has_side_effects=EFFECT),
    )(pltpu.with_memory_space_constraint(packed, pltpu.HBM), land)


def _ar_wait_body(x_ref, land_ref, send_sems, recv_sems, after_ref, x_out, land_out):
    del after_ref, x_out, land_out
    for r in range(1, N_DEV):
        cp = pltpu.make_async_remote_copy(src_ref=x_ref, dst_ref=land_ref.at[r - 1], send_sem=send_sems.at[r - 1],
                                          recv_sem=recv_sems.at[r - 1], device_id=_peer(r), device_id_type=MESH)
        cp.wait_send()
        cp.wait_recv()


def _ar_wait(packed, land, send_sems, recv_sems, after):
    return pl.pallas_call(
        functools.partial(_ar_wait_body),
        name="ar_wait",
        in_specs=[HBM, HBM, SEM, SEM, ANY],
        out_specs=[HBM, HBM],
        out_shape=[pltpu.HBM(packed.shape, packed.dtype), pltpu.HBM(land.shape, land.dtype)],
        input_output_aliases={0: 0, 1: 1},
        compiler_params=pltpu.CompilerParams(has_side_effects=EFFECT),
    )(packed, land, send_sems, recv_sems, after)


def _ar_sum_body(me_ref, own_ref, *rest):
    o_ref = rest[N_DEV]
    acc = None
    for dev in range(N_DEV):
        term = jnp.where(me_ref[0] == dev, own_ref[...], rest[dev][...])
        acc = term if acc is None else acc + term
    o_ref[...] = acc


def _ar_sum(packed, land, me):
    R, C = packed.shape
    tr = _pick(R, 552, 8)
    own = pl.BlockSpec((tr, C), lambda i, me: (i, 0))
    slot = lambda dev: pl.BlockSpec((None, tr, C), lambda i, me: (jnp.maximum((dev ^ me[0]) - 1, 0), i, 0))
    return pl.pallas_call(
        functools.partial(_ar_sum_body),
        name="ar_sum",
        grid_spec=pltpu.PrefetchScalarGridSpec(
            num_scalar_prefetch=1,
            grid=(R // tr,),
            in_specs=[own] + [slot(dev) for dev in range(N_DEV)],
            out_specs=pl.BlockSpec((tr, C), lambda i, me: (i, 0)),
        ),
        out_shape=jax.ShapeDtypeStruct((R, C), F32),
        compiler_params=_params(("parallel",)),
    )(me, packed, *([land] * N_DEV))


def _pack(arrays):
    rows = []
    for a in arrays:
        flat = a.reshape(-1).astype(F32)
        pad = (-flat.shape[0]) % BLK
        rows.append(jnp.pad(flat, (0, pad)).reshape(-1, BLK))
    packed = jnp.concatenate(rows, axis=0)
    pad = (-packed.shape[0]) % 8
    return jnp.pad(packed, ((0, pad), (0, 0)))


def _unpack(packed, shapes):
    out, r = [], 0
    for s in shapes:
        n = 1
        for k in s:
            n *= k
        nr = -(-n // BLK)
        out.append(packed[r:r + nr].reshape(-1)[:n].reshape(s))
        r += nr
    return out


def kernel(x, norm1_g, w_in, q_norm_g, k_norm_g, attn_sinks, gate_ln_g, gate_ln_b, w_spatial, b_spatial, out_norm_attn_g, out_norm_gate_g, w_out, norm2_g, w_ffn_gate, w_ffn_up, w_ffn_down, loss_target, m_norm1_g, m_w_in, m_q_norm_g, m_k_norm_g, m_attn_sinks, m_gate_ln_g, m_gate_ln_b, m_w_spatial, m_b_spatial, m_out_norm_attn_g, m_out_norm_gate_g, m_w_out, m_norm2_g, m_w_ffn_gate, m_w_ffn_up, m_w_ffn_down, v_norm1_g, v_w_in, v_q_norm_g, v_k_norm_g, v_attn_sinks, v_gate_ln_g, v_gate_ln_b, v_w_spatial, v_b_spatial, v_out_norm_attn_g, v_out_norm_gate_g, v_w_out, v_norm2_g, v_w_ffn_gate, v_w_ffn_up, v_w_ffn_down):
    bl, seq, D = x.shape
    T = bl * seq
    attn_w, gate_w = out_norm_attn_g.shape[1], out_norm_gate_g.shape[1]
    d = _Dims(seq, attn_w, gate_w)
    G = d.n_groups
    in_w = d.in_w
    slab = w_ffn_gate.shape[2]
    dff = slab * N_CHIPS
    assert w_in.shape[2] * N_CHIPS == in_w and seq % BLK == 0 and attn_w % (2 * BLK) == 0

    pc = jnp.stack([2 * lax.axis_index("x") + lax.axis_index("y"), lax.axis_index("c")]).astype(jnp.int32)
    big = [w_in[0], w_out[0], w_ffn_gate[0], w_ffn_up[0], w_ffn_down[0]]
    names = ["in", "out", "gate", "up", "down"]
    xf = x.reshape(T, D)
    tgt = loss_target.reshape(T, D)
    send, recv, buf, behind = _ag_start(_cast_into(big[0], pc, "cast_in"), norm1_g, "ag_start_in")
    started = [(send, recv, buf)]
    h1 = _rms_fwd(xf, norm1_g, "norm1_fwd", after=behind)
    behind = h1
    for w, n in zip(big[1:], names[1:]):
        buf = _cast_into(w, pc, "cast_" + n, side_by_side=n in ("gate", "up"))
        send, recv, buf, behind = _ag_start(buf, behind, "ag_start_" + n, both_cores=n == "down")
        started.append((send, recv, buf))

    def gathered(k, after):
        send, recv, buf = started[k]
        direct = names[k] == "down"
        buf = _ag_wait(buf, send, recv, after, "ag_wait_" + names[k], both_cores=direct)
        if not direct:
            buf = _ag_pair(buf, pc, "ag_pair_" + names[k])
        rs, cs = big[k].shape
        return buf.reshape(rs, N_CHIPS * cs) if len(buf.shape) == 3 else buf.reshape(N_CHIPS, rs, cs)

    qg2 = jnp.tile(q_norm_g, (1, 2))
    kg2 = jnp.tile(k_norm_g, (1, 2))
    lg, lb, wsp = gate_ln_g[0], gate_ln_b[0], w_spatial[0]
    bcol = jnp.broadcast_to(b_spatial[0][:, :, None], (G, BLK, BLK))

    win_full = jnp.transpose(gathered(0, behind), (1, 0, 2)).reshape(D, in_w)
    proj = _matmul(h1, win_full, "nn", F32, "proj_fwd", tm=MM_TILE // 2, tn=in_w)
    ya, yg, yn = _mixer_fwd(d, proj, attn_sinks, qg2, kg2, lg, lb, wsp, bcol, out_norm_attn_g, out_norm_gate_g)
    wout_full = gathered(1, yn).reshape(attn_w + gate_w, D)
    x1 = _matmul(yn, wout_full, "nn", F32, "out_fwd", tm=MM_TILE, tn=MM_TILE, add=xf)
    h2 = _rms_fwd(x1, norm2_g, "norm2_fwd")
    wg_full, wu_full = gathered(2, h2), gathered(3, h2)
    a, b, f = _ffn_up(h2, wg_full, wu_full)
    wd_full = gathered(4, f).reshape(dff, D)
    dx2, dx2b, loss_local = _ffn_down_loss(f, wd_full, x1, tgt)

    def swap_start(g, n):
        g4 = g.reshape(N_CHIPS, 2, g.shape[1] // 2, g.shape[2])
        return _swap_start(g4, "rs_swap_start_" + n)

    def reduce_start(swapping, n, after):
        send, recv, g4, got, _ = swapping
        g4, got = _swap_wait(g4, got, send, recv, after, "rs_swap_wait_" + n)
        return _rs_start(_add_pair(g4, got, pc, "rs_add_pair_" + n), got, "rs_start_" + n)

    reducing = {}
    g_d = _matmul(f, dx2b, "tn", BF16, "ffn_down_dw", tm=slab, tn=MM_TILE, out_slab="r")
    swap_d = swap_start(g_d, "down")
    da, db = _ffn_down_dx(dx2b, wd_full, a, b, swap_d[4])
    g_g = _matmul(h2, da, "tn", BF16, "ffn_gate_dw", tm=MM_TILE, tn=slab, out_slab="c")
    swap_g = swap_start(g_g, "gate")
    reducing["down"] = reduce_start(swap_d, "down", swap_g[4])
    g_u = _matmul(h2, db, "tn", BF16, "ffn_up_dw", tm=MM_TILE, tn=slab, out_slab="c", after=reducing["down"][4])
    swap_u = swap_start(g_u, "up")
    reducing["gate"] = reduce_start(swap_g, "gate", swap_u[4])
    dh2 = _matmul(da, wg_full, "nt", F32, "ffn_gate_dx", tm=MM_TILE, tn=MM_TILE, tk=dff // 2,
                  after=reducing["gate"][4])
    dh2 = _matmul(db, wu_full, "nt", F32, "ffn_up_dx", tm=MM_TILE, tn=MM_TILE, tk=dff // 2, add=dh2)
    reducing["up"] = reduce_start(swap_u, "up", dh2)
    dx1, dx1b, dg_norm2 = _rms_bwd(x1, norm2_g, dh2, dx2, "norm2_bwd", True)
    g_o = _matmul(yn, dx1b, "tn", BF16, "out_dw", tm=MM_TILE // 2, tn=MM_TILE, out_slab="r",
                  after=reducing["up"][4])
    swap_o = swap_start(g_o, "out")
    dy = _matmul(dx1b, wout_full, "nt", F32, "out_dx", tm=MM_TILE, tn=MM_TILE, after=swap_o[4])
    (dproj, dkv, dqg, dkg, dsk, dlg, dlb, dwsp, dbsp, dgoa, dgog) = _mixer_bwd(
        d, proj, ya, yg, dy, attn_sinks, qg2, kg2, lg, lb, wsp, bcol, out_norm_attn_g, out_norm_gate_g)
    dproj = _put_kv(d, dproj, dkv)
    reducing["out"] = reduce_start(swap_o, "out", dproj)
    g_in_full = _matmul(h1, dproj, "tn", BF16, "proj_dw", tm=MM_TILE // 2, tn=in_w, tk=T // 2,
                        after=reducing["out"][4])
    g_i = jnp.transpose(g_in_full.reshape(D, N_CHIPS, in_w // N_CHIPS), (1, 0, 2))
    g4_i = g_i.reshape(N_CHIPS, 2, D // 2, in_w // N_CHIPS)
    pair_i = _add_pair(g4_i, _swap_halves(g4_i, pc, "rs_swap_in"), pc, "rs_add_pair_in")
    reducing["in"] = _rs_start(pair_i, g_i, "rs_start_in")
    dh1 = _matmul(dproj, win_full, "nt", F32, "proj_dx", tm=MM_TILE, tn=MM_TILE, after=reducing["in"][4])
    dx, dg_norm1 = _rms_bwd(xf, norm1_g, dh1, dx1, "norm1_bwd", False)

    dqg64 = dqg[:, :HEAD_DIM] + dqg[:, HEAD_DIM:]
    dkg64 = dkg[:, :HEAD_DIM] + dkg[:, HEAD_DIM:]
    small_g_local = [dg_norm1, dqg64, dkg64, dsk[:, :d.n_heads], dlg, dlb, dwsp, dbsp, dgoa, dgog, dg_norm2,
                     loss_local]
    ar_send, ar_recv, ar_own, ar_land, ar_token = _ar_start(_pack(small_g_local))

    big_m = [m_w_in[0], m_w_out[0], m_w_ffn_gate[0], m_w_ffn_up[0], m_w_ffn_down[0]]
    big_v = [v_w_in[0], v_w_out[0], v_w_ffn_gate[0], v_w_ffn_up[0], v_w_ffn_down[0]]
    big_grads, big_d, big_nm, big_nv = [], [], [], []
    for w, m, v, n in zip(big, big_m, big_v, names):
        send, recv, pair, land, _ = reducing[n]
        pair, land = _rs_wait(pair, land, send, recv, ar_token, "rs_wait_" + n)
        own, got = _add_chips(pair, land, pc, "rs_add_chips_" + n)
        outs = _adamw_halves(w, own, got, m, v, pc, "adamw_" + n)
        for lst, o in zip((big_grads, big_d, big_nm, big_nv), outs):
            lst.append(o.reshape(w.shape))

    small_names_w = [norm1_g, q_norm_g, k_norm_g, attn_sinks, gate_ln_g, gate_ln_b, w_spatial, b_spatial,
                     out_norm_attn_g, out_norm_gate_g, norm2_g]
    small_m = [m_norm1_g, m_q_norm_g, m_k_norm_g, m_attn_sinks, m_gate_ln_g, m_gate_ln_b, m_w_spatial, m_b_spatial,
               m_out_norm_attn_g, m_out_norm_gate_g, m_norm2_g]
    small_v = [v_norm1_g, v_q_norm_g, v_k_norm_g, v_attn_sinks, v_gate_ln_g, v_gate_ln_b, v_w_spatial, v_b_spatial,
               v_out_norm_attn_g, v_out_norm_gate_g, v_norm2_g]
    shapes = [w.shape for w in small_names_w] + [loss_local.shape]
    ride = [jnp.zeros(loss_local.shape, F32)]
    ar_own, ar_land = _ar_wait(ar_own, ar_land, ar_send, ar_recv, big_nv[-1])
    me = (4 * lax.axis_index("x") + 2 * lax.axis_index("y") + lax.axis_index("c")).astype(jnp.int32).reshape(1)
    sg = _ar_sum(ar_own, ar_land, me)
    sd, snm, snv = _adamw(_pack(small_names_w + ride), sg, _pack(small_m + ride), _pack(small_v + ride), "adamw_small")
    small_g, small_d, small_nm, small_nv = (_unpack(t, shapes) for t in (sg, sd, snm, snv))
    loss = small_g[-1][0, 0]

    def order(small, bigs):
        s = list(small)
        bg = [t[None] for t in bigs]
        return [s[0], bg[0], s[1], s[2], s[3], s[4], s[5], s[6], s[7], s[8], s[9], bg[1], s[10], bg[2], bg[3], bg[4]]

    grad_x = dx.reshape(bl, seq, D)
    return (loss, grad_x, *order(small_g, big_grads), *order(small_d, big_d), *order(small_nm, big_nm),
            *order(small_nv, big_nv))
```

```python
import functools

import jax
import jax.numpy as jnp
from jax import lax
from jax.experimental import pallas as pl
from jax.experimental.pallas import tpu as pltpu

F32 = jnp.float32
BF16 = jnp.bfloat16
MESH = pl.DeviceIdType.MESH

EPS = 1e-6
HEAD_DIM = 64
N_KV_HEADS = 2
BLK = 128
N_CHIPS = 4
N_DEV = 8
NEG = -1e30

ADAM_LR = 0.001
ADAM_B1 = 0.9
ADAM_B2 = 0.999
ADAM_EPS = 1e-08
ADAM_WD = 0.01
ADAM_STEP = 10

VMEM_LIMIT = 56 * 1024 * 1024

NN = (((1,), (0,)), ((), ()))
NT = (((1,), (1,)), ((), ()))
TN = (((0,), (0,)), ((), ()))
HBM = pl.BlockSpec(memory_space=pltpu.HBM)
ANY = pl.BlockSpec(memory_space=pl.ANY)
SEM = pl.BlockSpec(memory_space=pltpu.SEMAPHORE)
EFFECT = pltpu.SideEffectType.DATAFLOW_SIDE_EFFECTING


def _dot(a, b, dn):
    return lax.dot_general(a, b, dn, preferred_element_type=F32)


def _pick(dim, pref, align=128):
    if dim <= pref:
        return dim
    t = (pref // align) * align
    while t >= align:
        if dim % t == 0:
            return t
        t -= align
    return dim


def _params(sem):
    return pltpu.CompilerParams(dimension_semantics=sem, vmem_limit_bytes=VMEM_LIMIT)


MM_CHUNK = 512
MM_TILE = 1024


def _col_chunks(tn):
    return [slice(c0, min(c0 + MM_CHUNK, tn)) for c0 in range(0, tn, MM_CHUNK)]


def _mm_body(dn, nk, has_add, has_after, *refs):
    a_ref, b_ref = refs[:2]
    add_ref = refs[2] if has_add else None
    o_ref = refs[2 + has_add + has_after]
    chunks = _col_chunks(o_ref.shape[-1])

    def dot(cols):
        return _dot(a_ref[...], b_ref[cols, :] if dn == NT else b_ref[:, cols], dn)

    def finish(cols, r):
        if add_ref is not None:
            r = r + add_ref[:, cols]
        o_ref[:, cols] = r.astype(o_ref.dtype)

    if nk == 1:
        for cols in chunks:
            finish(cols, dot(cols))
        return
    acc_ref = refs[-1]
    k = pl.program_id(2)

    @pl.when(k == 0)
    def _():
        for cols in chunks:
            acc_ref[:, cols] = dot(cols)

    if nk > 2:
        @pl.when(jnp.logical_and(k > 0, k < nk - 1))
        def _():
            for cols in chunks:
                acc_ref[:, cols] += dot(cols)

    @pl.when(k == nk - 1)
    def _():
        for cols in chunks:
            finish(cols, acc_ref[:, cols] + dot(cols))


def _matmul(a, b, mode, out_dtype, name, *, tm, tn, tk=None, add=None, out_slab=None, after=None):
    if mode == "nn":
        (M, K), N = a.shape, b.shape[1]
    elif mode == "nt":
        (M, K), N = a.shape, b.shape[0]
    else:
        (K, M), N = a.shape, b.shape[1]
    tk = K if tk is None else tk
    tm, tn, tk = _pick(M, tm), _pick(N, tn), _pick(K, tk)
    if out_slab == "c":
        tn = _pick(N // N_CHIPS, tn)
    if out_slab == "r":
        tm = _pick(M // N_CHIPS, tm)
    gm, gn, gk = M // tm, N // tn, K // tk

    if mode == "tn":
        a_spec = pl.BlockSpec((tk, tm), lambda j, i, k: (k, i))
        b_spec = pl.BlockSpec((tk, tn), lambda j, i, k: (k, j))
    else:
        a_spec = pl.BlockSpec((tm, tk), lambda j, i, k: (i, k))
        if mode == "nn":
            b_spec = pl.BlockSpec((tk, tn), lambda j, i, k: (k, j))
        else:
            b_spec = pl.BlockSpec((tn, tk), lambda j, i, k: (j, k))

    if out_slab == "c":
        per = (N // N_CHIPS) // tn
        o_spec = pl.BlockSpec((None, tm, tn), lambda j, i, k: (j // per, i, j % per))
        o_shape = jax.ShapeDtypeStruct((N_CHIPS, M, N // N_CHIPS), out_dtype)
    elif out_slab == "r":
        per = (M // N_CHIPS) // tm
        o_spec = pl.BlockSpec((None, tm, tn), lambda j, i, k: (i // per, i % per, j))
        o_shape = jax.ShapeDtypeStruct((N_CHIPS, M // N_CHIPS, N), out_dtype)
    else:
        o_spec = pl.BlockSpec((tm, tn), lambda j, i, k: (i, j))
        o_shape = jax.ShapeDtypeStruct((M, N), out_dtype)

    dn = {"nn": NN, "nt": NT, "tn": TN}[mode]
    in_specs = [a_spec, b_spec]
    args = [a, b]
    if add is not None:
        in_specs.append(pl.BlockSpec((tm, tn), lambda j, i, k: (i, j)))
        args.append(add)
    if after is not None:
        in_specs.append(ANY)
        args.append(after)
    return pl.pallas_call(
        functools.partial(_mm_body, dn, gk, add is not None, after is not None),
        name=name,
        grid=(gn, gm, gk),
        in_specs=in_specs,
        out_specs=o_spec,
        out_shape=o_shape,
        scratch_shapes=[pltpu.VMEM((tm, tn), F32)] if gk > 1 else [],
        compiler_params=_params(("parallel", "parallel", "arbitrary")),
    )(*args)


def _rms_fwd_body(tr, x_hbm, g_ref, *rest):
    h_hbm = rest[-1]
    D = g_ref.shape[1]

    def step(x_ref, h_ref):
        x = x_ref[...]
        r = lax.rsqrt(jnp.mean(x * x, axis=-1, keepdims=True) + EPS)
        h_ref[...] = (x * r * g_ref[...]).astype(h_ref.dtype)

    ins = pl.BlockSpec((tr, D), lambda i: (i, 0), pipeline_mode=pl.Buffered(3))
    outs = pl.BlockSpec((tr, D), lambda i: (i, 0))
    pltpu.emit_pipeline(step, grid=(x_hbm.shape[0] // tr,), in_specs=[ins], out_specs=[outs])(x_hbm, h_hbm)


def _rms_fwd(x, g, name, after=None):
    T, D = x.shape
    tr = _pick(T, 256, 16)
    extra = [] if after is None else [after]
    return pl.pallas_call(
        functools.partial(_rms_fwd_body, tr),
        name=name,
        in_specs=[ANY, pl.BlockSpec((1, D), lambda: (0, 0))] + [ANY] * len(extra),
        out_specs=ANY,
        out_shape=jax.ShapeDtypeStruct((T, D), BF16),
        compiler_params=pltpu.CompilerParams(vmem_limit_bytes=VMEM_LIMIT),
    )(x, g, *extra)


def _rms_bwd_body(with_bf16, tr, x_hbm, g_ref, dh_hbm, res_hbm, dx_hbm, *rest):
    dg_ref = rest[-1]
    dg_ref[...] = jnp.zeros_like(dg_ref)
    D = g_ref.shape[1]

    def step(x_ref, dh_ref, res_ref, dx_ref, *dxb):
        x = x_ref[...]
        r = lax.rsqrt(jnp.mean(x * x, axis=-1, keepdims=True) + EPS)
        xh = x * r
        dh = dh_ref[...]
        dg_ref[...] += jnp.sum(dh * xh, axis=0, keepdims=True)
        t = dh * g_ref[...]
        dx = res_ref[...] + r * (t - xh * jnp.mean(t * xh, axis=-1, keepdims=True))
        dx_ref[...] = dx
        if with_bf16:
            dxb[0][...] = dx.astype(BF16)

    ins = pl.BlockSpec((tr, D), lambda i: (i, 0), pipeline_mode=pl.Buffered(3))
    outs = pl.BlockSpec((tr, D), lambda i: (i, 0))
    n_out = 2 if with_bf16 else 1
    pltpu.emit_pipeline(step, grid=(x_hbm.shape[0] // tr,), in_specs=[ins] * 3, out_specs=[outs] * n_out)(
        x_hbm, dh_hbm, res_hbm, dx_hbm, *rest[:-1])


def _rms_bwd(x, g, dh, res, name, with_bf16):
    T, D = x.shape
    tr = _pick(T, 256, 16)
    vec = pl.BlockSpec((1, D), lambda: (0, 0))
    extra = [jax.ShapeDtypeStruct((T, D), BF16)] if with_bf16 else []
    return pl.pallas_call(
        functools.partial(_rms_bwd_body, with_bf16, tr),
        name=name,
        in_specs=[ANY, vec, ANY, ANY],
        out_specs=[ANY] + [ANY] * len(extra) + [vec],
        out_shape=[jax.ShapeDtypeStruct((T, D), F32)] + extra + [jax.ShapeDtypeStruct((1, D), F32)],
        compiler_params=pltpu.CompilerParams(vmem_limit_bytes=VMEM_LIMIT),
    )(x, g, dh, res)


def _ffn_up_body(h_ref, wg_ref, wu_ref, a_ref, b_ref, f_ref):
    for cols in _col_chunks(a_ref.shape[-1]):
        a = _dot(h_ref[...], wg_ref[:, cols], NN)
        b = _dot(h_ref[...], wu_ref[:, cols], NN)
        a_ref[:, cols] = a
        b_ref[:, cols] = b
        f_ref[:, cols] = (a * (1.0 / (1.0 + jnp.exp(-a))) * b).astype(f_ref.dtype)


def _ffn_up(h, wg, wu):
    T, D = h.shape
    F = wg.shape[1]
    tm, tn = _pick(T, MM_TILE), _pick(F, MM_CHUNK)
    hs = pl.BlockSpec((tm, D), lambda j, i: (i, 0))
    ws = pl.BlockSpec((D, tn), lambda j, i: (0, j))
    os = pl.BlockSpec((tm, tn), lambda j, i: (i, j))
    return pl.pallas_call(
        functools.partial(_ffn_up_body),
        name="ffn_up_fwd",
        grid=(F // tn, T // tm),
        in_specs=[hs, ws, ws],
        out_specs=[os, os, os],
        out_shape=[jax.ShapeDtypeStruct((T, F), F32), jax.ShapeDtypeStruct((T, F), F32),
                   jax.ShapeDtypeStruct((T, F), BF16)],
        compiler_params=_params(("parallel", "parallel")),
    )(h, wg, wu)


def _ffn_down_dx_body(dx_ref, wd_ref, a_ref, b_ref, after_ref, da_ref, db_ref):
    del after_ref
    for cols in _col_chunks(da_ref.shape[-1]):
        df = _dot(dx_ref[...], wd_ref[cols, :], NT)
        a = a_ref[:, cols]
        s = 1.0 / (1.0 + jnp.exp(-a))
        da_ref[:, cols] = (df * b_ref[:, cols] * (s * (1.0 + a * (1.0 - s)))).astype(da_ref.dtype)
        db_ref[:, cols] = (df * (a * s)).astype(db_ref.dtype)


def _ffn_down_dx(dx2b, wd, a, b, after):
    T, D = dx2b.shape
    F = wd.shape[0]
    tm, tn = _pick(T, MM_TILE // 2), _pick(F, F // N_CHIPS)
    xs = pl.BlockSpec((tm, D), lambda j, i: (i, 0))
    ws = pl.BlockSpec((tn, D), lambda j, i: (j, 0))
    os = pl.BlockSpec((tm, tn), lambda j, i: (i, j))
    return pl.pallas_call(
        functools.partial(_ffn_down_dx_body),
        name="ffn_down_dx",
        grid=(F // tn, T // tm),
        in_specs=[xs, ws, os, os, ANY],
        out_specs=[os, os],
        out_shape=[jax.ShapeDtypeStruct((T, F), BF16), jax.ShapeDtypeStruct((T, F), BF16)],
        compiler_params=_params(("parallel", "parallel")),
    )(dx2b, wd, a, b, after)


def _ffn_down_loss_body(nk, inv_d, f_ref, wd_ref, x1_ref, tgt_ref, dx2_ref, dx2b_ref, loss_ref, *scratch):
    j, i, k = pl.program_id(0), pl.program_id(1), pl.program_id(2)
    chunks = _col_chunks(dx2_ref.shape[-1])

    def dot(cols):
        return _dot(f_ref[...], wd_ref[:, cols], NN)

    @pl.when(jnp.logical_and(jnp.logical_and(j == 0, i == 0), k == 0))
    def _():
        loss_ref[...] = jnp.zeros_like(loss_ref)

    def finish(ffn_of):
        total = jnp.zeros((1, 1), F32)
        for cols in chunks:
            e = ffn_of(cols) + x1_ref[:, cols] - tgt_ref[:, cols]
            dx2 = e * inv_d
            dx2_ref[:, cols] = dx2
            dx2b_ref[:, cols] = dx2.astype(BF16)
            total = total + jnp.sum(jnp.sum(e * e, axis=-1, keepdims=True), axis=0, keepdims=True)
        loss_ref[...] += (0.5 * inv_d) * total

    if nk == 1:
        finish(dot)
        return
    acc_ref = scratch[0]

    @pl.when(k == 0)
    def _():
        for cols in chunks:
            acc_ref[:, cols] = dot(cols)

    if nk > 2:
        @pl.when(jnp.logical_and(k > 0, k < nk - 1))
        def _():
            for cols in chunks:
                acc_ref[:, cols] += dot(cols)

    @pl.when(k == nk - 1)
    def _():
        finish(lambda cols: acc_ref[:, cols] + dot(cols))


def _ffn_down_loss(f, wd, x1, tgt):
    T, F = f.shape
    D = wd.shape[1]
    tm, tn, tk = _pick(T, MM_TILE), _pick(D, MM_TILE), _pick(F, F // 2)
    gm, gn, gk = T // tm, D // tn, F // tk
    tile = pl.BlockSpec((tm, tn), lambda j, i, k: (i, j))
    return pl.pallas_call(
        functools.partial(_ffn_down_loss_body, gk, 1.0 / D),
        name="ffn_down_loss",
        grid=(gn, gm, gk),
        in_specs=[pl.BlockSpec((tm, tk), lambda j, i, k: (i, k)), pl.BlockSpec((tk, tn), lambda j, i, k: (k, j)),
                  tile, tile],
        out_specs=[tile, tile, pl.BlockSpec((1, 1), lambda j, i, k: (0, 0))],
        out_shape=[jax.ShapeDtypeStruct((T, D), F32), jax.ShapeDtypeStruct((T, D), BF16),
                   jax.ShapeDtypeStruct((1, 1), F32)],
        scratch_shapes=[pltpu.VMEM((tm, tn), F32)] if gk > 1 else [],
        compiler_params=_params(("arbitrary", "arbitrary", "arbitrary")),
    )(f, wd, x1, tgt)


def _lo_mask(shape):
    return lax.broadcasted_iota(jnp.int32, shape, len(shape) - 1) < HEAD_DIM


def _half_sums(t, lo):
    s_lo = jnp.sum(jnp.where(lo, t, 0.0), axis=-1, keepdims=True)
    s_hi = jnp.sum(jnp.where(lo, 0.0, t), axis=-1, keepdims=True)
    return jnp.where(lo, s_lo, s_hi)


def _head_rstd(t, lo):
    return lax.rsqrt(_half_sums(t * t, lo) * (1.0 / HEAD_DIM) + EPS)


def _place(t, lo, kv_head):
    if kv_head == 0:
        t_lo = jnp.where(lo, t, 0.0)
        t_hi = pltpu.roll(t_lo, HEAD_DIM, 1)
    else:
        t_hi = jnp.where(lo, 0.0, t)
        t_lo = pltpu.roll(t_hi, HEAD_DIM, 1)
    return jnp.concatenate([t_lo, t_hi], axis=0).astype(BF16)


def _unplace(c0, c1, lo):
    return jnp.where(lo, c0 + pltpu.roll(c0, HEAD_DIM, 1), c1 + pltpu.roll(c1, HEAD_DIM, 1))


def _band(kv_cur, kv_prev, kg, lo2):
    kb = jnp.concatenate([kv_prev[:, :BLK], kv_cur[:, :BLK]], axis=0)
    vb = jnp.concatenate([kv_prev[:, BLK:], kv_cur[:, BLK:]], axis=0)
    rk = _head_rstd(kb, lo2)
    kn = kb * rk * kg
    kk = [_place(kn, lo2, h) for h in range(N_KV_HEADS)]
    vv = [_place(vb, lo2, h) for h in range(N_KV_HEADS)]
    return kb, rk, kk, vv


def _score_geometry(first_i32):
    qi = lax.broadcasted_iota(jnp.int32, (BLK, 4 * BLK), 0)
    col = lax.broadcasted_iota(jnp.int32, (BLK, 4 * BLK), 1)
    kj = col & (2 * BLK - 1)
    dist = qi + BLK - kj
    valid = (dist >= 0) & (dist < BLK) & (kj >= first_i32 * BLK)
    return col, dist.astype(F32), valid


def _pair_logits(qn, kk, col, distf, valid, slope0, slope1):
    s = _dot(qn.astype(BF16), kk, NT) * (HEAD_DIM ** -0.5)
    slope = jnp.where(col < 2 * BLK, slope0, slope1)
    return jnp.where(valid, s - slope * distf, NEG)


def _pair_probs(qn, kk, col, distf, valid, slope0, slope1, sink0, sink1):
    return _softmax_halves(_pair_logits(qn, kk, col, distf, valid, slope0, slope1), sink0, sink1)


def _softmax_halves(logits, sink0, sink1):
    probs, psink = [], []
    for hh, sk in ((0, sink0), (1, sink1)):
        l = logits[:, 2 * BLK * hh:2 * BLK * (hh + 1)]
        m = jnp.maximum(jnp.max(l, axis=-1, keepdims=True), sk)
        p = jnp.exp(l - m)
        es = jnp.exp(sk - m)
        inv = 1.0 / (jnp.sum(p, axis=-1, keepdims=True) + es)
        probs.append(p * inv)
        psink.append(es * inv)
    return probs, psink


def _gelu(z, with_grad=False):
    cdf = 0.5 * (1.0 + lax.erf(z * (0.5 ** 0.5)))
    if not with_grad:
        return z * cdf
    return z * cdf, cdf + z * jnp.exp(-0.5 * z * z) * ((2.0 * jnp.pi) ** -0.5)


def _tril_w(w):
    r = lax.broadcasted_iota(jnp.int32, (BLK, BLK), 0)
    c = lax.broadcasted_iota(jnp.int32, (BLK, BLK), 1)
    return jnp.where(r >= c, w, 0.0), r >= c


def _gate_fwd_group(zu, zv, lg, lb, w, bcol, with_grad=False):
    u, v = _gelu(zu, with_grad), _gelu(zv, with_grad)
    if with_grad:
        (u, du_dz), (v, dv_dz) = u, v
    mu = jnp.mean(v, axis=-1, keepdims=True)
    vc = v - mu
    rs = lax.rsqrt(jnp.mean(vc * vc, axis=-1, keepdims=True) + EPS)
    vh = vc * rs
    vn = vh * lg + lb
    wt, tril = _tril_w(w)
    mixed = _dot(wt.astype(BF16), vn.astype(BF16), NN) + bcol
    if with_grad:
        return u, vh, rs, vn, wt, tril, mixed, du_dz, dv_dz
    return u, vh, rs, vn, wt, tril, mixed


class _Dims:
    def __init__(self, seq, attn_w, gate_w):
        self.seq, self.attn_w, self.gate_w = seq, attn_w, gate_w
        self.n_heads = attn_w // HEAD_DIM
        self.group = self.n_heads // N_KV_HEADS
        self.n_pairs = attn_w // BLK
        self.n_groups = gate_w // BLK
        self.kv_col = attn_w // (2 * BLK)
        self.u0 = attn_w + 2 * BLK
        self.v0 = self.u0 + gate_w
        self.in_w = self.v0 + gate_w
        self.slopes = [2.0 ** (-8.0 * (h + 1) / self.n_heads) for h in range(self.n_heads)]


def _mixer_fwd_body(d, sink_ref, proj_ref, kvp_ref, qg_ref, kg_ref, lg_ref, lb_ref, w_ref, b_ref, goa_ref, gog_ref,
                    ya_ref, yg_ref, y_ref, logit_scr, prob_scr):
    i = pl.program_id(0)
    first = (i % (d.seq // BLK) == 0).astype(jnp.int32)
    lo = _lo_mask((BLK, BLK))
    lo2 = _lo_mask((2 * BLK, BLK))
    kv_cur = proj_ref[:, d.attn_w:d.attn_w + 2 * BLK]
    _, _, kk, vv = _band(kv_cur, kvp_ref[...], kg_ref[...], lo2)
    col, distf, valid = _score_geometry(first)
    qg = qg_ref[...]
    for j in range(d.n_pairs):
        h0, h1 = 2 * j, 2 * j + 1
        q2 = proj_ref[:, BLK * j:BLK * (j + 1)]
        qn = q2 * _head_rstd(q2, lo) * qg
        logit_scr[j] = _pair_logits(qn, kk[h0 // d.group], col, distf, valid, d.slopes[h0], d.slopes[h1])
    for j in range(d.n_pairs):
        probs, _ = _softmax_halves(logit_scr[j], sink_ref[0, 2 * j], sink_ref[0, 2 * j + 1])
        prob_scr[j] = jnp.concatenate(probs, axis=1).astype(BF16)
    for g in range(d.n_groups):
        zu = proj_ref[:, d.u0 + BLK * g:d.u0 + BLK * (g + 1)]
        zv = proj_ref[:, d.v0 + BLK * g:d.v0 + BLK * (g + 1)]
        u, _, _, _, _, _, mixed = _gate_fwd_group(zu, zv, lg_ref[g:g + 1, :], lb_ref[g:g + 1, :], w_ref[g], b_ref[g])
        yg_ref[:, BLK * g:BLK * (g + 1)] = u * mixed
    for j in range(d.n_pairs):
        ya_ref[:, BLK * j:BLK * (j + 1)] = _dot(prob_scr[j], vv[2 * j // d.group], NN)
    ya = ya_ref[...]
    ra = lax.rsqrt(jnp.mean(ya * ya, axis=-1, keepdims=True) + EPS)
    y_ref[:, :d.attn_w] = (ya * ra * goa_ref[...]).astype(y_ref.dtype)
    yg = yg_ref[...]
    rg = lax.rsqrt(jnp.mean(yg * yg, axis=-1, keepdims=True) + EPS)
    y_ref[:, d.attn_w:] = (yg * rg * gog_ref[...]).astype(y_ref.dtype)


def _mixer_specs(d, T):
    row = lambda w: pl.BlockSpec((BLK, w), lambda i: (i, 0))
    const2 = lambda a: pl.BlockSpec(a.shape, lambda i: (0, 0))
    const3 = lambda a: pl.BlockSpec(a.shape, lambda i: (0, 0, 0))
    kv_prev = pl.BlockSpec((BLK, 2 * BLK), lambda i: (jnp.maximum(i - 1, 0), d.kv_col))
    return row, const2, const3, kv_prev


def _mixer_fwd(d, proj, sinks, qg2, kg2, lg, lb, wsp, bcol, goa, gog):
    T = proj.shape[0]
    row, const2, const3, kv_prev = _mixer_specs(d, T)
    return pl.pallas_call(
        functools.partial(_mixer_fwd_body, d),
        name="mixer_fwd",
        grid=(T // BLK,),
        in_specs=[pl.BlockSpec(memory_space=pltpu.SMEM), row(d.in_w), kv_prev, const2(qg2), const2(kg2),
                  const2(lg), const2(lb), const3(wsp), const3(bcol), const2(goa), const2(gog)],
        out_specs=[row(d.attn_w), row(d.gate_w), row(d.attn_w + d.gate_w)],
        out_shape=[jax.ShapeDtypeStruct((T, d.attn_w), F32), jax.ShapeDtypeStruct((T, d.gate_w), F32),
                   jax.ShapeDtypeStruct((T, d.attn_w + d.gate_w), BF16)],
        scratch_shapes=[pltpu.VMEM((d.n_pairs, BLK, 4 * BLK), F32), pltpu.VMEM((d.n_pairs, BLK, 4 * BLK), BF16)],
        compiler_params=_params(("parallel",)),
    )(sinks, proj, proj, qg2, kg2, lg, lb, wsp, bcol, goa, gog)


def _mixer_bwd_body(d, sink_ref, proj_ref, kvp_ref, ya_ref, yg_ref, dy_ref, qg_ref, kg_ref, lg_ref, lb_ref, w_ref,
                    b_ref, goa_ref, gog_ref,
                    dproj_ref, dkv_ref, dqg_ref, dkg_ref, dsk_ref, dlg_ref, dlb_ref, dw_ref, db_ref, dgoa_ref,
                    dgog_ref):
    i = pl.program_id(0)

    @pl.when(i == 0)
    def _():
        for r in (dqg_ref, dkg_ref, dsk_ref, dlg_ref, dlb_ref, dw_ref, db_ref, dgoa_ref, dgog_ref):
            r[...] = jnp.zeros_like(r)

    first = (i % (d.seq // BLK) == 0).astype(jnp.int32)
    lo = _lo_mask((BLK, BLK))
    lo2 = _lo_mask((2 * BLK, BLK))
    lane_row = lax.broadcasted_iota(jnp.int32, (1, BLK), 1)

    ya = ya_ref[...]
    ra = lax.rsqrt(jnp.mean(ya * ya, axis=-1, keepdims=True) + EPS)
    yah = ya * ra
    dyn = dy_ref[:, :d.attn_w]
    dgoa_ref[...] += jnp.sum(dyn * yah, axis=0, keepdims=True)
    t = dyn * goa_ref[...]
    dya = ra * (t - yah * jnp.mean(t * yah, axis=-1, keepdims=True))
    yg = yg_ref[...]
    rg = lax.rsqrt(jnp.mean(yg * yg, axis=-1, keepdims=True) + EPS)
    ygh = yg * rg
    dyn = dy_ref[:, d.attn_w:]
    dgog_ref[...] += jnp.sum(dyn * ygh, axis=0, keepdims=True)
    t = dyn * gog_ref[...]
    dyg = rg * (t - ygh * jnp.mean(t * ygh, axis=-1, keepdims=True))

    for g in range(d.n_groups):
        ucols = slice(d.u0 + BLK * g, d.u0 + BLK * (g + 1))
        vcols = slice(d.v0 + BLK * g, d.v0 + BLK * (g + 1))
        zu = proj_ref[:, ucols]
        zv = proj_ref[:, vcols]
        lg = lg_ref[g:g + 1, :]
        u, vh, rs, vn, wt, tril, mixed, du_dz, dv_dz = _gate_fwd_group(
            zu, zv, lg, lb_ref[g:g + 1, :], w_ref[g], b_ref[g], with_grad=True)
        dyg_g = dyg[:, BLK * g:BLK * (g + 1)]
        du = dyg_g * mixed
        dmix = dyg_g * u
        dmb = dmix.astype(BF16)
        db_ref[g:g + 1, :] += jnp.sum(jnp.transpose(dmix), axis=0, keepdims=True)
        dw_ref[g] += jnp.where(tril, _dot(dmb, vn.astype(BF16), NT), 0.0)
        dvn = _dot(wt.astype(BF16), dmb, TN)
        dlg_ref[g:g + 1, :] += jnp.sum(dvn * vh, axis=0, keepdims=True)
        dlb_ref[g:g + 1, :] += jnp.sum(dvn, axis=0, keepdims=True)
        dvh = dvn * lg
        dv = rs * (dvh - jnp.mean(dvh, axis=-1, keepdims=True) - vh * jnp.mean(dvh * vh, axis=-1, keepdims=True))
        dproj_ref[:, ucols] = (du * du_dz).astype(dproj_ref.dtype)
        dproj_ref[:, vcols] = (dv * dv_dz).astype(dproj_ref.dtype)

    kv_cur = proj_ref[:, d.attn_w:d.attn_w + 2 * BLK]
    kg = kg_ref[...]
    kb, rk, kk, vv = _band(kv_cur, kvp_ref[...], kg, lo2)
    col, distf, valid = _score_geometry(first)
    qg = qg_ref[...]
    ck = [jnp.zeros((BLK, 2 * BLK), F32) for _ in range(N_KV_HEADS)]
    cv = [jnp.zeros((BLK, 2 * BLK), F32) for _ in range(N_KV_HEADS)]
    lo_rows = lax.broadcasted_iota(jnp.int32, (BLK, 2 * BLK), 0) < HEAD_DIM
    dsk = jnp.zeros((1, BLK), F32)
    dqg = jnp.zeros((1, BLK), F32)
    for j in range(d.n_pairs):
        h0, h1 = 2 * j, 2 * j + 1
        kh = h0 // d.group
        cols = slice(BLK * j, BLK * (j + 1))
        q2 = proj_ref[:, cols]
        rq = _head_rstd(q2, lo)
        qh = q2 * rq
        qn = qh * qg
        probs, psink = _pair_probs(qn, kk[kh], col, distf, valid, d.slopes[h0], d.slopes[h1],
                                   sink_ref[0, h0], sink_ref[0, h1])
        do2 = dya[:, cols]
        prod = do2 * ya[:, cols]
        delta = (jnp.sum(jnp.where(lo, prod, 0.0), axis=-1, keepdims=True),
                 jnp.sum(jnp.where(lo, 0.0, prod), axis=-1, keepdims=True))
        do2b = do2.astype(BF16)
        dp = _dot(do2b, vv[kh], NT)
        ds = []
        for hh in (0, 1):
            ds.append(probs[hh] * (dp[:, 2 * BLK * hh:2 * BLK * (hh + 1)] - delta[hh]))
            dsink = -jnp.sum(psink[hh] * delta[hh], axis=0, keepdims=True)
            dsk = dsk + jnp.where(lane_row == (h0 + hh), dsink, 0.0)
        dsb = (jnp.concatenate(ds, axis=1) * (HEAD_DIM ** -0.5)).astype(BF16)
        pb = jnp.concatenate(probs, axis=1).astype(BF16)
        qnb = qn.astype(BF16)
        dqn = _dot(dsb, kk[kh], NN)
        dkk = _dot(qnb, dsb, TN)
        dvv = _dot(do2b, pb, TN)
        ck[kh] = ck[kh] + jnp.where(lo_rows, dkk[:, :2 * BLK], 0.0) + jnp.where(lo_rows, 0.0, dkk[:, 2 * BLK:])
        cv[kh] = cv[kh] + jnp.where(lo_rows, dvv[:, :2 * BLK], 0.0) + jnp.where(lo_rows, 0.0, dvv[:, 2 * BLK:])
        dqg = dqg + jnp.sum(dqn * qh, axis=0, keepdims=True)
        t = dqn * qg
        dq2 = rq * (t - qh * (_half_sums(t * qh, lo) * (1.0 / HEAD_DIM)))
        dproj_ref[:, cols] = dq2.astype(dproj_ref.dtype)
    dsk_ref[...] += dsk
    dqg_ref[...] += dqg
    dkn = _unplace(jnp.transpose(ck[0]), jnp.transpose(ck[1]), lo2)
    dvb = _unplace(jnp.transpose(cv[0]), jnp.transpose(cv[1]), lo2)
    khat = kb * rk
    dkg_ref[...] += jnp.sum(dkn * khat, axis=0, keepdims=True)
    t = dkn * kg
    dkb = rk * (t - khat * (_half_sums(t * khat, lo2) * (1.0 / HEAD_DIM)))
    rows_cur = pl.ds(pl.multiple_of(i * BLK, BLK), BLK)
    rows_prev = pl.ds(pl.multiple_of(jnp.maximum(i - 1, 0) * BLK, BLK), BLK)
    dkv_ref[rows_cur, :] = jnp.concatenate([dkb[BLK:], dvb[BLK:]], axis=1)
    dkv_ref[rows_prev, :] += jnp.concatenate([dkb[:BLK], dvb[:BLK]], axis=1)
    dproj_ref[:, d.attn_w:d.attn_w + 2 * BLK] = jnp.zeros((BLK, 2 * BLK), dproj_ref.dtype)


def _mixer_bwd(d, proj, ya, yg, dy, sinks, qg2, kg2, lg, lb, wsp, bcol, goa, gog):
    T = proj.shape[0]
    row, const2, const3, kv_prev = _mixer_specs(d, T)
    acc2 = lambda s: pl.BlockSpec(s, lambda i: (0, 0))
    G = d.n_groups
    out_shapes = [((T, d.in_w), BF16), ((T, 2 * BLK), F32), ((1, BLK), F32), ((1, BLK), F32), ((1, BLK), F32),
                  ((G, BLK), F32), ((G, BLK), F32), ((G, BLK, BLK), F32), ((G, BLK), F32),
                  ((1, d.attn_w), F32), ((1, d.gate_w), F32)]
    out_specs = [row(d.in_w)] + [acc2(s) for s, _ in out_shapes[1:7]] + \
                [pl.BlockSpec((G, BLK, BLK), lambda i: (0, 0, 0))] + [acc2(s) for s, _ in out_shapes[8:]]
    return pl.pallas_call(
        functools.partial(_mixer_bwd_body, d),
        name="mixer_bwd",
        grid=(T // BLK,),
        in_specs=[pl.BlockSpec(memory_space=pltpu.SMEM), row(d.in_w), kv_prev, row(d.attn_w), row(d.gate_w),
                  row(d.attn_w + d.gate_w), const2(qg2), const2(kg2), const2(lg), const2(lb), const3(wsp),
                  const3(bcol), const2(goa), const2(gog)],
        out_specs=out_specs,
        out_shape=[jax.ShapeDtypeStruct(s, t) for s, t in out_shapes],
        compiler_params=_params(("arbitrary",)),
    )(sinks, proj, proj, ya, yg, dy, qg2, kg2, lg, lb, wsp, bcol, goa, gog)


def _put_kv_body(dkv_ref, dproj_in_ref, dproj_ref):
    del dproj_in_ref
    dproj_ref[...] = dkv_ref[...].astype(dproj_ref.dtype)


def _put_kv(d, dproj, dkv):
    T = dproj.shape[0]
    tr = _pick(T, 1024, 16)
    return pl.pallas_call(
        functools.partial(_put_kv_body),
        name="put_kv",
        grid=(T // tr,),
        in_specs=[pl.BlockSpec((tr, 2 * BLK), lambda i: (i, 0)), pl.BlockSpec(memory_space=pl.ANY)],
        out_specs=pl.BlockSpec((tr, 2 * BLK), lambda i: (i, d.kv_col)),
        out_shape=jax.ShapeDtypeStruct(dproj.shape, dproj.dtype),
        input_output_aliases={1: 0},
        compiler_params=_params(("parallel",)),
    )(dkv, dproj)


def _add_pair_body(pc_ref, own_ref, got_ref, o_ref):
    del pc_ref
    o_ref[...] = (own_ref[...].astype(F32) + got_ref[...].astype(F32)).astype(o_ref.dtype)


def _add_pair(g4, got, pc, name):
    n, _, h, C = g4.shape
    tr = _pick(h, 512, 16)
    return pl.pallas_call(
        functools.partial(_add_pair_body),
        name=name,
        grid_spec=pltpu.PrefetchScalarGridSpec(
            num_scalar_prefetch=1,
            grid=(n, h // tr),
            in_specs=[pl.BlockSpec((None, None, tr, C), lambda q, i, pc: (q, pc[1], i, 0)),
                      pl.BlockSpec((None, tr, C), lambda q, i, pc: (q, i, 0))],
            out_specs=pl.BlockSpec((None, tr, C), lambda q, i, pc: (q, i, 0)),
        ),
        out_shape=jax.ShapeDtypeStruct((n, h, C), g4.dtype),
        compiler_params=_params(("parallel", "parallel")),
    )(pc, g4, got)


def _adamw_update(w, g, m, v):
    m = ADAM_B1 * m + (1.0 - ADAM_B1) * g
    v = ADAM_B2 * v + (1.0 - ADAM_B2) * (g * g)
    m_hat = m / (1.0 - ADAM_B1 ** ADAM_STEP)
    v_hat = v / (1.0 - ADAM_B2 ** ADAM_STEP)
    return -ADAM_LR * (m_hat / (jnp.sqrt(v_hat) + ADAM_EPS) + ADAM_WD * w), m, v


def _adamw_body(w_ref, g_ref, m_ref, v_ref, d_ref, nm_ref, nv_ref):
    d_ref[...], nm_ref[...], nv_ref[...] = _adamw_update(w_ref[...], g_ref[...], m_ref[...], v_ref[...])


def _adamw(w, g, m, v, name):
    R, C = w.shape
    tr = _pick(R, 512, 8)
    blk = pl.BlockSpec((tr, C), lambda i: (i, 0))
    return pl.pallas_call(
        functools.partial(_adamw_body),
        name=name,
        grid=(R // tr,),
        in_specs=[blk] * 4,
        out_specs=[blk] * 3,
        out_shape=[jax.ShapeDtypeStruct((R, C), F32)] * 3,
        compiler_params=_params(("parallel",)),
    )(w, g, m, v)


def _adamw_step_body(w_ref, gin_ref, m_ref, v_ref, g_ref, d_ref, nm_ref, nv_ref):
    g = gin_ref[...].astype(F32)
    g_ref[...] = g
    d_ref[...], nm_ref[...], nv_ref[...] = _adamw_update(w_ref[...], g, m_ref[...], v_ref[...])


def _adamw_halves_body(tr, steps, pc_ref, w_ref, own_ref, got_ref, m_ref, v_ref, g_ref, d_ref, nm_ref, nv_ref):
    C = own_ref.shape[1]
    ins = pl.BlockSpec((tr, C), lambda i: (i, 0), pipeline_mode=pl.Buffered(3))
    outs = pl.BlockSpec((tr, C), lambda i: (i, 0))
    pipe = pltpu.emit_pipeline(_adamw_step_body, grid=(steps,), in_specs=[ins] * 4, out_specs=[outs] * 4)
    c = pc_ref[1]
    for hh, gin in ((c, own_ref), (1 - c, got_ref)):
        pipe(w_ref.at[hh], gin, m_ref.at[hh], v_ref.at[hh],
             g_ref.at[hh], d_ref.at[hh], nm_ref.at[hh], nv_ref.at[hh])


def _adamw_halves(w, own, got, m, v, pc, name):
    h, C = own.shape
    tr = _pick(h, max(8, 393216 // C), 8)
    return pl.pallas_call(
        functools.partial(_adamw_halves_body, tr, h // tr),
        name=name,
        in_specs=[pl.BlockSpec(memory_space=pltpu.SMEM)] + [ANY] * 5,
        out_specs=[ANY] * 4,
        out_shape=[jax.ShapeDtypeStruct((2, h, C), F32)] * 4,
        compiler_params=pltpu.CompilerParams(vmem_limit_bytes=VMEM_LIMIT),
    )(pc, w.reshape(2, h, C), own, got, m.reshape(2, h, C), v.reshape(2, h, C))


def _me():
    x, y, c = lax.axis_index("x"), lax.axis_index("y"), lax.axis_index("c")
    chips = [(1 - x, y), (x, 1 - y), (1 - x, 1 - y)]
    return x, y, c, chips


def _cast_into_body(pc_ref, w_ref, o_ref):
    del pc_ref
    o_ref[...] = w_ref[...].astype(o_ref.dtype)


def _cast_into(w, pc, name, side_by_side=False):
    Rs, C = w.shape
    h = Rs // 2
    tr = _pick(h, 512, 16)
    if side_by_side:
        out_spec = pl.BlockSpec((None, tr, C), lambda hh, i, pc: (hh, i, pc[0]))
        out_shape = jax.ShapeDtypeStruct((2, h, N_CHIPS * C), BF16)
    else:
        out_spec = pl.BlockSpec((None, None, tr, C), lambda hh, i, pc: (pc[0], hh, i, 0))
        out_shape = jax.ShapeDtypeStruct((N_CHIPS, 2, h, C), BF16)
    return pl.pallas_call(
        functools.partial(_cast_into_body),
        name=name,
        grid_spec=pltpu.PrefetchScalarGridSpec(
            num_scalar_prefetch=1,
            grid=(2, h // tr),
            in_specs=[pl.BlockSpec((None, tr, C), lambda hh, i, pc: (hh, i, 0))],
            out_specs=out_spec,
        ),
        out_shape=out_shape,
        compiler_params=_params(("parallel", "parallel")),
    )(pc, w.reshape(2, h, C))


MAX_PIECES = 4


def _send_tile_to_sibling(src_of, dst_of, tr, dst_total, send_sems, recv_sem, last):
    x, y, c, _ = _me()
    pieces = MAX_PIECES if tr % (16 * MAX_PIECES) == 0 else (2 if tr % 32 == 0 else 1)
    n = tr // pieces
    copies = [pltpu.make_async_remote_copy(src_ref=src_of(k * n, n), dst_ref=dst_of(k * n, n), send_sem=send_sems.at[k],
                                           recv_sem=recv_sem, device_id=(x, y, 1 - c), device_id_type=MESH)
              for k in range(pieces)]
    for cp in copies:
        cp.start()
    for cp in copies:
        cp.wait_send()

    @pl.when(last)
    def _():
        pltpu.make_async_remote_copy(src_ref=dst_total, dst_ref=dst_total, send_sem=send_sems.at[0], recv_sem=recv_sem,
                                     device_id=(x, y, 1 - c), device_id_type=MESH).wait_recv()


TILE_SEMS = [pltpu.SemaphoreType.DMA((MAX_PIECES,)), pltpu.SemaphoreType.DMA(())]


def _ag_pair_body(tr, n_i, pc_ref, tile_ref, buf_ref, send_sem, recv_sem):
    j, i = pl.program_id(0), pl.program_id(1)
    q = pc_ref[0] ^ (j + 1)
    c = pc_ref[1]
    r_tile = pl.multiple_of(i * tr, tr)
    last = jnp.logical_and(j == N_CHIPS - 2, i == n_i - 1)
    if len(buf_ref.shape) == 4:
        _send_tile_to_sibling(lambda r0, n: tile_ref.at[:, :, pl.ds(r0, n)],
                              lambda r0, n: buf_ref.at[pl.ds(q, 1), pl.ds(c, 1), pl.ds(r_tile + r0, n)], tr,
                              buf_ref.at[pl.ds(0, N_CHIPS - 1), 0], send_sem, recv_sem, last)
    else:
        cs = buf_ref.shape[2] // N_CHIPS
        cols = pl.ds(pl.multiple_of(q * cs, BLK), cs)
        _send_tile_to_sibling(lambda r0, n: tile_ref.at[:, pl.ds(r0, n)],
                              lambda r0, n: buf_ref.at[pl.ds(c, 1), pl.ds(r_tile + r0, n), cols], tr,
                              buf_ref.at[0, :, pl.ds(0, (N_CHIPS - 1) * cs)], send_sem, recv_sem, last)


def _ag_pair(buf, pc, name):
    if len(buf.shape) == 4:
        _, _, h, C = buf.shape
        tile = lambda tr: pl.BlockSpec((1, 1, tr, C), lambda j, i, pc: (pc[0] ^ (j + 1), pc[1], i, 0))
    else:
        _, h, C = buf.shape
        tile = lambda tr: pl.BlockSpec((1, tr, C // N_CHIPS), lambda j, i, pc: (pc[1], i, pc[0] ^ (j + 1)))
    tr = _pick(h, 512, 16)
    return pl.pallas_call(
        functools.partial(_ag_pair_body, tr, h // tr),
        name=name,
        grid_spec=pltpu.PrefetchScalarGridSpec(
            num_scalar_prefetch=1,
            grid=(N_CHIPS - 1, h // tr),
            in_specs=[tile(tr)],
            out_specs=HBM,
            scratch_shapes=TILE_SEMS,
        ),
        out_shape=jax.ShapeDtypeStruct(buf.shape, buf.dtype),
        input_output_aliases={1: 0},
        compiler_params=_params(("arbitrary", "arbitrary")),
    )(pc, buf)


def _swap_halves_body(tr, n_q, n_i, pc_ref, tile_ref, got_ref, send_sem, recv_sem):
    del pc_ref
    q, i = pl.program_id(0), pl.program_id(1)
    r_tile = pl.multiple_of(i * tr, tr)
    _send_tile_to_sibling(lambda r0, n: tile_ref.at[:, :, pl.ds(r0, n)],
                          lambda r0, n: got_ref.at[pl.ds(q, 1), :, pl.ds(r_tile + r0, n)], tr, got_ref, send_sem, recv_sem,
                          jnp.logical_and(q == n_q - 1, i == n_i - 1))


def _swap_halves(g4, pc, name):
    n, _, h, C = g4.shape
    tr = _pick(h, 512, 16)
    return pl.pallas_call(
        functools.partial(_swap_halves_body, tr, n, h // tr),
        name=name,
        grid_spec=pltpu.PrefetchScalarGridSpec(
            num_scalar_prefetch=1,
            grid=(n, h // tr),
            in_specs=[pl.BlockSpec((1, 1, tr, C), lambda q, i, pc: (q, 1 - pc[1], i, 0))],
            out_specs=HBM,
            scratch_shapes=TILE_SEMS,
        ),
        out_shape=jax.ShapeDtypeStruct((n, 1, h, C), g4.dtype),
        compiler_params=_params(("arbitrary", "arbitrary")),
    )(pc, g4).reshape(n, h, C)


def _ici_copy(src, dst, send_sems, recv_sems, j, chip, c):
    return pltpu.make_async_remote_copy(src_ref=src, dst_ref=dst, send_sem=send_sems.at[j], recv_sem=recv_sems.at[j],
                                        device_id=(chip[0], chip[1], c), device_id_type=MESH)


def _token_spec():
    return jax.ShapeDtypeStruct((8, BLK), F32), pl.BlockSpec(memory_space=pltpu.VMEM)


def _slab(buf_ref, q, c):
    if len(buf_ref.shape) == 4:
        return buf_ref.at[q, c]
    cs = buf_ref.shape[2] // N_CHIPS
    return buf_ref.at[c, :, pl.ds(pl.multiple_of(q * cs, BLK), cs)]


def _ag_start_body(both_cores, buf_ref, after_ref, send_sems, recv_sems, buf_thru, token_ref):
    del after_ref, buf_thru
    x, y, c, chips = _me()
    mine = _slab(buf_ref, 2 * x + y, c)
    for j, chip in enumerate(chips):
        _ici_copy(mine, mine, send_sems, recv_sems, j, chip, c).start()
    if both_cores:
        for j, chip in enumerate(chips):
            _ici_copy(mine, mine, send_sems, recv_sems, N_CHIPS - 1 + j, chip, 1 - c).start()
    token_ref[...] = jnp.zeros_like(token_ref)


def _ag_start(buf, after, name, both_cores=False):
    tok_shape, tok_spec = _token_spec()
    sems = pltpu.SemaphoreType.DMA(((N_CHIPS - 1) * (2 if both_cores else 1),))
    return pl.pallas_call(
        functools.partial(_ag_start_body, both_cores),
        name=name,
        in_specs=[HBM, ANY],
        out_specs=[SEM, SEM, HBM, tok_spec],
        out_shape=[sems, sems, pltpu.HBM(buf.shape, buf.dtype), tok_shape],
        input_output_aliases={0: 2},
        compiler_params=pltpu.CompilerParams(has_side_effects=EFFECT),
    )(pltpu.with_memory_space_constraint(buf, pltpu.HBM), after)


def _ag_wait_body(both_cores, buf_ref, send_sems, recv_sems, after_ref, buf_out):
    del after_ref, buf_out
    x, y, c, chips = _me()
    mine = _slab(buf_ref, 2 * x + y, c)
    for j, chip in enumerate(chips):
        theirs = _slab(buf_ref, 2 * chip[0] + chip[1], c)
        _ici_copy(mine, mine, send_sems, recv_sems, j, chip, c).wait_send()
        _ici_copy(theirs, theirs, send_sems, recv_sems, j, chip, c).wait_recv()
    if both_cores:
        for j, chip in enumerate(chips):
            theirs = _slab(buf_ref, 2 * chip[0] + chip[1], 1 - c)
            _ici_copy(mine, mine, send_sems, recv_sems, N_CHIPS - 1 + j, chip, 1 - c).wait_send()
            _ici_copy(theirs, theirs, send_sems, recv_sems, N_CHIPS - 1 + j, chip, 1 - c).wait_recv()


def _ag_wait(buf, send_sems, recv_sems, after, name, both_cores=False):
    return pl.pallas_call(
        functools.partial(_ag_wait_body, both_cores),
        name=name,
        in_specs=[HBM, SEM, SEM, ANY],
        out_specs=HBM,
        out_shape=pltpu.HBM(buf.shape, buf.dtype),
        input_output_aliases={0: 0},
        compiler_params=pltpu.CompilerParams(has_side_effects=EFFECT),
    )(buf, send_sems, recv_sems, after)


def _rs_start_body(pair_ref, land_ref, after_ref, send_sems, recv_sems, pair_thru, land_thru, token_ref):
    del after_ref, pair_thru, land_thru
    x, y, c, chips = _me()
    for j, chip in enumerate(chips):
        _ici_copy(pair_ref.at[2 * chip[0] + chip[1]], land_ref.at[j], send_sems, recv_sems, j, chip, c).start()
    token_ref[...] = jnp.zeros_like(token_ref)


def _rs_start(pair, after, name):
    n, h, C = pair.shape
    tok_shape, tok_spec = _token_spec()
    sems = pltpu.SemaphoreType.DMA((N_CHIPS - 1,))
    land = pltpu.with_memory_space_constraint(lax.empty((N_CHIPS - 1, h, C), pair.dtype), pltpu.HBM)
    return pl.pallas_call(
        functools.partial(_rs_start_body),
        name=name,
        in_specs=[HBM, HBM, ANY],
        out_specs=[SEM, SEM, HBM, HBM, tok_spec],
        out_shape=[sems, sems, pltpu.HBM(pair.shape, pair.dtype), pltpu.HBM(land.shape, land.dtype), tok_shape],
        input_output_aliases={0: 2, 1: 3},
        compiler_params=pltpu.CompilerParams(has_side_effects=EFFECT),
    )(pltpu.with_memory_space_constraint(pair, pltpu.HBM), land, after)


def _rs_wait_body(pair_ref, land_ref, send_sems, recv_sems, after_ref, pair_out, land_out):
    del after_ref, pair_out, land_out
    x, y, c, chips = _me()
    for j, chip in enumerate(chips):
        _ici_copy(pair_ref.at[0], land_ref.at[j], send_sems, recv_sems, j, chip, c).wait_send()
        _ici_copy(pair_ref.at[0], land_ref.at[j], send_sems, recv_sems, j, chip, c).wait_recv()


def _rs_wait(pair, land, send_sems, recv_sems, after, name):
    return pl.pallas_call(
        functools.partial(_rs_wait_body),
        name=name,
        in_specs=[HBM, HBM, SEM, SEM, ANY],
        out_specs=[HBM, HBM],
        out_shape=[pltpu.HBM(pair.shape, pair.dtype), pltpu.HBM(land.shape, land.dtype)],
        input_output_aliases={0: 0, 1: 1},
        compiler_params=pltpu.CompilerParams(has_side_effects=EFFECT),
    )(pair, land, send_sems, recv_sems, after)


def _swap_copy(g4_ref, got_ref, send_sem, recv_sem):
    x, y, c, _ = _me()
    return pltpu.make_async_remote_copy(src_ref=g4_ref.at[:, 1 - c], dst_ref=got_ref, send_sem=send_sem,
                                        recv_sem=recv_sem, device_id=(x, y, 1 - c), device_id_type=MESH)


def _swap_start_body(g4_ref, got_ref, send_sem, recv_sem, g4_thru, got_thru, token_ref):
    del g4_thru, got_thru
    _swap_copy(g4_ref, got_ref, send_sem, recv_sem).start()
    token_ref[...] = jnp.zeros_like(token_ref)


def _swap_start(g4, name):
    n, _, h, C = g4.shape
    tok_shape, tok_spec = _token_spec()
    sem = pltpu.SemaphoreType.DMA(())
    got = pltpu.with_memory_space_constraint(lax.empty((n, h, C), g4.dtype), pltpu.HBM)
    return pl.pallas_call(
        functools.partial(_swap_start_body),
        name=name,
        in_specs=[HBM, HBM],
        out_specs=[SEM, SEM, HBM, HBM, tok_spec],
        out_shape=[sem, sem, pltpu.HBM(g4.shape, g4.dtype), pltpu.HBM(got.shape, got.dtype), tok_shape],
        input_output_aliases={0: 2, 1: 3},
        compiler_params=pltpu.CompilerParams(has_side_effects=EFFECT),
    )(pltpu.with_memory_space_constraint(g4, pltpu.HBM), got)


def _swap_wait_body(g4_ref, got_ref, send_sem, recv_sem, after_ref, g4_out, got_out):
    del after_ref, g4_out, got_out
    cp = _swap_copy(g4_ref, got_ref, send_sem, recv_sem)
    cp.wait_send()
    cp.wait_recv()


def _swap_wait(g4, got, send_sem, recv_sem, after, name):
    return pl.pallas_call(
        functools.partial(_swap_wait_body),
        name=name,
        in_specs=[HBM, HBM, SEM, SEM, ANY],
        out_specs=[HBM, HBM],
        out_shape=[pltpu.HBM(g4.shape, g4.dtype), pltpu.HBM(got.shape, got.dtype)],
        input_output_aliases={0: 0, 1: 1},
        compiler_params=pltpu.CompilerParams(has_side_effects=EFFECT),
    )(g4, got, send_sem, recv_sem, after)


def _add_chips_body(tr, n_i, pc_ref, own_ref, l0_ref, l1_ref, l2_ref, o_ref, got_ref, send_sems, recv_sem):
    del pc_ref
    i = pl.program_id(0)
    r = own_ref[...].astype(F32) + l0_ref[...].astype(F32)
    o_ref[...] = (r + l1_ref[...].astype(F32) + l2_ref[...].astype(F32)).astype(o_ref.dtype)
    r_tile = pl.multiple_of(i * tr, tr)
    _send_tile_to_sibling(lambda r0, n: o_ref.at[pl.ds(r0, n)], lambda r0, n: got_ref.at[pl.ds(r_tile + r0, n)], tr,
                          got_ref, send_sems, recv_sem, i == n_i - 1)


def _add_chips(pair, land, pc, name):
    _, h, C = pair.shape
    tr = _pick(h, 256, 16)
    slot = lambda j: pl.BlockSpec((None, tr, C), lambda i, pc: (j, i, 0))
    return pl.pallas_call(
        functools.partial(_add_chips_body, tr, h // tr),
        name=name,
        grid_spec=pltpu.PrefetchScalarGridSpec(
            num_scalar_prefetch=1,
            grid=(h // tr,),
            in_specs=[pl.BlockSpec((None, tr, C), lambda i, pc: (pc[0], i, 0)), slot(0), slot(1), slot(2)],
            out_specs=[pl.BlockSpec((tr, C), lambda i, pc: (i, 0)), HBM],
            scratch_shapes=TILE_SEMS,
        ),
        out_shape=[jax.ShapeDtypeStruct((h, C), pair.dtype), jax.ShapeDtypeStruct((h, C), pair.dtype)],
        compiler_params=_params(("arbitrary",)),
    )(pc, pair, land, land, land)


def _peer(r):
    x, y, c, _ = _me()
    return (x ^ ((r >> 2) & 1), y ^ ((r >> 1) & 1), c ^ (r & 1))


def _ar_start_body(x_ref, land_ref, send_sems, recv_sems, x_thru, land_thru, token_ref):
    del x_thru, land_thru
    for r in range(1, N_DEV):
        pltpu.make_async_remote_copy(src_ref=x_ref, dst_ref=land_ref.at[r - 1], send_sem=send_sems.at[r - 1],
                                     recv_sem=recv_sems.at[r - 1], device_id=_peer(r), device_id_type=MESH).start()
    token_ref[...] = jnp.zeros_like(token_ref)


def _ar_start(packed):
    tok_shape, tok_spec = _token_spec()
    sems = pltpu.SemaphoreType.DMA((N_DEV - 1,))
    land = pltpu.with_memory_space_constraint(lax.empty((N_DEV - 1,) + packed.shape, packed.dtype), pltpu.HBM)
    return pl.pallas_call(
        functools.partial(_ar_start_body),
        name="ar_start",
        in_specs=[HBM, HBM],
        out_specs=[SEM, SEM, HBM, HBM, tok_spec],
        out_shape=[sems, sems, pltpu.HBM(packed.shape, packed.dtype), pltpu.HBM(land.shape, land.dtype), tok_shape],
        input_output_aliases={0: 2, 1: 3},
        compiler_params=pltpu.CompilerParams(has_side_effects=EFFECT),
    )(pltpu.with_memory_space_constraint(packed, pltpu.HBM), land)


def _ar_wait_body(x_ref, land_ref, send_sems, recv_sems, after_ref, x_out, land_out):
    del after_ref, x_out, land_out
    for r in range(1, N_DEV):
        cp = pltpu.make_async_remote_copy(src_ref=x_ref, dst_ref=land_ref.at[r - 1], send_sem=send_sems.at[r - 1],
                                          recv_sem=recv_sems.at[r - 1], device_id=_peer(r), device_id_type=MESH)
        cp.wait_send()
        cp.wait_recv()


def _ar_wait(packed, land, send_sems, recv_sems, after):
    return pl.pallas_call(
        functools.partial(_ar_wait_body),
        name="ar_wait",
        in_specs=[HBM, HBM, SEM, SEM, ANY],
        out_specs=[HBM, HBM],
        out_shape=[pltpu.HBM(packed.shape, packed.dtype), pltpu.HBM(land.shape, land.dtype)],
        input_output_aliases={0: 0, 1: 1},
        compiler_params=pltpu.CompilerParams(has_side_effects=EFFECT),
    )(packed, land, send_sems, recv_sems, after)


def _ar_sum_body(me_ref, own_ref, *rest):
    o_ref = rest[N_DEV]
    acc = None
    for dev in range(N_DEV):
        term = jnp.where(me_ref[0] == dev, own_ref[...], rest[dev][...])
        acc = term if acc is None else acc + term
    o_ref[...] = acc


def _ar_sum(packed, land, me):
    R, C = packed.shape
    tr = _pick(R, 552, 8)
    own = pl.BlockSpec((tr, C), lambda i, me: (i, 0))
    slot = lambda dev: pl.BlockSpec((None, tr, C), lambda i, me: (jnp.maximum((dev ^ me[0]) - 1, 0), i, 0))
    return pl.pallas_call(
        functools.partial(_ar_sum_body),
        name="ar_sum",
        grid_spec=pltpu.PrefetchScalarGridSpec(
            num_scalar_prefetch=1,
            grid=(R // tr,),
            in_specs=[own] + [slot(dev) for dev in range(N_DEV)],
            out_specs=pl.BlockSpec((tr, C), lambda i, me: (i, 0)),
        ),
        out_shape=jax.ShapeDtypeStruct((R, C), F32),
        compiler_params=_params(("parallel",)),
    )(me, packed, *([land] * N_DEV))


def _pack(arrays):
    rows = []
    for a in arrays:
        flat = a.reshape(-1).astype(F32)
        pad = (-flat.shape[0]) % BLK
        rows.append(jnp.pad(flat, (0, pad)).reshape(-1, BLK))
    packed = jnp.concatenate(rows, axis=0)
    pad = (-packed.shape[0]) % 8
    return jnp.pad(packed, ((0, pad), (0, 0)))


def _unpack(packed, shapes):
    out, r = [], 0
    for s in shapes:
        n = 1
        for k in s:
            n *= k
        nr = -(-n // BLK)
        out.append(packed[r:r + nr].reshape(-1)[:n].reshape(s))
        r += nr
    return out


def kernel(x, norm1_g, w_in, q_norm_g, k_norm_g, attn_sinks, gate_ln_g, gate_ln_b, w_spatial, b_spatial, out_norm_attn_g, out_norm_gate_g, w_out, norm2_g, w_ffn_gate, w_ffn_up, w_ffn_down, loss_target, m_norm1_g, m_w_in, m_q_norm_g, m_k_norm_g, m_attn_sinks, m_gate_ln_g, m_gate_ln_b, m_w_spatial, m_b_spatial, m_out_norm_attn_g, m_out_norm_gate_g, m_w_out, m_norm2_g, m_w_ffn_gate, m_w_ffn_up, m_w_ffn_down, v_norm1_g, v_w_in, v_q_norm_g, v_k_norm_g, v_attn_sinks, v_gate_ln_g, v_gate_ln_b, v_w_spatial, v_b_spatial, v_out_norm_attn_g, v_out_norm_gate_g, v_w_out, v_norm2_g, v_w_ffn_gate, v_w_ffn_up, v_w_ffn_down):
    bl, seq, D = x.shape
    T = bl * seq
    attn_w, gate_w = out_norm_attn_g.shape[1], out_norm_gate_g.shape[1]
    d = _Dims(seq, attn_w, gate_w)
    G = d.n_groups
    in_w = d.in_w
    slab = w_ffn_gate.shape[2]
    dff = slab * N_CHIPS
    assert w_in.shape[2] * N_CHIPS == in_w and seq % BLK == 0 and attn_w % (2 * BLK) == 0

    pc = jnp.stack([2 * lax.axis_index("x") + lax.axis_index("y"), lax.axis_index("c")]).astype(jnp.int32)
    big = [w_in[0], w_out[0], w_ffn_gate[0], w_ffn_up[0], w_ffn_down[0]]
    names = ["in", "out", "gate", "up", "down"]
    xf = x.reshape(T, D)
    tgt = loss_target.reshape(T, D)
    send, recv, buf, behind = _ag_start(_cast_into(big[0], pc, "cast_in"), norm1_g, "ag_start_in")
    started = [(send, recv, buf)]
    h1 = _rms_fwd(xf, norm1_g, "norm1_fwd", after=behind)
    behind = h1
    for w, n in zip(big[1:], names[1:]):
        buf = _cast_into(w, pc, "cast_" + n, side_by_side=n in ("gate", "up"))
        send, recv, buf, behind = _ag_start(buf, behind, "ag_start_" + n, both_cores=n == "down")
        started.append((send, recv, buf))

    def gathered(k, after):
        send, recv, buf = started[k]
        direct = names[k] == "down"
        buf = _ag_wait(buf, send, recv, after, "ag_wait_" + names[k], both_cores=direct)
        if not direct:
            buf = _ag_pair(buf, pc, "ag_pair_" + names[k])
        rs, cs = big[k].shape
        return buf.reshape(rs, N_CHIPS * cs) if len(buf.shape) == 3 else buf.reshape(N_CHIPS, rs, cs)

    qg2 = jnp.tile(q_norm_g, (1, 2))
    kg2 = jnp.tile(k_norm_g, (1, 2))
    lg, lb, wsp = gate_ln_g[0], gate_ln_b[0], w_spatial[0]
    bcol = jnp.broadcast_to(b_spatial[0][:, :, None], (G, BLK, BLK))

    win_full = jnp.transpose(gathered(0, behind), (1, 0, 2)).reshape(D, in_w)
    proj = _matmul(h1, win_full, "nn", F32, "proj_fwd", tm=MM_TILE // 2, tn=in_w)
    ya, yg, yn = _mixer_fwd(d, proj, attn_sinks, qg2, kg2, lg, lb, wsp, bcol, out_norm_attn_g, out_norm_gate_g)
    wout_full = gathered(1, yn).reshape(attn_w + gate_w, D)
    x1 = _matmul(yn, wout_full, "nn", F32, "out_fwd", tm=MM_TILE, tn=MM_TILE, add=xf)
    h2 = _rms_fwd(x1, norm2_g, "norm2_fwd")
    wg_full, wu_full = gathered(2, h2), gathered(3, h2)
    a, b, f = _ffn_up(h2, wg_full, wu_full)
    wd_full = gathered(4, f).reshape(dff, D)
    dx2, dx2b, loss_local = _ffn_down_loss(f, wd_full, x1, tgt)

    def swap_start(g, n):
        g4 = g.reshape(N_CHIPS, 2, g.shape[1] // 2, g.shape[2])
        return _swap_start(g4, "rs_swap_start_" + n)

    def reduce_start(swapping, n, after):
        send, recv, g4, got, _ = swapping
        g4, got = _swap_wait(g4, got, send, recv, after, "rs_swap_wait_" + n)
        return _rs_start(_add_pair(g4, got, pc, "rs_add_pair_" + n), got, "rs_start_" + n)

    reducing = {}
    g_d = _matmul(f, dx2b, "tn", BF16, "ffn_down_dw", tm=slab, tn=MM_TILE, out_slab="r")
    swap_d = swap_start(g_d, "down")
    da, db = _ffn_down_dx(dx2b, wd_full, a, b, swap_d[4])
    g_g = _matmul(h2, da, "tn", BF16, "ffn_gate_dw", tm=MM_TILE, tn=slab, out_slab="c")
    swap_g = swap_start(g_g, "gate")
    reducing["down"] = reduce_start(swap_d, "down", swap_g[4])
    g_u = _matmul(h2, db, "tn", BF16, "ffn_up_dw", tm=MM_TILE, tn=slab, out_slab="c", after=reducing["down"][4])
    swap_u = swap_start(g_u, "up")
    reducing["gate"] = reduce_start(swap_g, "gate", swap_u[4])
    dh2 = _matmul(da, wg_full, "nt", F32, "ffn_gate_dx", tm=MM_TILE, tn=MM_TILE, tk=dff // 2,
                  after=reducing["gate"][4])
    dh2 = _matmul(db, wu_full, "nt", F32, "ffn_up_dx", tm=MM_TILE, tn=MM_TILE, tk=dff // 2, add=dh2)
    reducing["up"] = reduce_start(swap_u, "up", dh2)
    dx1, dx1b, dg_norm2 = _rms_bwd(x1, norm2_g, dh2, dx2, "norm2_bwd", True)
    g_o = _matmul(yn, dx1b, "tn", BF16, "out_dw", tm=MM_TILE // 2, tn=MM_TILE, out_slab="r",
                  after=reducing["up"][4])
    swap_o = swap_start(g_o, "out")
    dy = _matmul(dx1b, wout_full, "nt", F32, "out_dx", tm=MM_TILE, tn=MM_TILE, after=swap_o[4])
    (dproj, dkv, dqg, dkg, dsk, dlg, dlb, dwsp, dbsp, dgoa, dgog) = _mixer_bwd(
        d, proj, ya, yg, dy, attn_sinks, qg2, kg2, lg, lb, wsp, bcol, out_norm_attn_g, out_norm_gate_g)
    dproj = _put_kv(d, dproj, dkv)
    reducing["out"] = reduce_start(swap_o, "out", dproj)
    g_in_full = _matmul(h1, dproj, "tn", BF16, "proj_dw", tm=MM_TILE // 2, tn=in_w, tk=T // 2,
                        after=reducing["out"][4])
    g_i = jnp.transpose(g_in_full.reshape(D, N_CHIPS, in_w // N_CHIPS), (1, 0, 2))
    g4_i = g_i.reshape(N_CHIPS, 2, D // 2, in_w // N_CHIPS)
    pair_i = _add_pair(g4_i, _swap_halves(g4_i, pc, "rs_swap_in"), pc, "rs_add_pair_in")
    reducing["in"] = _rs_start(pair_i, g_i, "rs_start_in")
    dh1 = _matmul(dproj, win_full, "nt", F32, "proj_dx", tm=MM_TILE, tn=MM_TILE, after=reducing["in"][4])
    dx, dg_norm1 = _rms_bwd(xf, norm1_g, dh1, dx1, "norm1_bwd", False)

    dqg64 = dqg[:, :HEAD_DIM] + dqg[:, HEAD_DIM:]
    dkg64 = dkg[:, :HEAD_DIM] + dkg[:, HEAD_DIM:]
    small_g_local = [dg_norm1, dqg64, dkg64, dsk[:, :d.n_heads], dlg, dlb, dwsp, dbsp, dgoa, dgog, dg_norm2,
                     loss_local]
    ar_send, ar_recv, ar_own, ar_land, ar_token = _ar_start(_pack(small_g_local))

    big_m = [m_w_in[0], m_w_out[0], m_w_ffn_gate[0], m_w_ffn_up[0], m_w_ffn_down[0]]
    big_v = [v_w_in[0], v_w_out[0], v_w_ffn_gate[0], v_w_ffn_up[0], v_w_ffn_down[0]]
    big_grads, big_d, big_nm, big_nv = [], [], [], []
    for w, m, v, n in zip(big, big_m, big_v, names):
        send, recv, pair, land, _ = reducing[n]
        pair, land = _rs_wait(pair, land, send, recv, ar_token, "rs_wait_" + n)
        own, got = _add_chips(pair, land, pc, "rs_add_chips_" + n)
        outs = _adamw_halves(w, own, got, m, v, pc, "adamw_" + n)
        for lst, o in zip((big_grads, big_d, big_nm, big_nv), outs):
            lst.append(o.reshape(w.shape))

    small_names_w = [norm1_g, q_norm_g, k_norm_g, attn_sinks, gate_ln_g, gate_ln_b, w_spatial, b_spatial,
                     out_norm_attn_g, out_norm_gate_g, norm2_g]
    small_m = [m_norm1_g, m_q_norm_g, m_k_norm_g, m_attn_sinks, m_gate_ln_g, m_gate_ln_b, m_w_spatial, m_b_spatial,
               m_out_norm_attn_g, m_out_norm_gate_g, m_norm2_g]
    small_v = [v_norm1_g, v_q_norm_g, v_k_norm_g, v_attn_sinks, v_gate_ln_g, v_gate_ln_b, v_w_spatial, v_b_spatial,
               v_out_norm_attn_g, v_out_norm_gate_g, v_norm2_g]
    shapes = [w.shape for w in small_names_w] + [loss_local.shape]
    ride = [jnp.zeros(loss_local.shape, F32)]
    ar_own, ar_land = _ar_wait(ar_own, ar_land, ar_send, ar_recv, big_nv[-1])
    me = (4 * lax.axis_index("x") + 2 * lax.axis_index("y") + lax.axis_index("c")).astype(jnp.int32).reshape(1)
    sg = _ar_sum(ar_own, ar_land, me)
    sd, snm, snv = _adamw(_pack(small_names_w + ride), sg, _pack(small_m + ride), _pack(small_v + ride), "adamw_small")
    small_g, small_d, small_nm, small_nv = (_unpack(t, shapes) for t in (sg, sd, snm, snv))
    loss = small_g[-1][0, 0]

    def order(small, bigs):
        s = list(small)
        bg = [t[None] for t in bigs]
        return [s[0], bg[0], s[1], s[2], s[3], s[4], s[5], s[6], s[7], s[8], s[9], bg[1], s[10], bg[2], bg[3], bg[4]]

    grad_x = dx.reshape(bl, seq, D)
    return (loss, grad_x, *order(small_g, big_grads), *order(small_d, big_d), *order(small_nm, big_nm),
            *order(small_nv, big_nv))
```
